```python
import jax, jax.numpy as jnp
from jax import lax
import numpy as np

D_MODEL = 1024
BATCH = 8
SEQ = 8192
DEPTH = 4

MLA_HEADS = 8
MLA_Q_LORA = 256
MLA_KV_LORA = 128
MLA_NOPE = 64
MLA_ROPE = 32
MLA_V = 64
SWA_HEADS = 8
SWA_KV_HEADS = 2
SWA_HEAD_DIM = 64
SWA_WINDOW = 128
SB_HEADS = 8
SB_HEAD_DIM = 64
D_FF = 4 * D_MODEL
BLOCK = 128
ROPE_THETA = 10000.0
EPS = 1e-6
N_BRANCHES = 3
MIX_A = MLA_HEADS * MLA_V
MIX_B = SWA_HEADS * SWA_HEAD_DIM
MIX_C = SB_HEADS * SB_HEAD_DIM

SPLIT_SIZES = (
    MLA_Q_LORA, MLA_KV_LORA, MLA_ROPE,
    SWA_HEADS * SWA_HEAD_DIM, SWA_KV_HEADS * SWA_HEAD_DIM, SWA_KV_HEADS * SWA_HEAD_DIM,
    SB_HEADS * SB_HEAD_DIM, SB_HEADS * SB_HEAD_DIM, SB_HEADS * SB_HEAD_DIM,
    N_BRANCHES * D_MODEL,
)
IN_WIDTH = sum(SPLIT_SIZES)
SPLIT_POINTS = [int(v) for v in np.cumsum(SPLIT_SIZES)[:-1]]

kernel_name = "hybrid_mla_swa_sinks_stickbreak_gated"


def rms_norm(x, g):
    xf = x.astype(jnp.float32)
    y = xf * lax.rsqrt(jnp.mean(xf * xf, axis=-1, keepdims=True) + EPS)
    return (y * g.astype(jnp.float32)).astype(x.dtype)


def rope(x, positions):
    d = x.shape[-1]
    inv = 1.0 / (ROPE_THETA ** (jnp.arange(0, d, 2, dtype=jnp.float32) / d))
    ang = positions.astype(jnp.float32)[..., None] * inv
    cos = jnp.cos(ang)[:, :, None, :]
    sin = jnp.sin(ang)[:, :, None, :]
    xf = x.astype(jnp.float32)
    x1, x2 = xf[..., : d // 2], xf[..., d // 2:]
    return jnp.concatenate([x1 * cos - x2 * sin, x2 * cos + x1 * sin], axis=-1).astype(x.dtype)


def to_blocks(t):
    b, s = t.shape[:2]
    return jnp.moveaxis(t.reshape(b, s // BLOCK, BLOCK, *t.shape[2:]), 1, 0)


def from_blocks(t):
    t = jnp.moveaxis(t, 0, 1)
    return t.reshape(t.shape[0], t.shape[1] * t.shape[2], *t.shape[3:])


def mla_branch(c_q, c_kv, k_rope, positions, g_q_lat, g_kv_lat, w_uq, w_ukv):
    B, S, _ = c_q.shape
    q = (rms_norm(c_q, g_q_lat) @ w_uq).reshape(B, S, MLA_HEADS, MLA_NOPE + MLA_ROPE)
    q_nope = q[..., :MLA_NOPE]
    q_pe = rope(q[..., MLA_NOPE:], positions)
    kv = (rms_norm(c_kv, g_kv_lat) @ w_ukv).reshape(B, S, MLA_HEADS, MLA_NOPE + MLA_V)
    k_nope, v = kv[..., :MLA_NOPE], kv[..., MLA_NOPE:]
    k_pe = rope(k_rope[:, :, None, :], positions)[:, :, 0]
    scale = (MLA_NOPE + MLA_ROPE) ** -0.5
    key_pos = jnp.arange(S)

    def block(args):
        qn, qp, i = args
        s = (jnp.einsum('bqhd,bkhd->bhqk', qn, k_nope, preferred_element_type=jnp.float32)
             + jnp.einsum('bqhd,bkd->bhqk', qp, k_pe, preferred_element_type=jnp.float32)) * scale
        q_pos = i * BLOCK + jnp.arange(BLOCK)
        s = jnp.where(key_pos[None, :] <= q_pos[:, None], s, -jnp.inf)
        p = jax.nn.softmax(s, axis=-1).astype(v.dtype)
        return jnp.einsum('bhqk,bkhd->bqhd', p, v)

    out = lax.map(block, (to_blocks(q_nope), to_blocks(q_pe), jnp.arange(S // BLOCK)))
    return from_blocks(out).reshape(B, S, MIX_A)


def swa_branch(q, k, v, positions, sinks):
    B, S, _ = q.shape
    G = SWA_HEADS // SWA_KV_HEADS
    n = S // BLOCK
    q = rope(q.reshape(B, S, SWA_HEADS, SWA_HEAD_DIM), positions)
    k = rope(k.reshape(B, S, SWA_KV_HEADS, SWA_HEAD_DIM), positions)
    v = v.reshape(B, S, SWA_KV_HEADS, SWA_HEAD_DIM)
    qb = q.reshape(B, n, BLOCK, SWA_KV_HEADS, G, SWA_HEAD_DIM)

    def band(t):
        tb = t.reshape(B, n, BLOCK, *t.shape[2:])
        prev = jnp.pad(tb, ((0, 0), (1, 0), (0, 0), (0, 0), (0, 0)))[:, :-1]
        return jnp.concatenate([prev, tb], axis=2)

    kb, vb = band(k), band(v)
    s = jnp.einsum('bnqkgd,bnskd->bnkgqs', qb, kb, preferred_element_type=jnp.float32) * SWA_HEAD_DIM ** -0.5
    qi = jnp.arange(BLOCK)[:, None] + BLOCK
    si = jnp.arange(2 * BLOCK)[None, :]
    diff = qi - si
    valid = (diff >= 0) & (diff < SWA_WINDOW)
    blk = jnp.arange(n)
    valid = valid[None] & ((blk[:, None, None] > 0) | (si[None] >= BLOCK))
    s = jnp.where(valid[None, :, None, None], s, -jnp.inf)
    sink = sinks.astype(jnp.float32).reshape(SWA_KV_HEADS, G)[None, None, :, :, None, None]
    m = jnp.maximum(jnp.max(s, axis=-1, keepdims=True), sink)
    p = jnp.exp(s - m)
    p = (p / (jnp.sum(p, axis=-1, keepdims=True) + jnp.exp(sink - m))).astype(v.dtype)
    o = jnp.einsum('bnkgqs,bnskd->bnqkgd', p, vb)
    return o.reshape(B, S, MIX_B)


def stick_breaking_branch(q, k, v):
    B, S, _ = q.shape
    q = q.reshape(B, S, SB_HEADS, SB_HEAD_DIM)
    k = k.reshape(B, S, SB_HEADS, SB_HEAD_DIM)
    v = v.reshape(B, S, SB_HEADS, SB_HEAD_DIM)
    scale = SB_HEAD_DIM ** -0.5
    key_pos = jnp.arange(S)

    def block(args):
        qb, i = args
        z = jnp.einsum('bqhd,bkhd->bhqk', qb, k, preferred_element_type=jnp.float32) * scale
        q_pos = i * BLOCK + jnp.arange(BLOCK)
        before = key_pos[None, :] < q_pos[:, None]
        log_1m_beta = jnp.where(before, jax.nn.log_sigmoid(-z), 0.0)
        tail = lax.cumsum(log_1m_beta, axis=3, reverse=True) - log_1m_beta
        a = jnp.where(before, jnp.exp(jax.nn.log_sigmoid(z) + tail), 0.0).astype(v.dtype)
        return jnp.einsum('bhqk,bkhd->bqhd', a, v)

    out = lax.map(block, (to_blocks(q), jnp.arange(S // BLOCK)))
    return from_blocks(out).reshape(B, S, MIX_C)


def hybrid_layer(x, positions, g_mix_pre, w_in, b_gate, g_q_lat, g_kv_lat, w_uq, w_ukv, swa_sinks,
                 w_o_mla, w_o_swa, w_o_sb, w_out, g_mix_post, g_mlp_pre, w_up, w_down, g_mlp_post):
    B, S, D = x.shape
    h = rms_norm(x, g_mix_pre)
    proj = h @ w_in
    (c_q, c_kv, k_rope, q_swa, k_swa, v_swa, q_sb, k_sb, v_sb, gate_logits) = jnp.split(proj, SPLIT_POINTS, axis=-1)
    o_a = mla_branch(c_q, c_kv, k_rope, positions, g_q_lat, g_kv_lat, w_uq, w_ukv) @ w_o_mla
    o_b = swa_branch(q_swa, k_swa, v_swa, positions, swa_sinks) @ w_o_swa
    o_c = stick_breaking_branch(q_sb, k_sb, v_sb) @ w_o_sb
    gates = jax.nn.sigmoid((gate_logits + b_gate).astype(jnp.float32)).astype(x.dtype)
    gates = gates.reshape(B, S, N_BRANCHES, D)
    mixed = gates[:, :, 0] * o_a + gates[:, :, 1] * o_b + gates[:, :, 2] * o_c
    x = x + rms_norm(mixed @ w_out, g_mix_post)
    h = rms_norm(x, g_mlp_pre)
    u = jnp.square(jax.nn.relu(h @ w_up))
    return x + rms_norm(u @ w_down, g_mlp_post)


def _fwd_setup_inputs(seed: int = 0) -> dict:
    key = jax.random.key(seed)
    ks = jax.random.split(key, 24)
    f32 = jnp.float32

    def dense(k, fan_in, fan_out):
        return jax.random.normal(k, (DEPTH, fan_in, fan_out), f32) * fan_in ** -0.5

    def gain(k, n):
        return 1.0 + 0.05 * jax.random.normal(k, (DEPTH, n), f32)

    x = jax.random.normal(ks[0], (BATCH, SEQ, D_MODEL), f32)
    start = jax.random.randint(ks[1], (BATCH, 1), 0, 1024, dtype=jnp.int32)
    positions = start + jnp.arange(SEQ, dtype=jnp.int32)[None, :]
    return {
        "x": x,
        "positions": positions,
        "g_mix_pre": gain(ks[2], D_MODEL),
        "w_in": dense(ks[3], D_MODEL, IN_WIDTH),
        "b_gate": 0.02 * jax.random.normal(ks[4], (DEPTH, N_BRANCHES * D_MODEL), f32),
        "g_q_lat": gain(ks[5], MLA_Q_LORA),
        "g_kv_lat": gain(ks[6], MLA_KV_LORA),
        "w_uq": dense(ks[7], MLA_Q_LORA, MLA_HEADS * (MLA_NOPE + MLA_ROPE)),
        "w_ukv": dense(ks[8], MLA_KV_LORA, MLA_HEADS * (MLA_NOPE + MLA_V)),
        "swa_sinks": 0.5 * jax.random.normal(ks[9], (DEPTH, SWA_HEADS), f32),
        "w_o_mla": dense(ks[10], MIX_A, D_MODEL),
        "w_o_swa": dense(ks[11], MIX_B, D_MODEL),
        "w_o_sb": dense(ks[12], MIX_C, D_MODEL),
        "w_out": dense(ks[13], D_MODEL, D_MODEL),
        "g_mix_post": gain(ks[14], D_MODEL),
        "g_mlp_pre": gain(ks[15], D_MODEL),
        "w_up": dense(ks[16], D_MODEL, D_FF),
        "w_down": dense(ks[17], D_FF, D_MODEL),
        "g_mlp_post": gain(ks[18], D_MODEL),
    }


def _fwd_reference(x, positions, g_mix_pre, w_in, b_gate, g_q_lat, g_kv_lat, w_uq, w_ukv, swa_sinks,
              w_o_mla, w_o_swa, w_o_sb, w_out, g_mix_post, g_mlp_pre, w_up, w_down, g_mlp_post):
    for l in range(DEPTH):
        x = hybrid_layer(x, positions, g_mix_pre[l], w_in[l], b_gate[l], g_q_lat[l], g_kv_lat[l],
                         w_uq[l], w_ukv[l], swa_sinks[l], w_o_mla[l], w_o_swa[l], w_o_sb[l], w_out[l],
                         g_mix_post[l], g_mlp_pre[l], w_up[l], w_down[l], g_mlp_post[l])
    return x


import jax as _jax
import jax.numpy as _jnp

TWIN_FORMAT = 'train_step'
FWD_PARAMS = ['x', 'positions', 'g_mix_pre', 'w_in', 'b_gate', 'g_q_lat', 'g_kv_lat', 'w_uq', 'w_ukv', 'swa_sinks', 'w_o_mla', 'w_o_swa', 'w_o_sb', 'w_out', 'g_mix_post', 'g_mlp_pre', 'w_up', 'w_down', 'g_mlp_post']
TWIN_WEIGHTS = ['g_mix_pre', 'w_in', 'b_gate', 'g_q_lat', 'g_kv_lat', 'w_uq', 'w_ukv', 'swa_sinks', 'w_o_mla', 'w_o_swa', 'w_o_sb', 'w_out', 'g_mix_post', 'g_mlp_pre', 'w_up', 'w_down', 'g_mlp_post']
TWIN_DIFF_INPUT = 'x'
TWIN_INPUTS = ['x', 'positions', 'g_mix_pre', 'w_in', 'b_gate', 'g_q_lat', 'g_kv_lat', 'w_uq', 'w_ukv', 'swa_sinks', 'w_o_mla', 'w_o_swa', 'w_o_sb', 'w_out', 'g_mix_post', 'g_mlp_pre', 'w_up', 'w_down', 'g_mlp_post', 'loss_target', 'm_g_mix_pre', 'm_w_in', 'm_b_gate', 'm_g_q_lat', 'm_g_kv_lat', 'm_w_uq', 'm_w_ukv', 'm_swa_sinks', 'm_w_o_mla', 'm_w_o_swa', 'm_w_o_sb', 'm_w_out', 'm_g_mix_post', 'm_g_mlp_pre', 'm_w_up', 'm_w_down', 'm_g_mlp_post', 'v_g_mix_pre', 'v_w_in', 'v_b_gate', 'v_g_q_lat', 'v_g_kv_lat', 'v_w_uq', 'v_w_ukv', 'v_swa_sinks', 'v_w_o_mla', 'v_w_o_swa', 'v_w_o_sb', 'v_w_out', 'v_g_mix_post', 'v_g_mlp_pre', 'v_w_up', 'v_w_down', 'v_g_mlp_post']
TWIN_OUTPUTS = ['loss', 'grad_x', 'grad_g_mix_pre', 'grad_w_in', 'grad_b_gate', 'grad_g_q_lat', 'grad_g_kv_lat', 'grad_w_uq', 'grad_w_ukv', 'grad_swa_sinks', 'grad_w_o_mla', 'grad_w_o_swa', 'grad_w_o_sb', 'grad_w_out', 'grad_g_mix_post', 'grad_g_mlp_pre', 'grad_w_up', 'grad_w_down', 'grad_g_mlp_post', 'delta_g_mix_pre', 'delta_w_in', 'delta_b_gate', 'delta_g_q_lat', 'delta_g_kv_lat', 'delta_w_uq', 'delta_w_ukv', 'delta_swa_sinks', 'delta_w_o_mla', 'delta_w_o_swa', 'delta_w_o_sb', 'delta_w_out', 'delta_g_mix_post', 'delta_g_mlp_pre', 'delta_w_up', 'delta_w_down', 'delta_g_mlp_post', 'new_m_g_mix_pre', 'new_m_w_in', 'new_m_b_gate', 'new_m_g_q_lat', 'new_m_g_kv_lat', 'new_m_w_uq', 'new_m_w_ukv', 'new_m_swa_sinks', 'new_m_w_o_mla', 'new_m_w_o_swa', 'new_m_w_o_sb', 'new_m_w_out', 'new_m_g_mix_post', 'new_m_g_mlp_pre', 'new_m_w_up', 'new_m_w_down', 'new_m_g_mlp_post', 'new_v_g_mix_pre', 'new_v_w_in', 'new_v_b_gate', 'new_v_g_q_lat', 'new_v_g_kv_lat', 'new_v_w_uq', 'new_v_w_ukv', 'new_v_swa_sinks', 'new_v_w_o_mla', 'new_v_w_o_swa', 'new_v_w_o_sb', 'new_v_w_out', 'new_v_g_mix_post', 'new_v_g_mlp_pre', 'new_v_w_up', 'new_v_w_down', 'new_v_g_mlp_post']
TWIN_LEAF_KINDS = {'loss': 'loss', 'grad_x': 'grad_x', 'grad_g_mix_pre': 'grad_w', 'grad_w_in': 'grad_w', 'grad_b_gate': 'grad_w', 'grad_g_q_lat': 'grad_w', 'grad_g_kv_lat': 'grad_w', 'grad_w_uq': 'grad_w', 'grad_w_ukv': 'grad_w', 'grad_swa_sinks': 'grad_w', 'grad_w_o_mla': 'grad_w', 'grad_w_o_swa': 'grad_w', 'grad_w_o_sb': 'grad_w', 'grad_w_out': 'grad_w', 'grad_g_mix_post': 'grad_w', 'grad_g_mlp_pre': 'grad_w', 'grad_w_up': 'grad_w', 'grad_w_down': 'grad_w', 'grad_g_mlp_post': 'grad_w', 'delta_g_mix_pre': 'delta_w', 'delta_w_in': 'delta_w', 'delta_b_gate': 'delta_w', 'delta_g_q_lat': 'delta_w', 'delta_g_kv_lat': 'delta_w', 'delta_w_uq': 'delta_w', 'delta_w_ukv': 'delta_w', 'delta_swa_sinks': 'delta_w', 'delta_w_o_mla': 'delta_w', 'delta_w_o_swa': 'delta_w', 'delta_w_o_sb': 'delta_w', 'delta_w_out': 'delta_w', 'delta_g_mix_post': 'delta_w', 'delta_g_mlp_pre': 'delta_w', 'delta_w_up': 'delta_w', 'delta_w_down': 'delta_w', 'delta_g_mlp_post': 'delta_w', 'new_m_g_mix_pre': 'new_m', 'new_m_w_in': 'new_m', 'new_m_b_gate': 'new_m', 'new_m_g_q_lat': 'new_m', 'new_m_g_kv_lat': 'new_m', 'new_m_w_uq': 'new_m', 'new_m_w_ukv': 'new_m', 'new_m_swa_sinks': 'new_m', 'new_m_w_o_mla': 'new_m', 'new_m_w_o_swa': 'new_m', 'new_m_w_o_sb': 'new_m', 'new_m_w_out': 'new_m', 'new_m_g_mix_post': 'new_m', 'new_m_g_mlp_pre': 'new_m', 'new_m_w_up': 'new_m', 'new_m_w_down': 'new_m', 'new_m_g_mlp_post': 'new_m', 'new_v_g_mix_pre': 'new_v', 'new_v_w_in': 'new_v', 'new_v_b_gate': 'new_v', 'new_v_g_q_lat': 'new_v', 'new_v_g_kv_lat': 'new_v', 'new_v_w_uq': 'new_v', 'new_v_w_ukv': 'new_v', 'new_v_swa_sinks': 'new_v', 'new_v_w_o_mla': 'new_v', 'new_v_w_o_swa': 'new_v', 'new_v_w_o_sb': 'new_v', 'new_v_w_out': 'new_v', 'new_v_g_mix_post': 'new_v', 'new_v_g_mlp_pre': 'new_v', 'new_v_w_up': 'new_v', 'new_v_w_down': 'new_v', 'new_v_g_mlp_post': 'new_v'}


def _forward(args):
    return _fwd_reference(*[args[k] for k in FWD_PARAMS])


def _output_shape():
    def fwd():
        inp = _fwd_setup_inputs(0)
        return _fwd_reference(*[inp[k] for k in FWD_PARAMS])
    out = _jax.eval_shape(fwd)
    return out.shape, out.dtype

N_MICROBATCH = 1
ADAM_LR = 0.001
ADAM_B1 = 0.9
ADAM_B2 = 0.999
ADAM_EPS = 1e-08
ADAM_WD = 0.01
ADAM_STEP = 10
PER_EXAMPLE_BATCH_AXIS = {'x': 0, 'positions': 0, 'loss_target': 0}
SHARED_INPUTS = []
_WEIGHT_DTYPES = {'g_mix_pre': _jnp.float32, 'w_in': _jnp.float32, 'b_gate': _jnp.float32, 'g_q_lat': _jnp.float32, 'g_kv_lat': _jnp.float32, 'w_uq': _jnp.float32, 'w_ukv': _jnp.float32, 'swa_sinks': _jnp.float32, 'w_o_mla': _jnp.float32, 'w_o_swa': _jnp.float32, 'w_o_sb': _jnp.float32, 'w_out': _jnp.float32, 'g_mix_post': _jnp.float32, 'g_mlp_pre': _jnp.float32, 'w_up': _jnp.float32, 'w_down': _jnp.float32, 'g_mlp_post': _jnp.float32}
MOMENT_SCALE = {'g_mix_pre': 4.937881e+01, 'w_in': 2.141244e+01, 'b_gate': 1.154014e+01, 'g_q_lat': 3.541081e+00, 'g_kv_lat': 8.346782e+01, 'w_uq': 2.173632e+00, 'w_ukv': 2.909128e+01, 'swa_sinks': 1.496379e+00, 'w_o_mla': 2.932108e+01, 'w_o_swa': 2.621158e+01, 'w_o_sb': 2.965815e+01, 'w_out': 4.886521e+01, 'g_mix_post': 8.614041e+01, 'g_mlp_pre': 2.581310e+01, 'w_up': 1.311184e+01, 'w_down': 5.814612e+01, 'g_mlp_post': 9.002266e+01}


def _to_microbatches(a, axis):
    t = _jnp.moveaxis(a, axis, 0)
    t = t.reshape((N_MICROBATCH, t.shape[0] // N_MICROBATCH) + t.shape[1:])
    return _jnp.moveaxis(t, 1, axis + 1)


def setup_inputs(seed: int = 0) -> dict:
    inp = _fwd_setup_inputs(seed)
    key = _jax.random.fold_in(_jax.random.key(seed), 7919)
    shape, _ = _output_shape()
    out = dict(inp)
    out["loss_target"] = _jax.random.normal(_jax.random.fold_in(key, 0), shape, _jnp.float32)
    for i, name in enumerate(TWIN_WEIGHTS):
        w = inp[name].astype(_jnp.float32)
        if MOMENT_SCALE is None:
            s = _jnp.sqrt(_jnp.mean(_jnp.square(w)) + 1e-30)
        else:
            s = MOMENT_SCALE[name]
        km, kv = _jax.random.split(_jax.random.fold_in(key, i + 1))
        out[name] = w
        out["m_" + name] = s * _jax.random.normal(km, w.shape, _jnp.float32)
        out["v_" + name] = (s * s) * _jax.random.uniform(kv, w.shape, _jnp.float32, 0.5, 1.5)
    if N_MICROBATCH > 1:
        for name, axis in PER_EXAMPLE_BATCH_AXIS.items():
            out[name] = _to_microbatches(out[name], axis)
    return {'x': out['x'], 'positions': out['positions'], 'g_mix_pre': out['g_mix_pre'], 'w_in': out['w_in'], 'b_gate': out['b_gate'], 'g_q_lat': out['g_q_lat'], 'g_kv_lat': out['g_kv_lat'], 'w_uq': out['w_uq'], 'w_ukv': out['w_ukv'], 'swa_sinks': out['swa_sinks'], 'w_o_mla': out['w_o_mla'], 'w_o_swa': out['w_o_swa'], 'w_o_sb': out['w_o_sb'], 'w_out': out['w_out'], 'g_mix_post': out['g_mix_post'], 'g_mlp_pre': out['g_mlp_pre'], 'w_up': out['w_up'], 'w_down': out['w_down'], 'g_mlp_post': out['g_mlp_post'], 'loss_target': out['loss_target'], 'm_g_mix_pre': out['m_g_mix_pre'], 'm_w_in': out['m_w_in'], 'm_b_gate': out['m_b_gate'], 'm_g_q_lat': out['m_g_q_lat'], 'm_g_kv_lat': out['m_g_kv_lat'], 'm_w_uq': out['m_w_uq'], 'm_w_ukv': out['m_w_ukv'], 'm_swa_sinks': out['m_swa_sinks'], 'm_w_o_mla': out['m_w_o_mla'], 'm_w_o_swa': out['m_w_o_swa'], 'm_w_o_sb': out['m_w_o_sb'], 'm_w_out': out['m_w_out'], 'm_g_mix_post': out['m_g_mix_post'], 'm_g_mlp_pre': out['m_g_mlp_pre'], 'm_w_up': out['m_w_up'], 'm_w_down': out['m_w_down'], 'm_g_mlp_post': out['m_g_mlp_post'], 'v_g_mix_pre': out['v_g_mix_pre'], 'v_w_in': out['v_w_in'], 'v_b_gate': out['v_b_gate'], 'v_g_q_lat': out['v_g_q_lat'], 'v_g_kv_lat': out['v_g_kv_lat'], 'v_w_uq': out['v_w_uq'], 'v_w_ukv': out['v_w_ukv'], 'v_swa_sinks': out['v_swa_sinks'], 'v_w_o_mla': out['v_w_o_mla'], 'v_w_o_swa': out['v_w_o_swa'], 'v_w_o_sb': out['v_w_o_sb'], 'v_w_out': out['v_w_out'], 'v_g_mix_post': out['v_g_mix_post'], 'v_g_mlp_pre': out['v_g_mlp_pre'], 'v_w_up': out['v_w_up'], 'v_w_down': out['v_w_down'], 'v_g_mlp_post': out['v_g_mlp_post']}


def _loss(weights, diff, rest, loss_target):
    with _jax.named_scope("forward"):
        args = {**rest, TWIN_DIFF_INPUT: diff, **{k: w.astype(_WEIGHT_DTYPES[k]) for k, w in weights.items()}}
        y = _forward(args)
    with _jax.named_scope("loss_head"):
        err = _jnp.square(y.astype(_jnp.float32) - loss_target)
        return 0.5 * _jnp.sum(_jnp.mean(err, axis=-1)) if err.ndim else 0.5 * err


def _adamw(w, g, m, v):
    m = ADAM_B1 * m + (1.0 - ADAM_B1) * g
    v = ADAM_B2 * v + (1.0 - ADAM_B2) * _jnp.square(g)
    m_hat = m / (1.0 - ADAM_B1 ** ADAM_STEP)
    v_hat = v / (1.0 - ADAM_B2 ** ADAM_STEP)
    delta = -ADAM_LR * (m_hat / (_jnp.sqrt(v_hat) + ADAM_EPS) + ADAM_WD * w)
    return delta, m, v


def reference(x, positions, g_mix_pre, w_in, b_gate, g_q_lat, g_kv_lat, w_uq, w_ukv, swa_sinks, w_o_mla, w_o_swa, w_o_sb, w_out, g_mix_post, g_mlp_pre, w_up, w_down, g_mlp_post, loss_target, m_g_mix_pre, m_w_in, m_b_gate, m_g_q_lat, m_g_kv_lat, m_w_uq, m_w_ukv, m_swa_sinks, m_w_o_mla, m_w_o_swa, m_w_o_sb, m_w_out, m_g_mix_post, m_g_mlp_pre, m_w_up, m_w_down, m_g_mlp_post, v_g_mix_pre, v_w_in, v_b_gate, v_g_q_lat, v_g_kv_lat, v_w_uq, v_w_ukv, v_swa_sinks, v_w_o_mla, v_w_o_swa, v_w_o_sb, v_w_out, v_g_mix_post, v_g_mlp_pre, v_w_up, v_w_down, v_g_mlp_post):
    given = dict(x=x, positions=positions, g_mix_pre=g_mix_pre, w_in=w_in, b_gate=b_gate, g_q_lat=g_q_lat, g_kv_lat=g_kv_lat, w_uq=w_uq, w_ukv=w_ukv, swa_sinks=swa_sinks, w_o_mla=w_o_mla, w_o_swa=w_o_swa, w_o_sb=w_o_sb, w_out=w_out, g_mix_post=g_mix_post, g_mlp_pre=g_mlp_pre, w_up=w_up, w_down=w_down, g_mlp_post=g_mlp_post, loss_target=loss_target, m_g_mix_pre=m_g_mix_pre, m_w_in=m_w_in, m_b_gate=m_b_gate, m_g_q_lat=m_g_q_lat, m_g_kv_lat=m_g_kv_lat, m_w_uq=m_w_uq, m_w_ukv=m_w_ukv, m_swa_sinks=m_swa_sinks, m_w_o_mla=m_w_o_mla, m_w_o_swa=m_w_o_swa, m_w_o_sb=m_w_o_sb, m_w_out=m_w_out, m_g_mix_post=m_g_mix_post, m_g_mlp_pre=m_g_mlp_pre, m_w_up=m_w_up, m_w_down=m_w_down, m_g_mlp_post=m_g_mlp_post, v_g_mix_pre=v_g_mix_pre, v_w_in=v_w_in, v_b_gate=v_b_gate, v_g_q_lat=v_g_q_lat, v_g_kv_lat=v_g_kv_lat, v_w_uq=v_w_uq, v_w_ukv=v_w_ukv, v_swa_sinks=v_swa_sinks, v_w_o_mla=v_w_o_mla, v_w_o_swa=v_w_o_swa, v_w_o_sb=v_w_o_sb, v_w_out=v_w_out, v_g_mix_post=v_g_mix_post, v_g_mlp_pre=v_g_mlp_pre, v_w_up=v_w_up, v_w_down=v_w_down, v_g_mlp_post=v_g_mlp_post)
    weights = {n: given[n] for n in TWIN_WEIGHTS}
    shared = {n: given[n] for n in SHARED_INPUTS}
    per_example = {n: given[n] for n in ['x', 'positions']}
    grad_fn = _jax.value_and_grad(_loss, argnums=(0, 1))

    def one_microbatch(ex, loss_target):
        ex = dict(ex)
        diff = ex.pop(TWIN_DIFF_INPUT)
        return grad_fn(weights, diff, {**shared, **ex}, loss_target)

    if N_MICROBATCH == 1:
        loss, (grad_w, grad_x) = one_microbatch(per_example, given["loss_target"])
    else:
        def body(carry, xs):
            loss_sum, grad_sum = carry
            l_k, (gw_k, gx_k) = one_microbatch(xs[0], xs[1])
            with _jax.named_scope("update"):
                return (loss_sum + l_k, _jax.tree.map(_jnp.add, grad_sum, gw_k)), gx_k

        init = (_jnp.zeros((), _jnp.float32), _jax.tree.map(_jnp.zeros_like, weights))
        (loss, grad_w), grad_x = _jax.lax.scan(body, init, (per_example, given["loss_target"]))
    with _jax.named_scope("update"):
        delta_w, new_m, new_v = {}, {}, {}
        for n in TWIN_WEIGHTS:
            delta_w[n], new_m[n], new_v[n] = _adamw(weights[n], grad_w[n], given["m_" + n], given["v_" + n])
    return (loss, grad_x, *[grad_w[n] for n in TWIN_WEIGHTS], *[delta_w[n] for n in TWIN_WEIGHTS],
            *[new_m[n] for n in TWIN_WEIGHTS], *[new_v[n] for n in TWIN_WEIGHTS])
```

```python
import functools

import jax
import jax.numpy as jnp
from jax import lax
from jax.experimental import pallas as pl
from jax.experimental.pallas import tpu as pltpu

F32, BF16 = jnp.float32, jnp.bfloat16

D_MODEL = 1024
DEPTH = 4
MLA_HEADS, MLA_Q_LORA, MLA_KV_LORA, MLA_NOPE, MLA_ROPE, MLA_V = 8, 256, 128, 64, 32, 64
SWA_HEADS, SWA_KV_HEADS, SWA_HEAD_DIM, SWA_WINDOW = 8, 2, 64, 128
SB_HEADS, SB_HEAD_DIM = 8, 64
D_FF = 4 * D_MODEL
ROPE_THETA = 10000.0
EPS = 1e-6
N_DEV = 8
ADAM_LR, ADAM_B1, ADAM_B2, ADAM_EPS, ADAM_WD, ADAM_STEP = 0.001, 0.9, 0.999, 1e-08, 0.01, 10

LANES = 128
VMEM_LIMIT_MAX = 60 * 1024 * 1024
VMEM_LIMIT_MIN = 32 * 1024 * 1024

ORIG_COLS = dict(c_q=(0, 256), c_kv=(256, 128), k_rope=(384, 32), q_swa=(416, 512), k_swa=(928, 128), v_swa=(1056, 128),
                 q_sb=(1184, 512), k_sb=(1696, 512), v_sb=(2208, 512), gates=(2720, 3072))
IN_WIDTH = 5792
PERM_ORDER = ("gates", "q_swa", "q_sb", "k_sb", "v_sb", "c_q", "c_kv", "k_swa", "v_swa", "k_rope")
PERM_WIDTH = dict(gates=3072, q_swa=512, q_sb=512, k_sb=512, v_sb=512, c_q=256, c_kv=128, k_swa=128, v_swa=128, k_rope=128)
PERM_OFF = {}
_o = 0
for _n in PERM_ORDER:
    PERM_OFF[_n] = _o
    _o += PERM_WIDTH[_n]
IN_WIDTH_P = _o
MLA_QW = MLA_HEADS * LANES
MLA_KVW = MLA_HEADS * LANES + MLA_HEADS * MLA_V

NT = (((1,), (1,)), ((), ()))
TN = (((0,), (0,)), ((), ()))
NEG = -1e30


def _cparams(sem, block_bytes):
    limit = int(min(VMEM_LIMIT_MAX, max(VMEM_LIMIT_MIN, 2 * block_bytes + (16 << 20))))
    return pltpu.CompilerParams(dimension_semantics=sem, vmem_limit_bytes=limit)


def _nbytes(shape, dtype):
    n = 1
    for s in shape:
        n *= s
    return n * jnp.dtype(dtype).itemsize


def _ri(arr, tile, width=None, cb=0):
    width = arr.shape[1] if width is None else width
    return (arr, (tile, width), lambda i, cb=cb: (i, cb))


def _bi(arr):
    return (arr, arr.shape, lambda i: (0, 0))


def _ro(rows, width, dtype, tile):
    return ((rows, width), dtype, (tile, width), lambda i: (i, 0))


def _rowwise(name, fn, ins, outs, reds=(), *, n_tiles):
    n_in, n_out = len(ins), len(outs)

    def body(*refs):
        vals = fn(*[r[...] for r in refs[:n_in]])
        for r, v in zip(refs[n_in:n_in + n_out], vals[:n_out]):
            if isinstance(v, (list, tuple)):
                for j, vj in enumerate(v):
                    r[j] = vj.astype(r.dtype)
            else:
                r[...] = v.astype(r.dtype)
        if reds:
            @pl.when(pl.program_id(0) == 0)
            def _():
                for r in refs[n_in + n_out:]:
                    r[...] = jnp.zeros_like(r)
            for r, v in zip(refs[n_in + n_out:], vals[n_out:]):
                r[...] += v

    block_bytes = sum(_nbytes(b, a.dtype) for a, b, _ in ins) + sum(_nbytes(b, d) for _, d, b, _ in outs)
    res = pl.pallas_call(
        body, name=name, grid=(n_tiles,),
        in_specs=[pl.BlockSpec(b, m) for _, b, m in ins],
        out_specs=[pl.BlockSpec(b, m) for _, _, b, m in outs] + [pl.BlockSpec((1, w), lambda i: (0, 0)) for w in reds],
        out_shape=[jax.ShapeDtypeStruct(s, d) for s, d, _, _ in outs] + [jax.ShapeDtypeStruct((1, w), F32) for w in reds],
        compiler_params=_cparams(("arbitrary",) if reds else ("parallel",), block_bytes),
    )(*[a for a, _, _ in ins])
    return res


def _rms(x, g):
    r = lax.rsqrt(jnp.mean(x * x, axis=1, keepdims=True) + EPS)
    return x * r * g


def _rms_bwd(x, g, dy):
    r = lax.rsqrt(jnp.mean(x * x, axis=1, keepdims=True) + EPS)
    xn = x * r
    dxn = dy * g
    dx = r * (dxn - xn * jnp.mean(dxn * xn, axis=1, keepdims=True))
    return dx, jnp.sum(dy * xn, axis=0, keepdims=True)


def _swap_halves(x, half):
    n = x.shape[1]
    lane = lax.broadcasted_iota(jnp.int32, x.shape, 1)
    first = (lane % (2 * half)) < half
    return jnp.where(first, pltpu.roll(x, n - half, axis=1), pltpu.roll(x, half, axis=1))


def _rope(x, c, sg, half):
    return x * c + _swap_halves(x, half) * sg


def _rope_t(dy, c, sg, half):
    return dy * c - _swap_halves(dy, half) * sg


def _mm_nn(name, a, b, outs, *, tm, tn, extras=(), epilogue=None):
    M, K = a.shape
    N = b.shape[1]
    tm = min(tm, M)
    n_e = len(extras)

    def body(*refs):
        a_ref, b_ref = refs[:2]
        acc = jnp.dot(a_ref[...].astype(BF16), b_ref[...].astype(BF16), preferred_element_type=F32)
        vals = (acc,) * len(outs) if epilogue is None else epilogue(acc, *[r[...] for r in refs[2:2 + n_e]])
        for r, v in zip(refs[2 + n_e:], vals):
            r[...] = v.astype(r.dtype)

    block_bytes = (_nbytes((tm, K), a.dtype) + _nbytes((K, tn), b.dtype) + sum(_nbytes((tm, tn), e.dtype) for e in extras)
                   + sum(_nbytes((tm, tn), d) for d in outs) + _nbytes((tm, tn), F32))
    return pl.pallas_call(
        body, name=name, grid=(N // tn, M // tm),
        in_specs=[pl.BlockSpec((tm, K), lambda j, i: (i, 0)), pl.BlockSpec((K, tn), lambda j, i: (0, j))]
        + [pl.BlockSpec((tm, tn), lambda j, i: (i, j)) for _ in extras],
        out_specs=[pl.BlockSpec((tm, tn), lambda j, i: (i, j)) for _ in outs],
        out_shape=[jax.ShapeDtypeStruct((M, N), d) for d in outs],
        compiler_params=_cparams(("parallel", "parallel"), block_bytes),
    )(a, b, *extras)


def _mm_nt(name, a, b, outs, *, tm, tn, extras=(), epilogue=None):
    M, N = a.shape
    K = b.shape[0]
    tm = min(tm, M)
    n_e = len(extras)

    def body(*refs):
        a_ref, b_ref = refs[:2]
        acc = lax.dot_general(a_ref[...].astype(BF16), b_ref[...].astype(BF16), NT, preferred_element_type=F32)
        vals = (acc,) * len(outs) if epilogue is None else epilogue(acc, *[r[...] for r in refs[2:2 + n_e]])
        for r, v in zip(refs[2 + n_e:], vals):
            r[...] = v.astype(r.dtype)

    block_bytes = (_nbytes((tm, N), a.dtype) + _nbytes((tn, N), b.dtype) + sum(_nbytes((tm, tn), e.dtype) for e in extras)
                   + sum(_nbytes((tm, tn), d) for d in outs) + _nbytes((tm, tn), F32))
    return pl.pallas_call(
        body, name=name, grid=(K // tn, M // tm),
        in_specs=[pl.BlockSpec((tm, N), lambda j, i: (i, 0)), pl.BlockSpec((tn, N), lambda j, i: (j, 0))]
        + [pl.BlockSpec((tm, tn), lambda j, i: (i, j)) for _ in extras],
        out_specs=[pl.BlockSpec((tm, tn), lambda j, i: (i, j)) for _ in outs],
        out_shape=[jax.ShapeDtypeStruct((M, K), d) for d in outs],
        compiler_params=_cparams(("parallel", "parallel"), block_bytes),
    )(a, b, *extras)


def _mm_tn(name, a, b, *, tm, tn, ts):
    S, K = a.shape
    N = b.shape[1]
    ts = min(ts, S)

    def body(a_ref, b_ref, o_ref):
        @pl.when(pl.program_id(2) == 0)
        def _():
            o_ref[...] = jnp.zeros_like(o_ref)
        o_ref[...] += lax.dot_general(a_ref[...].astype(BF16), b_ref[...].astype(BF16), TN, preferred_element_type=F32)

    block_bytes = _nbytes((ts, tm), a.dtype) + _nbytes((ts, tn), b.dtype) + 2 * _nbytes((tm, tn), F32)
    return pl.pallas_call(
        body, name=name, grid=(K // tm, N // tn, S // ts),
        in_specs=[pl.BlockSpec((ts, tm), lambda i, j, s: (s, i)), pl.BlockSpec((ts, tn), lambda i, j, s: (s, j))],
        out_specs=pl.BlockSpec((tm, tn), lambda i, j, s: (i, j)),
        out_shape=jax.ShapeDtypeStruct((K, N), F32),
        compiler_params=_cparams(("parallel", "parallel", "arbitrary"), block_bytes),
    )(a, b)


def _lane_pack(cols, rows):
    lane = lax.broadcasted_iota(jnp.int32, (rows, LANES), 1)
    val = jnp.zeros((rows, LANES), F32)
    for h, c in enumerate(cols):
        val = jnp.where(lane == h, c, val)
    return val


def _mask(kb, row, tk, window):
    col = kb * tk + lax.broadcasted_iota(jnp.int32, (1, tk), 1)
    ok = col <= row
    if window:
        ok = ok & ((row - col) < window)
    return ok


def _kb_range(i, tq, tk, window):
    nd = tq // tk
    if window:
        lo = jnp.maximum(i * tq - (window - 1), 0) // tk
        return lo, lo, (i + 1) * nd
    return 0, i * nd, (i + 1) * nd


def _softmax_fwd(name, q, k, v, sinks, *, groups, heads, q_spec, k_spec, v_spec, o_width, tq, tk, scale, window):
    S = q.shape[0]
    nq = S // tq
    nh = len(heads)
    has_sink = sinks is not None
    dv = heads[0][2].stop - heads[0][2].start

    def body(*refs):
        if has_sink:
            sink_ref, q_ref, k_ref, v_ref, o_ref, lse_ref = refs
        else:
            q_ref, k_ref, v_ref, o_ref, lse_ref = refs
        g, i = pl.program_id(0), pl.program_id(1)
        q_all = q_ref[...]
        row = i * tq + lax.broadcasted_iota(jnp.int32, (tq, 1), 0)
        lo, mid, hi = _kb_range(i, tq, tk, window)
        outs, lses = [], []
        for h, (qs, ks, vs) in enumerate(heads):
            qh = q_all[:, qs]

            def blk(kb, carry, masked, qh=qh, ks=ks, vs=vs):
                m, l, acc = carry
                r0 = pl.multiple_of(kb * tk, tk)
                kk = k_ref[pl.ds(r0, tk), :][:, ks]
                vv = v_ref[pl.ds(r0, tk), :][:, vs]
                s = lax.dot_general(qh, kk, NT, preferred_element_type=F32) * scale
                if masked:
                    ok = _mask(kb, row, tk, window)
                    s = jnp.where(ok, s, NEG)
                m_new = jnp.maximum(m, jnp.max(s, axis=1, keepdims=True))
                alpha = jnp.exp(m - m_new)
                p = jnp.exp(s - m_new)
                if masked:
                    p = jnp.where(ok, p, 0.0)
                l = alpha * l + jnp.sum(p, axis=1, keepdims=True)
                acc = alpha * acc + jnp.dot(p.astype(BF16), vv, preferred_element_type=F32)
                return m_new, l, acc

            if has_sink:
                m0 = jnp.full((tq, 1), sink_ref[g * nh + h], F32)
                l0 = jnp.ones((tq, 1), F32)
            else:
                m0 = jnp.full((tq, 1), NEG, F32)
                l0 = jnp.zeros((tq, 1), F32)
            carry = (m0, l0, jnp.zeros((tq, dv), F32))
            carry = lax.fori_loop(lo, mid, functools.partial(blk, masked=False), carry)
            m, l, acc = lax.fori_loop(mid, hi, functools.partial(blk, masked=True), carry)
            outs.append(acc / l)
            lses.append(m + jnp.log(l))
        o_ref[...] = jnp.concatenate(outs, axis=1).astype(o_ref.dtype)
        lse_ref[0] = _lane_pack(lses, tq)

    in_specs = [pl.BlockSpec(*q_spec), pl.BlockSpec(*k_spec), pl.BlockSpec(*v_spec)]
    args = [q, k, v]
    if has_sink:
        in_specs = [pl.BlockSpec(memory_space=pltpu.SMEM)] + in_specs
        args = [sinks] + args
    wo = nh * dv
    block_bytes = _nbytes(q_spec[0], q.dtype) + _nbytes(k_spec[0], k.dtype) + _nbytes(v_spec[0], v.dtype) + 4 * tq * (wo + LANES)
    return pl.pallas_call(
        body, name=name, grid=(groups, nq), in_specs=in_specs,
        out_specs=[pl.BlockSpec((tq, wo), lambda g, i: (i, g)), pl.BlockSpec((1, tq, LANES), lambda g, i: (g, i, 0))],
        out_shape=[jax.ShapeDtypeStruct((S, o_width), BF16), jax.ShapeDtypeStruct((groups, S, LANES), F32)],
        compiler_params=_cparams(("parallel", "arbitrary"), block_bytes),
    )(*args)


def _softmax_bwd(name, q, k, v, o, do, lse, sinks, *, groups, heads, q_spec, k_spec, v_spec, tq, tk, scale, window):
    S = q.shape[0]
    nq = S // tq
    nh = len(heads)
    has_sink = sinks is not None
    dv = heads[0][2].stop - heads[0][2].start
    wo = nh * dv

    def body(*refs):
        if has_sink:
            sink_ref, q_ref, k_ref, v_ref, o_ref, do_ref, lse_ref, dq_ref, dk_ref, dv_ref, dsink_ref = refs
        else:
            q_ref, k_ref, v_ref, o_ref, do_ref, lse_ref, dq_ref, dk_ref, dv_ref = refs
        g, i = pl.program_id(0), pl.program_id(1)

        @pl.when(i == 0)
        def _():
            dk_ref[...] = jnp.zeros_like(dk_ref)
            dv_ref[...] = jnp.zeros_like(dv_ref)
            if has_sink:
                dsink_ref[...] = jnp.zeros_like(dsink_ref)

        q_all = q_ref[...]
        o_all = o_ref[...].astype(F32)
        do_all = do_ref[...].astype(F32)
        lse_all = lse_ref[0]
        row = i * tq + lax.broadcasted_iota(jnp.int32, (tq, 1), 0)
        lo, mid, hi = _kb_range(i, tq, tk, window)
        per_head = []
        for h, (qs, ks, vs) in enumerate(heads):
            osl = slice(h * dv, (h + 1) * dv)
            doh = do_all[:, osl]
            delta = jnp.sum(doh * o_all[:, osl], axis=1, keepdims=True)
            per_head.append((q_all[:, qs], doh.astype(BF16), delta, lse_all[:, h:h + 1], ks, vs))

        def blk(kb, dqs, masked):
            r0 = pl.multiple_of(kb * tk, tk)
            k_all = k_ref[pl.ds(r0, tk), :]
            v_all = v_ref[pl.ds(r0, tk), :]
            if masked:
                ok = _mask(kb, row, tk, window)
            dk_parts, dv_parts, new_dqs = {}, {}, []
            for (qh, doh, delta, lse_h, ks, vs), dq in zip(per_head, dqs):
                kk, vv = k_all[:, ks], v_all[:, vs]
                s = lax.dot_general(qh, kk, NT, preferred_element_type=F32) * scale
                p = jnp.exp(s - lse_h)
                if masked:
                    p = jnp.where(ok, p, 0.0)
                dp = lax.dot_general(doh, vv, NT, preferred_element_type=F32)
                ds = p * (dp - delta) * scale
                new_dqs.append(dq + jnp.dot(ds.astype(BF16), kk, preferred_element_type=F32))
                dk_c = jnp.dot(ds.T.astype(BF16), qh, preferred_element_type=F32)
                dv_c = jnp.dot(p.T.astype(BF16), doh, preferred_element_type=F32)
                dk_parts[ks.start] = dk_parts[ks.start] + dk_c if ks.start in dk_parts else dk_c
                dv_parts[vs.start] = dv_parts[vs.start] + dv_c if vs.start in dv_parts else dv_c
            dk_ref[pl.ds(r0, tk), :] += jnp.concatenate([dk_parts[s0] for s0 in sorted(dk_parts)], axis=1)
            dv_ref[pl.ds(r0, tk), :] += jnp.concatenate([dv_parts[s0] for s0 in sorted(dv_parts)], axis=1)
            return tuple(new_dqs)

        dqs = tuple(jnp.zeros((tq, qs.stop - qs.start), F32) for qs, _, _ in heads)
        dqs = lax.fori_loop(lo, mid, functools.partial(blk, masked=False), dqs)
        dqs = lax.fori_loop(mid, hi, functools.partial(blk, masked=True), dqs)
        dq_ref[...] = jnp.concatenate(dqs, axis=1)
        if has_sink:
            for h, (_, _, delta, lse_h, _, _) in enumerate(per_head):
                p_sink = jnp.exp(sink_ref[g * nh + h] - lse_h)
                dsink_ref[0, h:h + 1, :] += jnp.broadcast_to(-jnp.sum(p_sink * delta, axis=0, keepdims=True), (1, LANES))

    in_specs = [pl.BlockSpec(*q_spec), pl.BlockSpec(*k_spec), pl.BlockSpec(*v_spec),
                pl.BlockSpec((tq, wo), lambda g, i: (i, g)), pl.BlockSpec((tq, wo), lambda g, i: (i, g)),
                pl.BlockSpec((1, tq, LANES), lambda g, i: (g, i, 0))]
    args = [q, k, v, o, do, lse]
    out_specs = [pl.BlockSpec(*q_spec), pl.BlockSpec(*k_spec), pl.BlockSpec(*v_spec)]
    out_shape = [jax.ShapeDtypeStruct(q.shape, F32), jax.ShapeDtypeStruct(k.shape, F32), jax.ShapeDtypeStruct(v.shape, F32)]
    if has_sink:
        in_specs = [pl.BlockSpec(memory_space=pltpu.SMEM)] + in_specs
        args = [sinks] + args
        out_specs.append(pl.BlockSpec((1, 8, LANES), lambda g, i: (g, 0, 0)))
        out_shape.append(jax.ShapeDtypeStruct((groups, 8, LANES), F32))
    block_bytes = (_nbytes(q_spec[0], q.dtype) + _nbytes(k_spec[0], k.dtype) + _nbytes(v_spec[0], v.dtype) + 6 * tq * wo + 4 * tq * LANES
                   + _nbytes(q_spec[0], F32) + _nbytes(k_spec[0], F32) + _nbytes(v_spec[0], F32))
    return pl.pallas_call(
        body, name=name, grid=(groups, nq), in_specs=in_specs, out_specs=out_specs, out_shape=out_shape,
        compiler_params=_cparams(("parallel", "arbitrary"), block_bytes),
    )(*args)


def _split_dot(x, u):
    hi = x.astype(BF16)
    lo = (x - hi.astype(F32)).astype(BF16)
    return jnp.dot(hi, u, preferred_element_type=F32) + jnp.dot(lo, u, preferred_element_type=F32)


def _suffix_matrix(tk):
    r = lax.broadcasted_iota(jnp.int32, (tk, tk), 0)
    c = lax.broadcasted_iota(jnp.int32, (tk, tk), 1)
    return (r > c).astype(BF16)


def _sb_block(qh, kk, kb, row, tk, scale, u, run_l, masked):
    z = lax.dot_general(qh, kk, NT, preferred_element_type=F32) * scale
    l = -(jnp.maximum(z, 0.0) + jnp.log(1.0 + jnp.exp(-jnp.abs(z))))
    ok = None
    if masked:
        col = kb * tk + lax.broadcasted_iota(jnp.int32, (1, tk), 1)
        ok = col < row
        l = jnp.where(ok, l, 0.0)
    tail = _split_dot(l, u) + run_l
    return z, l, tail, ok


def _sb_fwd(name, pb, *, q_cb, k_cb, v_cb, tq, tk, scale):
    S = pb.shape[0]
    nq, nd = S // tq, tq // tk
    hd = SB_HEAD_DIM
    groups = SB_HEADS * hd // LANES

    def body(q_ref, k_ref, v_ref, o16_ref, o32_ref):
        i = pl.program_id(1)
        q_all = q_ref[...]
        row = i * tq + lax.broadcasted_iota(jnp.int32, (tq, 1), 0)
        u = _suffix_matrix(tk)
        outs = []
        for h in range(LANES // hd):
            sl = slice(h * hd, (h + 1) * hd)
            qh = q_all[:, sl]

            def blk(t, carry, masked, base, qh=qh, sl=sl):
                run_l, acc = carry
                kb = base - 1 - t
                r0 = pl.multiple_of(kb * tk, tk)
                kk = k_ref[pl.ds(r0, tk), :][:, sl]
                vv = v_ref[pl.ds(r0, tk), :][:, sl]
                z, l, tail, ok = _sb_block(qh, kk, kb, row, tk, scale, u, run_l, masked)
                e = z + l + tail
                if masked:
                    e = jnp.where(ok, e, NEG)
                a = jnp.exp(e)
                acc = acc + _split_dot_rhs(a, vv)
                return run_l + jnp.sum(l, axis=1, keepdims=True), acc

            carry = (jnp.zeros((tq, 1), F32), jnp.zeros((tq, hd), F32))
            carry = lax.fori_loop(0, nd, functools.partial(blk, masked=True, base=(i + 1) * nd), carry)
            _, acc = lax.fori_loop(0, i * nd, functools.partial(blk, masked=False, base=i * nd), carry)
            outs.append(acc)
        o = jnp.concatenate(outs, axis=1)
        o16_ref[...] = o.astype(BF16)
        o32_ref[...] = o

    block_bytes = 2 * tq * LANES + 2 * 2 * S * LANES + 6 * tq * LANES
    return pl.pallas_call(
        body, name=name, grid=(groups, nq),
        in_specs=[pl.BlockSpec((tq, LANES), lambda g, i: (i, q_cb + g)), pl.BlockSpec((S, LANES), lambda g, i: (0, k_cb + g)),
                  pl.BlockSpec((S, LANES), lambda g, i: (0, v_cb + g))],
        out_specs=[pl.BlockSpec((tq, LANES), lambda g, i: (i, g)), pl.BlockSpec((tq, LANES), lambda g, i: (i, g))],
        out_shape=[jax.ShapeDtypeStruct((S, groups * LANES), BF16), jax.ShapeDtypeStruct((S, groups * LANES), F32)],
        compiler_params=_cparams(("parallel", "arbitrary"), block_bytes),
    )(pb, pb, pb)


def _split_dot_rhs(a, vv):
    hi = a.astype(BF16)
    lo = (a - hi.astype(F32)).astype(BF16)
    return jnp.dot(hi, vv, preferred_element_type=F32) + jnp.dot(lo, vv, preferred_element_type=F32)


def _sb_bwd(name, pb, o32, do, *, q_cb, k_cb, v_cb, tq, tk, scale):
    S = pb.shape[0]
    nq, nd = S // tq, tq // tk
    hd = SB_HEAD_DIM
    groups = SB_HEADS * hd // LANES
    nh = LANES // hd

    def body(q_ref, k_ref, v_ref, o_ref, do_ref, dq_ref, dk_ref, dv_ref):
        i = pl.program_id(1)

        @pl.when(i == 0)
        def _():
            dk_ref[...] = jnp.zeros_like(dk_ref)
            dv_ref[...] = jnp.zeros_like(dv_ref)

        q_all = q_ref[...]
        o_all = o_ref[...]
        do_all = do_ref[...].astype(F32)
        row = i * tq + lax.broadcasted_iota(jnp.int32, (tq, 1), 0)
        u = _suffix_matrix(tk)
        per_head = []
        for h in range(nh):
            sl = slice(h * hd, (h + 1) * hd)
            doh = do_all[:, sl].astype(BF16)
            total = jnp.sum(doh.astype(F32) * o_all[:, sl], axis=1, keepdims=True)
            per_head.append((q_all[:, sl], doh, total, sl))

        def blk(t, carry, masked, base):
            kb = base - 1 - t
            r0 = pl.multiple_of(kb * tk, tk)
            k_all = k_ref[pl.ds(r0, tk), :]
            v_all = v_ref[pl.ds(r0, tk), :]
            new, dk_c, dv_c = [], [], []
            for (qh, doh, total, sl), (run_l, run_g, dq) in zip(per_head, carry):
                kk, vv = k_all[:, sl], v_all[:, sl]
                z, l, tail, ok = _sb_block(qh, kk, kb, row, tk, scale, u, run_l, masked)
                beta = jnp.exp(z - (jnp.maximum(z, 0.0) + jnp.log(1.0 + jnp.exp(-jnp.abs(z)))))
                e = z + l + tail
                if masked:
                    e = jnp.where(ok, e, NEG)
                a = jnp.exp(e)
                da = lax.dot_general(doh, vv, NT, preferred_element_type=F32)
                gr = da * a
                prefix = total - (_split_dot(gr, u) + run_g)
                dz = (gr - beta * prefix) * scale
                if masked:
                    dz = jnp.where(ok, dz, 0.0)
                dzb = dz.astype(BF16)
                new.append((run_l + jnp.sum(l, axis=1, keepdims=True), run_g + jnp.sum(gr, axis=1, keepdims=True),
                            dq + jnp.dot(dzb, kk, preferred_element_type=F32)))
                dk_c.append(jnp.dot(dz.T.astype(BF16), qh, preferred_element_type=F32))
                dv_c.append(jnp.dot(a.T.astype(BF16), doh, preferred_element_type=F32))
            dk_ref[pl.ds(r0, tk), :] += jnp.concatenate(dk_c, axis=1)
            dv_ref[pl.ds(r0, tk), :] += jnp.concatenate(dv_c, axis=1)
            return tuple(new)

        zc = jnp.zeros((tq, 1), F32)
        carry = tuple((zc, zc, jnp.zeros((tq, hd), F32)) for _ in range(nh))
        carry = lax.fori_loop(0, nd, functools.partial(blk, masked=True, base=(i + 1) * nd), carry)
        carry = lax.fori_loop(0, i * nd, functools.partial(blk, masked=False, base=i * nd), carry)
        dq_ref[...] = jnp.concatenate([c[2] for c in carry], axis=1)

    W = groups * LANES
    block_bytes = 2 * tq * LANES + 2 * 2 * S * LANES + 3 * 4 * tq * LANES + 2 * 4 * S * LANES
    return pl.pallas_call(
        body, name=name, grid=(groups, nq),
        in_specs=[pl.BlockSpec((tq, LANES), lambda g, i: (i, q_cb + g)), pl.BlockSpec((S, LANES), lambda g, i: (0, k_cb + g)),
                  pl.BlockSpec((S, LANES), lambda g, i: (0, v_cb + g)), pl.BlockSpec((tq, LANES), lambda g, i: (i, g)),
                  pl.BlockSpec((tq, LANES), lambda g, i: (i, g))],
        out_specs=[pl.BlockSpec((tq, LANES), lambda g, i: (i, g)), pl.BlockSpec((S, LANES), lambda g, i: (0, g)),
                   pl.BlockSpec((S, LANES), lambda g, i: (0, g))],
        out_shape=[jax.ShapeDtypeStruct((S, W), F32)] * 3,
        compiler_params=_cparams(("parallel", "arbitrary"), block_bytes),
    )(pb, pb, pb, o32, do)


def _exchange(name, src, per_peer):
    blk = src.shape[1:] if per_peer else src.shape

    def body(src_ref, out_ref, send_sems, recv_sems, local_sem):
        x, y, c = lax.axis_index("x"), lax.axis_index("y"), lax.axis_index("c")
        me = 4 * x + 2 * y + c

        def copy(k):
            px, py, pc = x ^ (k >> 2), y ^ ((k >> 1) & 1), c ^ (k & 1)
            s = src_ref.at[4 * px + 2 * py + pc] if per_peer else src_ref
            return pltpu.make_async_remote_copy(
                src_ref=s, dst_ref=out_ref.at[me], send_sem=send_sems.at[k - 1], recv_sem=recv_sems.at[k - 1],
                device_id=(px, py, pc), device_id_type=pl.DeviceIdType.MESH)

        mine = pltpu.make_async_copy(src_ref.at[me] if per_peer else src_ref, out_ref.at[me], local_sem)
        mine.start()
        copies = [copy(k) for k in range(1, N_DEV)]
        for cp in copies:
            cp.start()
        for cp in copies:
            cp.wait_recv()
        for cp in copies:
            cp.wait_send()
        mine.wait()

    return pl.pallas_call(
        body, name=name,
        in_specs=[pl.BlockSpec(memory_space=pltpu.HBM)], out_specs=pl.BlockSpec(memory_space=pltpu.HBM),
        out_shape=jax.ShapeDtypeStruct((N_DEV,) + tuple(blk), src.dtype),
        scratch_shapes=[pltpu.SemaphoreType.DMA((N_DEV - 1,)), pltpu.SemaphoreType.DMA((N_DEV - 1,)), pltpu.SemaphoreType.DMA],
    )(src)


def _adamw(name, parts, w, m, v, *, tile):
    R, C = w.shape

    def body(p_ref, w_ref, m_ref, v_ref, g_ref, d_ref, nm_ref, nv_ref):
        g = p_ref[0].astype(F32)
        for d in range(1, N_DEV):
            g = g + p_ref[d].astype(F32)
        wv = w_ref[...]
        mm = ADAM_B1 * m_ref[...] + (1.0 - ADAM_B1) * g
        vv = ADAM_B2 * v_ref[...] + (1.0 - ADAM_B2) * jnp.square(g)
        m_hat = mm / (1.0 - ADAM_B1 ** ADAM_STEP)
        v_hat = vv / (1.0 - ADAM_B2 ** ADAM_STEP)
        g_ref[...] = g
        d_ref[...] = -ADAM_LR * (m_hat / (jnp.sqrt(v_hat) + ADAM_EPS) + ADAM_WD * wv)
        nm_ref[...] = mm
        nv_ref[...] = vv

    blk = pl.BlockSpec((tile, C), lambda i: (i, 0))
    block_bytes = N_DEV * _nbytes((tile, C), parts.dtype) + 7 * _nbytes((tile, C), F32)
    return pl.pallas_call(
        body, name=name, grid=(R // tile,),
        in_specs=[pl.BlockSpec((N_DEV, tile, C), lambda i: (0, i, 0)), blk, blk, blk],
        out_specs=[blk] * 4, out_shape=[jax.ShapeDtypeStruct((R, C), F32)] * 4,
        compiler_params=_cparams(("parallel",), block_bytes),
    )(parts, w, m, v)


def _perm_w_in(w):
    cols = [w[:, ORIG_COLS[n][0]:ORIG_COLS[n][0] + ORIG_COLS[n][1]] for n in PERM_ORDER]
    cols.append(jnp.zeros((w.shape[0], IN_WIDTH_P - IN_WIDTH), w.dtype))
    return jnp.concatenate(cols, axis=1)


def _unperm_w_in(wp):
    order = sorted(ORIG_COLS, key=lambda n: ORIG_COLS[n][0])
    return jnp.concatenate([wp[:, PERM_OFF[n]:PERM_OFF[n] + ORIG_COLS[n][1]] for n in order], axis=1)


def _perm_w_uq(w):
    w3 = w.reshape(w.shape[0], MLA_HEADS, MLA_NOPE + MLA_ROPE)
    return jnp.pad(w3, ((0, 0), (0, 0), (0, LANES - MLA_NOPE - MLA_ROPE))).reshape(w.shape[0], MLA_QW)


def _unperm_w_uq(wp):
    return wp.reshape(wp.shape[0], MLA_HEADS, LANES)[:, :, :MLA_NOPE + MLA_ROPE].reshape(wp.shape[0], -1)


def _perm_w_ukv(w):
    w3 = w.reshape(w.shape[0], MLA_HEADS, MLA_NOPE + MLA_V)
    kp = jnp.pad(w3[:, :, :MLA_NOPE], ((0, 0), (0, 0), (0, LANES - MLA_NOPE))).reshape(w.shape[0], MLA_HEADS * LANES)
    return jnp.concatenate([kp, w3[:, :, MLA_NOPE:].reshape(w.shape[0], MLA_HEADS * MLA_V)], axis=1)


def _unperm_w_ukv(wp):
    n = wp.shape[0]
    kp = wp[:, :MLA_HEADS * LANES].reshape(n, MLA_HEADS, LANES)[:, :, :MLA_NOPE]
    vp = wp[:, MLA_HEADS * LANES:].reshape(n, MLA_HEADS, MLA_V)
    return jnp.concatenate([kp, vp], axis=2).reshape(n, MLA_HEADS * (MLA_NOPE + MLA_V))


BIG = ("w_in", "w_uq", "w_ukv", "w_o_mla", "w_o_swa", "w_o_sb", "w_out", "w_up", "w_down")
ROW_SHARDED = ("w_out", "w_down")
SMALL = ("g_mix_pre", "b_gate", "g_q_lat", "g_kv_lat", "swa_sinks", "g_mix_post", "g_mlp_pre", "g_mlp_post")
PACK_COLS = 1024


def _to_shards(name, full):
    R, C = full.shape
    if name in ROW_SHARDED:
        return full.reshape(N_DEV, -1)
    return full.reshape(R, N_DEV, C // N_DEV).transpose(1, 0, 2).reshape(N_DEV, -1)


def _from_shards(name, flat, shard_shape):
    L, r, c = shard_shape
    a = flat.reshape(N_DEV, L, r, c)
    if name in ROW_SHARDED:
        return a.transpose(1, 0, 2, 3).reshape(L, N_DEV * r, c)
    return a.transpose(1, 2, 0, 3).reshape(L, r, N_DEV * c)


def _tables(positions):
    pos = positions.astype(F32).reshape(-1, 1)

    def cs(d):
        inv = 1.0 / (ROPE_THETA ** (jnp.arange(0, d, 2, dtype=F32) / d))
        ang = pos * inv
        c, s = jnp.cos(ang), jnp.sin(ang)
        return jnp.concatenate([c, c], axis=1), jnp.concatenate([-s, s], axis=1)

    c64, s64 = cs(SWA_HEAD_DIM)
    c32, s32 = cs(MLA_ROPE)
    n = pos.shape[0]
    one, zero = jnp.ones((n, MLA_NOPE), F32), jnp.zeros((n, MLA_NOPE), F32)
    pad0 = jnp.zeros((n, LANES - MLA_NOPE - MLA_ROPE), F32)
    cq = jnp.concatenate([one, c32, pad0], axis=1)
    sq = jnp.concatenate([zero, s32, pad0], axis=1)
    padk = jnp.zeros((n, LANES - MLA_ROPE), F32)
    return dict(
        c_swa_q=jnp.tile(c64, (1, SWA_HEADS)), s_swa_q=jnp.tile(s64, (1, SWA_HEADS)),
        c_swa_k=jnp.tile(c64, (1, SWA_KV_HEADS)), s_swa_k=jnp.tile(s64, (1, SWA_KV_HEADS)),
        c_mla_q=jnp.tile(cq, (1, MLA_HEADS)), s_mla_q=jnp.tile(sq, (1, MLA_HEADS)),
        c_mla_k=jnp.concatenate([c32, padk], axis=1), s_mla_k=jnp.concatenate([s32, padk], axis=1))


def _cb(name):
    return PERM_OFF[name] // PERM_WIDTH[name]


MLA_SPEC = dict(groups=MLA_HEADS // 2, tq=128, tk=128, scale=(MLA_NOPE + MLA_ROPE) ** -0.5, window=0,
                heads=[(slice(h * LANES, (h + 1) * LANES), slice(h * LANES, (h + 1) * LANES), slice(h * MLA_V, (h + 1) * MLA_V)) for h in range(2)])
SWA_G = SWA_HEADS // SWA_KV_HEADS
SWA_SPEC = dict(groups=SWA_KV_HEADS, tq=128, tk=128, scale=SWA_HEAD_DIM ** -0.5, window=SWA_WINDOW,
                heads=[(slice(g * SWA_HEAD_DIM, (g + 1) * SWA_HEAD_DIM), slice(0, SWA_HEAD_DIM), slice(0, SWA_HEAD_DIM)) for g in range(SWA_G)])
SB_SPEC = dict(tq=128, tk=128, scale=SB_HEAD_DIM ** -0.5, q_cb=PERM_OFF["q_sb"] // LANES, k_cb=PERM_OFF["k_sb"] // LANES,
               v_cb=PERM_OFF["v_sb"] // LANES)


def _mla_specs(S):
    tq = MLA_SPEC["tq"]
    return dict(q_spec=((tq, 2 * LANES), lambda g, i: (i, g)), k_spec=((S, 2 * LANES), lambda g, i: (0, g)),
                v_spec=((S, 2 * MLA_V), lambda g, i: (0, g)))


def _swa_specs(S):
    tq = SWA_SPEC["tq"]
    return dict(q_spec=((tq, SWA_G * SWA_HEAD_DIM), lambda g, i: (i, g)), k_spec=((S, SWA_HEAD_DIM), lambda g, i: (g, 0)),
                v_spec=((S, SWA_HEAD_DIM), lambda g, i: (g, 0)))


def _layer_fwd(l, x, W, P, tb, T):
    S = x.shape[0]
    nt = S // T
    h, = _rowwise(f"l{l}_norm_in", lambda xv, g: (_rms(xv, g),), [_ri(x, T), _bi(P["g_mix_pre"])], [_ro(S, D_MODEL, BF16, T)], n_tiles=nt)
    proj, proj16 = _mm_nn(f"l{l}_mm_in", h, W["w_in"], [F32, BF16], tm=512, tn=IN_WIDTH_P // 2)

    def mix_prep(cq, ckv, qs, ks, vs, gq, gkv, cq_t, sq_t, ck_t, sk_t):
        ksr = _rope(ks, ck_t, sk_t, SWA_HEAD_DIM // 2)
        hd = SWA_HEAD_DIM
        return (_rms(cq, gq), _rms(ckv, gkv), _rope(qs, cq_t, sq_t, hd // 2),
                [ksr[:, :hd], ksr[:, hd:]], [vs[:, :hd], vs[:, hd:]])

    kv3 = lambda dt: ((SWA_KV_HEADS, S, SWA_HEAD_DIM), dt, (SWA_KV_HEADS, T, SWA_HEAD_DIM), lambda i: (0, i, 0))
    cqn, ckvn, q_swa, k_swa, v_swa = _rowwise(
        f"l{l}_mix_prep", mix_prep,
        [_ri(proj, T, 256, _cb("c_q")), _ri(proj, T, 128, _cb("c_kv")), _ri(proj, T, 512, _cb("q_swa")), _ri(proj, T, 128, _cb("k_swa")),
         _ri(proj, T, 128, _cb("v_swa")), _bi(P["g_q_lat"]), _bi(P["g_kv_lat"]), _ri(tb["c_swa_q"], T), _ri(tb["s_swa_q"], T),
         _ri(tb["c_swa_k"], T), _ri(tb["s_swa_k"], T)],
        [_ro(S, 256, BF16, T), _ro(S, 128, BF16, T), _ro(S, 512, BF16, T), kv3(BF16), kv3(BF16)], n_tiles=nt)
    k_swa = k_swa.reshape(SWA_KV_HEADS * S, SWA_HEAD_DIM)
    v_swa = v_swa.reshape(SWA_KV_HEADS * S, SWA_HEAD_DIM)
    q_lat, = _mm_nn(f"l{l}_mm_uq", cqn, W["w_uq"], [F32], tm=1024, tn=MLA_QW)
    kv_lat, = _mm_nn(f"l{l}_mm_ukv", ckvn, W["w_ukv"], [F32], tm=1024, tn=MLA_KVW)

    def mla_prep(q, kk, vv, kr, cq_t, sq_t, ck_t, sk_t):
        kpe = pltpu.roll(_rope(kr, ck_t, sk_t, MLA_ROPE // 2), MLA_NOPE, axis=1)
        return _rope(q, cq_t, sq_t, MLA_ROPE // 2), kk + jnp.tile(kpe, (1, MLA_HEADS)), vv

    q_mla, k_mla, v_mla = _rowwise(
        f"l{l}_mla_prep", mla_prep,
        [_ri(q_lat, T), _ri(kv_lat, T, MLA_HEADS * LANES, 0), _ri(kv_lat, T, MLA_HEADS * MLA_V, 2), _ri(proj, T, 128, _cb("k_rope")),
         _ri(tb["c_mla_q"], T), _ri(tb["s_mla_q"], T), _ri(tb["c_mla_k"], T), _ri(tb["s_mla_k"], T)],
        [_ro(S, MLA_QW, BF16, T), _ro(S, MLA_HEADS * LANES, BF16, T), _ro(S, MLA_HEADS * MLA_V, BF16, T)], n_tiles=nt)

    att_a, lse_a = _softmax_fwd(f"l{l}_mla_fwd", q_mla, k_mla, v_mla, None, o_width=MLA_HEADS * MLA_V, **MLA_SPEC, **_mla_specs(S))
    att_b, lse_b = _softmax_fwd(f"l{l}_swa_fwd", q_swa, k_swa, v_swa, P["swa_sinks"], o_width=SWA_HEADS * SWA_HEAD_DIM, **SWA_SPEC, **_swa_specs(S))
    att_c, att_c32 = _sb_fwd(f"l{l}_sb_fwd", proj16, **SB_SPEC)
    o_a, = _mm_nn(f"l{l}_mm_oa", att_a, W["w_o_mla"], [F32], tm=1024, tn=D_MODEL)
    o_b, = _mm_nn(f"l{l}_mm_ob", att_b, W["w_o_swa"], [F32], tm=1024, tn=D_MODEL)
    o_c, = _mm_nn(f"l{l}_mm_oc", att_c, W["w_o_sb"], [F32], tm=1024, tn=D_MODEL)

    def gate_mix(gl, b, oa, ob, oc):
        gt = jax.nn.sigmoid(gl + b)
        return (gt[:, :D_MODEL] * oa + gt[:, D_MODEL:2 * D_MODEL] * ob + gt[:, 2 * D_MODEL:] * oc,)

    mixed, = _rowwise(f"l{l}_gate_mix", gate_mix, [_ri(proj, T, 3072, 0), _bi(P["b_gate"]), _ri(o_a, T), _ri(o_b, T), _ri(o_c, T)],
                      [_ro(S, D_MODEL, BF16, T)], n_tiles=nt)
    y, = _mm_nn(f"l{l}_mm_out", mixed, W["w_out"], [F32], tm=1024, tn=D_MODEL)

    def resid_norm(xv, yv, gpost, gpre):
        x1 = xv + _rms(yv, gpost)
        return x1, _rms(x1, gpre)

    x1, h2 = _rowwise(f"l{l}_resid_norm", resid_norm, [_ri(x, T), _ri(y, T), _bi(P["g_mix_post"]), _bi(P["g_mlp_pre"])],
                      [_ro(S, D_MODEL, F32, T), _ro(S, D_MODEL, BF16, T)], n_tiles=nt)
    up, u = _mm_nn(f"l{l}_mm_up", h2, W["w_up"], [F32, BF16], tm=512, tn=2048,
                   epilogue=lambda acc: (acc, jnp.square(jnp.maximum(acc, 0.0))))
    dn, = _mm_nn(f"l{l}_mm_down", u, W["w_down"], [F32], tm=512, tn=D_MODEL)
    x2, = _rowwise(f"l{l}_resid_out", lambda xv, dv, g: (xv + _rms(dv, g),), [_ri(x1, T), _ri(dn, T), _bi(P["g_mlp_post"])],
                   [_ro(S, D_MODEL, F32, T)], n_tiles=nt)
    saved = dict(x=x, h=h, proj=proj, proj16=proj16, cqn=cqn, ckvn=ckvn, q_swa=q_swa, k_swa=k_swa, v_swa=v_swa, q_mla=q_mla, k_mla=k_mla,
                 v_mla=v_mla, att_a=att_a, lse_a=lse_a, att_b=att_b, lse_b=lse_b, att_c=att_c, att_c32=att_c32, o_a=o_a, o_b=o_b, o_c=o_c,
                 mixed=mixed, y=y, x1=x1, h2=h2, up=up, u=u, dn=dn)
    return x2, saved


def _layer_bwd(l, dx2, sv, W, P, tb, T):
    S = dx2.shape[0]
    nt = S // T
    G = {}

    def post_norm_bwd(v, g, dy):
        return _rms_bwd(v, g, dy)

    d_dn, G["g_mlp_post"] = _rowwise(f"l{l}_b_post2", post_norm_bwd, [_ri(sv["dn"], T), _bi(P["g_mlp_post"]), _ri(dx2, T)],
                                    [_ro(S, D_MODEL, BF16, T)], [D_MODEL], n_tiles=nt)
    d_up, = _mm_nt(f"l{l}_b_mm_down", d_dn, W["w_down"], [BF16], tm=512, tn=2048, extras=[sv["up"]],
                   epilogue=lambda acc, upv: (acc * (2.0 * jnp.maximum(upv, 0.0)),))
    G["w_down"] = _mm_tn(f"l{l}_g_down", sv["u"], d_dn, tm=2048, tn=D_MODEL, ts=512)
    d_h2, = _mm_nt(f"l{l}_b_mm_up", d_up, W["w_up"], [F32], tm=512, tn=D_MODEL)
    G["w_up"] = _mm_tn(f"l{l}_g_up", sv["h2"], d_up, tm=D_MODEL, tn=2048, ts=512)

    def pre_norm_bwd(v, g, dy, dres):
        dx, dg = _rms_bwd(v, g, dy)
        return dres + dx, dg

    dx1, G["g_mlp_pre"] = _rowwise(f"l{l}_b_pre2", pre_norm_bwd, [_ri(sv["x1"], T), _bi(P["g_mlp_pre"]), _ri(d_h2, T), _ri(dx2, T)],
                                  [_ro(S, D_MODEL, F32, T)], [D_MODEL], n_tiles=nt)
    d_y, G["g_mix_post"] = _rowwise(f"l{l}_b_post1", post_norm_bwd, [_ri(sv["y"], T), _bi(P["g_mix_post"]), _ri(dx1, T)],
                                   [_ro(S, D_MODEL, BF16, T)], [D_MODEL], n_tiles=nt)
    d_mixed, = _mm_nt(f"l{l}_b_mm_out", d_y, W["w_out"], [F32], tm=1024, tn=D_MODEL)
    G["w_out"] = _mm_tn(f"l{l}_g_out", sv["mixed"], d_y, tm=D_MODEL, tn=D_MODEL, ts=512)

    def gate_bwd(dm, gl, b, oa, ob, oc):
        gt = jax.nn.sigmoid(gl + b)
        outs, dgl = [], []
        for k, o in enumerate((oa, ob, oc)):
            gk = gt[:, k * D_MODEL:(k + 1) * D_MODEL]
            outs.append(dm * gk)
            dgl.append(dm * o * gk * (1.0 - gk))
        dgl = jnp.concatenate(dgl, axis=1)
        return (*outs, dgl, jnp.sum(dgl, axis=0, keepdims=True))

    d_oa, d_ob, d_oc, d_gl, G["b_gate"] = _rowwise(
        f"l{l}_b_gate", gate_bwd, [_ri(d_mixed, T), _ri(sv["proj"], T, 3072, 0), _bi(P["b_gate"]), _ri(sv["o_a"], T), _ri(sv["o_b"], T), _ri(sv["o_c"], T)],
        [_ro(S, D_MODEL, BF16, T)] * 3 + [_ro(S, 3 * D_MODEL, BF16, T)], [3 * D_MODEL], n_tiles=nt)
    d_att = {}
    for br, d_o, att in (("mla", d_oa, sv["att_a"]), ("swa", d_ob, sv["att_b"]), ("sb", d_oc, sv["att_c"])):
        d_att[br], = _mm_nt(f"l{l}_b_mm_o_{br}", d_o, W["w_o_" + br], [F32], tm=1024, tn=512)
        G["w_o_" + br] = _mm_tn(f"l{l}_g_o_{br}", att, d_o, tm=512, tn=D_MODEL, ts=512)

    dq_mla, dk_mla, dv_mla = _softmax_bwd(f"l{l}_mla_bwd", sv["q_mla"], sv["k_mla"], sv["v_mla"], sv["att_a"], d_att["mla"], sv["lse_a"], None,
                                          **MLA_SPEC, **_mla_specs(S))
    dq_swa, dk_swa, dv_swa, dsink = _softmax_bwd(f"l{l}_swa_bwd", sv["q_swa"], sv["k_swa"], sv["v_swa"], sv["att_b"], d_att["swa"], sv["lse_b"],
                                                 P["swa_sinks"], **SWA_SPEC, **_swa_specs(S))
    G["swa_sinks"] = dsink[:, :SWA_G, 0].reshape(1, SWA_HEADS)
    dq_sb, dk_sb, dv_sb = _sb_bwd(f"l{l}_sb_bwd", sv["proj16"], sv["att_c32"], d_att["sb"], **SB_SPEC)

    def mla_prep_bwd(dq, dk, dvv, cq_t, sq_t, ck_t, sk_t):
        dks = dk[:, :LANES]
        for hh in range(1, MLA_HEADS):
            dks = dks + dk[:, hh * LANES:(hh + 1) * LANES]
        d_kr = _rope_t(pltpu.roll(dks, LANES - MLA_NOPE, axis=1), ck_t, sk_t, MLA_ROPE // 2)
        return _rope_t(dq, cq_t, sq_t, MLA_ROPE // 2), jnp.concatenate([dk, dvv], axis=1), d_kr

    d_q_lat, d_kv_lat, d_krope = _rowwise(
        f"l{l}_b_mla_prep", mla_prep_bwd,
        [_ri(dq_mla, T), _ri(dk_mla, T), _ri(dv_mla, T), _ri(tb["c_mla_q"], T), _ri(tb["s_mla_q"], T), _ri(tb["c_mla_k"], T), _ri(tb["s_mla_k"], T)],
        [_ro(S, MLA_QW, BF16, T), _ro(S, MLA_KVW, BF16, T), _ro(S, LANES, BF16, T)], n_tiles=nt)
    d_cqn, = _mm_nt(f"l{l}_b_mm_uq", d_q_lat, W["w_uq"], [F32], tm=1024, tn=MLA_Q_LORA)
    G["w_uq"] = _mm_tn(f"l{l}_g_uq", sv["cqn"], d_q_lat, tm=MLA_Q_LORA, tn=MLA_QW, ts=512)
    d_ckvn, = _mm_nt(f"l{l}_b_mm_ukv", d_kv_lat, W["w_ukv"], [F32], tm=1024, tn=MLA_KV_LORA)
    G["w_ukv"] = _mm_tn(f"l{l}_g_ukv", sv["ckvn"], d_kv_lat, tm=MLA_KV_LORA, tn=MLA_KVW, ts=512)

    def mix_prep_bwd(cq, ckv, gq, gkv, dcqn, dckvn, dqs, dks, dvs, cq_t, sq_t, ck_t, sk_t):
        d_cq, dgq = _rms_bwd(cq, gq, dcqn)
        d_ckv, dgkv = _rms_bwd(ckv, gkv, dckvn)
        dk2 = jnp.concatenate([dks[0], dks[1]], axis=1)
        dv2 = jnp.concatenate([dvs[0], dvs[1]], axis=1)
        return (d_cq, d_ckv, _rope_t(dqs, cq_t, sq_t, SWA_HEAD_DIM // 2), _rope_t(dk2, ck_t, sk_t, SWA_HEAD_DIM // 2), dv2, dgq, dgkv)

    kv3 = lambda a: (a.reshape(SWA_KV_HEADS, S, SWA_HEAD_DIM), (SWA_KV_HEADS, T, SWA_HEAD_DIM), lambda i: (0, i, 0))
    d_cq, d_ckv, d_qswa, d_kswa, d_vswa, G["g_q_lat"], G["g_kv_lat"] = _rowwise(
        f"l{l}_b_mix_prep", mix_prep_bwd,
        [_ri(sv["proj"], T, 256, _cb("c_q")), _ri(sv["proj"], T, 128, _cb("c_kv")), _bi(P["g_q_lat"]), _bi(P["g_kv_lat"]), _ri(d_cqn, T), _ri(d_ckvn, T),
         _ri(dq_swa, T), kv3(dk_swa), kv3(dv_swa), _ri(tb["c_swa_q"], T), _ri(tb["s_swa_q"], T), _ri(tb["c_swa_k"], T), _ri(tb["s_swa_k"], T)],
        [_ro(S, 256, BF16, T), _ro(S, 128, BF16, T), _ro(S, 512, BF16, T), _ro(S, 128, BF16, T), _ro(S, 128, BF16, T)], [256, 128], n_tiles=nt)
    pieces = dict(gates=d_gl, q_swa=d_qswa, q_sb=dq_sb, k_sb=dk_sb, v_sb=dv_sb, c_q=d_cq, c_kv=d_ckv, k_swa=d_kswa, v_swa=d_vswa, k_rope=d_krope)
    d_proj = jnp.concatenate([pieces[n].astype(BF16) for n in PERM_ORDER], axis=1)
    d_h, = _mm_nt(f"l{l}_b_mm_in", d_proj, W["w_in"], [F32], tm=512, tn=512)
    G["w_in"] = _mm_tn(f"l{l}_g_in", sv["h"], d_proj, tm=D_MODEL, tn=IN_WIDTH_P // 2, ts=512)
    dx, G["g_mix_pre"] = _rowwise(f"l{l}_b_pre1", pre_norm_bwd, [_ri(sv["x"], T), _bi(P["g_mix_pre"]), _ri(d_h, T), _ri(dx1, T)],
                                 [_ro(S, D_MODEL, F32, T)], [D_MODEL], n_tiles=nt)
    return dx, G


def _pack_rows(vecs, rows):
    flat = jnp.concatenate([v.reshape(-1) for v in vecs])
    return jnp.pad(flat, (0, rows * PACK_COLS - flat.shape[0])).reshape(rows, PACK_COLS)


def kernel(x, positions, g_mix_pre, w_in, b_gate, g_q_lat, g_kv_lat, w_uq, w_ukv, swa_sinks, w_o_mla, w_o_swa, w_o_sb, w_out, g_mix_post, g_mlp_pre, w_up, w_down, g_mlp_post, loss_target, m_g_mix_pre, m_w_in, m_b_gate, m_g_q_lat, m_g_kv_lat, m_w_uq, m_w_ukv, m_swa_sinks, m_w_o_mla, m_w_o_swa, m_w_o_sb, m_w_out, m_g_mix_post, m_g_mlp_pre, m_w_up, m_w_down, m_g_mlp_post, v_g_mix_pre, v_w_in, v_b_gate, v_g_q_lat, v_g_kv_lat, v_w_uq, v_w_ukv, v_swa_sinks, v_w_o_mla, v_w_o_swa, v_w_o_sb, v_w_out, v_g_mix_post, v_g_mlp_pre, v_w_up, v_w_down, v_g_mlp_post):
    a = dict(locals())
    S = x.shape[1]
    depth = w_in.shape[0]
    T = min(256, S)
    xs = x.reshape(S, D_MODEL)
    tb = _tables(positions)

    flat = jnp.concatenate([a[n].astype(BF16).reshape(-1) for n in BIG])
    gathered = _exchange("gather_weights", flat.reshape(-1, PACK_COLS), per_peer=False).reshape(N_DEV, -1)
    full, off = {}, 0
    for n in BIG:
        cnt = a[n].size
        full[n] = _from_shards(n, gathered[:, off:off + cnt], a[n].shape)
        off += cnt
    layers = []
    for l in range(depth):
        W = {n: full[n][l] for n in BIG}
        W["w_in"] = _perm_w_in(W["w_in"])
        W["w_uq"] = _perm_w_uq(W["w_uq"])
        W["w_ukv"] = _perm_w_ukv(W["w_ukv"])
        P = {n: a[n][l].reshape(1, -1) for n in SMALL if n != "swa_sinks"}
        P["swa_sinks"] = a["swa_sinks"][l]
        layers.append((W, P))

    saved = []
    h = xs
    for l, (W, P) in enumerate(layers):
        h, sv = _layer_fwd(l, h, W, P, tb, T)
        saved.append(sv)

    def loss_head(yv, tv):
        err = yv - tv
        part = 0.5 * jnp.sum(jnp.mean(err * err, axis=1, keepdims=True), axis=0, keepdims=True)
        return err * (1.0 / D_MODEL), jnp.broadcast_to(part, (1, LANES))

    dh, loss_row = _rowwise("loss_head", loss_head, [_ri(h, T), _ri(loss_target.reshape(S, D_MODEL), T)], [_ro(S, D_MODEL, F32, T)], [LANES],
                            n_tiles=S // T)
    loss = lax.psum(loss_row[0, 0], ("x", "y", "c"))

    grads = [None] * depth
    for l in reversed(range(depth)):
        W, P = layers[l]
        dh, grads[l] = _layer_bwd(l, dh, saved[l], W, P, tb, T)
    grad_x = dh.reshape(x.shape)

    unperm = dict(w_in=_unperm_w_in, w_uq=_unperm_w_uq, w_ukv=_unperm_w_ukv)
    parts = []
    for n in BIG:
        per_layer = [_to_shards(n, unperm.get(n, lambda t: t)(grads[l][n]).astype(BF16)) for l in range(depth)]
        parts.append(jnp.concatenate(per_layer, axis=1))
    send = jnp.concatenate(parts, axis=1).reshape(N_DEV, -1, PACK_COLS)
    recv = _exchange("scatter_grads", send, per_peer=True).reshape(N_DEV, -1)
    out = {}
    off = 0
    for n in BIG:
        shp = a[n].shape
        cnt = a[n].size
        rows, cols = shp[0] * shp[1], shp[2]
        res = _adamw("adamw_" + n, recv[:, off:off + cnt].reshape(N_DEV, rows, cols), a[n].reshape(rows, cols), a["m_" + n].reshape(rows, cols),
                     a["v_" + n].reshape(rows, cols), tile=min(256, rows))
        out[n] = [r.reshape(shp) for r in res]
        off += cnt

    small_total = sum(a[n].size for n in SMALL)
    small_rows = -(-small_total // (8 * PACK_COLS)) * 8
    sg = _pack_rows([jnp.stack([grads[l][n].reshape(-1) for l in range(depth)]) for n in SMALL], small_rows)
    sg_all = _exchange("gather_small_grads", sg, per_peer=False)
    res = _adamw("adamw_small", sg_all, _pack_rows([a[n] for n in SMALL], small_rows), _pack_rows([a["m_" + n] for n in SMALL], small_rows),
                 _pack_rows([a["v_" + n] for n in SMALL], small_rows), tile=small_rows)
    off = 0
    for n in SMALL:
        cnt = a[n].size
        out[n] = [r.reshape(-1)[off:off + cnt].reshape(a[n].shape) for r in res]
        off += cnt

    order = ("g_mix_pre", "w_in", "b_gate", "g_q_lat", "g_kv_lat", "w_uq", "w_ukv", "swa_sinks", "w_o_mla", "w_o_swa", "w_o_sb", "w_out",
             "g_mix_post", "g_mlp_pre", "w_up", "w_down", "g_mlp_post")
    return (loss, grad_x, *[out[n][0] for n in order], *[out[n][1] for n in order], *[out[n][2] for n in order], *[out[n][3] for n in order])
```

```python
import functools

import jax
import jax.numpy as jnp
from jax import lax
from jax.experimental import pallas as pl
from jax.experimental.pallas import tpu as pltpu

F32, BF16 = jnp.float32, jnp.bfloat16

D_MODEL = 1024
DEPTH = 4
MLA_HEADS, MLA_Q_LORA, MLA_KV_LORA, MLA_NOPE, MLA_ROPE, MLA_V = 8, 256, 128, 64, 32, 64
SWA_HEADS, SWA_KV_HEADS, SWA_HEAD_DIM, SWA_WINDOW = 8, 2, 64, 128
SB_HEADS, SB_HEAD_DIM = 8, 64
D_FF = 4 * D_MODEL
ROPE_THETA = 10000.0
EPS = 1e-6
N_DEV = 8
ADAM_LR, ADAM_B1, ADAM_B2, ADAM_EPS, ADAM_WD, ADAM_STEP = 0.001, 0.9, 0.999, 1e-08, 0.01, 10

LANES = 128
VMEM_LIMIT_MAX = 60 * 1024 * 1024
VMEM_LIMIT_MIN = 32 * 1024 * 1024

ORIG_COLS = dict(c_q=(0, 256), c_kv=(256, 128), k_rope=(384, 32), q_swa=(416, 512), k_swa=(928, 128), v_swa=(1056, 128),
                 q_sb=(1184, 512), k_sb=(1696, 512), v_sb=(2208, 512), gates=(2720, 3072))
IN_WIDTH = 5792
PERM_ORDER = ("gates", "q_swa", "q_sb", "k_sb", "v_sb", "c_q", "c_kv", "k_swa", "v_swa", "k_rope")
PERM_WIDTH = dict(gates=3072, q_swa=512, q_sb=512, k_sb=512, v_sb=512, c_q=256, c_kv=128, k_swa=128, v_swa=128, k_rope=128)
PERM_OFF = {}
_o = 0
for _n in PERM_ORDER:
    PERM_OFF[_n] = _o
    _o += PERM_WIDTH[_n]
IN_WIDTH_P = _o
MLA_QW = MLA_HEADS * LANES
MLA_KVW = MLA_HEADS * LANES + MLA_HEADS * MLA_V

NT = (((1,), (1,)), ((), ()))
TN = (((0,), (0,)), ((), ()))
NEG = -1e30


def _cparams(sem, block_bytes):
    limit = int(min(VMEM_LIMIT_MAX, max(VMEM_LIMIT_MIN, 2 * block_bytes + (16 << 20))))
    return pltpu.CompilerParams(dimension_semantics=sem, vmem_limit_bytes=limit)


def _nbytes(shape, dtype):
    n = 1
    for s in shape:
        n *= s
    return n * jnp.dtype(dtype).itemsize


def _ri(arr, tile, width=None, cb=0):
    width = arr.shape[1] if width is None else width
    return (arr, (tile, width), lambda i, cb=cb: (i, cb))


def _bi(arr):
    return (arr, arr.shape, lambda i: (0, 0))


def _ro(rows, width, dtype, tile):
    return ((rows, width), dtype, (tile, width), lambda i: (i, 0))


def _rowwise(name, fn, ins, outs, reds=(), *, n_tiles):
    n_in, n_out = len(ins), len(outs)

    def body(*refs):
        vals = fn(*[r[...] for r in refs[:n_in]])
        for r, v in zip(refs[n_in:n_in + n_out], vals[:n_out]):
            if isinstance(v, (list, tuple)):
                for j, vj in enumerate(v):
                    r[j] = vj.astype(r.dtype)
            else:
                r[...] = v.astype(r.dtype)
        if reds:
            @pl.when(pl.program_id(0) == 0)
            def _():
                for r in refs[n_in + n_out:]:
                    r[...] = jnp.zeros_like(r)
            for r, v in zip(refs[n_in + n_out:], vals[n_out:]):
                r[...] += v

    block_bytes = sum(_nbytes(b, a.dtype) for a, b, _ in ins) + sum(_nbytes(b, d) for _, d, b, _ in outs)
    res = pl.pallas_call(
        body, name=name, grid=(n_tiles,),
        in_specs=[pl.BlockSpec(b, m) for _, b, m in ins],
        out_specs=[pl.BlockSpec(b, m) for _, _, b, m in outs] + [pl.BlockSpec((1, w), lambda i: (0, 0)) for w in reds],
        out_shape=[jax.ShapeDtypeStruct(s, d) for s, d, _, _ in outs] + [jax.ShapeDtypeStruct((1, w), F32) for w in reds],
        compiler_params=_cparams(("arbitrary",) if reds else ("parallel",), block_bytes),
    )(*[a for a, _, _ in ins])
    return res


def _rms(x, g):
    r = lax.rsqrt(jnp.mean(x * x, axis=1, keepdims=True) + EPS)
    return x * r * g


def _rms_bwd(x, g, dy):
    r = lax.rsqrt(jnp.mean(x * x, axis=1, keepdims=True) + EPS)
    xn = x * r
    dxn = dy * g
    dx = r * (dxn - xn * jnp.mean(dxn * xn, axis=1, keepdims=True))
    return dx, jnp.sum(dy * xn, axis=0, keepdims=True)


def _swap_halves(x, half):
    n = x.shape[1]
    lane = lax.broadcasted_iota(jnp.int32, x.shape, 1)
    first = (lane % (2 * half)) < half
    return jnp.where(first, pltpu.roll(x, n - half, axis=1), pltpu.roll(x, half, axis=1))


def _rope(x, c, sg, half):
    return x * c + _swap_halves(x, half) * sg


def _rope_t(dy, c, sg, half):
    return dy * c - _swap_halves(dy, half) * sg


def _mm_nn(name, a, b, outs, *, tm, tn, extras=(), epilogue=None):
    M, K = a.shape
    N = b.shape[1]
    tm = min(tm, M)
    n_e = len(extras)

    def body(*refs):
        a_ref, b_ref = refs[:2]
        acc = jnp.dot(a_ref[...].astype(BF16), b_ref[...].astype(BF16), preferred_element_type=F32)
        vals = (acc,) * len(outs) if epilogue is None else epilogue(acc, *[r[...] for r in refs[2:2 + n_e]])
        for r, v in zip(refs[2 + n_e:], vals):
            r[...] = v.astype(r.dtype)

    block_bytes = (_nbytes((tm, K), a.dtype) + _nbytes((K, tn), b.dtype) + sum(_nbytes((tm, tn), e.dtype) for e in extras)
                   + sum(_nbytes((tm, tn), d) for d in outs) + _nbytes((tm, tn), F32))
    return pl.pallas_call(
        body, name=name, grid=(N // tn, M // tm),
        in_specs=[pl.BlockSpec((tm, K), lambda j, i: (i, 0)), pl.BlockSpec((K, tn), lambda j, i: (0, j))]
        + [pl.BlockSpec((tm, tn), lambda j, i: (i, j)) for _ in extras],
        out_specs=[pl.BlockSpec((tm, tn), lambda j, i: (i, j)) for _ in outs],
        out_shape=[jax.ShapeDtypeStruct((M, N), d) for d in outs],
        compiler_params=_cparams(("parallel", "parallel"), block_bytes),
    )(a, b, *extras)


def _mm_nt(name, a, b, outs, *, tm, tn, extras=(), epilogue=None):
    M, N = a.shape
    K = b.shape[0]
    tm = min(tm, M)
    n_e = len(extras)

    def body(*refs):
        a_ref, b_ref = refs[:2]
        acc = lax.dot_general(a_ref[...].astype(BF16), b_ref[...].astype(BF16), NT, preferred_element_type=F32)
        vals = (acc,) * len(outs) if epilogue is None else epilogue(acc, *[r[...] for r in refs[2:2 + n_e]])
        for r, v in zip(refs[2 + n_e:], vals):
            r[...] = v.astype(r.dtype)

    block_bytes = (_nbytes((tm, N), a.dtype) + _nbytes((tn, N), b.dtype) + sum(_nbytes((tm, tn), e.dtype) for e in extras)
                   + sum(_nbytes((tm, tn), d) for d in outs) + _nbytes((tm, tn), F32))
    return pl.pallas_call(
        body, name=name, grid=(K // tn, M // tm),
        in_specs=[pl.BlockSpec((tm, N), lambda j, i: (i, 0)), pl.BlockSpec((tn, N), lambda j, i: (j, 0))]
        + [pl.BlockSpec((tm, tn), lambda j, i: (i, j)) for _ in extras],
        out_specs=[pl.BlockSpec((tm, tn), lambda j, i: (i, j)) for _ in outs],
        out_shape=[jax.ShapeDtypeStruct((M, K), d) for d in outs],
        compiler_params=_cparams(("parallel", "parallel"), block_bytes),
    )(a, b, *extras)


def _mm_tn(name, a, b, *, tm, tn, ts):
    S, K = a.shape
    N = b.shape[1]
    ts = min(ts, S)

    def body(a_ref, b_ref, o_ref):
        @pl.when(pl.program_id(2) == 0)
        def _():
            o_ref[...] = jnp.zeros_like(o_ref)
        o_ref[...] += lax.dot_general(a_ref[...].astype(BF16), b_ref[...].astype(BF16), TN, preferred_element_type=F32)

    block_bytes = _nbytes((ts, tm), a.dtype) + _nbytes((ts, tn), b.dtype) + 2 * _nbytes((tm, tn), F32)
    return pl.pallas_call(
        body, name=name, grid=(K // tm, N // tn, S // ts),
        in_specs=[pl.BlockSpec((ts, tm), lambda i, j, s: (s, i)), pl.BlockSpec((ts, tn), lambda i, j, s: (s, j))],
        out_specs=pl.BlockSpec((tm, tn), lambda i, j, s: (i, j)),
        out_shape=jax.ShapeDtypeStruct((K, N), F32),
        compiler_params=_cparams(("parallel", "parallel", "arbitrary"), block_bytes),
    )(a, b)


def _lane_pack(cols, rows):
    lane = lax.broadcasted_iota(jnp.int32, (rows, LANES), 1)
    val = jnp.zeros((rows, LANES), F32)
    for h, c in enumerate(cols):
        val = jnp.where(lane == h, c, val)
    return val


def _mask(kb, row, tk, window):
    col = kb * tk + lax.broadcasted_iota(jnp.int32, (1, tk), 1)
    ok = col <= row
    if window:
        ok = ok & ((row - col) < window)
    return ok


def _kb_range(i, tq, tk, window):
    end = ((i + 1) * tq + tk - 1) // tk
    if window:
        lo = jnp.maximum(i * tq - (window - 1), 0) // tk
        return lo, lo, end
    return 0, (i * tq) // tk, end


def _softmax_fwd(name, q, k, v, sinks, *, groups, heads, q_spec, k_spec, v_spec, o_width, tq, tk, scale, window, prescale):
    S = q.shape[0]
    nq = S // tq
    nh = len(heads)
    has_sink = sinks is not None
    dv = heads[0][2].stop - heads[0][2].start

    def body(*refs):
        if has_sink:
            sink_ref, q_ref, k_ref, v_ref, o_ref, lse_ref = refs
        else:
            q_ref, k_ref, v_ref, o_ref, lse_ref = refs
        g, i = pl.program_id(0), pl.program_id(1)
        q_all = q_ref[...]
        row = i * tq + lax.broadcasted_iota(jnp.int32, (tq, 1), 0)
        lo, mid, hi = _kb_range(i, tq, tk, window)
        qhs = [q_all[:, qs] * scale if prescale else q_all[:, qs] for qs, _, _ in heads]

        def blk(kb, carry, masked):
            r0 = pl.multiple_of(kb * tk, tk)
            k_all = k_ref[pl.ds(r0, tk), :]
            v_all = v_ref[pl.ds(r0, tk), :]
            if masked:
                ok = _mask(kb, row, tk, window)
            new = []
            for qh, (_, ks, vs), (m, l, acc) in zip(qhs, heads, carry):
                s = lax.dot_general(qh, k_all[:, ks], NT, preferred_element_type=F32)
                if not prescale:
                    s = s * scale
                if masked:
                    s = jnp.where(ok, s, NEG)
                m_new = jnp.maximum(m, jnp.max(s, axis=1, keepdims=True))
                alpha = jnp.exp(m - m_new)
                p = jnp.exp(s - m_new)
                if masked:
                    p = jnp.where(ok, p, 0.0)
                l = alpha * l + jnp.sum(p, axis=1, keepdims=True)
                acc = alpha * acc + jnp.dot(p.astype(BF16), v_all[:, vs], preferred_element_type=F32)
                new.append((m_new, l, acc))
            return tuple(new)

        carry = []
        for h in range(nh):
            if has_sink:
                m0 = jnp.full((tq, 1), sink_ref[g * nh + h], F32)
                l0 = jnp.ones((tq, 1), F32)
            else:
                m0 = jnp.full((tq, 1), NEG, F32)
                l0 = jnp.zeros((tq, 1), F32)
            carry.append((m0, l0, jnp.zeros((tq, dv), F32)))
        carry = lax.fori_loop(lo, mid, functools.partial(blk, masked=False), tuple(carry))
        carry = lax.fori_loop(mid, hi, functools.partial(blk, masked=True), carry)
        o_ref[...] = jnp.concatenate([acc / l for _, l, acc in carry], axis=1).astype(o_ref.dtype)
        lse_ref[0] = _lane_pack([m + jnp.log(l) for m, l, _ in carry], tq)

    in_specs = [pl.BlockSpec(*q_spec), pl.BlockSpec(*k_spec), pl.BlockSpec(*v_spec)]
    args = [q, k, v]
    if has_sink:
        in_specs = [pl.BlockSpec(memory_space=pltpu.SMEM)] + in_specs
        args = [sinks] + args
    wo = nh * dv
    block_bytes = _nbytes(q_spec[0], q.dtype) + _nbytes(k_spec[0], k.dtype) + _nbytes(v_spec[0], v.dtype) + 4 * tq * (wo + LANES)
    return pl.pallas_call(
        body, name=name, grid=(groups, nq), in_specs=in_specs,
        out_specs=[pl.BlockSpec((tq, wo), lambda g, i: (i, g)), pl.BlockSpec((1, tq, LANES), lambda g, i: (g, i, 0))],
        out_shape=[jax.ShapeDtypeStruct((S, o_width), BF16), jax.ShapeDtypeStruct((groups, S, LANES), F32)],
        compiler_params=_cparams(("parallel", "arbitrary"), block_bytes),
    )(*args)


def _softmax_bwd(name, q, k, v, o, do, lse, sinks, *, groups, heads, q_spec, k_spec, v_spec, tq, tk, scale, window, prescale):
    S = q.shape[0]
    nq = S // tq
    nh = len(heads)
    has_sink = sinks is not None
    dv = heads[0][2].stop - heads[0][2].start
    wo = nh * dv

    def body(*refs):
        if has_sink:
            sink_ref, q_ref, k_ref, v_ref, o_ref, do_ref, lse_ref, dq_ref, dk_ref, dv_ref, dsink_ref = refs
        else:
            q_ref, k_ref, v_ref, o_ref, do_ref, lse_ref, dq_ref, dk_ref, dv_ref = refs
        g, i = pl.program_id(0), pl.program_id(1)

        @pl.when(i == 0)
        def _():
            dk_ref[...] = jnp.zeros_like(dk_ref)
            dv_ref[...] = jnp.zeros_like(dv_ref)
            if has_sink:
                dsink_ref[...] = jnp.zeros_like(dsink_ref)

        q_all = q_ref[...]
        o_all = o_ref[...].astype(F32)
        do_all = do_ref[...].astype(F32)
        lse_all = lse_ref[0]
        row = i * tq + lax.broadcasted_iota(jnp.int32, (tq, 1), 0)
        lo, mid, hi = _kb_range(i, tq, tk, window)
        per_head = []
        for h, (qs, ks, vs) in enumerate(heads):
            osl = slice(h * dv, (h + 1) * dv)
            doh = do_all[:, osl]
            delta = jnp.sum(doh * o_all[:, osl], axis=1, keepdims=True)
            qh = q_all[:, qs] * scale if prescale else q_all[:, qs]
            per_head.append((qh, doh.astype(BF16), delta, lse_all[:, h:h + 1], ks, vs))

        def blk(kb, dqs, masked):
            r0 = pl.multiple_of(kb * tk, tk)
            k_all = k_ref[pl.ds(r0, tk), :]
            v_all = v_ref[pl.ds(r0, tk), :]
            if masked:
                ok = _mask(kb, row, tk, window)
            dk_parts, dv_parts, new_dqs = {}, {}, []
            for (qh, doh, delta, lse_h, ks, vs), dq in zip(per_head, dqs):
                kk, vv = k_all[:, ks], v_all[:, vs]
                s = lax.dot_general(qh, kk, NT, preferred_element_type=F32)
                if not prescale:
                    s = s * scale
                p = jnp.exp(s - lse_h)
                if masked:
                    p = jnp.where(ok, p, 0.0)
                dp = lax.dot_general(doh, vv, NT, preferred_element_type=F32)
                ds = p * (dp - delta)
                if not prescale:
                    ds = ds * scale
                new_dqs.append(dq + jnp.dot(ds.astype(BF16), kk, preferred_element_type=F32))
                dk_c = jnp.dot(ds.T.astype(BF16), qh, preferred_element_type=F32)
                dv_c = jnp.dot(p.T.astype(BF16), doh, preferred_element_type=F32)
                dk_parts[ks.start] = dk_parts[ks.start] + dk_c if ks.start in dk_parts else dk_c
                dv_parts[vs.start] = dv_parts[vs.start] + dv_c if vs.start in dv_parts else dv_c
            dk_ref[pl.ds(r0, tk), :] += jnp.concatenate([dk_parts[s0] for s0 in sorted(dk_parts)], axis=1)
            dv_ref[pl.ds(r0, tk), :] += jnp.concatenate([dv_parts[s0] for s0 in sorted(dv_parts)], axis=1)
            return tuple(new_dqs)

        dqs = tuple(jnp.zeros((tq, qs.stop - qs.start), F32) for qs, _, _ in heads)
        dqs = lax.fori_loop(lo, mid, functools.partial(blk, masked=False), dqs)
        dqs = lax.fori_loop(mid, hi, functools.partial(blk, masked=True), dqs)
        dq_ref[...] = jnp.concatenate([dq * scale if prescale else dq for dq in dqs], axis=1)
        if has_sink:
            for h, (_, _, delta, lse_h, _, _) in enumerate(per_head):
                p_sink = jnp.exp(sink_ref[g * nh + h] - lse_h)
                dsink_ref[0, h:h + 1, :] += jnp.broadcast_to(-jnp.sum(p_sink * delta, axis=0, keepdims=True), (1, LANES))

    in_specs = [pl.BlockSpec(*q_spec), pl.BlockSpec(*k_spec), pl.BlockSpec(*v_spec),
                pl.BlockSpec((tq, wo), lambda g, i: (i, g)), pl.BlockSpec((tq, wo), lambda g, i: (i, g)),
                pl.BlockSpec((1, tq, LANES), lambda g, i: (g, i, 0))]
    args = [q, k, v, o, do, lse]
    out_specs = [pl.BlockSpec(*q_spec), pl.BlockSpec(*k_spec), pl.BlockSpec(*v_spec)]
    out_shape = [jax.ShapeDtypeStruct(q.shape, F32), jax.ShapeDtypeStruct(k.shape, F32), jax.ShapeDtypeStruct(v.shape, F32)]
    if has_sink:
        in_specs = [pl.BlockSpec(memory_space=pltpu.SMEM)] + in_specs
        args = [sinks] + args
        out_specs.append(pl.BlockSpec((1, 8, LANES), lambda g, i: (g, 0, 0)))
        out_shape.append(jax.ShapeDtypeStruct((groups, 8, LANES), F32))
    block_bytes = (_nbytes(q_spec[0], q.dtype) + _nbytes(k_spec[0], k.dtype) + _nbytes(v_spec[0], v.dtype) + 6 * tq * wo + 4 * tq * LANES
                   + _nbytes(q_spec[0], F32) + _nbytes(k_spec[0], F32) + _nbytes(v_spec[0], F32))
    return pl.pallas_call(
        body, name=name, grid=(groups, nq), in_specs=in_specs, out_specs=out_specs, out_shape=out_shape,
        compiler_params=_cparams(("parallel", "arbitrary"), block_bytes),
    )(*args)


def _split_dot(x, u):
    hi = x.astype(BF16)
    lo = (x - hi.astype(F32)).astype(BF16)
    return jnp.dot(hi, u, preferred_element_type=F32) + jnp.dot(lo, u, preferred_element_type=F32)


def _suffix_ones():
    r = lax.broadcasted_iota(jnp.int32, (LANES, 2 * LANES), 0)
    c = lax.broadcasted_iota(jnp.int32, (LANES, 2 * LANES), 1)
    return ((r > c) | (c >= LANES)).astype(BF16)


def _suffix_scan(x, uo, run):
    nc = x.shape[1] // LANES
    out = [None] * nc
    for c in reversed(range(nc)):
        st = _split_dot(x[:, c * LANES:(c + 1) * LANES], uo)
        out[c] = st[:, :LANES] + run
        run = run + st[:, LANES:]
    return (out[0] if nc == 1 else jnp.concatenate(out, axis=1)), run


def _sb_logits(qh, kk, kb, row, tq, masked):
    z = lax.dot_general(qh, kk, NT, preferred_element_type=F32)
    nz = -z
    l = jnp.minimum(nz, 0.0) - jnp.log(1.0 + jnp.exp(jnp.minimum(z, nz)))
    ok = None
    if masked:
        col = kb * tq + lax.broadcasted_iota(jnp.int32, (1, tq), 1)
        ok = col < row
        l = jnp.where(ok, l, 0.0)
    return z, l, ok


def _sb_fwd(name, pb, *, q_cb, k_cb, v_cb, tq, scale):
    S = pb.shape[0]
    nq = S // tq
    hd = SB_HEAD_DIM
    groups = SB_HEADS * hd // LANES
    nh = LANES // hd

    def body(q_ref, k_ref, v_ref, o16_ref, o32_ref):
        i = pl.program_id(1)
        q_all = q_ref[...]
        row = i * tq + lax.broadcasted_iota(jnp.int32, (tq, 1), 0)
        uo = _suffix_ones()
        sls = [slice(h * hd, (h + 1) * hd) for h in range(nh)]
        qhs = [q_all[:, sl] * scale for sl in sls]

        def blk(t, carry, masked, base):
            kb = base - t
            r0 = pl.multiple_of(kb * tq, tq)
            k_all = k_ref[pl.ds(r0, tq), :]
            v_all = v_ref[pl.ds(r0, tq), :]
            new = []
            for qh, sl, (run_l, acc) in zip(qhs, sls, carry):
                z, l, ok = _sb_logits(qh, k_all[:, sl], kb, row, tq, masked)
                tail, run_l = _suffix_scan(l, uo, run_l)
                e = z + l + tail
                if masked:
                    e = jnp.where(ok, e, NEG)
                new.append((run_l, acc + jnp.dot(jnp.exp(e).astype(BF16), v_all[:, sl], preferred_element_type=F32)))
            return tuple(new)

        carry = tuple((jnp.zeros((tq, LANES), F32), jnp.zeros((tq, hd), F32)) for _ in range(nh))
        carry = blk(0, carry, True, i)
        carry = lax.fori_loop(0, i, functools.partial(blk, masked=False, base=i - 1), carry)
        o = jnp.concatenate([acc for _, acc in carry], axis=1)
        o16_ref[...] = o.astype(BF16)
        o32_ref[...] = o

    block_bytes = 2 * tq * LANES + 2 * 2 * S * LANES + 6 * tq * LANES
    return pl.pallas_call(
        body, name=name, grid=(groups, nq),
        in_specs=[pl.BlockSpec((tq, LANES), lambda g, i: (i, q_cb + g)), pl.BlockSpec((S, LANES), lambda g, i: (0, k_cb + g)),
                  pl.BlockSpec((S, LANES), lambda g, i: (0, v_cb + g))],
        out_specs=[pl.BlockSpec((tq, LANES), lambda g, i: (i, g)), pl.BlockSpec((tq, LANES), lambda g, i: (i, g))],
        out_shape=[jax.ShapeDtypeStruct((S, groups * LANES), BF16), jax.ShapeDtypeStruct((S, groups * LANES), F32)],
        compiler_params=_cparams(("parallel", "arbitrary"), block_bytes),
    )(pb, pb, pb)


def _sb_bwd(name, pb, o32, do, *, q_cb, k_cb, v_cb, tq, scale):
    S = pb.shape[0]
    nq = S // tq
    hd = SB_HEAD_DIM
    groups = SB_HEADS * hd // LANES
    nh = LANES // hd

    def body(q_ref, k_ref, v_ref, o_ref, do_ref, dq_ref, dk_ref, dv_ref):
        i = pl.program_id(1)

        @pl.when(i == 0)
        def _():
            dk_ref[...] = jnp.zeros_like(dk_ref)
            dv_ref[...] = jnp.zeros_like(dv_ref)

        q_all = q_ref[...]
        o_all = o_ref[...]
        do_all = do_ref[...].astype(F32)
        row = i * tq + lax.broadcasted_iota(jnp.int32, (tq, 1), 0)
        uo = _suffix_ones()
        per_head = []
        for h in range(nh):
            sl = slice(h * hd, (h + 1) * hd)
            doh = do_all[:, sl].astype(BF16)
            total = jnp.sum(doh.astype(F32) * o_all[:, sl], axis=1, keepdims=True)
            per_head.append((q_all[:, sl] * scale, doh, jnp.broadcast_to(total, (tq, LANES)), sl))

        def blk(t, carry, masked, base):
            kb = base - t
            r0 = pl.multiple_of(kb * tq, tq)
            k_all = k_ref[pl.ds(r0, tq), :]
            v_all = v_ref[pl.ds(r0, tq), :]
            new, dk_c, dv_c = [], [], []
            for (qh, doh, total, sl), (run_l, run_g, dq) in zip(per_head, carry):
                kk, vv = k_all[:, sl], v_all[:, sl]
                z, l, ok = _sb_logits(qh, kk, kb, row, tq, masked)
                tail, run_l = _suffix_scan(l, uo, run_l)
                e = z + l
                beta = jnp.exp(e)
                e = e + tail
                if masked:
                    e = jnp.where(ok, e, NEG)
                a = jnp.exp(e).astype(BF16).astype(F32)
                gr = lax.dot_general(doh, vv, NT, preferred_element_type=F32) * a
                right, run_g = _suffix_scan(gr, uo, run_g)
                nc = tq // LANES
                prefix = (total if nc == 1 else jnp.tile(total, (1, nc))) - right
                dz = gr - beta * prefix
                if masked:
                    dz = jnp.where(ok, dz, 0.0)
                new.append((run_l, run_g, dq + jnp.dot(dz.astype(BF16), kk, preferred_element_type=F32)))
                dk_c.append(jnp.dot(dz.T.astype(BF16), qh, preferred_element_type=F32))
                dv_c.append(jnp.dot(a.T.astype(BF16), doh, preferred_element_type=F32))
            dk_ref[pl.ds(r0, tq), :] += jnp.concatenate(dk_c, axis=1)
            dv_ref[pl.ds(r0, tq), :] += jnp.concatenate(dv_c, axis=1)
            return tuple(new)

        zc = jnp.zeros((tq, LANES), F32)
        carry = tuple((zc, zc, jnp.zeros((tq, hd), F32)) for _ in range(nh))
        carry = blk(0, carry, True, i)
        carry = lax.fori_loop(0, i, functools.partial(blk, masked=False, base=i - 1), carry)
        dq_ref[...] = jnp.concatenate([c[2] * scale for c in carry], axis=1)

    W = groups * LANES
    block_bytes = 2 * tq * LANES + 2 * 2 * S * LANES + 3 * 4 * tq * LANES + 2 * 4 * S * LANES
    return pl.pallas_call(
        body, name=name, grid=(groups, nq),
        in_specs=[pl.BlockSpec((tq, LANES), lambda g, i: (i, q_cb + g)), pl.BlockSpec((S, LANES), lambda g, i: (0, k_cb + g)),
                  pl.BlockSpec((S, LANES), lambda g, i: (0, v_cb + g)), pl.BlockSpec((tq, LANES), lambda g, i: (i, g)),
                  pl.BlockSpec((tq, LANES), lambda g, i: (i, g))],
        out_specs=[pl.BlockSpec((tq, LANES), lambda g, i: (i, g)), pl.BlockSpec((S, LANES), lambda g, i: (0, g)),
                   pl.BlockSpec((S, LANES), lambda g, i: (0, g))],
        out_shape=[jax.ShapeDtypeStruct((S, W), F32)] * 3,
        compiler_params=_cparams(("parallel", "arbitrary"), block_bytes),
    )(pb, pb, pb, o32, do)


def _exchange(name, srcs, per_peer):
    n = len(srcs)

    def body(*refs):
        src_refs, out_refs = refs[:n], refs[n:2 * n]
        send_sems, recv_sems, local_sems = refs[2 * n:]
        x, y, c = lax.axis_index("x"), lax.axis_index("y"), lax.axis_index("c")
        me = 4 * x + 2 * y + c

        def copy(j, k):
            px, py, pc = x ^ (k >> 2), y ^ ((k >> 1) & 1), c ^ (k & 1)
            s = src_refs[j].at[4 * px + 2 * py + pc] if per_peer else src_refs[j]
            return pltpu.make_async_remote_copy(
                src_ref=s, dst_ref=out_refs[j].at[me], send_sem=send_sems.at[j, k - 1], recv_sem=recv_sems.at[j, k - 1],
                device_id=(px, py, pc), device_id_type=pl.DeviceIdType.MESH)

        mine = [pltpu.make_async_copy(src_refs[j].at[me] if per_peer else src_refs[j], out_refs[j].at[me], local_sems.at[j]) for j in range(n)]
        copies = [copy(j, k) for k in range(1, N_DEV) for j in range(n)]
        for cp in mine + copies:
            cp.start()
        for cp in copies:
            cp.wait_recv()
        for cp in copies:
            cp.wait_send()
        for cp in mine:
            cp.wait()

    hbm = pl.BlockSpec(memory_space=pltpu.HBM)
    return pl.pallas_call(
        body, name=name, in_specs=[hbm] * n, out_specs=[hbm] * n,
        out_shape=[jax.ShapeDtypeStruct((N_DEV,) + tuple(s.shape[1:] if per_peer else s.shape), s.dtype) for s in srcs],
        scratch_shapes=[pltpu.SemaphoreType.DMA((n, N_DEV - 1)), pltpu.SemaphoreType.DMA((n, N_DEV - 1)), pltpu.SemaphoreType.DMA((n,))],
    )(*srcs)


def _adamw(name, parts, w, m, v, *, tile):
    R, C = w.shape

    def body(p_ref, w_ref, m_ref, v_ref, g_ref, d_ref, nm_ref, nv_ref):
        g = p_ref[0].astype(F32)
        for d in range(1, N_DEV):
            g = g + p_ref[d].astype(F32)
        wv = w_ref[...]
        mm = ADAM_B1 * m_ref[...] + (1.0 - ADAM_B1) * g
        vv = ADAM_B2 * v_ref[...] + (1.0 - ADAM_B2) * jnp.square(g)
        m_hat = mm / (1.0 - ADAM_B1 ** ADAM_STEP)
        v_hat = vv / (1.0 - ADAM_B2 ** ADAM_STEP)
        g_ref[...] = g
        d_ref[...] = -ADAM_LR * (m_hat / (jnp.sqrt(v_hat) + ADAM_EPS) + ADAM_WD * wv)
        nm_ref[...] = mm
        nv_ref[...] = vv

    blk = pl.BlockSpec((tile, C), lambda i: (i, 0))
    block_bytes = N_DEV * _nbytes((tile, C), parts.dtype) + 7 * _nbytes((tile, C), F32)
    return pl.pallas_call(
        body, name=name, grid=(R // tile,),
        in_specs=[pl.BlockSpec((N_DEV, tile, C), lambda i: (0, i, 0)), blk, blk, blk],
        out_specs=[blk] * 4, out_shape=[jax.ShapeDtypeStruct((R, C), F32)] * 4,
        compiler_params=_cparams(("parallel",), block_bytes),
    )(parts, w, m, v)


def _perm_w_in(w):
    cols = [w[:, ORIG_COLS[n][0]:ORIG_COLS[n][0] + ORIG_COLS[n][1]] for n in PERM_ORDER]
    cols.append(jnp.zeros((w.shape[0], IN_WIDTH_P - IN_WIDTH), w.dtype))
    return jnp.concatenate(cols, axis=1)


def _unperm_w_in(wp):
    order = sorted(ORIG_COLS, key=lambda n: ORIG_COLS[n][0])
    return jnp.concatenate([wp[:, PERM_OFF[n]:PERM_OFF[n] + ORIG_COLS[n][1]] for n in order], axis=1)


def _perm_w_uq(w):
    w3 = w.reshape(w.shape[0], MLA_HEADS, MLA_NOPE + MLA_ROPE)
    return jnp.pad(w3, ((0, 0), (0, 0), (0, LANES - MLA_NOPE - MLA_ROPE))).reshape(w.shape[0], MLA_QW)


def _unperm_w_uq(wp):
    return wp.reshape(wp.shape[0], MLA_HEADS, LANES)[:, :, :MLA_NOPE + MLA_ROPE].reshape(wp.shape[0], -1)


def _perm_w_ukv(w):
    w3 = w.reshape(w.shape[0], MLA_HEADS, MLA_NOPE + MLA_V)
    kp = jnp.pad(w3[:, :, :MLA_NOPE], ((0, 0), (0, 0), (0, LANES - MLA_NOPE))).reshape(w.shape[0], MLA_HEADS * LANES)
    return jnp.concatenate([kp, w3[:, :, MLA_NOPE:].reshape(w.shape[0], MLA_HEADS * MLA_V)], axis=1)


def _unperm_w_ukv(wp):
    n = wp.shape[0]
    kp = wp[:, :MLA_HEADS * LANES].reshape(n, MLA_HEADS, LANES)[:, :, :MLA_NOPE]
    vp = wp[:, MLA_HEADS * LANES:].reshape(n, MLA_HEADS, MLA_V)
    return jnp.concatenate([kp, vp], axis=2).reshape(n, MLA_HEADS * (MLA_NOPE + MLA_V))


BIG = ("w_in", "w_uq", "w_ukv", "w_o_mla", "w_o_swa", "w_o_sb", "w_out", "w_up", "w_down")
ROW_SHARDED = ("w_out", "w_down")
SMALL = ("g_mix_pre", "b_gate", "g_q_lat", "g_kv_lat", "swa_sinks", "g_mix_post", "g_mlp_pre", "g_mlp_post")
PACK_COLS = 1024


def _to_shards(name, full):
    L, R, C = full.shape
    if name in ROW_SHARDED:
        return full.reshape(L, N_DEV, R // N_DEV, C).transpose(1, 0, 2, 3).reshape(N_DEV, L * R // N_DEV, C)
    return full.reshape(L, R, N_DEV, C // N_DEV).transpose(2, 0, 1, 3).reshape(N_DEV, L * R, C // N_DEV)


def _from_shards(name, gathered, shard_shape):
    L, r, c = shard_shape
    a = gathered.reshape(N_DEV, L, r, c)
    if name in ROW_SHARDED:
        return a.transpose(1, 0, 2, 3).reshape(L, N_DEV * r, c)
    return a.transpose(1, 2, 0, 3).reshape(L, r, N_DEV * c)


def _tables(positions):
    pos = positions.astype(F32).reshape(-1, 1)

    def cs(d):
        inv = 1.0 / (ROPE_THETA ** (jnp.arange(0, d, 2, dtype=F32) / d))
        ang = pos * inv
        c, s = jnp.cos(ang), jnp.sin(ang)
        return jnp.concatenate([c, c], axis=1), jnp.concatenate([-s, s], axis=1)

    c64, s64 = cs(SWA_HEAD_DIM)
    c32, s32 = cs(MLA_ROPE)
    n = pos.shape[0]
    one, zero = jnp.ones((n, MLA_NOPE), F32), jnp.zeros((n, MLA_NOPE), F32)
    pad0 = jnp.zeros((n, LANES - MLA_NOPE - MLA_ROPE), F32)
    cq = jnp.concatenate([one, c32, pad0], axis=1)
    sq = jnp.concatenate([zero, s32, pad0], axis=1)
    padk = jnp.zeros((n, LANES - MLA_ROPE), F32)
    return dict(
        c_swa_q=jnp.tile(c64, (1, SWA_HEADS)), s_swa_q=jnp.tile(s64, (1, SWA_HEADS)),
        c_swa_k=jnp.tile(c64, (1, SWA_KV_HEADS)), s_swa_k=jnp.tile(s64, (1, SWA_KV_HEADS)),
        c_mla_q=jnp.tile(cq, (1, MLA_HEADS)), s_mla_q=jnp.tile(sq, (1, MLA_HEADS)),
        c_mla_k=jnp.concatenate([c32, padk], axis=1), s_mla_k=jnp.concatenate([s32, padk], axis=1))


def _cb(name):
    return PERM_OFF[name] // PERM_WIDTH[name]


MLA_SPEC = dict(groups=MLA_HEADS // 2, tq=256, tk=512, scale=(MLA_NOPE + MLA_ROPE) ** -0.5, window=0, prescale=False,
                heads=[(slice(h * LANES, (h + 1) * LANES), slice(h * LANES, (h + 1) * LANES), slice(h * MLA_V, (h + 1) * MLA_V)) for h in range(2)])
SWA_G = SWA_HEADS // SWA_KV_HEADS
SWA_SPEC = dict(groups=SWA_KV_HEADS, tq=128, tk=128, scale=SWA_HEAD_DIM ** -0.5, window=SWA_WINDOW, prescale=True,
                heads=[(slice(g * SWA_HEAD_DIM, (g + 1) * SWA_HEAD_DIM), slice(0, SWA_HEAD_DIM), slice(0, SWA_HEAD_DIM)) for g in range(SWA_G)])
SB_SPEC = dict(tq=256, scale=SB_HEAD_DIM ** -0.5, q_cb=PERM_OFF["q_sb"] // LANES, k_cb=PERM_OFF["k_sb"] // LANES,
               v_cb=PERM_OFF["v_sb"] // LANES)


def _mla_specs(S):
    tq = MLA_SPEC["tq"]
    return dict(q_spec=((tq, 2 * LANES), lambda g, i: (i, g)), k_spec=((S, 2 * LANES), lambda g, i: (0, g)),
                v_spec=((S, 2 * MLA_V), lambda g, i: (0, g)))


def _swa_specs(S):
    tq = SWA_SPEC["tq"]
    return dict(q_spec=((tq, SWA_G * SWA_HEAD_DIM), lambda g, i: (i, g)), k_spec=((S, SWA_HEAD_DIM), lambda g, i: (g, 0)),
                v_spec=((S, SWA_HEAD_DIM), lambda g, i: (g, 0)))


def _layer_fwd(l, x, W, P, tb, T):
    S = x.shape[0]
    nt = S // T
    h, = _rowwise(f"l{l}_norm_in", lambda xv, g: (_rms(xv, g),), [_ri(x, T), _bi(P["g_mix_pre"])], [_ro(S, D_MODEL, BF16, T)], n_tiles=nt)
    proj, proj16 = _mm_nn(f"l{l}_mm_in", h, W["w_in"], [F32, BF16], tm=512, tn=IN_WIDTH_P // 2)

    def mix_prep(cq, ckv, qs, ks, vs, gq, gkv, cq_t, sq_t, ck_t, sk_t):
        ksr = _rope(ks, ck_t, sk_t, SWA_HEAD_DIM // 2)
        hd = SWA_HEAD_DIM
        return (_rms(cq, gq), _rms(ckv, gkv), _rope(qs, cq_t, sq_t, hd // 2),
                [ksr[:, :hd], ksr[:, hd:]], [vs[:, :hd], vs[:, hd:]])

    kv3 = lambda dt: ((SWA_KV_HEADS, S, SWA_HEAD_DIM), dt, (SWA_KV_HEADS, T, SWA_HEAD_DIM), lambda i: (0, i, 0))
    cqn, ckvn, q_swa, k_swa, v_swa = _rowwise(
        f"l{l}_mix_prep", mix_prep,
        [_ri(proj, T, 256, _cb("c_q")), _ri(proj, T, 128, _cb("c_kv")), _ri(proj, T, 512, _cb("q_swa")), _ri(proj, T, 128, _cb("k_swa")),
         _ri(proj, T, 128, _cb("v_swa")), _bi(P["g_q_lat"]), _bi(P["g_kv_lat"]), _ri(tb["c_swa_q"], T), _ri(tb["s_swa_q"], T),
         _ri(tb["c_swa_k"], T), _ri(tb["s_swa_k"], T)],
        [_ro(S, 256, BF16, T), _ro(S, 128, BF16, T), _ro(S, 512, BF16, T), kv3(BF16), kv3(BF16)], n_tiles=nt)
    k_swa = k_swa.reshape(SWA_KV_HEADS * S, SWA_HEAD_DIM)
    v_swa = v_swa.reshape(SWA_KV_HEADS * S, SWA_HEAD_DIM)
    q_lat, = _mm_nn(f"l{l}_mm_uq", cqn, W["w_uq"], [F32], tm=1024, tn=MLA_QW)
    kv_lat, = _mm_nn(f"l{l}_mm_ukv", ckvn, W["w_ukv"], [F32], tm=1024, tn=MLA_KVW)

    def mla_prep(q, kk, vv, kr, cq_t, sq_t, ck_t, sk_t):
        kpe = pltpu.roll(_rope(kr, ck_t, sk_t, MLA_ROPE // 2), MLA_NOPE, axis=1)
        return _rope(q, cq_t, sq_t, MLA_ROPE // 2), kk + jnp.tile(kpe, (1, MLA_HEADS)), vv

    q_mla, k_mla, v_mla = _rowwise(
        f"l{l}_mla_prep", mla_prep,
        [_ri(q_lat, T), _ri(kv_lat, T, MLA_HEADS * LANES, 0), _ri(kv_lat, T, MLA_HEADS * MLA_V, 2), _ri(proj, T, 128, _cb("k_rope")),
         _ri(tb["c_mla_q"], T), _ri(tb["s_mla_q"], T), _ri(tb["c_mla_k"], T), _ri(tb["s_mla_k"], T)],
        [_ro(S, MLA_QW, BF16, T), _ro(S, MLA_HEADS * LANES, BF16, T), _ro(S, MLA_HEADS * MLA_V, BF16, T)], n_tiles=nt)

    att_a, lse_a = _softmax_fwd(f"l{l}_mla_fwd", q_mla, k_mla, v_mla, None, o_width=MLA_HEADS * MLA_V, **MLA_SPEC, **_mla_specs(S))
    att_b, lse_b = _softmax_fwd(f"l{l}_swa_fwd", q_swa, k_swa, v_swa, P["swa_sinks"], o_width=SWA_HEADS * SWA_HEAD_DIM, **SWA_SPEC, **_swa_specs(S))
    att_c, att_c32 = _sb_fwd(f"l{l}_sb_fwd", proj16, **SB_SPEC)
    o_a, = _mm_nn(f"l{l}_mm_oa", att_a, W["w_o_mla"], [F32], tm=1024, tn=D_MODEL)
    o_b, = _mm_nn(f"l{l}_mm_ob", att_b, W["w_o_swa"], [F32], tm=1024, tn=D_MODEL)
    o_c, = _mm_nn(f"l{l}_mm_oc", att_c, W["w_o_sb"], [F32], tm=1024, tn=D_MODEL)

    def gate_mix(gl, b, oa, ob, oc):
        gt = jax.nn.sigmoid(gl + b)
        return (gt[:, :D_MODEL] * oa + gt[:, D_MODEL:2 * D_MODEL] * ob + gt[:, 2 * D_MODEL:] * oc,)

    mixed, = _rowwise(f"l{l}_gate_mix", gate_mix, [_ri(proj, T, 3072, 0), _bi(P["b_gate"]), _ri(o_a, T), _ri(o_b, T), _ri(o_c, T)],
                      [_ro(S, D_MODEL, BF16, T)], n_tiles=nt)
    y, = _mm_nn(f"l{l}_mm_out", mixed, W["w_out"], [F32], tm=1024, tn=D_MODEL)

    def resid_norm(xv, yv, gpost, gpre):
        x1 = xv + _rms(yv, gpost)
        return x1, _rms(x1, gpre)

    x1, h2 = _rowwise(f"l{l}_resid_norm", resid_norm, [_ri(x, T), _ri(y, T), _bi(P["g_mix_post"]), _bi(P["g_mlp_pre"])],
                      [_ro(S, D_MODEL, F32, T), _ro(S, D_MODEL, BF16, T)], n_tiles=nt)
    up, u = _mm_nn(f"l{l}_mm_up", h2, W["w_up"], [F32, BF16], tm=512, tn=2048,
                   epilogue=lambda acc: (acc, jnp.square(jnp.maximum(acc, 0.0))))
    dn, = _mm_nn(f"l{l}_mm_down", u, W["w_down"], [F32], tm=512, tn=D_MODEL)
    x2, = _rowwise(f"l{l}_resid_out", lambda xv, dv, g: (xv + _rms(dv, g),), [_ri(x1, T), _ri(dn, T), _bi(P["g_mlp_post"])],
                   [_ro(S, D_MODEL, F32, T)], n_tiles=nt)
    saved = dict(x=x, h=h, proj=proj, proj16=proj16, cqn=cqn, ckvn=ckvn, q_swa=q_swa, k_swa=k_swa, v_swa=v_swa, q_mla=q_mla, k_mla=k_mla,
                 v_mla=v_mla, att_a=att_a, lse_a=lse_a, att_b=att_b, lse_b=lse_b, att_c=att_c, att_c32=att_c32, o_a=o_a, o_b=o_b, o_c=o_c,
                 mixed=mixed, y=y, x1=x1, h2=h2, up=up, u=u, dn=dn)
    return x2, saved


def _layer_bwd(l, dx2, sv, W, P, tb, T):
    S = dx2.shape[0]
    nt = S // T
    G = {}

    def post_norm_bwd(v, g, dy):
        return _rms_bwd(v, g, dy)

    d_dn, G["g_mlp_post"] = _rowwise(f"l{l}_b_post2", post_norm_bwd, [_ri(sv["dn"], T), _bi(P["g_mlp_post"]), _ri(dx2, T)],
                                    [_ro(S, D_MODEL, BF16, T)], [D_MODEL], n_tiles=nt)
    d_up, = _mm_nt(f"l{l}_b_mm_down", d_dn, W["w_down"], [BF16], tm=512, tn=2048, extras=[sv["up"]],
                   epilogue=lambda acc, upv: (acc * (2.0 * jnp.maximum(upv, 0.0)),))
    G["w_down"] = _mm_tn(f"l{l}_g_down", sv["u"], d_dn, tm=2048, tn=D_MODEL, ts=512)
    d_h2, = _mm_nt(f"l{l}_b_mm_up", d_up, W["w_up"], [F32], tm=512, tn=D_MODEL)
    G["w_up"] = _mm_tn(f"l{l}_g_up", sv["h2"], d_up, tm=D_MODEL, tn=2048, ts=512)

    def pre_norm_bwd(v, g, dy, dres):
        dx, dg = _rms_bwd(v, g, dy)
        return dres + dx, dg

    dx1, G["g_mlp_pre"] = _rowwise(f"l{l}_b_pre2", pre_norm_bwd, [_ri(sv["x1"], T), _bi(P["g_mlp_pre"]), _ri(d_h2, T), _ri(dx2, T)],
                                  [_ro(S, D_MODEL, F32, T)], [D_MODEL], n_tiles=nt)
    d_y, G["g_mix_post"] = _rowwise(f"l{l}_b_post1", post_norm_bwd, [_ri(sv["y"], T), _bi(P["g_mix_post"]), _ri(dx1, T)],
                                   [_ro(S, D_MODEL, BF16, T)], [D_MODEL], n_tiles=nt)
    d_mixed, = _mm_nt(f"l{l}_b_mm_out", d_y, W["w_out"], [F32], tm=1024, tn=D_MODEL)
    G["w_out"] = _mm_tn(f"l{l}_g_out", sv["mixed"], d_y, tm=D_MODEL, tn=D_MODEL, ts=512)

    def gate_bwd(dm, gl, b, oa, ob, oc):
        gt = jax.nn.sigmoid(gl + b)
        outs, dgl = [], []
        for k, o in enumerate((oa, ob, oc)):
            gk = gt[:, k * D_MODEL:(k + 1) * D_MODEL]
            outs.append(dm * gk)
            dgl.append(dm * o * gk * (1.0 - gk))
        dgl = jnp.concatenate(dgl, axis=1)
        return (*outs, dgl, jnp.sum(dgl, axis=0, keepdims=True))

    d_oa, d_ob, d_oc, d_gl, G["b_gate"] = _rowwise(
        f"l{l}_b_gate", gate_bwd, [_ri(d_mixed, T), _ri(sv["proj"], T, 3072, 0), _bi(P["b_gate"]), _ri(sv["o_a"], T), _ri(sv["o_b"], T), _ri(sv["o_c"], T)],
        [_ro(S, D_MODEL, BF16, T)] * 3 + [_ro(S, 3 * D_MODEL, BF16, T)], [3 * D_MODEL], n_tiles=nt)
    d_att = {}
    for br, d_o, att in (("mla", d_oa, sv["att_a"]), ("swa", d_ob, sv["att_b"]), ("sb", d_oc, sv["att_c"])):
        d_att[br], = _mm_nt(f"l{l}_b_mm_o_{br}", d_o, W["w_o_" + br], [F32], tm=1024, tn=512)
        G["w_o_" + br] = _mm_tn(f"l{l}_g_o_{br}", att, d_o, tm=512, tn=D_MODEL, ts=512)

    dq_mla, dk_mla, dv_mla = _softmax_bwd(f"l{l}_mla_bwd", sv["q_mla"], sv["k_mla"], sv["v_mla"], sv["att_a"], d_att["mla"], sv["lse_a"], None,
                                          **MLA_SPEC, **_mla_specs(S))
    dq_swa, dk_swa, dv_swa, dsink = _softmax_bwd(f"l{l}_swa_bwd", sv["q_swa"], sv["k_swa"], sv["v_swa"], sv["att_b"], d_att["swa"], sv["lse_b"],
                                                 P["swa_sinks"], **SWA_SPEC, **_swa_specs(S))
    G["swa_sinks"] = dsink[:, :SWA_G, 0].reshape(1, SWA_HEADS)
    dq_sb, dk_sb, dv_sb = _sb_bwd(f"l{l}_sb_bwd", sv["proj16"], sv["att_c32"], d_att["sb"], **SB_SPEC)

    def mla_prep_bwd(dq, dk, dvv, cq_t, sq_t, ck_t, sk_t):
        dks = dk[:, :LANES]
        for hh in range(1, MLA_HEADS):
            dks = dks + dk[:, hh * LANES:(hh + 1) * LANES]
        d_kr = _rope_t(pltpu.roll(dks, LANES - MLA_NOPE, axis=1), ck_t, sk_t, MLA_ROPE // 2)
        return _rope_t(dq, cq_t, sq_t, MLA_ROPE // 2), jnp.concatenate([dk, dvv], axis=1), d_kr

    d_q_lat, d_kv_lat, d_krope = _rowwise(
        f"l{l}_b_mla_prep", mla_prep_bwd,
        [_ri(dq_mla, T), _ri(dk_mla, T), _ri(dv_mla, T), _ri(tb["c_mla_q"], T), _ri(tb["s_mla_q"], T), _ri(tb["c_mla_k"], T), _ri(tb["s_mla_k"], T)],
        [_ro(S, MLA_QW, BF16, T), _ro(S, MLA_KVW, BF16, T), _ro(S, LANES, BF16, T)], n_tiles=nt)
    d_cqn, = _mm_nt(f"l{l}_b_mm_uq", d_q_lat, W["w_uq"], [F32], tm=1024, tn=MLA_Q_LORA)
    G["w_uq"] = _mm_tn(f"l{l}_g_uq", sv["cqn"], d_q_lat, tm=MLA_Q_LORA, tn=MLA_QW, ts=512)
    d_ckvn, = _mm_nt(f"l{l}_b_mm_ukv", d_kv_lat, W["w_ukv"], [F32], tm=1024, tn=MLA_KV_LORA)
    G["w_ukv"] = _mm_tn(f"l{l}_g_ukv", sv["ckvn"], d_kv_lat, tm=MLA_KV_LORA, tn=MLA_KVW, ts=512)

    def mix_prep_bwd(cq, ckv, gq, gkv, dcqn, dckvn, dqs, dks, dvs, cq_t, sq_t, ck_t, sk_t):
        d_cq, dgq = _rms_bwd(cq, gq, dcqn)
        d_ckv, dgkv = _rms_bwd(ckv, gkv, dckvn)
        dk2 = jnp.concatenate([dks[0], dks[1]], axis=1)
        dv2 = jnp.concatenate([dvs[0], dvs[1]], axis=1)
        return (d_cq, d_ckv, _rope_t(dqs, cq_t, sq_t, SWA_HEAD_DIM // 2), _rope_t(dk2, ck_t, sk_t, SWA_HEAD_DIM // 2), dv2, dgq, dgkv)

    kv3 = lambda a: (a.reshape(SWA_KV_HEADS, S, SWA_HEAD_DIM), (SWA_KV_HEADS, T, SWA_HEAD_DIM), lambda i: (0, i, 0))
    d_cq, d_ckv, d_qswa, d_kswa, d_vswa, G["g_q_lat"], G["g_kv_lat"] = _rowwise(
        f"l{l}_b_mix_prep", mix_prep_bwd,
        [_ri(sv["proj"], T, 256, _cb("c_q")), _ri(sv["proj"], T, 128, _cb("c_kv")), _bi(P["g_q_lat"]), _bi(P["g_kv_lat"]), _ri(d_cqn, T), _ri(d_ckvn, T),
         _ri(dq_swa, T), kv3(dk_swa), kv3(dv_swa), _ri(tb["c_swa_q"], T), _ri(tb["s_swa_q"], T), _ri(tb["c_swa_k"], T), _ri(tb["s_swa_k"], T)],
        [_ro(S, 256, BF16, T), _ro(S, 128, BF16, T), _ro(S, 512, BF16, T), _ro(S, 128, BF16, T), _ro(S, 128, BF16, T)], [256, 128], n_tiles=nt)
    pieces = dict(gates=d_gl, q_swa=d_qswa, q_sb=dq_sb, k_sb=dk_sb, v_sb=dv_sb, c_q=d_cq, c_kv=d_ckv, k_swa=d_kswa, v_swa=d_vswa, k_rope=d_krope)
    d_proj = jnp.concatenate([pieces[n].astype(BF16) for n in PERM_ORDER], axis=1)
    d_h, = _mm_nt(f"l{l}_b_mm_in", d_proj, W["w_in"], [F32], tm=512, tn=512)
    G["w_in"] = _mm_tn(f"l{l}_g_in", sv["h"], d_proj, tm=D_MODEL, tn=IN_WIDTH_P // 2, ts=512)
    dx, G["g_mix_pre"] = _rowwise(f"l{l}_b_pre1", pre_norm_bwd, [_ri(sv["x"], T), _bi(P["g_mix_pre"]), _ri(d_h, T), _ri(dx1, T)],
                                 [_ro(S, D_MODEL, F32, T)], [D_MODEL], n_tiles=nt)
    return dx, G


def _pack_rows(vecs, rows):
    flat = jnp.concatenate([v.reshape(-1) for v in vecs])
    return jnp.pad(flat, (0, rows * PACK_COLS - flat.shape[0])).reshape(rows, PACK_COLS)


def kernel(x, positions, g_mix_pre, w_in, b_gate, g_q_lat, g_kv_lat, w_uq, w_ukv, swa_sinks, w_o_mla, w_o_swa, w_o_sb, w_out, g_mix_post, g_mlp_pre, w_up, w_down, g_mlp_post, loss_target, m_g_mix_pre, m_w_in, m_b_gate, m_g_q_lat, m_g_kv_lat, m_w_uq, m_w_ukv, m_swa_sinks, m_w_o_mla, m_w_o_swa, m_w_o_sb, m_w_out, m_g_mix_post, m_g_mlp_pre, m_w_up, m_w_down, m_g_mlp_post, v_g_mix_pre, v_w_in, v_b_gate, v_g_q_lat, v_g_kv_lat, v_w_uq, v_w_ukv, v_swa_sinks, v_w_o_mla, v_w_o_swa, v_w_o_sb, v_w_out, v_g_mix_post, v_g_mlp_pre, v_w_up, v_w_down, v_g_mlp_post):
    a = dict(locals())
    S = x.shape[1]
    depth = w_in.shape[0]
    T = min(256, S)
    xs = x.reshape(S, D_MODEL)
    tb = _tables(positions)

    gathered = _exchange("gather_weights", [a[n].astype(BF16).reshape(-1, a[n].shape[2]) for n in BIG], per_peer=False)
    full = {n: _from_shards(n, g, a[n].shape) for n, g in zip(BIG, gathered)}
    layers = []
    for l in range(depth):
        W = {n: full[n][l] for n in BIG}
        W["w_in"] = _perm_w_in(W["w_in"])
        W["w_uq"] = _perm_w_uq(W["w_uq"])
        W["w_ukv"] = _perm_w_ukv(W["w_ukv"])
        P = {n: a[n][l].reshape(1, -1) for n in SMALL if n != "swa_sinks"}
        P["swa_sinks"] = a["swa_sinks"][l]
        layers.append((W, P))

    saved = []
    h = xs
    for l, (W, P) in enumerate(layers):
        h, sv = _layer_fwd(l, h, W, P, tb, T)
        saved.append(sv)

    def loss_head(yv, tv):
        err = yv - tv
        part = 0.5 * jnp.sum(jnp.mean(err * err, axis=1, keepdims=True), axis=0, keepdims=True)
        return err * (1.0 / D_MODEL), jnp.broadcast_to(part, (1, LANES))

    dh, loss_row = _rowwise("loss_head", loss_head, [_ri(h, T), _ri(loss_target.reshape(S, D_MODEL), T)], [_ro(S, D_MODEL, F32, T)], [LANES],
                            n_tiles=S // T)
    loss = lax.psum(loss_row[0, 0], ("x", "y", "c"))

    grads = [None] * depth
    for l in reversed(range(depth)):
        W, P = layers[l]
        dh, grads[l] = _layer_bwd(l, dh, saved[l], W, P, tb, T)
    grad_x = dh.reshape(x.shape)

    unperm = dict(w_in=_unperm_w_in, w_uq=_unperm_w_uq, w_ukv=_unperm_w_ukv)
    send = [_to_shards(n, jnp.stack([unperm.get(n, lambda t: t)(grads[l][n]).astype(BF16) for l in range(depth)])) for n in BIG]
    recv = _exchange("scatter_grads", send, per_peer=True)
    out = {}
    for n, parts in zip(BIG, recv):
        shp = a[n].shape
        rows, cols = shp[0] * shp[1], shp[2]
        res = _adamw("adamw_" + n, parts, a[n].reshape(rows, cols), a["m_" + n].reshape(rows, cols), a["v_" + n].reshape(rows, cols),
                     tile=min(256, rows))
        out[n] = [r.reshape(shp) for r in res]

    small_total = sum(a[n].size for n in SMALL)
    small_rows = -(-small_total // (8 * PACK_COLS)) * 8
    sg = _pack_rows([jnp.stack([grads[l][n].reshape(-1) for l in range(depth)]) for n in SMALL], small_rows)
    sg_all, = _exchange("gather_small_grads", [sg], per_peer=False)
    res = _adamw("adamw_small", sg_all, _pack_rows([a[n] for n in SMALL], small_rows), _pack_rows([a["m_" + n] for n in SMALL], small_rows),
                 _pack_rows([a["v_" + n] for n in SMALL], small_rows), tile=small_rows)
    off = 0
    for n in SMALL:
        cnt = a[n].size
        out[n] = [r.reshape(-1)[off:off + cnt].reshape(a[n].shape) for r in res]
        off += cnt

    order = ("g_mix_pre", "w_in", "b_gate", "g_q_lat", "g_kv_lat", "w_uq", "w_ukv", "swa_sinks", "w_o_mla", "w_o_swa", "w_o_sb", "w_out",
             "g_mix_post", "g_mlp_pre", "w_up", "w_down", "g_mlp_post")
    return (loss, grad_x, *[out[n][0] for n in order], *[out[n][1] for n in order], *[out[n][2] for n in order], *[out[n][3] for n in order])
```

```python
import functools

import jax
import jax.numpy as jnp
from jax import lax
from jax.experimental import pallas as pl
from jax.experimental.pallas import tpu as pltpu

F32, BF16 = jnp.float32, jnp.bfloat16

D_MODEL = 1024
DEPTH = 4
MLA_HEADS, MLA_Q_LORA, MLA_KV_LORA, MLA_NOPE, MLA_ROPE, MLA_V = 8, 256, 128, 64, 32, 64
SWA_HEADS, SWA_KV_HEADS, SWA_HEAD_DIM, SWA_WINDOW = 8, 2, 64, 128
SB_HEADS, SB_HEAD_DIM = 8, 64
D_FF = 4 * D_MODEL
ROPE_THETA = 10000.0
EPS = 1e-6
N_DEV = 8
ADAM_LR, ADAM_B1, ADAM_B2, ADAM_EPS, ADAM_WD, ADAM_STEP = 0.001, 0.9, 0.999, 1e-08, 0.01, 10

LANES = 128
VMEM_LIMIT_MAX = 60 * 1024 * 1024
VMEM_LIMIT_MIN = 32 * 1024 * 1024

ORIG_COLS = dict(c_q=(0, 256), c_kv=(256, 128), k_rope=(384, 32), q_swa=(416, 512), k_swa=(928, 128), v_swa=(1056, 128),
                 q_sb=(1184, 512), k_sb=(1696, 512), v_sb=(2208, 512), gates=(2720, 3072))
IN_WIDTH = 5792
PERM_ORDER = ("gates", "q_swa", "q_sb", "k_sb", "v_sb", "c_q", "c_kv", "k_swa", "v_swa", "k_rope")
PERM_WIDTH = dict(gates=3072, q_swa=512, q_sb=512, k_sb=512, v_sb=512, c_q=256, c_kv=128, k_swa=128, v_swa=128, k_rope=128)
PERM_OFF = {}
_o = 0
for _n in PERM_ORDER:
    PERM_OFF[_n] = _o
    _o += PERM_WIDTH[_n]
IN_WIDTH_P = _o
MLA_QW = MLA_HEADS * LANES
MLA_KVW = MLA_HEADS * LANES + MLA_HEADS * MLA_V

NT = (((1,), (1,)), ((), ()))
TN = (((0,), (0,)), ((), ()))
NEG = -1e30


def _cparams(sem, block_bytes):
    limit = int(min(VMEM_LIMIT_MAX, max(VMEM_LIMIT_MIN, 2 * block_bytes + (16 << 20))))
    return pltpu.CompilerParams(dimension_semantics=sem, vmem_limit_bytes=limit)


def _nbytes(shape, dtype):
    n = 1
    for s in shape:
        n *= s
    return n * jnp.dtype(dtype).itemsize


def _ri(arr, tile, width=None, cb=0):
    width = arr.shape[1] if width is None else width
    return (arr, (tile, width), lambda i, cb=cb: (i, cb))


def _bi(arr):
    return (arr, arr.shape, lambda i: (0, 0))


def _ro(rows, width, dtype, tile):
    return ((rows, width), dtype, (tile, width), lambda i: (i, 0))


def _rowwise(name, fn, ins, outs, reds=(), *, n_tiles):
    n_in, n_out = len(ins), len(outs)

    def body(*refs):
        vals = fn(*[r[...] for r in refs[:n_in]])
        for r, v in zip(refs[n_in:n_in + n_out], vals[:n_out]):
            if isinstance(v, (list, tuple)):
                for j, vj in enumerate(v):
                    r[j] = vj.astype(r.dtype)
            else:
                r[...] = v.astype(r.dtype)
        if reds:
            @pl.when(pl.program_id(0) == 0)
            def _():
                for r in refs[n_in + n_out:]:
                    r[...] = jnp.zeros_like(r)
            for r, v in zip(refs[n_in + n_out:], vals[n_out:]):
                r[...] += v

    block_bytes = sum(_nbytes(b, a.dtype) for a, b, _ in ins) + sum(_nbytes(b, d) for _, d, b, _ in outs)
    res = pl.pallas_call(
        body, name=name, grid=(n_tiles,),
        in_specs=[pl.BlockSpec(b, m) for _, b, m in ins],
        out_specs=[pl.BlockSpec(b, m) for _, _, b, m in outs] + [pl.BlockSpec((1, w), lambda i: (0, 0)) for w in reds],
        out_shape=[jax.ShapeDtypeStruct(s, d) for s, d, _, _ in outs] + [jax.ShapeDtypeStruct((1, w), F32) for w in reds],
        compiler_params=_cparams(("arbitrary",) if reds else ("parallel",), block_bytes),
    )(*[a for a, _, _ in ins])
    return res


def _rms(x, g):
    r = lax.rsqrt(jnp.mean(x * x, axis=1, keepdims=True) + EPS)
    return x * r * g


def _rms_bwd(x, g, dy):
    r = lax.rsqrt(jnp.mean(x * x, axis=1, keepdims=True) + EPS)
    xn = x * r
    dxn = dy * g
    dx = r * (dxn - xn * jnp.mean(dxn * xn, axis=1, keepdims=True))
    return dx, jnp.sum(dy * xn, axis=0, keepdims=True)


def _swap_halves(x, half):
    n = x.shape[1]
    lane = lax.broadcasted_iota(jnp.int32, x.shape, 1)
    first = (lane % (2 * half)) < half
    return jnp.where(first, pltpu.roll(x, n - half, axis=1), pltpu.roll(x, half, axis=1))


def _rope(x, c, sg, half):
    return x * c + _swap_halves(x, half) * sg


def _rope_t(dy, c, sg, half):
    return dy * c - _swap_halves(dy, half) * sg


def _mm_nn(name, a, b, outs, *, tm, tn, extras=(), epilogue=None):
    M, K = a.shape
    N = b.shape[1]
    tm = min(tm, M)
    n_e = len(extras)

    def body(*refs):
        a_ref, b_ref = refs[:2]
        acc = jnp.dot(a_ref[...].astype(BF16), b_ref[...].astype(BF16), preferred_element_type=F32)
        vals = (acc,) * len(outs) if epilogue is None else epilogue(acc, *[r[...] for r in refs[2:2 + n_e]])
        for r, v in zip(refs[2 + n_e:], vals):
            r[...] = v.astype(r.dtype)

    block_bytes = (_nbytes((tm, K), a.dtype) + _nbytes((K, tn), b.dtype) + sum(_nbytes((tm, tn), e.dtype) for e in extras)
                   + sum(_nbytes((tm, tn), d) for d in outs) + _nbytes((tm, tn), F32))
    return pl.pallas_call(
        body, name=name, grid=(N // tn, M // tm),
        in_specs=[pl.BlockSpec((tm, K), lambda j, i: (i, 0)), pl.BlockSpec((K, tn), lambda j, i: (0, j))]
        + [pl.BlockSpec((tm, tn), lambda j, i: (i, j)) for _ in extras],
        out_specs=[pl.BlockSpec((tm, tn), lambda j, i: (i, j)) for _ in outs],
        out_shape=[jax.ShapeDtypeStruct((M, N), d) for d in outs],
        compiler_params=_cparams(("parallel", "parallel"), block_bytes),
    )(a, b, *extras)


def _mm_nt(name, a, b, outs, *, tm, tn, extras=(), epilogue=None):
    M, N = a.shape
    K = b.shape[0]
    tm = min(tm, M)
    n_e = len(extras)

    def body(*refs):
        a_ref, b_ref = refs[:2]
        acc = lax.dot_general(a_ref[...].astype(BF16), b_ref[...].astype(BF16), NT, preferred_element_type=F32)
        vals = (acc,) * len(outs) if epilogue is None else epilogue(acc, *[r[...] for r in refs[2:2 + n_e]])
        for r, v in zip(refs[2 + n_e:], vals):
            r[...] = v.astype(r.dtype)

    block_bytes = (_nbytes((tm, N), a.dtype) + _nbytes((tn, N), b.dtype) + sum(_nbytes((tm, tn), e.dtype) for e in extras)
                   + sum(_nbytes((tm, tn), d) for d in outs) + _nbytes((tm, tn), F32))
    return pl.pallas_call(
        body, name=name, grid=(K // tn, M // tm),
        in_specs=[pl.BlockSpec((tm, N), lambda j, i: (i, 0)), pl.BlockSpec((tn, N), lambda j, i: (j, 0))]
        + [pl.BlockSpec((tm, tn), lambda j, i: (i, j)) for _ in extras],
        out_specs=[pl.BlockSpec((tm, tn), lambda j, i: (i, j)) for _ in outs],
        out_shape=[jax.ShapeDtypeStruct((M, K), d) for d in outs],
        compiler_params=_cparams(("parallel", "parallel"), block_bytes),
    )(a, b, *extras)


def _mm_tn(name, a, b, *, tm, tn, ts):
    S, K = a.shape
    N = b.shape[1]
    ts = min(ts, S)

    def body(a_ref, b_ref, o_ref):
        @pl.when(pl.program_id(2) == 0)
        def _():
            o_ref[...] = jnp.zeros_like(o_ref)
        o_ref[...] += lax.dot_general(a_ref[...].astype(BF16), b_ref[...].astype(BF16), TN, preferred_element_type=F32)

    block_bytes = _nbytes((ts, tm), a.dtype) + _nbytes((ts, tn), b.dtype) + 2 * _nbytes((tm, tn), F32)
    return pl.pallas_call(
        body, name=name, grid=(K // tm, N // tn, S // ts),
        in_specs=[pl.BlockSpec((ts, tm), lambda i, j, s: (s, i)), pl.BlockSpec((ts, tn), lambda i, j, s: (s, j))],
        out_specs=pl.BlockSpec((tm, tn), lambda i, j, s: (i, j)),
        out_shape=jax.ShapeDtypeStruct((K, N), F32),
        compiler_params=_cparams(("parallel", "parallel", "arbitrary"), block_bytes),
    )(a, b)


def _lane_pack(cols, rows):
    lane = lax.broadcasted_iota(jnp.int32, (rows, LANES), 1)
    val = jnp.zeros((rows, LANES), F32)
    for h, c in enumerate(cols):
        val = jnp.where(lane == h, c, val)
    return val


def _mask(kb, row, tk, window):
    col = kb * tk + lax.broadcasted_iota(jnp.int32, (1, tk), 1)
    ok = col <= row
    if window:
        ok = ok & ((row - col) < window)
    return ok


def _kb_range(i, tq, tk, window):
    end = ((i + 1) * tq + tk - 1) // tk
    if window:
        lo = jnp.maximum(i * tq - (window - 1), 0) // tk
        return lo, lo, end
    return 0, (i * tq) // tk, end


def _softmax_fwd(name, q, k, v, sinks, *, groups, heads, q_spec, k_spec, v_spec, o_width, tq, tk, scale, window, prescale):
    S = q.shape[0]
    nq = S // tq
    nh = len(heads)
    has_sink = sinks is not None
    dv = heads[0][2].stop - heads[0][2].start

    def body(*refs):
        if has_sink:
            sink_ref, q_ref, k_ref, v_ref, o_ref, lse_ref = refs
        else:
            q_ref, k_ref, v_ref, o_ref, lse_ref = refs
        g, i = pl.program_id(0), pl.program_id(1)
        q_all = q_ref[...]
        row = i * tq + lax.broadcasted_iota(jnp.int32, (tq, 1), 0)
        lo, mid, hi = _kb_range(i, tq, tk, window)
        qhs = [q_all[:, qs] * scale if prescale else q_all[:, qs] for qs, _, _ in heads]

        def blk(kb, carry, masked):
            r0 = pl.multiple_of(kb * tk, tk)
            k_all = k_ref[pl.ds(r0, tk), :]
            v_all = v_ref[pl.ds(r0, tk), :]
            if masked:
                ok = _mask(kb, row, tk, window)
            new = []
            for qh, (_, ks, vs), (m, l, acc) in zip(qhs, heads, carry):
                s = lax.dot_general(qh, k_all[:, ks], NT, preferred_element_type=F32)
                if not prescale:
                    s = s * scale
                if masked:
                    s = jnp.where(ok, s, NEG)
                m_new = jnp.maximum(m, jnp.max(s, axis=1, keepdims=True))
                alpha = jnp.exp(m - m_new)
                p = jnp.exp(s - m_new)
                if masked:
                    p = jnp.where(ok, p, 0.0)
                l = alpha * l + jnp.sum(p, axis=1, keepdims=True)
                acc = alpha * acc + jnp.dot(p.astype(BF16), v_all[:, vs], preferred_element_type=F32)
                new.append((m_new, l, acc))
            return tuple(new)

        carry = []
        for h in range(nh):
            if has_sink:
                m0 = jnp.full((tq, 1), sink_ref[g * nh + h], F32)
                l0 = jnp.ones((tq, 1), F32)
            else:
                m0 = jnp.full((tq, 1), NEG, F32)
                l0 = jnp.zeros((tq, 1), F32)
            carry.append((m0, l0, jnp.zeros((tq, dv), F32)))
        carry = lax.fori_loop(lo, mid, functools.partial(blk, masked=False), tuple(carry))
        carry = lax.fori_loop(mid, hi, functools.partial(blk, masked=True), carry)
        o_ref[...] = jnp.concatenate([acc / l for _, l, acc in carry], axis=1).astype(o_ref.dtype)
        lse_ref[0] = _lane_pack([m + jnp.log(l) for m, l, _ in carry], tq)

    in_specs = [pl.BlockSpec(*q_spec), pl.BlockSpec(*k_spec), pl.BlockSpec(*v_spec)]
    args = [q, k, v]
    if has_sink:
        in_specs = [pl.BlockSpec(memory_space=pltpu.SMEM)] + in_specs
        args = [sinks] + args
    wo = nh * dv
    block_bytes = _nbytes(q_spec[0], q.dtype) + _nbytes(k_spec[0], k.dtype) + _nbytes(v_spec[0], v.dtype) + 4 * tq * (wo + LANES)
    return pl.pallas_call(
        body, name=name, grid=(groups, nq), in_specs=in_specs,
        out_specs=[pl.BlockSpec((tq, wo), lambda g, i: (i, g)), pl.BlockSpec((1, tq, LANES), lambda g, i: (g, i, 0))],
        out_shape=[jax.ShapeDtypeStruct((S, o_width), BF16), jax.ShapeDtypeStruct((groups, S, LANES), F32)],
        compiler_params=_cparams(("parallel", "arbitrary"), block_bytes),
    )(*args)


def _softmax_bwd(name, q, k, v, o, do, lse, sinks, *, groups, heads, q_spec, k_spec, v_spec, tq, tk, scale, window, prescale):
    S = q.shape[0]
    nq = S // tq
    nh = len(heads)
    has_sink = sinks is not None
    dv = heads[0][2].stop - heads[0][2].start
    wo = nh * dv

    def body(*refs):
        if has_sink:
            sink_ref, q_ref, k_ref, v_ref, o_ref, do_ref, lse_ref, dq_ref, dk_ref, dv_ref, dsink_ref = refs
        else:
            q_ref, k_ref, v_ref, o_ref, do_ref, lse_ref, dq_ref, dk_ref, dv_ref = refs
        g, i = pl.program_id(0), pl.program_id(1)

        @pl.when(i == 0)
        def _():
            dk_ref[...] = jnp.zeros_like(dk_ref)
            dv_ref[...] = jnp.zeros_like(dv_ref)
            if has_sink:
                dsink_ref[...] = jnp.zeros_like(dsink_ref)

        q_all = q_ref[...]
        o_all = o_ref[...].astype(F32)
        do_all = do_ref[...].astype(F32)
        lse_all = lse_ref[0]
        row = i * tq + lax.broadcasted_iota(jnp.int32, (tq, 1), 0)
        lo, mid, hi = _kb_range(i, tq, tk, window)
        per_head = []
        for h, (qs, ks, vs) in enumerate(heads):
            osl = slice(h * dv, (h + 1) * dv)
            doh = do_all[:, osl]
            delta = jnp.sum(doh * o_all[:, osl], axis=1, keepdims=True)
            qh = q_all[:, qs] * scale if prescale else q_all[:, qs]
            per_head.append((qh, doh.astype(BF16), delta, lse_all[:, h:h + 1], ks, vs))

        def blk(kb, dqs, masked):
            r0 = pl.multiple_of(kb * tk, tk)
            k_all = k_ref[pl.ds(r0, tk), :]
            v_all = v_ref[pl.ds(r0, tk), :]
            if masked:
                ok = _mask(kb, row, tk, window)
            dk_parts, dv_parts, new_dqs = {}, {}, []
            for (qh, doh, delta, lse_h, ks, vs), dq in zip(per_head, dqs):
                kk, vv = k_all[:, ks], v_all[:, vs]
                s = lax.dot_general(qh, kk, NT, preferred_element_type=F32)
                if not prescale:
                    s = s * scale
                p = jnp.exp(s - lse_h)
                if masked:
                    p = jnp.where(ok, p, 0.0)
                dp = lax.dot_general(doh, vv, NT, preferred_element_type=F32)
                ds = p * (dp - delta)
                if not prescale:
                    ds = ds * scale
                new_dqs.append(dq + jnp.dot(ds.astype(BF16), kk, preferred_element_type=F32))
                dk_c = jnp.dot(ds.T.astype(BF16), qh, preferred_element_type=F32)
                dv_c = jnp.dot(p.T.astype(BF16), doh, preferred_element_type=F32)
                dk_parts[ks.start] = dk_parts[ks.start] + dk_c if ks.start in dk_parts else dk_c
                dv_parts[vs.start] = dv_parts[vs.start] + dv_c if vs.start in dv_parts else dv_c
            dk_ref[pl.ds(r0, tk), :] += jnp.concatenate([dk_parts[s0] for s0 in sorted(dk_parts)], axis=1)
            dv_ref[pl.ds(r0, tk), :] += jnp.concatenate([dv_parts[s0] for s0 in sorted(dv_parts)], axis=1)
            return tuple(new_dqs)

        dqs = tuple(jnp.zeros((tq, qs.stop - qs.start), F32) for qs, _, _ in heads)
        dqs = lax.fori_loop(lo, mid, functools.partial(blk, masked=False), dqs)
        dqs = lax.fori_loop(mid, hi, functools.partial(blk, masked=True), dqs)
        dq_ref[...] = jnp.concatenate([dq * scale if prescale else dq for dq in dqs], axis=1)
        if has_sink:
            for h, (_, _, delta, lse_h, _, _) in enumerate(per_head):
                p_sink = jnp.exp(sink_ref[g * nh + h] - lse_h)
                dsink_ref[0, h:h + 1, :] += jnp.broadcast_to(-jnp.sum(p_sink * delta, axis=0, keepdims=True), (1, LANES))

    in_specs = [pl.BlockSpec(*q_spec), pl.BlockSpec(*k_spec), pl.BlockSpec(*v_spec),
                pl.BlockSpec((tq, wo), lambda g, i: (i, g)), pl.BlockSpec((tq, wo), lambda g, i: (i, g)),
                pl.BlockSpec((1, tq, LANES), lambda g, i: (g, i, 0))]
    args = [q, k, v, o, do, lse]
    out_specs = [pl.BlockSpec(*q_spec), pl.BlockSpec(*k_spec), pl.BlockSpec(*v_spec)]
    out_shape = [jax.ShapeDtypeStruct(q.shape, F32), jax.ShapeDtypeStruct(k.shape, F32), jax.ShapeDtypeStruct(v.shape, F32)]
    if has_sink:
        in_specs = [pl.BlockSpec(memory_space=pltpu.SMEM)] + in_specs
        args = [sinks] + args
        out_specs.append(pl.BlockSpec((1, 8, LANES), lambda g, i: (g, 0, 0)))
        out_shape.append(jax.ShapeDtypeStruct((groups, 8, LANES), F32))
    block_bytes = (_nbytes(q_spec[0], q.dtype) + _nbytes(k_spec[0], k.dtype) + _nbytes(v_spec[0], v.dtype) + 6 * tq * wo + 4 * tq * LANES
                   + _nbytes(q_spec[0], F32) + _nbytes(k_spec[0], F32) + _nbytes(v_spec[0], F32))
    return pl.pallas_call(
        body, name=name, grid=(groups, nq), in_specs=in_specs, out_specs=out_specs, out_shape=out_shape,
        compiler_params=_cparams(("parallel", "arbitrary"), block_bytes),
    )(*args)


def _split_dot(x, u2):
    hi = x.astype(BF16)
    lo = (x - hi.astype(F32)).astype(BF16)
    return jnp.dot(jnp.concatenate([hi, lo], axis=1), u2, preferred_element_type=F32)


def _suffix_ones():
    r = lax.broadcasted_iota(jnp.int32, (2 * LANES, 2 * LANES), 0) % LANES
    c = lax.broadcasted_iota(jnp.int32, (2 * LANES, 2 * LANES), 1)
    return ((r > c) | (c >= LANES)).astype(BF16)


def _suffix_scan(x, uo, run):
    nc = x.shape[1] // LANES
    out = [None] * nc
    for c in reversed(range(nc)):
        st = _split_dot(x[:, c * LANES:(c + 1) * LANES], uo)
        out[c] = st[:, :LANES] + run
        run = run + st[:, LANES:]
    return (out[0] if nc == 1 else jnp.concatenate(out, axis=1)), run


def _sb_logits(qh, kk, kb, row, tq, masked):
    z = lax.dot_general(qh, kk, NT, preferred_element_type=F32)
    nz = -z
    l = jnp.minimum(nz, 0.0) - jnp.log(1.0 + jnp.exp(jnp.minimum(z, nz)))
    ok = None
    if masked:
        col = kb * tq + lax.broadcasted_iota(jnp.int32, (1, tq), 1)
        ok = col < row
        l = jnp.where(ok, l, 0.0)
    return z, l, ok


def _sb_fwd(name, pb, *, q_cb, k_cb, v_cb, tq, scale):
    S = pb.shape[0]
    nq = S // tq
    hd = SB_HEAD_DIM
    groups = SB_HEADS * hd // LANES
    nh = LANES // hd

    def body(q_ref, k_ref, v_ref, o16_ref, o32_ref):
        i = pl.program_id(1)
        q_all = q_ref[...]
        row = i * tq + lax.broadcasted_iota(jnp.int32, (tq, 1), 0)
        uo = _suffix_ones()
        sls = [slice(h * hd, (h + 1) * hd) for h in range(nh)]
        qhs = [q_all[:, sl] * scale for sl in sls]

        def blk(t, carry, masked, base):
            kb = base - t
            r0 = pl.multiple_of(kb * tq, tq)
            k_all = k_ref[pl.ds(r0, tq), :]
            v_all = v_ref[pl.ds(r0, tq), :]
            new = []
            for qh, sl, (run_l, acc) in zip(qhs, sls, carry):
                z, l, ok = _sb_logits(qh, k_all[:, sl], kb, row, tq, masked)
                tail, run_l = _suffix_scan(l, uo, run_l)
                e = z + l + tail
                if masked:
                    e = jnp.where(ok, e, NEG)
                new.append((run_l, acc + jnp.dot(jnp.exp(e).astype(BF16), v_all[:, sl], preferred_element_type=F32)))
            return tuple(new)

        carry = tuple((jnp.zeros((tq, LANES), F32), jnp.zeros((tq, hd), F32)) for _ in range(nh))
        carry = blk(0, carry, True, i)
        carry = lax.fori_loop(0, i, functools.partial(blk, masked=False, base=i - 1), carry)
        o = jnp.concatenate([acc for _, acc in carry], axis=1)
        o16_ref[...] = o.astype(BF16)
        o32_ref[...] = o

    block_bytes = 2 * tq * LANES + 2 * 2 * S * LANES + 6 * tq * LANES
    return pl.pallas_call(
        body, name=name, grid=(groups, nq),
        in_specs=[pl.BlockSpec((tq, LANES), lambda g, i: (i, q_cb + g)), pl.BlockSpec((S, LANES), lambda g, i: (0, k_cb + g)),
                  pl.BlockSpec((S, LANES), lambda g, i: (0, v_cb + g))],
        out_specs=[pl.BlockSpec((tq, LANES), lambda g, i: (i, g)), pl.BlockSpec((tq, LANES), lambda g, i: (i, g))],
        out_shape=[jax.ShapeDtypeStruct((S, groups * LANES), BF16), jax.ShapeDtypeStruct((S, groups * LANES), F32)],
        compiler_params=_cparams(("parallel", "arbitrary"), block_bytes),
    )(pb, pb, pb)


def _sb_bwd(name, pb, o32, do, *, q_cb, k_cb, v_cb, tq, scale):
    S = pb.shape[0]
    nq = S // tq
    hd = SB_HEAD_DIM
    groups = SB_HEADS * hd // LANES
    nh = LANES // hd

    def body(q_ref, k_ref, v_ref, o_ref, do_ref, dq_ref, dk_ref, dv_ref):
        i = pl.program_id(1)

        @pl.when(i == 0)
        def _():
            dk_ref[...] = jnp.zeros_like(dk_ref)
            dv_ref[...] = jnp.zeros_like(dv_ref)

        q_all = q_ref[...]
        o_all = o_ref[...]
        do_all = do_ref[...].astype(F32)
        row = i * tq + lax.broadcasted_iota(jnp.int32, (tq, 1), 0)
        uo = _suffix_ones()
        per_head = []
        for h in range(nh):
            sl = slice(h * hd, (h + 1) * hd)
            doh = do_all[:, sl].astype(BF16)
            total = jnp.sum(doh.astype(F32) * o_all[:, sl], axis=1, keepdims=True)
            per_head.append((q_all[:, sl] * scale, doh, jnp.broadcast_to(total, (tq, LANES)), sl))

        def blk(t, carry, masked, base):
            kb = base - t
            r0 = pl.multiple_of(kb * tq, tq)
            k_all = k_ref[pl.ds(r0, tq), :]
            v_all = v_ref[pl.ds(r0, tq), :]
            new, dk_c, dv_c = [], [], []
            for (qh, doh, total, sl), (run_l, run_g, dq) in zip(per_head, carry):
                kk, vv = k_all[:, sl], v_all[:, sl]
                z, l, ok = _sb_logits(qh, kk, kb, row, tq, masked)
                tail, run_l = _suffix_scan(l, uo, run_l)
                e = z + l
                beta = jnp.exp(e)
                e = e + tail
                if masked:
                    e = jnp.where(ok, e, NEG)
                a = jnp.exp(e).astype(BF16).astype(F32)
                gr = lax.dot_general(doh, vv, NT, preferred_element_type=F32) * a
                right, run_g = _suffix_scan(gr, uo, run_g)
                nc = tq // LANES
                prefix = (total if nc == 1 else jnp.tile(total, (1, nc))) - right
                dz = gr - beta * prefix
                if masked:
                    dz = jnp.where(ok, dz, 0.0)
                new.append((run_l, run_g, dq + jnp.dot(dz.astype(BF16), kk, preferred_element_type=F32)))
                dk_c.append(jnp.dot(dz.T.astype(BF16), qh, preferred_element_type=F32))
                dv_c.append(jnp.dot(a.T.astype(BF16), doh, preferred_element_type=F32))
            dk_ref[pl.ds(r0, tq), :] += jnp.concatenate(dk_c, axis=1)
            dv_ref[pl.ds(r0, tq), :] += jnp.concatenate(dv_c, axis=1)
            return tuple(new)

        zc = jnp.zeros((tq, LANES), F32)
        carry = tuple((zc, zc, jnp.zeros((tq, hd), F32)) for _ in range(nh))
        carry = blk(0, carry, True, i)
        carry = lax.fori_loop(0, i, functools.partial(blk, masked=False, base=i - 1), carry)
        dq_ref[...] = jnp.concatenate([c[2] * scale for c in carry], axis=1)

    W = groups * LANES
    block_bytes = 2 * tq * LANES + 2 * 2 * S * LANES + 3 * 4 * tq * LANES + 2 * 4 * S * LANES
    return pl.pallas_call(
        body, name=name, grid=(groups, nq),
        in_specs=[pl.BlockSpec((tq, LANES), lambda g, i: (i, q_cb + g)), pl.BlockSpec((S, LANES), lambda g, i: (0, k_cb + g)),
                  pl.BlockSpec((S, LANES), lambda g, i: (0, v_cb + g)), pl.BlockSpec((tq, LANES), lambda g, i: (i, g)),
                  pl.BlockSpec((tq, LANES), lambda g, i: (i, g))],
        out_specs=[pl.BlockSpec((tq, LANES), lambda g, i: (i, g)), pl.BlockSpec((S, LANES), lambda g, i: (0, g)),
                   pl.BlockSpec((S, LANES), lambda g, i: (0, g))],
        out_shape=[jax.ShapeDtypeStruct((S, W), F32)] * 3,
        compiler_params=_cparams(("parallel", "arbitrary"), block_bytes),
    )(pb, pb, pb, o32, do)


def _exchange(name, srcs, per_peer):
    n = len(srcs)

    def body(*refs):
        src_refs, out_refs = refs[:n], refs[n:2 * n]
        send_sems, recv_sems, local_sems = refs[2 * n:]
        x, y, c = lax.axis_index("x"), lax.axis_index("y"), lax.axis_index("c")
        me = 4 * x + 2 * y + c

        def copy(j, k):
            px, py, pc = x ^ (k >> 2), y ^ ((k >> 1) & 1), c ^ (k & 1)
            s = src_refs[j].at[4 * px + 2 * py + pc] if per_peer else src_refs[j]
            return pltpu.make_async_remote_copy(
                src_ref=s, dst_ref=out_refs[j].at[me], send_sem=send_sems.at[j, k - 1], recv_sem=recv_sems.at[j, k - 1],
                device_id=(px, py, pc), device_id_type=pl.DeviceIdType.MESH)

        mine = [pltpu.make_async_copy(src_refs[j].at[me] if per_peer else src_refs[j], out_refs[j].at[me], local_sems.at[j]) for j in range(n)]
        copies = [copy(j, k) for k in range(1, N_DEV) for j in range(n)]
        for cp in mine + copies:
            cp.start()
        for cp in copies:
            cp.wait_recv()
        for cp in copies:
            cp.wait_send()
        for cp in mine:
            cp.wait()

    hbm = pl.BlockSpec(memory_space=pltpu.HBM)
    return pl.pallas_call(
        body, name=name, in_specs=[hbm] * n, out_specs=[hbm] * n,
        out_shape=[jax.ShapeDtypeStruct((N_DEV,) + tuple(s.shape[1:] if per_peer else s.shape), s.dtype) for s in srcs],
        scratch_shapes=[pltpu.SemaphoreType.DMA((n, N_DEV - 1)), pltpu.SemaphoreType.DMA((n, N_DEV - 1)), pltpu.SemaphoreType.DMA((n,))],
    )(*srcs)


def _adamw(name, parts, w, m, v, *, tile):
    R, C = w.shape

    def body(p_ref, w_ref, m_ref, v_ref, g_ref, d_ref, nm_ref, nv_ref):
        g = p_ref[0].astype(F32)
        for d in range(1, N_DEV):
            g = g + p_ref[d].astype(F32)
        wv = w_ref[...]
        mm = ADAM_B1 * m_ref[...] + (1.0 - ADAM_B1) * g
        vv = ADAM_B2 * v_ref[...] + (1.0 - ADAM_B2) * jnp.square(g)
        m_hat = mm / (1.0 - ADAM_B1 ** ADAM_STEP)
        v_hat = vv / (1.0 - ADAM_B2 ** ADAM_STEP)
        g_ref[...] = g
        d_ref[...] = -ADAM_LR * (m_hat / (jnp.sqrt(v_hat) + ADAM_EPS) + ADAM_WD * wv)
        nm_ref[...] = mm
        nv_ref[...] = vv

    blk = pl.BlockSpec((tile, C), lambda i: (i, 0))
    block_bytes = N_DEV * _nbytes((tile, C), parts.dtype) + 7 * _nbytes((tile, C), F32)
    return pl.pallas_call(
        body, name=name, grid=(R // tile,),
        in_specs=[pl.BlockSpec((N_DEV, tile, C), lambda i: (0, i, 0)), blk, blk, blk],
        out_specs=[blk] * 4, out_shape=[jax.ShapeDtypeStruct((R, C), F32)] * 4,
        compiler_params=_cparams(("parallel",), block_bytes),
    )(parts, w, m, v)


def _perm_w_in(w):
    cols = [w[:, ORIG_COLS[n][0]:ORIG_COLS[n][0] + ORIG_COLS[n][1]] for n in PERM_ORDER]
    cols.append(jnp.zeros((w.shape[0], IN_WIDTH_P - IN_WIDTH), w.dtype))
    return jnp.concatenate(cols, axis=1)


def _unperm_w_in(wp):
    order = sorted(ORIG_COLS, key=lambda n: ORIG_COLS[n][0])
    return jnp.concatenate([wp[:, PERM_OFF[n]:PERM_OFF[n] + ORIG_COLS[n][1]] for n in order], axis=1)


def _perm_w_uq(w):
    w3 = w.reshape(w.shape[0], MLA_HEADS, MLA_NOPE + MLA_ROPE)
    return jnp.pad(w3, ((0, 0), (0, 0), (0, LANES - MLA_NOPE - MLA_ROPE))).reshape(w.shape[0], MLA_QW)


def _unperm_w_uq(wp):
    return wp.reshape(wp.shape[0], MLA_HEADS, LANES)[:, :, :MLA_NOPE + MLA_ROPE].reshape(wp.shape[0], -1)


def _perm_w_ukv(w):
    w3 = w.reshape(w.shape[0], MLA_HEADS, MLA_NOPE + MLA_V)
    kp = jnp.pad(w3[:, :, :MLA_NOPE], ((0, 0), (0, 0), (0, LANES - MLA_NOPE))).reshape(w.shape[0], MLA_HEADS * LANES)
    return jnp.concatenate([kp, w3[:, :, MLA_NOPE:].reshape(w.shape[0], MLA_HEADS * MLA_V)], axis=1)


def _unperm_w_ukv(wp):
    n = wp.shape[0]
    kp = wp[:, :MLA_HEADS * LANES].reshape(n, MLA_HEADS, LANES)[:, :, :MLA_NOPE]
    vp = wp[:, MLA_HEADS * LANES:].reshape(n, MLA_HEADS, MLA_V)
    return jnp.concatenate([kp, vp], axis=2).reshape(n, MLA_HEADS * (MLA_NOPE + MLA_V))


BIG = ("w_in", "w_uq", "w_ukv", "w_o_mla", "w_o_swa", "w_o_sb", "w_out", "w_up", "w_down")
ROW_SHARDED = ("w_out", "w_down")
SMALL = ("g_mix_pre", "b_gate", "g_q_lat", "g_kv_lat", "swa_sinks", "g_mix_post", "g_mlp_pre", "g_mlp_post")
PACK_COLS = 1024


def _to_shards(name, full):
    L, R, C = full.shape
    if name in ROW_SHARDED:
        return full.reshape(L, N_DEV, R // N_DEV, C).transpose(1, 0, 2, 3).reshape(N_DEV, L * R // N_DEV, C)
    return full.reshape(L, R, N_DEV, C // N_DEV).transpose(2, 0, 1, 3).reshape(N_DEV, L * R, C // N_DEV)


def _from_shards(name, gathered, shard_shape):
    L, r, c = shard_shape
    a = gathered.reshape(N_DEV, L, r, c)
    if name in ROW_SHARDED:
        return a.transpose(1, 0, 2, 3).reshape(L, N_DEV * r, c)
    return a.transpose(1, 2, 0, 3).reshape(L, r, N_DEV * c)


def _tables(positions):
    pos = positions.astype(F32).reshape(-1, 1)

    def cs(d):
        inv = 1.0 / (ROPE_THETA ** (jnp.arange(0, d, 2, dtype=F32) / d))
        ang = pos * inv
        c, s = jnp.cos(ang), jnp.sin(ang)
        return jnp.concatenate([c, c], axis=1), jnp.concatenate([-s, s], axis=1)

    c64, s64 = cs(SWA_HEAD_DIM)
    c32, s32 = cs(MLA_ROPE)
    n = pos.shape[0]
    one, zero = jnp.ones((n, MLA_NOPE), F32), jnp.zeros((n, MLA_NOPE), F32)
    pad0 = jnp.zeros((n, LANES - MLA_NOPE - MLA_ROPE), F32)
    cq = jnp.concatenate([one, c32, pad0], axis=1)
    sq = jnp.concatenate([zero, s32, pad0], axis=1)
    padk = jnp.zeros((n, LANES - MLA_ROPE), F32)
    return dict(
        c_swa_q=jnp.tile(c64, (1, SWA_HEADS)), s_swa_q=jnp.tile(s64, (1, SWA_HEADS)),
        c_swa_k=jnp.tile(c64, (1, SWA_KV_HEADS)), s_swa_k=jnp.tile(s64, (1, SWA_KV_HEADS)),
        c_mla_q=jnp.tile(cq, (1, MLA_HEADS)), s_mla_q=jnp.tile(sq, (1, MLA_HEADS)),
        c_mla_k=jnp.concatenate([c32, padk], axis=1), s_mla_k=jnp.concatenate([s32, padk], axis=1))


def _cb(name):
    return PERM_OFF[name] // PERM_WIDTH[name]


MLA_SPEC = dict(groups=MLA_HEADS // 2, tq=512, tk=512, scale=(MLA_NOPE + MLA_ROPE) ** -0.5, window=0, prescale=False,
                heads=[(slice(h * LANES, (h + 1) * LANES), slice(h * LANES, (h + 1) * LANES), slice(h * MLA_V, (h + 1) * MLA_V)) for h in range(2)])
SWA_G = SWA_HEADS // SWA_KV_HEADS
SWA_SPEC = dict(groups=SWA_KV_HEADS, tq=256, tk=128, scale=SWA_HEAD_DIM ** -0.5, window=SWA_WINDOW, prescale=True,
                heads=[(slice(g * SWA_HEAD_DIM, (g + 1) * SWA_HEAD_DIM), slice(0, SWA_HEAD_DIM), slice(0, SWA_HEAD_DIM)) for g in range(SWA_G)])
SB_SPEC = dict(tq=512, scale=SB_HEAD_DIM ** -0.5, q_cb=PERM_OFF["q_sb"] // LANES, k_cb=PERM_OFF["k_sb"] // LANES,
               v_cb=PERM_OFF["v_sb"] // LANES)


def _mla_specs(S):
    tq = MLA_SPEC["tq"]
    return dict(q_spec=((tq, 2 * LANES), lambda g, i: (i, g)), k_spec=((S, 2 * LANES), lambda g, i: (0, g)),
                v_spec=((S, 2 * MLA_V), lambda g, i: (0, g)))


def _swa_specs(S):
    tq = SWA_SPEC["tq"]
    return dict(q_spec=((tq, SWA_G * SWA_HEAD_DIM), lambda g, i: (i, g)), k_spec=((S, SWA_HEAD_DIM), lambda g, i: (g, 0)),
                v_spec=((S, SWA_HEAD_DIM), lambda g, i: (g, 0)))


def _layer_fwd(l, x, W, P, tb, T):
    S = x.shape[0]
    nt = S // T
    h, = _rowwise(f"l{l}_norm_in", lambda xv, g: (_rms(xv, g),), [_ri(x, T), _bi(P["g_mix_pre"])], [_ro(S, D_MODEL, BF16, T)], n_tiles=nt)
    proj, proj16 = _mm_nn(f"l{l}_mm_in", h, W["w_in"], [F32, BF16], tm=512, tn=IN_WIDTH_P // 2)

    def mix_prep(cq, ckv, qs, ks, vs, gq, gkv, cq_t, sq_t, ck_t, sk_t):
        ksr = _rope(ks, ck_t, sk_t, SWA_HEAD_DIM // 2)
        hd = SWA_HEAD_DIM
        return (_rms(cq, gq), _rms(ckv, gkv), _rope(qs, cq_t, sq_t, hd // 2),
                [ksr[:, :hd], ksr[:, hd:]], [vs[:, :hd], vs[:, hd:]])

    kv3 = lambda dt: ((SWA_KV_HEADS, S, SWA_HEAD_DIM), dt, (SWA_KV_HEADS, T, SWA_HEAD_DIM), lambda i: (0, i, 0))
    cqn, ckvn, q_swa, k_swa, v_swa = _rowwise(
        f"l{l}_mix_prep", mix_prep,
        [_ri(proj, T, 256, _cb("c_q")), _ri(proj, T, 128, _cb("c_kv")), _ri(proj, T, 512, _cb("q_swa")), _ri(proj, T, 128, _cb("k_swa")),
         _ri(proj, T, 128, _cb("v_swa")), _bi(P["g_q_lat"]), _bi(P["g_kv_lat"]), _ri(tb["c_swa_q"], T), _ri(tb["s_swa_q"], T),
         _ri(tb["c_swa_k"], T), _ri(tb["s_swa_k"], T)],
        [_ro(S, 256, BF16, T), _ro(S, 128, BF16, T), _ro(S, 512, BF16, T), kv3(BF16), kv3(BF16)], n_tiles=nt)
    k_swa = k_swa.reshape(SWA_KV_HEADS * S, SWA_HEAD_DIM)
    v_swa = v_swa.reshape(SWA_KV_HEADS * S, SWA_HEAD_DIM)
    q_lat, = _mm_nn(f"l{l}_mm_uq", cqn, W["w_uq"], [F32], tm=1024, tn=MLA_QW)
    kv_lat, = _mm_nn(f"l{l}_mm_ukv", ckvn, W["w_ukv"], [F32], tm=1024, tn=MLA_KVW)

    def mla_prep(q, kk, vv, kr, cq_t, sq_t, ck_t, sk_t):
        kpe = pltpu.roll(_rope(kr, ck_t, sk_t, MLA_ROPE // 2), MLA_NOPE, axis=1)
        return _rope(q, cq_t, sq_t, MLA_ROPE // 2), kk + jnp.tile(kpe, (1, MLA_HEADS)), vv

    q_mla, k_mla, v_mla = _rowwise(
        f"l{l}_mla_prep", mla_prep,
        [_ri(q_lat, T), _ri(kv_lat, T, MLA_HEADS * LANES, 0), _ri(kv_lat, T, MLA_HEADS * MLA_V, 2), _ri(proj, T, 128, _cb("k_rope")),
         _ri(tb["c_mla_q"], T), _ri(tb["s_mla_q"], T), _ri(tb["c_mla_k"], T), _ri(tb["s_mla_k"], T)],
        [_ro(S, MLA_QW, BF16, T), _ro(S, MLA_HEADS * LANES, BF16, T), _ro(S, MLA_HEADS * MLA_V, BF16, T)], n_tiles=nt)

    att_a, lse_a = _softmax_fwd(f"l{l}_mla_fwd", q_mla, k_mla, v_mla, None, o_width=MLA_HEADS * MLA_V, **MLA_SPEC, **_mla_specs(S))
    att_b, lse_b = _softmax_fwd(f"l{l}_swa_fwd", q_swa, k_swa, v_swa, P["swa_sinks"], o_width=SWA_HEADS * SWA_HEAD_DIM, **SWA_SPEC, **_swa_specs(S))
    att_c, att_c32 = _sb_fwd(f"l{l}_sb_fwd", proj16, **SB_SPEC)
    o_a, = _mm_nn(f"l{l}_mm_oa", att_a, W["w_o_mla"], [F32], tm=1024, tn=D_MODEL)
    o_b, = _mm_nn(f"l{l}_mm_ob", att_b, W["w_o_swa"], [F32], tm=1024, tn=D_MODEL)
    o_c, = _mm_nn(f"l{l}_mm_oc", att_c, W["w_o_sb"], [F32], tm=1024, tn=D_MODEL)

    def gate_mix(gl, b, oa, ob, oc):
        gt = jax.nn.sigmoid(gl + b)
        return (gt[:, :D_MODEL] * oa + gt[:, D_MODEL:2 * D_MODEL] * ob + gt[:, 2 * D_MODEL:] * oc,)

    mixed, = _rowwise(f"l{l}_gate_mix", gate_mix, [_ri(proj, T, 3072, 0), _bi(P["b_gate"]), _ri(o_a, T), _ri(o_b, T), _ri(o_c, T)],
                      [_ro(S, D_MODEL, BF16, T)], n_tiles=nt)
    y, = _mm_nn(f"l{l}_mm_out", mixed, W["w_out"], [F32], tm=1024, tn=D_MODEL)

    def resid_norm(xv, yv, gpost, gpre):
        x1 = xv + _rms(yv, gpost)
        return x1, _rms(x1, gpre)

    x1, h2 = _rowwise(f"l{l}_resid_norm", resid_norm, [_ri(x, T), _ri(y, T), _bi(P["g_mix_post"]), _bi(P["g_mlp_pre"])],
                      [_ro(S, D_MODEL, F32, T), _ro(S, D_MODEL, BF16, T)], n_tiles=nt)
    up, u = _mm_nn(f"l{l}_mm_up", h2, W["w_up"], [F32, BF16], tm=512, tn=2048,
                   epilogue=lambda acc: (acc, jnp.square(jnp.maximum(acc, 0.0))))
    dn, = _mm_nn(f"l{l}_mm_down", u, W["w_down"], [F32], tm=512, tn=D_MODEL)
    x2, = _rowwise(f"l{l}_resid_out", lambda xv, dv, g: (xv + _rms(dv, g),), [_ri(x1, T), _ri(dn, T), _bi(P["g_mlp_post"])],
                   [_ro(S, D_MODEL, F32, T)], n_tiles=nt)
    saved = dict(x=x, h=h, proj=proj, proj16=proj16, cqn=cqn, ckvn=ckvn, q_swa=q_swa, k_swa=k_swa, v_swa=v_swa, q_mla=q_mla, k_mla=k_mla,
                 v_mla=v_mla, att_a=att_a, lse_a=lse_a, att_b=att_b, lse_b=lse_b, att_c=att_c, att_c32=att_c32, o_a=o_a, o_b=o_b, o_c=o_c,
                 mixed=mixed, y=y, x1=x1, h2=h2, up=up, u=u, dn=dn)
    return x2, saved


def _layer_bwd(l, dx2, sv, W, P, tb, T):
    S = dx2.shape[0]
    nt = S // T
    G = {}

    def post_norm_bwd(v, g, dy):
        return _rms_bwd(v, g, dy)

    d_dn, G["g_mlp_post"] = _rowwise(f"l{l}_b_post2", post_norm_bwd, [_ri(sv["dn"], T), _bi(P["g_mlp_post"]), _ri(dx2, T)],
                                    [_ro(S, D_MODEL, BF16, T)], [D_MODEL], n_tiles=nt)
    d_up, = _mm_nt(f"l{l}_b_mm_down", d_dn, W["w_down"], [BF16], tm=512, tn=2048, extras=[sv["up"]],
                   epilogue=lambda acc, upv: (acc * (2.0 * jnp.maximum(upv, 0.0)),))
    G["w_down"] = _mm_tn(f"l{l}_g_down", sv["u"], d_dn, tm=2048, tn=D_MODEL, ts=512)
    d_h2, = _mm_nt(f"l{l}_b_mm_up", d_up, W["w_up"], [F32], tm=512, tn=D_MODEL)
    G["w_up"] = _mm_tn(f"l{l}_g_up", sv["h2"], d_up, tm=D_MODEL, tn=2048, ts=512)

    def pre_norm_bwd(v, g, dy, dres):
        dx, dg = _rms_bwd(v, g, dy)
        return dres + dx, dg

    dx1, G["g_mlp_pre"] = _rowwise(f"l{l}_b_pre2", pre_norm_bwd, [_ri(sv["x1"], T), _bi(P["g_mlp_pre"]), _ri(d_h2, T), _ri(dx2, T)],
                                  [_ro(S, D_MODEL, F32, T)], [D_MODEL], n_tiles=nt)
    d_y, G["g_mix_post"] = _rowwise(f"l{l}_b_post1", post_norm_bwd, [_ri(sv["y"], T), _bi(P["g_mix_post"]), _ri(dx1, T)],
                                   [_ro(S, D_MODEL, BF16, T)], [D_MODEL], n_tiles=nt)
    d_mixed, = _mm_nt(f"l{l}_b_mm_out", d_y, W["w_out"], [F32], tm=1024, tn=D_MODEL)
    G["w_out"] = _mm_tn(f"l{l}_g_out", sv["mixed"], d_y, tm=D_MODEL, tn=D_MODEL, ts=512)

    def gate_bwd(dm, gl, b, oa, ob, oc):
        gt = jax.nn.sigmoid(gl + b)
        outs, dgl = [], []
        for k, o in enumerate((oa, ob, oc)):
            gk = gt[:, k * D_MODEL:(k + 1) * D_MODEL]
            outs.append(dm * gk)
            dgl.append(dm * o * gk * (1.0 - gk))
        dgl = jnp.concatenate(dgl, axis=1)
        return (*outs, dgl, jnp.sum(dgl, axis=0, keepdims=True))

    d_oa, d_ob, d_oc, d_gl, G["b_gate"] = _rowwise(
        f"l{l}_b_gate", gate_bwd, [_ri(d_mixed, T), _ri(sv["proj"], T, 3072, 0), _bi(P["b_gate"]), _ri(sv["o_a"], T), _ri(sv["o_b"], T), _ri(sv["o_c"], T)],
        [_ro(S, D_MODEL, BF16, T)] * 3 + [_ro(S, 3 * D_MODEL, BF16, T)], [3 * D_MODEL], n_tiles=nt)
    d_att = {}
    for br, d_o, att in (("mla", d_oa, sv["att_a"]), ("swa", d_ob, sv["att_b"]), ("sb", d_oc, sv["att_c"])):
        d_att[br], = _mm_nt(f"l{l}_b_mm_o_{br}", d_o, W["w_o_" + br], [F32], tm=1024, tn=512)
        G["w_o_" + br] = _mm_tn(f"l{l}_g_o_{br}", att, d_o, tm=512, tn=D_MODEL, ts=512)

    dq_mla, dk_mla, dv_mla = _softmax_bwd(f"l{l}_mla_bwd", sv["q_mla"], sv["k_mla"], sv["v_mla"], sv["att_a"], d_att["mla"], sv["lse_a"], None,
                                          **MLA_SPEC, **_mla_specs(S))
    dq_swa, dk_swa, dv_swa, dsink = _softmax_bwd(f"l{l}_swa_bwd", sv["q_swa"], sv["k_swa"], sv["v_swa"], sv["att_b"], d_att["swa"], sv["lse_b"],
                                                 P["swa_sinks"], **SWA_SPEC, **_swa_specs(S))
    G["swa_sinks"] = dsink[:, :SWA_G, 0].reshape(1, SWA_HEADS)
    dq_sb, dk_sb, dv_sb = _sb_bwd(f"l{l}_sb_bwd", sv["proj16"], sv["att_c32"], d_att["sb"], **SB_SPEC)

    def mla_prep_bwd(dq, dk, dvv, cq_t, sq_t, ck_t, sk_t):
        dks = dk[:, :LANES]
        for hh in range(1, MLA_HEADS):
            dks = dks + dk[:, hh * LANES:(hh + 1) * LANES]
        d_kr = _rope_t(pltpu.roll(dks, LANES - MLA_NOPE, axis=1), ck_t, sk_t, MLA_ROPE // 2)
        return _rope_t(dq, cq_t, sq_t, MLA_ROPE // 2), jnp.concatenate([dk, dvv], axis=1), d_kr

    d_q_lat, d_kv_lat, d_krope = _rowwise(
        f"l{l}_b_mla_prep", mla_prep_bwd,
        [_ri(dq_mla, T), _ri(dk_mla, T), _ri(dv_mla, T), _ri(tb["c_mla_q"], T), _ri(tb["s_mla_q"], T), _ri(tb["c_mla_k"], T), _ri(tb["s_mla_k"], T)],
        [_ro(S, MLA_QW, BF16, T), _ro(S, MLA_KVW, BF16, T), _ro(S, LANES, BF16, T)], n_tiles=nt)
    d_cqn, = _mm_nt(f"l{l}_b_mm_uq", d_q_lat, W["w_uq"], [F32], tm=1024, tn=MLA_Q_LORA)
    G["w_uq"] = _mm_tn(f"l{l}_g_uq", sv["cqn"], d_q_lat, tm=MLA_Q_LORA, tn=MLA_QW, ts=512)
    d_ckvn, = _mm_nt(f"l{l}_b_mm_ukv", d_kv_lat, W["w_ukv"], [F32], tm=1024, tn=MLA_KV_LORA)
    G["w_ukv"] = _mm_tn(f"l{l}_g_ukv", sv["ckvn"], d_kv_lat, tm=MLA_KV_LORA, tn=MLA_KVW, ts=512)

    def mix_prep_bwd(cq, ckv, gq, gkv, dcqn, dckvn, dqs, dks, dvs, cq_t, sq_t, ck_t, sk_t):
        d_cq, dgq = _rms_bwd(cq, gq, dcqn)
        d_ckv, dgkv = _rms_bwd(ckv, gkv, dckvn)
        dk2 = jnp.concatenate([dks[0], dks[1]], axis=1)
        dv2 = jnp.concatenate([dvs[0], dvs[1]], axis=1)
        return (d_cq, d_ckv, _rope_t(dqs, cq_t, sq_t, SWA_HEAD_DIM // 2), _rope_t(dk2, ck_t, sk_t, SWA_HEAD_DIM // 2), dv2, dgq, dgkv)

    kv3 = lambda a: (a.reshape(SWA_KV_HEADS, S, SWA_HEAD_DIM), (SWA_KV_HEADS, T, SWA_HEAD_DIM), lambda i: (0, i, 0))
    d_cq, d_ckv, d_qswa, d_kswa, d_vswa, G["g_q_lat"], G["g_kv_lat"] = _rowwise(
        f"l{l}_b_mix_prep", mix_prep_bwd,
        [_ri(sv["proj"], T, 256, _cb("c_q")), _ri(sv["proj"], T, 128, _cb("c_kv")), _bi(P["g_q_lat"]), _bi(P["g_kv_lat"]), _ri(d_cqn, T), _ri(d_ckvn, T),
         _ri(dq_swa, T), kv3(dk_swa), kv3(dv_swa), _ri(tb["c_swa_q"], T), _ri(tb["s_swa_q"], T), _ri(tb["c_swa_k"], T), _ri(tb["s_swa_k"], T)],
        [_ro(S, 256, BF16, T), _ro(S, 128, BF16, T), _ro(S, 512, BF16, T), _ro(S, 128, BF16, T), _ro(S, 128, BF16, T)], [256, 128], n_tiles=nt)
    pieces = dict(gates=d_gl, q_swa=d_qswa, q_sb=dq_sb, k_sb=dk_sb, v_sb=dv_sb, c_q=d_cq, c_kv=d_ckv, k_swa=d_kswa, v_swa=d_vswa, k_rope=d_krope)
    d_proj = jnp.concatenate([pieces[n].astype(BF16) for n in PERM_ORDER], axis=1)
    d_h, = _mm_nt(f"l{l}_b_mm_in", d_proj, W["w_in"], [F32], tm=512, tn=512)
    G["w_in"] = _mm_tn(f"l{l}_g_in", sv["h"], d_proj, tm=D_MODEL, tn=IN_WIDTH_P // 2, ts=512)
    dx, G["g_mix_pre"] = _rowwise(f"l{l}_b_pre1", pre_norm_bwd, [_ri(sv["x"], T), _bi(P["g_mix_pre"]), _ri(d_h, T), _ri(dx1, T)],
                                 [_ro(S, D_MODEL, F32, T)], [D_MODEL], n_tiles=nt)
    return dx, G


def _pack_rows(vecs, rows):
    flat = jnp.concatenate([v.reshape(-1) for v in vecs])
    return jnp.pad(flat, (0, rows * PACK_COLS - flat.shape[0])).reshape(rows, PACK_COLS)


def kernel(x, positions, g_mix_pre, w_in, b_gate, g_q_lat, g_kv_lat, w_uq, w_ukv, swa_sinks, w_o_mla, w_o_swa, w_o_sb, w_out, g_mix_post, g_mlp_pre, w_up, w_down, g_mlp_post, loss_target, m_g_mix_pre, m_w_in, m_b_gate, m_g_q_lat, m_g_kv_lat, m_w_uq, m_w_ukv, m_swa_sinks, m_w_o_mla, m_w_o_swa, m_w_o_sb, m_w_out, m_g_mix_post, m_g_mlp_pre, m_w_up, m_w_down, m_g_mlp_post, v_g_mix_pre, v_w_in, v_b_gate, v_g_q_lat, v_g_kv_lat, v_w_uq, v_w_ukv, v_swa_sinks, v_w_o_mla, v_w_o_swa, v_w_o_sb, v_w_out, v_g_mix_post, v_g_mlp_pre, v_w_up, v_w_down, v_g_mlp_post):
    a = dict(locals())
    S = x.shape[1]
    depth = w_in.shape[0]
    T = min(256, S)
    xs = x.reshape(S, D_MODEL)
    tb = _tables(positions)

    gathered = _exchange("gather_weights", [a[n].astype(BF16).reshape(-1, a[n].shape[2]) for n in BIG], per_peer=False)
    full = {n: _from_shards(n, g, a[n].shape) for n, g in zip(BIG, gathered)}
    layers = []
    for l in range(depth):
        W = {n: full[n][l] for n in BIG}
        W["w_in"] = _perm_w_in(W["w_in"])
        W["w_uq"] = _perm_w_uq(W["w_uq"])
        W["w_ukv"] = _perm_w_ukv(W["w_ukv"])
        P = {n: a[n][l].reshape(1, -1) for n in SMALL if n != "swa_sinks"}
        P["swa_sinks"] = a["swa_sinks"][l]
        layers.append((W, P))

    saved = []
    h = xs
    for l, (W, P) in enumerate(layers):
        h, sv = _layer_fwd(l, h, W, P, tb, T)
        saved.append(sv)

    def loss_head(yv, tv):
        err = yv - tv
        part = 0.5 * jnp.sum(jnp.mean(err * err, axis=1, keepdims=True), axis=0, keepdims=True)
        return err * (1.0 / D_MODEL), jnp.broadcast_to(part, (1, LANES))

    dh, loss_row = _rowwise("loss_head", loss_head, [_ri(h, T), _ri(loss_target.reshape(S, D_MODEL), T)], [_ro(S, D_MODEL, F32, T)], [LANES],
                            n_tiles=S // T)
    loss = lax.psum(loss_row[0, 0], ("x", "y", "c"))

    grads = [None] * depth
    for l in reversed(range(depth)):
        W, P = layers[l]
        dh, grads[l] = _layer_bwd(l, dh, saved[l], W, P, tb, T)
    grad_x = dh.reshape(x.shape)

    unperm = dict(w_in=_unperm_w_in, w_uq=_unperm_w_uq, w_ukv=_unperm_w_ukv)
    send = [_to_shards(n, jnp.stack([unperm.get(n, lambda t: t)(grads[l][n]).astype(BF16) for l in range(depth)])) for n in BIG]
    recv = _exchange("scatter_grads", send, per_peer=True)
    out = {}
    for n, parts in zip(BIG, recv):
        shp = a[n].shape
        rows, cols = shp[0] * shp[1], shp[2]
        res = _adamw("adamw_" + n, parts, a[n].reshape(rows, cols), a["m_" + n].reshape(rows, cols), a["v_" + n].reshape(rows, cols),
                     tile=min(256, rows))
        out[n] = [r.reshape(shp) for r in res]

    small_total = sum(a[n].size for n in SMALL)
    small_rows = -(-small_total // (8 * PACK_COLS)) * 8
    sg = _pack_rows([jnp.stack([grads[l][n].reshape(-1) for l in range(depth)]) for n in SMALL], small_rows)
    sg_all, = _exchange("gather_small_grads", [sg], per_peer=False)
    res = _adamw("adamw_small", sg_all, _pack_rows([a[n] for n in SMALL], small_rows), _pack_rows([a["m_" + n] for n in SMALL], small_rows),
                 _pack_rows([a["v_" + n] for n in SMALL], small_rows), tile=small_rows)
    off = 0
    for n in SMALL:
        cnt = a[n].size
        out[n] = [r.reshape(-1)[off:off + cnt].reshape(a[n].shape) for r in res]
        off += cnt

    order = ("g_mix_pre", "w_in", "b_gate", "g_q_lat", "g_kv_lat", "w_uq", "w_ukv", "swa_sinks", "w_o_mla", "w_o_swa", "w_o_sb", "w_out",
             "g_mix_post", "g_mlp_pre", "w_up", "w_down", "g_mlp_post")
    return (loss, grad_x, *[out[n][0] for n in order], *[out[n][1] for n in order], *[out[n][2] for n in order], *[out[n][3] for n in order])
```

```python
import functools

import jax
import jax.numpy as jnp
from jax import lax
from jax.experimental import pallas as pl
from jax.experimental.pallas import tpu as pltpu

F32, BF16 = jnp.float32, jnp.bfloat16

D_MODEL = 1024
DEPTH = 4
MLA_HEADS, MLA_Q_LORA, MLA_KV_LORA, MLA_NOPE, MLA_ROPE, MLA_V = 8, 256, 128, 64, 32, 64
SWA_HEADS, SWA_KV_HEADS, SWA_HEAD_DIM, SWA_WINDOW = 8, 2, 64, 128
SB_HEADS, SB_HEAD_DIM = 8, 64
D_FF = 4 * D_MODEL
ROPE_THETA = 10000.0
EPS = 1e-6
N_DEV = 8
ADAM_LR, ADAM_B1, ADAM_B2, ADAM_EPS, ADAM_WD, ADAM_STEP = 0.001, 0.9, 0.999, 1e-08, 0.01, 10

LANES = 128
VMEM_LIMIT_MAX = 60 * 1024 * 1024
VMEM_LIMIT_MIN = 32 * 1024 * 1024

ORIG_COLS = dict(c_q=(0, 256), c_kv=(256, 128), k_rope=(384, 32), q_swa=(416, 512), k_swa=(928, 128), v_swa=(1056, 128),
                 q_sb=(1184, 512), k_sb=(1696, 512), v_sb=(2208, 512), gates=(2720, 3072))
IN_WIDTH = 5792
PERM_ORDER = ("gates", "q_swa", "q_sb", "k_sb", "v_sb", "c_q", "c_kv", "k_swa", "v_swa", "k_rope")
PERM_WIDTH = dict(gates=3072, q_swa=512, q_sb=512, k_sb=512, v_sb=512, c_q=256, c_kv=128, k_swa=128, v_swa=128, k_rope=128)
PERM_OFF = {}
_o = 0
for _n in PERM_ORDER:
    PERM_OFF[_n] = _o
    _o += PERM_WIDTH[_n]
IN_WIDTH_P = _o
MLA_QW = MLA_HEADS * LANES
MLA_KVW = MLA_HEADS * LANES + MLA_HEADS * MLA_V

NT = (((1,), (1,)), ((), ()))
TN = (((0,), (0,)), ((), ()))
NEG = -1e30


def _cparams(sem, block_bytes):
    limit = int(min(VMEM_LIMIT_MAX, max(VMEM_LIMIT_MIN, 2 * block_bytes + (16 << 20))))
    return pltpu.CompilerParams(dimension_semantics=sem, vmem_limit_bytes=limit)


def _nbytes(shape, dtype):
    n = 1
    for s in shape:
        n *= s
    return n * jnp.dtype(dtype).itemsize


def _ri(arr, tile, width=None, cb=0):
    width = arr.shape[1] if width is None else width
    return (arr, (tile, width), lambda i, cb=cb: (i, cb))


def _bi(arr):
    return (arr, arr.shape, lambda i: (0, 0))


def _ro(rows, width, dtype, tile):
    return ((rows, width), dtype, (tile, width), lambda i: (i, 0))


def _rowwise(name, fn, ins, outs, reds=(), *, n_tiles):
    n_in, n_out = len(ins), len(outs)

    def body(*refs):
        vals = fn(*[r[...] for r in refs[:n_in]])
        for r, v in zip(refs[n_in:n_in + n_out], vals[:n_out]):
            if isinstance(v, (list, tuple)):
                for j, vj in enumerate(v):
                    r[j] = vj.astype(r.dtype)
            else:
                r[...] = v.astype(r.dtype)
        if reds:
            @pl.when(pl.program_id(0) == 0)
            def _():
                for r in refs[n_in + n_out:]:
                    r[...] = jnp.zeros_like(r)
            for r, v in zip(refs[n_in + n_out:], vals[n_out:]):
                r[...] += v

    block_bytes = sum(_nbytes(b, a.dtype) for a, b, _ in ins) + sum(_nbytes(b, d) for _, d, b, _ in outs)
    res = pl.pallas_call(
        body, name=name, grid=(n_tiles,),
        in_specs=[pl.BlockSpec(b, m) for _, b, m in ins],
        out_specs=[pl.BlockSpec(b, m) for _, _, b, m in outs] + [pl.BlockSpec((1, w), lambda i: (0, 0)) for w in reds],
        out_shape=[jax.ShapeDtypeStruct(s, d) for s, d, _, _ in outs] + [jax.ShapeDtypeStruct((1, w), F32) for w in reds],
        compiler_params=_cparams(("arbitrary",) if reds else ("parallel",), block_bytes),
    )(*[a for a, _, _ in ins])
    return res


def _rms(x, g):
    r = lax.rsqrt(jnp.mean(x * x, axis=1, keepdims=True) + EPS)
    return x * r * g


def _rms_bwd(x, g, dy):
    r = lax.rsqrt(jnp.mean(x * x, axis=1, keepdims=True) + EPS)
    xn = x * r
    dxn = dy * g
    dx = r * (dxn - xn * jnp.mean(dxn * xn, axis=1, keepdims=True))
    return dx, jnp.sum(dy * xn, axis=0, keepdims=True)


def _swap_halves(x, half):
    n = x.shape[1]
    lane = lax.broadcasted_iota(jnp.int32, x.shape, 1)
    first = (lane % (2 * half)) < half
    return jnp.where(first, pltpu.roll(x, n - half, axis=1), pltpu.roll(x, half, axis=1))


def _rope(x, c, sg, half):
    return x * c + _swap_halves(x, half) * sg


def _rope_t(dy, c, sg, half):
    return dy * c - _swap_halves(dy, half) * sg


def _mm_nn(name, a, b, outs, *, tm, tn, extras=(), epilogue=None):
    M, K = a.shape
    N = b.shape[1]
    tm = min(tm, M)
    n_e = len(extras)

    def body(*refs):
        a_ref, b_ref = refs[:2]
        acc = jnp.dot(a_ref[...].astype(BF16), b_ref[...].astype(BF16), preferred_element_type=F32)
        vals = (acc,) * len(outs) if epilogue is None else epilogue(acc, *[r[...] for r in refs[2:2 + n_e]])
        for r, v in zip(refs[2 + n_e:], vals):
            r[...] = v.astype(r.dtype)

    block_bytes = (_nbytes((tm, K), a.dtype) + _nbytes((K, tn), b.dtype) + sum(_nbytes((tm, tn), e.dtype) for e in extras)
                   + sum(_nbytes((tm, tn), d) for d in outs) + _nbytes((tm, tn), F32))
    return pl.pallas_call(
        body, name=name, grid=(N // tn, M // tm),
        in_specs=[pl.BlockSpec((tm, K), lambda j, i: (i, 0)), pl.BlockSpec((K, tn), lambda j, i: (0, j))]
        + [pl.BlockSpec((tm, tn), lambda j, i: (i, j)) for _ in extras],
        out_specs=[pl.BlockSpec((tm, tn), lambda j, i: (i, j)) for _ in outs],
        out_shape=[jax.ShapeDtypeStruct((M, N), d) for d in outs],
        compiler_params=_cparams(("parallel", "parallel"), block_bytes),
    )(a, b, *extras)


def _mm_nt(name, a, b, outs, *, tm, tn, extras=(), epilogue=None):
    M, N = a.shape
    K = b.shape[0]
    tm = min(tm, M)
    n_e = len(extras)

    def body(*refs):
        a_ref, b_ref = refs[:2]
        acc = lax.dot_general(a_ref[...].astype(BF16), b_ref[...].astype(BF16), NT, preferred_element_type=F32)
        vals = (acc,) * len(outs) if epilogue is None else epilogue(acc, *[r[...] for r in refs[2:2 + n_e]])
        for r, v in zip(refs[2 + n_e:], vals):
            r[...] = v.astype(r.dtype)

    block_bytes = (_nbytes((tm, N), a.dtype) + _nbytes((tn, N), b.dtype) + sum(_nbytes((tm, tn), e.dtype) for e in extras)
                   + sum(_nbytes((tm, tn), d) for d in outs) + _nbytes((tm, tn), F32))
    return pl.pallas_call(
        body, name=name, grid=(K // tn, M // tm),
        in_specs=[pl.BlockSpec((tm, N), lambda j, i: (i, 0)), pl.BlockSpec((tn, N), lambda j, i: (j, 0))]
        + [pl.BlockSpec((tm, tn), lambda j, i: (i, j)) for _ in extras],
        out_specs=[pl.BlockSpec((tm, tn), lambda j, i: (i, j)) for _ in outs],
        out_shape=[jax.ShapeDtypeStruct((M, K), d) for d in outs],
        compiler_params=_cparams(("parallel", "parallel"), block_bytes),
    )(a, b, *extras)


def _mm_tn(name, a, b, *, tm, tn, ts):
    S, K = a.shape
    N = b.shape[1]
    ts = min(ts, S)

    def body(a_ref, b_ref, o_ref):
        @pl.when(pl.program_id(2) == 0)
        def _():
            o_ref[...] = jnp.zeros_like(o_ref)
        o_ref[...] += lax.dot_general(a_ref[...].astype(BF16), b_ref[...].astype(BF16), TN, preferred_element_type=F32)

    block_bytes = _nbytes((ts, tm), a.dtype) + _nbytes((ts, tn), b.dtype) + 2 * _nbytes((tm, tn), F32)
    return pl.pallas_call(
        body, name=name, grid=(K // tm, N // tn, S // ts),
        in_specs=[pl.BlockSpec((ts, tm), lambda i, j, s: (s, i)), pl.BlockSpec((ts, tn), lambda i, j, s: (s, j))],
        out_specs=pl.BlockSpec((tm, tn), lambda i, j, s: (i, j)),
        out_shape=jax.ShapeDtypeStruct((K, N), F32),
        compiler_params=_cparams(("parallel", "parallel", "arbitrary"), block_bytes),
    )(a, b)


def _lane_pack(cols, rows):
    lane = lax.broadcasted_iota(jnp.int32, (rows, LANES), 1)
    val = jnp.zeros((rows, LANES), F32)
    for h, c in enumerate(cols):
        val = jnp.where(lane == h, c, val)
    return val


def _mask(kb, row, tk, window):
    col = kb * tk + lax.broadcasted_iota(jnp.int32, (1, tk), 1)
    ok = col <= row
    if window:
        ok = ok & ((row - col) < window)
    return ok


def _kb_range(i, tq, tk, window):
    end = ((i + 1) * tq + tk - 1) // tk
    if window:
        lo = jnp.maximum(i * tq - (window - 1), 0) // tk
        return lo, lo, end
    return 0, (i * tq) // tk, end


def _softmax_fwd(name, q, k, v, sinks, *, groups, heads, q_spec, k_spec, v_spec, o_width, tq, tk, scale, window, q_scaled):
    S = q.shape[0]
    nq = S // tq
    nh = len(heads)
    has_sink = sinks is not None
    dv = heads[0][2].stop - heads[0][2].start

    def body(*refs):
        if has_sink:
            sink_ref, q_ref, k_ref, v_ref, o_ref, lse_ref = refs
        else:
            q_ref, k_ref, v_ref, o_ref, lse_ref = refs
        g, i = pl.program_id(0), pl.program_id(1)
        q_all = q_ref[...]
        row = i * tq + lax.broadcasted_iota(jnp.int32, (tq, 1), 0)
        lo, mid, hi = _kb_range(i, tq, tk, window)
        qhs = [q_all[:, qs] if q_scaled else q_all[:, qs] * scale for qs, _, _ in heads]

        def blk(kb, carry, masked):
            r0 = pl.multiple_of(kb * tk, tk)
            k_all = k_ref[pl.ds(r0, tk), :]
            v_all = v_ref[pl.ds(r0, tk), :]
            if masked:
                ok = _mask(kb, row, tk, window)
            new = []
            for qh, (_, ks, vs), (m, l, acc) in zip(qhs, heads, carry):
                s = lax.dot_general(qh, k_all[:, ks], NT, preferred_element_type=F32)
                if masked:
                    s = jnp.where(ok, s, NEG)
                m_new = jnp.maximum(m, jnp.max(s, axis=1, keepdims=True))
                alpha = jnp.exp(m - m_new)
                p = jnp.exp(s - m_new)
                if masked:
                    p = jnp.where(ok, p, 0.0)
                l = alpha * l + jnp.sum(p, axis=1, keepdims=True)
                acc = alpha * acc + jnp.dot(p.astype(BF16), v_all[:, vs], preferred_element_type=F32)
                new.append((m_new, l, acc))
            return tuple(new)

        carry = []
        for h in range(nh):
            if has_sink:
                m0 = jnp.full((tq, 1), sink_ref[g * nh + h], F32)
                l0 = jnp.ones((tq, 1), F32)
            else:
                m0 = jnp.full((tq, 1), NEG, F32)
                l0 = jnp.zeros((tq, 1), F32)
            carry.append((m0, l0, jnp.zeros((tq, dv), F32)))
        carry = lax.fori_loop(lo, mid, functools.partial(blk, masked=False), tuple(carry))
        carry = lax.fori_loop(mid, hi, functools.partial(blk, masked=True), carry)
        o_ref[...] = jnp.concatenate([acc / l for _, l, acc in carry], axis=1).astype(o_ref.dtype)
        lse_ref[0] = _lane_pack([m + jnp.log(l) for m, l, _ in carry], tq)

    in_specs = [pl.BlockSpec(*q_spec), pl.BlockSpec(*k_spec), pl.BlockSpec(*v_spec)]
    args = [q, k, v]
    if has_sink:
        in_specs = [pl.BlockSpec(memory_space=pltpu.SMEM)] + in_specs
        args = [sinks] + args
    wo = nh * dv
    block_bytes = _nbytes(q_spec[0], q.dtype) + _nbytes(k_spec[0], k.dtype) + _nbytes(v_spec[0], v.dtype) + 4 * tq * (wo + LANES)
    return pl.pallas_call(
        body, name=name, grid=(groups, nq), in_specs=in_specs,
        out_specs=[pl.BlockSpec((tq, wo), lambda g, i: (i, g)), pl.BlockSpec((1, tq, LANES), lambda g, i: (g, i, 0))],
        out_shape=[jax.ShapeDtypeStruct((S, o_width), BF16), jax.ShapeDtypeStruct((groups, S, LANES), F32)],
        compiler_params=_cparams(("parallel", "arbitrary"), block_bytes),
    )(*args)


def _softmax_bwd(name, q, k, v, o, do, lse, sinks, *, groups, heads, q_spec, k_spec, v_spec, tq, tk, scale, window, q_scaled):
    S = q.shape[0]
    nq = S // tq
    nh = len(heads)
    has_sink = sinks is not None
    dv = heads[0][2].stop - heads[0][2].start
    wo = nh * dv

    def body(*refs):
        if has_sink:
            sink_ref, q_ref, k_ref, v_ref, o_ref, do_ref, lse_ref, dq_ref, dk_ref, dv_ref, dsink_ref = refs
        else:
            q_ref, k_ref, v_ref, o_ref, do_ref, lse_ref, dq_ref, dk_ref, dv_ref = refs
        g, i = pl.program_id(0), pl.program_id(1)

        @pl.when(i == 0)
        def _():
            dk_ref[...] = jnp.zeros_like(dk_ref)
            dv_ref[...] = jnp.zeros_like(dv_ref)
            if has_sink:
                dsink_ref[...] = jnp.zeros_like(dsink_ref)

        q_all = q_ref[...]
        o_all = o_ref[...].astype(F32)
        do_all = do_ref[...].astype(F32)
        lse_all = lse_ref[0]
        row = i * tq + lax.broadcasted_iota(jnp.int32, (tq, 1), 0)
        lo, mid, hi = _kb_range(i, tq, tk, window)
        per_head = []
        for h, (qs, ks, vs) in enumerate(heads):
            osl = slice(h * dv, (h + 1) * dv)
            doh = do_all[:, osl]
            delta = jnp.sum(doh * o_all[:, osl], axis=1, keepdims=True)
            qh = q_all[:, qs] if q_scaled else q_all[:, qs] * scale
            per_head.append((qh, doh.astype(BF16), delta, lse_all[:, h:h + 1], ks, vs))

        def blk(kb, dqs, masked):
            r0 = pl.multiple_of(kb * tk, tk)
            k_all = k_ref[pl.ds(r0, tk), :]
            v_all = v_ref[pl.ds(r0, tk), :]
            if masked:
                ok = _mask(kb, row, tk, window)
            dk_parts, dv_parts, new_dqs = {}, {}, []
            for (qh, doh, delta, lse_h, ks, vs), dq in zip(per_head, dqs):
                kk, vv = k_all[:, ks], v_all[:, vs]
                s = lax.dot_general(qh, kk, NT, preferred_element_type=F32)
                p = jnp.exp(s - lse_h)
                if masked:
                    p = jnp.where(ok, p, 0.0)
                dp = lax.dot_general(doh, vv, NT, preferred_element_type=F32)
                ds = p * (dp - delta)
                new_dqs.append(dq + jnp.dot(ds.astype(BF16), kk, preferred_element_type=F32))
                dk_c = jnp.dot(ds.T.astype(BF16), qh, preferred_element_type=F32)
                dv_c = jnp.dot(p.T.astype(BF16), doh, preferred_element_type=F32)
                dk_parts[ks.start] = dk_parts[ks.start] + dk_c if ks.start in dk_parts else dk_c
                dv_parts[vs.start] = dv_parts[vs.start] + dv_c if vs.start in dv_parts else dv_c
            dk_ref[pl.ds(r0, tk), :] += jnp.concatenate([dk_parts[s0] for s0 in sorted(dk_parts)], axis=1)
            dv_ref[pl.ds(r0, tk), :] += jnp.concatenate([dv_parts[s0] for s0 in sorted(dv_parts)], axis=1)
            return tuple(new_dqs)

        dqs = tuple(jnp.zeros((tq, qs.stop - qs.start), F32) for qs, _, _ in heads)
        dqs = lax.fori_loop(lo, mid, functools.partial(blk, masked=False), dqs)
        dqs = lax.fori_loop(mid, hi, functools.partial(blk, masked=True), dqs)
        dq_ref[...] = jnp.concatenate([dq * scale for dq in dqs], axis=1)
        if has_sink:
            for h, (_, _, delta, lse_h, _, _) in enumerate(per_head):
                p_sink = jnp.exp(sink_ref[g * nh + h] - lse_h)
                dsink_ref[0, h:h + 1, :] += jnp.broadcast_to(-jnp.sum(p_sink * delta, axis=0, keepdims=True), (1, LANES))

    in_specs = [pl.BlockSpec(*q_spec), pl.BlockSpec(*k_spec), pl.BlockSpec(*v_spec),
                pl.BlockSpec((tq, wo), lambda g, i: (i, g)), pl.BlockSpec((tq, wo), lambda g, i: (i, g)),
                pl.BlockSpec((1, tq, LANES), lambda g, i: (g, i, 0))]
    args = [q, k, v, o, do, lse]
    out_specs = [pl.BlockSpec(*q_spec), pl.BlockSpec(*k_spec), pl.BlockSpec(*v_spec)]
    out_shape = [jax.ShapeDtypeStruct(q.shape, F32), jax.ShapeDtypeStruct(k.shape, F32), jax.ShapeDtypeStruct(v.shape, F32)]
    if has_sink:
        in_specs = [pl.BlockSpec(memory_space=pltpu.SMEM)] + in_specs
        args = [sinks] + args
        out_specs.append(pl.BlockSpec((1, 8, LANES), lambda g, i: (g, 0, 0)))
        out_shape.append(jax.ShapeDtypeStruct((groups, 8, LANES), F32))
    block_bytes = (_nbytes(q_spec[0], q.dtype) + _nbytes(k_spec[0], k.dtype) + _nbytes(v_spec[0], v.dtype) + 6 * tq * wo + 4 * tq * LANES
                   + _nbytes(q_spec[0], F32) + _nbytes(k_spec[0], F32) + _nbytes(v_spec[0], F32))
    return pl.pallas_call(
        body, name=name, grid=(groups, nq), in_specs=in_specs, out_specs=out_specs, out_shape=out_shape,
        compiler_params=_cparams(("parallel", "arbitrary"), block_bytes),
    )(*args)


def _split_dot(x, u2):
    hi = x.astype(BF16)
    lo = (x - hi.astype(F32)).astype(BF16)
    return jnp.dot(jnp.concatenate([hi, lo], axis=1), u2, preferred_element_type=F32)


def _suffix_ones():
    r = lax.broadcasted_iota(jnp.int32, (2 * LANES, 2 * LANES), 0) % LANES
    c = lax.broadcasted_iota(jnp.int32, (2 * LANES, 2 * LANES), 1)
    return ((r > c) | (c >= LANES)).astype(BF16)


def _suffix_scan(x, uo, run):
    nc = x.shape[1] // LANES
    out = [None] * nc
    for c in reversed(range(nc)):
        st = _split_dot(x[:, c * LANES:(c + 1) * LANES], uo)
        out[c] = st[:, :LANES] + run
        run = run + st[:, LANES:]
    return (out[0] if nc == 1 else jnp.concatenate(out, axis=1)), run


SB_DEAD_LOG = -110.0


def _sb_logits(qh, kk, kb, row, tq, masked):
    z = lax.dot_general(qh, kk, NT, preferred_element_type=F32)
    nz = -z
    l = jnp.minimum(nz, 0.0) - jnp.log(1.0 + jnp.exp(jnp.minimum(z, nz)))
    ok = None
    if masked:
        col = kb * tq + lax.broadcasted_iota(jnp.int32, (1, tq), 1)
        ok = col < row
        l = jnp.where(ok, l, 0.0)
    return z, l, ok


def _sb_walk(blk, carry, i):
    carry = blk(0, carry, True, i)

    def live(c):
        t, heads = c
        top = jnp.max(heads[0][0])
        for h in heads[1:]:
            top = jnp.maximum(top, jnp.max(h[0]))
        return jnp.logical_and(t < i, top > SB_DEAD_LOG)

    def step(c):
        t, heads = c
        return t + 1, blk(t, heads, False, i - 1)

    return lax.while_loop(live, step, (jnp.int32(0), carry))[1]


def _sb_fwd(name, pb, *, q_cb, k_cb, v_cb, tq, scale):
    S = pb.shape[0]
    nq = S // tq
    hd = SB_HEAD_DIM
    groups = SB_HEADS * hd // LANES
    nh = LANES // hd

    def body(q_ref, k_ref, v_ref, o16_ref, o32_ref):
        i = pl.program_id(1)
        q_all = q_ref[...]
        row = i * tq + lax.broadcasted_iota(jnp.int32, (tq, 1), 0)
        uo = _suffix_ones()
        sls = [slice(h * hd, (h + 1) * hd) for h in range(nh)]
        qhs = [q_all[:, sl] * scale for sl in sls]

        def blk(t, carry, masked, base):
            kb = base - t
            r0 = pl.multiple_of(kb * tq, tq)
            k_all = k_ref[pl.ds(r0, tq), :]
            v_all = v_ref[pl.ds(r0, tq), :]
            new = []
            for qh, sl, (run_l, acc) in zip(qhs, sls, carry):
                z, l, ok = _sb_logits(qh, k_all[:, sl], kb, row, tq, masked)
                tail, run_l = _suffix_scan(l, uo, run_l)
                e = z + l + tail
                if masked:
                    e = jnp.where(ok, e, NEG)
                new.append((run_l, acc + jnp.dot(jnp.exp(e).astype(BF16), v_all[:, sl], preferred_element_type=F32)))
            return tuple(new)

        carry = tuple((jnp.zeros((tq, LANES), F32), jnp.zeros((tq, hd), F32)) for _ in range(nh))
        carry = _sb_walk(blk, carry, i)
        o = jnp.concatenate([acc for _, acc in carry], axis=1)
        o16_ref[...] = o.astype(BF16)
        o32_ref[...] = o

    block_bytes = 2 * tq * LANES + 2 * 2 * S * LANES + 6 * tq * LANES
    return pl.pallas_call(
        body, name=name, grid=(groups, nq),
        in_specs=[pl.BlockSpec((tq, LANES), lambda g, i: (i, q_cb + g)), pl.BlockSpec((S, LANES), lambda g, i: (0, k_cb + g)),
                  pl.BlockSpec((S, LANES), lambda g, i: (0, v_cb + g))],
        out_specs=[pl.BlockSpec((tq, LANES), lambda g, i: (i, g)), pl.BlockSpec((tq, LANES), lambda g, i: (i, g))],
        out_shape=[jax.ShapeDtypeStruct((S, groups * LANES), BF16), jax.ShapeDtypeStruct((S, groups * LANES), F32)],
        compiler_params=_cparams(("parallel", "arbitrary"), block_bytes),
    )(pb, pb, pb)


def _sb_bwd(name, pb, o32, do, *, q_cb, k_cb, v_cb, tq, scale):
    S = pb.shape[0]
    nq = S // tq
    hd = SB_HEAD_DIM
    groups = SB_HEADS * hd // LANES
    nh = LANES // hd

    def body(q_ref, k_ref, v_ref, o_ref, do_ref, dq_ref, dk_ref, dv_ref):
        i = pl.program_id(1)

        @pl.when(i == 0)
        def _():
            dk_ref[...] = jnp.zeros_like(dk_ref)
            dv_ref[...] = jnp.zeros_like(dv_ref)

        q_all = q_ref[...]
        o_all = o_ref[...]
        do_all = do_ref[...].astype(F32)
        row = i * tq + lax.broadcasted_iota(jnp.int32, (tq, 1), 0)
        uo = _suffix_ones()
        per_head = []
        for h in range(nh):
            sl = slice(h * hd, (h + 1) * hd)
            doh = do_all[:, sl].astype(BF16)
            total = jnp.sum(doh.astype(F32) * o_all[:, sl], axis=1, keepdims=True)
            per_head.append((q_all[:, sl] * scale, doh, jnp.broadcast_to(total, (tq, LANES)), sl))

        def blk(t, carry, masked, base):
            kb = base - t
            r0 = pl.multiple_of(kb * tq, tq)
            k_all = k_ref[pl.ds(r0, tq), :]
            v_all = v_ref[pl.ds(r0, tq), :]
            new, dk_c, dv_c = [], [], []
            for (qh, doh, total, sl), (run_l, run_g, dq) in zip(per_head, carry):
                kk, vv = k_all[:, sl], v_all[:, sl]
                z, l, ok = _sb_logits(qh, kk, kb, row, tq, masked)
                tail, run_l = _suffix_scan(l, uo, run_l)
                e = z + l
                beta = jnp.exp(e)
                e = e + tail
                if masked:
                    e = jnp.where(ok, e, NEG)
                a = jnp.exp(e).astype(BF16).astype(F32)
                gr = lax.dot_general(doh, vv, NT, preferred_element_type=F32) * a
                right, run_g = _suffix_scan(gr, uo, run_g)
                nc = tq // LANES
                prefix = (total if nc == 1 else jnp.tile(total, (1, nc))) - right
                dz = gr - beta * prefix
                if masked:
                    dz = jnp.where(ok, dz, 0.0)
                new.append((run_l, run_g, dq + jnp.dot(dz.astype(BF16), kk, preferred_element_type=F32)))
                dk_c.append(jnp.dot(dz.T.astype(BF16), qh, preferred_element_type=F32))
                dv_c.append(jnp.dot(a.T.astype(BF16), doh, preferred_element_type=F32))
            dk_ref[pl.ds(r0, tq), :] += jnp.concatenate(dk_c, axis=1)
            dv_ref[pl.ds(r0, tq), :] += jnp.concatenate(dv_c, axis=1)
            return tuple(new)

        zc = jnp.zeros((tq, LANES), F32)
        carry = tuple((zc, zc, jnp.zeros((tq, hd), F32)) for _ in range(nh))
        carry = _sb_walk(blk, carry, i)
        dq_ref[...] = jnp.concatenate([c[2] * scale for c in carry], axis=1)

    W = groups * LANES
    block_bytes = 2 * tq * LANES + 2 * 2 * S * LANES + 3 * 4 * tq * LANES + 2 * 4 * S * LANES
    return pl.pallas_call(
        body, name=name, grid=(groups, nq),
        in_specs=[pl.BlockSpec((tq, LANES), lambda g, i: (i, q_cb + g)), pl.BlockSpec((S, LANES), lambda g, i: (0, k_cb + g)),
                  pl.BlockSpec((S, LANES), lambda g, i: (0, v_cb + g)), pl.BlockSpec((tq, LANES), lambda g, i: (i, g)),
                  pl.BlockSpec((tq, LANES), lambda g, i: (i, g))],
        out_specs=[pl.BlockSpec((tq, LANES), lambda g, i: (i, g)), pl.BlockSpec((S, LANES), lambda g, i: (0, g)),
                   pl.BlockSpec((S, LANES), lambda g, i: (0, g))],
        out_shape=[jax.ShapeDtypeStruct((S, W), F32)] * 3,
        compiler_params=_cparams(("parallel", "arbitrary"), block_bytes),
    )(pb, pb, pb, o32, do)


def _exchange(name, srcs, per_peer):
    n = len(srcs)

    def body(*refs):
        src_refs, out_refs = refs[:n], refs[n:2 * n]
        send_sems, recv_sems, local_sems = refs[2 * n:]
        x, y, c = lax.axis_index("x"), lax.axis_index("y"), lax.axis_index("c")
        me = 4 * x + 2 * y + c

        def copy(j, k):
            px, py, pc = x ^ (k >> 2), y ^ ((k >> 1) & 1), c ^ (k & 1)
            s = src_refs[j].at[4 * px + 2 * py + pc] if per_peer else src_refs[j]
            return pltpu.make_async_remote_copy(
                src_ref=s, dst_ref=out_refs[j].at[me], send_sem=send_sems.at[j, k - 1], recv_sem=recv_sems.at[j, k - 1],
                device_id=(px, py, pc), device_id_type=pl.DeviceIdType.MESH)

        mine = [pltpu.make_async_copy(src_refs[j].at[me] if per_peer else src_refs[j], out_refs[j].at[me], local_sems.at[j]) for j in range(n)]
        copies = [copy(j, k) for k in range(1, N_DEV) for j in range(n)]
        for cp in mine + copies:
            cp.start()
        for cp in copies:
            cp.wait_recv()
        for cp in copies:
            cp.wait_send()
        for cp in mine:
            cp.wait()

    hbm = pl.BlockSpec(memory_space=pltpu.HBM)
    return pl.pallas_call(
        body, name=name, in_specs=[hbm] * n, out_specs=[hbm] * n,
        out_shape=[jax.ShapeDtypeStruct((N_DEV,) + tuple(s.shape[1:] if per_peer else s.shape), s.dtype) for s in srcs],
        scratch_shapes=[pltpu.SemaphoreType.DMA((n, N_DEV - 1)), pltpu.SemaphoreType.DMA((n, N_DEV - 1)), pltpu.SemaphoreType.DMA((n,))],
    )(*srcs)


def _adamw(name, parts, w, m, v, *, tile):
    R, C = w.shape

    def body(p_ref, w_ref, m_ref, v_ref, g_ref, d_ref, nm_ref, nv_ref):
        g = p_ref[0].astype(F32)
        for d in range(1, N_DEV):
            g = g + p_ref[d].astype(F32)
        wv = w_ref[...]
        mm = ADAM_B1 * m_ref[...] + (1.0 - ADAM_B1) * g
        vv = ADAM_B2 * v_ref[...] + (1.0 - ADAM_B2) * jnp.square(g)
        m_hat = mm / (1.0 - ADAM_B1 ** ADAM_STEP)
        v_hat = vv / (1.0 - ADAM_B2 ** ADAM_STEP)
        g_ref[...] = g
        d_ref[...] = -ADAM_LR * (m_hat / (jnp.sqrt(v_hat) + ADAM_EPS) + ADAM_WD * wv)
        nm_ref[...] = mm
        nv_ref[...] = vv

    blk = pl.BlockSpec((tile, C), lambda i: (i, 0))
    block_bytes = N_DEV * _nbytes((tile, C), parts.dtype) + 7 * _nbytes((tile, C), F32)
    return pl.pallas_call(
        body, name=name, grid=(R // tile,),
        in_specs=[pl.BlockSpec((N_DEV, tile, C), lambda i: (0, i, 0)), blk, blk, blk],
        out_specs=[blk] * 4, out_shape=[jax.ShapeDtypeStruct((R, C), F32)] * 4,
        compiler_params=_cparams(("parallel",), block_bytes),
    )(parts, w, m, v)


def _perm_w_in(w):
    cols = [w[:, ORIG_COLS[n][0]:ORIG_COLS[n][0] + ORIG_COLS[n][1]] for n in PERM_ORDER]
    cols.append(jnp.zeros((w.shape[0], IN_WIDTH_P - IN_WIDTH), w.dtype))
    return jnp.concatenate(cols, axis=1)


def _unperm_w_in(wp):
    order = sorted(ORIG_COLS, key=lambda n: ORIG_COLS[n][0])
    return jnp.concatenate([wp[:, PERM_OFF[n]:PERM_OFF[n] + ORIG_COLS[n][1]] for n in order], axis=1)


def _perm_w_uq(w):
    w3 = w.reshape(w.shape[0], MLA_HEADS, MLA_NOPE + MLA_ROPE)
    return jnp.pad(w3, ((0, 0), (0, 0), (0, LANES - MLA_NOPE - MLA_ROPE))).reshape(w.shape[0], MLA_QW)


def _unperm_w_uq(wp):
    return wp.reshape(wp.shape[0], MLA_HEADS, LANES)[:, :, :MLA_NOPE + MLA_ROPE].reshape(wp.shape[0], -1)


def _perm_w_ukv(w):
    w3 = w.reshape(w.shape[0], MLA_HEADS, MLA_NOPE + MLA_V)
    kp = jnp.pad(w3[:, :, :MLA_NOPE], ((0, 0), (0, 0), (0, LANES - MLA_NOPE))).reshape(w.shape[0], MLA_HEADS * LANES)
    return jnp.concatenate([kp, w3[:, :, MLA_NOPE:].reshape(w.shape[0], MLA_HEADS * MLA_V)], axis=1)


def _unperm_w_ukv(wp):
    n = wp.shape[0]
    kp = wp[:, :MLA_HEADS * LANES].reshape(n, MLA_HEADS, LANES)[:, :, :MLA_NOPE]
    vp = wp[:, MLA_HEADS * LANES:].reshape(n, MLA_HEADS, MLA_V)
    return jnp.concatenate([kp, vp], axis=2).reshape(n, MLA_HEADS * (MLA_NOPE + MLA_V))


BIG = ("w_in", "w_uq", "w_ukv", "w_o_mla", "w_o_swa", "w_o_sb", "w_out", "w_up", "w_down")
ROW_SHARDED = ("w_out", "w_down")
SMALL = ("g_mix_pre", "b_gate", "g_q_lat", "g_kv_lat", "swa_sinks", "g_mix_post", "g_mlp_pre", "g_mlp_post")
PACK_COLS = 1024


def _to_shards(name, full):
    L, R, C = full.shape
    if name in ROW_SHARDED:
        return full.reshape(L, N_DEV, R // N_DEV, C).transpose(1, 0, 2, 3).reshape(N_DEV, L * R // N_DEV, C)
    return full.reshape(L, R, N_DEV, C // N_DEV).transpose(2, 0, 1, 3).reshape(N_DEV, L * R, C // N_DEV)


def _from_shards(name, gathered, shard_shape):
    L, r, c = shard_shape
    a = gathered.reshape(N_DEV, L, r, c)
    if name in ROW_SHARDED:
        return a.transpose(1, 0, 2, 3).reshape(L, N_DEV * r, c)
    return a.transpose(1, 2, 0, 3).reshape(L, r, N_DEV * c)


def _tables(positions):
    pos = positions.astype(F32).reshape(-1, 1)

    def cs(d):
        inv = 1.0 / (ROPE_THETA ** (jnp.arange(0, d, 2, dtype=F32) / d))
        ang = pos * inv
        c, s = jnp.cos(ang), jnp.sin(ang)
        return jnp.concatenate([c, c], axis=1), jnp.concatenate([-s, s], axis=1)

    c64, s64 = cs(SWA_HEAD_DIM)
    c32, s32 = cs(MLA_ROPE)
    n = pos.shape[0]
    one, zero = jnp.ones((n, MLA_NOPE), F32), jnp.zeros((n, MLA_NOPE), F32)
    pad0 = jnp.zeros((n, LANES - MLA_NOPE - MLA_ROPE), F32)
    cq = jnp.concatenate([one, c32, pad0], axis=1)
    sq = jnp.concatenate([zero, s32, pad0], axis=1)
    padk = jnp.zeros((n, LANES - MLA_ROPE), F32)
    return dict(
        c_swa_q=jnp.tile(c64, (1, SWA_HEADS)), s_swa_q=jnp.tile(s64, (1, SWA_HEADS)),
        c_swa_k=jnp.tile(c64, (1, SWA_KV_HEADS)), s_swa_k=jnp.tile(s64, (1, SWA_KV_HEADS)),
        c_mla_q=jnp.tile(cq, (1, MLA_HEADS)), s_mla_q=jnp.tile(sq, (1, MLA_HEADS)),
        c_mla_k=jnp.concatenate([c32, padk], axis=1), s_mla_k=jnp.concatenate([s32, padk], axis=1))


def _cb(name):
    return PERM_OFF[name] // PERM_WIDTH[name]


MLA_SPEC = dict(groups=MLA_HEADS // 2, tq=512, tk=512, scale=(MLA_NOPE + MLA_ROPE) ** -0.5, window=0, q_scaled=True,
                heads=[(slice(h * LANES, (h + 1) * LANES), slice(h * LANES, (h + 1) * LANES), slice(h * MLA_V, (h + 1) * MLA_V)) for h in range(2)])
SWA_G = SWA_HEADS // SWA_KV_HEADS
SWA_SPEC = dict(groups=SWA_KV_HEADS, tq=256, tk=128, scale=SWA_HEAD_DIM ** -0.5, window=SWA_WINDOW, q_scaled=False,
                heads=[(slice(g * SWA_HEAD_DIM, (g + 1) * SWA_HEAD_DIM), slice(0, SWA_HEAD_DIM), slice(0, SWA_HEAD_DIM)) for g in range(SWA_G)])
SB_SPEC = dict(tq=512, scale=SB_HEAD_DIM ** -0.5, q_cb=PERM_OFF["q_sb"] // LANES, k_cb=PERM_OFF["k_sb"] // LANES,
               v_cb=PERM_OFF["v_sb"] // LANES)


def _mla_specs(S):
    tq = MLA_SPEC["tq"]
    return dict(q_spec=((tq, 2 * LANES), lambda g, i: (i, g)), k_spec=((S, 2 * LANES), lambda g, i: (0, g)),
                v_spec=((S, 2 * MLA_V), lambda g, i: (0, g)))


def _swa_specs(S):
    tq = SWA_SPEC["tq"]
    return dict(q_spec=((tq, SWA_G * SWA_HEAD_DIM), lambda g, i: (i, g)), k_spec=((S, SWA_HEAD_DIM), lambda g, i: (g, 0)),
                v_spec=((S, SWA_HEAD_DIM), lambda g, i: (g, 0)))


def _layer_fwd(l, x, W, P, tb, T):
    S = x.shape[0]
    nt = S // T
    h, = _rowwise(f"l{l}_norm_in", lambda xv, g: (_rms(xv, g),), [_ri(x, T), _bi(P["g_mix_pre"])], [_ro(S, D_MODEL, BF16, T)], n_tiles=nt)
    proj, proj16 = _mm_nn(f"l{l}_mm_in", h, W["w_in"], [F32, BF16], tm=512, tn=IN_WIDTH_P // 2)

    def mix_prep(cq, ckv, qs, ks, vs, gq, gkv, cq_t, sq_t, ck_t, sk_t):
        ksr = _rope(ks, ck_t, sk_t, SWA_HEAD_DIM // 2)
        hd = SWA_HEAD_DIM
        return (_rms(cq, gq), _rms(ckv, gkv), _rope(qs, cq_t, sq_t, hd // 2),
                [ksr[:, :hd], ksr[:, hd:]], [vs[:, :hd], vs[:, hd:]])

    kv3 = lambda dt: ((SWA_KV_HEADS, S, SWA_HEAD_DIM), dt, (SWA_KV_HEADS, T, SWA_HEAD_DIM), lambda i: (0, i, 0))
    cqn, ckvn, q_swa, k_swa, v_swa = _rowwise(
        f"l{l}_mix_prep", mix_prep,
        [_ri(proj, T, 256, _cb("c_q")), _ri(proj, T, 128, _cb("c_kv")), _ri(proj, T, 512, _cb("q_swa")), _ri(proj, T, 128, _cb("k_swa")),
         _ri(proj, T, 128, _cb("v_swa")), _bi(P["g_q_lat"]), _bi(P["g_kv_lat"]), _ri(tb["c_swa_q"], T), _ri(tb["s_swa_q"], T),
         _ri(tb["c_swa_k"], T), _ri(tb["s_swa_k"], T)],
        [_ro(S, 256, BF16, T), _ro(S, 128, BF16, T), _ro(S, 512, BF16, T), kv3(BF16), kv3(BF16)], n_tiles=nt)
    k_swa = k_swa.reshape(SWA_KV_HEADS * S, SWA_HEAD_DIM)
    v_swa = v_swa.reshape(SWA_KV_HEADS * S, SWA_HEAD_DIM)
    q_lat, = _mm_nn(f"l{l}_mm_uq", cqn, W["w_uq"], [F32], tm=1024, tn=MLA_QW)
    kv_lat, = _mm_nn(f"l{l}_mm_ukv", ckvn, W["w_ukv"], [F32], tm=1024, tn=MLA_KVW)

    def mla_prep(q, kk, vv, kr, cq_t, sq_t, ck_t, sk_t):
        kpe = pltpu.roll(_rope(kr, ck_t, sk_t, MLA_ROPE // 2), MLA_NOPE, axis=1)
        return _rope(q, cq_t, sq_t, MLA_ROPE // 2) * MLA_SPEC["scale"], kk + jnp.tile(kpe, (1, MLA_HEADS)), vv

    q_mla, k_mla, v_mla = _rowwise(
        f"l{l}_mla_prep", mla_prep,
        [_ri(q_lat, T), _ri(kv_lat, T, MLA_HEADS * LANES, 0), _ri(kv_lat, T, MLA_HEADS * MLA_V, 2), _ri(proj, T, 128, _cb("k_rope")),
         _ri(tb["c_mla_q"], T), _ri(tb["s_mla_q"], T), _ri(tb["c_mla_k"], T), _ri(tb["s_mla_k"], T)],
        [_ro(S, MLA_QW, BF16, T), _ro(S, MLA_HEADS * LANES, BF16, T), _ro(S, MLA_HEADS * MLA_V, BF16, T)], n_tiles=nt)

    att_a, lse_a = _softmax_fwd(f"l{l}_mla_fwd", q_mla, k_mla, v_mla, None, o_width=MLA_HEADS * MLA_V, **MLA_SPEC, **_mla_specs(S))
    att_b, lse_b = _softmax_fwd(f"l{l}_swa_fwd", q_swa, k_swa, v_swa, P["swa_sinks"], o_width=SWA_HEADS * SWA_HEAD_DIM, **SWA_SPEC, **_swa_specs(S))
    att_c, att_c32 = _sb_fwd(f"l{l}_sb_fwd", proj16, **SB_SPEC)
    o_a, = _mm_nn(f"l{l}_mm_oa", att_a, W["w_o_mla"], [F32], tm=1024, tn=D_MODEL)
    o_b, = _mm_nn(f"l{l}_mm_ob", att_b, W["w_o_swa"], [F32], tm=1024, tn=D_MODEL)
    o_c, = _mm_nn(f"l{l}_mm_oc", att_c, W["w_o_sb"], [F32], tm=1024, tn=D_MODEL)

    def gate_mix(gl, b, oa, ob, oc):
        gt = jax.nn.sigmoid(gl + b)
        return (gt[:, :D_MODEL] * oa + gt[:, D_MODEL:2 * D_MODEL] * ob + gt[:, 2 * D_MODEL:] * oc,)

    mixed, = _rowwise(f"l{l}_gate_mix", gate_mix, [_ri(proj, T, 3072, 0), _bi(P["b_gate"]), _ri(o_a, T), _ri(o_b, T), _ri(o_c, T)],
                      [_ro(S, D_MODEL, BF16, T)], n_tiles=nt)
    y, = _mm_nn(f"l{l}_mm_out", mixed, W["w_out"], [F32], tm=1024, tn=D_MODEL)

    def resid_norm(xv, yv, gpost, gpre):
        x1 = xv + _rms(yv, gpost)
        return x1, _rms(x1, gpre)

    x1, h2 = _rowwise(f"l{l}_resid_norm", resid_norm, [_ri(x, T), _ri(y, T), _bi(P["g_mix_post"]), _bi(P["g_mlp_pre"])],
                      [_ro(S, D_MODEL, F32, T), _ro(S, D_MODEL, BF16, T)], n_tiles=nt)
    up, u = _mm_nn(f"l{l}_mm_up", h2, W["w_up"], [F32, BF16], tm=512, tn=2048,
                   epilogue=lambda acc: (acc, jnp.square(jnp.maximum(acc, 0.0))))
    dn, = _mm_nn(f"l{l}_mm_down", u, W["w_down"], [F32], tm=512, tn=D_MODEL)
    x2, = _rowwise(f"l{l}_resid_out", lambda xv, dv, g: (xv + _rms(dv, g),), [_ri(x1, T), _ri(dn, T), _bi(P["g_mlp_post"])],
                   [_ro(S, D_MODEL, F32, T)], n_tiles=nt)
    saved = dict(x=x, h=h, proj=proj, proj16=proj16, cqn=cqn, ckvn=ckvn, q_swa=q_swa, k_swa=k_swa, v_swa=v_swa, q_mla=q_mla, k_mla=k_mla,
                 v_mla=v_mla, att_a=att_a, lse_a=lse_a, att_b=att_b, lse_b=lse_b, att_c=att_c, att_c32=att_c32, o_a=o_a, o_b=o_b, o_c=o_c,
                 mixed=mixed, y=y, x1=x1, h2=h2, up=up, u=u, dn=dn)
    return x2, saved


def _layer_bwd(l, dx2, sv, W, P, tb, T):
    S = dx2.shape[0]
    nt = S // T
    G = {}

    def post_norm_bwd(v, g, dy):
        return _rms_bwd(v, g, dy)

    d_dn, G["g_mlp_post"] = _rowwise(f"l{l}_b_post2", post_norm_bwd, [_ri(sv["dn"], T), _bi(P["g_mlp_post"]), _ri(dx2, T)],
                                    [_ro(S, D_MODEL, BF16, T)], [D_MODEL], n_tiles=nt)
    d_up, = _mm_nt(f"l{l}_b_mm_down", d_dn, W["w_down"], [BF16], tm=512, tn=2048, extras=[sv["up"]],
                   epilogue=lambda acc, upv: (acc * (2.0 * jnp.maximum(upv, 0.0)),))
    G["w_down"] = _mm_tn(f"l{l}_g_down", sv["u"], d_dn, tm=2048, tn=D_MODEL, ts=512)
    d_h2, = _mm_nt(f"l{l}_b_mm_up", d_up, W["w_up"], [F32], tm=512, tn=D_MODEL)
    G["w_up"] = _mm_tn(f"l{l}_g_up", sv["h2"], d_up, tm=D_MODEL, tn=2048, ts=512)

    def pre_norm_bwd(v, g, dy, dres):
        dx, dg = _rms_bwd(v, g, dy)
        return dres + dx, dg

    dx1, G["g_mlp_pre"] = _rowwise(f"l{l}_b_pre2", pre_norm_bwd, [_ri(sv["x1"], T), _bi(P["g_mlp_pre"]), _ri(d_h2, T), _ri(dx2, T)],
                                  [_ro(S, D_MODEL, F32, T)], [D_MODEL], n_tiles=nt)
    d_y, G["g_mix_post"] = _rowwise(f"l{l}_b_post1", post_norm_bwd, [_ri(sv["y"], T), _bi(P["g_mix_post"]), _ri(dx1, T)],
                                   [_ro(S, D_MODEL, BF16, T)], [D_MODEL], n_tiles=nt)
    d_mixed, = _mm_nt(f"l{l}_b_mm_out", d_y, W["w_out"], [F32], tm=1024, tn=D_MODEL)
    G["w_out"] = _mm_tn(f"l{l}_g_out", sv["mixed"], d_y, tm=D_MODEL, tn=D_MODEL, ts=512)

    def gate_bwd(dm, gl, b, oa, ob, oc):
        gt = jax.nn.sigmoid(gl + b)
        outs, dgl = [], []
        for k, o in enumerate((oa, ob, oc)):
            gk = gt[:, k * D_MODEL:(k + 1) * D_MODEL]
            outs.append(dm * gk)
            dgl.append(dm * o * gk * (1.0 - gk))
        dgl = jnp.concatenate(dgl, axis=1)
        return (*outs, dgl, jnp.sum(dgl, axis=0, keepdims=True))

    d_oa, d_ob, d_oc, d_gl, G["b_gate"] = _rowwise(
        f"l{l}_b_gate", gate_bwd, [_ri(d_mixed, T), _ri(sv["proj"], T, 3072, 0), _bi(P["b_gate"]), _ri(sv["o_a"], T), _ri(sv["o_b"], T), _ri(sv["o_c"], T)],
        [_ro(S, D_MODEL, BF16, T)] * 3 + [_ro(S, 3 * D_MODEL, BF16, T)], [3 * D_MODEL], n_tiles=nt)
    d_att = {}
    for br, d_o, att in (("mla", d_oa, sv["att_a"]), ("swa", d_ob, sv["att_b"]), ("sb", d_oc, sv["att_c"])):
        d_att[br], = _mm_nt(f"l{l}_b_mm_o_{br}", d_o, W["w_o_" + br], [F32], tm=1024, tn=512)
        G["w_o_" + br] = _mm_tn(f"l{l}_g_o_{br}", att, d_o, tm=512, tn=D_MODEL, ts=512)

    dq_mla, dk_mla, dv_mla = _softmax_bwd(f"l{l}_mla_bwd", sv["q_mla"], sv["k_mla"], sv["v_mla"], sv["att_a"], d_att["mla"], sv["lse_a"], None,
                                          **MLA_SPEC, **_mla_specs(S))
    dq_swa, dk_swa, dv_swa, dsink = _softmax_bwd(f"l{l}_swa_bwd", sv["q_swa"], sv["k_swa"], sv["v_swa"], sv["att_b"], d_att["swa"], sv["lse_b"],
                                                 P["swa_sinks"], **SWA_SPEC, **_swa_specs(S))
    G["swa_sinks"] = dsink[:, :SWA_G, 0].reshape(1, SWA_HEADS)
    dq_sb, dk_sb, dv_sb = _sb_bwd(f"l{l}_sb_bwd", sv["proj16"], sv["att_c32"], d_att["sb"], **SB_SPEC)

    def mla_prep_bwd(dq, dk, dvv, cq_t, sq_t, ck_t, sk_t):
        dks = dk[:, :LANES]
        for hh in range(1, MLA_HEADS):
            dks = dks + dk[:, hh * LANES:(hh + 1) * LANES]
        d_kr = _rope_t(pltpu.roll(dks, LANES - MLA_NOPE, axis=1), ck_t, sk_t, MLA_ROPE // 2)
        return _rope_t(dq, cq_t, sq_t, MLA_ROPE // 2), jnp.concatenate([dk, dvv], axis=1), d_kr

    d_q_lat, d_kv_lat, d_krope = _rowwise(
        f"l{l}_b_mla_prep", mla_prep_bwd,
        [_ri(dq_mla, T), _ri(dk_mla, T), _ri(dv_mla, T), _ri(tb["c_mla_q"], T), _ri(tb["s_mla_q"], T), _ri(tb["c_mla_k"], T), _ri(tb["s_mla_k"], T)],
        [_ro(S, MLA_QW, BF16, T), _ro(S, MLA_KVW, BF16, T), _ro(S, LANES, BF16, T)], n_tiles=nt)
    d_cqn, = _mm_nt(f"l{l}_b_mm_uq", d_q_lat, W["w_uq"], [F32], tm=1024, tn=MLA_Q_LORA)
    G["w_uq"] = _mm_tn(f"l{l}_g_uq", sv["cqn"], d_q_lat, tm=MLA_Q_LORA, tn=MLA_QW, ts=512)
    d_ckvn, = _mm_nt(f"l{l}_b_mm_ukv", d_kv_lat, W["w_ukv"], [F32], tm=1024, tn=MLA_KV_LORA)
    G["w_ukv"] = _mm_tn(f"l{l}_g_ukv", sv["ckvn"], d_kv_lat, tm=MLA_KV_LORA, tn=MLA_KVW, ts=512)

    def mix_prep_bwd(cq, ckv, gq, gkv, dcqn, dckvn, dqs, dks, dvs, cq_t, sq_t, ck_t, sk_t):
        d_cq, dgq = _rms_bwd(cq, gq, dcqn)
        d_ckv, dgkv = _rms_bwd(ckv, gkv, dckvn)
        dk2 = jnp.concatenate([dks[0], dks[1]], axis=1)
        dv2 = jnp.concatenate([dvs[0], dvs[1]], axis=1)
        return (d_cq, d_ckv, _rope_t(dqs, cq_t, sq_t, SWA_HEAD_DIM // 2), _rope_t(dk2, ck_t, sk_t, SWA_HEAD_DIM // 2), dv2, dgq, dgkv)

    kv3 = lambda a: (a.reshape(SWA_KV_HEADS, S, SWA_HEAD_DIM), (SWA_KV_HEADS, T, SWA_HEAD_DIM), lambda i: (0, i, 0))
    d_cq, d_ckv, d_qswa, d_kswa, d_vswa, G["g_q_lat"], G["g_kv_lat"] = _rowwise(
        f"l{l}_b_mix_prep", mix_prep_bwd,
        [_ri(sv["proj"], T, 256, _cb("c_q")), _ri(sv["proj"], T, 128, _cb("c_kv")), _bi(P["g_q_lat"]), _bi(P["g_kv_lat"]), _ri(d_cqn, T), _ri(d_ckvn, T),
         _ri(dq_swa, T), kv3(dk_swa), kv3(dv_swa), _ri(tb["c_swa_q"], T), _ri(tb["s_swa_q"], T), _ri(tb["c_swa_k"], T), _ri(tb["s_swa_k"], T)],
        [_ro(S, 256, BF16, T), _ro(S, 128, BF16, T), _ro(S, 512, BF16, T), _ro(S, 128, BF16, T), _ro(S, 128, BF16, T)], [256, 128], n_tiles=nt)
    pieces = dict(gates=d_gl, q_swa=d_qswa, q_sb=dq_sb, k_sb=dk_sb, v_sb=dv_sb, c_q=d_cq, c_kv=d_ckv, k_swa=d_kswa, v_swa=d_vswa, k_rope=d_krope)
    d_proj = jnp.concatenate([pieces[n].astype(BF16) for n in PERM_ORDER], axis=1)
    d_h, = _mm_nt(f"l{l}_b_mm_in", d_proj, W["w_in"], [F32], tm=512, tn=512)
    G["w_in"] = _mm_tn(f"l{l}_g_in", sv["h"], d_proj, tm=D_MODEL, tn=IN_WIDTH_P // 2, ts=512)
    dx, G["g_mix_pre"] = _rowwise(f"l{l}_b_pre1", pre_norm_bwd, [_ri(sv["x"], T), _bi(P["g_mix_pre"]), _ri(d_h, T), _ri(dx1, T)],
                                 [_ro(S, D_MODEL, F32, T)], [D_MODEL], n_tiles=nt)
    return dx, G


def _pack_rows(vecs, rows):
    flat = jnp.concatenate([v.reshape(-1) for v in vecs])
    return jnp.pad(flat, (0, rows * PACK_COLS - flat.shape[0])).reshape(rows, PACK_COLS)


def kernel(x, positions, g_mix_pre, w_in, b_gate, g_q_lat, g_kv_lat, w_uq, w_ukv, swa_sinks, w_o_mla, w_o_swa, w_o_sb, w_out, g_mix_post, g_mlp_pre, w_up, w_down, g_mlp_post, loss_target, m_g_mix_pre, m_w_in, m_b_gate, m_g_q_lat, m_g_kv_lat, m_w_uq, m_w_ukv, m_swa_sinks, m_w_o_mla, m_w_o_swa, m_w_o_sb, m_w_out, m_g_mix_post, m_g_mlp_pre, m_w_up, m_w_down, m_g_mlp_post, v_g_mix_pre, v_w_in, v_b_gate, v_g_q_lat, v_g_kv_lat, v_w_uq, v_w_ukv, v_swa_sinks, v_w_o_mla, v_w_o_swa, v_w_o_sb, v_w_out, v_g_mix_post, v_g_mlp_pre, v_w_up, v_w_down, v_g_mlp_post):
    a = dict(locals())
    S = x.shape[1]
    depth = w_in.shape[0]
    T = min(256, S)
    xs = x.reshape(S, D_MODEL)
    tb = _tables(positions)

    gathered = _exchange("gather_weights", [a[n].astype(BF16).reshape(-1, a[n].shape[2]) for n in BIG], per_peer=False)
    full = {n: _from_shards(n, g, a[n].shape) for n, g in zip(BIG, gathered)}
    layers = []
    for l in range(depth):
        W = {n: full[n][l] for n in BIG}
        W["w_in"] = _perm_w_in(W["w_in"])
        W["w_uq"] = _perm_w_uq(W["w_uq"])
        W["w_ukv"] = _perm_w_ukv(W["w_ukv"])
        P = {n: a[n][l].reshape(1, -1) for n in SMALL if n != "swa_sinks"}
        P["swa_sinks"] = a["swa_sinks"][l]
        layers.append((W, P))

    saved = []
    h = xs
    for l, (W, P) in enumerate(layers):
        h, sv = _layer_fwd(l, h, W, P, tb, T)
        saved.append(sv)

    def loss_head(yv, tv):
        err = yv - tv
        part = 0.5 * jnp.sum(jnp.mean(err * err, axis=1, keepdims=True), axis=0, keepdims=True)
        return err * (1.0 / D_MODEL), jnp.broadcast_to(part, (1, LANES))

    dh, loss_row = _rowwise("loss_head", loss_head, [_ri(h, T), _ri(loss_target.reshape(S, D_MODEL), T)], [_ro(S, D_MODEL, F32, T)], [LANES],
                            n_tiles=S // T)
    loss = lax.psum(loss_row[0, 0], ("x", "y", "c"))

    grads = [None] * depth
    for l in reversed(range(depth)):
        W, P = layers[l]
        dh, grads[l] = _layer_bwd(l, dh, saved[l], W, P, tb, T)
    grad_x = dh.reshape(x.shape)

    unperm = dict(w_in=_unperm_w_in, w_uq=_unperm_w_uq, w_ukv=_unperm_w_ukv)
    send = [_to_shards(n, jnp.stack([unperm.get(n, lambda t: t)(grads[l][n]).astype(BF16) for l in range(depth)])) for n in BIG]
    recv = _exchange("scatter_grads", send, per_peer=True)
    out = {}
    for n, parts in zip(BIG, recv):
        shp = a[n].shape
        rows, cols = shp[0] * shp[1], shp[2]
        res = _adamw("adamw_" + n, parts, a[n].reshape(rows, cols), a["m_" + n].reshape(rows, cols), a["v_" + n].reshape(rows, cols),
                     tile=min(256, rows))
        out[n] = [r.reshape(shp) for r in res]

    small_total = sum(a[n].size for n in SMALL)
    small_rows = -(-small_total // (8 * PACK_COLS)) * 8
    sg = _pack_rows([jnp.stack([grads[l][n].reshape(-1) for l in range(depth)]) for n in SMALL], small_rows)
    sg_all, = _exchange("gather_small_grads", [sg], per_peer=False)
    res = _adamw("adamw_small", sg_all, _pack_rows([a[n] for n in SMALL], small_rows), _pack_rows([a["m_" + n] for n in SMALL], small_rows),
                 _pack_rows([a["v_" + n] for n in SMALL], small_rows), tile=small_rows)
    off = 0
    for n in SMALL:
        cnt = a[n].size
        out[n] = [r.reshape(-1)[off:off + cnt].reshape(a[n].shape) for r in res]
        off += cnt

    order = ("g_mix_pre", "w_in", "b_gate", "g_q_lat", "g_kv_lat", "w_uq", "w_ukv", "swa_sinks", "w_o_mla", "w_o_swa", "w_o_sb", "w_out",
             "g_mix_post", "g_mlp_pre", "w_up", "w_down", "g_mlp_post")
    return (loss, grad_x, *[out[n][0] for n in order], *[out[n][1] for n in order], *[out[n][2] for n in order], *[out[n][3] for n in order])
```

```python
import functools

import jax
import jax.numpy as jnp
from jax import lax
from jax.experimental import pallas as pl
from jax.experimental.pallas import tpu as pltpu

F32, BF16 = jnp.float32, jnp.bfloat16

D_MODEL = 1024
DEPTH = 4
MLA_HEADS, MLA_Q_LORA, MLA_KV_LORA, MLA_NOPE, MLA_ROPE, MLA_V = 8, 256, 128, 64, 32, 64
SWA_HEADS, SWA_KV_HEADS, SWA_HEAD_DIM, SWA_WINDOW = 8, 2, 64, 128
SB_HEADS, SB_HEAD_DIM = 8, 64
D_FF = 4 * D_MODEL
ROPE_THETA = 10000.0
EPS = 1e-6
N_DEV = 8
ADAM_LR, ADAM_B1, ADAM_B2, ADAM_EPS, ADAM_WD, ADAM_STEP = 0.001, 0.9, 0.999, 1e-08, 0.01, 10

LANES = 128
VMEM_LIMIT_MAX = 60 * 1024 * 1024
VMEM_LIMIT_MIN = 32 * 1024 * 1024

ORIG_COLS = dict(c_q=(0, 256), c_kv=(256, 128), k_rope=(384, 32), q_swa=(416, 512), k_swa=(928, 128), v_swa=(1056, 128),
                 q_sb=(1184, 512), k_sb=(1696, 512), v_sb=(2208, 512), gates=(2720, 3072))
IN_WIDTH = 5792
PERM_ORDER = ("gates", "q_swa", "q_sb", "k_sb", "v_sb", "c_q", "c_kv", "k_swa", "v_swa", "k_rope")
PERM_WIDTH = dict(gates=3072, q_swa=512, q_sb=512, k_sb=512, v_sb=512, c_q=256, c_kv=128, k_swa=128, v_swa=128, k_rope=128)
PERM_OFF = {}
_o = 0
for _n in PERM_ORDER:
    PERM_OFF[_n] = _o
    _o += PERM_WIDTH[_n]
IN_WIDTH_P = _o
MLA_QW = MLA_HEADS * LANES
MLA_KVW = MLA_HEADS * LANES + MLA_HEADS * MLA_V

NT = (((1,), (1,)), ((), ()))
TN = (((0,), (0,)), ((), ()))
NEG = -1e30


def _cparams(sem, block_bytes):
    limit = int(min(VMEM_LIMIT_MAX, max(VMEM_LIMIT_MIN, 2 * block_bytes + (16 << 20))))
    return pltpu.CompilerParams(dimension_semantics=sem, vmem_limit_bytes=limit)


def _nbytes(shape, dtype):
    n = 1
    for s in shape:
        n *= s
    return n * jnp.dtype(dtype).itemsize


def _ri(arr, tile, width=None, cb=0):
    width = arr.shape[1] if width is None else width
    return (arr, (tile, width), lambda i, cb=cb: (i, cb))


def _bi(arr):
    return (arr, arr.shape, lambda i: (0, 0))


def _ro(rows, width, dtype, tile):
    return ((rows, width), dtype, (tile, width), lambda i: (i, 0))


def _rowwise(name, fn, ins, outs, reds=(), *, n_tiles):
    n_in, n_out = len(ins), len(outs)

    def body(*refs):
        vals = fn(*[r[...] for r in refs[:n_in]])
        for r, v in zip(refs[n_in:n_in + n_out], vals[:n_out]):
            if isinstance(v, (list, tuple)):
                for j, vj in enumerate(v):
                    r[j] = vj.astype(r.dtype)
            else:
                r[...] = v.astype(r.dtype)
        if reds:
            @pl.when(pl.program_id(0) == 0)
            def _():
                for r in refs[n_in + n_out:]:
                    r[...] = jnp.zeros_like(r)
            for r, v in zip(refs[n_in + n_out:], vals[n_out:]):
                r[...] += v

    block_bytes = sum(_nbytes(b, a.dtype) for a, b, _ in ins) + sum(_nbytes(b, d) for _, d, b, _ in outs)
    res = pl.pallas_call(
        body, name=name, grid=(n_tiles,),
        in_specs=[pl.BlockSpec(b, m) for _, b, m in ins],
        out_specs=[pl.BlockSpec(b, m) for _, _, b, m in outs] + [pl.BlockSpec((1, w), lambda i: (0, 0)) for w in reds],
        out_shape=[jax.ShapeDtypeStruct(s, d) for s, d, _, _ in outs] + [jax.ShapeDtypeStruct((1, w), F32) for w in reds],
        compiler_params=_cparams(("arbitrary",) if reds else ("parallel",), block_bytes),
    )(*[a for a, _, _ in ins])
    return res


def _rms(x, g):
    r = lax.rsqrt(jnp.mean(x * x, axis=1, keepdims=True) + EPS)
    return x * r * g


def _rms_bwd(x, g, dy):
    r = lax.rsqrt(jnp.mean(x * x, axis=1, keepdims=True) + EPS)
    xn = x * r
    dxn = dy * g
    dx = r * (dxn - xn * jnp.mean(dxn * xn, axis=1, keepdims=True))
    return dx, jnp.sum(dy * xn, axis=0, keepdims=True)


def _swap_halves(x, half):
    n = x.shape[1]
    lane = lax.broadcasted_iota(jnp.int32, x.shape, 1)
    first = (lane % (2 * half)) < half
    return jnp.where(first, pltpu.roll(x, n - half, axis=1), pltpu.roll(x, half, axis=1))


def _rope(x, c, sg, half):
    return x * c + _swap_halves(x, half) * sg


def _rope_t(dy, c, sg, half):
    return dy * c - _swap_halves(dy, half) * sg


def _mm_nn(name, a, b, outs, *, tm, tn, extras=(), epilogue=None):
    M, K = a.shape
    N = b.shape[1]
    tm = min(tm, M)
    n_e = len(extras)

    def body(*refs):
        a_ref, b_ref = refs[:2]
        acc = jnp.dot(a_ref[...].astype(BF16), b_ref[...].astype(BF16), preferred_element_type=F32)
        vals = (acc,) * len(outs) if epilogue is None else epilogue(acc, *[r[...] for r in refs[2:2 + n_e]])
        for r, v in zip(refs[2 + n_e:], vals):
            r[...] = v.astype(r.dtype)

    block_bytes = (_nbytes((tm, K), a.dtype) + _nbytes((K, tn), b.dtype) + sum(_nbytes((tm, tn), e.dtype) for e in extras)
                   + sum(_nbytes((tm, tn), d) for d in outs) + _nbytes((tm, tn), F32))
    return pl.pallas_call(
        body, name=name, grid=(N // tn, M // tm),
        in_specs=[pl.BlockSpec((tm, K), lambda j, i: (i, 0)), pl.BlockSpec((K, tn), lambda j, i: (0, j))]
        + [pl.BlockSpec((tm, tn), lambda j, i: (i, j)) for _ in extras],
        out_specs=[pl.BlockSpec((tm, tn), lambda j, i: (i, j)) for _ in outs],
        out_shape=[jax.ShapeDtypeStruct((M, N), d) for d in outs],
        compiler_params=_cparams(("parallel", "parallel"), block_bytes),
    )(a, b, *extras)


def _mm_nt(name, a, b, outs, *, tm, tn, extras=(), epilogue=None):
    M, N = a.shape
    K = b.shape[0]
    tm = min(tm, M)
    n_e = len(extras)

    def body(*refs):
        a_ref, b_ref = refs[:2]
        acc = lax.dot_general(a_ref[...].astype(BF16), b_ref[...].astype(BF16), NT, preferred_element_type=F32)
        vals = (acc,) * len(outs) if epilogue is None else epilogue(acc, *[r[...] for r in refs[2:2 + n_e]])
        for r, v in zip(refs[2 + n_e:], vals):
            r[...] = v.astype(r.dtype)

    block_bytes = (_nbytes((tm, N), a.dtype) + _nbytes((tn, N), b.dtype) + sum(_nbytes((tm, tn), e.dtype) for e in extras)
                   + sum(_nbytes((tm, tn), d) for d in outs) + _nbytes((tm, tn), F32))
    return pl.pallas_call(
        body, name=name, grid=(K // tn, M // tm),
        in_specs=[pl.BlockSpec((tm, N), lambda j, i: (i, 0)), pl.BlockSpec((tn, N), lambda j, i: (j, 0))]
        + [pl.BlockSpec((tm, tn), lambda j, i: (i, j)) for _ in extras],
        out_specs=[pl.BlockSpec((tm, tn), lambda j, i: (i, j)) for _ in outs],
        out_shape=[jax.ShapeDtypeStruct((M, K), d) for d in outs],
        compiler_params=_cparams(("parallel", "parallel"), block_bytes),
    )(a, b, *extras)


def _mm_tn(name, a, b, *, tm, tn, ts):
    S, K = a.shape
    N = b.shape[1]
    ts = min(ts, S)

    def body(a_ref, b_ref, o_ref):
        @pl.when(pl.program_id(2) == 0)
        def _():
            o_ref[...] = jnp.zeros_like(o_ref)
        o_ref[...] += lax.dot_general(a_ref[...].astype(BF16), b_ref[...].astype(BF16), TN, preferred_element_type=F32)

    block_bytes = _nbytes((ts, tm), a.dtype) + _nbytes((ts, tn), b.dtype) + 2 * _nbytes((tm, tn), F32)
    return pl.pallas_call(
        body, name=name, grid=(K // tm, N // tn, S // ts),
        in_specs=[pl.BlockSpec((ts, tm), lambda i, j, s: (s, i)), pl.BlockSpec((ts, tn), lambda i, j, s: (s, j))],
        out_specs=pl.BlockSpec((tm, tn), lambda i, j, s: (i, j)),
        out_shape=jax.ShapeDtypeStruct((K, N), F32),
        compiler_params=_cparams(("parallel", "parallel", "arbitrary"), block_bytes),
    )(a, b)


def _lane_pack(cols, rows):
    lane = lax.broadcasted_iota(jnp.int32, (rows, LANES), 1)
    val = jnp.zeros((rows, LANES), F32)
    for h, c in enumerate(cols):
        val = jnp.where(lane == h, c, val)
    return val


def _mask(kb, row, tk, window):
    col = kb * tk + lax.broadcasted_iota(jnp.int32, (1, tk), 1)
    ok = col <= row
    if window:
        ok = ok & ((row - col) < window)
    return ok


def _kb_range(i, tq, tk, window):
    end = ((i + 1) * tq + tk - 1) // tk
    if window:
        lo = jnp.maximum(i * tq - (window - 1), 0) // tk
        return lo, lo, end
    return 0, (i * tq) // tk, end


def _softmax_fwd(name, q, k, v, sinks, *, groups, heads, q_spec, k_spec, v_spec, o_width, tq, tk, scale, window, q_scaled, ride=None):
    S = q.shape[0]
    nq = S // tq
    nh = len(heads)
    has_sink = sinks is not None
    dv = heads[0][2].stop - heads[0][2].start

    def body(*refs):
        if has_sink:
            sink_ref, q_ref, k_ref, v_ref, o_ref, lse_ref = refs
        else:
            q_ref, k_ref, v_ref, o_ref, lse_ref = refs
        g, i = pl.program_id(0), pl.program_id(1)
        q_all = q_ref[...]
        row = i * tq + lax.broadcasted_iota(jnp.int32, (tq, 1), 0)
        lo, mid, hi = _kb_range(i, tq, tk, window)
        qhs = [q_all[:, qs] if q_scaled else q_all[:, qs] * scale for qs, _, _ in heads]

        def blk(kb, carry, masked):
            r0 = pl.multiple_of(kb * tk, tk)
            k_all = k_ref[pl.ds(r0, tk), :]
            v_all = v_ref[pl.ds(r0, tk), :]
            if masked:
                ok = _mask(kb, row, tk, window)
            new = []
            for qh, (_, ks, vs), (m, l, acc) in zip(qhs, heads, carry):
                s = lax.dot_general(qh, k_all[:, ks], NT, preferred_element_type=F32)
                if masked:
                    s = jnp.where(ok, s, NEG)
                m_new = jnp.maximum(m, jnp.max(s, axis=1, keepdims=True))
                alpha = jnp.exp(m - m_new)
                p = jnp.exp(s - m_new)
                if masked:
                    p = jnp.where(ok, p, 0.0)
                l = alpha * l + jnp.sum(p, axis=1, keepdims=True)
                acc = alpha * acc + jnp.dot(p.astype(BF16), v_all[:, vs], preferred_element_type=F32)
                new.append((m_new, l, acc))
            return tuple(new)

        carry = []
        for h in range(nh):
            if has_sink:
                m0 = jnp.full((tq, 1), sink_ref[g * nh + h], F32)
                l0 = jnp.ones((tq, 1), F32)
            else:
                m0 = jnp.full((tq, 1), NEG, F32)
                l0 = jnp.zeros((tq, 1), F32)
            carry.append((m0, l0, jnp.zeros((tq, dv), F32)))
        carry = lax.fori_loop(lo, mid, functools.partial(blk, masked=False), tuple(carry))
        carry = lax.fori_loop(mid, hi, functools.partial(blk, masked=True), carry)
        o_ref[...] = jnp.concatenate([acc / l for _, l, acc in carry], axis=1).astype(o_ref.dtype)
        lse_ref[0] = _lane_pack([m + jnp.log(l) for m, l, _ in carry], tq)

    in_specs = [pl.BlockSpec(*q_spec), pl.BlockSpec(*k_spec), pl.BlockSpec(*v_spec)]
    args = [q, k, v]
    if has_sink:
        in_specs = [pl.BlockSpec(memory_space=pltpu.SMEM)] + in_specs
        args = [sinks] + args
    wo = nh * dv
    block_bytes = _nbytes(q_spec[0], q.dtype) + _nbytes(k_spec[0], k.dtype) + _nbytes(v_spec[0], v.dtype) + 4 * tq * (wo + LANES)
    return _pallas(
        body, name=name, grid=(groups, nq), in_specs=in_specs,
        out_specs=[pl.BlockSpec((tq, wo), lambda g, i: (i, g)), pl.BlockSpec((1, tq, LANES), lambda g, i: (g, i, 0))],
        out_shape=[jax.ShapeDtypeStruct((S, o_width), BF16), jax.ShapeDtypeStruct((groups, S, LANES), F32)],
        args=args, sem=("parallel", "arbitrary"), block_bytes=block_bytes, ride=ride)


def _softmax_bwd(name, q, k, v, o, do, lse, sinks, *, groups, heads, q_spec, k_spec, v_spec, tq, tk, scale, window, q_scaled, ride=None):
    S = q.shape[0]
    nq = S // tq
    nh = len(heads)
    has_sink = sinks is not None
    dv = heads[0][2].stop - heads[0][2].start
    wo = nh * dv

    def body(*refs):
        if has_sink:
            sink_ref, q_ref, k_ref, v_ref, o_ref, do_ref, lse_ref, dq_ref, dk_ref, dv_ref, dsink_ref = refs
        else:
            q_ref, k_ref, v_ref, o_ref, do_ref, lse_ref, dq_ref, dk_ref, dv_ref = refs
        g, i = pl.program_id(0), pl.program_id(1)

        @pl.when(i == 0)
        def _():
            dk_ref[...] = jnp.zeros_like(dk_ref)
            dv_ref[...] = jnp.zeros_like(dv_ref)
            if has_sink:
                dsink_ref[...] = jnp.zeros_like(dsink_ref)

        q_all = q_ref[...]
        o_all = o_ref[...].astype(F32)
        do_all = do_ref[...].astype(F32)
        lse_all = lse_ref[0]
        row = i * tq + lax.broadcasted_iota(jnp.int32, (tq, 1), 0)
        lo, mid, hi = _kb_range(i, tq, tk, window)
        per_head = []
        for h, (qs, ks, vs) in enumerate(heads):
            osl = slice(h * dv, (h + 1) * dv)
            doh = do_all[:, osl]
            delta = jnp.sum(doh * o_all[:, osl], axis=1, keepdims=True)
            qh = q_all[:, qs] if q_scaled else q_all[:, qs] * scale
            per_head.append((qh, doh.astype(BF16), delta, lse_all[:, h:h + 1], ks, vs))

        def blk(kb, dqs, masked):
            r0 = pl.multiple_of(kb * tk, tk)
            k_all = k_ref[pl.ds(r0, tk), :]
            v_all = v_ref[pl.ds(r0, tk), :]
            if masked:
                ok = _mask(kb, row, tk, window)
            dk_parts, dv_parts, new_dqs = {}, {}, []
            for (qh, doh, delta, lse_h, ks, vs), dq in zip(per_head, dqs):
                kk, vv = k_all[:, ks], v_all[:, vs]
                s = lax.dot_general(qh, kk, NT, preferred_element_type=F32)
                p = jnp.exp(s - lse_h)
                if masked:
                    p = jnp.where(ok, p, 0.0)
                dp = lax.dot_general(doh, vv, NT, preferred_element_type=F32)
                ds = p * (dp - delta)
                new_dqs.append(dq + jnp.dot(ds.astype(BF16), kk, preferred_element_type=F32))
                dk_c = jnp.dot(ds.T.astype(BF16), qh, preferred_element_type=F32)
                dv_c = jnp.dot(p.T.astype(BF16), doh, preferred_element_type=F32)
                dk_parts[ks.start] = dk_parts[ks.start] + dk_c if ks.start in dk_parts else dk_c
                dv_parts[vs.start] = dv_parts[vs.start] + dv_c if vs.start in dv_parts else dv_c
            dk_ref[pl.ds(r0, tk), :] += jnp.concatenate([dk_parts[s0] for s0 in sorted(dk_parts)], axis=1)
            dv_ref[pl.ds(r0, tk), :] += jnp.concatenate([dv_parts[s0] for s0 in sorted(dv_parts)], axis=1)
            return tuple(new_dqs)

        dqs = tuple(jnp.zeros((tq, qs.stop - qs.start), F32) for qs, _, _ in heads)
        dqs = lax.fori_loop(lo, mid, functools.partial(blk, masked=False), dqs)
        dqs = lax.fori_loop(mid, hi, functools.partial(blk, masked=True), dqs)
        dq_ref[...] = jnp.concatenate([dq * scale for dq in dqs], axis=1)
        if has_sink:
            for h, (_, _, delta, lse_h, _, _) in enumerate(per_head):
                p_sink = jnp.exp(sink_ref[g * nh + h] - lse_h)
                dsink_ref[0, h:h + 1, :] += jnp.broadcast_to(-jnp.sum(p_sink * delta, axis=0, keepdims=True), (1, LANES))

    in_specs = [pl.BlockSpec(*q_spec), pl.BlockSpec(*k_spec), pl.BlockSpec(*v_spec),
                pl.BlockSpec((tq, wo), lambda g, i: (i, g)), pl.BlockSpec((tq, wo), lambda g, i: (i, g)),
                pl.BlockSpec((1, tq, LANES), lambda g, i: (g, i, 0))]
    args = [q, k, v, o, do, lse]
    out_specs = [pl.BlockSpec(*q_spec), pl.BlockSpec(*k_spec), pl.BlockSpec(*v_spec)]
    out_shape = [jax.ShapeDtypeStruct(q.shape, F32), jax.ShapeDtypeStruct(k.shape, F32), jax.ShapeDtypeStruct(v.shape, F32)]
    if has_sink:
        in_specs = [pl.BlockSpec(memory_space=pltpu.SMEM)] + in_specs
        args = [sinks] + args
        out_specs.append(pl.BlockSpec((1, 8, LANES), lambda g, i: (g, 0, 0)))
        out_shape.append(jax.ShapeDtypeStruct((groups, 8, LANES), F32))
    block_bytes = (_nbytes(q_spec[0], q.dtype) + _nbytes(k_spec[0], k.dtype) + _nbytes(v_spec[0], v.dtype) + 6 * tq * wo + 4 * tq * LANES
                   + _nbytes(q_spec[0], F32) + _nbytes(k_spec[0], F32) + _nbytes(v_spec[0], F32))
    return _pallas(body, name=name, grid=(groups, nq), in_specs=in_specs, out_specs=out_specs, out_shape=out_shape, args=args,
                   sem=("parallel", "arbitrary"), block_bytes=block_bytes, ride=ride)


def _swa_window(i, tq, row):
    start = pl.multiple_of(jnp.maximum(i * tq - SWA_WINDOW, 0), SWA_WINDOW)
    col = start + lax.broadcasted_iota(jnp.int32, (1, tq + SWA_WINDOW), 1)
    return start, (col <= row) & ((row - col) < SWA_WINDOW)


def _swa_fwd(name, q, k, v, sinks, *, tq):
    S = q.shape[0]
    hd, span = SWA_HEAD_DIM, tq + SWA_WINDOW
    scale = hd ** -0.5

    def body(sink_ref, q_ref, k_ref, v_ref, o_ref, lse_ref):
        g, i = pl.program_id(0), pl.program_id(1)
        row = i * tq + lax.broadcasted_iota(jnp.int32, (tq, 1), 0)
        start, ok = _swa_window(i, tq, row)
        kk, vv = k_ref[pl.ds(start, span), :], v_ref[pl.ds(start, span), :]
        q_all = q_ref[...]
        outs, lses = [], []
        for h in range(SWA_G):
            s = lax.dot_general(q_all[:, h * hd:(h + 1) * hd] * scale, kk, NT, preferred_element_type=F32)
            s = jnp.where(ok, s, NEG)
            sink = sink_ref[g * SWA_G + h]
            m = jnp.maximum(jnp.max(s, axis=1, keepdims=True), sink)
            p = jnp.exp(s - m)
            l = jnp.sum(p, axis=1, keepdims=True) + jnp.exp(sink - m)
            outs.append(jnp.dot(p.astype(BF16), vv, preferred_element_type=F32) / l)
            lses.append(m + jnp.log(l))
        o_ref[...] = jnp.concatenate(outs, axis=1).astype(o_ref.dtype)
        lse_ref[0] = _lane_pack(lses, tq)

    wq = SWA_G * hd
    block_bytes = 2 * tq * wq * 2 + 2 * 2 * S * hd + 4 * tq * LANES
    return pl.pallas_call(
        body, name=name, grid=(SWA_KV_HEADS, S // tq),
        in_specs=[pl.BlockSpec(memory_space=pltpu.SMEM), pl.BlockSpec((tq, wq), lambda g, i: (i, g)),
                  pl.BlockSpec((S, hd), lambda g, i: (g, 0)), pl.BlockSpec((S, hd), lambda g, i: (g, 0))],
        out_specs=[pl.BlockSpec((tq, wq), lambda g, i: (i, g)), pl.BlockSpec((1, tq, LANES), lambda g, i: (g, i, 0))],
        out_shape=[jax.ShapeDtypeStruct((S, SWA_HEADS * hd), BF16), jax.ShapeDtypeStruct((SWA_KV_HEADS, S, LANES), F32)],
        compiler_params=_cparams(("parallel", "parallel"), block_bytes),
    )(sinks, q, k, v)


def _swa_bwd(name, q, k, v, o, do, lse, sinks, *, tq):
    S = q.shape[0]
    hd, span = SWA_HEAD_DIM, tq + SWA_WINDOW
    scale = hd ** -0.5

    def body(sink_ref, q_ref, k_ref, v_ref, o_ref, do_ref, lse_ref, dq_ref, dk_ref, dv_ref, dsink_ref):
        g, i = pl.program_id(0), pl.program_id(1)

        @pl.when(i == 0)
        def _():
            dk_ref[...] = jnp.zeros_like(dk_ref)
            dv_ref[...] = jnp.zeros_like(dv_ref)
            dsink_ref[...] = jnp.zeros_like(dsink_ref)

        row = i * tq + lax.broadcasted_iota(jnp.int32, (tq, 1), 0)
        start, ok = _swa_window(i, tq, row)
        kk, vv = k_ref[pl.ds(start, span), :], v_ref[pl.ds(start, span), :]
        q_all, o_all, do_all, lse_all = q_ref[...], o_ref[...].astype(F32), do_ref[...].astype(F32), lse_ref[0]
        dqs, dk_c, dv_c = [], None, None
        for h in range(SWA_G):
            sl = slice(h * hd, (h + 1) * hd)
            qh = q_all[:, sl] * scale
            doh = do_all[:, sl]
            delta = jnp.sum(doh * o_all[:, sl], axis=1, keepdims=True)
            lse_h = lse_all[:, h:h + 1]
            doh = doh.astype(BF16)
            s = jnp.where(ok, lax.dot_general(qh, kk, NT, preferred_element_type=F32), NEG)
            p = jnp.exp(s - lse_h)
            ds = p * (lax.dot_general(doh, vv, NT, preferred_element_type=F32) - delta)
            dqs.append(jnp.dot(ds.astype(BF16), kk, preferred_element_type=F32) * scale)
            dk_h = jnp.dot(ds.T.astype(BF16), qh, preferred_element_type=F32)
            dv_h = jnp.dot(p.T.astype(BF16), doh, preferred_element_type=F32)
            dk_c = dk_h if dk_c is None else dk_c + dk_h
            dv_c = dv_h if dv_c is None else dv_c + dv_h
            p_sink = jnp.exp(sink_ref[g * SWA_G + h] - lse_h)
            dsink_ref[0, h:h + 1, :] += jnp.broadcast_to(-jnp.sum(p_sink * delta, axis=0, keepdims=True), (1, LANES))
        dq_ref[...] = jnp.concatenate(dqs, axis=1)
        dk_ref[pl.ds(start, span), :] += dk_c
        dv_ref[pl.ds(start, span), :] += dv_c

    wq = SWA_G * hd
    qb = pl.BlockSpec((tq, wq), lambda g, i: (i, g))
    kb = pl.BlockSpec((S, hd), lambda g, i: (g, 0))
    block_bytes = 2 * tq * wq * (2 + 2 + 4 + 4) + 2 * S * hd * (2 + 2 + 4 + 4) + 4 * tq * LANES
    return pl.pallas_call(
        body, name=name, grid=(SWA_KV_HEADS, S // tq),
        in_specs=[pl.BlockSpec(memory_space=pltpu.SMEM), qb, kb, kb, qb, qb, pl.BlockSpec((1, tq, LANES), lambda g, i: (g, i, 0))],
        out_specs=[qb, kb, kb, pl.BlockSpec((1, 8, LANES), lambda g, i: (g, 0, 0))],
        out_shape=[jax.ShapeDtypeStruct(q.shape, F32), jax.ShapeDtypeStruct(k.shape, F32), jax.ShapeDtypeStruct(v.shape, F32),
                   jax.ShapeDtypeStruct((SWA_KV_HEADS, 8, LANES), F32)],
        compiler_params=_cparams(("parallel", "arbitrary"), block_bytes),
    )(sinks, q, k, v, o, do, lse)


def _split_dot(x, u2):
    hi = x.astype(BF16)
    lo = (x - hi.astype(F32)).astype(BF16)
    return jnp.dot(jnp.concatenate([hi, lo], axis=1), u2, preferred_element_type=F32)


def _suffix_ones():
    r = lax.broadcasted_iota(jnp.int32, (2 * LANES, 2 * LANES), 0) % LANES
    c = lax.broadcasted_iota(jnp.int32, (2 * LANES, 2 * LANES), 1)
    return ((r > c) | (c >= LANES)).astype(BF16)


def _suffix_scan(x, uo, run):
    nc = x.shape[1] // LANES
    out = [None] * nc
    for c in reversed(range(nc)):
        st = _split_dot(x[:, c * LANES:(c + 1) * LANES], uo)
        out[c] = st[:, :LANES] + run
        run = run + st[:, LANES:]
    return (out[0] if nc == 1 else jnp.concatenate(out, axis=1)), run


SB_DEAD_LOG = -110.0


def _sb_logits(qh, kk, kb, row, tq, masked):
    z = lax.dot_general(qh, kk, NT, preferred_element_type=F32)
    nz = -z
    l = jnp.minimum(nz, 0.0) - jnp.log(1.0 + jnp.exp(jnp.minimum(z, nz)))
    ok = None
    if masked:
        col = kb * tq + lax.broadcasted_iota(jnp.int32, (1, tq), 1)
        ok = col < row
        l = jnp.where(ok, l, 0.0)
    return z, l, ok


def _sb_walk(blk, carry, i):
    carry = blk(0, carry, True, i)

    def live(c):
        t, heads = c
        top = jnp.max(heads[0][0])
        for h in heads[1:]:
            top = jnp.maximum(top, jnp.max(h[0]))
        return jnp.logical_and(t < i, top > SB_DEAD_LOG)

    def step(c):
        t, heads = c
        return t + 1, blk(t, heads, False, i - 1)

    return lax.while_loop(live, step, (jnp.int32(0), carry))[1]


def _sb_fwd(name, pb, *, q_cb, k_cb, v_cb, tq, scale):
    S = pb.shape[0]
    nq = S // tq
    hd = SB_HEAD_DIM
    groups = SB_HEADS * hd // LANES
    nh = LANES // hd

    def body(q_ref, k_ref, v_ref, o16_ref, o32_ref):
        i = pl.program_id(1)
        q_all = q_ref[...]
        row = i * tq + lax.broadcasted_iota(jnp.int32, (tq, 1), 0)
        uo = _suffix_ones()
        sls = [slice(h * hd, (h + 1) * hd) for h in range(nh)]
        qhs = [q_all[:, sl] * scale for sl in sls]

        def blk(t, carry, masked, base):
            kb = base - t
            r0 = pl.multiple_of(kb * tq, tq)
            k_all = k_ref[pl.ds(r0, tq), :]
            v_all = v_ref[pl.ds(r0, tq), :]
            new = []
            for qh, sl, (run_l, acc) in zip(qhs, sls, carry):
                z, l, ok = _sb_logits(qh, k_all[:, sl], kb, row, tq, masked)
                tail, run_l = _suffix_scan(l, uo, run_l)
                e = z + l + tail
                if masked:
                    e = jnp.where(ok, e, NEG)
                new.append((run_l, acc + jnp.dot(jnp.exp(e).astype(BF16), v_all[:, sl], preferred_element_type=F32)))
            return tuple(new)

        carry = tuple((jnp.zeros((tq, LANES), F32), jnp.zeros((tq, hd), F32)) for _ in range(nh))
        carry = _sb_walk(blk, carry, i)
        o = jnp.concatenate([acc for _, acc in carry], axis=1)
        o16_ref[...] = o.astype(BF16)
        o32_ref[...] = o

    block_bytes = 2 * tq * LANES + 2 * 2 * S * LANES + 6 * tq * LANES
    return pl.pallas_call(
        body, name=name, grid=(groups, nq),
        in_specs=[pl.BlockSpec((tq, LANES), lambda g, i: (i, q_cb + g)), pl.BlockSpec((S, LANES), lambda g, i: (0, k_cb + g)),
                  pl.BlockSpec((S, LANES), lambda g, i: (0, v_cb + g))],
        out_specs=[pl.BlockSpec((tq, LANES), lambda g, i: (i, g)), pl.BlockSpec((tq, LANES), lambda g, i: (i, g))],
        out_shape=[jax.ShapeDtypeStruct((S, groups * LANES), BF16), jax.ShapeDtypeStruct((S, groups * LANES), F32)],
        compiler_params=_cparams(("parallel", "arbitrary"), block_bytes),
    )(pb, pb, pb)


def _sb_bwd(name, pb, o32, do, *, q_cb, k_cb, v_cb, tq, scale):
    S = pb.shape[0]
    nq = S // tq
    hd = SB_HEAD_DIM
    groups = SB_HEADS * hd // LANES
    nh = LANES // hd

    def body(q_ref, k_ref, v_ref, o_ref, do_ref, dq_ref, dk_ref, dv_ref):
        i = pl.program_id(1)

        @pl.when(i == 0)
        def _():
            dk_ref[...] = jnp.zeros_like(dk_ref)
            dv_ref[...] = jnp.zeros_like(dv_ref)

        q_all = q_ref[...]
        o_all = o_ref[...]
        do_all = do_ref[...].astype(F32)
        row = i * tq + lax.broadcasted_iota(jnp.int32, (tq, 1), 0)
        uo = _suffix_ones()
        per_head = []
        for h in range(nh):
            sl = slice(h * hd, (h + 1) * hd)
            doh = do_all[:, sl].astype(BF16)
            total = jnp.sum(doh.astype(F32) * o_all[:, sl], axis=1, keepdims=True)
            per_head.append((q_all[:, sl] * scale, doh, jnp.broadcast_to(total, (tq, LANES)), sl))

        def blk(t, carry, masked, base):
            kb = base - t
            r0 = pl.multiple_of(kb * tq, tq)
            k_all = k_ref[pl.ds(r0, tq), :]
            v_all = v_ref[pl.ds(r0, tq), :]
            new, dk_c, dv_c = [], [], []
            for (qh, doh, total, sl), (run_l, run_g, dq) in zip(per_head, carry):
                kk, vv = k_all[:, sl], v_all[:, sl]
                z, l, ok = _sb_logits(qh, kk, kb, row, tq, masked)
                tail, run_l = _suffix_scan(l, uo, run_l)
                e = z + l
                beta = jnp.exp(e)
                e = e + tail
                if masked:
                    e = jnp.where(ok, e, NEG)
                a = jnp.exp(e).astype(BF16).astype(F32)
                gr = lax.dot_general(doh, vv, NT, preferred_element_type=F32) * a
                right, run_g = _suffix_scan(gr, uo, run_g)
                nc = tq // LANES
                prefix = (total if nc == 1 else jnp.tile(total, (1, nc))) - right
                dz = gr - beta * prefix
                if masked:
                    dz = jnp.where(ok, dz, 0.0)
                new.append((run_l, run_g, dq + jnp.dot(dz.astype(BF16), kk, preferred_element_type=F32)))
                dk_c.append(jnp.dot(dz.T.astype(BF16), qh, preferred_element_type=F32))
                dv_c.append(jnp.dot(a.T.astype(BF16), doh, preferred_element_type=F32))
            dk_ref[pl.ds(r0, tq), :] += jnp.concatenate(dk_c, axis=1)
            dv_ref[pl.ds(r0, tq), :] += jnp.concatenate(dv_c, axis=1)
            return tuple(new)

        zc = jnp.zeros((tq, LANES), F32)
        carry = tuple((zc, zc, jnp.zeros((tq, hd), F32)) for _ in range(nh))
        carry = _sb_walk(blk, carry, i)
        dq_ref[...] = jnp.concatenate([c[2] * scale for c in carry], axis=1)

    W = groups * LANES
    block_bytes = 2 * tq * LANES + 2 * 2 * S * LANES + 3 * 4 * tq * LANES + 2 * 4 * S * LANES
    return pl.pallas_call(
        body, name=name, grid=(groups, nq),
        in_specs=[pl.BlockSpec((tq, LANES), lambda g, i: (i, q_cb + g)), pl.BlockSpec((S, LANES), lambda g, i: (0, k_cb + g)),
                  pl.BlockSpec((S, LANES), lambda g, i: (0, v_cb + g)), pl.BlockSpec((tq, LANES), lambda g, i: (i, g)),
                  pl.BlockSpec((tq, LANES), lambda g, i: (i, g))],
        out_specs=[pl.BlockSpec((tq, LANES), lambda g, i: (i, g)), pl.BlockSpec((S, LANES), lambda g, i: (0, g)),
                   pl.BlockSpec((S, LANES), lambda g, i: (0, g))],
        out_shape=[jax.ShapeDtypeStruct((S, W), F32)] * 3,
        compiler_params=_cparams(("parallel", "arbitrary"), block_bytes),
    )(pb, pb, pb, o32, do)


def _exchange_copies(src_refs, out_refs, send_sems, recv_sems, local_sems, per_peer):
    x, y, c = lax.axis_index("x"), lax.axis_index("y"), lax.axis_index("c")
    me = 4 * x + 2 * y + c
    n = len(src_refs)

    def copy(j, k):
        px, py, pc = x ^ (k >> 2), y ^ ((k >> 1) & 1), c ^ (k & 1)
        s = src_refs[j].at[4 * px + 2 * py + pc] if per_peer else src_refs[j]
        return pltpu.make_async_remote_copy(
            src_ref=s, dst_ref=out_refs[j].at[me], send_sem=send_sems.at[j, k - 1], recv_sem=recv_sems.at[j, k - 1],
            device_id=(px, py, pc), device_id_type=pl.DeviceIdType.MESH)

    mine = [pltpu.make_async_copy(src_refs[j].at[me] if per_peer else src_refs[j], out_refs[j].at[me], local_sems.at[j]) for j in range(n)]
    return mine, [copy(j, k) for k in range(1, N_DEV) for j in range(n)]


def _exchange_start(*refs, per_peer):
    mine, copies = _exchange_copies(*refs, per_peer)
    for cp in mine + copies:
        cp.start()


def _exchange_wait(*refs, per_peer):
    mine, copies = _exchange_copies(*refs, per_peer)
    for cp in copies:
        cp.wait_recv()
    for cp in copies:
        cp.wait_send()
    for cp in mine:
        cp.wait()


def _exchange_specs(srcs, per_peer):
    n = len(srcs)
    out_shape = [jax.ShapeDtypeStruct((N_DEV,) + tuple(s.shape[1:] if per_peer else s.shape), s.dtype) for s in srcs]
    sems = [pltpu.SemaphoreType.DMA((n, N_DEV - 1)), pltpu.SemaphoreType.DMA((n, N_DEV - 1)), pltpu.SemaphoreType.DMA((n,))]
    return [pl.BlockSpec(memory_space=pltpu.HBM)] * n, out_shape, sems


def _exchange(name, srcs, per_peer):
    n = len(srcs)

    def body(*refs):
        parts = (refs[:n], refs[n:2 * n], *refs[2 * n:])
        _exchange_start(*parts, per_peer=per_peer)
        _exchange_wait(*parts, per_peer=per_peer)

    hbm, out_shape, sems = _exchange_specs(srcs, per_peer)
    return pl.pallas_call(body, name=name, in_specs=hbm, out_specs=hbm, out_shape=out_shape, scratch_shapes=sems)(*srcs)


def _pallas(body, *, name, grid, in_specs, out_specs, out_shape, args, sem, block_bytes, ride=None):
    if ride is None:
        return pl.pallas_call(body, name=name, grid=grid, in_specs=in_specs, out_specs=out_specs, out_shape=out_shape,
                              compiler_params=_cparams(sem, block_bytes))(*args), None
    srcs, per_peer = ride
    n, n_in, n_out = len(srcs), len(in_specs), len(out_specs)

    def riding(*refs):
        ins, xsrc = refs[:n_in], refs[n_in:n_in + n]
        outs, xout = refs[n_in + n:n_in + n + n_out], refs[n_in + n + n_out:n_in + 2 * n + n_out]
        parts = (xsrc, xout, *refs[n_in + 2 * n + n_out:])
        ids = [pl.program_id(d) for d in range(len(grid))]
        first = functools.reduce(jnp.logical_and, [i == 0 for i in ids])
        last = functools.reduce(jnp.logical_and, [i == g - 1 for i, g in zip(ids, grid)])
        pl.when(first)(functools.partial(_exchange_start, *parts, per_peer=per_peer))
        body(*ins, *outs)
        pl.when(last)(functools.partial(_exchange_wait, *parts, per_peer=per_peer))

    hbm, x_shape, sems = _exchange_specs(srcs, per_peer)
    res = pl.pallas_call(riding, name=name, grid=grid, in_specs=list(in_specs) + hbm, out_specs=list(out_specs) + hbm,
                         out_shape=list(out_shape) + x_shape, scratch_shapes=sems,
                         compiler_params=_cparams(("arbitrary",) * len(grid), block_bytes))(*args, *srcs)
    return res[:n_out], res[n_out:]


def _adamw(name, parts, w, m, v, *, tile):
    R, C = w.shape

    def body(p_ref, w_ref, m_ref, v_ref, g_ref, d_ref, nm_ref, nv_ref):
        g = p_ref[0].astype(F32)
        for d in range(1, N_DEV):
            g = g + p_ref[d].astype(F32)
        wv = w_ref[...]
        mm = ADAM_B1 * m_ref[...] + (1.0 - ADAM_B1) * g
        vv = ADAM_B2 * v_ref[...] + (1.0 - ADAM_B2) * jnp.square(g)
        m_hat = mm / (1.0 - ADAM_B1 ** ADAM_STEP)
        v_hat = vv / (1.0 - ADAM_B2 ** ADAM_STEP)
        g_ref[...] = g
        d_ref[...] = -ADAM_LR * (m_hat / (jnp.sqrt(v_hat) + ADAM_EPS) + ADAM_WD * wv)
        nm_ref[...] = mm
        nv_ref[...] = vv

    blk = pl.BlockSpec((tile, C), lambda i: (i, 0))
    block_bytes = N_DEV * _nbytes((tile, C), parts.dtype) + 7 * _nbytes((tile, C), F32)
    return pl.pallas_call(
        body, name=name, grid=(R // tile,),
        in_specs=[pl.BlockSpec((N_DEV, tile, C), lambda i: (0, i, 0)), blk, blk, blk],
        out_specs=[blk] * 4, out_shape=[jax.ShapeDtypeStruct((R, C), F32)] * 4,
        compiler_params=_cparams(("parallel",), block_bytes),
    )(parts, w, m, v)


def _perm_w_in(w):
    cols = [w[:, ORIG_COLS[n][0]:ORIG_COLS[n][0] + ORIG_COLS[n][1]] for n in PERM_ORDER]
    cols.append(jnp.zeros((w.shape[0], IN_WIDTH_P - IN_WIDTH), w.dtype))
    return jnp.concatenate(cols, axis=1)


def _unperm_w_in(wp):
    order = sorted(ORIG_COLS, key=lambda n: ORIG_COLS[n][0])
    return jnp.concatenate([wp[:, PERM_OFF[n]:PERM_OFF[n] + ORIG_COLS[n][1]] for n in order], axis=1)


def _perm_w_uq(w):
    w3 = w.reshape(w.shape[0], MLA_HEADS, MLA_NOPE + MLA_ROPE)
    return jnp.pad(w3, ((0, 0), (0, 0), (0, LANES - MLA_NOPE - MLA_ROPE))).reshape(w.shape[0], MLA_QW)


def _unperm_w_uq(wp):
    return wp.reshape(wp.shape[0], MLA_HEADS, LANES)[:, :, :MLA_NOPE + MLA_ROPE].reshape(wp.shape[0], -1)


def _perm_w_ukv(w):
    w3 = w.reshape(w.shape[0], MLA_HEADS, MLA_NOPE + MLA_V)
    kp = jnp.pad(w3[:, :, :MLA_NOPE], ((0, 0), (0, 0), (0, LANES - MLA_NOPE))).reshape(w.shape[0], MLA_HEADS * LANES)
    return jnp.concatenate([kp, w3[:, :, MLA_NOPE:].reshape(w.shape[0], MLA_HEADS * MLA_V)], axis=1)


def _unperm_w_ukv(wp):
    n = wp.shape[0]
    kp = wp[:, :MLA_HEADS * LANES].reshape(n, MLA_HEADS, LANES)[:, :, :MLA_NOPE]
    vp = wp[:, MLA_HEADS * LANES:].reshape(n, MLA_HEADS, MLA_V)
    return jnp.concatenate([kp, vp], axis=2).reshape(n, MLA_HEADS * (MLA_NOPE + MLA_V))


BIG = ("w_in", "w_uq", "w_ukv", "w_o_mla", "w_o_swa", "w_o_sb", "w_out", "w_up", "w_down")
ROW_SHARDED = ("w_out", "w_down")
SMALL = ("g_mix_pre", "b_gate", "g_q_lat", "g_kv_lat", "swa_sinks", "g_mix_post", "g_mlp_pre", "g_mlp_post")
PACK_COLS = 1024


def _to_shards(name, full):
    L, R, C = full.shape
    if name in ROW_SHARDED:
        return full.reshape(L, N_DEV, R // N_DEV, C).transpose(1, 0, 2, 3).reshape(N_DEV, L * R // N_DEV, C)
    return full.reshape(L, R, N_DEV, C // N_DEV).transpose(2, 0, 1, 3).reshape(N_DEV, L * R, C // N_DEV)


def _from_shards(name, gathered, shard_shape):
    L, r, c = shard_shape
    a = gathered.reshape(N_DEV, L, r, c)
    if name in ROW_SHARDED:
        return a.transpose(1, 0, 2, 3).reshape(L, N_DEV * r, c)
    return a.transpose(1, 2, 0, 3).reshape(L, r, N_DEV * c)


def _tables(positions):
    pos = positions.astype(F32).reshape(-1, 1)

    def cs(d):
        inv = 1.0 / (ROPE_THETA ** (jnp.arange(0, d, 2, dtype=F32) / d))
        ang = pos * inv
        c, s = jnp.cos(ang), jnp.sin(ang)
        return jnp.concatenate([c, c], axis=1), jnp.concatenate([-s, s], axis=1)

    c64, s64 = cs(SWA_HEAD_DIM)
    c32, s32 = cs(MLA_ROPE)
    n = pos.shape[0]
    one, zero = jnp.ones((n, MLA_NOPE), F32), jnp.zeros((n, MLA_NOPE), F32)
    pad0 = jnp.zeros((n, LANES - MLA_NOPE - MLA_ROPE), F32)
    cq = jnp.concatenate([one, c32, pad0], axis=1)
    sq = jnp.concatenate([zero, s32, pad0], axis=1)
    padk = jnp.zeros((n, LANES - MLA_ROPE), F32)
    return dict(
        c_swa_q=jnp.tile(c64, (1, SWA_HEADS)), s_swa_q=jnp.tile(s64, (1, SWA_HEADS)),
        c_swa_k=jnp.tile(c64, (1, SWA_KV_HEADS)), s_swa_k=jnp.tile(s64, (1, SWA_KV_HEADS)),
        c_mla_q=jnp.tile(cq, (1, MLA_HEADS)), s_mla_q=jnp.tile(sq, (1, MLA_HEADS)),
        c_mla_k=jnp.concatenate([c32, padk], axis=1), s_mla_k=jnp.concatenate([s32, padk], axis=1))


def _cb(name):
    return PERM_OFF[name] // PERM_WIDTH[name]


MLA_SPEC = dict(groups=MLA_HEADS // 2, tq=512, tk=512, scale=(MLA_NOPE + MLA_ROPE) ** -0.5, window=0, q_scaled=True,
                heads=[(slice(h * LANES, (h + 1) * LANES), slice(h * LANES, (h + 1) * LANES), slice(h * MLA_V, (h + 1) * MLA_V)) for h in range(2)])
SWA_G = SWA_HEADS // SWA_KV_HEADS
SWA_TQ = 256
SB_SPEC = dict(tq=512, scale=SB_HEAD_DIM ** -0.5, q_cb=PERM_OFF["q_sb"] // LANES, k_cb=PERM_OFF["k_sb"] // LANES,
               v_cb=PERM_OFF["v_sb"] // LANES)


def _mla_specs(S):
    tq = MLA_SPEC["tq"]
    return dict(q_spec=((tq, 2 * LANES), lambda g, i: (i, g)), k_spec=((S, 2 * LANES), lambda g, i: (0, g)),
                v_spec=((S, 2 * MLA_V), lambda g, i: (0, g)))


def _layer_fwd(l, x, W, P, tb, T, ride):
    S = x.shape[0]
    nt = S // T
    h, = _rowwise(f"l{l}_norm_in", lambda xv, g: (_rms(xv, g),), [_ri(x, T), _bi(P["g_mix_pre"])], [_ro(S, D_MODEL, BF16, T)], n_tiles=nt)
    proj, proj16 = _mm_nn(f"l{l}_mm_in", h, W["w_in"], [F32, BF16], tm=512, tn=IN_WIDTH_P // 2)

    def mix_prep(cq, ckv, qs, ks, vs, gq, gkv, cq_t, sq_t, ck_t, sk_t):
        ksr = _rope(ks, ck_t, sk_t, SWA_HEAD_DIM // 2)
        hd = SWA_HEAD_DIM
        return (_rms(cq, gq), _rms(ckv, gkv), _rope(qs, cq_t, sq_t, hd // 2),
                [ksr[:, :hd], ksr[:, hd:]], [vs[:, :hd], vs[:, hd:]])

    kv3 = lambda dt: ((SWA_KV_HEADS, S, SWA_HEAD_DIM), dt, (SWA_KV_HEADS, T, SWA_HEAD_DIM), lambda i: (0, i, 0))
    cqn, ckvn, q_swa, k_swa, v_swa = _rowwise(
        f"l{l}_mix_prep", mix_prep,
        [_ri(proj, T, 256, _cb("c_q")), _ri(proj, T, 128, _cb("c_kv")), _ri(proj, T, 512, _cb("q_swa")), _ri(proj, T, 128, _cb("k_swa")),
         _ri(proj, T, 128, _cb("v_swa")), _bi(P["g_q_lat"]), _bi(P["g_kv_lat"]), _ri(tb["c_swa_q"], T), _ri(tb["s_swa_q"], T),
         _ri(tb["c_swa_k"], T), _ri(tb["s_swa_k"], T)],
        [_ro(S, 256, BF16, T), _ro(S, 128, BF16, T), _ro(S, 512, BF16, T), kv3(BF16), kv3(BF16)], n_tiles=nt)
    k_swa = k_swa.reshape(SWA_KV_HEADS * S, SWA_HEAD_DIM)
    v_swa = v_swa.reshape(SWA_KV_HEADS * S, SWA_HEAD_DIM)
    q_lat, = _mm_nn(f"l{l}_mm_uq", cqn, W["w_uq"], [F32], tm=1024, tn=MLA_QW)
    kv_lat, = _mm_nn(f"l{l}_mm_ukv", ckvn, W["w_ukv"], [F32], tm=1024, tn=MLA_KVW)

    def mla_prep(q, kk, vv, kr, cq_t, sq_t, ck_t, sk_t):
        kpe = pltpu.roll(_rope(kr, ck_t, sk_t, MLA_ROPE // 2), MLA_NOPE, axis=1)
        return _rope(q, cq_t, sq_t, MLA_ROPE // 2) * MLA_SPEC["scale"], kk + jnp.tile(kpe, (1, MLA_HEADS)), vv

    q_mla, k_mla, v_mla = _rowwise(
        f"l{l}_mla_prep", mla_prep,
        [_ri(q_lat, T), _ri(kv_lat, T, MLA_HEADS * LANES, 0), _ri(kv_lat, T, MLA_HEADS * MLA_V, 2), _ri(proj, T, 128, _cb("k_rope")),
         _ri(tb["c_mla_q"], T), _ri(tb["s_mla_q"], T), _ri(tb["c_mla_k"], T), _ri(tb["s_mla_k"], T)],
        [_ro(S, MLA_QW, BF16, T), _ro(S, MLA_HEADS * LANES, BF16, T), _ro(S, MLA_HEADS * MLA_V, BF16, T)], n_tiles=nt)

    (att_a, lse_a), rode = _softmax_fwd(f"l{l}_mla_fwd", q_mla, k_mla, v_mla, None, o_width=MLA_HEADS * MLA_V, ride=ride, **MLA_SPEC, **_mla_specs(S))
    att_b, lse_b = _swa_fwd(f"l{l}_swa_fwd", q_swa, k_swa, v_swa, P["swa_sinks"], tq=SWA_TQ)
    att_c, att_c32 = _sb_fwd(f"l{l}_sb_fwd", proj16, **SB_SPEC)
    o_a, = _mm_nn(f"l{l}_mm_oa", att_a, W["w_o_mla"], [F32], tm=1024, tn=D_MODEL)
    o_b, = _mm_nn(f"l{l}_mm_ob", att_b, W["w_o_swa"], [F32], tm=1024, tn=D_MODEL)
    o_c, = _mm_nn(f"l{l}_mm_oc", att_c, W["w_o_sb"], [F32], tm=1024, tn=D_MODEL)

    def gate_mix(gl, b, oa, ob, oc):
        gt = jax.nn.sigmoid(gl + b)
        return (gt[:, :D_MODEL] * oa + gt[:, D_MODEL:2 * D_MODEL] * ob + gt[:, 2 * D_MODEL:] * oc,)

    mixed, = _rowwise(f"l{l}_gate_mix", gate_mix, [_ri(proj, T, 3072, 0), _bi(P["b_gate"]), _ri(o_a, T), _ri(o_b, T), _ri(o_c, T)],
                      [_ro(S, D_MODEL, BF16, T)], n_tiles=nt)
    y, = _mm_nn(f"l{l}_mm_out", mixed, W["w_out"], [F32], tm=1024, tn=D_MODEL)

    def resid_norm(xv, yv, gpost, gpre):
        x1 = xv + _rms(yv, gpost)
        return x1, _rms(x1, gpre)

    x1, h2 = _rowwise(f"l{l}_resid_norm", resid_norm, [_ri(x, T), _ri(y, T), _bi(P["g_mix_post"]), _bi(P["g_mlp_pre"])],
                      [_ro(S, D_MODEL, F32, T), _ro(S, D_MODEL, BF16, T)], n_tiles=nt)
    up, u = _mm_nn(f"l{l}_mm_up", h2, W["w_up"], [F32, BF16], tm=512, tn=2048,
                   epilogue=lambda acc: (acc, jnp.square(jnp.maximum(acc, 0.0))))
    dn, = _mm_nn(f"l{l}_mm_down", u, W["w_down"], [F32], tm=512, tn=D_MODEL)
    x2, = _rowwise(f"l{l}_resid_out", lambda xv, dv, g: (xv + _rms(dv, g),), [_ri(x1, T), _ri(dn, T), _bi(P["g_mlp_post"])],
                   [_ro(S, D_MODEL, F32, T)], n_tiles=nt)
    saved = dict(x=x, h=h, proj=proj, proj16=proj16, cqn=cqn, ckvn=ckvn, q_swa=q_swa, k_swa=k_swa, v_swa=v_swa, q_mla=q_mla, k_mla=k_mla,
                 v_mla=v_mla, att_a=att_a, lse_a=lse_a, att_b=att_b, lse_b=lse_b, att_c=att_c, att_c32=att_c32, o_a=o_a, o_b=o_b, o_c=o_c,
                 mixed=mixed, y=y, x1=x1, h2=h2, up=up, u=u, dn=dn)
    return x2, saved, rode


def _layer_bwd(l, dx2, sv, W, P, tb, T, ride):
    S = dx2.shape[0]
    nt = S // T
    G = {}

    def post_norm_bwd(v, g, dy):
        return _rms_bwd(v, g, dy)

    d_dn, G["g_mlp_post"] = _rowwise(f"l{l}_b_post2", post_norm_bwd, [_ri(sv["dn"], T), _bi(P["g_mlp_post"]), _ri(dx2, T)],
                                    [_ro(S, D_MODEL, BF16, T)], [D_MODEL], n_tiles=nt)
    d_up, = _mm_nt(f"l{l}_b_mm_down", d_dn, W["w_down"], [BF16], tm=512, tn=2048, extras=[sv["up"]],
                   epilogue=lambda acc, upv: (acc * (2.0 * jnp.maximum(upv, 0.0)),))
    G["w_down"] = _mm_tn(f"l{l}_g_down", sv["u"], d_dn, tm=2048, tn=D_MODEL, ts=512)
    d_h2, = _mm_nt(f"l{l}_b_mm_up", d_up, W["w_up"], [F32], tm=512, tn=D_MODEL)
    G["w_up"] = _mm_tn(f"l{l}_g_up", sv["h2"], d_up, tm=D_MODEL, tn=2048, ts=512)

    def pre_norm_bwd(v, g, dy, dres):
        dx, dg = _rms_bwd(v, g, dy)
        return dres + dx, dg

    dx1, G["g_mlp_pre"] = _rowwise(f"l{l}_b_pre2", pre_norm_bwd, [_ri(sv["x1"], T), _bi(P["g_mlp_pre"]), _ri(d_h2, T), _ri(dx2, T)],
                                  [_ro(S, D_MODEL, F32, T)], [D_MODEL], n_tiles=nt)
    d_y, G["g_mix_post"] = _rowwise(f"l{l}_b_post1", post_norm_bwd, [_ri(sv["y"], T), _bi(P["g_mix_post"]), _ri(dx1, T)],
                                   [_ro(S, D_MODEL, BF16, T)], [D_MODEL], n_tiles=nt)
    d_mixed, = _mm_nt(f"l{l}_b_mm_out", d_y, W["w_out"], [F32], tm=1024, tn=D_MODEL)
    G["w_out"] = _mm_tn(f"l{l}_g_out", sv["mixed"], d_y, tm=D_MODEL, tn=D_MODEL, ts=512)

    def gate_bwd(dm, gl, b, oa, ob, oc):
        gt = jax.nn.sigmoid(gl + b)
        outs, dgl = [], []
        for k, o in enumerate((oa, ob, oc)):
            gk = gt[:, k * D_MODEL:(k + 1) * D_MODEL]
            outs.append(dm * gk)
            dgl.append(dm * o * gk * (1.0 - gk))
        dgl = jnp.concatenate(dgl, axis=1)
        return (*outs, dgl, jnp.sum(dgl, axis=0, keepdims=True))

    d_oa, d_ob, d_oc, d_gl, G["b_gate"] = _rowwise(
        f"l{l}_b_gate", gate_bwd, [_ri(d_mixed, T), _ri(sv["proj"], T, 3072, 0), _bi(P["b_gate"]), _ri(sv["o_a"], T), _ri(sv["o_b"], T), _ri(sv["o_c"], T)],
        [_ro(S, D_MODEL, BF16, T)] * 3 + [_ro(S, 3 * D_MODEL, BF16, T)], [3 * D_MODEL], n_tiles=nt)
    d_att = {}
    for br, d_o, att in (("mla", d_oa, sv["att_a"]), ("swa", d_ob, sv["att_b"]), ("sb", d_oc, sv["att_c"])):
        d_att[br], = _mm_nt(f"l{l}_b_mm_o_{br}", d_o, W["w_o_" + br], [F32], tm=1024, tn=512)
        G["w_o_" + br] = _mm_tn(f"l{l}_g_o_{br}", att, d_o, tm=512, tn=D_MODEL, ts=512)

    (dq_mla, dk_mla, dv_mla), rode = _softmax_bwd(f"l{l}_mla_bwd", sv["q_mla"], sv["k_mla"], sv["v_mla"], sv["att_a"], d_att["mla"], sv["lse_a"],
                                                  None, ride=ride, **MLA_SPEC, **_mla_specs(S))
    dq_swa, dk_swa, dv_swa, dsink = _swa_bwd(f"l{l}_swa_bwd", sv["q_swa"], sv["k_swa"], sv["v_swa"], sv["att_b"], d_att["swa"], sv["lse_b"],
                                             P["swa_sinks"], tq=SWA_TQ)
    G["swa_sinks"] = dsink[:, :SWA_G, 0].reshape(1, SWA_HEADS)
    dq_sb, dk_sb, dv_sb = _sb_bwd(f"l{l}_sb_bwd", sv["proj16"], sv["att_c32"], d_att["sb"], **SB_SPEC)

    def mla_prep_bwd(dq, dk, dvv, cq_t, sq_t, ck_t, sk_t):
        dks = dk[:, :LANES]
        for hh in range(1, MLA_HEADS):
            dks = dks + dk[:, hh * LANES:(hh + 1) * LANES]
        d_kr = _rope_t(pltpu.roll(dks, LANES - MLA_NOPE, axis=1), ck_t, sk_t, MLA_ROPE // 2)
        return _rope_t(dq, cq_t, sq_t, MLA_ROPE // 2), jnp.concatenate([dk, dvv], axis=1), d_kr

    d_q_lat, d_kv_lat, d_krope = _rowwise(
        f"l{l}_b_mla_prep", mla_prep_bwd,
        [_ri(dq_mla, T), _ri(dk_mla, T), _ri(dv_mla, T), _ri(tb["c_mla_q"], T), _ri(tb["s_mla_q"], T), _ri(tb["c_mla_k"], T), _ri(tb["s_mla_k"], T)],
        [_ro(S, MLA_QW, BF16, T), _ro(S, MLA_KVW, BF16, T), _ro(S, LANES, BF16, T)], n_tiles=nt)
    d_cqn, = _mm_nt(f"l{l}_b_mm_uq", d_q_lat, W["w_uq"], [F32], tm=1024, tn=MLA_Q_LORA)
    G["w_uq"] = _mm_tn(f"l{l}_g_uq", sv["cqn"], d_q_lat, tm=MLA_Q_LORA, tn=MLA_QW, ts=512)
    d_ckvn, = _mm_nt(f"l{l}_b_mm_ukv", d_kv_lat, W["w_ukv"], [F32], tm=1024, tn=MLA_KV_LORA)
    G["w_ukv"] = _mm_tn(f"l{l}_g_ukv", sv["ckvn"], d_kv_lat, tm=MLA_KV_LORA, tn=MLA_KVW, ts=512)

    def mix_prep_bwd(cq, ckv, gq, gkv, dcqn, dckvn, dqs, dks, dvs, cq_t, sq_t, ck_t, sk_t):
        d_cq, dgq = _rms_bwd(cq, gq, dcqn)
        d_ckv, dgkv = _rms_bwd(ckv, gkv, dckvn)
        dk2 = jnp.concatenate([dks[0], dks[1]], axis=1)
        dv2 = jnp.concatenate([dvs[0], dvs[1]], axis=1)
        return (d_cq, d_ckv, _rope_t(dqs, cq_t, sq_t, SWA_HEAD_DIM // 2), _rope_t(dk2, ck_t, sk_t, SWA_HEAD_DIM // 2), dv2, dgq, dgkv)

    kv3 = lambda a: (a.reshape(SWA_KV_HEADS, S, SWA_HEAD_DIM), (SWA_KV_HEADS, T, SWA_HEAD_DIM), lambda i: (0, i, 0))
    d_cq, d_ckv, d_qswa, d_kswa, d_vswa, G["g_q_lat"], G["g_kv_lat"] = _rowwise(
        f"l{l}_b_mix_prep", mix_prep_bwd,
        [_ri(sv["proj"], T, 256, _cb("c_q")), _ri(sv["proj"], T, 128, _cb("c_kv")), _bi(P["g_q_lat"]), _bi(P["g_kv_lat"]), _ri(d_cqn, T), _ri(d_ckvn, T),
         _ri(dq_swa, T), kv3(dk_swa), kv3(dv_swa), _ri(tb["c_swa_q"], T), _ri(tb["s_swa_q"], T), _ri(tb["c_swa_k"], T), _ri(tb["s_swa_k"], T)],
        [_ro(S, 256, BF16, T), _ro(S, 128, BF16, T), _ro(S, 512, BF16, T), _ro(S, 128, BF16, T), _ro(S, 128, BF16, T)], [256, 128], n_tiles=nt)
    pieces = dict(gates=d_gl, q_swa=d_qswa, q_sb=dq_sb, k_sb=dk_sb, v_sb=dv_sb, c_q=d_cq, c_kv=d_ckv, k_swa=d_kswa, v_swa=d_vswa, k_rope=d_krope)
    d_proj = jnp.concatenate([pieces[n].astype(BF16) for n in PERM_ORDER], axis=1)
    d_h, = _mm_nt(f"l{l}_b_mm_in", d_proj, W["w_in"], [F32], tm=512, tn=512)
    G["w_in"] = _mm_tn(f"l{l}_g_in", sv["h"], d_proj, tm=D_MODEL, tn=IN_WIDTH_P // 2, ts=512)
    dx, G["g_mix_pre"] = _rowwise(f"l{l}_b_pre1", pre_norm_bwd, [_ri(sv["x"], T), _bi(P["g_mix_pre"]), _ri(d_h, T), _ri(dx1, T)],
                                 [_ro(S, D_MODEL, F32, T)], [D_MODEL], n_tiles=nt)
    return dx, G, rode


def _pack_rows(vecs, rows):
    flat = jnp.concatenate([v.reshape(-1) for v in vecs])
    return jnp.pad(flat, (0, rows * PACK_COLS - flat.shape[0])).reshape(rows, PACK_COLS)


def kernel(x, positions, g_mix_pre, w_in, b_gate, g_q_lat, g_kv_lat, w_uq, w_ukv, swa_sinks, w_o_mla, w_o_swa, w_o_sb, w_out, g_mix_post, g_mlp_pre, w_up, w_down, g_mlp_post, loss_target, m_g_mix_pre, m_w_in, m_b_gate, m_g_q_lat, m_g_kv_lat, m_w_uq, m_w_ukv, m_swa_sinks, m_w_o_mla, m_w_o_swa, m_w_o_sb, m_w_out, m_g_mix_post, m_g_mlp_pre, m_w_up, m_w_down, m_g_mlp_post, v_g_mix_pre, v_w_in, v_b_gate, v_g_q_lat, v_g_kv_lat, v_w_uq, v_w_ukv, v_swa_sinks, v_w_o_mla, v_w_o_swa, v_w_o_sb, v_w_out, v_g_mix_post, v_g_mlp_pre, v_w_up, v_w_down, v_g_mlp_post):
    a = dict(locals())
    S = x.shape[1]
    depth = w_in.shape[0]
    T = min(256, S)
    xs = x.reshape(S, D_MODEL)
    tb = _tables(positions)

    def weight_shards(l):
        return [a[n][l].astype(BF16) for n in BIG]

    gathered = _exchange("gather_weights", weight_shards(0), per_peer=False)
    layers, saved = [], []
    h = xs
    for l in range(depth):
        W = {n: _from_shards(n, g, (1,) + a[n].shape[1:])[0] for n, g in zip(BIG, gathered)}
        W["w_in"] = _perm_w_in(W["w_in"])
        W["w_uq"] = _perm_w_uq(W["w_uq"])
        W["w_ukv"] = _perm_w_ukv(W["w_ukv"])
        P = {n: a[n][l].reshape(1, -1) for n in SMALL if n != "swa_sinks"}
        P["swa_sinks"] = a["swa_sinks"][l]
        layers.append((W, P))
        h, sv, gathered = _layer_fwd(l, h, W, P, tb, T, (weight_shards(l + 1), False) if l + 1 < depth else None)
        saved.append(sv)

    def loss_head(yv, tv):
        err = yv - tv
        part = 0.5 * jnp.sum(jnp.mean(err * err, axis=1, keepdims=True), axis=0, keepdims=True)
        return err * (1.0 / D_MODEL), jnp.broadcast_to(part, (1, LANES))

    dh, loss_row = _rowwise("loss_head", loss_head, [_ri(h, T), _ri(loss_target.reshape(S, D_MODEL), T)], [_ro(S, D_MODEL, F32, T)], [LANES],
                            n_tiles=S // T)
    loss = lax.psum(loss_row[0, 0], ("x", "y", "c"))

    unperm = dict(w_in=_unperm_w_in, w_uq=_unperm_w_uq, w_ukv=_unperm_w_ukv)
    grads, recv, pending = [None] * depth, [None] * depth, None
    for l in reversed(range(depth)):
        W, P = layers[l]
        dh, grads[l], rode = _layer_bwd(l, dh, saved[l], W, P, tb, T, (pending, True) if pending is not None else None)
        if rode is not None:
            recv[l + 1] = rode
        pending = [_to_shards(n, unperm.get(n, lambda t: t)(grads[l][n]).astype(BF16)[None]) for n in BIG]
    recv[0] = _exchange("scatter_grads", pending, per_peer=True)
    grad_x = dh.reshape(x.shape)

    out = {}
    for j, n in enumerate(BIG):
        shp = a[n].shape
        rows, cols = shp[0] * shp[1], shp[2]
        parts = jnp.concatenate([recv[l][j] for l in range(depth)], axis=1)
        res = _adamw("adamw_" + n, parts, a[n].reshape(rows, cols), a["m_" + n].reshape(rows, cols), a["v_" + n].reshape(rows, cols),
                     tile=min(256, rows))
        out[n] = [r.reshape(shp) for r in res]

    small_total = sum(a[n].size for n in SMALL)
    small_rows = -(-small_total // (8 * PACK_COLS)) * 8
    sg = _pack_rows([jnp.stack([grads[l][n].reshape(-1) for l in range(depth)]) for n in SMALL], small_rows)
    sg_all, = _exchange("gather_small_grads", [sg], per_peer=False)
    res = _adamw("adamw_small", sg_all, _pack_rows([a[n] for n in SMALL], small_rows), _pack_rows([a["m_" + n] for n in SMALL], small_rows),
                 _pack_rows([a["v_" + n] for n in SMALL], small_rows), tile=small_rows)
    off = 0
    for n in SMALL:
        cnt = a[n].size
        out[n] = [r.reshape(-1)[off:off + cnt].reshape(a[n].shape) for r in res]
        off += cnt

    order = ("g_mix_pre", "w_in", "b_gate", "g_q_lat", "g_kv_lat", "w_uq", "w_ukv", "swa_sinks", "w_o_mla", "w_o_swa", "w_o_sb", "w_out",
             "g_mix_post", "g_mlp_pre", "w_up", "w_down", "g_mlp_post")
    return (loss, grad_x, *[out[n][0] for n in order], *[out[n][1] for n in order], *[out[n][2] for n in order], *[out[n][3] for n in order])
```

```python
import functools

import jax
import jax.numpy as jnp
from jax import lax
from jax.experimental import pallas as pl
from jax.experimental.pallas import tpu as pltpu

F32, BF16 = jnp.float32, jnp.bfloat16

D_MODEL = 1024
DEPTH = 4
MLA_HEADS, MLA_Q_LORA, MLA_KV_LORA, MLA_NOPE, MLA_ROPE, MLA_V = 8, 256, 128, 64, 32, 64
SWA_HEADS, SWA_KV_HEADS, SWA_HEAD_DIM, SWA_WINDOW = 8, 2, 64, 128
SB_HEADS, SB_HEAD_DIM = 8, 64
D_FF = 4 * D_MODEL
ROPE_THETA = 10000.0
EPS = 1e-6
N_DEV = 8
ADAM_LR, ADAM_B1, ADAM_B2, ADAM_EPS, ADAM_WD, ADAM_STEP = 0.001, 0.9, 0.999, 1e-08, 0.01, 10

LANES = 128
VMEM_LIMIT_MAX = 60 * 1024 * 1024
VMEM_LIMIT_MIN = 32 * 1024 * 1024

ORIG_COLS = dict(c_q=(0, 256), c_kv=(256, 128), k_rope=(384, 32), q_swa=(416, 512), k_swa=(928, 128), v_swa=(1056, 128),
                 q_sb=(1184, 512), k_sb=(1696, 512), v_sb=(2208, 512), gates=(2720, 3072))
IN_WIDTH = 5792
PERM_ORDER = ("gates", "q_swa", "q_sb", "k_sb", "v_sb", "c_q", "c_kv", "k_swa", "v_swa", "k_rope")
PERM_WIDTH = dict(gates=3072, q_swa=512, q_sb=512, k_sb=512, v_sb=512, c_q=256, c_kv=128, k_swa=128, v_swa=128, k_rope=128)
PERM_OFF = {}
_o = 0
for _n in PERM_ORDER:
    PERM_OFF[_n] = _o
    _o += PERM_WIDTH[_n]
IN_WIDTH_P = _o
MLA_QW = MLA_HEADS * LANES
MLA_KVW = MLA_HEADS * LANES + MLA_HEADS * MLA_V

NT = (((1,), (1,)), ((), ()))
TN = (((0,), (0,)), ((), ()))
NEG = -1e30


def _cparams(sem, block_bytes):
    limit = int(min(VMEM_LIMIT_MAX, max(VMEM_LIMIT_MIN, 2 * block_bytes + (16 << 20))))
    return pltpu.CompilerParams(dimension_semantics=sem, vmem_limit_bytes=limit)


def _nbytes(shape, dtype):
    n = 1
    for s in shape:
        n *= s
    return n * jnp.dtype(dtype).itemsize


def _ri(arr, tile, width=None, cb=0):
    width = arr.shape[1] if width is None else width
    return (arr, (tile, width), lambda i, cb=cb: (i, cb))


def _bi(arr):
    return (arr, arr.shape, lambda i: (0, 0))


def _ro(rows, width, dtype, tile):
    return ((rows, width), dtype, (tile, width), lambda i: (i, 0))


def _rowwise(name, fn, ins, outs, reds=(), *, n_tiles):
    n_in, n_out = len(ins), len(outs)

    def body(*refs):
        vals = fn(*[r[...] for r in refs[:n_in]])
        for r, v in zip(refs[n_in:n_in + n_out], vals[:n_out]):
            if isinstance(v, (list, tuple)):
                for j, vj in enumerate(v):
                    r[j] = vj.astype(r.dtype)
            else:
                r[...] = v.astype(r.dtype)
        if reds:
            @pl.when(pl.program_id(0) == 0)
            def _():
                for r in refs[n_in + n_out:]:
                    r[...] = jnp.zeros_like(r)
            for r, v in zip(refs[n_in + n_out:], vals[n_out:]):
                r[...] += v

    block_bytes = sum(_nbytes(b, a.dtype) for a, b, _ in ins) + sum(_nbytes(b, d) for _, d, b, _ in outs)
    res = pl.pallas_call(
        body, name=name, grid=(n_tiles,),
        in_specs=[pl.BlockSpec(b, m) for _, b, m in ins],
        out_specs=[pl.BlockSpec(b, m) for _, _, b, m in outs] + [pl.BlockSpec((1, w), lambda i: (0, 0)) for w in reds],
        out_shape=[jax.ShapeDtypeStruct(s, d) for s, d, _, _ in outs] + [jax.ShapeDtypeStruct((1, w), F32) for w in reds],
        compiler_params=_cparams(("arbitrary",) if reds else ("parallel",), block_bytes),
    )(*[a for a, _, _ in ins])
    return res


def _rms(x, g):
    r = lax.rsqrt(jnp.mean(x * x, axis=1, keepdims=True) + EPS)
    return x * r * g


def _rms_bwd(x, g, dy):
    r = lax.rsqrt(jnp.mean(x * x, axis=1, keepdims=True) + EPS)
    xn = x * r
    dxn = dy * g
    dx = r * (dxn - xn * jnp.mean(dxn * xn, axis=1, keepdims=True))
    return dx, jnp.sum(dy * xn, axis=0, keepdims=True)


def _swap_halves(x, half):
    n = x.shape[1]
    lane = lax.broadcasted_iota(jnp.int32, x.shape, 1)
    first = (lane % (2 * half)) < half
    return jnp.where(first, pltpu.roll(x, n - half, axis=1), pltpu.roll(x, half, axis=1))


def _rope(x, c, sg, half):
    return x * c + _swap_halves(x, half) * sg


def _rope_t(dy, c, sg, half):
    return dy * c - _swap_halves(dy, half) * sg


def _mm_nn(name, a, b, outs, *, tm, tn, extras=(), epilogue=None):
    M, K = a.shape
    N = b.shape[1]
    tm = min(tm, M)
    n_e = len(extras)

    def body(*refs):
        a_ref, b_ref = refs[:2]
        acc = jnp.dot(a_ref[...].astype(BF16), b_ref[...].astype(BF16), preferred_element_type=F32)
        vals = (acc,) * len(outs) if epilogue is None else epilogue(acc, *[r[...] for r in refs[2:2 + n_e]])
        for r, v in zip(refs[2 + n_e:], vals):
            r[...] = v.astype(r.dtype)

    block_bytes = (_nbytes((tm, K), a.dtype) + _nbytes((K, tn), b.dtype) + sum(_nbytes((tm, tn), e.dtype) for e in extras)
                   + sum(_nbytes((tm, tn), d) for d in outs) + _nbytes((tm, tn), F32))
    return pl.pallas_call(
        body, name=name, grid=(N // tn, M // tm),
        in_specs=[pl.BlockSpec((tm, K), lambda j, i: (i, 0)), pl.BlockSpec((K, tn), lambda j, i: (0, j))]
        + [pl.BlockSpec((tm, tn), lambda j, i: (i, j)) for _ in extras],
        out_specs=[pl.BlockSpec((tm, tn), lambda j, i: (i, j)) for _ in outs],
        out_shape=[jax.ShapeDtypeStruct((M, N), d) for d in outs],
        compiler_params=_cparams(("parallel", "parallel"), block_bytes),
    )(a, b, *extras)


def _mm_nt(name, a, b, outs, *, tm, tn, extras=(), epilogue=None):
    M, N = a.shape
    K = b.shape[0]
    tm = min(tm, M)
    n_e = len(extras)

    def body(*refs):
        a_ref, b_ref = refs[:2]
        acc = lax.dot_general(a_ref[...].astype(BF16), b_ref[...].astype(BF16), NT, preferred_element_type=F32)
        vals = (acc,) * len(outs) if epilogue is None else epilogue(acc, *[r[...] for r in refs[2:2 + n_e]])
        for r, v in zip(refs[2 + n_e:], vals):
            r[...] = v.astype(r.dtype)

    block_bytes = (_nbytes((tm, N), a.dtype) + _nbytes((tn, N), b.dtype) + sum(_nbytes((tm, tn), e.dtype) for e in extras)
                   + sum(_nbytes((tm, tn), d) for d in outs) + _nbytes((tm, tn), F32))
    return pl.pallas_call(
        body, name=name, grid=(K // tn, M // tm),
        in_specs=[pl.BlockSpec((tm, N), lambda j, i: (i, 0)), pl.BlockSpec((tn, N), lambda j, i: (j, 0))]
        + [pl.BlockSpec((tm, tn), lambda j, i: (i, j)) for _ in extras],
        out_specs=[pl.BlockSpec((tm, tn), lambda j, i: (i, j)) for _ in outs],
        out_shape=[jax.ShapeDtypeStruct((M, K), d) for d in outs],
        compiler_params=_cparams(("parallel", "parallel"), block_bytes),
    )(a, b, *extras)


def _mm_tn(name, a, b, *, tm, tn, ts):
    S, K = a.shape
    N = b.shape[1]
    ts = min(ts, S)

    def body(a_ref, b_ref, o_ref):
        @pl.when(pl.program_id(2) == 0)
        def _():
            o_ref[...] = jnp.zeros_like(o_ref)
        o_ref[...] += lax.dot_general(a_ref[...].astype(BF16), b_ref[...].astype(BF16), TN, preferred_element_type=F32)

    block_bytes = _nbytes((ts, tm), a.dtype) + _nbytes((ts, tn), b.dtype) + 2 * _nbytes((tm, tn), F32)
    return pl.pallas_call(
        body, name=name, grid=(K // tm, N // tn, S // ts),
        in_specs=[pl.BlockSpec((ts, tm), lambda i, j, s: (s, i)), pl.BlockSpec((ts, tn), lambda i, j, s: (s, j))],
        out_specs=pl.BlockSpec((tm, tn), lambda i, j, s: (i, j)),
        out_shape=jax.ShapeDtypeStruct((K, N), F32),
        compiler_params=_cparams(("parallel", "parallel", "arbitrary"), block_bytes),
    )(a, b)


def _lane_pack(cols, rows):
    lane = lax.broadcasted_iota(jnp.int32, (rows, LANES), 1)
    val = jnp.zeros((rows, LANES), F32)
    for h, c in enumerate(cols):
        val = jnp.where(lane == h, c, val)
    return val


def _mask(kb, row, tk):
    return kb * tk + lax.broadcasted_iota(jnp.int32, (1, tk), 1) <= row


def _kb_range(i, tq, tk):
    return (i * tq) // tk, ((i + 1) * tq + tk - 1) // tk


def _softmax_fwd(name, q, k, v, *, groups, heads, q_spec, k_spec, v_spec, o_width, tq, tk, ride=None):
    S = q.shape[0]
    nq = S // tq
    nh = len(heads)
    dv = heads[0][2].stop - heads[0][2].start

    def body(q_ref, k_ref, v_ref, o_ref, lse_ref):
        i = pl.program_id(1)
        q_all = q_ref[...]
        row = i * tq + lax.broadcasted_iota(jnp.int32, (tq, 1), 0)
        mid, hi = _kb_range(i, tq, tk)
        qhs = [q_all[:, qs] for qs, _, _ in heads]

        def blk(kb, carry, masked):
            r0 = pl.multiple_of(kb * tk, tk)
            k_all = k_ref[pl.ds(r0, tk), :]
            v_all = v_ref[pl.ds(r0, tk), :]
            if masked:
                ok = _mask(kb, row, tk)
            new = []
            for qh, (_, ks, vs), (m, l, acc) in zip(qhs, heads, carry):
                s = lax.dot_general(qh, k_all[:, ks], NT, preferred_element_type=F32)
                if masked:
                    s = jnp.where(ok, s, NEG)
                m_new = jnp.maximum(m, jnp.max(s, axis=1, keepdims=True))
                alpha = jnp.exp(m - m_new)
                p = jnp.exp(s - m_new)
                if masked:
                    p = jnp.where(ok, p, 0.0)
                l = alpha * l + jnp.sum(p, axis=1, keepdims=True)
                acc = alpha * acc + jnp.dot(p.astype(BF16), v_all[:, vs], preferred_element_type=F32)
                new.append((m_new, l, acc))
            return tuple(new)

        carry = tuple((jnp.full((tq, 1), NEG, F32), jnp.zeros((tq, 1), F32), jnp.zeros((tq, dv), F32)) for _ in range(nh))
        carry = lax.fori_loop(0, mid, functools.partial(blk, masked=False), carry)
        carry = lax.fori_loop(mid, hi, functools.partial(blk, masked=True), carry)
        o_ref[...] = jnp.concatenate([acc / l for _, l, acc in carry], axis=1).astype(o_ref.dtype)
        lse_ref[0] = _lane_pack([m + jnp.log(l) for m, l, _ in carry], tq)

    wo = nh * dv
    block_bytes = _nbytes(q_spec[0], q.dtype) + _nbytes(k_spec[0], k.dtype) + _nbytes(v_spec[0], v.dtype) + 4 * tq * (wo + LANES)
    return _pallas(
        body, name=name, grid=(groups, nq), in_specs=[pl.BlockSpec(*q_spec), pl.BlockSpec(*k_spec), pl.BlockSpec(*v_spec)],
        out_specs=[pl.BlockSpec((tq, wo), lambda g, i: (i, g)), pl.BlockSpec((1, tq, LANES), lambda g, i: (g, i, 0))],
        out_shape=[jax.ShapeDtypeStruct((S, o_width), BF16), jax.ShapeDtypeStruct((groups, S, LANES), F32)],
        args=[q, k, v], sem=("parallel", "arbitrary"), block_bytes=block_bytes, ride=ride)


def _softmax_bwd(name, q, k, v, o, do, lse, *, groups, heads, q_spec, k_spec, v_spec, tq, tk, scale, ride=None):
    S = q.shape[0]
    nq = S // tq
    nh = len(heads)
    dv = heads[0][2].stop - heads[0][2].start
    wo = nh * dv

    def body(q_ref, k_ref, v_ref, o_ref, do_ref, lse_ref, dq_ref, dk_ref, dv_ref):
        i = pl.program_id(1)

        @pl.when(i == 0)
        def _():
            dk_ref[...] = jnp.zeros_like(dk_ref)
            dv_ref[...] = jnp.zeros_like(dv_ref)

        q_all = q_ref[...]
        o_all = o_ref[...].astype(F32)
        do_all = do_ref[...].astype(F32)
        lse_all = lse_ref[0]
        row = i * tq + lax.broadcasted_iota(jnp.int32, (tq, 1), 0)
        mid, hi = _kb_range(i, tq, tk)
        per_head = []
        for h, (qs, ks, vs) in enumerate(heads):
            osl = slice(h * dv, (h + 1) * dv)
            doh = do_all[:, osl]
            delta = jnp.sum(doh * o_all[:, osl], axis=1, keepdims=True)
            per_head.append((q_all[:, qs], doh.astype(BF16), delta, lse_all[:, h:h + 1], ks, vs))

        def blk(kb, dqs, masked):
            r0 = pl.multiple_of(kb * tk, tk)
            k_all = k_ref[pl.ds(r0, tk), :]
            v_all = v_ref[pl.ds(r0, tk), :]
            if masked:
                ok = _mask(kb, row, tk)
            dk_parts, dv_parts, new_dqs = {}, {}, []
            for (qh, doh, delta, lse_h, ks, vs), dq in zip(per_head, dqs):
                kk, vv = k_all[:, ks], v_all[:, vs]
                s = lax.dot_general(qh, kk, NT, preferred_element_type=F32)
                p = jnp.exp(s - lse_h)
                if masked:
                    p = jnp.where(ok, p, 0.0)
                dp = lax.dot_general(doh, vv, NT, preferred_element_type=F32)
                ds = p * (dp - delta)
                new_dqs.append(dq + jnp.dot(ds.astype(BF16), kk, preferred_element_type=F32))
                dk_c = jnp.dot(ds.T.astype(BF16), qh, preferred_element_type=F32)
                dv_c = jnp.dot(p.T.astype(BF16), doh, preferred_element_type=F32)
                dk_parts[ks.start] = dk_parts[ks.start] + dk_c if ks.start in dk_parts else dk_c
                dv_parts[vs.start] = dv_parts[vs.start] + dv_c if vs.start in dv_parts else dv_c
            dk_ref[pl.ds(r0, tk), :] += jnp.concatenate([dk_parts[s0] for s0 in sorted(dk_parts)], axis=1)
            dv_ref[pl.ds(r0, tk), :] += jnp.concatenate([dv_parts[s0] for s0 in sorted(dv_parts)], axis=1)
            return tuple(new_dqs)

        dqs = tuple(jnp.zeros((tq, qs.stop - qs.start), F32) for qs, _, _ in heads)
        dqs = lax.fori_loop(0, mid, functools.partial(blk, masked=False), dqs)
        dqs = lax.fori_loop(mid, hi, functools.partial(blk, masked=True), dqs)
        dq_ref[...] = jnp.concatenate([dq * scale for dq in dqs], axis=1)

    in_specs = [pl.BlockSpec(*q_spec), pl.BlockSpec(*k_spec), pl.BlockSpec(*v_spec),
                pl.BlockSpec((tq, wo), lambda g, i: (i, g)), pl.BlockSpec((tq, wo), lambda g, i: (i, g)),
                pl.BlockSpec((1, tq, LANES), lambda g, i: (g, i, 0))]
    args = [q, k, v, o, do, lse]
    out_specs = [pl.BlockSpec(*q_spec), pl.BlockSpec(*k_spec), pl.BlockSpec(*v_spec)]
    out_shape = [jax.ShapeDtypeStruct(q.shape, F32), jax.ShapeDtypeStruct(k.shape, F32), jax.ShapeDtypeStruct(v.shape, F32)]
    block_bytes = (_nbytes(q_spec[0], q.dtype) + _nbytes(k_spec[0], k.dtype) + _nbytes(v_spec[0], v.dtype) + 6 * tq * wo + 4 * tq * LANES
                   + _nbytes(q_spec[0], F32) + _nbytes(k_spec[0], F32) + _nbytes(v_spec[0], F32))
    return _pallas(body, name=name, grid=(groups, nq), in_specs=in_specs, out_specs=out_specs, out_shape=out_shape, args=args,
                   sem=("parallel", "arbitrary"), block_bytes=block_bytes, ride=ride)


def _swa_window(i, tq, row):
    start = pl.multiple_of(jnp.maximum(i * tq - SWA_WINDOW, 0), SWA_WINDOW)
    col = start + lax.broadcasted_iota(jnp.int32, (1, tq + SWA_WINDOW), 1)
    return start, (col <= row) & ((row - col) < SWA_WINDOW)


def _swa_fwd(name, q, k, v, sinks, *, tq):
    S = q.shape[0]
    hd, span = SWA_HEAD_DIM, tq + SWA_WINDOW
    scale = hd ** -0.5

    def body(sink_ref, q_ref, k_ref, v_ref, o_ref, lse_ref):
        g, i = pl.program_id(0), pl.program_id(1)
        row = i * tq + lax.broadcasted_iota(jnp.int32, (tq, 1), 0)
        start, ok = _swa_window(i, tq, row)
        kk, vv = k_ref[pl.ds(start, span), :], v_ref[pl.ds(start, span), :]
        q_all = q_ref[...]
        outs, lses = [], []
        for h in range(SWA_G):
            s = lax.dot_general(q_all[:, h * hd:(h + 1) * hd] * scale, kk, NT, preferred_element_type=F32)
            s = jnp.where(ok, s, NEG)
            sink = sink_ref[g * SWA_G + h]
            m = jnp.maximum(jnp.max(s, axis=1, keepdims=True), sink)
            p = jnp.exp(s - m)
            l = jnp.sum(p, axis=1, keepdims=True) + jnp.exp(sink - m)
            outs.append(jnp.dot(p.astype(BF16), vv, preferred_element_type=F32) / l)
            lses.append(m + jnp.log(l))
        o_ref[...] = jnp.concatenate(outs, axis=1).astype(o_ref.dtype)
        lse_ref[0] = _lane_pack(lses, tq)

    wq = SWA_G * hd
    block_bytes = 2 * tq * wq * 2 + 2 * 2 * S * hd + 4 * tq * LANES
    return pl.pallas_call(
        body, name=name, grid=(SWA_KV_HEADS, S // tq),
        in_specs=[pl.BlockSpec(memory_space=pltpu.SMEM), pl.BlockSpec((tq, wq), lambda g, i: (i, g)),
                  pl.BlockSpec((S, hd), lambda g, i: (g, 0)), pl.BlockSpec((S, hd), lambda g, i: (g, 0))],
        out_specs=[pl.BlockSpec((tq, wq), lambda g, i: (i, g)), pl.BlockSpec((1, tq, LANES), lambda g, i: (g, i, 0))],
        out_shape=[jax.ShapeDtypeStruct((S, SWA_HEADS * hd), BF16), jax.ShapeDtypeStruct((SWA_KV_HEADS, S, LANES), F32)],
        compiler_params=_cparams(("parallel", "parallel"), block_bytes),
    )(sinks, q, k, v)


def _swa_bwd(name, q, k, v, o, do, lse, sinks, *, tq):
    S = q.shape[0]
    hd, span = SWA_HEAD_DIM, tq + SWA_WINDOW
    scale = hd ** -0.5

    def body(sink_ref, q_ref, k_ref, v_ref, o_ref, do_ref, lse_ref, dq_ref, dk_ref, dv_ref, dsink_ref):
        g, i = pl.program_id(0), pl.program_id(1)

        @pl.when(i == 0)
        def _():
            dk_ref[...] = jnp.zeros_like(dk_ref)
            dv_ref[...] = jnp.zeros_like(dv_ref)
            dsink_ref[...] = jnp.zeros_like(dsink_ref)

        row = i * tq + lax.broadcasted_iota(jnp.int32, (tq, 1), 0)
        start, ok = _swa_window(i, tq, row)
        kk, vv = k_ref[pl.ds(start, span), :], v_ref[pl.ds(start, span), :]
        q_all, o_all, do_all, lse_all = q_ref[...], o_ref[...].astype(F32), do_ref[...].astype(F32), lse_ref[0]
        dqs, dk_c, dv_c = [], None, None
        for h in range(SWA_G):
            sl = slice(h * hd, (h + 1) * hd)
            qh = q_all[:, sl] * scale
            doh = do_all[:, sl]
            delta = jnp.sum(doh * o_all[:, sl], axis=1, keepdims=True)
            lse_h = lse_all[:, h:h + 1]
            doh = doh.astype(BF16)
            s = jnp.where(ok, lax.dot_general(qh, kk, NT, preferred_element_type=F32), NEG)
            p = jnp.exp(s - lse_h)
            ds = p * (lax.dot_general(doh, vv, NT, preferred_element_type=F32) - delta)
            dqs.append(jnp.dot(ds.astype(BF16), kk, preferred_element_type=F32) * scale)
            dk_h = jnp.dot(ds.T.astype(BF16), qh, preferred_element_type=F32)
            dv_h = jnp.dot(p.T.astype(BF16), doh, preferred_element_type=F32)
            dk_c = dk_h if dk_c is None else dk_c + dk_h
            dv_c = dv_h if dv_c is None else dv_c + dv_h
            p_sink = jnp.exp(sink_ref[g * SWA_G + h] - lse_h)
            dsink_ref[0, h:h + 1, :] += jnp.broadcast_to(-jnp.sum(p_sink * delta, axis=0, keepdims=True), (1, LANES))
        dq_ref[...] = jnp.concatenate(dqs, axis=1)
        dk_ref[pl.ds(start, span), :] += dk_c
        dv_ref[pl.ds(start, span), :] += dv_c

    wq = SWA_G * hd
    qb = pl.BlockSpec((tq, wq), lambda g, i: (i, g))
    kb = pl.BlockSpec((S, hd), lambda g, i: (g, 0))
    block_bytes = 2 * tq * wq * (2 + 2 + 4 + 4) + 2 * S * hd * (2 + 2 + 4 + 4) + 4 * tq * LANES
    return pl.pallas_call(
        body, name=name, grid=(SWA_KV_HEADS, S // tq),
        in_specs=[pl.BlockSpec(memory_space=pltpu.SMEM), qb, kb, kb, qb, qb, pl.BlockSpec((1, tq, LANES), lambda g, i: (g, i, 0))],
        out_specs=[qb, kb, kb, pl.BlockSpec((1, 8, LANES), lambda g, i: (g, 0, 0))],
        out_shape=[jax.ShapeDtypeStruct(q.shape, F32), jax.ShapeDtypeStruct(k.shape, F32), jax.ShapeDtypeStruct(v.shape, F32),
                   jax.ShapeDtypeStruct((SWA_KV_HEADS, 8, LANES), F32)],
        compiler_params=_cparams(("parallel", "arbitrary"), block_bytes),
    )(sinks, q, k, v, o, do, lse)


def _split_dot(x, u2):
    hi = x.astype(BF16)
    lo = (x - hi.astype(F32)).astype(BF16)
    return jnp.dot(jnp.concatenate([hi, lo], axis=1), u2, preferred_element_type=F32)


def _suffix_ones():
    r = lax.broadcasted_iota(jnp.int32, (2 * LANES, 2 * LANES), 0) % LANES
    c = lax.broadcasted_iota(jnp.int32, (2 * LANES, 2 * LANES), 1)
    return ((r > c) | (c >= LANES)).astype(BF16)


def _suffix_scan(x, uo, run):
    nc = x.shape[1] // LANES
    out = [None] * nc
    for c in reversed(range(nc)):
        st = _split_dot(x[:, c * LANES:(c + 1) * LANES], uo)
        out[c] = st[:, :LANES] + run
        run = run + st[:, LANES:]
    return (out[0] if nc == 1 else jnp.concatenate(out, axis=1)), run


SB_DEAD_LOG = -110.0


def _sb_logits(qh, kk, r0, row, masked):
    z = lax.dot_general(qh, kk, NT, preferred_element_type=F32)
    nz = -z
    l = jnp.minimum(nz, 0.0) - jnp.log(1.0 + jnp.exp(jnp.minimum(z, nz)))
    ok = None
    if masked:
        col = r0 + lax.broadcasted_iota(jnp.int32, (1, kk.shape[0]), 1)
        ok = col < row
        l = jnp.where(ok, l, 0.0)
    return z, l, ok


def _sb_walk(blk, carry, i, tq, ts):
    carry = blk(pl.multiple_of(i * tq, tq), tq, carry, True)

    def live(c):
        t, heads = c
        top = jnp.max(heads[0][0])
        for h in heads[1:]:
            top = jnp.maximum(top, jnp.max(h[0]))
        return jnp.logical_and(t < i * (tq // ts), top > SB_DEAD_LOG)

    def step(c):
        t, heads = c
        return t + 1, blk(pl.multiple_of(i * tq - (t + 1) * ts, ts), ts, heads, False)

    return lax.while_loop(live, step, (jnp.int32(0), carry))[1]


def _sb_fwd(name, pb, *, q_cb, k_cb, v_cb, tq, ts, scale):
    S = pb.shape[0]
    nq = S // tq
    hd = SB_HEAD_DIM
    groups = SB_HEADS * hd // LANES
    nh = LANES // hd

    def body(q_ref, k_ref, v_ref, o16_ref, o32_ref):
        i = pl.program_id(1)
        q_all = q_ref[...]
        row = i * tq + lax.broadcasted_iota(jnp.int32, (tq, 1), 0)
        uo = _suffix_ones()
        sls = [slice(h * hd, (h + 1) * hd) for h in range(nh)]
        qhs = [q_all[:, sl] * scale for sl in sls]

        def blk(r0, width, carry, masked):
            k_all = k_ref[pl.ds(r0, width), :]
            v_all = v_ref[pl.ds(r0, width), :]
            new = []
            for qh, sl, (run_l, acc) in zip(qhs, sls, carry):
                z, l, ok = _sb_logits(qh, k_all[:, sl], r0, row, masked)
                tail, run_l = _suffix_scan(l, uo, run_l)
                e = z + l + tail
                if masked:
                    e = jnp.where(ok, e, NEG)
                new.append((run_l, acc + jnp.dot(jnp.exp(e).astype(BF16), v_all[:, sl], preferred_element_type=F32)))
            return tuple(new)

        carry = tuple((jnp.zeros((tq, LANES), F32), jnp.zeros((tq, hd), F32)) for _ in range(nh))
        carry = _sb_walk(blk, carry, i, tq, ts)
        o = jnp.concatenate([acc for _, acc in carry], axis=1)
        o16_ref[...] = o.astype(BF16)
        o32_ref[...] = o

    block_bytes = 2 * tq * LANES + 2 * 2 * S * LANES + 6 * tq * LANES
    return pl.pallas_call(
        body, name=name, grid=(groups, nq),
        in_specs=[pl.BlockSpec((tq, LANES), lambda g, i: (i, q_cb + g)), pl.BlockSpec((S, LANES), lambda g, i: (0, k_cb + g)),
                  pl.BlockSpec((S, LANES), lambda g, i: (0, v_cb + g))],
        out_specs=[pl.BlockSpec((tq, LANES), lambda g, i: (i, g)), pl.BlockSpec((tq, LANES), lambda g, i: (i, g))],
        out_shape=[jax.ShapeDtypeStruct((S, groups * LANES), BF16), jax.ShapeDtypeStruct((S, groups * LANES), F32)],
        compiler_params=_cparams(("parallel", "arbitrary"), block_bytes),
    )(pb, pb, pb)


def _sb_bwd(name, pb, o32, do, *, q_cb, k_cb, v_cb, tq, ts, scale):
    S = pb.shape[0]
    nq = S // tq
    hd = SB_HEAD_DIM
    groups = SB_HEADS * hd // LANES
    nh = LANES // hd

    def body(q_ref, k_ref, v_ref, o_ref, do_ref, dq_ref, dk_ref, dv_ref):
        i = pl.program_id(1)

        @pl.when(i == 0)
        def _():
            dk_ref[...] = jnp.zeros_like(dk_ref)
            dv_ref[...] = jnp.zeros_like(dv_ref)

        q_all = q_ref[...]
        o_all = o_ref[...]
        do_all = do_ref[...].astype(F32)
        row = i * tq + lax.broadcasted_iota(jnp.int32, (tq, 1), 0)
        uo = _suffix_ones()
        per_head = []
        for h in range(nh):
            sl = slice(h * hd, (h + 1) * hd)
            doh = do_all[:, sl].astype(BF16)
            total = jnp.sum(doh.astype(F32) * o_all[:, sl], axis=1, keepdims=True)
            per_head.append((q_all[:, sl] * scale, doh, jnp.broadcast_to(total, (tq, LANES)), sl))

        def blk(r0, width, carry, masked):
            k_all = k_ref[pl.ds(r0, width), :]
            v_all = v_ref[pl.ds(r0, width), :]
            new, dk_c, dv_c = [], [], []
            for (qh, doh, total, sl), (run_l, run_g, dq) in zip(per_head, carry):
                kk, vv = k_all[:, sl], v_all[:, sl]
                z, l, ok = _sb_logits(qh, kk, r0, row, masked)
                tail, run_l = _suffix_scan(l, uo, run_l)
                e = z + l
                beta = jnp.exp(e)
                e = e + tail
                if masked:
                    e = jnp.where(ok, e, NEG)
                a = jnp.exp(e).astype(BF16).astype(F32)
                gr = lax.dot_general(doh, vv, NT, preferred_element_type=F32) * a
                right, run_g = _suffix_scan(gr, uo, run_g)
                nc = width // LANES
                prefix = (total if nc == 1 else jnp.tile(total, (1, nc))) - right
                dz = gr - beta * prefix
                if masked:
                    dz = jnp.where(ok, dz, 0.0)
                new.append((run_l, run_g, dq + jnp.dot(dz.astype(BF16), kk, preferred_element_type=F32)))
                dk_c.append(jnp.dot(dz.T.astype(BF16), qh, preferred_element_type=F32))
                dv_c.append(jnp.dot(a.T.astype(BF16), doh, preferred_element_type=F32))
            dk_ref[pl.ds(r0, width), :] += jnp.concatenate(dk_c, axis=1)
            dv_ref[pl.ds(r0, width), :] += jnp.concatenate(dv_c, axis=1)
            return tuple(new)

        zc = jnp.zeros((tq, LANES), F32)
        carry = tuple((zc, zc, jnp.zeros((tq, hd), F32)) for _ in range(nh))
        carry = _sb_walk(blk, carry, i, tq, ts)
        dq_ref[...] = jnp.concatenate([c[2] * scale for c in carry], axis=1)

    W = groups * LANES
    block_bytes = 2 * tq * LANES + 2 * 2 * S * LANES + 3 * 4 * tq * LANES + 2 * 4 * S * LANES
    return pl.pallas_call(
        body, name=name, grid=(groups, nq),
        in_specs=[pl.BlockSpec((tq, LANES), lambda g, i: (i, q_cb + g)), pl.BlockSpec((S, LANES), lambda g, i: (0, k_cb + g)),
                  pl.BlockSpec((S, LANES), lambda g, i: (0, v_cb + g)), pl.BlockSpec((tq, LANES), lambda g, i: (i, g)),
                  pl.BlockSpec((tq, LANES), lambda g, i: (i, g))],
        out_specs=[pl.BlockSpec((tq, LANES), lambda g, i: (i, g)), pl.BlockSpec((S, LANES), lambda g, i: (0, g)),
                   pl.BlockSpec((S, LANES), lambda g, i: (0, g))],
        out_shape=[jax.ShapeDtypeStruct((S, W), F32)] * 3,
        compiler_params=_cparams(("parallel", "arbitrary"), block_bytes),
    )(pb, pb, pb, o32, do)


def _exchange_copies(src_refs, out_refs, send_sems, recv_sems, local_sems, per_peer):
    x, y, c = lax.axis_index("x"), lax.axis_index("y"), lax.axis_index("c")
    me = 4 * x + 2 * y + c
    n = len(src_refs)

    def copy(j, k):
        px, py, pc = x ^ (k >> 2), y ^ ((k >> 1) & 1), c ^ (k & 1)
        s = src_refs[j].at[4 * px + 2 * py + pc] if per_peer else src_refs[j]
        return pltpu.make_async_remote_copy(
            src_ref=s, dst_ref=out_refs[j].at[me], send_sem=send_sems.at[j, k - 1], recv_sem=recv_sems.at[j, k - 1],
            device_id=(px, py, pc), device_id_type=pl.DeviceIdType.MESH)

    mine = [pltpu.make_async_copy(src_refs[j].at[me] if per_peer else src_refs[j], out_refs[j].at[me], local_sems.at[j]) for j in range(n)]
    return mine, [copy(j, k) for k in range(1, N_DEV) for j in range(n)]


def _exchange_start(*refs, per_peer):
    mine, copies = _exchange_copies(*refs, per_peer)
    for cp in mine + copies:
        cp.start()


def _exchange_wait(*refs, per_peer):
    mine, copies = _exchange_copies(*refs, per_peer)
    for cp in copies:
        cp.wait_recv()
    for cp in copies:
        cp.wait_send()
    for cp in mine:
        cp.wait()


def _exchange_specs(srcs, per_peer):
    n = len(srcs)
    out_shape = [jax.ShapeDtypeStruct((N_DEV,) + tuple(s.shape[1:] if per_peer else s.shape), s.dtype) for s in srcs]
    sems = [pltpu.SemaphoreType.DMA((n, N_DEV - 1)), pltpu.SemaphoreType.DMA((n, N_DEV - 1)), pltpu.SemaphoreType.DMA((n,))]
    return [pl.BlockSpec(memory_space=pltpu.HBM)] * n, out_shape, sems


def _exchange(name, srcs, per_peer):
    n = len(srcs)

    def body(*refs):
        parts = (refs[:n], refs[n:2 * n], *refs[2 * n:])
        _exchange_start(*parts, per_peer=per_peer)
        _exchange_wait(*parts, per_peer=per_peer)

    hbm, out_shape, sems = _exchange_specs(srcs, per_peer)
    return pl.pallas_call(body, name=name, in_specs=hbm, out_specs=hbm, out_shape=out_shape, scratch_shapes=sems)(*srcs)


def _pallas(body, *, name, grid, in_specs, out_specs, out_shape, args, sem, block_bytes, ride=None):
    if ride is None:
        return pl.pallas_call(body, name=name, grid=grid, in_specs=in_specs, out_specs=out_specs, out_shape=out_shape,
                              compiler_params=_cparams(sem, block_bytes))(*args), None
    srcs, per_peer = ride
    n, n_in, n_out = len(srcs), len(in_specs), len(out_specs)

    def riding(*refs):
        ins, xsrc = refs[:n_in], refs[n_in:n_in + n]
        outs, xout = refs[n_in + n:n_in + n + n_out], refs[n_in + n + n_out:n_in + 2 * n + n_out]
        parts = (xsrc, xout, *refs[n_in + 2 * n + n_out:])
        ids = [pl.program_id(d) for d in range(len(grid))]
        first = functools.reduce(jnp.logical_and, [i == 0 for i in ids])
        last = functools.reduce(jnp.logical_and, [i == g - 1 for i, g in zip(ids, grid)])
        pl.when(first)(functools.partial(_exchange_start, *parts, per_peer=per_peer))
        body(*ins, *outs)
        pl.when(last)(functools.partial(_exchange_wait, *parts, per_peer=per_peer))

    hbm, x_shape, sems = _exchange_specs(srcs, per_peer)
    res = pl.pallas_call(riding, name=name, grid=grid, in_specs=list(in_specs) + hbm, out_specs=list(out_specs) + hbm,
                         out_shape=list(out_shape) + x_shape, scratch_shapes=sems,
                         compiler_params=_cparams(("arbitrary",) * len(grid), block_bytes))(*args, *srcs)
    return res[:n_out], res[n_out:]


def _adamw(name, parts, w, m, v, *, tile):
    R, C = w.shape

    def body(p_ref, w_ref, m_ref, v_ref, g_ref, d_ref, nm_ref, nv_ref):
        g = p_ref[0].astype(F32)
        for d in range(1, N_DEV):
            g = g + p_ref[d].astype(F32)
        wv = w_ref[...]
        mm = ADAM_B1 * m_ref[...] + (1.0 - ADAM_B1) * g
        vv = ADAM_B2 * v_ref[...] + (1.0 - ADAM_B2) * jnp.square(g)
        m_hat = mm / (1.0 - ADAM_B1 ** ADAM_STEP)
        v_hat = vv / (1.0 - ADAM_B2 ** ADAM_STEP)
        g_ref[...] = g
        d_ref[...] = -ADAM_LR * (m_hat / (jnp.sqrt(v_hat) + ADAM_EPS) + ADAM_WD * wv)
        nm_ref[...] = mm
        nv_ref[...] = vv

    blk = pl.BlockSpec((tile, C), lambda i: (i, 0))
    block_bytes = N_DEV * _nbytes((tile, C), parts.dtype) + 7 * _nbytes((tile, C), F32)
    return pl.pallas_call(
        body, name=name, grid=(R // tile,),
        in_specs=[pl.BlockSpec((N_DEV, tile, C), lambda i: (0, i, 0)), blk, blk, blk],
        out_specs=[blk] * 4, out_shape=[jax.ShapeDtypeStruct((R, C), F32)] * 4,
        compiler_params=_cparams(("parallel",), block_bytes),
    )(parts, w, m, v)


def _perm_w_in(w):
    cols = [w[:, ORIG_COLS[n][0]:ORIG_COLS[n][0] + ORIG_COLS[n][1]] for n in PERM_ORDER]
    cols.append(jnp.zeros((w.shape[0], IN_WIDTH_P - IN_WIDTH), w.dtype))
    return jnp.concatenate(cols, axis=1)


def _unperm_w_in(wp):
    order = sorted(ORIG_COLS, key=lambda n: ORIG_COLS[n][0])
    return jnp.concatenate([wp[:, PERM_OFF[n]:PERM_OFF[n] + ORIG_COLS[n][1]] for n in order], axis=1)


def _perm_w_uq(w):
    w3 = w.reshape(w.shape[0], MLA_HEADS, MLA_NOPE + MLA_ROPE)
    return jnp.pad(w3, ((0, 0), (0, 0), (0, LANES - MLA_NOPE - MLA_ROPE))).reshape(w.shape[0], MLA_QW)


def _unperm_w_uq(wp):
    return wp.reshape(wp.shape[0], MLA_HEADS, LANES)[:, :, :MLA_NOPE + MLA_ROPE].reshape(wp.shape[0], -1)


def _perm_w_ukv(w):
    w3 = w.reshape(w.shape[0], MLA_HEADS, MLA_NOPE + MLA_V)
    kp = jnp.pad(w3[:, :, :MLA_NOPE], ((0, 0), (0, 0), (0, LANES - MLA_NOPE))).reshape(w.shape[0], MLA_HEADS * LANES)
    return jnp.concatenate([kp, w3[:, :, MLA_NOPE:].reshape(w.shape[0], MLA_HEADS * MLA_V)], axis=1)


def _unperm_w_ukv(wp):
    n = wp.shape[0]
    kp = wp[:, :MLA_HEADS * LANES].reshape(n, MLA_HEADS, LANES)[:, :, :MLA_NOPE]
    vp = wp[:, MLA_HEADS * LANES:].reshape(n, MLA_HEADS, MLA_V)
    return jnp.concatenate([kp, vp], axis=2).reshape(n, MLA_HEADS * (MLA_NOPE + MLA_V))


BIG = ("w_in", "w_uq", "w_ukv", "w_o_mla", "w_o_swa", "w_o_sb", "w_out", "w_up", "w_down")
ROW_SHARDED = ("w_out", "w_down")
SMALL = ("g_mix_pre", "b_gate", "g_q_lat", "g_kv_lat", "swa_sinks", "g_mix_post", "g_mlp_pre", "g_mlp_post")
PACK_COLS = 1024


def _to_shards(name, full):
    L, R, C = full.shape
    if name in ROW_SHARDED:
        return full.reshape(L, N_DEV, R // N_DEV, C).transpose(1, 0, 2, 3).reshape(N_DEV, L * R // N_DEV, C)
    return full.reshape(L, R, N_DEV, C // N_DEV).transpose(2, 0, 1, 3).reshape(N_DEV, L * R, C // N_DEV)


def _from_shards(name, gathered, shard_shape):
    L, r, c = shard_shape
    a = gathered.reshape(N_DEV, L, r, c)
    if name in ROW_SHARDED:
        return a.transpose(1, 0, 2, 3).reshape(L, N_DEV * r, c)
    return a.transpose(1, 2, 0, 3).reshape(L, r, N_DEV * c)


def _tables(positions):
    pos = positions.astype(F32).reshape(-1, 1)

    def cs(d):
        inv = 1.0 / (ROPE_THETA ** (jnp.arange(0, d, 2, dtype=F32) / d))
        ang = pos * inv
        c, s = jnp.cos(ang), jnp.sin(ang)
        return jnp.concatenate([c, c], axis=1), jnp.concatenate([-s, s], axis=1)

    c64, s64 = cs(SWA_HEAD_DIM)
    c32, s32 = cs(MLA_ROPE)
    n = pos.shape[0]
    one, zero = jnp.ones((n, MLA_NOPE), F32), jnp.zeros((n, MLA_NOPE), F32)
    pad0 = jnp.zeros((n, LANES - MLA_NOPE - MLA_ROPE), F32)
    cq = jnp.concatenate([one, c32, pad0], axis=1)
    sq = jnp.concatenate([zero, s32, pad0], axis=1)
    padk = jnp.zeros((n, LANES - MLA_ROPE), F32)
    return dict(
        c_swa_q=jnp.tile(c64, (1, SWA_HEADS)), s_swa_q=jnp.tile(s64, (1, SWA_HEADS)),
        c_swa_k=jnp.tile(c64, (1, SWA_KV_HEADS)), s_swa_k=jnp.tile(s64, (1, SWA_KV_HEADS)),
        c_mla_q=jnp.tile(cq, (1, MLA_HEADS)), s_mla_q=jnp.tile(sq, (1, MLA_HEADS)),
        c_mla_k=jnp.concatenate([c32, padk], axis=1), s_mla_k=jnp.concatenate([s32, padk], axis=1))


def _cb(name):
    return PERM_OFF[name] // PERM_WIDTH[name]


MLA_SCALE = (MLA_NOPE + MLA_ROPE) ** -0.5
MLA_SPEC = dict(groups=MLA_HEADS // 2, tq=512, tk=512,
                heads=[(slice(h * LANES, (h + 1) * LANES), slice(h * LANES, (h + 1) * LANES), slice(h * MLA_V, (h + 1) * MLA_V)) for h in range(2)])
SWA_G = SWA_HEADS // SWA_KV_HEADS
SWA_TQ = 256
SB_SPEC = dict(tq=256, ts=128, scale=SB_HEAD_DIM ** -0.5, q_cb=PERM_OFF["q_sb"] // LANES, k_cb=PERM_OFF["k_sb"] // LANES,
               v_cb=PERM_OFF["v_sb"] // LANES)


def _mla_specs(S):
    tq = MLA_SPEC["tq"]
    return dict(q_spec=((tq, 2 * LANES), lambda g, i: (i, g)), k_spec=((S, 2 * LANES), lambda g, i: (0, g)),
                v_spec=((S, 2 * MLA_V), lambda g, i: (0, g)))


def _layer_fwd(l, x, W, P, tb, T, ride):
    S = x.shape[0]
    nt = S // T
    h, = _rowwise(f"l{l}_norm_in", lambda xv, g: (_rms(xv, g),), [_ri(x, T), _bi(P["g_mix_pre"])], [_ro(S, D_MODEL, BF16, T)], n_tiles=nt)
    proj, proj16 = _mm_nn(f"l{l}_mm_in", h, W["w_in"], [F32, BF16], tm=512, tn=IN_WIDTH_P // 2)

    def mix_prep(cq, ckv, qs, ks, vs, gq, gkv, cq_t, sq_t, ck_t, sk_t):
        ksr = _rope(ks, ck_t, sk_t, SWA_HEAD_DIM // 2)
        hd = SWA_HEAD_DIM
        return (_rms(cq, gq), _rms(ckv, gkv), _rope(qs, cq_t, sq_t, hd // 2),
                [ksr[:, :hd], ksr[:, hd:]], [vs[:, :hd], vs[:, hd:]])

    kv3 = lambda dt: ((SWA_KV_HEADS, S, SWA_HEAD_DIM), dt, (SWA_KV_HEADS, T, SWA_HEAD_DIM), lambda i: (0, i, 0))
    cqn, ckvn, q_swa, k_swa, v_swa = _rowwise(
        f"l{l}_mix_prep", mix_prep,
        [_ri(proj, T, 256, _cb("c_q")), _ri(proj, T, 128, _cb("c_kv")), _ri(proj, T, 512, _cb("q_swa")), _ri(proj, T, 128, _cb("k_swa")),
         _ri(proj, T, 128, _cb("v_swa")), _bi(P["g_q_lat"]), _bi(P["g_kv_lat"]), _ri(tb["c_swa_q"], T), _ri(tb["s_swa_q"], T),
         _ri(tb["c_swa_k"], T), _ri(tb["s_swa_k"], T)],
        [_ro(S, 256, BF16, T), _ro(S, 128, BF16, T), _ro(S, 512, BF16, T), kv3(BF16), kv3(BF16)], n_tiles=nt)
    k_swa = k_swa.reshape(SWA_KV_HEADS * S, SWA_HEAD_DIM)
    v_swa = v_swa.reshape(SWA_KV_HEADS * S, SWA_HEAD_DIM)
    q_lat, = _mm_nn(f"l{l}_mm_uq", cqn, W["w_uq"], [F32], tm=1024, tn=MLA_QW)
    kv_lat, = _mm_nn(f"l{l}_mm_ukv", ckvn, W["w_ukv"], [F32], tm=1024, tn=MLA_KVW)

    def mla_prep(q, kk, vv, kr, cq_t, sq_t, ck_t, sk_t):
        kpe = pltpu.roll(_rope(kr, ck_t, sk_t, MLA_ROPE // 2), MLA_NOPE, axis=1)
        return _rope(q, cq_t, sq_t, MLA_ROPE // 2) * MLA_SCALE, kk + jnp.tile(kpe, (1, MLA_HEADS)), vv

    q_mla, k_mla, v_mla = _rowwise(
        f"l{l}_mla_prep", mla_prep,
        [_ri(q_lat, T), _ri(kv_lat, T, MLA_HEADS * LANES, 0), _ri(kv_lat, T, MLA_HEADS * MLA_V, 2), _ri(proj, T, 128, _cb("k_rope")),
         _ri(tb["c_mla_q"], T), _ri(tb["s_mla_q"], T), _ri(tb["c_mla_k"], T), _ri(tb["s_mla_k"], T)],
        [_ro(S, MLA_QW, BF16, T), _ro(S, MLA_HEADS * LANES, BF16, T), _ro(S, MLA_HEADS * MLA_V, BF16, T)], n_tiles=nt)

    (att_a, lse_a), rode = _softmax_fwd(f"l{l}_mla_fwd", q_mla, k_mla, v_mla, o_width=MLA_HEADS * MLA_V, ride=ride, **MLA_SPEC, **_mla_specs(S))
    att_b, lse_b = _swa_fwd(f"l{l}_swa_fwd", q_swa, k_swa, v_swa, P["swa_sinks"], tq=SWA_TQ)
    att_c, att_c32 = _sb_fwd(f"l{l}_sb_fwd", proj16, **SB_SPEC)
    o_a, = _mm_nn(f"l{l}_mm_oa", att_a, W["w_o_mla"], [F32], tm=1024, tn=D_MODEL)
    o_b, = _mm_nn(f"l{l}_mm_ob", att_b, W["w_o_swa"], [F32], tm=1024, tn=D_MODEL)
    o_c, = _mm_nn(f"l{l}_mm_oc", att_c, W["w_o_sb"], [F32], tm=1024, tn=D_MODEL)

    def gate_mix(gl, b, oa, ob, oc):
        gt = jax.nn.sigmoid(gl + b)
        return (gt[:, :D_MODEL] * oa + gt[:, D_MODEL:2 * D_MODEL] * ob + gt[:, 2 * D_MODEL:] * oc,)

    mixed, = _rowwise(f"l{l}_gate_mix", gate_mix, [_ri(proj, T, 3072, 0), _bi(P["b_gate"]), _ri(o_a, T), _ri(o_b, T), _ri(o_c, T)],
                      [_ro(S, D_MODEL, BF16, T)], n_tiles=nt)
    y, = _mm_nn(f"l{l}_mm_out", mixed, W["w_out"], [F32], tm=1024, tn=D_MODEL)

    def resid_norm(xv, yv, gpost, gpre):
        x1 = xv + _rms(yv, gpost)
        return x1, _rms(x1, gpre)

    x1, h2 = _rowwise(f"l{l}_resid_norm", resid_norm, [_ri(x, T), _ri(y, T), _bi(P["g_mix_post"]), _bi(P["g_mlp_pre"])],
                      [_ro(S, D_MODEL, F32, T), _ro(S, D_MODEL, BF16, T)], n_tiles=nt)
    up, u = _mm_nn(f"l{l}_mm_up", h2, W["w_up"], [F32, BF16], tm=512, tn=2048,
                   epilogue=lambda acc: (acc, jnp.square(jnp.maximum(acc, 0.0))))
    dn, = _mm_nn(f"l{l}_mm_down", u, W["w_down"], [F32], tm=512, tn=D_MODEL)
    x2, = _rowwise(f"l{l}_resid_out", lambda xv, dv, g: (xv + _rms(dv, g),), [_ri(x1, T), _ri(dn, T), _bi(P["g_mlp_post"])],
                   [_ro(S, D_MODEL, F32, T)], n_tiles=nt)
    saved = dict(x=x, h=h, proj=proj, proj16=proj16, cqn=cqn, ckvn=ckvn, q_swa=q_swa, k_swa=k_swa, v_swa=v_swa, q_mla=q_mla, k_mla=k_mla,
                 v_mla=v_mla, att_a=att_a, lse_a=lse_a, att_b=att_b, lse_b=lse_b, att_c=att_c, att_c32=att_c32, o_a=o_a, o_b=o_b, o_c=o_c,
                 mixed=mixed, y=y, x1=x1, h2=h2, up=up, u=u, dn=dn)
    return x2, saved, rode


def _layer_bwd(l, dx2, sv, W, P, tb, T, ride):
    S = dx2.shape[0]
    nt = S // T
    G = {}

    def post_norm_bwd(v, g, dy):
        return _rms_bwd(v, g, dy)

    d_dn, G["g_mlp_post"] = _rowwise(f"l{l}_b_post2", post_norm_bwd, [_ri(sv["dn"], T), _bi(P["g_mlp_post"]), _ri(dx2, T)],
                                    [_ro(S, D_MODEL, BF16, T)], [D_MODEL], n_tiles=nt)
    d_up, = _mm_nt(f"l{l}_b_mm_down", d_dn, W["w_down"], [BF16], tm=512, tn=2048, extras=[sv["up"]],
                   epilogue=lambda acc, upv: (acc * (2.0 * jnp.maximum(upv, 0.0)),))
    G["w_down"] = _mm_tn(f"l{l}_g_down", sv["u"], d_dn, tm=2048, tn=D_MODEL, ts=512)
    d_h2, = _mm_nt(f"l{l}_b_mm_up", d_up, W["w_up"], [F32], tm=512, tn=D_MODEL)
    G["w_up"] = _mm_tn(f"l{l}_g_up", sv["h2"], d_up, tm=D_MODEL, tn=2048, ts=512)

    def pre_norm_bwd(v, g, dy, dres):
        dx, dg = _rms_bwd(v, g, dy)
        return dres + dx, dg

    dx1, G["g_mlp_pre"] = _rowwise(f"l{l}_b_pre2", pre_norm_bwd, [_ri(sv["x1"], T), _bi(P["g_mlp_pre"]), _ri(d_h2, T), _ri(dx2, T)],
                                  [_ro(S, D_MODEL, F32, T)], [D_MODEL], n_tiles=nt)
    d_y, G["g_mix_post"] = _rowwise(f"l{l}_b_post1", post_norm_bwd, [_ri(sv["y"], T), _bi(P["g_mix_post"]), _ri(dx1, T)],
                                   [_ro(S, D_MODEL, BF16, T)], [D_MODEL], n_tiles=nt)
    d_mixed, = _mm_nt(f"l{l}_b_mm_out", d_y, W["w_out"], [F32], tm=1024, tn=D_MODEL)
    G["w_out"] = _mm_tn(f"l{l}_g_out", sv["mixed"], d_y, tm=D_MODEL, tn=D_MODEL, ts=512)

    def gate_bwd(dm, gl, b, oa, ob, oc):
        gt = jax.nn.sigmoid(gl + b)
        outs, dgl = [], []
        for k, o in enumerate((oa, ob, oc)):
            gk = gt[:, k * D_MODEL:(k + 1) * D_MODEL]
            outs.append(dm * gk)
            dgl.append(dm * o * gk * (1.0 - gk))
        dgl = jnp.concatenate(dgl, axis=1)
        return (*outs, dgl, jnp.sum(dgl, axis=0, keepdims=True))

    d_oa, d_ob, d_oc, d_gl, G["b_gate"] = _rowwise(
        f"l{l}_b_gate", gate_bwd, [_ri(d_mixed, T), _ri(sv["proj"], T, 3072, 0), _bi(P["b_gate"]), _ri(sv["o_a"], T), _ri(sv["o_b"], T), _ri(sv["o_c"], T)],
        [_ro(S, D_MODEL, BF16, T)] * 3 + [_ro(S, 3 * D_MODEL, BF16, T)], [3 * D_MODEL], n_tiles=nt)
    d_att = {}
    for br, d_o, att in (("mla", d_oa, sv["att_a"]), ("swa", d_ob, sv["att_b"]), ("sb", d_oc, sv["att_c"])):
        d_att[br], = _mm_nt(f"l{l}_b_mm_o_{br}", d_o, W["w_o_" + br], [F32], tm=1024, tn=512)
        G["w_o_" + br] = _mm_tn(f"l{l}_g_o_{br}", att, d_o, tm=512, tn=D_MODEL, ts=512)

    (dq_mla, dk_mla, dv_mla), rode = _softmax_bwd(f"l{l}_mla_bwd", sv["q_mla"], sv["k_mla"], sv["v_mla"], sv["att_a"], d_att["mla"], sv["lse_a"],
                                                  scale=MLA_SCALE, ride=ride, **MLA_SPEC, **_mla_specs(S))
    dq_swa, dk_swa, dv_swa, dsink = _swa_bwd(f"l{l}_swa_bwd", sv["q_swa"], sv["k_swa"], sv["v_swa"], sv["att_b"], d_att["swa"], sv["lse_b"],
                                             P["swa_sinks"], tq=SWA_TQ)
    G["swa_sinks"] = dsink[:, :SWA_G, 0].reshape(1, SWA_HEADS)
    dq_sb, dk_sb, dv_sb = _sb_bwd(f"l{l}_sb_bwd", sv["proj16"], sv["att_c32"], d_att["sb"], **SB_SPEC)

    def mla_prep_bwd(dq, dk, dvv, cq_t, sq_t, ck_t, sk_t):
        dks = dk[:, :LANES]
        for hh in range(1, MLA_HEADS):
            dks = dks + dk[:, hh * LANES:(hh + 1) * LANES]
        d_kr = _rope_t(pltpu.roll(dks, LANES - MLA_NOPE, axis=1), ck_t, sk_t, MLA_ROPE // 2)
        return _rope_t(dq, cq_t, sq_t, MLA_ROPE // 2), jnp.concatenate([dk, dvv], axis=1), d_kr

    d_q_lat, d_kv_lat, d_krope = _rowwise(
        f"l{l}_b_mla_prep", mla_prep_bwd,
        [_ri(dq_mla, T), _ri(dk_mla, T), _ri(dv_mla, T), _ri(tb["c_mla_q"], T), _ri(tb["s_mla_q"], T), _ri(tb["c_mla_k"], T), _ri(tb["s_mla_k"], T)],
        [_ro(S, MLA_QW, BF16, T), _ro(S, MLA_KVW, BF16, T), _ro(S, LANES, BF16, T)], n_tiles=nt)
    d_cqn, = _mm_nt(f"l{l}_b_mm_uq", d_q_lat, W["w_uq"], [F32], tm=1024, tn=MLA_Q_LORA)
    G["w_uq"] = _mm_tn(f"l{l}_g_uq", sv["cqn"], d_q_lat, tm=MLA_Q_LORA, tn=MLA_QW, ts=512)
    d_ckvn, = _mm_nt(f"l{l}_b_mm_ukv", d_kv_lat, W["w_ukv"], [F32], tm=1024, tn=MLA_KV_LORA)
    G["w_ukv"] = _mm_tn(f"l{l}_g_ukv", sv["ckvn"], d_kv_lat, tm=MLA_KV_LORA, tn=MLA_KVW, ts=512)

    def mix_prep_bwd(cq, ckv, gq, gkv, dcqn, dckvn, dqs, dks, dvs, cq_t, sq_t, ck_t, sk_t):
        d_cq, dgq = _rms_bwd(cq, gq, dcqn)
        d_ckv, dgkv = _rms_bwd(ckv, gkv, dckvn)
        dk2 = jnp.concatenate([dks[0], dks[1]], axis=1)
        dv2 = jnp.concatenate([dvs[0], dvs[1]], axis=1)
        return (d_cq, d_ckv, _rope_t(dqs, cq_t, sq_t, SWA_HEAD_DIM // 2), _rope_t(dk2, ck_t, sk_t, SWA_HEAD_DIM // 2), dv2, dgq, dgkv)

    kv3 = lambda a: (a.reshape(SWA_KV_HEADS, S, SWA_HEAD_DIM), (SWA_KV_HEADS, T, SWA_HEAD_DIM), lambda i: (0, i, 0))
    d_cq, d_ckv, d_qswa, d_kswa, d_vswa, G["g_q_lat"], G["g_kv_lat"] = _rowwise(
        f"l{l}_b_mix_prep", mix_prep_bwd,
        [_ri(sv["proj"], T, 256, _cb("c_q")), _ri(sv["proj"], T, 128, _cb("c_kv")), _bi(P["g_q_lat"]), _bi(P["g_kv_lat"]), _ri(d_cqn, T), _ri(d_ckvn, T),
         _ri(dq_swa, T), kv3(dk_swa), kv3(dv_swa), _ri(tb["c_swa_q"], T), _ri(tb["s_swa_q"], T), _ri(tb["c_swa_k"], T), _ri(tb["s_swa_k"], T)],
        [_ro(S, 256, BF16, T), _ro(S, 128, BF16, T), _ro(S, 512, BF16, T), _ro(S, 128, BF16, T), _ro(S, 128, BF16, T)], [256, 128], n_tiles=nt)
    pieces = dict(gates=d_gl, q_swa=d_qswa, q_sb=dq_sb, k_sb=dk_sb, v_sb=dv_sb, c_q=d_cq, c_kv=d_ckv, k_swa=d_kswa, v_swa=d_vswa, k_rope=d_krope)
    d_proj = jnp.concatenate([pieces[n].astype(BF16) for n in PERM_ORDER], axis=1)
    d_h, = _mm_nt(f"l{l}_b_mm_in", d_proj, W["w_in"], [F32], tm=512, tn=512)
    G["w_in"] = _mm_tn(f"l{l}_g_in", sv["h"], d_proj, tm=D_MODEL, tn=IN_WIDTH_P // 2, ts=512)
    dx, G["g_mix_pre"] = _rowwise(f"l{l}_b_pre1", pre_norm_bwd, [_ri(sv["x"], T), _bi(P["g_mix_pre"]), _ri(d_h, T), _ri(dx1, T)],
                                 [_ro(S, D_MODEL, F32, T)], [D_MODEL], n_tiles=nt)
    return dx, G, rode


def _pack_rows(vecs, rows):
    flat = jnp.concatenate([v.reshape(-1) for v in vecs])
    return jnp.pad(flat, (0, rows * PACK_COLS - flat.shape[0])).reshape(rows, PACK_COLS)


def kernel(x, positions, g_mix_pre, w_in, b_gate, g_q_lat, g_kv_lat, w_uq, w_ukv, swa_sinks, w_o_mla, w_o_swa, w_o_sb, w_out, g_mix_post, g_mlp_pre, w_up, w_down, g_mlp_post, loss_target, m_g_mix_pre, m_w_in, m_b_gate, m_g_q_lat, m_g_kv_lat, m_w_uq, m_w_ukv, m_swa_sinks, m_w_o_mla, m_w_o_swa, m_w_o_sb, m_w_out, m_g_mix_post, m_g_mlp_pre, m_w_up, m_w_down, m_g_mlp_post, v_g_mix_pre, v_w_in, v_b_gate, v_g_q_lat, v_g_kv_lat, v_w_uq, v_w_ukv, v_swa_sinks, v_w_o_mla, v_w_o_swa, v_w_o_sb, v_w_out, v_g_mix_post, v_g_mlp_pre, v_w_up, v_w_down, v_g_mlp_post):
    a = dict(locals())
    S = x.shape[1]
    depth = w_in.shape[0]
    T = min(256, S)
    xs = x.reshape(S, D_MODEL)
    tb = _tables(positions)

    def weight_shards(l):
        return [a[n][l].astype(BF16) for n in BIG]

    gathered = _exchange("gather_weights", weight_shards(0), per_peer=False)
    layers, saved = [], []
    h = xs
    for l in range(depth):
        W = {n: _from_shards(n, g, (1,) + a[n].shape[1:])[0] for n, g in zip(BIG, gathered)}
        W["w_in"] = _perm_w_in(W["w_in"])
        W["w_uq"] = _perm_w_uq(W["w_uq"])
        W["w_ukv"] = _perm_w_ukv(W["w_ukv"])
        P = {n: a[n][l].reshape(1, -1) for n in SMALL if n != "swa_sinks"}
        P["swa_sinks"] = a["swa_sinks"][l]
        layers.append((W, P))
        h, sv, gathered = _layer_fwd(l, h, W, P, tb, T, (weight_shards(l + 1), False) if l + 1 < depth else None)
        saved.append(sv)

    def loss_head(yv, tv):
        err = yv - tv
        part = 0.5 * jnp.sum(jnp.mean(err * err, axis=1, keepdims=True), axis=0, keepdims=True)
        return err * (1.0 / D_MODEL), jnp.broadcast_to(part, (1, LANES))

    dh, loss_row = _rowwise("loss_head", loss_head, [_ri(h, T), _ri(loss_target.reshape(S, D_MODEL), T)], [_ro(S, D_MODEL, F32, T)], [LANES],
                            n_tiles=S // T)
    loss = lax.psum(loss_row[0, 0], ("x", "y", "c"))

    unperm = dict(w_in=_unperm_w_in, w_uq=_unperm_w_uq, w_ukv=_unperm_w_ukv)
    grads, recv, pending = [None] * depth, [None] * depth, None
    for l in reversed(range(depth)):
        W, P = layers[l]
        dh, grads[l], rode = _layer_bwd(l, dh, saved[l], W, P, tb, T, (pending, True) if pending is not None else None)
        if rode is not None:
            recv[l + 1] = rode
        pending = [_to_shards(n, unperm.get(n, lambda t: t)(grads[l][n]).astype(BF16)[None]) for n in BIG]
    recv[0] = _exchange("scatter_grads", pending, per_peer=True)
    grad_x = dh.reshape(x.shape)

    out = {}
    for j, n in enumerate(BIG):
        shp = a[n].shape
        rows, cols = shp[0] * shp[1], shp[2]
        parts = jnp.concatenate([recv[l][j] for l in range(depth)], axis=1)
        res = _adamw("adamw_" + n, parts, a[n].reshape(rows, cols), a["m_" + n].reshape(rows, cols), a["v_" + n].reshape(rows, cols),
                     tile=min(256, rows))
        out[n] = [r.reshape(shp) for r in res]

    small_total = sum(a[n].size for n in SMALL)
    small_rows = -(-small_total // (8 * PACK_COLS)) * 8
    sg = _pack_rows([jnp.stack([grads[l][n].reshape(-1) for l in range(depth)]) for n in SMALL], small_rows)
    sg_all, = _exchange("gather_small_grads", [sg], per_peer=False)
    res = _adamw("adamw_small", sg_all, _pack_rows([a[n] for n in SMALL], small_rows), _pack_rows([a["m_" + n] for n in SMALL], small_rows),
                 _pack_rows([a["v_" + n] for n in SMALL], small_rows), tile=small_rows)
    off = 0
    for n in SMALL:
        cnt = a[n].size
        out[n] = [r.reshape(-1)[off:off + cnt].reshape(a[n].shape) for r in res]
        off += cnt

    order = ("g_mix_pre", "w_in", "b_gate", "g_q_lat", "g_kv_lat", "w_uq", "w_ukv", "swa_sinks", "w_o_mla", "w_o_swa", "w_o_sb", "w_out",
             "g_mix_post", "g_mlp_pre", "w_up", "w_down", "g_mlp_post")
    return (loss, grad_x, *[out[n][0] for n in order], *[out[n][1] for n in order], *[out[n][2] for n in order], *[out[n][3] for n in order])
```

```python
import functools

import jax
import jax.numpy as jnp
from jax import lax
from jax.experimental import pallas as pl
from jax.experimental.pallas import tpu as pltpu

F32, BF16 = jnp.float32, jnp.bfloat16

D_MODEL = 1024
DEPTH = 4
MLA_HEADS, MLA_Q_LORA, MLA_KV_LORA, MLA_NOPE, MLA_ROPE, MLA_V = 8, 256, 128, 64, 32, 64
SWA_HEADS, SWA_KV_HEADS, SWA_HEAD_DIM, SWA_WINDOW = 8, 2, 64, 128
SB_HEADS, SB_HEAD_DIM = 8, 64
D_FF = 4 * D_MODEL
ROPE_THETA = 10000.0
EPS = 1e-6
N_DEV = 8
ADAM_LR, ADAM_B1, ADAM_B2, ADAM_EPS, ADAM_WD, ADAM_STEP = 0.001, 0.9, 0.999, 1e-08, 0.01, 10

LANES = 128
VMEM_LIMIT_MAX = 60 * 1024 * 1024
VMEM_LIMIT_MIN = 32 * 1024 * 1024

ORIG_COLS = dict(c_q=(0, 256), c_kv=(256, 128), k_rope=(384, 32), q_swa=(416, 512), k_swa=(928, 128), v_swa=(1056, 128),
                 q_sb=(1184, 512), k_sb=(1696, 512), v_sb=(2208, 512), gates=(2720, 3072))
IN_WIDTH = 5792
PERM_ORDER = ("gates", "q_swa", "q_sb", "k_sb", "v_sb", "c_q", "c_kv", "k_swa", "v_swa", "k_rope")
PERM_WIDTH = dict(gates=3072, q_swa=512, q_sb=512, k_sb=512, v_sb=512, c_q=256, c_kv=128, k_swa=128, v_swa=128, k_rope=128)
PERM_OFF = {}
_o = 0
for _n in PERM_ORDER:
    PERM_OFF[_n] = _o
    _o += PERM_WIDTH[_n]
IN_WIDTH_P = _o
MLA_QW = MLA_HEADS * LANES
MLA_KVW = MLA_HEADS * LANES + MLA_HEADS * MLA_V

NT = (((1,), (1,)), ((), ()))
TN = (((0,), (0,)), ((), ()))
NEG = -1e30


def _cparams(sem, block_bytes):
    limit = int(min(VMEM_LIMIT_MAX, max(VMEM_LIMIT_MIN, 2 * block_bytes + (16 << 20))))
    return pltpu.CompilerParams(dimension_semantics=sem, vmem_limit_bytes=limit)


def _nbytes(shape, dtype):
    n = 1
    for s in shape:
        n *= s
    return n * jnp.dtype(dtype).itemsize


def _ri(arr, tile, width=None, cb=0):
    width = arr.shape[1] if width is None else width
    return (arr, (tile, width), lambda i, cb=cb: (i, cb))


def _bi(arr):
    return (arr, arr.shape, lambda i: (0, 0))


def _ro(rows, width, dtype, tile):
    return ((rows, width), dtype, (tile, width), lambda i: (i, 0))


def _rowwise(name, fn, ins, outs, reds=(), *, n_tiles):
    n_in, n_out = len(ins), len(outs)

    def body(*refs):
        vals = fn(*[r[...] for r in refs[:n_in]])
        for r, v in zip(refs[n_in:n_in + n_out], vals[:n_out]):
            if isinstance(v, (list, tuple)):
                for j, vj in enumerate(v):
                    r[j] = vj.astype(r.dtype)
            else:
                r[...] = v.astype(r.dtype)
        if reds:
            @pl.when(pl.program_id(0) == 0)
            def _():
                for r in refs[n_in + n_out:]:
                    r[...] = jnp.zeros_like(r)
            for r, v in zip(refs[n_in + n_out:], vals[n_out:]):
                r[...] += v

    block_bytes = sum(_nbytes(b, a.dtype) for a, b, _ in ins) + sum(_nbytes(b, d) for _, d, b, _ in outs)
    res = pl.pallas_call(
        body, name=name, grid=(n_tiles,),
        in_specs=[pl.BlockSpec(b, m) for _, b, m in ins],
        out_specs=[pl.BlockSpec(b, m) for _, _, b, m in outs] + [pl.BlockSpec((1, w), lambda i: (0, 0)) for w in reds],
        out_shape=[jax.ShapeDtypeStruct(s, d) for s, d, _, _ in outs] + [jax.ShapeDtypeStruct((1, w), F32) for w in reds],
        compiler_params=_cparams(("arbitrary",) if reds else ("parallel",), block_bytes),
    )(*[a for a, _, _ in ins])
    return res


def _rms(x, g):
    r = lax.rsqrt(jnp.mean(x * x, axis=1, keepdims=True) + EPS)
    return x * r * g


def _rms_bwd(x, g, dy):
    r = lax.rsqrt(jnp.mean(x * x, axis=1, keepdims=True) + EPS)
    xn = x * r
    dxn = dy * g
    dx = r * (dxn - xn * jnp.mean(dxn * xn, axis=1, keepdims=True))
    return dx, jnp.sum(dy * xn, axis=0, keepdims=True)


def _swap_halves(x, half):
    n = x.shape[1]
    lane = lax.broadcasted_iota(jnp.int32, x.shape, 1)
    first = (lane % (2 * half)) < half
    return jnp.where(first, pltpu.roll(x, n - half, axis=1), pltpu.roll(x, half, axis=1))


def _rope(x, c, sg, half):
    return x * c + _swap_halves(x, half) * sg


def _rope_t(dy, c, sg, half):
    return dy * c - _swap_halves(dy, half) * sg


def _mm_nn(name, a, b, outs, *, tm, tn, extras=(), epilogue=None):
    M, K = a.shape
    N = b.shape[1]
    tm = min(tm, M)
    n_e = len(extras)

    def body(*refs):
        a_ref, b_ref = refs[:2]
        acc = jnp.dot(a_ref[...].astype(BF16), b_ref[...].astype(BF16), preferred_element_type=F32)
        vals = (acc,) * len(outs) if epilogue is None else epilogue(acc, *[r[...] for r in refs[2:2 + n_e]])
        for r, v in zip(refs[2 + n_e:], vals):
            r[...] = v.astype(r.dtype)

    block_bytes = (_nbytes((tm, K), a.dtype) + _nbytes((K, tn), b.dtype) + sum(_nbytes((tm, tn), e.dtype) for e in extras)
                   + sum(_nbytes((tm, tn), d) for d in outs) + _nbytes((tm, tn), F32))
    return pl.pallas_call(
        body, name=name, grid=(N // tn, M // tm),
        in_specs=[pl.BlockSpec((tm, K), lambda j, i: (i, 0)), pl.BlockSpec((K, tn), lambda j, i: (0, j))]
        + [pl.BlockSpec((tm, tn), lambda j, i: (i, j)) for _ in extras],
        out_specs=[pl.BlockSpec((tm, tn), lambda j, i: (i, j)) for _ in outs],
        out_shape=[jax.ShapeDtypeStruct((M, N), d) for d in outs],
        compiler_params=_cparams(("parallel", "parallel"), block_bytes),
    )(a, b, *extras)


def _mm_nt(name, a, b, outs, *, tm, tn, extras=(), epilogue=None):
    M, N = a.shape
    K = b.shape[0]
    tm = min(tm, M)
    n_e = len(extras)

    def body(*refs):
        a_ref, b_ref = refs[:2]
        acc = lax.dot_general(a_ref[...].astype(BF16), b_ref[...].astype(BF16), NT, preferred_element_type=F32)
        vals = (acc,) * len(outs) if epilogue is None else epilogue(acc, *[r[...] for r in refs[2:2 + n_e]])
        for r, v in zip(refs[2 + n_e:], vals):
            r[...] = v.astype(r.dtype)

    block_bytes = (_nbytes((tm, N), a.dtype) + _nbytes((tn, N), b.dtype) + sum(_nbytes((tm, tn), e.dtype) for e in extras)
                   + sum(_nbytes((tm, tn), d) for d in outs) + _nbytes((tm, tn), F32))
    return pl.pallas_call(
        body, name=name, grid=(K // tn, M // tm),
        in_specs=[pl.BlockSpec((tm, N), lambda j, i: (i, 0)), pl.BlockSpec((tn, N), lambda j, i: (j, 0))]
        + [pl.BlockSpec((tm, tn), lambda j, i: (i, j)) for _ in extras],
        out_specs=[pl.BlockSpec((tm, tn), lambda j, i: (i, j)) for _ in outs],
        out_shape=[jax.ShapeDtypeStruct((M, K), d) for d in outs],
        compiler_params=_cparams(("parallel", "parallel"), block_bytes),
    )(a, b, *extras)


def _mm_tn(name, a, b, *, tm, tn, ts):
    S, K = a.shape
    N = b.shape[1]
    ts = min(ts, S)

    def body(a_ref, b_ref, o_ref):
        @pl.when(pl.program_id(2) == 0)
        def _():
            o_ref[...] = jnp.zeros_like(o_ref)
        o_ref[...] += lax.dot_general(a_ref[...].astype(BF16), b_ref[...].astype(BF16), TN, preferred_element_type=F32)

    block_bytes = _nbytes((ts, tm), a.dtype) + _nbytes((ts, tn), b.dtype) + 2 * _nbytes((tm, tn), F32)
    return pl.pallas_call(
        body, name=name, grid=(K // tm, N // tn, S // ts),
        in_specs=[pl.BlockSpec((ts, tm), lambda i, j, s: (s, i)), pl.BlockSpec((ts, tn), lambda i, j, s: (s, j))],
        out_specs=pl.BlockSpec((tm, tn), lambda i, j, s: (i, j)),
        out_shape=jax.ShapeDtypeStruct((K, N), F32),
        compiler_params=_cparams(("parallel", "parallel", "arbitrary"), block_bytes),
    )(a, b)


def _lane_pack(cols, rows):
    lane = lax.broadcasted_iota(jnp.int32, (rows, LANES), 1)
    val = jnp.zeros((rows, LANES), F32)
    for h, c in enumerate(cols):
        val = jnp.where(lane == h, c, val)
    return val


def _mask(kb, row, tk):
    return kb * tk + lax.broadcasted_iota(jnp.int32, (1, tk), 1) <= row


def _kb_range(i, tq, tk):
    return (i * tq) // tk, ((i + 1) * tq + tk - 1) // tk


def _softmax_fwd(name, q, k, v, *, groups, heads, q_spec, k_spec, v_spec, o_width, tq, tk, ride=None):
    S = q.shape[0]
    nq = S // tq
    nh = len(heads)
    dv = heads[0][2].stop - heads[0][2].start

    def body(q_ref, k_ref, v_ref, o_ref, lse_ref):
        i = pl.program_id(1)
        q_all = q_ref[...]
        row = i * tq + lax.broadcasted_iota(jnp.int32, (tq, 1), 0)
        mid, hi = _kb_range(i, tq, tk)
        qhs = [q_all[:, qs] for qs, _, _ in heads]

        def blk(kb, carry, masked):
            r0 = pl.multiple_of(kb * tk, tk)
            k_all = k_ref[pl.ds(r0, tk), :]
            v_all = v_ref[pl.ds(r0, tk), :]
            if masked:
                ok = _mask(kb, row, tk)
            new = []
            for qh, (_, ks, vs), (m, l, acc) in zip(qhs, heads, carry):
                s = lax.dot_general(qh, k_all[:, ks], NT, preferred_element_type=F32)
                if masked:
                    s = jnp.where(ok, s, NEG)
                m_new = jnp.maximum(m, jnp.max(s, axis=1, keepdims=True))
                alpha = jnp.exp(m - m_new)
                p = jnp.exp(s - m_new)
                if masked:
                    p = jnp.where(ok, p, 0.0)
                l = alpha * l + jnp.sum(p, axis=1, keepdims=True)
                acc = alpha * acc + jnp.dot(p.astype(BF16), v_all[:, vs], preferred_element_type=F32)
                new.append((m_new, l, acc))
            return tuple(new)

        carry = tuple((jnp.full((tq, 1), NEG, F32), jnp.zeros((tq, 1), F32), jnp.zeros((tq, dv), F32)) for _ in range(nh))
        carry = lax.fori_loop(0, mid, functools.partial(blk, masked=False), carry)
        carry = lax.fori_loop(mid, hi, functools.partial(blk, masked=True), carry)
        o_ref[...] = jnp.concatenate([acc / l for _, l, acc in carry], axis=1).astype(o_ref.dtype)
        lse_ref[0] = _lane_pack([m + jnp.log(l) for m, l, _ in carry], tq)

    wo = nh * dv
    block_bytes = _nbytes(q_spec[0], q.dtype) + _nbytes(k_spec[0], k.dtype) + _nbytes(v_spec[0], v.dtype) + 4 * tq * (wo + LANES)
    return _pallas(
        body, name=name, grid=(groups, nq), in_specs=[pl.BlockSpec(*q_spec), pl.BlockSpec(*k_spec), pl.BlockSpec(*v_spec)],
        out_specs=[pl.BlockSpec((tq, wo), lambda g, i: (i, g)), pl.BlockSpec((1, tq, LANES), lambda g, i: (g, i, 0))],
        out_shape=[jax.ShapeDtypeStruct((S, o_width), BF16), jax.ShapeDtypeStruct((groups, S, LANES), F32)],
        args=[q, k, v], sem=("parallel", "arbitrary"), block_bytes=block_bytes, ride=ride)


def _softmax_bwd(name, q, k, v, o, do, lse, *, groups, heads, q_spec, k_spec, v_spec, tq, tk, scale, ride=None):
    S = q.shape[0]
    nq = S // tq
    nh = len(heads)
    dv = heads[0][2].stop - heads[0][2].start
    wo = nh * dv

    def body(q_ref, k_ref, v_ref, o_ref, do_ref, lse_ref, dq_ref, dk_ref, dv_ref):
        i = pl.program_id(1)

        @pl.when(i == 0)
        def _():
            dk_ref[...] = jnp.zeros_like(dk_ref)
            dv_ref[...] = jnp.zeros_like(dv_ref)

        q_all = q_ref[...]
        o_all = o_ref[...].astype(F32)
        do_all = do_ref[...].astype(F32)
        lse_all = lse_ref[0]
        row = i * tq + lax.broadcasted_iota(jnp.int32, (tq, 1), 0)
        mid, hi = _kb_range(i, tq, tk)
        per_head = []
        for h, (qs, ks, vs) in enumerate(heads):
            osl = slice(h * dv, (h + 1) * dv)
            doh = do_all[:, osl]
            delta = jnp.sum(doh * o_all[:, osl], axis=1, keepdims=True)
            per_head.append((q_all[:, qs], doh.astype(BF16), delta, lse_all[:, h:h + 1], ks, vs))

        def blk(kb, dqs, masked):
            r0 = pl.multiple_of(kb * tk, tk)
            k_all = k_ref[pl.ds(r0, tk), :]
            v_all = v_ref[pl.ds(r0, tk), :]
            if masked:
                ok = _mask(kb, row, tk)
            dk_parts, dv_parts, new_dqs = {}, {}, []
            for (qh, doh, delta, lse_h, ks, vs), dq in zip(per_head, dqs):
                kk, vv = k_all[:, ks], v_all[:, vs]
                s = lax.dot_general(qh, kk, NT, preferred_element_type=F32)
                p = jnp.exp(s - lse_h)
                if masked:
                    p = jnp.where(ok, p, 0.0)
                dp = lax.dot_general(doh, vv, NT, preferred_element_type=F32)
                ds = p * (dp - delta)
                new_dqs.append(dq + jnp.dot(ds.astype(BF16), kk, preferred_element_type=F32))
                dk_c = jnp.dot(ds.T.astype(BF16), qh, preferred_element_type=F32)
                dv_c = jnp.dot(p.T.astype(BF16), doh, preferred_element_type=F32)
                dk_parts[ks.start] = dk_parts[ks.start] + dk_c if ks.start in dk_parts else dk_c
                dv_parts[vs.start] = dv_parts[vs.start] + dv_c if vs.start in dv_parts else dv_c
            dk_ref[pl.ds(r0, tk), :] += jnp.concatenate([dk_parts[s0] for s0 in sorted(dk_parts)], axis=1)
            dv_ref[pl.ds(r0, tk), :] += jnp.concatenate([dv_parts[s0] for s0 in sorted(dv_parts)], axis=1)
            return tuple(new_dqs)

        dqs = tuple(jnp.zeros((tq, qs.stop - qs.start), F32) for qs, _, _ in heads)
        dqs = lax.fori_loop(0, mid, functools.partial(blk, masked=False), dqs)
        dqs = lax.fori_loop(mid, hi, functools.partial(blk, masked=True), dqs)
        dq_ref[...] = jnp.concatenate([dq * scale for dq in dqs], axis=1)

    in_specs = [pl.BlockSpec(*q_spec), pl.BlockSpec(*k_spec), pl.BlockSpec(*v_spec),
                pl.BlockSpec((tq, wo), lambda g, i: (i, g)), pl.BlockSpec((tq, wo), lambda g, i: (i, g)),
                pl.BlockSpec((1, tq, LANES), lambda g, i: (g, i, 0))]
    args = [q, k, v, o, do, lse]
    out_specs = [pl.BlockSpec(*q_spec), pl.BlockSpec(*k_spec), pl.BlockSpec(*v_spec)]
    out_shape = [jax.ShapeDtypeStruct(q.shape, F32), jax.ShapeDtypeStruct(k.shape, F32), jax.ShapeDtypeStruct(v.shape, F32)]
    block_bytes = (_nbytes(q_spec[0], q.dtype) + _nbytes(k_spec[0], k.dtype) + _nbytes(v_spec[0], v.dtype) + 6 * tq * wo + 4 * tq * LANES
                   + _nbytes(q_spec[0], F32) + _nbytes(k_spec[0], F32) + _nbytes(v_spec[0], F32))
    return _pallas(body, name=name, grid=(groups, nq), in_specs=in_specs, out_specs=out_specs, out_shape=out_shape, args=args,
                   sem=("parallel", "arbitrary"), block_bytes=block_bytes, ride=ride)


def _swa_window(i, tq, row):
    start = pl.multiple_of(jnp.maximum(i * tq - SWA_WINDOW, 0), SWA_WINDOW)
    col = start + lax.broadcasted_iota(jnp.int32, (1, tq + SWA_WINDOW), 1)
    return start, (col <= row) & ((row - col) < SWA_WINDOW)


def _swa_fwd(name, q, k, v, sinks, *, tq):
    S = q.shape[0]
    hd, span = SWA_HEAD_DIM, tq + SWA_WINDOW
    scale = hd ** -0.5

    def body(sink_ref, q_ref, k_ref, v_ref, o_ref, lse_ref):
        g, i = pl.program_id(0), pl.program_id(1)
        row = i * tq + lax.broadcasted_iota(jnp.int32, (tq, 1), 0)
        start, ok = _swa_window(i, tq, row)
        kk, vv = k_ref[pl.ds(start, span), :], v_ref[pl.ds(start, span), :]
        q_all = q_ref[...]
        outs, lses = [], []
        for h in range(SWA_G):
            s = lax.dot_general(q_all[:, h * hd:(h + 1) * hd] * scale, kk, NT, preferred_element_type=F32)
            s = jnp.where(ok, s, NEG)
            sink = sink_ref[g * SWA_G + h]
            m = jnp.maximum(jnp.max(s, axis=1, keepdims=True), sink)
            p = jnp.exp(s - m)
            l = jnp.sum(p, axis=1, keepdims=True) + jnp.exp(sink - m)
            outs.append(jnp.dot(p.astype(BF16), vv, preferred_element_type=F32) / l)
            lses.append(m + jnp.log(l))
        o_ref[...] = jnp.concatenate(outs, axis=1).astype(o_ref.dtype)
        lse_ref[0] = _lane_pack(lses, tq)

    wq = SWA_G * hd
    block_bytes = 2 * tq * wq * 2 + 2 * 2 * S * hd + 4 * tq * LANES
    return pl.pallas_call(
        body, name=name, grid=(SWA_KV_HEADS, S // tq),
        in_specs=[pl.BlockSpec(memory_space=pltpu.SMEM), pl.BlockSpec((tq, wq), lambda g, i: (i, g)),
                  pl.BlockSpec((S, hd), lambda g, i: (g, 0)), pl.BlockSpec((S, hd), lambda g, i: (g, 0))],
        out_specs=[pl.BlockSpec((tq, wq), lambda g, i: (i, g)), pl.BlockSpec((1, tq, LANES), lambda g, i: (g, i, 0))],
        out_shape=[jax.ShapeDtypeStruct((S, SWA_HEADS * hd), BF16), jax.ShapeDtypeStruct((SWA_KV_HEADS, S, LANES), F32)],
        compiler_params=_cparams(("parallel", "parallel"), block_bytes),
    )(sinks, q, k, v)


def _swa_bwd(name, q, k, v, o, do, lse, sinks, *, tq):
    S = q.shape[0]
    hd, span = SWA_HEAD_DIM, tq + SWA_WINDOW
    scale = hd ** -0.5

    def body(sink_ref, q_ref, k_ref, v_ref, o_ref, do_ref, lse_ref, dq_ref, dk_ref, dv_ref, dsink_ref):
        g, i = pl.program_id(0), pl.program_id(1)

        @pl.when(i == 0)
        def _():
            dk_ref[...] = jnp.zeros_like(dk_ref)
            dv_ref[...] = jnp.zeros_like(dv_ref)
            dsink_ref[...] = jnp.zeros_like(dsink_ref)

        row = i * tq + lax.broadcasted_iota(jnp.int32, (tq, 1), 0)
        start, ok = _swa_window(i, tq, row)
        kk, vv = k_ref[pl.ds(start, span), :], v_ref[pl.ds(start, span), :]
        q_all, o_all, do_all, lse_all = q_ref[...], o_ref[...].astype(F32), do_ref[...].astype(F32), lse_ref[0]
        dqs, dk_c, dv_c = [], None, None
        for h in range(SWA_G):
            sl = slice(h * hd, (h + 1) * hd)
            qh = q_all[:, sl] * scale
            doh = do_all[:, sl]
            delta = jnp.sum(doh * o_all[:, sl], axis=1, keepdims=True)
            lse_h = lse_all[:, h:h + 1]
            doh = doh.astype(BF16)
            s = jnp.where(ok, lax.dot_general(qh, kk, NT, preferred_element_type=F32), NEG)
            p = jnp.exp(s - lse_h)
            ds = p * (lax.dot_general(doh, vv, NT, preferred_element_type=F32) - delta)
            dqs.append(jnp.dot(ds.astype(BF16), kk, preferred_element_type=F32) * scale)
            dk_h = jnp.dot(ds.T.astype(BF16), qh, preferred_element_type=F32)
            dv_h = jnp.dot(p.T.astype(BF16), doh, preferred_element_type=F32)
            dk_c = dk_h if dk_c is None else dk_c + dk_h
            dv_c = dv_h if dv_c is None else dv_c + dv_h
            p_sink = jnp.exp(sink_ref[g * SWA_G + h] - lse_h)
            dsink_ref[0, h:h + 1, :] += jnp.broadcast_to(-jnp.sum(p_sink * delta, axis=0, keepdims=True), (1, LANES))
        dq_ref[...] = jnp.concatenate(dqs, axis=1)
        dk_ref[pl.ds(start, span), :] += dk_c
        dv_ref[pl.ds(start, span), :] += dv_c

    wq = SWA_G * hd
    qb = pl.BlockSpec((tq, wq), lambda g, i: (i, g))
    kb = pl.BlockSpec((S, hd), lambda g, i: (g, 0))
    block_bytes = 2 * tq * wq * (2 + 2 + 4 + 4) + 2 * S * hd * (2 + 2 + 4 + 4) + 4 * tq * LANES
    return pl.pallas_call(
        body, name=name, grid=(SWA_KV_HEADS, S // tq),
        in_specs=[pl.BlockSpec(memory_space=pltpu.SMEM), qb, kb, kb, qb, qb, pl.BlockSpec((1, tq, LANES), lambda g, i: (g, i, 0))],
        out_specs=[qb, kb, kb, pl.BlockSpec((1, 8, LANES), lambda g, i: (g, 0, 0))],
        out_shape=[jax.ShapeDtypeStruct(q.shape, F32), jax.ShapeDtypeStruct(k.shape, F32), jax.ShapeDtypeStruct(v.shape, F32),
                   jax.ShapeDtypeStruct((SWA_KV_HEADS, 8, LANES), F32)],
        compiler_params=_cparams(("parallel", "arbitrary"), block_bytes),
    )(sinks, q, k, v, o, do, lse)


def _split_dot(x, u2):
    hi = x.astype(BF16)
    lo = (x - hi.astype(F32)).astype(BF16)
    return jnp.dot(jnp.concatenate([hi, lo], axis=1), u2, preferred_element_type=F32)


def _suffix_ones():
    r = lax.broadcasted_iota(jnp.int32, (2 * LANES, 2 * LANES), 0) % LANES
    c = lax.broadcasted_iota(jnp.int32, (2 * LANES, 2 * LANES), 1)
    return ((r > c) | (c >= LANES)).astype(BF16)


def _suffix_scan(x, uo, run):
    nc = x.shape[1] // LANES
    out = [None] * nc
    for c in reversed(range(nc)):
        st = _split_dot(x[:, c * LANES:(c + 1) * LANES], uo)
        out[c] = st[:, :LANES] + run
        run = run + st[:, LANES:]
    return (out[0] if nc == 1 else jnp.concatenate(out, axis=1)), run


SB_DEAD_LOG = -110.0


def _sb_logits(qh, kk, r0, row, masked):
    z = lax.dot_general(qh, kk, NT, preferred_element_type=F32)
    nz = -z
    l = jnp.minimum(nz, 0.0) - jnp.log(1.0 + jnp.exp(jnp.minimum(z, nz)))
    ok = None
    if masked:
        col = r0 + lax.broadcasted_iota(jnp.int32, (1, kk.shape[0]), 1)
        ok = col < row
        l = jnp.where(ok, l, 0.0)
    return z, l, ok


def _sb_walk(blk, carry, i, tq, ts):
    carry = blk(pl.multiple_of(i * tq, tq), tq, carry, True)

    def live(c):
        t, heads = c
        top = jnp.max(heads[0][0])
        for h in heads[1:]:
            top = jnp.maximum(top, jnp.max(h[0]))
        return jnp.logical_and(t < i * (tq // ts), top > SB_DEAD_LOG)

    def step(c):
        t, heads = c
        return t + 1, blk(pl.multiple_of(i * tq - (t + 1) * ts, ts), ts, heads, False)

    return lax.while_loop(live, step, (jnp.int32(0), carry))[1]


def _sb_fwd(name, pb, *, q_cb, k_cb, v_cb, tq, ts, scale):
    S = pb.shape[0]
    nq = S // tq
    hd = SB_HEAD_DIM
    groups = SB_HEADS * hd // LANES
    nh = LANES // hd

    def body(q_ref, k_ref, v_ref, o16_ref, o32_ref):
        i = pl.program_id(1)
        q_all = q_ref[...]
        row = i * tq + lax.broadcasted_iota(jnp.int32, (tq, 1), 0)
        uo = _suffix_ones()
        sls = [slice(h * hd, (h + 1) * hd) for h in range(nh)]
        qhs = [q_all[:, sl] * scale for sl in sls]

        def blk(r0, width, carry, masked):
            k_all = k_ref[pl.ds(r0, width), :]
            v_all = v_ref[pl.ds(r0, width), :]
            new = []
            for qh, sl, (run_l, acc) in zip(qhs, sls, carry):
                z, l, ok = _sb_logits(qh, k_all[:, sl], r0, row, masked)
                tail, run_l = _suffix_scan(l, uo, run_l)
                e = z + l + tail
                if masked:
                    e = jnp.where(ok, e, NEG)
                new.append((run_l, acc + jnp.dot(jnp.exp(e).astype(BF16), v_all[:, sl], preferred_element_type=F32)))
            return tuple(new)

        carry = tuple((jnp.zeros((tq, LANES), F32), jnp.zeros((tq, hd), F32)) for _ in range(nh))
        carry = _sb_walk(blk, carry, i, tq, ts)
        o = jnp.concatenate([acc for _, acc in carry], axis=1)
        o16_ref[...] = o.astype(BF16)
        o32_ref[...] = o

    block_bytes = 2 * tq * LANES + 2 * 2 * S * LANES + 6 * tq * LANES
    return pl.pallas_call(
        body, name=name, grid=(groups, nq),
        in_specs=[pl.BlockSpec((tq, LANES), lambda g, i: (i, q_cb + g)), pl.BlockSpec((S, LANES), lambda g, i: (0, k_cb + g)),
                  pl.BlockSpec((S, LANES), lambda g, i: (0, v_cb + g))],
        out_specs=[pl.BlockSpec((tq, LANES), lambda g, i: (i, g)), pl.BlockSpec((tq, LANES), lambda g, i: (i, g))],
        out_shape=[jax.ShapeDtypeStruct((S, groups * LANES), BF16), jax.ShapeDtypeStruct((S, groups * LANES), F32)],
        compiler_params=_cparams(("parallel", "arbitrary"), block_bytes),
    )(pb, pb, pb)


def _sb_bwd(name, pb, o32, do, *, q_cb, k_cb, v_cb, tq, ts, scale):
    S = pb.shape[0]
    nq = S // tq
    hd = SB_HEAD_DIM
    groups = SB_HEADS * hd // LANES
    nh = LANES // hd

    def body(q_ref, k_ref, v_ref, o_ref, do_ref, dq_ref, dk_ref, dv_ref):
        i = pl.program_id(1)

        @pl.when(i == 0)
        def _():
            dk_ref[...] = jnp.zeros_like(dk_ref)
            dv_ref[...] = jnp.zeros_like(dv_ref)

        q_all = q_ref[...]
        o_all = o_ref[...]
        do_all = do_ref[...].astype(F32)
        row = i * tq + lax.broadcasted_iota(jnp.int32, (tq, 1), 0)
        uo = _suffix_ones()
        per_head = []
        for h in range(nh):
            sl = slice(h * hd, (h + 1) * hd)
            doh = do_all[:, sl].astype(BF16)
            total = jnp.sum(doh.astype(F32) * o_all[:, sl], axis=1, keepdims=True)
            per_head.append((q_all[:, sl] * scale, doh, jnp.broadcast_to(total, (tq, LANES)), sl))

        def blk(r0, width, carry, masked):
            k_all = k_ref[pl.ds(r0, width), :]
            v_all = v_ref[pl.ds(r0, width), :]
            new, dk_c, dv_c = [], [], []
            for (qh, doh, total, sl), (run_l, run_g, dq) in zip(per_head, carry):
                kk, vv = k_all[:, sl], v_all[:, sl]
                z, l, ok = _sb_logits(qh, kk, r0, row, masked)
                tail, run_l = _suffix_scan(l, uo, run_l)
                e = z + l
                beta = jnp.exp(e)
                e = e + tail
                if masked:
                    e = jnp.where(ok, e, NEG)
                a = jnp.exp(e).astype(BF16).astype(F32)
                gr = lax.dot_general(doh, vv, NT, preferred_element_type=F32) * a
                right, run_g = _suffix_scan(gr, uo, run_g)
                nc = width // LANES
                prefix = (total if nc == 1 else jnp.tile(total, (1, nc))) - right
                dz = gr - beta * prefix
                if masked:
                    dz = jnp.where(ok, dz, 0.0)
                new.append((run_l, run_g, dq + jnp.dot(dz.astype(BF16), kk, preferred_element_type=F32)))
                dk_c.append(jnp.dot(dz.T.astype(BF16), qh, preferred_element_type=F32))
                dv_c.append(jnp.dot(a.T.astype(BF16), doh, preferred_element_type=F32))
            dk_ref[pl.ds(r0, width), :] += jnp.concatenate(dk_c, axis=1)
            dv_ref[pl.ds(r0, width), :] += jnp.concatenate(dv_c, axis=1)
            return tuple(new)

        zc = jnp.zeros((tq, LANES), F32)
        carry = tuple((zc, zc, jnp.zeros((tq, hd), F32)) for _ in range(nh))
        carry = _sb_walk(blk, carry, i, tq, ts)
        dq_ref[...] = jnp.concatenate([c[2] * scale for c in carry], axis=1)

    W = groups * LANES
    block_bytes = 2 * tq * LANES + 2 * 2 * S * LANES + 3 * 4 * tq * LANES + 2 * 4 * S * LANES
    return pl.pallas_call(
        body, name=name, grid=(groups, nq),
        in_specs=[pl.BlockSpec((tq, LANES), lambda g, i: (i, q_cb + g)), pl.BlockSpec((S, LANES), lambda g, i: (0, k_cb + g)),
                  pl.BlockSpec((S, LANES), lambda g, i: (0, v_cb + g)), pl.BlockSpec((tq, LANES), lambda g, i: (i, g)),
                  pl.BlockSpec((tq, LANES), lambda g, i: (i, g))],
        out_specs=[pl.BlockSpec((tq, LANES), lambda g, i: (i, g)), pl.BlockSpec((S, LANES), lambda g, i: (0, g)),
                   pl.BlockSpec((S, LANES), lambda g, i: (0, g))],
        out_shape=[jax.ShapeDtypeStruct((S, W), F32)] * 3,
        compiler_params=_cparams(("parallel", "arbitrary"), block_bytes),
    )(pb, pb, pb, o32, do)


def _exchange_copies(src_refs, out_refs, send_sems, recv_sems, local_sems, per_peer):
    x, y, c = lax.axis_index("x"), lax.axis_index("y"), lax.axis_index("c")
    me = 4 * x + 2 * y + c
    n = len(src_refs)

    def copy(j, k):
        px, py, pc = x ^ (k >> 2), y ^ ((k >> 1) & 1), c ^ (k & 1)
        s = src_refs[j].at[4 * px + 2 * py + pc] if per_peer else src_refs[j]
        return pltpu.make_async_remote_copy(
            src_ref=s, dst_ref=out_refs[j].at[me], send_sem=send_sems.at[j, k - 1], recv_sem=recv_sems.at[j, k - 1],
            device_id=(px, py, pc), device_id_type=pl.DeviceIdType.MESH)

    mine = [pltpu.make_async_copy(src_refs[j].at[me] if per_peer else src_refs[j], out_refs[j].at[me], local_sems.at[j]) for j in range(n)]
    return mine, [copy(j, k) for k in range(1, N_DEV) for j in range(n)]


def _exchange_start(*refs, per_peer):
    mine, copies = _exchange_copies(*refs, per_peer)
    for cp in mine + copies:
        cp.start()


def _exchange_wait(*refs, per_peer):
    mine, copies = _exchange_copies(*refs, per_peer)
    for cp in copies:
        cp.wait_recv()
    for cp in copies:
        cp.wait_send()
    for cp in mine:
        cp.wait()


def _exchange_specs(srcs, per_peer):
    n = len(srcs)
    out_shape = [jax.ShapeDtypeStruct((N_DEV,) + tuple(s.shape[1:] if per_peer else s.shape), s.dtype) for s in srcs]
    sems = [pltpu.SemaphoreType.DMA((n, N_DEV - 1)), pltpu.SemaphoreType.DMA((n, N_DEV - 1)), pltpu.SemaphoreType.DMA((n,))]
    return [pl.BlockSpec(memory_space=pltpu.HBM)] * n, out_shape, sems


def _exchange(name, srcs, per_peer):
    n = len(srcs)

    def body(*refs):
        parts = (refs[:n], refs[n:2 * n], *refs[2 * n:])
        _exchange_start(*parts, per_peer=per_peer)
        _exchange_wait(*parts, per_peer=per_peer)

    hbm, out_shape, sems = _exchange_specs(srcs, per_peer)
    return pl.pallas_call(body, name=name, in_specs=hbm, out_specs=hbm, out_shape=out_shape, scratch_shapes=sems)(*srcs)


def _pallas(body, *, name, grid, in_specs, out_specs, out_shape, args, sem, block_bytes, ride=None):
    if ride is None:
        return pl.pallas_call(body, name=name, grid=grid, in_specs=in_specs, out_specs=out_specs, out_shape=out_shape,
                              compiler_params=_cparams(sem, block_bytes))(*args), None
    srcs, per_peer = ride
    n, n_in, n_out = len(srcs), len(in_specs), len(out_specs)

    def riding(*refs):
        ins, xsrc = refs[:n_in], refs[n_in:n_in + n]
        outs, xout = refs[n_in + n:n_in + n + n_out], refs[n_in + n + n_out:n_in + 2 * n + n_out]
        parts = (xsrc, xout, *refs[n_in + 2 * n + n_out:])
        ids = [pl.program_id(d) for d in range(len(grid))]
        first = functools.reduce(jnp.logical_and, [i == 0 for i in ids])
        last = functools.reduce(jnp.logical_and, [i == g - 1 for i, g in zip(ids, grid)])
        pl.when(first)(functools.partial(_exchange_start, *parts, per_peer=per_peer))
        body(*ins, *outs)
        pl.when(last)(functools.partial(_exchange_wait, *parts, per_peer=per_peer))

    hbm, x_shape, sems = _exchange_specs(srcs, per_peer)
    res = pl.pallas_call(riding, name=name, grid=grid, in_specs=list(in_specs) + hbm, out_specs=list(out_specs) + hbm,
                         out_shape=list(out_shape) + x_shape, scratch_shapes=sems,
                         compiler_params=_cparams(("arbitrary",) * len(grid), block_bytes))(*args, *srcs)
    return res[:n_out], res[n_out:]


def _adamw(name, parts, w, m, v, *, tile):
    R, C = w.shape

    def body(p_ref, w_ref, m_ref, v_ref, g_ref, d_ref, nm_ref, nv_ref):
        g = p_ref[0].astype(F32)
        for d in range(1, N_DEV):
            g = g + p_ref[d].astype(F32)
        wv = w_ref[...]
        mm = ADAM_B1 * m_ref[...] + (1.0 - ADAM_B1) * g
        vv = ADAM_B2 * v_ref[...] + (1.0 - ADAM_B2) * jnp.square(g)
        m_hat = mm / (1.0 - ADAM_B1 ** ADAM_STEP)
        v_hat = vv / (1.0 - ADAM_B2 ** ADAM_STEP)
        g_ref[...] = g
        d_ref[...] = -ADAM_LR * (m_hat / (jnp.sqrt(v_hat) + ADAM_EPS) + ADAM_WD * wv)
        nm_ref[...] = mm
        nv_ref[...] = vv

    blk = pl.BlockSpec((tile, C), lambda i: (i, 0))
    block_bytes = N_DEV * _nbytes((tile, C), parts.dtype) + 7 * _nbytes((tile, C), F32)
    return pl.pallas_call(
        body, name=name, grid=(R // tile,),
        in_specs=[pl.BlockSpec((N_DEV, tile, C), lambda i: (0, i, 0)), blk, blk, blk],
        out_specs=[blk] * 4, out_shape=[jax.ShapeDtypeStruct((R, C), F32)] * 4,
        compiler_params=_cparams(("parallel",), block_bytes),
    )(parts, w, m, v)


def _perm_w_in(w):
    cols = [w[:, ORIG_COLS[n][0]:ORIG_COLS[n][0] + ORIG_COLS[n][1]] for n in PERM_ORDER]
    cols.append(jnp.zeros((w.shape[0], IN_WIDTH_P - IN_WIDTH), w.dtype))
    return jnp.concatenate(cols, axis=1)


def _unperm_w_in(wp):
    order = sorted(ORIG_COLS, key=lambda n: ORIG_COLS[n][0])
    return jnp.concatenate([wp[:, PERM_OFF[n]:PERM_OFF[n] + ORIG_COLS[n][1]] for n in order], axis=1)


def _perm_w_uq(w):
    w3 = w.reshape(w.shape[0], MLA_HEADS, MLA_NOPE + MLA_ROPE)
    return jnp.pad(w3, ((0, 0), (0, 0), (0, LANES - MLA_NOPE - MLA_ROPE))).reshape(w.shape[0], MLA_QW)


def _unperm_w_uq(wp):
    return wp.reshape(wp.shape[0], MLA_HEADS, LANES)[:, :, :MLA_NOPE + MLA_ROPE].reshape(wp.shape[0], -1)


def _perm_w_ukv(w):
    w3 = w.reshape(w.shape[0], MLA_HEADS, MLA_NOPE + MLA_V)
    kp = jnp.pad(w3[:, :, :MLA_NOPE], ((0, 0), (0, 0), (0, LANES - MLA_NOPE))).reshape(w.shape[0], MLA_HEADS * LANES)
    return jnp.concatenate([kp, w3[:, :, MLA_NOPE:].reshape(w.shape[0], MLA_HEADS * MLA_V)], axis=1)


def _unperm_w_ukv(wp):
    n = wp.shape[0]
    kp = wp[:, :MLA_HEADS * LANES].reshape(n, MLA_HEADS, LANES)[:, :, :MLA_NOPE]
    vp = wp[:, MLA_HEADS * LANES:].reshape(n, MLA_HEADS, MLA_V)
    return jnp.concatenate([kp, vp], axis=2).reshape(n, MLA_HEADS * (MLA_NOPE + MLA_V))


BIG = ("w_in", "w_uq", "w_ukv", "w_o_mla", "w_o_swa", "w_o_sb", "w_out", "w_up", "w_down")
W_EARLY = ("w_in", "w_uq", "w_ukv")
W_LATE = ("w_o_mla", "w_o_swa", "w_o_sb", "w_out", "w_up", "w_down")
ROW_SHARDED = ("w_out", "w_down")
SMALL = ("g_mix_pre", "b_gate", "g_q_lat", "g_kv_lat", "swa_sinks", "g_mix_post", "g_mlp_pre", "g_mlp_post")
PACK_COLS = 1024


def _to_shards(name, full):
    L, R, C = full.shape
    if name in ROW_SHARDED:
        return full.reshape(L, N_DEV, R // N_DEV, C).transpose(1, 0, 2, 3).reshape(N_DEV, L * R // N_DEV, C)
    return full.reshape(L, R, N_DEV, C // N_DEV).transpose(2, 0, 1, 3).reshape(N_DEV, L * R, C // N_DEV)


def _from_shards(name, gathered, shard_shape):
    L, r, c = shard_shape
    a = gathered.reshape(N_DEV, L, r, c)
    if name in ROW_SHARDED:
        return a.transpose(1, 0, 2, 3).reshape(L, N_DEV * r, c)
    return a.transpose(1, 2, 0, 3).reshape(L, r, N_DEV * c)


def _tables(positions):
    pos = positions.astype(F32).reshape(-1, 1)

    def cs(d):
        inv = 1.0 / (ROPE_THETA ** (jnp.arange(0, d, 2, dtype=F32) / d))
        ang = pos * inv
        c, s = jnp.cos(ang), jnp.sin(ang)
        return jnp.concatenate([c, c], axis=1), jnp.concatenate([-s, s], axis=1)

    c64, s64 = cs(SWA_HEAD_DIM)
    c32, s32 = cs(MLA_ROPE)
    n = pos.shape[0]
    one, zero = jnp.ones((n, MLA_NOPE), F32), jnp.zeros((n, MLA_NOPE), F32)
    pad0 = jnp.zeros((n, LANES - MLA_NOPE - MLA_ROPE), F32)
    cq = jnp.concatenate([one, c32, pad0], axis=1)
    sq = jnp.concatenate([zero, s32, pad0], axis=1)
    padk = jnp.zeros((n, LANES - MLA_ROPE), F32)
    return dict(
        c_swa_q=jnp.tile(c64, (1, SWA_HEADS)), s_swa_q=jnp.tile(s64, (1, SWA_HEADS)),
        c_swa_k=jnp.tile(c64, (1, SWA_KV_HEADS)), s_swa_k=jnp.tile(s64, (1, SWA_KV_HEADS)),
        c_mla_q=jnp.tile(cq, (1, MLA_HEADS)), s_mla_q=jnp.tile(sq, (1, MLA_HEADS)),
        c_mla_k=jnp.concatenate([c32, padk], axis=1), s_mla_k=jnp.concatenate([s32, padk], axis=1))


def _cb(name):
    return PERM_OFF[name] // PERM_WIDTH[name]


MLA_SCALE = (MLA_NOPE + MLA_ROPE) ** -0.5
MLA_SPEC = dict(groups=MLA_HEADS // 2, tq=512, tk=512,
                heads=[(slice(h * LANES, (h + 1) * LANES), slice(h * LANES, (h + 1) * LANES), slice(h * MLA_V, (h + 1) * MLA_V)) for h in range(2)])
SWA_G = SWA_HEADS // SWA_KV_HEADS
SWA_TQ = 256
SB_SPEC = dict(tq=512, ts=256, scale=SB_HEAD_DIM ** -0.5, q_cb=PERM_OFF["q_sb"] // LANES, k_cb=PERM_OFF["k_sb"] // LANES,
               v_cb=PERM_OFF["v_sb"] // LANES)


def _mla_specs(S):
    tq = MLA_SPEC["tq"]
    return dict(q_spec=((tq, 2 * LANES), lambda g, i: (i, g)), k_spec=((S, 2 * LANES), lambda g, i: (0, g)),
                v_spec=((S, 2 * MLA_V), lambda g, i: (0, g)))


def _layer_fwd(l, x, W, P, tb, T, ride, late):
    S = x.shape[0]
    nt = S // T
    h, = _rowwise(f"l{l}_norm_in", lambda xv, g: (_rms(xv, g),), [_ri(x, T), _bi(P["g_mix_pre"])], [_ro(S, D_MODEL, BF16, T)], n_tiles=nt)
    proj, proj16 = _mm_nn(f"l{l}_mm_in", h, W["w_in"], [F32, BF16], tm=512, tn=IN_WIDTH_P // 2)

    def mix_prep(cq, ckv, qs, ks, vs, gq, gkv, cq_t, sq_t, ck_t, sk_t):
        ksr = _rope(ks, ck_t, sk_t, SWA_HEAD_DIM // 2)
        hd = SWA_HEAD_DIM
        return (_rms(cq, gq), _rms(ckv, gkv), _rope(qs, cq_t, sq_t, hd // 2),
                [ksr[:, :hd], ksr[:, hd:]], [vs[:, :hd], vs[:, hd:]])

    kv3 = lambda dt: ((SWA_KV_HEADS, S, SWA_HEAD_DIM), dt, (SWA_KV_HEADS, T, SWA_HEAD_DIM), lambda i: (0, i, 0))
    cqn, ckvn, q_swa, k_swa, v_swa = _rowwise(
        f"l{l}_mix_prep", mix_prep,
        [_ri(proj, T, 256, _cb("c_q")), _ri(proj, T, 128, _cb("c_kv")), _ri(proj, T, 512, _cb("q_swa")), _ri(proj, T, 128, _cb("k_swa")),
         _ri(proj, T, 128, _cb("v_swa")), _bi(P["g_q_lat"]), _bi(P["g_kv_lat"]), _ri(tb["c_swa_q"], T), _ri(tb["s_swa_q"], T),
         _ri(tb["c_swa_k"], T), _ri(tb["s_swa_k"], T)],
        [_ro(S, 256, BF16, T), _ro(S, 128, BF16, T), _ro(S, 512, BF16, T), kv3(BF16), kv3(BF16)], n_tiles=nt)
    k_swa = k_swa.reshape(SWA_KV_HEADS * S, SWA_HEAD_DIM)
    v_swa = v_swa.reshape(SWA_KV_HEADS * S, SWA_HEAD_DIM)
    q_lat, = _mm_nn(f"l{l}_mm_uq", cqn, W["w_uq"], [F32], tm=1024, tn=MLA_QW)
    kv_lat, = _mm_nn(f"l{l}_mm_ukv", ckvn, W["w_ukv"], [F32], tm=1024, tn=MLA_KVW)

    def mla_prep(q, kk, vv, kr, cq_t, sq_t, ck_t, sk_t):
        kpe = pltpu.roll(_rope(kr, ck_t, sk_t, MLA_ROPE // 2), MLA_NOPE, axis=1)
        return _rope(q, cq_t, sq_t, MLA_ROPE // 2) * MLA_SCALE, kk + jnp.tile(kpe, (1, MLA_HEADS)), vv

    q_mla, k_mla, v_mla = _rowwise(
        f"l{l}_mla_prep", mla_prep,
        [_ri(q_lat, T), _ri(kv_lat, T, MLA_HEADS * LANES, 0), _ri(kv_lat, T, MLA_HEADS * MLA_V, 2), _ri(proj, T, 128, _cb("k_rope")),
         _ri(tb["c_mla_q"], T), _ri(tb["s_mla_q"], T), _ri(tb["c_mla_k"], T), _ri(tb["s_mla_k"], T)],
        [_ro(S, MLA_QW, BF16, T), _ro(S, MLA_HEADS * LANES, BF16, T), _ro(S, MLA_HEADS * MLA_V, BF16, T)], n_tiles=nt)

    (att_a, lse_a), rode = _softmax_fwd(f"l{l}_mla_fwd", q_mla, k_mla, v_mla, o_width=MLA_HEADS * MLA_V, ride=ride, **MLA_SPEC, **_mla_specs(S))
    W = {**W, **late(rode)}
    att_b, lse_b = _swa_fwd(f"l{l}_swa_fwd", q_swa, k_swa, v_swa, P["swa_sinks"], tq=SWA_TQ)
    att_c, att_c32 = _sb_fwd(f"l{l}_sb_fwd", proj16, **SB_SPEC)
    o_a, = _mm_nn(f"l{l}_mm_oa", att_a, W["w_o_mla"], [F32], tm=1024, tn=D_MODEL)
    o_b, = _mm_nn(f"l{l}_mm_ob", att_b, W["w_o_swa"], [F32], tm=1024, tn=D_MODEL)
    o_c, = _mm_nn(f"l{l}_mm_oc", att_c, W["w_o_sb"], [F32], tm=1024, tn=D_MODEL)

    def gate_mix(gl, b, oa, ob, oc):
        gt = jax.nn.sigmoid(gl + b)
        return (gt[:, :D_MODEL] * oa + gt[:, D_MODEL:2 * D_MODEL] * ob + gt[:, 2 * D_MODEL:] * oc,)

    mixed, = _rowwise(f"l{l}_gate_mix", gate_mix, [_ri(proj, T, 3072, 0), _bi(P["b_gate"]), _ri(o_a, T), _ri(o_b, T), _ri(o_c, T)],
                      [_ro(S, D_MODEL, BF16, T)], n_tiles=nt)
    y, = _mm_nn(f"l{l}_mm_out", mixed, W["w_out"], [F32], tm=1024, tn=D_MODEL)

    def resid_norm(xv, yv, gpost, gpre):
        x1 = xv + _rms(yv, gpost)
        return x1, _rms(x1, gpre)

    x1, h2 = _rowwise(f"l{l}_resid_norm", resid_norm, [_ri(x, T), _ri(y, T), _bi(P["g_mix_post"]), _bi(P["g_mlp_pre"])],
                      [_ro(S, D_MODEL, F32, T), _ro(S, D_MODEL, BF16, T)], n_tiles=nt)
    up, u = _mm_nn(f"l{l}_mm_up", h2, W["w_up"], [F32, BF16], tm=512, tn=2048,
                   epilogue=lambda acc: (acc, jnp.square(jnp.maximum(acc, 0.0))))
    dn, = _mm_nn(f"l{l}_mm_down", u, W["w_down"], [F32], tm=512, tn=D_MODEL)
    x2, = _rowwise(f"l{l}_resid_out", lambda xv, dv, g: (xv + _rms(dv, g),), [_ri(x1, T), _ri(dn, T), _bi(P["g_mlp_post"])],
                   [_ro(S, D_MODEL, F32, T)], n_tiles=nt)
    saved = dict(x=x, h=h, proj=proj, proj16=proj16, cqn=cqn, ckvn=ckvn, q_swa=q_swa, k_swa=k_swa, v_swa=v_swa, q_mla=q_mla, k_mla=k_mla,
                 v_mla=v_mla, att_a=att_a, lse_a=lse_a, att_b=att_b, lse_b=lse_b, att_c=att_c, att_c32=att_c32, o_a=o_a, o_b=o_b, o_c=o_c,
                 mixed=mixed, y=y, x1=x1, h2=h2, up=up, u=u, dn=dn)
    return x2, saved, W, rode


def _layer_bwd(l, dx2, sv, W, P, tb, T, pending, to_send):
    S = dx2.shape[0]
    nt = S // T
    G = {}

    def post_norm_bwd(v, g, dy):
        return _rms_bwd(v, g, dy)

    d_dn, G["g_mlp_post"] = _rowwise(f"l{l}_b_post2", post_norm_bwd, [_ri(sv["dn"], T), _bi(P["g_mlp_post"]), _ri(dx2, T)],
                                    [_ro(S, D_MODEL, BF16, T)], [D_MODEL], n_tiles=nt)
    d_up, = _mm_nt(f"l{l}_b_mm_down", d_dn, W["w_down"], [BF16], tm=512, tn=2048, extras=[sv["up"]],
                   epilogue=lambda acc, upv: (acc * (2.0 * jnp.maximum(upv, 0.0)),))
    G["w_down"] = _mm_tn(f"l{l}_g_down", sv["u"], d_dn, tm=2048, tn=D_MODEL, ts=512)
    d_h2, = _mm_nt(f"l{l}_b_mm_up", d_up, W["w_up"], [F32], tm=512, tn=D_MODEL)
    G["w_up"] = _mm_tn(f"l{l}_g_up", sv["h2"], d_up, tm=D_MODEL, tn=2048, ts=512)

    def pre_norm_bwd(v, g, dy, dres):
        dx, dg = _rms_bwd(v, g, dy)
        return dres + dx, dg

    dx1, G["g_mlp_pre"] = _rowwise(f"l{l}_b_pre2", pre_norm_bwd, [_ri(sv["x1"], T), _bi(P["g_mlp_pre"]), _ri(d_h2, T), _ri(dx2, T)],
                                  [_ro(S, D_MODEL, F32, T)], [D_MODEL], n_tiles=nt)
    d_y, G["g_mix_post"] = _rowwise(f"l{l}_b_post1", post_norm_bwd, [_ri(sv["y"], T), _bi(P["g_mix_post"]), _ri(dx1, T)],
                                   [_ro(S, D_MODEL, BF16, T)], [D_MODEL], n_tiles=nt)
    d_mixed, = _mm_nt(f"l{l}_b_mm_out", d_y, W["w_out"], [F32], tm=1024, tn=D_MODEL)
    G["w_out"] = _mm_tn(f"l{l}_g_out", sv["mixed"], d_y, tm=D_MODEL, tn=D_MODEL, ts=512)

    def gate_bwd(dm, gl, b, oa, ob, oc):
        gt = jax.nn.sigmoid(gl + b)
        outs, dgl = [], []
        for k, o in enumerate((oa, ob, oc)):
            gk = gt[:, k * D_MODEL:(k + 1) * D_MODEL]
            outs.append(dm * gk)
            dgl.append(dm * o * gk * (1.0 - gk))
        dgl = jnp.concatenate(dgl, axis=1)
        return (*outs, dgl, jnp.sum(dgl, axis=0, keepdims=True))

    d_oa, d_ob, d_oc, d_gl, G["b_gate"] = _rowwise(
        f"l{l}_b_gate", gate_bwd, [_ri(d_mixed, T), _ri(sv["proj"], T, 3072, 0), _bi(P["b_gate"]), _ri(sv["o_a"], T), _ri(sv["o_b"], T), _ri(sv["o_c"], T)],
        [_ro(S, D_MODEL, BF16, T)] * 3 + [_ro(S, 3 * D_MODEL, BF16, T)], [3 * D_MODEL], n_tiles=nt)
    d_att = {}
    for br, d_o, att in (("mla", d_oa, sv["att_a"]), ("swa", d_ob, sv["att_b"]), ("sb", d_oc, sv["att_c"])):
        d_att[br], = _mm_nt(f"l{l}_b_mm_o_{br}", d_o, W["w_o_" + br], [F32], tm=1024, tn=512)
        G["w_o_" + br] = _mm_tn(f"l{l}_g_o_{br}", att, d_o, tm=512, tn=D_MODEL, ts=512)

    riders = list(pending) + [(n, l, to_send(n, G[n])) for n in W_LATE]
    (dq_mla, dk_mla, dv_mla), rode = _softmax_bwd(f"l{l}_mla_bwd", sv["q_mla"], sv["k_mla"], sv["v_mla"], sv["att_a"], d_att["mla"], sv["lse_a"],
                                                  scale=MLA_SCALE, ride=([s for _, _, s in riders], True), **MLA_SPEC, **_mla_specs(S))
    rode = [(n, ly, r) for (n, ly, _), r in zip(riders, rode)]
    dq_swa, dk_swa, dv_swa, dsink = _swa_bwd(f"l{l}_swa_bwd", sv["q_swa"], sv["k_swa"], sv["v_swa"], sv["att_b"], d_att["swa"], sv["lse_b"],
                                             P["swa_sinks"], tq=SWA_TQ)
    G["swa_sinks"] = dsink[:, :SWA_G, 0].reshape(1, SWA_HEADS)
    dq_sb, dk_sb, dv_sb = _sb_bwd(f"l{l}_sb_bwd", sv["proj16"], sv["att_c32"], d_att["sb"], **SB_SPEC)

    def mla_prep_bwd(dq, dk, dvv, cq_t, sq_t, ck_t, sk_t):
        dks = dk[:, :LANES]
        for hh in range(1, MLA_HEADS):
            dks = dks + dk[:, hh * LANES:(hh + 1) * LANES]
        d_kr = _rope_t(pltpu.roll(dks, LANES - MLA_NOPE, axis=1), ck_t, sk_t, MLA_ROPE // 2)
        return _rope_t(dq, cq_t, sq_t, MLA_ROPE // 2), jnp.concatenate([dk, dvv], axis=1), d_kr

    d_q_lat, d_kv_lat, d_krope = _rowwise(
        f"l{l}_b_mla_prep", mla_prep_bwd,
        [_ri(dq_mla, T), _ri(dk_mla, T), _ri(dv_mla, T), _ri(tb["c_mla_q"], T), _ri(tb["s_mla_q"], T), _ri(tb["c_mla_k"], T), _ri(tb["s_mla_k"], T)],
        [_ro(S, MLA_QW, BF16, T), _ro(S, MLA_KVW, BF16, T), _ro(S, LANES, BF16, T)], n_tiles=nt)
    d_cqn, = _mm_nt(f"l{l}_b_mm_uq", d_q_lat, W["w_uq"], [F32], tm=1024, tn=MLA_Q_LORA)
    G["w_uq"] = _mm_tn(f"l{l}_g_uq", sv["cqn"], d_q_lat, tm=MLA_Q_LORA, tn=MLA_QW, ts=512)
    d_ckvn, = _mm_nt(f"l{l}_b_mm_ukv", d_kv_lat, W["w_ukv"], [F32], tm=1024, tn=MLA_KV_LORA)
    G["w_ukv"] = _mm_tn(f"l{l}_g_ukv", sv["ckvn"], d_kv_lat, tm=MLA_KV_LORA, tn=MLA_KVW, ts=512)

    def mix_prep_bwd(cq, ckv, gq, gkv, dcqn, dckvn, dqs, dks, dvs, cq_t, sq_t, ck_t, sk_t):
        d_cq, dgq = _rms_bwd(cq, gq, dcqn)
        d_ckv, dgkv = _rms_bwd(ckv, gkv, dckvn)
        dk2 = jnp.concatenate([dks[0], dks[1]], axis=1)
        dv2 = jnp.concatenate([dvs[0], dvs[1]], axis=1)
        return (d_cq, d_ckv, _rope_t(dqs, cq_t, sq_t, SWA_HEAD_DIM // 2), _rope_t(dk2, ck_t, sk_t, SWA_HEAD_DIM // 2), dv2, dgq, dgkv)

    kv3 = lambda a: (a.reshape(SWA_KV_HEADS, S, SWA_HEAD_DIM), (SWA_KV_HEADS, T, SWA_HEAD_DIM), lambda i: (0, i, 0))
    d_cq, d_ckv, d_qswa, d_kswa, d_vswa, G["g_q_lat"], G["g_kv_lat"] = _rowwise(
        f"l{l}_b_mix_prep", mix_prep_bwd,
        [_ri(sv["proj"], T, 256, _cb("c_q")), _ri(sv["proj"], T, 128, _cb("c_kv")), _bi(P["g_q_lat"]), _bi(P["g_kv_lat"]), _ri(d_cqn, T), _ri(d_ckvn, T),
         _ri(dq_swa, T), kv3(dk_swa), kv3(dv_swa), _ri(tb["c_swa_q"], T), _ri(tb["s_swa_q"], T), _ri(tb["c_swa_k"], T), _ri(tb["s_swa_k"], T)],
        [_ro(S, 256, BF16, T), _ro(S, 128, BF16, T), _ro(S, 512, BF16, T), _ro(S, 128, BF16, T), _ro(S, 128, BF16, T)], [256, 128], n_tiles=nt)
    pieces = dict(gates=d_gl, q_swa=d_qswa, q_sb=dq_sb, k_sb=dk_sb, v_sb=dv_sb, c_q=d_cq, c_kv=d_ckv, k_swa=d_kswa, v_swa=d_vswa, k_rope=d_krope)
    d_proj = jnp.concatenate([pieces[n].astype(BF16) for n in PERM_ORDER], axis=1)
    d_h, = _mm_nt(f"l{l}_b_mm_in", d_proj, W["w_in"], [F32], tm=512, tn=512)
    G["w_in"] = _mm_tn(f"l{l}_g_in", sv["h"], d_proj, tm=D_MODEL, tn=IN_WIDTH_P // 2, ts=512)
    dx, G["g_mix_pre"] = _rowwise(f"l{l}_b_pre1", pre_norm_bwd, [_ri(sv["x"], T), _bi(P["g_mix_pre"]), _ri(d_h, T), _ri(dx1, T)],
                                 [_ro(S, D_MODEL, F32, T)], [D_MODEL], n_tiles=nt)
    return dx, G, rode


def _pack_rows(vecs, rows):
    flat = jnp.concatenate([v.reshape(-1) for v in vecs])
    return jnp.pad(flat, (0, rows * PACK_COLS - flat.shape[0])).reshape(rows, PACK_COLS)


def kernel(x, positions, g_mix_pre, w_in, b_gate, g_q_lat, g_kv_lat, w_uq, w_ukv, swa_sinks, w_o_mla, w_o_swa, w_o_sb, w_out, g_mix_post, g_mlp_pre, w_up, w_down, g_mlp_post, loss_target, m_g_mix_pre, m_w_in, m_b_gate, m_g_q_lat, m_g_kv_lat, m_w_uq, m_w_ukv, m_swa_sinks, m_w_o_mla, m_w_o_swa, m_w_o_sb, m_w_out, m_g_mix_post, m_g_mlp_pre, m_w_up, m_w_down, m_g_mlp_post, v_g_mix_pre, v_w_in, v_b_gate, v_g_q_lat, v_g_kv_lat, v_w_uq, v_w_ukv, v_swa_sinks, v_w_o_mla, v_w_o_swa, v_w_o_sb, v_w_out, v_g_mix_post, v_g_mlp_pre, v_w_up, v_w_down, v_g_mlp_post):
    a = dict(locals())
    S = x.shape[1]
    depth = w_in.shape[0]
    T = min(256, S)
    xs = x.reshape(S, D_MODEL)
    tb = _tables(positions)

    perm = dict(w_in=_perm_w_in, w_uq=_perm_w_uq, w_ukv=_perm_w_ukv)

    def shards(l, names):
        return [a[n][l].astype(BF16) for n in names]

    def whole(names, gathered):
        return {n: perm.get(n, lambda t: t)(_from_shards(n, g, (1,) + a[n].shape[1:])[0]) for n, g in zip(names, gathered)}

    early = whole(W_EARLY, _exchange("gather_weights", shards(0, W_EARLY), per_peer=False))
    layers, saved = [], []
    h = xs
    for l in range(depth):
        P = {n: a[n][l].reshape(1, -1) for n in SMALL if n != "swa_sinks"}
        P["swa_sinks"] = a["swa_sinks"][l]
        srcs = shards(l, W_LATE) + (shards(l + 1, W_EARLY) if l + 1 < depth else [])
        h, sv, W, rode = _layer_fwd(l, h, early, P, tb, T, (srcs, False), lambda r: whole(W_LATE, r[:len(W_LATE)]))
        early = whole(W_EARLY, rode[len(W_LATE):])
        layers.append((W, P))
        saved.append(sv)

    def loss_head(yv, tv):
        err = yv - tv
        part = 0.5 * jnp.sum(jnp.mean(err * err, axis=1, keepdims=True), axis=0, keepdims=True)
        return err * (1.0 / D_MODEL), jnp.broadcast_to(part, (1, LANES))

    dh, loss_row = _rowwise("loss_head", loss_head, [_ri(h, T), _ri(loss_target.reshape(S, D_MODEL), T)], [_ro(S, D_MODEL, F32, T)], [LANES],
                            n_tiles=S // T)
    loss = lax.psum(loss_row[0, 0], ("x", "y", "c"))

    unperm = dict(w_in=_unperm_w_in, w_uq=_unperm_w_uq, w_ukv=_unperm_w_ukv)

    def to_send(n, g):
        return _to_shards(n, unperm.get(n, lambda t: t)(g).astype(BF16)[None])

    grads, recv, pending = [None] * depth, {n: [None] * depth for n in BIG}, []
    for l in reversed(range(depth)):
        W, P = layers[l]
        dh, grads[l], rode = _layer_bwd(l, dh, saved[l], W, P, tb, T, pending, to_send)
        for n, ly, r in rode:
            recv[n][ly] = r
        pending = [(n, l, to_send(n, grads[l][n])) for n in W_EARLY]
    for (n, ly, _), r in zip(pending, _exchange("scatter_grads", [s for _, _, s in pending], per_peer=True)):
        recv[n][ly] = r
    grad_x = dh.reshape(x.shape)

    out = {}
    for n in BIG:
        shp = a[n].shape
        rows, cols = shp[0] * shp[1], shp[2]
        parts = jnp.concatenate(recv[n], axis=1)
        res = _adamw("adamw_" + n, parts, a[n].reshape(rows, cols), a["m_" + n].reshape(rows, cols), a["v_" + n].reshape(rows, cols),
                     tile=min(256, rows))
        out[n] = [r.reshape(shp) for r in res]

    small_total = sum(a[n].size for n in SMALL)
    small_rows = -(-small_total // (8 * PACK_COLS)) * 8
    sg = _pack_rows([jnp.stack([grads[l][n].reshape(-1) for l in range(depth)]) for n in SMALL], small_rows)
    sg_all, = _exchange("gather_small_grads", [sg], per_peer=False)
    res = _adamw("adamw_small", sg_all, _pack_rows([a[n] for n in SMALL], small_rows), _pack_rows([a["m_" + n] for n in SMALL], small_rows),
                 _pack_rows([a["v_" + n] for n in SMALL], small_rows), tile=small_rows)
    off = 0
    for n in SMALL:
        cnt = a[n].size
        out[n] = [r.reshape(-1)[off:off + cnt].reshape(a[n].shape) for r in res]
        off += cnt

    order = ("g_mix_pre", "w_in", "b_gate", "g_q_lat", "g_kv_lat", "w_uq", "w_ukv", "swa_sinks", "w_o_mla", "w_o_swa", "w_o_sb", "w_out",
             "g_mix_post", "g_mlp_pre", "w_up", "w_down", "g_mlp_post")
    return (loss, grad_x, *[out[n][0] for n in order], *[out[n][1] for n in order], *[out[n][2] for n in order], *[out[n][3] for n in order])
```

```python
import functools

import jax
import jax.numpy as jnp
from jax import lax
from jax.experimental import pallas as pl
from jax.experimental.pallas import tpu as pltpu

F32, BF16 = jnp.float32, jnp.bfloat16

D_MODEL = 1024
DEPTH = 4
MLA_HEADS, MLA_Q_LORA, MLA_KV_LORA, MLA_NOPE, MLA_ROPE, MLA_V = 8, 256, 128, 64, 32, 64
SWA_HEADS, SWA_KV_HEADS, SWA_HEAD_DIM, SWA_WINDOW = 8, 2, 64, 128
SB_HEADS, SB_HEAD_DIM = 8, 64
D_FF = 4 * D_MODEL
ROPE_THETA = 10000.0
EPS = 1e-6
N_DEV = 8
ADAM_LR, ADAM_B1, ADAM_B2, ADAM_EPS, ADAM_WD, ADAM_STEP = 0.001, 0.9, 0.999, 1e-08, 0.01, 10

LANES = 128
VMEM_LIMIT_MAX = 60 * 1024 * 1024
VMEM_LIMIT_MIN = 32 * 1024 * 1024

ORIG_COLS = dict(c_q=(0, 256), c_kv=(256, 128), k_rope=(384, 32), q_swa=(416, 512), k_swa=(928, 128), v_swa=(1056, 128),
                 q_sb=(1184, 512), k_sb=(1696, 512), v_sb=(2208, 512), gates=(2720, 3072))
IN_WIDTH = 5792
PERM_ORDER = ("gates", "q_swa", "q_sb", "k_sb", "v_sb", "c_q", "c_kv", "k_swa", "v_swa", "k_rope")
PERM_WIDTH = dict(gates=3072, q_swa=512, q_sb=512, k_sb=512, v_sb=512, c_q=256, c_kv=128, k_swa=128, v_swa=128, k_rope=128)
PERM_OFF = {}
_o = 0
for _n in PERM_ORDER:
    PERM_OFF[_n] = _o
    _o += PERM_WIDTH[_n]
IN_WIDTH_P = _o
MLA_QW = MLA_HEADS * LANES
MLA_KVW = MLA_HEADS * LANES + MLA_HEADS * MLA_V

NT = (((1,), (1,)), ((), ()))
TN = (((0,), (0,)), ((), ()))
NEG = -1e30


def _cparams(sem, block_bytes):
    limit = int(min(VMEM_LIMIT_MAX, max(VMEM_LIMIT_MIN, 2 * block_bytes + (16 << 20))))
    return pltpu.CompilerParams(dimension_semantics=sem, vmem_limit_bytes=limit)


def _nbytes(shape, dtype):
    n = 1
    for s in shape:
        n *= s
    return n * jnp.dtype(dtype).itemsize


def _ri(arr, tile, width=None, cb=0):
    width = arr.shape[1] if width is None else width
    return (arr, (tile, width), lambda i, cb=cb: (i, cb))


def _bi(arr):
    return (arr, arr.shape, lambda i: (0, 0))


def _ro(rows, width, dtype, tile):
    return ((rows, width), dtype, (tile, width), lambda i: (i, 0))


def _rowwise(name, fn, ins, outs, reds=(), *, n_tiles):
    n_in, n_out = len(ins), len(outs)

    def body(*refs):
        vals = fn(*[r[...] for r in refs[:n_in]])
        for r, v in zip(refs[n_in:n_in + n_out], vals[:n_out]):
            if isinstance(v, (list, tuple)):
                for j, vj in enumerate(v):
                    r[j] = vj.astype(r.dtype)
            else:
                r[...] = v.astype(r.dtype)
        if reds:
            @pl.when(pl.program_id(0) == 0)
            def _():
                for r in refs[n_in + n_out:]:
                    r[...] = jnp.zeros_like(r)
            for r, v in zip(refs[n_in + n_out:], vals[n_out:]):
                r[...] += v

    block_bytes = sum(_nbytes(b, a.dtype) for a, b, _ in ins) + sum(_nbytes(b, d) for _, d, b, _ in outs)
    res = pl.pallas_call(
        body, name=name, grid=(n_tiles,),
        in_specs=[pl.BlockSpec(b, m) for _, b, m in ins],
        out_specs=[pl.BlockSpec(b, m) for _, _, b, m in outs] + [pl.BlockSpec((1, w), lambda i: (0, 0)) for w in reds],
        out_shape=[jax.ShapeDtypeStruct(s, d) for s, d, _, _ in outs] + [jax.ShapeDtypeStruct((1, w), F32) for w in reds],
        compiler_params=_cparams(("arbitrary",) if reds else ("parallel",), block_bytes),
    )(*[a for a, _, _ in ins])
    return res


def _rms(x, g):
    r = lax.rsqrt(jnp.mean(x * x, axis=1, keepdims=True) + EPS)
    return x * r * g


def _rms_bwd(x, g, dy):
    r = lax.rsqrt(jnp.mean(x * x, axis=1, keepdims=True) + EPS)
    xn = x * r
    dxn = dy * g
    dx = r * (dxn - xn * jnp.mean(dxn * xn, axis=1, keepdims=True))
    return dx, jnp.sum(dy * xn, axis=0, keepdims=True)


def _swap_halves(x, half):
    n = x.shape[1]
    lane = lax.broadcasted_iota(jnp.int32, x.shape, 1)
    first = (lane % (2 * half)) < half
    return jnp.where(first, pltpu.roll(x, n - half, axis=1), pltpu.roll(x, half, axis=1))


def _rope(x, c, sg, half):
    return x * c + _swap_halves(x, half) * sg


def _rope_t(dy, c, sg, half):
    return dy * c - _swap_halves(dy, half) * sg


def _mm_nn(name, a, b, outs, *, tm, tn, extras=(), epilogue=None):
    M, K = a.shape
    N = b.shape[1]
    tm = min(tm, M)
    n_e = len(extras)

    def body(*refs):
        a_ref, b_ref = refs[:2]
        acc = jnp.dot(a_ref[...].astype(BF16), b_ref[...].astype(BF16), preferred_element_type=F32)
        vals = (acc,) * len(outs) if epilogue is None else epilogue(acc, *[r[...] for r in refs[2:2 + n_e]])
        for r, v in zip(refs[2 + n_e:], vals):
            r[...] = v.astype(r.dtype)

    block_bytes = (_nbytes((tm, K), a.dtype) + _nbytes((K, tn), b.dtype) + sum(_nbytes((tm, tn), e.dtype) for e in extras)
                   + sum(_nbytes((tm, tn), d) for d in outs) + _nbytes((tm, tn), F32))
    return pl.pallas_call(
        body, name=name, grid=(N // tn, M // tm),
        in_specs=[pl.BlockSpec((tm, K), lambda j, i: (i, 0)), pl.BlockSpec((K, tn), lambda j, i: (0, j))]
        + [pl.BlockSpec((tm, tn), lambda j, i: (i, j)) for _ in extras],
        out_specs=[pl.BlockSpec((tm, tn), lambda j, i: (i, j)) for _ in outs],
        out_shape=[jax.ShapeDtypeStruct((M, N), d) for d in outs],
        compiler_params=_cparams(("parallel", "parallel"), block_bytes),
    )(a, b, *extras)


def _mm_nt(name, a, b, outs, *, tm, tn, extras=(), epilogue=None):
    M, N = a.shape
    K = b.shape[0]
    tm = min(tm, M)
    n_e = len(extras)

    def body(*refs):
        a_ref, b_ref = refs[:2]
        acc = lax.dot_general(a_ref[...].astype(BF16), b_ref[...].astype(BF16), NT, preferred_element_type=F32)
        vals = (acc,) * len(outs) if epilogue is None else epilogue(acc, *[r[...] for r in refs[2:2 + n_e]])
        for r, v in zip(refs[2 + n_e:], vals):
            r[...] = v.astype(r.dtype)

    block_bytes = (_nbytes((tm, N), a.dtype) + _nbytes((tn, N), b.dtype) + sum(_nbytes((tm, tn), e.dtype) for e in extras)
                   + sum(_nbytes((tm, tn), d) for d in outs) + _nbytes((tm, tn), F32))
    return pl.pallas_call(
        body, name=name, grid=(K // tn, M // tm),
        in_specs=[pl.BlockSpec((tm, N), lambda j, i: (i, 0)), pl.BlockSpec((tn, N), lambda j, i: (j, 0))]
        + [pl.BlockSpec((tm, tn), lambda j, i: (i, j)) for _ in extras],
        out_specs=[pl.BlockSpec((tm, tn), lambda j, i: (i, j)) for _ in outs],
        out_shape=[jax.ShapeDtypeStruct((M, K), d) for d in outs],
        compiler_params=_cparams(("parallel", "parallel"), block_bytes),
    )(a, b, *extras)


def _mm_tn(name, a, b, *, tm, tn, ts):
    S, K = a.shape
    N = b.shape[1]
    ts = min(ts, S)

    def body(a_ref, b_ref, o_ref):
        @pl.when(pl.program_id(2) == 0)
        def _():
            o_ref[...] = jnp.zeros_like(o_ref)
        o_ref[...] += lax.dot_general(a_ref[...].astype(BF16), b_ref[...].astype(BF16), TN, preferred_element_type=F32)

    block_bytes = _nbytes((ts, tm), a.dtype) + _nbytes((ts, tn), b.dtype) + 2 * _nbytes((tm, tn), F32)
    return pl.pallas_call(
        body, name=name, grid=(K // tm, N // tn, S // ts),
        in_specs=[pl.BlockSpec((ts, tm), lambda i, j, s: (s, i)), pl.BlockSpec((ts, tn), lambda i, j, s: (s, j))],
        out_specs=pl.BlockSpec((tm, tn), lambda i, j, s: (i, j)),
        out_shape=jax.ShapeDtypeStruct((K, N), F32),
        compiler_params=_cparams(("parallel", "parallel", "arbitrary"), block_bytes),
    )(a, b)


def _lane_pack(cols, rows):
    lane = lax.broadcasted_iota(jnp.int32, (rows, LANES), 1)
    val = jnp.zeros((rows, LANES), F32)
    for h, c in enumerate(cols):
        val = jnp.where(lane == h, c, val)
    return val


def _mask(kb, row, tk):
    return kb * tk + lax.broadcasted_iota(jnp.int32, (1, tk), 1) <= row


def _kb_range(i, tq, tk):
    return (i * tq) // tk, ((i + 1) * tq + tk - 1) // tk


def _softmax_fwd(name, q, k, v, *, groups, heads, q_spec, k_spec, v_spec, o_width, tq, tk, ride=None):
    S = q.shape[0]
    nq = S // tq
    nh = len(heads)
    dv = heads[0][2].stop - heads[0][2].start

    def body(q_ref, k_ref, v_ref, o_ref, lse_ref):
        i = pl.program_id(1)
        q_all = q_ref[...]
        row = i * tq + lax.broadcasted_iota(jnp.int32, (tq, 1), 0)
        mid, hi = _kb_range(i, tq, tk)
        qhs = [q_all[:, qs] for qs, _, _ in heads]

        def blk(kb, carry, masked):
            r0 = pl.multiple_of(kb * tk, tk)
            k_all = k_ref[pl.ds(r0, tk), :]
            v_all = v_ref[pl.ds(r0, tk), :]
            if masked:
                ok = _mask(kb, row, tk)
            new = []
            for qh, (_, ks, vs), (m, l, acc) in zip(qhs, heads, carry):
                s = lax.dot_general(qh, k_all[:, ks], NT, preferred_element_type=F32)
                if masked:
                    s = jnp.where(ok, s, NEG)
                m_new = jnp.maximum(m, jnp.max(s, axis=1, keepdims=True))
                alpha = jnp.exp(m - m_new)
                p = jnp.exp(s - m_new)
                if masked:
                    p = jnp.where(ok, p, 0.0)
                l = alpha * l + jnp.sum(p, axis=1, keepdims=True)
                acc = alpha * acc + jnp.dot(p.astype(BF16), v_all[:, vs], preferred_element_type=F32)
                new.append((m_new, l, acc))
            return tuple(new)

        carry = tuple((jnp.full((tq, 1), NEG, F32), jnp.zeros((tq, 1), F32), jnp.zeros((tq, dv), F32)) for _ in range(nh))
        carry = lax.fori_loop(0, mid, functools.partial(blk, masked=False), carry)
        carry = lax.fori_loop(mid, hi, functools.partial(blk, masked=True), carry)
        o_ref[...] = jnp.concatenate([acc / l for _, l, acc in carry], axis=1).astype(o_ref.dtype)
        lse_ref[0] = _lane_pack([m + jnp.log(l) for m, l, _ in carry], tq)

    wo = nh * dv
    block_bytes = _nbytes(q_spec[0], q.dtype) + _nbytes(k_spec[0], k.dtype) + _nbytes(v_spec[0], v.dtype) + 4 * tq * (wo + LANES)
    return _pallas(
        body, name=name, grid=(groups, nq), in_specs=[pl.BlockSpec(*q_spec), pl.BlockSpec(*k_spec), pl.BlockSpec(*v_spec)],
        out_specs=[pl.BlockSpec((tq, wo), lambda g, i: (i, g)), pl.BlockSpec((1, tq, LANES), lambda g, i: (g, i, 0))],
        out_shape=[jax.ShapeDtypeStruct((S, o_width), BF16), jax.ShapeDtypeStruct((groups, S, LANES), F32)],
        args=[q, k, v], sem=("parallel", "arbitrary"), block_bytes=block_bytes, ride=ride)


def _softmax_bwd(name, q, k, v, o, do, lse, *, groups, heads, q_spec, k_spec, v_spec, tq, tk, scale, ride=None):
    S = q.shape[0]
    nq, nkb = S // tq, S // tk
    nh = len(heads)
    dv = heads[0][2].stop - heads[0][2].start
    wo = nh * dv
    wk, wv = k_spec[0][1], v_spec[0][1]

    def body(q_ref, k_ref, v_ref, o_ref, do_ref, lse_ref, dq_ref, dkt_ref, dvt_ref):
        i = pl.program_id(1)

        @pl.when(i == 0)
        def _():
            dkt_ref[...] = jnp.zeros_like(dkt_ref)
            dvt_ref[...] = jnp.zeros_like(dvt_ref)

        q_all = q_ref[...]
        o_all = o_ref[...].astype(F32)
        do_all = do_ref[...].astype(F32)
        lse_all = lse_ref[0]
        row = i * tq + lax.broadcasted_iota(jnp.int32, (tq, 1), 0)
        mid, hi = _kb_range(i, tq, tk)
        per_head = []
        for h, (qs, ks, vs) in enumerate(heads):
            osl = slice(h * dv, (h + 1) * dv)
            doh = do_all[:, osl]
            delta = jnp.sum(doh * o_all[:, osl], axis=1, keepdims=True)
            qh = q_all[:, qs]
            per_head.append((qh, qh.astype(F32).T.astype(BF16), doh.astype(BF16), doh.T.astype(BF16), delta, lse_all[:, h:h + 1], ks, vs))

        def blk(kb, dqs, masked):
            r0 = pl.multiple_of(kb * tk, tk)
            k_all = k_ref[pl.ds(r0, tk), :]
            v_all = v_ref[pl.ds(r0, tk), :]
            if masked:
                ok = _mask(kb, row, tk)
            dkt, dvt, new_dqs = [], [], []
            for (qh, qt, doh, dot_, delta, lse_h, ks, vs), dq in zip(per_head, dqs):
                kk, vv = k_all[:, ks], v_all[:, vs]
                s = lax.dot_general(qh, kk, NT, preferred_element_type=F32)
                p = jnp.exp(s - lse_h)
                if masked:
                    p = jnp.where(ok, p, 0.0)
                dp = lax.dot_general(doh, vv, NT, preferred_element_type=F32)
                ds = (p * (dp - delta)).astype(BF16)
                new_dqs.append(dq + jnp.dot(ds, kk, preferred_element_type=F32))
                dkt.append(jnp.dot(qt, ds, preferred_element_type=F32))
                dvt.append(jnp.dot(dot_, p.astype(BF16), preferred_element_type=F32))
            dkt_ref[kb] += jnp.concatenate(dkt, axis=0)
            dvt_ref[kb] += jnp.concatenate(dvt, axis=0)
            return tuple(new_dqs)

        dqs = tuple(jnp.zeros((tq, qs.stop - qs.start), F32) for qs, _, _ in heads)
        dqs = lax.fori_loop(0, mid, functools.partial(blk, masked=False), dqs)
        dqs = lax.fori_loop(mid, hi, functools.partial(blk, masked=True), dqs)
        dq_ref[...] = jnp.concatenate([dq * scale for dq in dqs], axis=1)

    in_specs = [pl.BlockSpec(*q_spec), pl.BlockSpec(*k_spec), pl.BlockSpec(*v_spec),
                pl.BlockSpec((tq, wo), lambda g, i: (i, g)), pl.BlockSpec((tq, wo), lambda g, i: (i, g)),
                pl.BlockSpec((1, tq, LANES), lambda g, i: (g, i, 0))]
    args = [q, k, v, o, do, lse]
    out_specs = [pl.BlockSpec(*q_spec), pl.BlockSpec((nkb, wk, tk), lambda g, i: (0, g, 0)), pl.BlockSpec((nkb, wv, tk), lambda g, i: (0, g, 0))]
    out_shape = [jax.ShapeDtypeStruct(q.shape, F32), jax.ShapeDtypeStruct((nkb, k.shape[1], tk), F32),
                 jax.ShapeDtypeStruct((nkb, v.shape[1], tk), F32)]
    block_bytes = (_nbytes(q_spec[0], q.dtype) + _nbytes(k_spec[0], k.dtype) + _nbytes(v_spec[0], v.dtype) + 6 * tq * wo + 4 * tq * LANES
                   + _nbytes(q_spec[0], F32) + _nbytes(k_spec[0], F32) + _nbytes(v_spec[0], F32))
    return _pallas(body, name=name, grid=(groups, nq), in_specs=in_specs, out_specs=out_specs, out_shape=out_shape, args=args,
                   sem=("parallel", "arbitrary"), block_bytes=block_bytes, ride=ride)


def _swa_window(i, tq, row):
    start = pl.multiple_of(jnp.maximum(i * tq - SWA_WINDOW, 0), SWA_WINDOW)
    col = start + lax.broadcasted_iota(jnp.int32, (1, tq + SWA_WINDOW), 1)
    return start, (col <= row) & ((row - col) < SWA_WINDOW)


def _swa_fwd(name, q, k, v, sinks, *, tq):
    S = q.shape[0]
    hd, span = SWA_HEAD_DIM, tq + SWA_WINDOW
    scale = hd ** -0.5

    def body(sink_ref, q_ref, k_ref, v_ref, o_ref, lse_ref):
        g, i = pl.program_id(0), pl.program_id(1)
        row = i * tq + lax.broadcasted_iota(jnp.int32, (tq, 1), 0)
        start, ok = _swa_window(i, tq, row)
        kk, vv = k_ref[pl.ds(start, span), :], v_ref[pl.ds(start, span), :]
        q_all = q_ref[...]
        outs, lses = [], []
        for h in range(SWA_G):
            s = lax.dot_general(q_all[:, h * hd:(h + 1) * hd] * scale, kk, NT, preferred_element_type=F32)
            s = jnp.where(ok, s, NEG)
            sink = sink_ref[g * SWA_G + h]
            m = jnp.maximum(jnp.max(s, axis=1, keepdims=True), sink)
            p = jnp.exp(s - m)
            l = jnp.sum(p, axis=1, keepdims=True) + jnp.exp(sink - m)
            outs.append(jnp.dot(p.astype(BF16), vv, preferred_element_type=F32) / l)
            lses.append(m + jnp.log(l))
        o_ref[...] = jnp.concatenate(outs, axis=1).astype(o_ref.dtype)
        lse_ref[0] = _lane_pack(lses, tq)

    wq = SWA_G * hd
    block_bytes = 2 * tq * wq * 2 + 2 * 2 * S * hd + 4 * tq * LANES
    return pl.pallas_call(
        body, name=name, grid=(SWA_KV_HEADS, S // tq),
        in_specs=[pl.BlockSpec(memory_space=pltpu.SMEM), pl.BlockSpec((tq, wq), lambda g, i: (i, g)),
                  pl.BlockSpec((S, hd), lambda g, i: (g, 0)), pl.BlockSpec((S, hd), lambda g, i: (g, 0))],
        out_specs=[pl.BlockSpec((tq, wq), lambda g, i: (i, g)), pl.BlockSpec((1, tq, LANES), lambda g, i: (g, i, 0))],
        out_shape=[jax.ShapeDtypeStruct((S, SWA_HEADS * hd), BF16), jax.ShapeDtypeStruct((SWA_KV_HEADS, S, LANES), F32)],
        compiler_params=_cparams(("parallel", "parallel"), block_bytes),
    )(sinks, q, k, v)


def _swa_bwd(name, q, k, v, o, do, lse, sinks, *, tq):
    S = q.shape[0]
    hd, span = SWA_HEAD_DIM, tq + SWA_WINDOW
    scale = hd ** -0.5

    def body(sink_ref, q_ref, k_ref, v_ref, o_ref, do_ref, lse_ref, dq_ref, dk_ref, dv_ref, dsink_ref):
        g, i = pl.program_id(0), pl.program_id(1)

        @pl.when(i == 0)
        def _():
            dk_ref[...] = jnp.zeros_like(dk_ref)
            dv_ref[...] = jnp.zeros_like(dv_ref)
            dsink_ref[...] = jnp.zeros_like(dsink_ref)

        row = i * tq + lax.broadcasted_iota(jnp.int32, (tq, 1), 0)
        start, ok = _swa_window(i, tq, row)
        kk, vv = k_ref[pl.ds(start, span), :], v_ref[pl.ds(start, span), :]
        q_all, o_all, do_all, lse_all = q_ref[...], o_ref[...].astype(F32), do_ref[...].astype(F32), lse_ref[0]
        dqs, dk_c, dv_c = [], None, None
        for h in range(SWA_G):
            sl = slice(h * hd, (h + 1) * hd)
            qh = q_all[:, sl] * scale
            doh = do_all[:, sl]
            delta = jnp.sum(doh * o_all[:, sl], axis=1, keepdims=True)
            lse_h = lse_all[:, h:h + 1]
            doh = doh.astype(BF16)
            s = jnp.where(ok, lax.dot_general(qh, kk, NT, preferred_element_type=F32), NEG)
            p = jnp.exp(s - lse_h)
            ds = p * (lax.dot_general(doh, vv, NT, preferred_element_type=F32) - delta)
            dqs.append(jnp.dot(ds.astype(BF16), kk, preferred_element_type=F32) * scale)
            dk_h = jnp.dot(ds.T.astype(BF16), qh, preferred_element_type=F32)
            dv_h = jnp.dot(p.T.astype(BF16), doh, preferred_element_type=F32)
            dk_c = dk_h if dk_c is None else dk_c + dk_h
            dv_c = dv_h if dv_c is None else dv_c + dv_h
            p_sink = jnp.exp(sink_ref[g * SWA_G + h] - lse_h)
            dsink_ref[0, h:h + 1, :] += jnp.broadcast_to(-jnp.sum(p_sink * delta, axis=0, keepdims=True), (1, LANES))
        dq_ref[...] = jnp.concatenate(dqs, axis=1)
        dk_ref[pl.ds(start, span), :] += dk_c
        dv_ref[pl.ds(start, span), :] += dv_c

    wq = SWA_G * hd
    qb = pl.BlockSpec((tq, wq), lambda g, i: (i, g))
    kb = pl.BlockSpec((S, hd), lambda g, i: (g, 0))
    block_bytes = 2 * tq * wq * (2 + 2 + 4 + 4) + 2 * S * hd * (2 + 2 + 4 + 4) + 4 * tq * LANES
    return pl.pallas_call(
        body, name=name, grid=(SWA_KV_HEADS, S // tq),
        in_specs=[pl.BlockSpec(memory_space=pltpu.SMEM), qb, kb, kb, qb, qb, pl.BlockSpec((1, tq, LANES), lambda g, i: (g, i, 0))],
        out_specs=[qb, kb, kb, pl.BlockSpec((1, 8, LANES), lambda g, i: (g, 0, 0))],
        out_shape=[jax.ShapeDtypeStruct(q.shape, F32), jax.ShapeDtypeStruct(k.shape, F32), jax.ShapeDtypeStruct(v.shape, F32),
                   jax.ShapeDtypeStruct((SWA_KV_HEADS, 8, LANES), F32)],
        compiler_params=_cparams(("parallel", "arbitrary"), block_bytes),
    )(sinks, q, k, v, o, do, lse)


def _split_dot(x, u2):
    hi = x.astype(BF16)
    lo = (x - hi.astype(F32)).astype(BF16)
    return jnp.dot(jnp.concatenate([hi, lo], axis=1), u2, preferred_element_type=F32)


def _suffix_ones():
    r = lax.broadcasted_iota(jnp.int32, (2 * LANES, 2 * LANES), 0) % LANES
    c = lax.broadcasted_iota(jnp.int32, (2 * LANES, 2 * LANES), 1)
    return ((r > c) | (c >= LANES)).astype(BF16)


def _suffix_scan(x, uo, run):
    nc = x.shape[1] // LANES
    out = [None] * nc
    for c in reversed(range(nc)):
        st = _split_dot(x[:, c * LANES:(c + 1) * LANES], uo)
        out[c] = st[:, :LANES] + run
        run = run + st[:, LANES:]
    return (out[0] if nc == 1 else jnp.concatenate(out, axis=1)), run


SB_DEAD_LOG = -110.0


def _sb_logits(qh, kk, r0, row, masked):
    z = lax.dot_general(qh, kk, NT, preferred_element_type=F32)
    nz = -z
    l = jnp.minimum(nz, 0.0) - jnp.log(1.0 + jnp.exp(jnp.minimum(z, nz)))
    ok = None
    if masked:
        col = r0 + lax.broadcasted_iota(jnp.int32, (1, kk.shape[0]), 1)
        ok = col < row
        l = jnp.where(ok, l, 0.0)
    return z, l, ok


def _sb_walk(blk, carry, i, tq, ts):
    carry = blk(pl.multiple_of(i * tq, tq), tq, carry, True)

    def live(c):
        t, heads = c
        top = jnp.max(heads[0][0])
        for h in heads[1:]:
            top = jnp.maximum(top, jnp.max(h[0]))
        return jnp.logical_and(t < i * (tq // ts), top > SB_DEAD_LOG)

    def step(c):
        t, heads = c
        return t + 1, blk(pl.multiple_of(i * tq - (t + 1) * ts, ts), ts, heads, False)

    return lax.while_loop(live, step, (jnp.int32(0), carry))[1]


def _sb_fwd(name, pb, *, q_cb, k_cb, v_cb, tq, ts, scale):
    S = pb.shape[0]
    nq = S // tq
    hd = SB_HEAD_DIM
    groups = SB_HEADS * hd // LANES
    nh = LANES // hd

    def body(q_ref, k_ref, v_ref, o16_ref, o32_ref):
        i = pl.program_id(1)
        q_all = q_ref[...]
        row = i * tq + lax.broadcasted_iota(jnp.int32, (tq, 1), 0)
        uo = _suffix_ones()
        sls = [slice(h * hd, (h + 1) * hd) for h in range(nh)]
        qhs = [q_all[:, sl] * scale for sl in sls]

        def blk(r0, width, carry, masked):
            k_all = k_ref[pl.ds(r0, width), :]
            v_all = v_ref[pl.ds(r0, width), :]
            new = []
            for qh, sl, (run_l, acc) in zip(qhs, sls, carry):
                z, l, ok = _sb_logits(qh, k_all[:, sl], r0, row, masked)
                tail, run_l = _suffix_scan(l, uo, run_l)
                e = z + l + tail
                if masked:
                    e = jnp.where(ok, e, NEG)
                new.append((run_l, acc + jnp.dot(jnp.exp(e).astype(BF16), v_all[:, sl], preferred_element_type=F32)))
            return tuple(new)

        carry = tuple((jnp.zeros((tq, LANES), F32), jnp.zeros((tq, hd), F32)) for _ in range(nh))
        carry = _sb_walk(blk, carry, i, tq, ts)
        o = jnp.concatenate([acc for _, acc in carry], axis=1)
        o16_ref[...] = o.astype(BF16)
        o32_ref[...] = o

    block_bytes = 2 * tq * LANES + 2 * 2 * S * LANES + 6 * tq * LANES
    return pl.pallas_call(
        body, name=name, grid=(groups, nq),
        in_specs=[pl.BlockSpec((tq, LANES), lambda g, i: (i, q_cb + g)), pl.BlockSpec((S, LANES), lambda g, i: (0, k_cb + g)),
                  pl.BlockSpec((S, LANES), lambda g, i: (0, v_cb + g))],
        out_specs=[pl.BlockSpec((tq, LANES), lambda g, i: (i, g)), pl.BlockSpec((tq, LANES), lambda g, i: (i, g))],
        out_shape=[jax.ShapeDtypeStruct((S, groups * LANES), BF16), jax.ShapeDtypeStruct((S, groups * LANES), F32)],
        compiler_params=_cparams(("parallel", "arbitrary"), block_bytes),
    )(pb, pb, pb)


def _sb_bwd(name, pb, o32, do, *, q_cb, k_cb, v_cb, tq, ts, scale):
    S = pb.shape[0]
    nq = S // tq
    hd = SB_HEAD_DIM
    groups = SB_HEADS * hd // LANES
    nh = LANES // hd

    def body(q_ref, k_ref, v_ref, o_ref, do_ref, dq_ref, dkt_ref, dvt_ref):
        i = pl.program_id(1)

        @pl.when(i == 0)
        def _():
            dkt_ref[...] = jnp.zeros_like(dkt_ref)
            dvt_ref[...] = jnp.zeros_like(dvt_ref)

        q_all = q_ref[...]
        o_all = o_ref[...]
        do_all = do_ref[...].astype(F32)
        row = i * tq + lax.broadcasted_iota(jnp.int32, (tq, 1), 0)
        uo = _suffix_ones()
        per_head = []
        for h in range(nh):
            sl = slice(h * hd, (h + 1) * hd)
            doh = do_all[:, sl].astype(BF16)
            total = jnp.sum(doh.astype(F32) * o_all[:, sl], axis=1, keepdims=True)
            qh = q_all[:, sl] * scale
            per_head.append((qh, qh.astype(F32).T.astype(BF16), doh, doh.astype(F32).T.astype(BF16), jnp.broadcast_to(total, (tq, LANES)), sl))

        def blk(r0, width, carry, masked):
            k_all = k_ref[pl.ds(r0, width), :]
            v_all = v_ref[pl.ds(r0, width), :]
            new, dk_c, dv_c = [], [], []
            for (qh, qt, doh, dot_, total, sl), (run_l, run_g, dq) in zip(per_head, carry):
                kk, vv = k_all[:, sl], v_all[:, sl]
                z, l, ok = _sb_logits(qh, kk, r0, row, masked)
                tail, run_l = _suffix_scan(l, uo, run_l)
                e = z + l
                beta = jnp.exp(e)
                e = e + tail
                if masked:
                    e = jnp.where(ok, e, NEG)
                a = jnp.exp(e).astype(BF16).astype(F32)
                gr = lax.dot_general(doh, vv, NT, preferred_element_type=F32) * a
                right, run_g = _suffix_scan(gr, uo, run_g)
                nc = width // LANES
                prefix = (total if nc == 1 else jnp.tile(total, (1, nc))) - right
                dz = gr - beta * prefix
                if masked:
                    dz = jnp.where(ok, dz, 0.0)
                dzb = dz.astype(BF16)
                new.append((run_l, run_g, dq + jnp.dot(dzb, kk, preferred_element_type=F32)))
                dk_c.append(jnp.dot(qt, dzb, preferred_element_type=F32))
                dv_c.append(jnp.dot(dot_, a.astype(BF16), preferred_element_type=F32))
            dkt, dvt = jnp.concatenate(dk_c, axis=0), jnp.concatenate(dv_c, axis=0)
            for j in range(width // ts):
                dkt_ref[r0 // ts + j] += dkt[:, j * ts:(j + 1) * ts]
                dvt_ref[r0 // ts + j] += dvt[:, j * ts:(j + 1) * ts]
            return tuple(new)

        zc = jnp.zeros((tq, LANES), F32)
        carry = tuple((zc, zc, jnp.zeros((tq, hd), F32)) for _ in range(nh))
        carry = _sb_walk(blk, carry, i, tq, ts)
        dq_ref[...] = jnp.concatenate([c[2] * scale for c in carry], axis=1)

    W = groups * LANES
    block_bytes = 2 * tq * LANES + 2 * 2 * S * LANES + 3 * 4 * tq * LANES + 2 * 4 * S * LANES
    return pl.pallas_call(
        body, name=name, grid=(groups, nq),
        in_specs=[pl.BlockSpec((tq, LANES), lambda g, i: (i, q_cb + g)), pl.BlockSpec((S, LANES), lambda g, i: (0, k_cb + g)),
                  pl.BlockSpec((S, LANES), lambda g, i: (0, v_cb + g)), pl.BlockSpec((tq, LANES), lambda g, i: (i, g)),
                  pl.BlockSpec((tq, LANES), lambda g, i: (i, g))],
        out_specs=[pl.BlockSpec((tq, LANES), lambda g, i: (i, g)), pl.BlockSpec((S // ts, LANES, ts), lambda g, i: (0, g, 0)),
                   pl.BlockSpec((S // ts, LANES, ts), lambda g, i: (0, g, 0))],
        out_shape=[jax.ShapeDtypeStruct((S, W), F32)] + [jax.ShapeDtypeStruct((S // ts, W, ts), F32)] * 2,
        compiler_params=_cparams(("parallel", "arbitrary"), block_bytes),
    )(pb, pb, pb, o32, do)


def _exchange_copies(src_refs, out_refs, send_sems, recv_sems, local_sems, per_peer):
    x, y, c = lax.axis_index("x"), lax.axis_index("y"), lax.axis_index("c")
    me = 4 * x + 2 * y + c
    n = len(src_refs)

    def copy(j, k):
        px, py, pc = x ^ (k >> 2), y ^ ((k >> 1) & 1), c ^ (k & 1)
        s = src_refs[j].at[4 * px + 2 * py + pc] if per_peer else src_refs[j]
        return pltpu.make_async_remote_copy(
            src_ref=s, dst_ref=out_refs[j].at[me], send_sem=send_sems.at[j, k - 1], recv_sem=recv_sems.at[j, k - 1],
            device_id=(px, py, pc), device_id_type=pl.DeviceIdType.MESH)

    mine = [pltpu.make_async_copy(src_refs[j].at[me] if per_peer else src_refs[j], out_refs[j].at[me], local_sems.at[j]) for j in range(n)]
    return mine, [copy(j, k) for k in range(1, N_DEV) for j in range(n)]


def _exchange_start(*refs, per_peer):
    mine, copies = _exchange_copies(*refs, per_peer)
    for cp in mine + copies:
        cp.start()


def _exchange_wait(*refs, per_peer):
    mine, copies = _exchange_copies(*refs, per_peer)
    for cp in copies:
        cp.wait_recv()
    for cp in copies:
        cp.wait_send()
    for cp in mine:
        cp.wait()


def _exchange_specs(srcs, per_peer):
    n = len(srcs)
    out_shape = [jax.ShapeDtypeStruct((N_DEV,) + tuple(s.shape[1:] if per_peer else s.shape), s.dtype) for s in srcs]
    sems = [pltpu.SemaphoreType.DMA((n, N_DEV - 1)), pltpu.SemaphoreType.DMA((n, N_DEV - 1)), pltpu.SemaphoreType.DMA((n,))]
    return [pl.BlockSpec(memory_space=pltpu.HBM)] * n, out_shape, sems


def _exchange(name, srcs, per_peer):
    n = len(srcs)

    def body(*refs):
        parts = (refs[:n], refs[n:2 * n], *refs[2 * n:])
        _exchange_start(*parts, per_peer=per_peer)
        _exchange_wait(*parts, per_peer=per_peer)

    hbm, out_shape, sems = _exchange_specs(srcs, per_peer)
    return pl.pallas_call(body, name=name, in_specs=hbm, out_specs=hbm, out_shape=out_shape, scratch_shapes=sems)(*srcs)


def _pallas(body, *, name, grid, in_specs, out_specs, out_shape, args, sem, block_bytes, ride=None):
    if ride is None:
        return pl.pallas_call(body, name=name, grid=grid, in_specs=in_specs, out_specs=out_specs, out_shape=out_shape,
                              compiler_params=_cparams(sem, block_bytes))(*args), None
    srcs, per_peer = ride
    n, n_in, n_out = len(srcs), len(in_specs), len(out_specs)

    def riding(*refs):
        ins, xsrc = refs[:n_in], refs[n_in:n_in + n]
        outs, xout = refs[n_in + n:n_in + n + n_out], refs[n_in + n + n_out:n_in + 2 * n + n_out]
        parts = (xsrc, xout, *refs[n_in + 2 * n + n_out:])
        ids = [pl.program_id(d) for d in range(len(grid))]
        first = functools.reduce(jnp.logical_and, [i == 0 for i in ids])
        last = functools.reduce(jnp.logical_and, [i == g - 1 for i, g in zip(ids, grid)])
        pl.when(first)(functools.partial(_exchange_start, *parts, per_peer=per_peer))
        body(*ins, *outs)
        pl.when(last)(functools.partial(_exchange_wait, *parts, per_peer=per_peer))

    hbm, x_shape, sems = _exchange_specs(srcs, per_peer)
    res = pl.pallas_call(riding, name=name, grid=grid, in_specs=list(in_specs) + hbm, out_specs=list(out_specs) + hbm,
                         out_shape=list(out_shape) + x_shape, scratch_shapes=sems,
                         compiler_params=_cparams(("arbitrary",) * len(grid), block_bytes))(*args, *srcs)
    return res[:n_out], res[n_out:]


def _adamw(name, parts, w, m, v, *, tile):
    R, C = w.shape

    def body(p_ref, w_ref, m_ref, v_ref, g_ref, d_ref, nm_ref, nv_ref):
        g = p_ref[0].astype(F32)
        for d in range(1, N_DEV):
            g = g + p_ref[d].astype(F32)
        wv = w_ref[...]
        mm = ADAM_B1 * m_ref[...] + (1.0 - ADAM_B1) * g
        vv = ADAM_B2 * v_ref[...] + (1.0 - ADAM_B2) * jnp.square(g)
        m_hat = mm / (1.0 - ADAM_B1 ** ADAM_STEP)
        v_hat = vv / (1.0 - ADAM_B2 ** ADAM_STEP)
        g_ref[...] = g
        d_ref[...] = -ADAM_LR * (m_hat / (jnp.sqrt(v_hat) + ADAM_EPS) + ADAM_WD * wv)
        nm_ref[...] = mm
        nv_ref[...] = vv

    blk = pl.BlockSpec((tile, C), lambda i: (i, 0))
    block_bytes = N_DEV * _nbytes((tile, C), parts.dtype) + 7 * _nbytes((tile, C), F32)
    return pl.pallas_call(
        body, name=name, grid=(R // tile,),
        in_specs=[pl.BlockSpec((N_DEV, tile, C), lambda i: (0, i, 0)), blk, blk, blk],
        out_specs=[blk] * 4, out_shape=[jax.ShapeDtypeStruct((R, C), F32)] * 4,
        compiler_params=_cparams(("parallel",), block_bytes),
    )(parts, w, m, v)


def _perm_w_in(w):
    cols = [w[:, ORIG_COLS[n][0]:ORIG_COLS[n][0] + ORIG_COLS[n][1]] for n in PERM_ORDER]
    cols.append(jnp.zeros((w.shape[0], IN_WIDTH_P - IN_WIDTH), w.dtype))
    return jnp.concatenate(cols, axis=1)


def _unperm_w_in(wp):
    order = sorted(ORIG_COLS, key=lambda n: ORIG_COLS[n][0])
    return jnp.concatenate([wp[:, PERM_OFF[n]:PERM_OFF[n] + ORIG_COLS[n][1]] for n in order], axis=1)


def _perm_w_uq(w):
    w3 = w.reshape(w.shape[0], MLA_HEADS, MLA_NOPE + MLA_ROPE)
    return jnp.pad(w3, ((0, 0), (0, 0), (0, LANES - MLA_NOPE - MLA_ROPE))).reshape(w.shape[0], MLA_QW)


def _unperm_w_uq(wp):
    return wp.reshape(wp.shape[0], MLA_HEADS, LANES)[:, :, :MLA_NOPE + MLA_ROPE].reshape(wp.shape[0], -1)


def _perm_w_ukv(w):
    w3 = w.reshape(w.shape[0], MLA_HEADS, MLA_NOPE + MLA_V)
    kp = jnp.pad(w3[:, :, :MLA_NOPE], ((0, 0), (0, 0), (0, LANES - MLA_NOPE))).reshape(w.shape[0], MLA_HEADS * LANES)
    return jnp.concatenate([kp, w3[:, :, MLA_NOPE:].reshape(w.shape[0], MLA_HEADS * MLA_V)], axis=1)


def _unperm_w_ukv(wp):
    n = wp.shape[0]
    kp = wp[:, :MLA_HEADS * LANES].reshape(n, MLA_HEADS, LANES)[:, :, :MLA_NOPE]
    vp = wp[:, MLA_HEADS * LANES:].reshape(n, MLA_HEADS, MLA_V)
    return jnp.concatenate([kp, vp], axis=2).reshape(n, MLA_HEADS * (MLA_NOPE + MLA_V))


BIG = ("w_in", "w_uq", "w_ukv", "w_o_mla", "w_o_swa", "w_o_sb", "w_out", "w_up", "w_down")
W_EARLY = ("w_in", "w_uq", "w_ukv")
W_LATE = ("w_o_mla", "w_o_swa", "w_o_sb", "w_out", "w_up", "w_down")
ROW_SHARDED = ("w_out", "w_down")
SMALL = ("g_mix_pre", "b_gate", "g_q_lat", "g_kv_lat", "swa_sinks", "g_mix_post", "g_mlp_pre", "g_mlp_post")
PACK_COLS = 1024


def _to_shards(name, full):
    L, R, C = full.shape
    if name in ROW_SHARDED:
        return full.reshape(L, N_DEV, R // N_DEV, C).transpose(1, 0, 2, 3).reshape(N_DEV, L * R // N_DEV, C)
    return full.reshape(L, R, N_DEV, C // N_DEV).transpose(2, 0, 1, 3).reshape(N_DEV, L * R, C // N_DEV)


def _from_shards(name, gathered, shard_shape):
    L, r, c = shard_shape
    a = gathered.reshape(N_DEV, L, r, c)
    if name in ROW_SHARDED:
        return a.transpose(1, 0, 2, 3).reshape(L, N_DEV * r, c)
    return a.transpose(1, 2, 0, 3).reshape(L, r, N_DEV * c)


def _tables(positions):
    pos = positions.astype(F32).reshape(-1, 1)

    def cs(d):
        inv = 1.0 / (ROPE_THETA ** (jnp.arange(0, d, 2, dtype=F32) / d))
        ang = pos * inv
        c, s = jnp.cos(ang), jnp.sin(ang)
        return jnp.concatenate([c, c], axis=1), jnp.concatenate([-s, s], axis=1)

    c64, s64 = cs(SWA_HEAD_DIM)
    c32, s32 = cs(MLA_ROPE)
    n = pos.shape[0]
    one, zero = jnp.ones((n, MLA_NOPE), F32), jnp.zeros((n, MLA_NOPE), F32)
    pad0 = jnp.zeros((n, LANES - MLA_NOPE - MLA_ROPE), F32)
    cq = jnp.concatenate([one, c32, pad0], axis=1)
    sq = jnp.concatenate([zero, s32, pad0], axis=1)
    padk = jnp.zeros((n, LANES - MLA_ROPE), F32)
    return dict(
        c_swa_q=jnp.tile(c64, (1, SWA_HEADS)), s_swa_q=jnp.tile(s64, (1, SWA_HEADS)),
        c_swa_k=jnp.tile(c64, (1, SWA_KV_HEADS)), s_swa_k=jnp.tile(s64, (1, SWA_KV_HEADS)),
        c_mla_q=jnp.tile(cq, (1, MLA_HEADS)), s_mla_q=jnp.tile(sq, (1, MLA_HEADS)),
        c_mla_k=jnp.concatenate([c32, padk], axis=1), s_mla_k=jnp.concatenate([s32, padk], axis=1))


def _cb(name):
    return PERM_OFF[name] // PERM_WIDTH[name]


MLA_SCALE = (MLA_NOPE + MLA_ROPE) ** -0.5
MLA_SPEC = dict(groups=MLA_HEADS // 2, tq=512, tk=512,
                heads=[(slice(h * LANES, (h + 1) * LANES), slice(h * LANES, (h + 1) * LANES), slice(h * MLA_V, (h + 1) * MLA_V)) for h in range(2)])
SWA_G = SWA_HEADS // SWA_KV_HEADS
SWA_TQ = 256
SB_SPEC = dict(tq=512, ts=256, scale=SB_HEAD_DIM ** -0.5, q_cb=PERM_OFF["q_sb"] // LANES, k_cb=PERM_OFF["k_sb"] // LANES,
               v_cb=PERM_OFF["v_sb"] // LANES)


def _mla_specs(S):
    tq = MLA_SPEC["tq"]
    return dict(q_spec=((tq, 2 * LANES), lambda g, i: (i, g)), k_spec=((S, 2 * LANES), lambda g, i: (0, g)),
                v_spec=((S, 2 * MLA_V), lambda g, i: (0, g)))


def _layer_fwd(l, x, W, P, tb, T, ride, late):
    S = x.shape[0]
    nt = S // T
    h, = _rowwise(f"l{l}_norm_in", lambda xv, g: (_rms(xv, g),), [_ri(x, T), _bi(P["g_mix_pre"])], [_ro(S, D_MODEL, BF16, T)], n_tiles=nt)
    proj, proj16 = _mm_nn(f"l{l}_mm_in", h, W["w_in"], [F32, BF16], tm=512, tn=IN_WIDTH_P // 2)

    def mix_prep(cq, ckv, qs, ks, vs, gq, gkv, cq_t, sq_t, ck_t, sk_t):
        ksr = _rope(ks, ck_t, sk_t, SWA_HEAD_DIM // 2)
        hd = SWA_HEAD_DIM
        return (_rms(cq, gq), _rms(ckv, gkv), _rope(qs, cq_t, sq_t, hd // 2),
                [ksr[:, :hd], ksr[:, hd:]], [vs[:, :hd], vs[:, hd:]])

    kv3 = lambda dt: ((SWA_KV_HEADS, S, SWA_HEAD_DIM), dt, (SWA_KV_HEADS, T, SWA_HEAD_DIM), lambda i: (0, i, 0))
    cqn, ckvn, q_swa, k_swa, v_swa = _rowwise(
        f"l{l}_mix_prep", mix_prep,
        [_ri(proj, T, 256, _cb("c_q")), _ri(proj, T, 128, _cb("c_kv")), _ri(proj, T, 512, _cb("q_swa")), _ri(proj, T, 128, _cb("k_swa")),
         _ri(proj, T, 128, _cb("v_swa")), _bi(P["g_q_lat"]), _bi(P["g_kv_lat"]), _ri(tb["c_swa_q"], T), _ri(tb["s_swa_q"], T),
         _ri(tb["c_swa_k"], T), _ri(tb["s_swa_k"], T)],
        [_ro(S, 256, BF16, T), _ro(S, 128, BF16, T), _ro(S, 512, BF16, T), kv3(BF16), kv3(BF16)], n_tiles=nt)
    k_swa = k_swa.reshape(SWA_KV_HEADS * S, SWA_HEAD_DIM)
    v_swa = v_swa.reshape(SWA_KV_HEADS * S, SWA_HEAD_DIM)
    q_lat, = _mm_nn(f"l{l}_mm_uq", cqn, W["w_uq"], [F32], tm=1024, tn=MLA_QW)
    kv_lat, = _mm_nn(f"l{l}_mm_ukv", ckvn, W["w_ukv"], [F32], tm=1024, tn=MLA_KVW)

    def mla_prep(q, kk, vv, kr, cq_t, sq_t, ck_t, sk_t):
        kpe = pltpu.roll(_rope(kr, ck_t, sk_t, MLA_ROPE // 2), MLA_NOPE, axis=1)
        return _rope(q, cq_t, sq_t, MLA_ROPE // 2) * MLA_SCALE, kk + jnp.tile(kpe, (1, MLA_HEADS)), vv

    q_mla, k_mla, v_mla = _rowwise(
        f"l{l}_mla_prep", mla_prep,
        [_ri(q_lat, T), _ri(kv_lat, T, MLA_HEADS * LANES, 0), _ri(kv_lat, T, MLA_HEADS * MLA_V, 2), _ri(proj, T, 128, _cb("k_rope")),
         _ri(tb["c_mla_q"], T), _ri(tb["s_mla_q"], T), _ri(tb["c_mla_k"], T), _ri(tb["s_mla_k"], T)],
        [_ro(S, MLA_QW, BF16, T), _ro(S, MLA_HEADS * LANES, BF16, T), _ro(S, MLA_HEADS * MLA_V, BF16, T)], n_tiles=nt)

    (att_a, lse_a), rode = _softmax_fwd(f"l{l}_mla_fwd", q_mla, k_mla, v_mla, o_width=MLA_HEADS * MLA_V, ride=ride, **MLA_SPEC, **_mla_specs(S))
    W = {**W, **late(rode)}
    att_b, lse_b = _swa_fwd(f"l{l}_swa_fwd", q_swa, k_swa, v_swa, P["swa_sinks"], tq=SWA_TQ)
    att_c, att_c32 = _sb_fwd(f"l{l}_sb_fwd", proj16, **SB_SPEC)
    o_a, = _mm_nn(f"l{l}_mm_oa", att_a, W["w_o_mla"], [F32], tm=1024, tn=D_MODEL)
    o_b, = _mm_nn(f"l{l}_mm_ob", att_b, W["w_o_swa"], [F32], tm=1024, tn=D_MODEL)
    o_c, = _mm_nn(f"l{l}_mm_oc", att_c, W["w_o_sb"], [F32], tm=1024, tn=D_MODEL)

    def gate_mix(gl, b, oa, ob, oc):
        gt = jax.nn.sigmoid(gl + b)
        return (gt[:, :D_MODEL] * oa + gt[:, D_MODEL:2 * D_MODEL] * ob + gt[:, 2 * D_MODEL:] * oc,)

    mixed, = _rowwise(f"l{l}_gate_mix", gate_mix, [_ri(proj, T, 3072, 0), _bi(P["b_gate"]), _ri(o_a, T), _ri(o_b, T), _ri(o_c, T)],
                      [_ro(S, D_MODEL, BF16, T)], n_tiles=nt)
    y, = _mm_nn(f"l{l}_mm_out", mixed, W["w_out"], [F32], tm=1024, tn=D_MODEL)

    def resid_norm(xv, yv, gpost, gpre):
        x1 = xv + _rms(yv, gpost)
        return x1, _rms(x1, gpre)

    x1, h2 = _rowwise(f"l{l}_resid_norm", resid_norm, [_ri(x, T), _ri(y, T), _bi(P["g_mix_post"]), _bi(P["g_mlp_pre"])],
                      [_ro(S, D_MODEL, F32, T), _ro(S, D_MODEL, BF16, T)], n_tiles=nt)
    up, u = _mm_nn(f"l{l}_mm_up", h2, W["w_up"], [F32, BF16], tm=512, tn=2048,
                   epilogue=lambda acc: (acc, jnp.square(jnp.maximum(acc, 0.0))))
    dn, = _mm_nn(f"l{l}_mm_down", u, W["w_down"], [F32], tm=512, tn=D_MODEL)
    x2, = _rowwise(f"l{l}_resid_out", lambda xv, dv, g: (xv + _rms(dv, g),), [_ri(x1, T), _ri(dn, T), _bi(P["g_mlp_post"])],
                   [_ro(S, D_MODEL, F32, T)], n_tiles=nt)
    saved = dict(x=x, h=h, proj=proj, proj16=proj16, cqn=cqn, ckvn=ckvn, q_swa=q_swa, k_swa=k_swa, v_swa=v_swa, q_mla=q_mla, k_mla=k_mla,
                 v_mla=v_mla, att_a=att_a, lse_a=lse_a, att_b=att_b, lse_b=lse_b, att_c=att_c, att_c32=att_c32, o_a=o_a, o_b=o_b, o_c=o_c,
                 mixed=mixed, y=y, x1=x1, h2=h2, up=up, u=u, dn=dn)
    return x2, saved, W, rode


def _layer_bwd(l, dx2, sv, W, P, tb, T, pending, to_send):
    S = dx2.shape[0]
    nt = S // T
    G = {}

    def post_norm_bwd(v, g, dy):
        return _rms_bwd(v, g, dy)

    d_dn, G["g_mlp_post"] = _rowwise(f"l{l}_b_post2", post_norm_bwd, [_ri(sv["dn"], T), _bi(P["g_mlp_post"]), _ri(dx2, T)],
                                    [_ro(S, D_MODEL, BF16, T)], [D_MODEL], n_tiles=nt)
    d_up, = _mm_nt(f"l{l}_b_mm_down", d_dn, W["w_down"], [BF16], tm=512, tn=2048, extras=[sv["up"]],
                   epilogue=lambda acc, upv: (acc * (2.0 * jnp.maximum(upv, 0.0)),))
    G["w_down"] = _mm_tn(f"l{l}_g_down", sv["u"], d_dn, tm=2048, tn=D_MODEL, ts=512)
    d_h2, = _mm_nt(f"l{l}_b_mm_up", d_up, W["w_up"], [F32], tm=512, tn=D_MODEL)
    G["w_up"] = _mm_tn(f"l{l}_g_up", sv["h2"], d_up, tm=D_MODEL, tn=2048, ts=512)

    def pre_norm_bwd(v, g, dy, dres):
        dx, dg = _rms_bwd(v, g, dy)
        return dres + dx, dg

    dx1, G["g_mlp_pre"] = _rowwise(f"l{l}_b_pre2", pre_norm_bwd, [_ri(sv["x1"], T), _bi(P["g_mlp_pre"]), _ri(d_h2, T), _ri(dx2, T)],
                                  [_ro(S, D_MODEL, F32, T)], [D_MODEL], n_tiles=nt)
    d_y, G["g_mix_post"] = _rowwise(f"l{l}_b_post1", post_norm_bwd, [_ri(sv["y"], T), _bi(P["g_mix_post"]), _ri(dx1, T)],
                                   [_ro(S, D_MODEL, BF16, T)], [D_MODEL], n_tiles=nt)
    d_mixed, = _mm_nt(f"l{l}_b_mm_out", d_y, W["w_out"], [F32], tm=1024, tn=D_MODEL)
    G["w_out"] = _mm_tn(f"l{l}_g_out", sv["mixed"], d_y, tm=D_MODEL, tn=D_MODEL, ts=512)

    def gate_bwd(dm, gl, b, oa, ob, oc):
        gt = jax.nn.sigmoid(gl + b)
        outs, dgl = [], []
        for k, o in enumerate((oa, ob, oc)):
            gk = gt[:, k * D_MODEL:(k + 1) * D_MODEL]
            outs.append(dm * gk)
            dgl.append(dm * o * gk * (1.0 - gk))
        dgl = jnp.concatenate(dgl, axis=1)
        return (*outs, dgl, jnp.sum(dgl, axis=0, keepdims=True))

    d_oa, d_ob, d_oc, d_gl, G["b_gate"] = _rowwise(
        f"l{l}_b_gate", gate_bwd, [_ri(d_mixed, T), _ri(sv["proj"], T, 3072, 0), _bi(P["b_gate"]), _ri(sv["o_a"], T), _ri(sv["o_b"], T), _ri(sv["o_c"], T)],
        [_ro(S, D_MODEL, BF16, T)] * 3 + [_ro(S, 3 * D_MODEL, BF16, T)], [3 * D_MODEL], n_tiles=nt)
    d_att = {}
    for br, d_o, att in (("mla", d_oa, sv["att_a"]), ("swa", d_ob, sv["att_b"]), ("sb", d_oc, sv["att_c"])):
        d_att[br], = _mm_nt(f"l{l}_b_mm_o_{br}", d_o, W["w_o_" + br], [F32], tm=1024, tn=512)
        G["w_o_" + br] = _mm_tn(f"l{l}_g_o_{br}", att, d_o, tm=512, tn=D_MODEL, ts=512)

    riders = list(pending) + [(n, l, to_send(n, G[n])) for n in W_LATE]
    (dq_mla, dkt_mla, dvt_mla), rode = _softmax_bwd(f"l{l}_mla_bwd", sv["q_mla"], sv["k_mla"], sv["v_mla"], sv["att_a"], d_att["mla"], sv["lse_a"],
                                                  scale=MLA_SCALE, ride=([s for _, _, s in riders], True), **MLA_SPEC, **_mla_specs(S))
    rode = [(n, ly, r) for (n, ly, _), r in zip(riders, rode)]
    dk_mla = dkt_mla.transpose(0, 2, 1).reshape(S, -1)
    dv_mla = dvt_mla.transpose(0, 2, 1).reshape(S, -1)
    dq_swa, dk_swa, dv_swa, dsink = _swa_bwd(f"l{l}_swa_bwd", sv["q_swa"], sv["k_swa"], sv["v_swa"], sv["att_b"], d_att["swa"], sv["lse_b"],
                                             P["swa_sinks"], tq=SWA_TQ)
    G["swa_sinks"] = dsink[:, :SWA_G, 0].reshape(1, SWA_HEADS)
    dq_sb, dkt_sb, dvt_sb = _sb_bwd(f"l{l}_sb_bwd", sv["proj16"], sv["att_c32"], d_att["sb"], **SB_SPEC)
    dk_sb = dkt_sb.transpose(0, 2, 1).reshape(S, -1)
    dv_sb = dvt_sb.transpose(0, 2, 1).reshape(S, -1)

    def mla_prep_bwd(dq, dk, dvv, cq_t, sq_t, ck_t, sk_t):
        dks = dk[:, :LANES]
        for hh in range(1, MLA_HEADS):
            dks = dks + dk[:, hh * LANES:(hh + 1) * LANES]
        d_kr = _rope_t(pltpu.roll(dks, LANES - MLA_NOPE, axis=1), ck_t, sk_t, MLA_ROPE // 2)
        return _rope_t(dq, cq_t, sq_t, MLA_ROPE // 2), jnp.concatenate([dk, dvv], axis=1), d_kr

    d_q_lat, d_kv_lat, d_krope = _rowwise(
        f"l{l}_b_mla_prep", mla_prep_bwd,
        [_ri(dq_mla, T), _ri(dk_mla, T), _ri(dv_mla, T), _ri(tb["c_mla_q"], T), _ri(tb["s_mla_q"], T), _ri(tb["c_mla_k"], T), _ri(tb["s_mla_k"], T)],
        [_ro(S, MLA_QW, BF16, T), _ro(S, MLA_KVW, BF16, T), _ro(S, LANES, BF16, T)], n_tiles=nt)
    d_cqn, = _mm_nt(f"l{l}_b_mm_uq", d_q_lat, W["w_uq"], [F32], tm=1024, tn=MLA_Q_LORA)
    G["w_uq"] = _mm_tn(f"l{l}_g_uq", sv["cqn"], d_q_lat, tm=MLA_Q_LORA, tn=MLA_QW, ts=512)
    d_ckvn, = _mm_nt(f"l{l}_b_mm_ukv", d_kv_lat, W["w_ukv"], [F32], tm=1024, tn=MLA_KV_LORA)
    G["w_ukv"] = _mm_tn(f"l{l}_g_ukv", sv["ckvn"], d_kv_lat, tm=MLA_KV_LORA, tn=MLA_KVW, ts=512)

    def mix_prep_bwd(cq, ckv, gq, gkv, dcqn, dckvn, dqs, dks, dvs, cq_t, sq_t, ck_t, sk_t):
        d_cq, dgq = _rms_bwd(cq, gq, dcqn)
        d_ckv, dgkv = _rms_bwd(ckv, gkv, dckvn)
        dk2 = jnp.concatenate([dks[0], dks[1]], axis=1)
        dv2 = jnp.concatenate([dvs[0], dvs[1]], axis=1)
        return (d_cq, d_ckv, _rope_t(dqs, cq_t, sq_t, SWA_HEAD_DIM // 2), _rope_t(dk2, ck_t, sk_t, SWA_HEAD_DIM // 2), dv2, dgq, dgkv)

    kv3 = lambda a: (a.reshape(SWA_KV_HEADS, S, SWA_HEAD_DIM), (SWA_KV_HEADS, T, SWA_HEAD_DIM), lambda i: (0, i, 0))
    d_cq, d_ckv, d_qswa, d_kswa, d_vswa, G["g_q_lat"], G["g_kv_lat"] = _rowwise(
        f"l{l}_b_mix_prep", mix_prep_bwd,
        [_ri(sv["proj"], T, 256, _cb("c_q")), _ri(sv["proj"], T, 128, _cb("c_kv")), _bi(P["g_q_lat"]), _bi(P["g_kv_lat"]), _ri(d_cqn, T), _ri(d_ckvn, T),
         _ri(dq_swa, T), kv3(dk_swa), kv3(dv_swa), _ri(tb["c_swa_q"], T), _ri(tb["s_swa_q"], T), _ri(tb["c_swa_k"], T), _ri(tb["s_swa_k"], T)],
        [_ro(S, 256, BF16, T), _ro(S, 128, BF16, T), _ro(S, 512, BF16, T), _ro(S, 128, BF16, T), _ro(S, 128, BF16, T)], [256, 128], n_tiles=nt)
    pieces = dict(gates=d_gl, q_swa=d_qswa, q_sb=dq_sb, k_sb=dk_sb, v_sb=dv_sb, c_q=d_cq, c_kv=d_ckv, k_swa=d_kswa, v_swa=d_vswa, k_rope=d_krope)
    d_proj = jnp.concatenate([pieces[n].astype(BF16) for n in PERM_ORDER], axis=1)
    d_h, = _mm_nt(f"l{l}_b_mm_in", d_proj, W["w_in"], [F32], tm=512, tn=512)
    G["w_in"] = _mm_tn(f"l{l}_g_in", sv["h"], d_proj, tm=D_MODEL, tn=IN_WIDTH_P // 2, ts=512)
    dx, G["g_mix_pre"] = _rowwise(f"l{l}_b_pre1", pre_norm_bwd, [_ri(sv["x"], T), _bi(P["g_mix_pre"]), _ri(d_h, T), _ri(dx1, T)],
                                 [_ro(S, D_MODEL, F32, T)], [D_MODEL], n_tiles=nt)
    return dx, G, rode


def _pack_rows(vecs, rows):
    flat = jnp.concatenate([v.reshape(-1) for v in vecs])
    return jnp.pad(flat, (0, rows * PACK_COLS - flat.shape[0])).reshape(rows, PACK_COLS)


def kernel(x, positions, g_mix_pre, w_in, b_gate, g_q_lat, g_kv_lat, w_uq, w_ukv, swa_sinks, w_o_mla, w_o_swa, w_o_sb, w_out, g_mix_post, g_mlp_pre, w_up, w_down, g_mlp_post, loss_target, m_g_mix_pre, m_w_in, m_b_gate, m_g_q_lat, m_g_kv_lat, m_w_uq, m_w_ukv, m_swa_sinks, m_w_o_mla, m_w_o_swa, m_w_o_sb, m_w_out, m_g_mix_post, m_g_mlp_pre, m_w_up, m_w_down, m_g_mlp_post, v_g_mix_pre, v_w_in, v_b_gate, v_g_q_lat, v_g_kv_lat, v_w_uq, v_w_ukv, v_swa_sinks, v_w_o_mla, v_w_o_swa, v_w_o_sb, v_w_out, v_g_mix_post, v_g_mlp_pre, v_w_up, v_w_down, v_g_mlp_post):
    a = dict(locals())
    S = x.shape[1]
    depth = w_in.shape[0]
    T = min(256, S)
    xs = x.reshape(S, D_MODEL)
    tb = _tables(positions)

    perm = dict(w_in=_perm_w_in, w_uq=_perm_w_uq, w_ukv=_perm_w_ukv)

    def shards(l, names):
        return [a[n][l].astype(BF16) for n in names]

    def whole(names, gathered):
        return {n: perm.get(n, lambda t: t)(_from_shards(n, g, (1,) + a[n].shape[1:])[0]) for n, g in zip(names, gathered)}

    early = whole(W_EARLY, _exchange("gather_weights", shards(0, W_EARLY), per_peer=False))
    layers, saved = [], []
    h = xs
    for l in range(depth):
        P = {n: a[n][l].reshape(1, -1) for n in SMALL if n != "swa_sinks"}
        P["swa_sinks"] = a["swa_sinks"][l]
        srcs = shards(l, W_LATE) + (shards(l + 1, W_EARLY) if l + 1 < depth else [])
        h, sv, W, rode = _layer_fwd(l, h, early, P, tb, T, (srcs, False), lambda r: whole(W_LATE, r[:len(W_LATE)]))
        early = whole(W_EARLY, rode[len(W_LATE):])
        layers.append((W, P))
        saved.append(sv)

    def loss_head(yv, tv):
        err = yv - tv
        part = 0.5 * jnp.sum(jnp.mean(err * err, axis=1, keepdims=True), axis=0, keepdims=True)
        return err * (1.0 / D_MODEL), jnp.broadcast_to(part, (1, LANES))

    dh, loss_row = _rowwise("loss_head", loss_head, [_ri(h, T), _ri(loss_target.reshape(S, D_MODEL), T)], [_ro(S, D_MODEL, F32, T)], [LANES],
                            n_tiles=S // T)
    loss = lax.psum(loss_row[0, 0], ("x", "y", "c"))

    unperm = dict(w_in=_unperm_w_in, w_uq=_unperm_w_uq, w_ukv=_unperm_w_ukv)

    def to_send(n, g):
        return _to_shards(n, unperm.get(n, lambda t: t)(g).astype(BF16)[None])

    grads, recv, pending = [None] * depth, {n: [None] * depth for n in BIG}, []
    for l in reversed(range(depth)):
        W, P = layers[l]
        dh, grads[l], rode = _layer_bwd(l, dh, saved[l], W, P, tb, T, pending, to_send)
        for n, ly, r in rode:
            recv[n][ly] = r
        pending = [(n, l, to_send(n, grads[l][n])) for n in W_EARLY]
    for (n, ly, _), r in zip(pending, _exchange("scatter_grads", [s for _, _, s in pending], per_peer=True)):
        recv[n][ly] = r
    grad_x = dh.reshape(x.shape)

    out = {}
    for n in BIG:
        shp = a[n].shape
        rows, cols = shp[0] * shp[1], shp[2]
        parts = jnp.concatenate(recv[n], axis=1)
        res = _adamw("adamw_" + n, parts, a[n].reshape(rows, cols), a["m_" + n].reshape(rows, cols), a["v_" + n].reshape(rows, cols),
                     tile=min(256, rows))
        out[n] = [r.reshape(shp) for r in res]

    small_total = sum(a[n].size for n in SMALL)
    small_rows = -(-small_total // (8 * PACK_COLS)) * 8
    sg = _pack_rows([jnp.stack([grads[l][n].reshape(-1) for l in range(depth)]) for n in SMALL], small_rows)
    sg_all, = _exchange("gather_small_grads", [sg], per_peer=False)
    res = _adamw("adamw_small", sg_all, _pack_rows([a[n] for n in SMALL], small_rows), _pack_rows([a["m_" + n] for n in SMALL], small_rows),
                 _pack_rows([a["v_" + n] for n in SMALL], small_rows), tile=small_rows)
    off = 0
    for n in SMALL:
        cnt = a[n].size
        out[n] = [r.reshape(-1)[off:off + cnt].reshape(a[n].shape) for r in res]
        off += cnt

    order = ("g_mix_pre", "w_in", "b_gate", "g_q_lat", "g_kv_lat", "w_uq", "w_ukv", "swa_sinks", "w_o_mla", "w_o_swa", "w_o_sb", "w_out",
             "g_mix_post", "g_mlp_pre", "w_up", "w_down", "g_mlp_post")
    return (loss, grad_x, *[out[n][0] for n in order], *[out[n][1] for n in order], *[out[n][2] for n in order], *[out[n][3] for n in order])
```

```python
import functools

import jax
import jax.numpy as jnp
from jax import lax
from jax.experimental import pallas as pl
from jax.experimental.pallas import tpu as pltpu

F32, BF16 = jnp.float32, jnp.bfloat16

D_MODEL = 1024
DEPTH = 4
MLA_HEADS, MLA_Q_LORA, MLA_KV_LORA, MLA_NOPE, MLA_ROPE, MLA_V = 8, 256, 128, 64, 32, 64
SWA_HEADS, SWA_KV_HEADS, SWA_HEAD_DIM, SWA_WINDOW = 8, 2, 64, 128
SB_HEADS, SB_HEAD_DIM = 8, 64
D_FF = 4 * D_MODEL
ROPE_THETA = 10000.0
EPS = 1e-6
N_DEV = 8
ADAM_LR, ADAM_B1, ADAM_B2, ADAM_EPS, ADAM_WD, ADAM_STEP = 0.001, 0.9, 0.999, 1e-08, 0.01, 10

LANES = 128
VMEM_LIMIT_MAX = 60 * 1024 * 1024
VMEM_LIMIT_MIN = 32 * 1024 * 1024

ORIG_COLS = dict(c_q=(0, 256), c_kv=(256, 128), k_rope=(384, 32), q_swa=(416, 512), k_swa=(928, 128), v_swa=(1056, 128),
                 q_sb=(1184, 512), k_sb=(1696, 512), v_sb=(2208, 512), gates=(2720, 3072))
IN_WIDTH = 5792
PERM_ORDER = ("gates", "q_swa", "q_sb", "k_sb", "v_sb", "c_q", "c_kv", "k_swa", "v_swa", "k_rope")
PERM_WIDTH = dict(gates=3072, q_swa=512, q_sb=512, k_sb=512, v_sb=512, c_q=256, c_kv=128, k_swa=128, v_swa=128, k_rope=128)
PERM_OFF = {}
_o = 0
for _n in PERM_ORDER:
    PERM_OFF[_n] = _o
    _o += PERM_WIDTH[_n]
IN_WIDTH_P = _o
MLA_QW = MLA_HEADS * LANES
MLA_KVW = MLA_HEADS * LANES + MLA_HEADS * MLA_V

NT = (((1,), (1,)), ((), ()))
TN = (((0,), (0,)), ((), ()))
NEG = -1e30


def _cparams(sem, block_bytes):
    limit = int(min(VMEM_LIMIT_MAX, max(VMEM_LIMIT_MIN, 2 * block_bytes + (16 << 20))))
    return pltpu.CompilerParams(dimension_semantics=sem, vmem_limit_bytes=limit)


def _nbytes(shape, dtype):
    n = 1
    for s in shape:
        n *= s
    return n * jnp.dtype(dtype).itemsize


def _ri(arr, tile, width=None, cb=0):
    width = arr.shape[1] if width is None else width
    return (arr, (tile, width), lambda i, cb=cb: (i, cb))


def _bi(arr):
    return (arr, arr.shape, lambda i: (0, 0))


def _ro(rows, width, dtype, tile):
    return ((rows, width), dtype, (tile, width), lambda i: (i, 0))


def _rowwise(name, fn, ins, outs, reds=(), *, n_tiles):
    n_in, n_out = len(ins), len(outs)

    def body(*refs):
        vals = fn(*[r[...] for r in refs[:n_in]])
        for r, v in zip(refs[n_in:n_in + n_out], vals[:n_out]):
            if isinstance(v, (list, tuple)):
                for j, vj in enumerate(v):
                    r[j] = vj.astype(r.dtype)
            else:
                r[...] = v.astype(r.dtype)
        if reds:
            @pl.when(pl.program_id(0) == 0)
            def _():
                for r in refs[n_in + n_out:]:
                    r[...] = jnp.zeros_like(r)
            for r, v in zip(refs[n_in + n_out:], vals[n_out:]):
                r[...] += v

    block_bytes = sum(_nbytes(b, a.dtype) for a, b, _ in ins) + sum(_nbytes(b, d) for _, d, b, _ in outs)
    res = pl.pallas_call(
        body, name=name, grid=(n_tiles,),
        in_specs=[pl.BlockSpec(b, m) for _, b, m in ins],
        out_specs=[pl.BlockSpec(b, m) for _, _, b, m in outs] + [pl.BlockSpec((1, w), lambda i: (0, 0)) for w in reds],
        out_shape=[jax.ShapeDtypeStruct(s, d) for s, d, _, _ in outs] + [jax.ShapeDtypeStruct((1, w), F32) for w in reds],
        compiler_params=_cparams(("arbitrary",) if reds else ("parallel",), block_bytes),
    )(*[a for a, _, _ in ins])
    return res


def _rms(x, g):
    r = lax.rsqrt(jnp.mean(x * x, axis=1, keepdims=True) + EPS)
    return x * r * g


def _rms_bwd(x, g, dy):
    r = lax.rsqrt(jnp.mean(x * x, axis=1, keepdims=True) + EPS)
    xn = x * r
    dxn = dy * g
    dx = r * (dxn - xn * jnp.mean(dxn * xn, axis=1, keepdims=True))
    return dx, jnp.sum(dy * xn, axis=0, keepdims=True)


def _swap_halves(x, half):
    n = x.shape[1]
    lane = lax.broadcasted_iota(jnp.int32, x.shape, 1)
    first = (lane % (2 * half)) < half
    return jnp.where(first, pltpu.roll(x, n - half, axis=1), pltpu.roll(x, half, axis=1))


def _rope(x, c, sg, half):
    return x * c + _swap_halves(x, half) * sg


def _rope_t(dy, c, sg, half):
    return dy * c - _swap_halves(dy, half) * sg


def _mm_nn(name, a, b, outs, *, tm, tn, extras=(), epilogue=None):
    M, K = a.shape
    N = b.shape[1]
    tm = min(tm, M)
    n_e = len(extras)

    def body(*refs):
        a_ref, b_ref = refs[:2]
        acc = jnp.dot(a_ref[...].astype(BF16), b_ref[...].astype(BF16), preferred_element_type=F32)
        vals = (acc,) * len(outs) if epilogue is None else epilogue(acc, *[r[...] for r in refs[2:2 + n_e]])
        for r, v in zip(refs[2 + n_e:], vals):
            r[...] = v.astype(r.dtype)

    block_bytes = (_nbytes((tm, K), a.dtype) + _nbytes((K, tn), b.dtype) + sum(_nbytes((tm, tn), e.dtype) for e in extras)
                   + sum(_nbytes((tm, tn), d) for d in outs) + _nbytes((tm, tn), F32))
    return pl.pallas_call(
        body, name=name, grid=(N // tn, M // tm),
        in_specs=[pl.BlockSpec((tm, K), lambda j, i: (i, 0)), pl.BlockSpec((K, tn), lambda j, i: (0, j))]
        + [pl.BlockSpec((tm, tn), lambda j, i: (i, j)) for _ in extras],
        out_specs=[pl.BlockSpec((tm, tn), lambda j, i: (i, j)) for _ in outs],
        out_shape=[jax.ShapeDtypeStruct((M, N), d) for d in outs],
        compiler_params=_cparams(("parallel", "parallel"), block_bytes),
    )(a, b, *extras)


def _mm_nt(name, a, b, outs, *, tm, tn, extras=(), epilogue=None):
    M, N = a.shape
    K = b.shape[0]
    tm = min(tm, M)
    n_e = len(extras)

    def body(*refs):
        a_ref, b_ref = refs[:2]
        acc = lax.dot_general(a_ref[...].astype(BF16), b_ref[...].astype(BF16), NT, preferred_element_type=F32)
        vals = (acc,) * len(outs) if epilogue is None else epilogue(acc, *[r[...] for r in refs[2:2 + n_e]])
        for r, v in zip(refs[2 + n_e:], vals):
            r[...] = v.astype(r.dtype)

    block_bytes = (_nbytes((tm, N), a.dtype) + _nbytes((tn, N), b.dtype) + sum(_nbytes((tm, tn), e.dtype) for e in extras)
                   + sum(_nbytes((tm, tn), d) for d in outs) + _nbytes((tm, tn), F32))
    return pl.pallas_call(
        body, name=name, grid=(K // tn, M // tm),
        in_specs=[pl.BlockSpec((tm, N), lambda j, i: (i, 0)), pl.BlockSpec((tn, N), lambda j, i: (j, 0))]
        + [pl.BlockSpec((tm, tn), lambda j, i: (i, j)) for _ in extras],
        out_specs=[pl.BlockSpec((tm, tn), lambda j, i: (i, j)) for _ in outs],
        out_shape=[jax.ShapeDtypeStruct((M, K), d) for d in outs],
        compiler_params=_cparams(("parallel", "parallel"), block_bytes),
    )(a, b, *extras)


def _mm_tn(name, a, b, *, tm, tn, ts):
    S, K = a.shape
    N = b.shape[1]
    ts = min(ts, S)

    def body(a_ref, b_ref, o_ref):
        @pl.when(pl.program_id(2) == 0)
        def _():
            o_ref[...] = jnp.zeros_like(o_ref)
        o_ref[...] += lax.dot_general(a_ref[...].astype(BF16), b_ref[...].astype(BF16), TN, preferred_element_type=F32)

    block_bytes = _nbytes((ts, tm), a.dtype) + _nbytes((ts, tn), b.dtype) + 2 * _nbytes((tm, tn), F32)
    return pl.pallas_call(
        body, name=name, grid=(K // tm, N // tn, S // ts),
        in_specs=[pl.BlockSpec((ts, tm), lambda i, j, s: (s, i)), pl.BlockSpec((ts, tn), lambda i, j, s: (s, j))],
        out_specs=pl.BlockSpec((tm, tn), lambda i, j, s: (i, j)),
        out_shape=jax.ShapeDtypeStruct((K, N), F32),
        compiler_params=_cparams(("parallel", "parallel", "arbitrary"), block_bytes),
    )(a, b)


def _lane_pack(cols, rows):
    lane = lax.broadcasted_iota(jnp.int32, (rows, LANES), 1)
    val = jnp.zeros((rows, LANES), F32)
    for h, c in enumerate(cols):
        val = jnp.where(lane == h, c, val)
    return val


def _mask(kb, row, tk):
    return kb * tk + lax.broadcasted_iota(jnp.int32, (1, tk), 1) <= row


def _kb_range(i, tq, tk):
    return (i * tq) // tk, ((i + 1) * tq + tk - 1) // tk


def _softmax_fwd(name, q, k, v, *, groups, heads, q_spec, k_spec, v_spec, o_width, tq, tk, ride=None):
    S = q.shape[0]
    nq = S // tq
    nh = len(heads)
    dv = heads[0][2].stop - heads[0][2].start

    def body(q_ref, k_ref, v_ref, o_ref, lse_ref):
        i = pl.program_id(1)
        q_all = q_ref[...]
        row = i * tq + lax.broadcasted_iota(jnp.int32, (tq, 1), 0)
        mid, hi = _kb_range(i, tq, tk)
        qhs = [q_all[:, qs] for qs, _, _ in heads]

        def blk(kb, carry, masked):
            r0 = pl.multiple_of(kb * tk, tk)
            k_all = k_ref[pl.ds(r0, tk), :]
            v_all = v_ref[pl.ds(r0, tk), :]
            if masked:
                ok = _mask(kb, row, tk)
            ones = jnp.ones((tk, dv), BF16)
            new = []
            for qh, (_, ks, vs), (m, acc) in zip(qhs, heads, carry):
                s = lax.dot_general(qh, k_all[:, ks], NT, preferred_element_type=F32)
                if masked:
                    s = jnp.where(ok, s, NEG)
                m_new = jnp.maximum(m, jnp.max(s, axis=1, keepdims=True))
                p = jnp.exp(s - m_new)
                if masked:
                    p = jnp.where(ok, p, 0.0)
                pv = jnp.dot(p.astype(BF16), jnp.concatenate([v_all[:, vs], ones], axis=1), preferred_element_type=F32)
                new.append((m_new, jnp.exp(m - m_new) * acc + pv))
            return tuple(new)

        carry = tuple((jnp.full((tq, 1), NEG, F32), jnp.zeros((tq, 2 * dv), F32)) for _ in range(nh))
        carry = lax.fori_loop(0, mid, functools.partial(blk, masked=False), carry)
        carry = lax.fori_loop(mid, hi, functools.partial(blk, masked=True), carry)
        o_ref[...] = jnp.concatenate([acc[:, :dv] / acc[:, dv:] for _, acc in carry], axis=1).astype(o_ref.dtype)
        lse_ref[0] = _lane_pack([m + jnp.log(acc[:, dv:dv + 1]) for m, acc in carry], tq)

    wo = nh * dv
    block_bytes = _nbytes(q_spec[0], q.dtype) + _nbytes(k_spec[0], k.dtype) + _nbytes(v_spec[0], v.dtype) + 4 * tq * (wo + LANES)
    return _pallas(
        body, name=name, grid=(groups, nq), in_specs=[pl.BlockSpec(*q_spec), pl.BlockSpec(*k_spec), pl.BlockSpec(*v_spec)],
        out_specs=[pl.BlockSpec((tq, wo), lambda g, i: (i, g)), pl.BlockSpec((1, tq, LANES), lambda g, i: (g, i, 0))],
        out_shape=[jax.ShapeDtypeStruct((S, o_width), BF16), jax.ShapeDtypeStruct((groups, S, LANES), F32)],
        args=[q, k, v], sem=("parallel", "arbitrary"), block_bytes=block_bytes, ride=ride)


def _softmax_bwd(name, q, k, v, o, do, lse, *, groups, heads, q_spec, k_spec, v_spec, tq, tk, scale, ride=None):
    S = q.shape[0]
    nq, nkb = S // tq, S // tk
    nh = len(heads)
    dv = heads[0][2].stop - heads[0][2].start
    wo = nh * dv
    wk, wv = k_spec[0][1], v_spec[0][1]

    def body(q_ref, k_ref, v_ref, o_ref, do_ref, lse_ref, dq_ref, dkt_ref, dvt_ref):
        i = pl.program_id(1)

        @pl.when(i == 0)
        def _():
            dkt_ref[...] = jnp.zeros_like(dkt_ref)
            dvt_ref[...] = jnp.zeros_like(dvt_ref)

        q_all = q_ref[...]
        o_all = o_ref[...].astype(F32)
        do_all = do_ref[...].astype(F32)
        lse_all = lse_ref[0]
        row = i * tq + lax.broadcasted_iota(jnp.int32, (tq, 1), 0)
        mid, hi = _kb_range(i, tq, tk)
        per_head = []
        for h, (qs, ks, vs) in enumerate(heads):
            osl = slice(h * dv, (h + 1) * dv)
            doh = do_all[:, osl]
            delta = jnp.sum(doh * o_all[:, osl], axis=1, keepdims=True)
            qh = q_all[:, qs]
            per_head.append((qh, qh.astype(F32).T.astype(BF16), doh.astype(BF16), doh.T.astype(BF16), delta, lse_all[:, h:h + 1], ks, vs))

        def blk(kb, dqs, masked):
            r0 = pl.multiple_of(kb * tk, tk)
            k_all = k_ref[pl.ds(r0, tk), :]
            v_all = v_ref[pl.ds(r0, tk), :]
            if masked:
                ok = _mask(kb, row, tk)
            dkt, dvt, new_dqs = [], [], []
            for (qh, qt, doh, dot_, delta, lse_h, ks, vs), dq in zip(per_head, dqs):
                kk, vv = k_all[:, ks], v_all[:, vs]
                s = lax.dot_general(qh, kk, NT, preferred_element_type=F32)
                p = jnp.exp(s - lse_h)
                if masked:
                    p = jnp.where(ok, p, 0.0)
                dp = lax.dot_general(doh, vv, NT, preferred_element_type=F32)
                ds = (p * (dp - delta)).astype(BF16)
                new_dqs.append(dq + jnp.dot(ds, kk, preferred_element_type=F32))
                dkt.append(jnp.dot(qt, ds, preferred_element_type=F32))
                dvt.append(jnp.dot(dot_, p.astype(BF16), preferred_element_type=F32))
            dkt_ref[kb] += jnp.concatenate(dkt, axis=0)
            dvt_ref[kb] += jnp.concatenate(dvt, axis=0)
            return tuple(new_dqs)

        dqs = tuple(jnp.zeros((tq, qs.stop - qs.start), F32) for qs, _, _ in heads)
        dqs = lax.fori_loop(0, mid, functools.partial(blk, masked=False), dqs)
        dqs = lax.fori_loop(mid, hi, functools.partial(blk, masked=True), dqs)
        dq_ref[...] = jnp.concatenate([dq * scale for dq in dqs], axis=1)

    in_specs = [pl.BlockSpec(*q_spec), pl.BlockSpec(*k_spec), pl.BlockSpec(*v_spec),
                pl.BlockSpec((tq, wo), lambda g, i: (i, g)), pl.BlockSpec((tq, wo), lambda g, i: (i, g)),
                pl.BlockSpec((1, tq, LANES), lambda g, i: (g, i, 0))]
    args = [q, k, v, o, do, lse]
    out_specs = [pl.BlockSpec(*q_spec), pl.BlockSpec((nkb, wk, tk), lambda g, i: (0, g, 0)), pl.BlockSpec((nkb, wv, tk), lambda g, i: (0, g, 0))]
    out_shape = [jax.ShapeDtypeStruct(q.shape, F32), jax.ShapeDtypeStruct((nkb, k.shape[1], tk), F32),
                 jax.ShapeDtypeStruct((nkb, v.shape[1], tk), F32)]
    block_bytes = (_nbytes(q_spec[0], q.dtype) + _nbytes(k_spec[0], k.dtype) + _nbytes(v_spec[0], v.dtype) + 6 * tq * wo + 4 * tq * LANES
                   + _nbytes(q_spec[0], F32) + _nbytes(k_spec[0], F32) + _nbytes(v_spec[0], F32))
    return _pallas(body, name=name, grid=(groups, nq), in_specs=in_specs, out_specs=out_specs, out_shape=out_shape, args=args,
                   sem=("parallel", "arbitrary"), block_bytes=block_bytes, ride=ride)


def _swa_window(i, tq, row):
    start = pl.multiple_of(jnp.maximum(i * tq - SWA_WINDOW, 0), SWA_WINDOW)
    col = start + lax.broadcasted_iota(jnp.int32, (1, tq + SWA_WINDOW), 1)
    return start, (col <= row) & ((row - col) < SWA_WINDOW)


def _swa_fwd(name, q, k, v, sinks, *, tq):
    S = q.shape[0]
    hd, span = SWA_HEAD_DIM, tq + SWA_WINDOW
    scale = hd ** -0.5

    def body(sink_ref, q_ref, k_ref, v_ref, o_ref, lse_ref):
        g, i = pl.program_id(0), pl.program_id(1)
        row = i * tq + lax.broadcasted_iota(jnp.int32, (tq, 1), 0)
        start, ok = _swa_window(i, tq, row)
        kk, vv = k_ref[pl.ds(start, span), :], v_ref[pl.ds(start, span), :]
        q_all = q_ref[...]
        outs, lses = [], []
        for h in range(SWA_G):
            s = lax.dot_general(q_all[:, h * hd:(h + 1) * hd] * scale, kk, NT, preferred_element_type=F32)
            s = jnp.where(ok, s, NEG)
            sink = sink_ref[g * SWA_G + h]
            m = jnp.maximum(jnp.max(s, axis=1, keepdims=True), sink)
            p = jnp.exp(s - m)
            l = jnp.sum(p, axis=1, keepdims=True) + jnp.exp(sink - m)
            outs.append(jnp.dot(p.astype(BF16), vv, preferred_element_type=F32) / l)
            lses.append(m + jnp.log(l))
        o_ref[...] = jnp.concatenate(outs, axis=1).astype(o_ref.dtype)
        lse_ref[0] = _lane_pack(lses, tq)

    wq = SWA_G * hd
    block_bytes = 2 * tq * wq * 2 + 2 * 2 * S * hd + 4 * tq * LANES
    return pl.pallas_call(
        body, name=name, grid=(SWA_KV_HEADS, S // tq),
        in_specs=[pl.BlockSpec(memory_space=pltpu.SMEM), pl.BlockSpec((tq, wq), lambda g, i: (i, g)),
                  pl.BlockSpec((S, hd), lambda g, i: (g, 0)), pl.BlockSpec((S, hd), lambda g, i: (g, 0))],
        out_specs=[pl.BlockSpec((tq, wq), lambda g, i: (i, g)), pl.BlockSpec((1, tq, LANES), lambda g, i: (g, i, 0))],
        out_shape=[jax.ShapeDtypeStruct((S, SWA_HEADS * hd), BF16), jax.ShapeDtypeStruct((SWA_KV_HEADS, S, LANES), F32)],
        compiler_params=_cparams(("parallel", "parallel"), block_bytes),
    )(sinks, q, k, v)


def _swa_bwd(name, q, k, v, o, do, lse, sinks, *, tq):
    S = q.shape[0]
    hd, span = SWA_HEAD_DIM, tq + SWA_WINDOW
    scale = hd ** -0.5

    def body(sink_ref, q_ref, k_ref, v_ref, o_ref, do_ref, lse_ref, dq_ref, dk_ref, dv_ref, dsink_ref):
        g, i = pl.program_id(0), pl.program_id(1)

        @pl.when(i == 0)
        def _():
            dk_ref[...] = jnp.zeros_like(dk_ref)
            dv_ref[...] = jnp.zeros_like(dv_ref)
            dsink_ref[...] = jnp.zeros_like(dsink_ref)

        row = i * tq + lax.broadcasted_iota(jnp.int32, (tq, 1), 0)
        start, ok = _swa_window(i, tq, row)
        kk, vv = k_ref[pl.ds(start, span), :], v_ref[pl.ds(start, span), :]
        q_all, o_all, do_all, lse_all = q_ref[...], o_ref[...].astype(F32), do_ref[...].astype(F32), lse_ref[0]
        dqs, dk_c, dv_c = [], None, None
        for h in range(SWA_G):
            sl = slice(h * hd, (h + 1) * hd)
            qh = q_all[:, sl] * scale
            doh = do_all[:, sl]
            delta = jnp.sum(doh * o_all[:, sl], axis=1, keepdims=True)
            lse_h = lse_all[:, h:h + 1]
            doh = doh.astype(BF16)
            s = jnp.where(ok, lax.dot_general(qh, kk, NT, preferred_element_type=F32), NEG)
            p = jnp.exp(s - lse_h)
            ds = p * (lax.dot_general(doh, vv, NT, preferred_element_type=F32) - delta)
            dqs.append(jnp.dot(ds.astype(BF16), kk, preferred_element_type=F32) * scale)
            dk_h = jnp.dot(ds.T.astype(BF16), qh, preferred_element_type=F32)
            dv_h = jnp.dot(p.T.astype(BF16), doh, preferred_element_type=F32)
            dk_c = dk_h if dk_c is None else dk_c + dk_h
            dv_c = dv_h if dv_c is None else dv_c + dv_h
            p_sink = jnp.exp(sink_ref[g * SWA_G + h] - lse_h)
            dsink_ref[0, h:h + 1, :] += jnp.broadcast_to(-jnp.sum(p_sink * delta, axis=0, keepdims=True), (1, LANES))
        dq_ref[...] = jnp.concatenate(dqs, axis=1)
        dk_ref[pl.ds(start, span), :] += dk_c
        dv_ref[pl.ds(start, span), :] += dv_c

    wq = SWA_G * hd
    qb = pl.BlockSpec((tq, wq), lambda g, i: (i, g))
    kb = pl.BlockSpec((S, hd), lambda g, i: (g, 0))
    block_bytes = 2 * tq * wq * (2 + 2 + 4 + 4) + 2 * S * hd * (2 + 2 + 4 + 4) + 4 * tq * LANES
    return pl.pallas_call(
        body, name=name, grid=(SWA_KV_HEADS, S // tq),
        in_specs=[pl.BlockSpec(memory_space=pltpu.SMEM), qb, kb, kb, qb, qb, pl.BlockSpec((1, tq, LANES), lambda g, i: (g, i, 0))],
        out_specs=[qb, kb, kb, pl.BlockSpec((1, 8, LANES), lambda g, i: (g, 0, 0))],
        out_shape=[jax.ShapeDtypeStruct(q.shape, F32), jax.ShapeDtypeStruct(k.shape, F32), jax.ShapeDtypeStruct(v.shape, F32),
                   jax.ShapeDtypeStruct((SWA_KV_HEADS, 8, LANES), F32)],
        compiler_params=_cparams(("parallel", "arbitrary"), block_bytes),
    )(sinks, q, k, v, o, do, lse)


def _split_dot(x, u2):
    hi = x.astype(BF16)
    lo = (x - hi.astype(F32)).astype(BF16)
    return jnp.dot(jnp.concatenate([hi, lo], axis=1), u2, preferred_element_type=F32)


def _suffix_ones():
    r = lax.broadcasted_iota(jnp.int32, (2 * LANES, 2 * LANES), 0) % LANES
    c = lax.broadcasted_iota(jnp.int32, (2 * LANES, 2 * LANES), 1)
    return ((r > c) | (c >= LANES)).astype(BF16)


def _suffix_scan(x, uo, run):
    nc = x.shape[1] // LANES
    out = [None] * nc
    for c in reversed(range(nc)):
        st = _split_dot(x[:, c * LANES:(c + 1) * LANES], uo)
        out[c] = st[:, :LANES] + run
        run = run + st[:, LANES:]
    return (out[0] if nc == 1 else jnp.concatenate(out, axis=1)), run


SB_DEAD_LOG = -110.0


def _sb_logits(qh, kk, r0, row, masked):
    z = lax.dot_general(qh, kk, NT, preferred_element_type=F32)
    nz = -z
    l = jnp.minimum(nz, 0.0) - jnp.log(1.0 + jnp.exp(jnp.minimum(z, nz)))
    ok = None
    if masked:
        col = r0 + lax.broadcasted_iota(jnp.int32, (1, kk.shape[0]), 1)
        ok = col < row
        l = jnp.where(ok, l, 0.0)
    return z, l, ok


def _sb_walk(blk, carry, i, tq, ts):
    carry = blk(pl.multiple_of(i * tq, tq), tq, carry, True)

    def live(c):
        t, heads = c
        top = jnp.max(heads[0][0])
        for h in heads[1:]:
            top = jnp.maximum(top, jnp.max(h[0]))
        return jnp.logical_and(t < i * (tq // ts), top > SB_DEAD_LOG)

    def step(c):
        t, heads = c
        return t + 1, blk(pl.multiple_of(i * tq - (t + 1) * ts, ts), ts, heads, False)

    return lax.while_loop(live, step, (jnp.int32(0), carry))[1]


def _sb_fwd(name, pb, *, q_cb, k_cb, v_cb, tq, ts, scale):
    S = pb.shape[0]
    nq = S // tq
    hd = SB_HEAD_DIM
    groups = SB_HEADS * hd // LANES
    nh = LANES // hd

    def body(q_ref, k_ref, v_ref, o16_ref, o32_ref):
        i = pl.program_id(1)
        q_all = q_ref[...]
        row = i * tq + lax.broadcasted_iota(jnp.int32, (tq, 1), 0)
        uo = _suffix_ones()
        sls = [slice(h * hd, (h + 1) * hd) for h in range(nh)]
        qhs = [q_all[:, sl] * scale for sl in sls]

        def blk(r0, width, carry, masked):
            k_all = k_ref[pl.ds(r0, width), :]
            v_all = v_ref[pl.ds(r0, width), :]
            new = []
            for qh, sl, (run_l, acc) in zip(qhs, sls, carry):
                z, l, ok = _sb_logits(qh, k_all[:, sl], r0, row, masked)
                tail, run_l = _suffix_scan(l, uo, run_l)
                e = z + l + tail
                if masked:
                    e = jnp.where(ok, e, NEG)
                new.append((run_l, acc + jnp.dot(jnp.exp(e).astype(BF16), v_all[:, sl], preferred_element_type=F32)))
            return tuple(new)

        carry = tuple((jnp.zeros((tq, LANES), F32), jnp.zeros((tq, hd), F32)) for _ in range(nh))
        carry = _sb_walk(blk, carry, i, tq, ts)
        o = jnp.concatenate([acc for _, acc in carry], axis=1)
        o16_ref[...] = o.astype(BF16)
        o32_ref[...] = o

    block_bytes = 2 * tq * LANES + 2 * 2 * S * LANES + 6 * tq * LANES
    return pl.pallas_call(
        body, name=name, grid=(groups, nq),
        in_specs=[pl.BlockSpec((tq, LANES), lambda g, i: (i, q_cb + g)), pl.BlockSpec((S, LANES), lambda g, i: (0, k_cb + g)),
                  pl.BlockSpec((S, LANES), lambda g, i: (0, v_cb + g))],
        out_specs=[pl.BlockSpec((tq, LANES), lambda g, i: (i, g)), pl.BlockSpec((tq, LANES), lambda g, i: (i, g))],
        out_shape=[jax.ShapeDtypeStruct((S, groups * LANES), BF16), jax.ShapeDtypeStruct((S, groups * LANES), F32)],
        compiler_params=_cparams(("parallel", "arbitrary"), block_bytes),
    )(pb, pb, pb)


def _sb_bwd(name, pb, o32, do, *, q_cb, k_cb, v_cb, tq, ts, scale):
    S = pb.shape[0]
    nq = S // tq
    hd = SB_HEAD_DIM
    groups = SB_HEADS * hd // LANES
    nh = LANES // hd

    def body(q_ref, k_ref, v_ref, o_ref, do_ref, dq_ref, dkt_ref, dvt_ref):
        i = pl.program_id(1)

        @pl.when(i == 0)
        def _():
            dkt_ref[...] = jnp.zeros_like(dkt_ref)
            dvt_ref[...] = jnp.zeros_like(dvt_ref)

        q_all = q_ref[...]
        o_all = o_ref[...]
        do_all = do_ref[...].astype(F32)
        row = i * tq + lax.broadcasted_iota(jnp.int32, (tq, 1), 0)
        uo = _suffix_ones()
        per_head = []
        for h in range(nh):
            sl = slice(h * hd, (h + 1) * hd)
            doh = do_all[:, sl].astype(BF16)
            total = jnp.sum(doh.astype(F32) * o_all[:, sl], axis=1, keepdims=True)
            qh = q_all[:, sl] * scale
            per_head.append((qh, qh.astype(F32).T.astype(BF16), doh, doh.astype(F32).T.astype(BF16), jnp.broadcast_to(total, (tq, LANES)), sl))

        def blk(r0, width, carry, masked):
            k_all = k_ref[pl.ds(r0, width), :]
            v_all = v_ref[pl.ds(r0, width), :]
            new, dk_c, dv_c = [], [], []
            for (qh, qt, doh, dot_, total, sl), (run_l, run_g, dq) in zip(per_head, carry):
                kk, vv = k_all[:, sl], v_all[:, sl]
                z, l, ok = _sb_logits(qh, kk, r0, row, masked)
                tail, run_l = _suffix_scan(l, uo, run_l)
                e = z + l
                beta = jnp.exp(e)
                e = e + tail
                if masked:
                    e = jnp.where(ok, e, NEG)
                a = jnp.exp(e).astype(BF16).astype(F32)
                gr = lax.dot_general(doh, vv, NT, preferred_element_type=F32) * a
                right, run_g = _suffix_scan(gr, uo, run_g)
                nc = width // LANES
                prefix = (total if nc == 1 else jnp.tile(total, (1, nc))) - right
                dz = gr - beta * prefix
                if masked:
                    dz = jnp.where(ok, dz, 0.0)
                dzb = dz.astype(BF16)
                new.append((run_l, run_g, dq + jnp.dot(dzb, kk, preferred_element_type=F32)))
                dk_c.append(jnp.dot(qt, dzb, preferred_element_type=F32))
                dv_c.append(jnp.dot(dot_, a.astype(BF16), preferred_element_type=F32))
            dkt, dvt = jnp.concatenate(dk_c, axis=0), jnp.concatenate(dv_c, axis=0)
            for j in range(width // ts):
                dkt_ref[r0 // ts + j] += dkt[:, j * ts:(j + 1) * ts]
                dvt_ref[r0 // ts + j] += dvt[:, j * ts:(j + 1) * ts]
            return tuple(new)

        zc = jnp.zeros((tq, LANES), F32)
        carry = tuple((zc, zc, jnp.zeros((tq, hd), F32)) for _ in range(nh))
        carry = _sb_walk(blk, carry, i, tq, ts)
        dq_ref[...] = jnp.concatenate([c[2] * scale for c in carry], axis=1)

    W = groups * LANES
    block_bytes = 2 * tq * LANES + 2 * 2 * S * LANES + 3 * 4 * tq * LANES + 2 * 4 * S * LANES
    return pl.pallas_call(
        body, name=name, grid=(groups, nq),
        in_specs=[pl.BlockSpec((tq, LANES), lambda g, i: (i, q_cb + g)), pl.BlockSpec((S, LANES), lambda g, i: (0, k_cb + g)),
                  pl.BlockSpec((S, LANES), lambda g, i: (0, v_cb + g)), pl.BlockSpec((tq, LANES), lambda g, i: (i, g)),
                  pl.BlockSpec((tq, LANES), lambda g, i: (i, g))],
        out_specs=[pl.BlockSpec((tq, LANES), lambda g, i: (i, g)), pl.BlockSpec((S // ts, LANES, ts), lambda g, i: (0, g, 0)),
                   pl.BlockSpec((S // ts, LANES, ts), lambda g, i: (0, g, 0))],
        out_shape=[jax.ShapeDtypeStruct((S, W), F32)] + [jax.ShapeDtypeStruct((S // ts, W, ts), F32)] * 2,
        compiler_params=_cparams(("parallel", "arbitrary"), block_bytes),
    )(pb, pb, pb, o32, do)


def _exchange_copies(src_refs, out_refs, send_sems, recv_sems, local_sems, per_peer):
    x, y, c = lax.axis_index("x"), lax.axis_index("y"), lax.axis_index("c")
    me = 4 * x + 2 * y + c
    n = len(src_refs)

    def copy(j, k):
        px, py, pc = x ^ (k >> 2), y ^ ((k >> 1) & 1), c ^ (k & 1)
        s = src_refs[j].at[4 * px + 2 * py + pc] if per_peer else src_refs[j]
        return pltpu.make_async_remote_copy(
            src_ref=s, dst_ref=out_refs[j].at[me], send_sem=send_sems.at[j, k - 1], recv_sem=recv_sems.at[j, k - 1],
            device_id=(px, py, pc), device_id_type=pl.DeviceIdType.MESH)

    mine = [pltpu.make_async_copy(src_refs[j].at[me] if per_peer else src_refs[j], out_refs[j].at[me], local_sems.at[j]) for j in range(n)]
    return mine, [copy(j, k) for k in range(1, N_DEV) for j in range(n)]


def _exchange_start(*refs, per_peer):
    mine, copies = _exchange_copies(*refs, per_peer)
    for cp in mine + copies:
        cp.start()


def _exchange_wait(*refs, per_peer):
    mine, copies = _exchange_copies(*refs, per_peer)
    for cp in copies:
        cp.wait_recv()
    for cp in copies:
        cp.wait_send()
    for cp in mine:
        cp.wait()


def _exchange_specs(srcs, per_peer):
    n = len(srcs)
    out_shape = [jax.ShapeDtypeStruct((N_DEV,) + tuple(s.shape[1:] if per_peer else s.shape), s.dtype) for s in srcs]
    sems = [pltpu.SemaphoreType.DMA((n, N_DEV - 1)), pltpu.SemaphoreType.DMA((n, N_DEV - 1)), pltpu.SemaphoreType.DMA((n,))]
    return [pl.BlockSpec(memory_space=pltpu.HBM)] * n, out_shape, sems


def _exchange(name, srcs, per_peer):
    n = len(srcs)

    def body(*refs):
        parts = (refs[:n], refs[n:2 * n], *refs[2 * n:])
        _exchange_start(*parts, per_peer=per_peer)
        _exchange_wait(*parts, per_peer=per_peer)

    hbm, out_shape, sems = _exchange_specs(srcs, per_peer)
    return pl.pallas_call(body, name=name, in_specs=hbm, out_specs=hbm, out_shape=out_shape, scratch_shapes=sems)(*srcs)


def _pallas(body, *, name, grid, in_specs, out_specs, out_shape, args, sem, block_bytes, ride=None):
    if ride is None:
        return pl.pallas_call(body, name=name, grid=grid, in_specs=in_specs, out_specs=out_specs, out_shape=out_shape,
                              compiler_params=_cparams(sem, block_bytes))(*args), None
    srcs, per_peer = ride
    n, n_in, n_out = len(srcs), len(in_specs), len(out_specs)

    def riding(*refs):
        ins, xsrc = refs[:n_in], refs[n_in:n_in + n]
        outs, xout = refs[n_in + n:n_in + n + n_out], refs[n_in + n + n_out:n_in + 2 * n + n_out]
        parts = (xsrc, xout, *refs[n_in + 2 * n + n_out:])
        ids = [pl.program_id(d) for d in range(len(grid))]
        first = functools.reduce(jnp.logical_and, [i == 0 for i in ids])
        last = functools.reduce(jnp.logical_and, [i == g - 1 for i, g in zip(ids, grid)])
        pl.when(first)(functools.partial(_exchange_start, *parts, per_peer=per_peer))
        body(*ins, *outs)
        pl.when(last)(functools.partial(_exchange_wait, *parts, per_peer=per_peer))

    hbm, x_shape, sems = _exchange_specs(srcs, per_peer)
    res = pl.pallas_call(riding, name=name, grid=grid, in_specs=list(in_specs) + hbm, out_specs=list(out_specs) + hbm,
                         out_shape=list(out_shape) + x_shape, scratch_shapes=sems,
                         compiler_params=_cparams(("arbitrary",) * len(grid), block_bytes))(*args, *srcs)
    return res[:n_out], res[n_out:]


def _adamw(name, parts, w, m, v, *, tile):
    R, C = w.shape

    def body(p_ref, w_ref, m_ref, v_ref, g_ref, d_ref, nm_ref, nv_ref):
        g = p_ref[0].astype(F32)
        for d in range(1, N_DEV):
            g = g + p_ref[d].astype(F32)
        wv = w_ref[...]
        mm = ADAM_B1 * m_ref[...] + (1.0 - ADAM_B1) * g
        vv = ADAM_B2 * v_ref[...] + (1.0 - ADAM_B2) * jnp.square(g)
        m_hat = mm / (1.0 - ADAM_B1 ** ADAM_STEP)
        v_hat = vv / (1.0 - ADAM_B2 ** ADAM_STEP)
        g_ref[...] = g
        d_ref[...] = -ADAM_LR * (m_hat / (jnp.sqrt(v_hat) + ADAM_EPS) + ADAM_WD * wv)
        nm_ref[...] = mm
        nv_ref[...] = vv

    blk = pl.BlockSpec((tile, C), lambda i: (i, 0))
    block_bytes = N_DEV * _nbytes((tile, C), parts.dtype) + 7 * _nbytes((tile, C), F32)
    return pl.pallas_call(
        body, name=name, grid=(R // tile,),
        in_specs=[pl.BlockSpec((N_DEV, tile, C), lambda i: (0, i, 0)), blk, blk, blk],
        out_specs=[blk] * 4, out_shape=[jax.ShapeDtypeStruct((R, C), F32)] * 4,
        compiler_params=_cparams(("parallel",), block_bytes),
    )(parts, w, m, v)


def _perm_w_in(w):
    cols = [w[:, ORIG_COLS[n][0]:ORIG_COLS[n][0] + ORIG_COLS[n][1]] for n in PERM_ORDER]
    cols.append(jnp.zeros((w.shape[0], IN_WIDTH_P - IN_WIDTH), w.dtype))
    return jnp.concatenate(cols, axis=1)


def _unperm_w_in(wp):
    order = sorted(ORIG_COLS, key=lambda n: ORIG_COLS[n][0])
    return jnp.concatenate([wp[:, PERM_OFF[n]:PERM_OFF[n] + ORIG_COLS[n][1]] for n in order], axis=1)


def _perm_w_uq(w):
    w3 = w.reshape(w.shape[0], MLA_HEADS, MLA_NOPE + MLA_ROPE)
    return jnp.pad(w3, ((0, 0), (0, 0), (0, LANES - MLA_NOPE - MLA_ROPE))).reshape(w.shape[0], MLA_QW)


def _unperm_w_uq(wp):
    return wp.reshape(wp.shape[0], MLA_HEADS, LANES)[:, :, :MLA_NOPE + MLA_ROPE].reshape(wp.shape[0], -1)


def _perm_w_ukv(w):
    w3 = w.reshape(w.shape[0], MLA_HEADS, MLA_NOPE + MLA_V)
    kp = jnp.pad(w3[:, :, :MLA_NOPE], ((0, 0), (0, 0), (0, LANES - MLA_NOPE))).reshape(w.shape[0], MLA_HEADS * LANES)
    return jnp.concatenate([kp, w3[:, :, MLA_NOPE:].reshape(w.shape[0], MLA_HEADS * MLA_V)], axis=1)


def _unperm_w_ukv(wp):
    n = wp.shape[0]
    kp = wp[:, :MLA_HEADS * LANES].reshape(n, MLA_HEADS, LANES)[:, :, :MLA_NOPE]
    vp = wp[:, MLA_HEADS * LANES:].reshape(n, MLA_HEADS, MLA_V)
    return jnp.concatenate([kp, vp], axis=2).reshape(n, MLA_HEADS * (MLA_NOPE + MLA_V))


BIG = ("w_in", "w_uq", "w_ukv", "w_o_mla", "w_o_swa", "w_o_sb", "w_out", "w_up", "w_down")
W_EARLY = ("w_in", "w_uq", "w_ukv")
W_LATE = ("w_o_mla", "w_o_swa", "w_o_sb", "w_out", "w_up", "w_down")
ROW_SHARDED = ("w_out", "w_down")
SMALL = ("g_mix_pre", "b_gate", "g_q_lat", "g_kv_lat", "swa_sinks", "g_mix_post", "g_mlp_pre", "g_mlp_post")
PACK_COLS = 1024


def _to_shards(name, full):
    L, R, C = full.shape
    if name in ROW_SHARDED:
        return full.reshape(L, N_DEV, R // N_DEV, C).transpose(1, 0, 2, 3).reshape(N_DEV, L * R // N_DEV, C)
    return full.reshape(L, R, N_DEV, C // N_DEV).transpose(2, 0, 1, 3).reshape(N_DEV, L * R, C // N_DEV)


def _from_shards(name, gathered, shard_shape):
    L, r, c = shard_shape
    a = gathered.reshape(N_DEV, L, r, c)
    if name in ROW_SHARDED:
        return a.transpose(1, 0, 2, 3).reshape(L, N_DEV * r, c)
    return a.transpose(1, 2, 0, 3).reshape(L, r, N_DEV * c)


def _tables(positions):
    pos = positions.astype(F32).reshape(-1, 1)

    def cs(d):
        inv = 1.0 / (ROPE_THETA ** (jnp.arange(0, d, 2, dtype=F32) / d))
        ang = pos * inv
        c, s = jnp.cos(ang), jnp.sin(ang)
        return jnp.concatenate([c, c], axis=1), jnp.concatenate([-s, s], axis=1)

    c64, s64 = cs(SWA_HEAD_DIM)
    c32, s32 = cs(MLA_ROPE)
    n = pos.shape[0]
    one, zero = jnp.ones((n, MLA_NOPE), F32), jnp.zeros((n, MLA_NOPE), F32)
    pad0 = jnp.zeros((n, LANES - MLA_NOPE - MLA_ROPE), F32)
    cq = jnp.concatenate([one, c32, pad0], axis=1)
    sq = jnp.concatenate([zero, s32, pad0], axis=1)
    padk = jnp.zeros((n, LANES - MLA_ROPE), F32)
    return dict(
        c_swa_q=jnp.tile(c64, (1, SWA_HEADS)), s_swa_q=jnp.tile(s64, (1, SWA_HEADS)),
        c_swa_k=jnp.tile(c64, (1, SWA_KV_HEADS)), s_swa_k=jnp.tile(s64, (1, SWA_KV_HEADS)),
        c_mla_q=jnp.tile(cq, (1, MLA_HEADS)), s_mla_q=jnp.tile(sq, (1, MLA_HEADS)),
        c_mla_k=jnp.concatenate([c32, padk], axis=1), s_mla_k=jnp.concatenate([s32, padk], axis=1))


def _cb(name):
    return PERM_OFF[name] // PERM_WIDTH[name]


MLA_SCALE = (MLA_NOPE + MLA_ROPE) ** -0.5
MLA_FWD_TK = 1024
MLA_SPEC = dict(groups=MLA_HEADS // 2, tq=512, tk=512,
                heads=[(slice(h * LANES, (h + 1) * LANES), slice(h * LANES, (h + 1) * LANES), slice(h * MLA_V, (h + 1) * MLA_V)) for h in range(2)])
SWA_G = SWA_HEADS // SWA_KV_HEADS
SWA_TQ = 256
SB_SPEC = dict(tq=512, ts=256, scale=SB_HEAD_DIM ** -0.5, q_cb=PERM_OFF["q_sb"] // LANES, k_cb=PERM_OFF["k_sb"] // LANES,
               v_cb=PERM_OFF["v_sb"] // LANES)


def _mla_specs(S):
    tq = MLA_SPEC["tq"]
    return dict(q_spec=((tq, 2 * LANES), lambda g, i: (i, g)), k_spec=((S, 2 * LANES), lambda g, i: (0, g)),
                v_spec=((S, 2 * MLA_V), lambda g, i: (0, g)))


def _layer_fwd(l, x, W, P, tb, T, ride, late):
    S = x.shape[0]
    nt = S // T
    h, = _rowwise(f"l{l}_norm_in", lambda xv, g: (_rms(xv, g),), [_ri(x, T), _bi(P["g_mix_pre"])], [_ro(S, D_MODEL, BF16, T)], n_tiles=nt)
    proj, proj16 = _mm_nn(f"l{l}_mm_in", h, W["w_in"], [F32, BF16], tm=512, tn=IN_WIDTH_P // 2)

    def mix_prep(cq, ckv, qs, ks, vs, gq, gkv, cq_t, sq_t, ck_t, sk_t):
        ksr = _rope(ks, ck_t, sk_t, SWA_HEAD_DIM // 2)
        hd = SWA_HEAD_DIM
        return (_rms(cq, gq), _rms(ckv, gkv), _rope(qs, cq_t, sq_t, hd // 2),
                [ksr[:, :hd], ksr[:, hd:]], [vs[:, :hd], vs[:, hd:]])

    kv3 = lambda dt: ((SWA_KV_HEADS, S, SWA_HEAD_DIM), dt, (SWA_KV_HEADS, T, SWA_HEAD_DIM), lambda i: (0, i, 0))
    cqn, ckvn, q_swa, k_swa, v_swa = _rowwise(
        f"l{l}_mix_prep", mix_prep,
        [_ri(proj, T, 256, _cb("c_q")), _ri(proj, T, 128, _cb("c_kv")), _ri(proj, T, 512, _cb("q_swa")), _ri(proj, T, 128, _cb("k_swa")),
         _ri(proj, T, 128, _cb("v_swa")), _bi(P["g_q_lat"]), _bi(P["g_kv_lat"]), _ri(tb["c_swa_q"], T), _ri(tb["s_swa_q"], T),
         _ri(tb["c_swa_k"], T), _ri(tb["s_swa_k"], T)],
        [_ro(S, 256, BF16, T), _ro(S, 128, BF16, T), _ro(S, 512, BF16, T), kv3(BF16), kv3(BF16)], n_tiles=nt)
    k_swa = k_swa.reshape(SWA_KV_HEADS * S, SWA_HEAD_DIM)
    v_swa = v_swa.reshape(SWA_KV_HEADS * S, SWA_HEAD_DIM)
    q_lat, = _mm_nn(f"l{l}_mm_uq", cqn, W["w_uq"], [F32], tm=1024, tn=MLA_QW)
    kv_lat, = _mm_nn(f"l{l}_mm_ukv", ckvn, W["w_ukv"], [F32], tm=1024, tn=MLA_KVW)

    def mla_prep(q, kk, vv, kr, cq_t, sq_t, ck_t, sk_t):
        kpe = pltpu.roll(_rope(kr, ck_t, sk_t, MLA_ROPE // 2), MLA_NOPE, axis=1)
        return _rope(q, cq_t, sq_t, MLA_ROPE // 2) * MLA_SCALE, kk + jnp.tile(kpe, (1, MLA_HEADS)), vv

    q_mla, k_mla, v_mla = _rowwise(
        f"l{l}_mla_prep", mla_prep,
        [_ri(q_lat, T), _ri(kv_lat, T, MLA_HEADS * LANES, 0), _ri(kv_lat, T, MLA_HEADS * MLA_V, 2), _ri(proj, T, 128, _cb("k_rope")),
         _ri(tb["c_mla_q"], T), _ri(tb["s_mla_q"], T), _ri(tb["c_mla_k"], T), _ri(tb["s_mla_k"], T)],
        [_ro(S, MLA_QW, BF16, T), _ro(S, MLA_HEADS * LANES, BF16, T), _ro(S, MLA_HEADS * MLA_V, BF16, T)], n_tiles=nt)

    (att_a, lse_a), rode = _softmax_fwd(f"l{l}_mla_fwd", q_mla, k_mla, v_mla, o_width=MLA_HEADS * MLA_V, ride=ride, **{**MLA_SPEC, "tk": min(MLA_FWD_TK, S)}, **_mla_specs(S))
    W = {**W, **late(rode)}
    att_b, lse_b = _swa_fwd(f"l{l}_swa_fwd", q_swa, k_swa, v_swa, P["swa_sinks"], tq=SWA_TQ)
    att_c, att_c32 = _sb_fwd(f"l{l}_sb_fwd", proj16, **SB_SPEC)
    o_a, = _mm_nn(f"l{l}_mm_oa", att_a, W["w_o_mla"], [F32], tm=1024, tn=D_MODEL)
    o_b, = _mm_nn(f"l{l}_mm_ob", att_b, W["w_o_swa"], [F32], tm=1024, tn=D_MODEL)
    o_c, = _mm_nn(f"l{l}_mm_oc", att_c, W["w_o_sb"], [F32], tm=1024, tn=D_MODEL)

    def gate_mix(gl, b, oa, ob, oc):
        gt = jax.nn.sigmoid(gl + b)
        return (gt[:, :D_MODEL] * oa + gt[:, D_MODEL:2 * D_MODEL] * ob + gt[:, 2 * D_MODEL:] * oc,)

    mixed, = _rowwise(f"l{l}_gate_mix", gate_mix, [_ri(proj, T, 3072, 0), _bi(P["b_gate"]), _ri(o_a, T), _ri(o_b, T), _ri(o_c, T)],
                      [_ro(S, D_MODEL, BF16, T)], n_tiles=nt)
    y, = _mm_nn(f"l{l}_mm_out", mixed, W["w_out"], [F32], tm=1024, tn=D_MODEL)

    def resid_norm(xv, yv, gpost, gpre):
        x1 = xv + _rms(yv, gpost)
        return x1, _rms(x1, gpre)

    x1, h2 = _rowwise(f"l{l}_resid_norm", resid_norm, [_ri(x, T), _ri(y, T), _bi(P["g_mix_post"]), _bi(P["g_mlp_pre"])],
                      [_ro(S, D_MODEL, F32, T), _ro(S, D_MODEL, BF16, T)], n_tiles=nt)
    up, u = _mm_nn(f"l{l}_mm_up", h2, W["w_up"], [F32, BF16], tm=512, tn=2048,
                   epilogue=lambda acc: (acc, jnp.square(jnp.maximum(acc, 0.0))))
    dn, = _mm_nn(f"l{l}_mm_down", u, W["w_down"], [F32], tm=512, tn=D_MODEL)
    x2, = _rowwise(f"l{l}_resid_out", lambda xv, dv, g: (xv + _rms(dv, g),), [_ri(x1, T), _ri(dn, T), _bi(P["g_mlp_post"])],
                   [_ro(S, D_MODEL, F32, T)], n_tiles=nt)
    saved = dict(x=x, h=h, proj=proj, proj16=proj16, cqn=cqn, ckvn=ckvn, q_swa=q_swa, k_swa=k_swa, v_swa=v_swa, q_mla=q_mla, k_mla=k_mla,
                 v_mla=v_mla, att_a=att_a, lse_a=lse_a, att_b=att_b, lse_b=lse_b, att_c=att_c, att_c32=att_c32, o_a=o_a, o_b=o_b, o_c=o_c,
                 mixed=mixed, y=y, x1=x1, h2=h2, up=up, u=u, dn=dn)
    return x2, saved, W, rode


def _layer_bwd(l, dx2, sv, W, P, tb, T, pending, to_send):
    S = dx2.shape[0]
    nt = S // T
    G = {}

    def post_norm_bwd(v, g, dy):
        return _rms_bwd(v, g, dy)

    d_dn, G["g_mlp_post"] = _rowwise(f"l{l}_b_post2", post_norm_bwd, [_ri(sv["dn"], T), _bi(P["g_mlp_post"]), _ri(dx2, T)],
                                    [_ro(S, D_MODEL, BF16, T)], [D_MODEL], n_tiles=nt)
    d_up, = _mm_nt(f"l{l}_b_mm_down", d_dn, W["w_down"], [BF16], tm=512, tn=2048, extras=[sv["up"]],
                   epilogue=lambda acc, upv: (acc * (2.0 * jnp.maximum(upv, 0.0)),))
    G["w_down"] = _mm_tn(f"l{l}_g_down", sv["u"], d_dn, tm=2048, tn=D_MODEL, ts=512)
    d_h2, = _mm_nt(f"l{l}_b_mm_up", d_up, W["w_up"], [F32], tm=512, tn=D_MODEL)
    G["w_up"] = _mm_tn(f"l{l}_g_up", sv["h2"], d_up, tm=D_MODEL, tn=2048, ts=512)

    def pre_norm_bwd(v, g, dy, dres):
        dx, dg = _rms_bwd(v, g, dy)
        return dres + dx, dg

    dx1, G["g_mlp_pre"] = _rowwise(f"l{l}_b_pre2", pre_norm_bwd, [_ri(sv["x1"], T), _bi(P["g_mlp_pre"]), _ri(d_h2, T), _ri(dx2, T)],
                                  [_ro(S, D_MODEL, F32, T)], [D_MODEL], n_tiles=nt)
    d_y, G["g_mix_post"] = _rowwise(f"l{l}_b_post1", post_norm_bwd, [_ri(sv["y"], T), _bi(P["g_mix_post"]), _ri(dx1, T)],
                                   [_ro(S, D_MODEL, BF16, T)], [D_MODEL], n_tiles=nt)
    d_mixed, = _mm_nt(f"l{l}_b_mm_out", d_y, W["w_out"], [F32], tm=1024, tn=D_MODEL)
    G["w_out"] = _mm_tn(f"l{l}_g_out", sv["mixed"], d_y, tm=D_MODEL, tn=D_MODEL, ts=512)

    def gate_bwd(dm, gl, b, oa, ob, oc):
        gt = jax.nn.sigmoid(gl + b)
        outs, dgl = [], []
        for k, o in enumerate((oa, ob, oc)):
            gk = gt[:, k * D_MODEL:(k + 1) * D_MODEL]
            outs.append(dm * gk)
            dgl.append(dm * o * gk * (1.0 - gk))
        dgl = jnp.concatenate(dgl, axis=1)
        return (*outs, dgl, jnp.sum(dgl, axis=0, keepdims=True))

    d_oa, d_ob, d_oc, d_gl, G["b_gate"] = _rowwise(
        f"l{l}_b_gate", gate_bwd, [_ri(d_mixed, T), _ri(sv["proj"], T, 3072, 0), _bi(P["b_gate"]), _ri(sv["o_a"], T), _ri(sv["o_b"], T), _ri(sv["o_c"], T)],
        [_ro(S, D_MODEL, BF16, T)] * 3 + [_ro(S, 3 * D_MODEL, BF16, T)], [3 * D_MODEL], n_tiles=nt)
    d_att = {}
    for br, d_o, att in (("mla", d_oa, sv["att_a"]), ("swa", d_ob, sv["att_b"]), ("sb", d_oc, sv["att_c"])):
        d_att[br], = _mm_nt(f"l{l}_b_mm_o_{br}", d_o, W["w_o_" + br], [F32], tm=1024, tn=512)
        G["w_o_" + br] = _mm_tn(f"l{l}_g_o_{br}", att, d_o, tm=512, tn=D_MODEL, ts=512)

    riders = list(pending) + [(n, l, to_send(n, G[n])) for n in W_LATE]
    (dq_mla, dkt_mla, dvt_mla), rode = _softmax_bwd(f"l{l}_mla_bwd", sv["q_mla"], sv["k_mla"], sv["v_mla"], sv["att_a"], d_att["mla"], sv["lse_a"],
                                                  scale=MLA_SCALE, ride=([s for _, _, s in riders], True), **MLA_SPEC, **_mla_specs(S))
    rode = [(n, ly, r) for (n, ly, _), r in zip(riders, rode)]
    dk_mla = dkt_mla.transpose(0, 2, 1).reshape(S, -1)
    dv_mla = dvt_mla.transpose(0, 2, 1).reshape(S, -1)
    dq_swa, dk_swa, dv_swa, dsink = _swa_bwd(f"l{l}_swa_bwd", sv["q_swa"], sv["k_swa"], sv["v_swa"], sv["att_b"], d_att["swa"], sv["lse_b"],
                                             P["swa_sinks"], tq=SWA_TQ)
    G["swa_sinks"] = dsink[:, :SWA_G, 0].reshape(1, SWA_HEADS)
    dq_sb, dkt_sb, dvt_sb = _sb_bwd(f"l{l}_sb_bwd", sv["proj16"], sv["att_c32"], d_att["sb"], **SB_SPEC)
    dk_sb = dkt_sb.transpose(0, 2, 1).reshape(S, -1)
    dv_sb = dvt_sb.transpose(0, 2, 1).reshape(S, -1)

    def mla_prep_bwd(dq, dk, dvv, cq_t, sq_t, ck_t, sk_t):
        dks = dk[:, :LANES]
        for hh in range(1, MLA_HEADS):
            dks = dks + dk[:, hh * LANES:(hh + 1) * LANES]
        d_kr = _rope_t(pltpu.roll(dks, LANES - MLA_NOPE, axis=1), ck_t, sk_t, MLA_ROPE // 2)
        return _rope_t(dq, cq_t, sq_t, MLA_ROPE // 2), jnp.concatenate([dk, dvv], axis=1), d_kr

    d_q_lat, d_kv_lat, d_krope = _rowwise(
        f"l{l}_b_mla_prep", mla_prep_bwd,
        [_ri(dq_mla, T), _ri(dk_mla, T), _ri(dv_mla, T), _ri(tb["c_mla_q"], T), _ri(tb["s_mla_q"], T), _ri(tb["c_mla_k"], T), _ri(tb["s_mla_k"], T)],
        [_ro(S, MLA_QW, BF16, T), _ro(S, MLA_KVW, BF16, T), _ro(S, LANES, BF16, T)], n_tiles=nt)
    d_cqn, = _mm_nt(f"l{l}_b_mm_uq", d_q_lat, W["w_uq"], [F32], tm=1024, tn=MLA_Q_LORA)
    G["w_uq"] = _mm_tn(f"l{l}_g_uq", sv["cqn"], d_q_lat, tm=MLA_Q_LORA, tn=MLA_QW, ts=512)
    d_ckvn, = _mm_nt(f"l{l}_b_mm_ukv", d_kv_lat, W["w_ukv"], [F32], tm=1024, tn=MLA_KV_LORA)
    G["w_ukv"] = _mm_tn(f"l{l}_g_ukv", sv["ckvn"], d_kv_lat, tm=MLA_KV_LORA, tn=MLA_KVW, ts=512)

    def mix_prep_bwd(cq, ckv, gq, gkv, dcqn, dckvn, dqs, dks, dvs, cq_t, sq_t, ck_t, sk_t):
        d_cq, dgq = _rms_bwd(cq, gq, dcqn)
        d_ckv, dgkv = _rms_bwd(ckv, gkv, dckvn)
        dk2 = jnp.concatenate([dks[0], dks[1]], axis=1)
        dv2 = jnp.concatenate([dvs[0], dvs[1]], axis=1)
        return (d_cq, d_ckv, _rope_t(dqs, cq_t, sq_t, SWA_HEAD_DIM // 2), _rope_t(dk2, ck_t, sk_t, SWA_HEAD_DIM // 2), dv2, dgq, dgkv)

    kv3 = lambda a: (a.reshape(SWA_KV_HEADS, S, SWA_HEAD_DIM), (SWA_KV_HEADS, T, SWA_HEAD_DIM), lambda i: (0, i, 0))
    d_cq, d_ckv, d_qswa, d_kswa, d_vswa, G["g_q_lat"], G["g_kv_lat"] = _rowwise(
        f"l{l}_b_mix_prep", mix_prep_bwd,
        [_ri(sv["proj"], T, 256, _cb("c_q")), _ri(sv["proj"], T, 128, _cb("c_kv")), _bi(P["g_q_lat"]), _bi(P["g_kv_lat"]), _ri(d_cqn, T), _ri(d_ckvn, T),
         _ri(dq_swa, T), kv3(dk_swa), kv3(dv_swa), _ri(tb["c_swa_q"], T), _ri(tb["s_swa_q"], T), _ri(tb["c_swa_k"], T), _ri(tb["s_swa_k"], T)],
        [_ro(S, 256, BF16, T), _ro(S, 128, BF16, T), _ro(S, 512, BF16, T), _ro(S, 128, BF16, T), _ro(S, 128, BF16, T)], [256, 128], n_tiles=nt)
    pieces = dict(gates=d_gl, q_swa=d_qswa, q_sb=dq_sb, k_sb=dk_sb, v_sb=dv_sb, c_q=d_cq, c_kv=d_ckv, k_swa=d_kswa, v_swa=d_vswa, k_rope=d_krope)
    d_proj = jnp.concatenate([pieces[n].astype(BF16) for n in PERM_ORDER], axis=1)
    d_h, = _mm_nt(f"l{l}_b_mm_in", d_proj, W["w_in"], [F32], tm=512, tn=512)
    G["w_in"] = _mm_tn(f"l{l}_g_in", sv["h"], d_proj, tm=D_MODEL, tn=IN_WIDTH_P // 2, ts=512)
    dx, G["g_mix_pre"] = _rowwise(f"l{l}_b_pre1", pre_norm_bwd, [_ri(sv["x"], T), _bi(P["g_mix_pre"]), _ri(d_h, T), _ri(dx1, T)],
                                 [_ro(S, D_MODEL, F32, T)], [D_MODEL], n_tiles=nt)
    return dx, G, rode


def _pack_rows(vecs, rows):
    flat = jnp.concatenate([v.reshape(-1) for v in vecs])
    return jnp.pad(flat, (0, rows * PACK_COLS - flat.shape[0])).reshape(rows, PACK_COLS)


def kernel(x, positions, g_mix_pre, w_in, b_gate, g_q_lat, g_kv_lat, w_uq, w_ukv, swa_sinks, w_o_mla, w_o_swa, w_o_sb, w_out, g_mix_post, g_mlp_pre, w_up, w_down, g_mlp_post, loss_target, m_g_mix_pre, m_w_in, m_b_gate, m_g_q_lat, m_g_kv_lat, m_w_uq, m_w_ukv, m_swa_sinks, m_w_o_mla, m_w_o_swa, m_w_o_sb, m_w_out, m_g_mix_post, m_g_mlp_pre, m_w_up, m_w_down, m_g_mlp_post, v_g_mix_pre, v_w_in, v_b_gate, v_g_q_lat, v_g_kv_lat, v_w_uq, v_w_ukv, v_swa_sinks, v_w_o_mla, v_w_o_swa, v_w_o_sb, v_w_out, v_g_mix_post, v_g_mlp_pre, v_w_up, v_w_down, v_g_mlp_post):
    a = dict(locals())
    S = x.shape[1]
    depth = w_in.shape[0]
    T = min(256, S)
    xs = x.reshape(S, D_MODEL)
    tb = _tables(positions)

    perm = dict(w_in=_perm_w_in, w_uq=_perm_w_uq, w_ukv=_perm_w_ukv)

    def shards(l, names):
        return [a[n][l].astype(BF16) for n in names]

    def whole(names, gathered):
        return {n: perm.get(n, lambda t: t)(_from_shards(n, g, (1,) + a[n].shape[1:])[0]) for n, g in zip(names, gathered)}

    early = whole(W_EARLY, _exchange("gather_weights", shards(0, W_EARLY), per_peer=False))
    layers, saved = [], []
    h = xs
    for l in range(depth):
        P = {n: a[n][l].reshape(1, -1) for n in SMALL if n != "swa_sinks"}
        P["swa_sinks"] = a["swa_sinks"][l]
        srcs = shards(l, W_LATE) + (shards(l + 1, W_EARLY) if l + 1 < depth else [])
        h, sv, W, rode = _layer_fwd(l, h, early, P, tb, T, (srcs, False), lambda r: whole(W_LATE, r[:len(W_LATE)]))
        early = whole(W_EARLY, rode[len(W_LATE):])
        layers.append((W, P))
        saved.append(sv)

    def loss_head(yv, tv):
        err = yv - tv
        part = 0.5 * jnp.sum(jnp.mean(err * err, axis=1, keepdims=True), axis=0, keepdims=True)
        return err * (1.0 / D_MODEL), jnp.broadcast_to(part, (1, LANES))

    dh, loss_row = _rowwise("loss_head", loss_head, [_ri(h, T), _ri(loss_target.reshape(S, D_MODEL), T)], [_ro(S, D_MODEL, F32, T)], [LANES],
                            n_tiles=S // T)
    loss = lax.psum(loss_row[0, 0], ("x", "y", "c"))

    unperm = dict(w_in=_unperm_w_in, w_uq=_unperm_w_uq, w_ukv=_unperm_w_ukv)

    def to_send(n, g):
        return _to_shards(n, unperm.get(n, lambda t: t)(g).astype(BF16)[None])

    grads, recv, pending = [None] * depth, {n: [None] * depth for n in BIG}, []
    for l in reversed(range(depth)):
        W, P = layers[l]
        dh, grads[l], rode = _layer_bwd(l, dh, saved[l], W, P, tb, T, pending, to_send)
        for n, ly, r in rode:
            recv[n][ly] = r
        pending = [(n, l, to_send(n, grads[l][n])) for n in W_EARLY]
    for (n, ly, _), r in zip(pending, _exchange("scatter_grads", [s for _, _, s in pending], per_peer=True)):
        recv[n][ly] = r
    grad_x = dh.reshape(x.shape)

    out = {}
    for n in BIG:
        shp = a[n].shape
        rows, cols = shp[0] * shp[1], shp[2]
        parts = jnp.concatenate(recv[n], axis=1)
        res = _adamw("adamw_" + n, parts, a[n].reshape(rows, cols), a["m_" + n].reshape(rows, cols), a["v_" + n].reshape(rows, cols),
                     tile=min(256, rows))
        out[n] = [r.reshape(shp) for r in res]

    small_total = sum(a[n].size for n in SMALL)
    small_rows = -(-small_total // (8 * PACK_COLS)) * 8
    sg = _pack_rows([jnp.stack([grads[l][n].reshape(-1) for l in range(depth)]) for n in SMALL], small_rows)
    sg_all, = _exchange("gather_small_grads", [sg], per_peer=False)
    res = _adamw("adamw_small", sg_all, _pack_rows([a[n] for n in SMALL], small_rows), _pack_rows([a["m_" + n] for n in SMALL], small_rows),
                 _pack_rows([a["v_" + n] for n in SMALL], small_rows), tile=small_rows)
    off = 0
    for n in SMALL:
        cnt = a[n].size
        out[n] = [r.reshape(-1)[off:off + cnt].reshape(a[n].shape) for r in res]
        off += cnt

    order = ("g_mix_pre", "w_in", "b_gate", "g_q_lat", "g_kv_lat", "w_uq", "w_ukv", "swa_sinks", "w_o_mla", "w_o_swa", "w_o_sb", "w_out",
             "g_mix_post", "g_mlp_pre", "w_up", "w_down", "g_mlp_post")
    return (loss, grad_x, *[out[n][0] for n in order], *[out[n][1] for n in order], *[out[n][2] for n in order], *[out[n][3] for n in order])
```

```python
import functools

import jax
import jax.numpy as jnp
from jax import lax
from jax.experimental import pallas as pl
from jax.experimental.pallas import tpu as pltpu

F32, BF16 = jnp.float32, jnp.bfloat16

D_MODEL = 1024
DEPTH = 4
MLA_HEADS, MLA_Q_LORA, MLA_KV_LORA, MLA_NOPE, MLA_ROPE, MLA_V = 8, 256, 128, 64, 32, 64
SWA_HEADS, SWA_KV_HEADS, SWA_HEAD_DIM, SWA_WINDOW = 8, 2, 64, 128
SB_HEADS, SB_HEAD_DIM = 8, 64
D_FF = 4 * D_MODEL
ROPE_THETA = 10000.0
EPS = 1e-6
N_DEV = 8
ADAM_LR, ADAM_B1, ADAM_B2, ADAM_EPS, ADAM_WD, ADAM_STEP = 0.001, 0.9, 0.999, 1e-08, 0.01, 10

LANES = 128
VMEM_LIMIT_MAX = 60 * 1024 * 1024
VMEM_LIMIT_MIN = 32 * 1024 * 1024

ORIG_COLS = dict(c_q=(0, 256), c_kv=(256, 128), k_rope=(384, 32), q_swa=(416, 512), k_swa=(928, 128), v_swa=(1056, 128),
                 q_sb=(1184, 512), k_sb=(1696, 512), v_sb=(2208, 512), gates=(2720, 3072))
IN_WIDTH = 5792
PERM_ORDER = ("gates", "q_swa", "q_sb", "k_sb", "v_sb", "c_q", "c_kv", "k_swa", "v_swa", "k_rope")
PERM_WIDTH = dict(gates=3072, q_swa=512, q_sb=512, k_sb=512, v_sb=512, c_q=256, c_kv=128, k_swa=128, v_swa=128, k_rope=128)
PERM_OFF = {}
_o = 0
for _n in PERM_ORDER:
    PERM_OFF[_n] = _o
    _o += PERM_WIDTH[_n]
IN_WIDTH_P = _o
MLA_QW = MLA_HEADS * LANES
MLA_KVW = MLA_HEADS * LANES + MLA_HEADS * MLA_V

ROW_TILE = 512
NT = (((1,), (1,)), ((), ()))
TN = (((0,), (0,)), ((), ()))
NEG = -1e30


def _cparams(sem, block_bytes):
    limit = int(min(VMEM_LIMIT_MAX, max(VMEM_LIMIT_MIN, 2 * block_bytes + (16 << 20))))
    return pltpu.CompilerParams(dimension_semantics=sem, vmem_limit_bytes=limit)


def _nbytes(shape, dtype):
    n = 1
    for s in shape:
        n *= s
    return n * jnp.dtype(dtype).itemsize


def _ri(arr, tile, width=None, cb=0):
    width = arr.shape[1] if width is None else width
    return (arr, (tile, width), lambda i, cb=cb: (i, cb))


def _bi(arr):
    return (arr, arr.shape, lambda i: (0, 0))


def _ro(rows, width, dtype, tile):
    return ((rows, width), dtype, (tile, width), lambda i: (i, 0))


def _rowwise(name, fn, ins, outs, reds=(), *, n_tiles):
    n_in, n_out = len(ins), len(outs)

    def body(*refs):
        vals = fn(*[r[...] for r in refs[:n_in]])
        for r, v in zip(refs[n_in:n_in + n_out], vals[:n_out]):
            if isinstance(v, (list, tuple)):
                for j, vj in enumerate(v):
                    r[j] = vj.astype(r.dtype)
            else:
                r[...] = v.astype(r.dtype)
        if reds:
            @pl.when(pl.program_id(0) == 0)
            def _():
                for r in refs[n_in + n_out:]:
                    r[...] = jnp.zeros_like(r)
            for r, v in zip(refs[n_in + n_out:], vals[n_out:]):
                r[...] += v

    block_bytes = sum(_nbytes(b, a.dtype) for a, b, _ in ins) + sum(_nbytes(b, d) for _, d, b, _ in outs)
    res = pl.pallas_call(
        body, name=name, grid=(n_tiles,),
        in_specs=[pl.BlockSpec(b, m) for _, b, m in ins],
        out_specs=[pl.BlockSpec(b, m) for _, _, b, m in outs] + [pl.BlockSpec((1, w), lambda i: (0, 0)) for w in reds],
        out_shape=[jax.ShapeDtypeStruct(s, d) for s, d, _, _ in outs] + [jax.ShapeDtypeStruct((1, w), F32) for w in reds],
        compiler_params=_cparams(("arbitrary",) if reds else ("parallel",), block_bytes),
    )(*[a for a, _, _ in ins])
    return res


def _rms(x, g):
    r = lax.rsqrt(jnp.mean(x * x, axis=1, keepdims=True) + EPS)
    return x * r * g


def _rms_bwd(x, g, dy):
    r = lax.rsqrt(jnp.mean(x * x, axis=1, keepdims=True) + EPS)
    xn = x * r
    dxn = dy * g
    dx = r * (dxn - xn * jnp.mean(dxn * xn, axis=1, keepdims=True))
    return dx, jnp.sum(dy * xn, axis=0, keepdims=True)


def _swap_halves(x, half):
    n = x.shape[1]
    lane = lax.broadcasted_iota(jnp.int32, x.shape, 1)
    first = (lane % (2 * half)) < half
    return jnp.where(first, pltpu.roll(x, n - half, axis=1), pltpu.roll(x, half, axis=1))


def _rope(x, c, sg, half):
    return x * c + _swap_halves(x, half) * sg


def _rope_t(dy, c, sg, half):
    return dy * c - _swap_halves(dy, half) * sg


def _mm_nn(name, a, b, outs, *, tm, tn, extras=(), epilogue=None):
    M, K = a.shape
    N = b.shape[1]
    tm = min(tm, M)
    n_e = len(extras)

    def body(*refs):
        a_ref, b_ref = refs[:2]
        acc = jnp.dot(a_ref[...].astype(BF16), b_ref[...].astype(BF16), preferred_element_type=F32)
        vals = (acc,) * len(outs) if epilogue is None else epilogue(acc, *[r[...] for r in refs[2:2 + n_e]])
        for r, v in zip(refs[2 + n_e:], vals):
            r[...] = v.astype(r.dtype)

    block_bytes = (_nbytes((tm, K), a.dtype) + _nbytes((K, tn), b.dtype) + sum(_nbytes((tm, tn), e.dtype) for e in extras)
                   + sum(_nbytes((tm, tn), d) for d in outs) + _nbytes((tm, tn), F32))
    return pl.pallas_call(
        body, name=name, grid=(N // tn, M // tm),
        in_specs=[pl.BlockSpec((tm, K), lambda j, i: (i, 0)), pl.BlockSpec((K, tn), lambda j, i: (0, j))]
        + [pl.BlockSpec((tm, tn), lambda j, i: (i, j)) for _ in extras],
        out_specs=[pl.BlockSpec((tm, tn), lambda j, i: (i, j)) for _ in outs],
        out_shape=[jax.ShapeDtypeStruct((M, N), d) for d in outs],
        compiler_params=_cparams(("parallel", "parallel"), block_bytes),
    )(a, b, *extras)


def _mm_nt(name, a, b, outs, *, tm, tn, extras=(), epilogue=None):
    M, N = a.shape
    K = b.shape[0]
    tm = min(tm, M)
    n_e = len(extras)

    def body(*refs):
        a_ref, b_ref = refs[:2]
        acc = lax.dot_general(a_ref[...].astype(BF16), b_ref[...].astype(BF16), NT, preferred_element_type=F32)
        vals = (acc,) * len(outs) if epilogue is None else epilogue(acc, *[r[...] for r in refs[2:2 + n_e]])
        for r, v in zip(refs[2 + n_e:], vals):
            r[...] = v.astype(r.dtype)

    block_bytes = (_nbytes((tm, N), a.dtype) + _nbytes((tn, N), b.dtype) + sum(_nbytes((tm, tn), e.dtype) for e in extras)
                   + sum(_nbytes((tm, tn), d) for d in outs) + _nbytes((tm, tn), F32))
    return pl.pallas_call(
        body, name=name, grid=(K // tn, M // tm),
        in_specs=[pl.BlockSpec((tm, N), lambda j, i: (i, 0)), pl.BlockSpec((tn, N), lambda j, i: (j, 0))]
        + [pl.BlockSpec((tm, tn), lambda j, i: (i, j)) for _ in extras],
        out_specs=[pl.BlockSpec((tm, tn), lambda j, i: (i, j)) for _ in outs],
        out_shape=[jax.ShapeDtypeStruct((M, K), d) for d in outs],
        compiler_params=_cparams(("parallel", "parallel"), block_bytes),
    )(a, b, *extras)


def _mm_tn(name, a, b, *, tm, tn, ts):
    S, K = a.shape
    N = b.shape[1]
    ts = min(ts, S)

    def body(a_ref, b_ref, o_ref):
        @pl.when(pl.program_id(2) == 0)
        def _():
            o_ref[...] = jnp.zeros_like(o_ref)
        o_ref[...] += lax.dot_general(a_ref[...].astype(BF16), b_ref[...].astype(BF16), TN, preferred_element_type=F32)

    block_bytes = _nbytes((ts, tm), a.dtype) + _nbytes((ts, tn), b.dtype) + 2 * _nbytes((tm, tn), F32)
    return pl.pallas_call(
        body, name=name, grid=(K // tm, N // tn, S // ts),
        in_specs=[pl.BlockSpec((ts, tm), lambda i, j, s: (s, i)), pl.BlockSpec((ts, tn), lambda i, j, s: (s, j))],
        out_specs=pl.BlockSpec((tm, tn), lambda i, j, s: (i, j)),
        out_shape=jax.ShapeDtypeStruct((K, N), F32),
        compiler_params=_cparams(("parallel", "parallel", "arbitrary"), block_bytes),
    )(a, b)


def _lane_pack(cols, rows):
    lane = lax.broadcasted_iota(jnp.int32, (rows, LANES), 1)
    val = jnp.zeros((rows, LANES), F32)
    for h, c in enumerate(cols):
        val = jnp.where(lane == h, c, val)
    return val


def _mask(kb, row, tk):
    return kb * tk + lax.broadcasted_iota(jnp.int32, (1, tk), 1) <= row


def _kb_range(i, tq, tk):
    return (i * tq) // tk, ((i + 1) * tq + tk - 1) // tk


def _softmax_fwd(name, q, k, v, *, groups, heads, q_spec, k_spec, v_spec, o_width, tq, tk, ride=None):
    S = q.shape[0]
    nq = S // tq
    nh = len(heads)
    dv = heads[0][2].stop - heads[0][2].start

    def body(q_ref, k_ref, v_ref, o_ref, lse_ref):
        i = pl.program_id(1)
        q_all = q_ref[...]
        row = i * tq + lax.broadcasted_iota(jnp.int32, (tq, 1), 0)
        mid, hi = _kb_range(i, tq, tk)
        qhs = [q_all[:, qs] for qs, _, _ in heads]

        def blk(kb, carry, masked):
            r0 = pl.multiple_of(kb * tk, tk)
            k_all = k_ref[pl.ds(r0, tk), :]
            v_all = v_ref[pl.ds(r0, tk), :]
            if masked:
                ok = _mask(kb, row, tk)
            ones = jnp.ones((tk, dv), BF16)
            new = []
            for qh, (_, ks, vs), (m, acc) in zip(qhs, heads, carry):
                s = lax.dot_general(qh, k_all[:, ks], NT, preferred_element_type=F32)
                if masked:
                    s = jnp.where(ok, s, NEG)
                m_new = jnp.maximum(m, jnp.max(s, axis=1, keepdims=True))
                p = jnp.exp(s - m_new)
                if masked:
                    p = jnp.where(ok, p, 0.0)
                pv = jnp.dot(p.astype(BF16), jnp.concatenate([v_all[:, vs], ones], axis=1), preferred_element_type=F32)
                new.append((m_new, jnp.exp(m - m_new) * acc + pv))
            return tuple(new)

        carry = tuple((jnp.full((tq, 1), NEG, F32), jnp.zeros((tq, 2 * dv), F32)) for _ in range(nh))
        carry = lax.fori_loop(0, mid, functools.partial(blk, masked=False), carry)
        carry = lax.fori_loop(mid, hi, functools.partial(blk, masked=True), carry)
        o_ref[...] = jnp.concatenate([acc[:, :dv] / acc[:, dv:] for _, acc in carry], axis=1).astype(o_ref.dtype)
        lse_ref[0] = _lane_pack([m + jnp.log(acc[:, dv:dv + 1]) for m, acc in carry], tq)

    wo = nh * dv
    block_bytes = _nbytes(q_spec[0], q.dtype) + _nbytes(k_spec[0], k.dtype) + _nbytes(v_spec[0], v.dtype) + 4 * tq * (wo + LANES)
    return _pallas(
        body, name=name, grid=(groups, nq), in_specs=[pl.BlockSpec(*q_spec), pl.BlockSpec(*k_spec), pl.BlockSpec(*v_spec)],
        out_specs=[pl.BlockSpec((tq, wo), lambda g, i: (i, g)), pl.BlockSpec((1, tq, LANES), lambda g, i: (g, i, 0))],
        out_shape=[jax.ShapeDtypeStruct((S, o_width), BF16), jax.ShapeDtypeStruct((groups, S, LANES), F32)],
        args=[q, k, v], sem=("parallel", "arbitrary"), block_bytes=block_bytes, ride=ride)


def _softmax_bwd(name, q, k, v, o, do, lse, *, groups, heads, q_spec, k_spec, v_spec, tq, tk, scale, ride=None):
    S = q.shape[0]
    nq, nkb = S // tq, S // tk
    nh = len(heads)
    dv = heads[0][2].stop - heads[0][2].start
    wo = nh * dv
    wk, wv = k_spec[0][1], v_spec[0][1]

    def body(q_ref, k_ref, v_ref, o_ref, do_ref, lse_ref, dq_ref, dkt_ref, dvt_ref):
        i = pl.program_id(1)

        @pl.when(i == 0)
        def _():
            dkt_ref[...] = jnp.zeros_like(dkt_ref)
            dvt_ref[...] = jnp.zeros_like(dvt_ref)

        q_all = q_ref[...]
        o_all = o_ref[...].astype(F32)
        do_all = do_ref[...].astype(F32)
        lse_all = lse_ref[0]
        row = i * tq + lax.broadcasted_iota(jnp.int32, (tq, 1), 0)
        mid, hi = _kb_range(i, tq, tk)
        per_head = []
        for h, (qs, ks, vs) in enumerate(heads):
            osl = slice(h * dv, (h + 1) * dv)
            doh = do_all[:, osl]
            delta = jnp.sum(doh * o_all[:, osl], axis=1, keepdims=True)
            qh = q_all[:, qs]
            per_head.append((qh, qh.astype(F32).T.astype(BF16), doh.astype(BF16), doh.T.astype(BF16), delta, lse_all[:, h:h + 1], ks, vs))

        def blk(kb, dqs, masked):
            r0 = pl.multiple_of(kb * tk, tk)
            k_all = k_ref[pl.ds(r0, tk), :]
            v_all = v_ref[pl.ds(r0, tk), :]
            if masked:
                ok = _mask(kb, row, tk)
            dkt, dvt, new_dqs = [], [], []
            for (qh, qt, doh, dot_, delta, lse_h, ks, vs), dq in zip(per_head, dqs):
                kk, vv = k_all[:, ks], v_all[:, vs]
                s = lax.dot_general(qh, kk, NT, preferred_element_type=F32)
                p = jnp.exp(s - lse_h)
                if masked:
                    p = jnp.where(ok, p, 0.0)
                dp = lax.dot_general(doh, vv, NT, preferred_element_type=F32)
                ds = (p * (dp - delta)).astype(BF16)
                new_dqs.append(dq + jnp.dot(ds, kk, preferred_element_type=F32))
                dkt.append(jnp.dot(qt, ds, preferred_element_type=F32))
                dvt.append(jnp.dot(dot_, p.astype(BF16), preferred_element_type=F32))
            dkt_ref[kb] += jnp.concatenate(dkt, axis=0)
            dvt_ref[kb] += jnp.concatenate(dvt, axis=0)
            return tuple(new_dqs)

        dqs = tuple(jnp.zeros((tq, qs.stop - qs.start), F32) for qs, _, _ in heads)
        dqs = lax.fori_loop(0, mid, functools.partial(blk, masked=False), dqs)
        dqs = lax.fori_loop(mid, hi, functools.partial(blk, masked=True), dqs)
        dq_ref[...] = jnp.concatenate([dq * scale for dq in dqs], axis=1)

    in_specs = [pl.BlockSpec(*q_spec), pl.BlockSpec(*k_spec), pl.BlockSpec(*v_spec),
                pl.BlockSpec((tq, wo), lambda g, i: (i, g)), pl.BlockSpec((tq, wo), lambda g, i: (i, g)),
                pl.BlockSpec((1, tq, LANES), lambda g, i: (g, i, 0))]
    args = [q, k, v, o, do, lse]
    out_specs = [pl.BlockSpec(*q_spec), pl.BlockSpec((nkb, wk, tk), lambda g, i: (0, g, 0)), pl.BlockSpec((nkb, wv, tk), lambda g, i: (0, g, 0))]
    out_shape = [jax.ShapeDtypeStruct(q.shape, F32), jax.ShapeDtypeStruct((nkb, k.shape[1], tk), F32),
                 jax.ShapeDtypeStruct((nkb, v.shape[1], tk), F32)]
    block_bytes = (_nbytes(q_spec[0], q.dtype) + _nbytes(k_spec[0], k.dtype) + _nbytes(v_spec[0], v.dtype) + 6 * tq * wo + 4 * tq * LANES
                   + _nbytes(q_spec[0], F32) + _nbytes(k_spec[0], F32) + _nbytes(v_spec[0], F32))
    return _pallas(body, name=name, grid=(groups, nq), in_specs=in_specs, out_specs=out_specs, out_shape=out_shape, args=args,
                   sem=("parallel", "arbitrary"), block_bytes=block_bytes, ride=ride)


def _swa_window(i, tq, row):
    start = pl.multiple_of(jnp.maximum(i * tq - SWA_WINDOW, 0), SWA_WINDOW)
    col = start + lax.broadcasted_iota(jnp.int32, (1, tq + SWA_WINDOW), 1)
    return start, (col <= row) & ((row - col) < SWA_WINDOW)


def _swa_fwd(name, q, k, v, sinks, *, tq):
    S = q.shape[0]
    hd, span = SWA_HEAD_DIM, tq + SWA_WINDOW
    scale = hd ** -0.5

    def body(sink_ref, q_ref, k_ref, v_ref, o_ref, lse_ref):
        g, i = pl.program_id(0), pl.program_id(1)
        row = i * tq + lax.broadcasted_iota(jnp.int32, (tq, 1), 0)
        start, ok = _swa_window(i, tq, row)
        kk, vv = k_ref[pl.ds(start, span), :], v_ref[pl.ds(start, span), :]
        q_all = q_ref[...]
        outs, lses = [], []
        for h in range(SWA_G):
            s = lax.dot_general(q_all[:, h * hd:(h + 1) * hd] * scale, kk, NT, preferred_element_type=F32)
            s = jnp.where(ok, s, NEG)
            sink = sink_ref[g * SWA_G + h]
            m = jnp.maximum(jnp.max(s, axis=1, keepdims=True), sink)
            p = jnp.exp(s - m)
            l = jnp.sum(p, axis=1, keepdims=True) + jnp.exp(sink - m)
            outs.append(jnp.dot(p.astype(BF16), vv, preferred_element_type=F32) / l)
            lses.append(m + jnp.log(l))
        o_ref[...] = jnp.concatenate(outs, axis=1).astype(o_ref.dtype)
        lse_ref[0] = _lane_pack(lses, tq)

    wq = SWA_G * hd
    block_bytes = 2 * tq * wq * 2 + 2 * 2 * S * hd + 4 * tq * LANES
    return pl.pallas_call(
        body, name=name, grid=(SWA_KV_HEADS, S // tq),
        in_specs=[pl.BlockSpec(memory_space=pltpu.SMEM), pl.BlockSpec((tq, wq), lambda g, i: (i, g)),
                  pl.BlockSpec((S, hd), lambda g, i: (g, 0)), pl.BlockSpec((S, hd), lambda g, i: (g, 0))],
        out_specs=[pl.BlockSpec((tq, wq), lambda g, i: (i, g)), pl.BlockSpec((1, tq, LANES), lambda g, i: (g, i, 0))],
        out_shape=[jax.ShapeDtypeStruct((S, SWA_HEADS * hd), BF16), jax.ShapeDtypeStruct((SWA_KV_HEADS, S, LANES), F32)],
        compiler_params=_cparams(("parallel", "parallel"), block_bytes),
    )(sinks, q, k, v)


def _swa_bwd(name, q, k, v, o, do, lse, sinks, *, tq):
    S = q.shape[0]
    hd, span = SWA_HEAD_DIM, tq + SWA_WINDOW
    scale = hd ** -0.5
    nblk = S // SWA_WINDOW

    def body(sink_ref, q_ref, k_ref, v_ref, o_ref, do_ref, lse_ref, dq_ref, dkt_ref, dvt_ref, dsink_ref):
        g, i = pl.program_id(0), pl.program_id(1)

        @pl.when(i == 0)
        def _():
            dkt_ref[...] = jnp.zeros_like(dkt_ref)
            dvt_ref[...] = jnp.zeros_like(dvt_ref)
            dsink_ref[...] = jnp.zeros_like(dsink_ref)

        row = i * tq + lax.broadcasted_iota(jnp.int32, (tq, 1), 0)
        start, ok = _swa_window(i, tq, row)
        kk, vv = k_ref[pl.ds(start, span), :], v_ref[pl.ds(start, span), :]
        q_all, o_all, do_all, lse_all = q_ref[...], o_ref[...].astype(F32), do_ref[...].astype(F32), lse_ref[0]
        dqs, qts, dots, dss, ps = [], [], [], [], []
        for h in range(SWA_G):
            sl = slice(h * hd, (h + 1) * hd)
            qh = q_all[:, sl] * scale
            doh = do_all[:, sl]
            delta = jnp.sum(doh * o_all[:, sl], axis=1, keepdims=True)
            lse_h = lse_all[:, h:h + 1]
            qts.append(qh.astype(F32).T.astype(BF16))
            dots.append(doh.T.astype(BF16))
            doh = doh.astype(BF16)
            s = jnp.where(ok, lax.dot_general(qh, kk, NT, preferred_element_type=F32), NEG)
            p = jnp.exp(s - lse_h)
            ds = (p * (lax.dot_general(doh, vv, NT, preferred_element_type=F32) - delta)).astype(BF16)
            dqs.append(jnp.dot(ds, kk, preferred_element_type=F32) * scale)
            dss.append(ds)
            ps.append(p.astype(BF16))
            p_sink = jnp.exp(sink_ref[g * SWA_G + h] - lse_h)
            dsink_ref[0, h:h + 1, :] += jnp.broadcast_to(-jnp.sum(p_sink * delta, axis=0, keepdims=True), (1, LANES))
        dq_ref[...] = jnp.concatenate(dqs, axis=1)
        dkt = jnp.dot(jnp.concatenate(qts, axis=1), jnp.concatenate(dss, axis=0), preferred_element_type=F32)
        dvt = jnp.dot(jnp.concatenate(dots, axis=1), jnp.concatenate(ps, axis=0), preferred_element_type=F32)
        for j in range(span // SWA_WINDOW):
            cols = slice(j * SWA_WINDOW, (j + 1) * SWA_WINDOW)
            dkt_ref[start // SWA_WINDOW + j] += dkt[:, cols]
            dvt_ref[start // SWA_WINDOW + j] += dvt[:, cols]

    wq = SWA_G * hd
    qb = pl.BlockSpec((tq, wq), lambda g, i: (i, g))
    kb = pl.BlockSpec((S, hd), lambda g, i: (g, 0))
    tb = pl.BlockSpec((nblk, hd, SWA_WINDOW), lambda g, i: (g, 0, 0))
    t_shape = jax.ShapeDtypeStruct((SWA_KV_HEADS * nblk, hd, SWA_WINDOW), F32)
    block_bytes = 2 * tq * wq * (2 + 2 + 4 + 4) + 2 * S * hd * (2 + 2 + 4 + 4) + 4 * tq * LANES
    return pl.pallas_call(
        body, name=name, grid=(SWA_KV_HEADS, S // tq),
        in_specs=[pl.BlockSpec(memory_space=pltpu.SMEM), qb, kb, kb, qb, qb, pl.BlockSpec((1, tq, LANES), lambda g, i: (g, i, 0))],
        out_specs=[qb, tb, tb, pl.BlockSpec((1, 8, LANES), lambda g, i: (g, 0, 0))],
        out_shape=[jax.ShapeDtypeStruct(q.shape, F32), t_shape, t_shape, jax.ShapeDtypeStruct((SWA_KV_HEADS, 8, LANES), F32)],
        compiler_params=_cparams(("parallel", "arbitrary"), block_bytes),
    )(sinks, q, k, v, o, do, lse)


def _split_dot(x, u2):
    hi = x.astype(BF16)
    lo = (x - hi.astype(F32)).astype(BF16)
    return jnp.dot(jnp.concatenate([hi, lo], axis=1), u2, preferred_element_type=F32)


def _suffix_ones():
    r = lax.broadcasted_iota(jnp.int32, (2 * LANES, 2 * LANES), 0) % LANES
    c = lax.broadcasted_iota(jnp.int32, (2 * LANES, 2 * LANES), 1)
    return ((r > c) | (c >= LANES)).astype(BF16)


def _suffix_scan(x, uo, run):
    nc = x.shape[1] // LANES
    out = [None] * nc
    for c in reversed(range(nc)):
        st = _split_dot(x[:, c * LANES:(c + 1) * LANES], uo)
        out[c] = st[:, :LANES] + run
        run = run + st[:, LANES:]
    return (out[0] if nc == 1 else jnp.concatenate(out, axis=1)), run


SB_DEAD_LOG = -110.0


def _sb_logits(qh, kk, r0, row, masked):
    z = lax.dot_general(qh, kk, NT, preferred_element_type=F32)
    nz = -z
    l = jnp.minimum(nz, 0.0) - jnp.log(1.0 + jnp.exp(jnp.minimum(z, nz)))
    ok = None
    if masked:
        col = r0 + lax.broadcasted_iota(jnp.int32, (1, kk.shape[0]), 1)
        ok = col < row
        l = jnp.where(ok, l, 0.0)
    return z, l, ok


def _sb_walk(blk, carry, i, tq, ts):
    carry = blk(pl.multiple_of(i * tq, tq), tq, carry, True)

    def live(c):
        t, heads = c
        top = jnp.max(heads[0][0])
        for h in heads[1:]:
            top = jnp.maximum(top, jnp.max(h[0]))
        return jnp.logical_and(t < i * (tq // ts), top > SB_DEAD_LOG)

    def step(c):
        t, heads = c
        return t + 1, blk(pl.multiple_of(i * tq - (t + 1) * ts, ts), ts, heads, False)

    return lax.while_loop(live, step, (jnp.int32(0), carry))[1]


def _sb_fwd(name, pb, *, q_cb, k_cb, v_cb, tq, ts, scale):
    S = pb.shape[0]
    nq = S // tq
    hd = SB_HEAD_DIM
    groups = SB_HEADS * hd // LANES
    nh = LANES // hd

    def body(q_ref, k_ref, v_ref, o16_ref, o32_ref):
        i = pl.program_id(1)
        q_all = q_ref[...]
        row = i * tq + lax.broadcasted_iota(jnp.int32, (tq, 1), 0)
        uo = _suffix_ones()
        sls = [slice(h * hd, (h + 1) * hd) for h in range(nh)]
        qhs = [q_all[:, sl] * scale for sl in sls]

        def blk(r0, width, carry, masked):
            k_all = k_ref[pl.ds(r0, width), :]
            v_all = v_ref[pl.ds(r0, width), :]
            new = []
            for qh, sl, (run_l, acc) in zip(qhs, sls, carry):
                z, l, ok = _sb_logits(qh, k_all[:, sl], r0, row, masked)
                tail, run_l = _suffix_scan(l, uo, run_l)
                e = z + l + tail
                if masked:
                    e = jnp.where(ok, e, NEG)
                new.append((run_l, acc + jnp.dot(jnp.exp(e).astype(BF16), v_all[:, sl], preferred_element_type=F32)))
            return tuple(new)

        carry = tuple((jnp.zeros((tq, LANES), F32), jnp.zeros((tq, hd), F32)) for _ in range(nh))
        carry = _sb_walk(blk, carry, i, tq, ts)
        o = jnp.concatenate([acc for _, acc in carry], axis=1)
        o16_ref[...] = o.astype(BF16)
        o32_ref[...] = o

    block_bytes = 2 * tq * LANES + 2 * 2 * S * LANES + 6 * tq * LANES
    return pl.pallas_call(
        body, name=name, grid=(groups, nq),
        in_specs=[pl.BlockSpec((tq, LANES), lambda g, i: (i, q_cb + g)), pl.BlockSpec((S, LANES), lambda g, i: (0, k_cb + g)),
                  pl.BlockSpec((S, LANES), lambda g, i: (0, v_cb + g))],
        out_specs=[pl.BlockSpec((tq, LANES), lambda g, i: (i, g)), pl.BlockSpec((tq, LANES), lambda g, i: (i, g))],
        out_shape=[jax.ShapeDtypeStruct((S, groups * LANES), BF16), jax.ShapeDtypeStruct((S, groups * LANES), F32)],
        compiler_params=_cparams(("parallel", "arbitrary"), block_bytes),
    )(pb, pb, pb)


def _sb_bwd(name, pb, o32, do, *, q_cb, k_cb, v_cb, tq, ts, scale):
    S = pb.shape[0]
    nq = S // tq
    hd = SB_HEAD_DIM
    groups = SB_HEADS * hd // LANES
    nh = LANES // hd

    def body(q_ref, k_ref, v_ref, o_ref, do_ref, dq_ref, dkt_ref, dvt_ref):
        i = pl.program_id(1)

        @pl.when(i == 0)
        def _():
            dkt_ref[...] = jnp.zeros_like(dkt_ref)
            dvt_ref[...] = jnp.zeros_like(dvt_ref)

        q_all = q_ref[...]
        o_all = o_ref[...]
        do_all = do_ref[...].astype(F32)
        row = i * tq + lax.broadcasted_iota(jnp.int32, (tq, 1), 0)
        uo = _suffix_ones()
        per_head = []
        for h in range(nh):
            sl = slice(h * hd, (h + 1) * hd)
            doh = do_all[:, sl].astype(BF16)
            total = jnp.sum(doh.astype(F32) * o_all[:, sl], axis=1, keepdims=True)
            qh = q_all[:, sl] * scale
            per_head.append((qh, qh.astype(F32).T.astype(BF16), doh, doh.astype(F32).T.astype(BF16), jnp.broadcast_to(total, (tq, LANES)), sl))

        def blk(r0, width, carry, masked):
            k_all = k_ref[pl.ds(r0, width), :]
            v_all = v_ref[pl.ds(r0, width), :]
            new, dk_c, dv_c = [], [], []
            for (qh, qt, doh, dot_, total, sl), (run_l, run_g, dq) in zip(per_head, carry):
                kk, vv = k_all[:, sl], v_all[:, sl]
                z, l, ok = _sb_logits(qh, kk, r0, row, masked)
                tail, run_l = _suffix_scan(l, uo, run_l)
                e = z + l
                beta = jnp.exp(e)
                e = e + tail
                if masked:
                    e = jnp.where(ok, e, NEG)
                a = jnp.exp(e).astype(BF16).astype(F32)
                gr = lax.dot_general(doh, vv, NT, preferred_element_type=F32) * a
                right, run_g = _suffix_scan(gr, uo, run_g)
                nc = width // LANES
                prefix = (total if nc == 1 else jnp.tile(total, (1, nc))) - right
                dz = gr - beta * prefix
                if masked:
                    dz = jnp.where(ok, dz, 0.0)
                dzb = dz.astype(BF16)
                new.append((run_l, run_g, dq + jnp.dot(dzb, kk, preferred_element_type=F32)))
                dk_c.append(jnp.dot(qt, dzb, preferred_element_type=F32))
                dv_c.append(jnp.dot(dot_, a.astype(BF16), preferred_element_type=F32))
            dkt, dvt = jnp.concatenate(dk_c, axis=0), jnp.concatenate(dv_c, axis=0)
            for j in range(width // ts):
                dkt_ref[r0 // ts + j] += dkt[:, j * ts:(j + 1) * ts]
                dvt_ref[r0 // ts + j] += dvt[:, j * ts:(j + 1) * ts]
            return tuple(new)

        zc = jnp.zeros((tq, LANES), F32)
        carry = tuple((zc, zc, jnp.zeros((tq, hd), F32)) for _ in range(nh))
        carry = _sb_walk(blk, carry, i, tq, ts)
        dq_ref[...] = jnp.concatenate([c[2] * scale for c in carry], axis=1)

    W = groups * LANES
    block_bytes = 2 * tq * LANES + 2 * 2 * S * LANES + 3 * 4 * tq * LANES + 2 * 4 * S * LANES
    return pl.pallas_call(
        body, name=name, grid=(groups, nq),
        in_specs=[pl.BlockSpec((tq, LANES), lambda g, i: (i, q_cb + g)), pl.BlockSpec((S, LANES), lambda g, i: (0, k_cb + g)),
                  pl.BlockSpec((S, LANES), lambda g, i: (0, v_cb + g)), pl.BlockSpec((tq, LANES), lambda g, i: (i, g)),
                  pl.BlockSpec((tq, LANES), lambda g, i: (i, g))],
        out_specs=[pl.BlockSpec((tq, LANES), lambda g, i: (i, g)), pl.BlockSpec((S // ts, LANES, ts), lambda g, i: (0, g, 0)),
                   pl.BlockSpec((S // ts, LANES, ts), lambda g, i: (0, g, 0))],
        out_shape=[jax.ShapeDtypeStruct((S, W), F32)] + [jax.ShapeDtypeStruct((S // ts, W, ts), F32)] * 2,
        compiler_params=_cparams(("parallel", "arbitrary"), block_bytes),
    )(pb, pb, pb, o32, do)


def _exchange_copies(src_refs, out_refs, send_sems, recv_sems, local_sems, per_peer):
    x, y, c = lax.axis_index("x"), lax.axis_index("y"), lax.axis_index("c")
    me = 4 * x + 2 * y + c
    n = len(src_refs)

    def copy(j, k):
        px, py, pc = x ^ (k >> 2), y ^ ((k >> 1) & 1), c ^ (k & 1)
        s = src_refs[j].at[4 * px + 2 * py + pc] if per_peer else src_refs[j]
        return pltpu.make_async_remote_copy(
            src_ref=s, dst_ref=out_refs[j].at[me], send_sem=send_sems.at[j, k - 1], recv_sem=recv_sems.at[j, k - 1],
            device_id=(px, py, pc), device_id_type=pl.DeviceIdType.MESH)

    mine = [pltpu.make_async_copy(src_refs[j].at[me] if per_peer else src_refs[j], out_refs[j].at[me], local_sems.at[j]) for j in range(n)]
    return mine, [copy(j, k) for k in range(1, N_DEV) for j in range(n)]


def _exchange_start(*refs, per_peer):
    mine, copies = _exchange_copies(*refs, per_peer)
    for cp in mine + copies:
        cp.start()


def _exchange_wait(*refs, per_peer):
    mine, copies = _exchange_copies(*refs, per_peer)
    for cp in copies:
        cp.wait_recv()
    for cp in copies:
        cp.wait_send()
    for cp in mine:
        cp.wait()


def _exchange_specs(srcs, per_peer):
    n = len(srcs)
    out_shape = [jax.ShapeDtypeStruct((N_DEV,) + tuple(s.shape[1:] if per_peer else s.shape), s.dtype) for s in srcs]
    sems = [pltpu.SemaphoreType.DMA((n, N_DEV - 1)), pltpu.SemaphoreType.DMA((n, N_DEV - 1)), pltpu.SemaphoreType.DMA((n,))]
    return [pl.BlockSpec(memory_space=pltpu.HBM)] * n, out_shape, sems


def _exchange(name, srcs, per_peer):
    n = len(srcs)

    def body(*refs):
        parts = (refs[:n], refs[n:2 * n], *refs[2 * n:])
        _exchange_start(*parts, per_peer=per_peer)
        _exchange_wait(*parts, per_peer=per_peer)

    hbm, out_shape, sems = _exchange_specs(srcs, per_peer)
    return pl.pallas_call(body, name=name, in_specs=hbm, out_specs=hbm, out_shape=out_shape, scratch_shapes=sems)(*srcs)


def _pallas(body, *, name, grid, in_specs, out_specs, out_shape, args, sem, block_bytes, ride=None):
    if ride is None:
        return pl.pallas_call(body, name=name, grid=grid, in_specs=in_specs, out_specs=out_specs, out_shape=out_shape,
                              compiler_params=_cparams(sem, block_bytes))(*args), None
    srcs, per_peer = ride
    n, n_in, n_out = len(srcs), len(in_specs), len(out_specs)

    def riding(*refs):
        ins, xsrc = refs[:n_in], refs[n_in:n_in + n]
        outs, xout = refs[n_in + n:n_in + n + n_out], refs[n_in + n + n_out:n_in + 2 * n + n_out]
        parts = (xsrc, xout, *refs[n_in + 2 * n + n_out:])
        ids = [pl.program_id(d) for d in range(len(grid))]
        first = functools.reduce(jnp.logical_and, [i == 0 for i in ids])
        last = functools.reduce(jnp.logical_and, [i == g - 1 for i, g in zip(ids, grid)])
        pl.when(first)(functools.partial(_exchange_start, *parts, per_peer=per_peer))
        body(*ins, *outs)
        pl.when(last)(functools.partial(_exchange_wait, *parts, per_peer=per_peer))

    hbm, x_shape, sems = _exchange_specs(srcs, per_peer)
    res = pl.pallas_call(riding, name=name, grid=grid, in_specs=list(in_specs) + hbm, out_specs=list(out_specs) + hbm,
                         out_shape=list(out_shape) + x_shape, scratch_shapes=sems,
                         compiler_params=_cparams(("arbitrary",) * len(grid), block_bytes))(*args, *srcs)
    return res[:n_out], res[n_out:]


def _adamw(name, parts, w, m, v, *, tile):
    R, C = w.shape

    def body(p_ref, w_ref, m_ref, v_ref, g_ref, d_ref, nm_ref, nv_ref):
        g = p_ref[0].astype(F32)
        for d in range(1, N_DEV):
            g = g + p_ref[d].astype(F32)
        wv = w_ref[...]
        mm = ADAM_B1 * m_ref[...] + (1.0 - ADAM_B1) * g
        vv = ADAM_B2 * v_ref[...] + (1.0 - ADAM_B2) * jnp.square(g)
        m_hat = mm / (1.0 - ADAM_B1 ** ADAM_STEP)
        v_hat = vv / (1.0 - ADAM_B2 ** ADAM_STEP)
        g_ref[...] = g
        d_ref[...] = -ADAM_LR * (m_hat / (jnp.sqrt(v_hat) + ADAM_EPS) + ADAM_WD * wv)
        nm_ref[...] = mm
        nv_ref[...] = vv

    blk = pl.BlockSpec((tile, C), lambda i: (i, 0))
    block_bytes = N_DEV * _nbytes((tile, C), parts.dtype) + 7 * _nbytes((tile, C), F32)
    return pl.pallas_call(
        body, name=name, grid=(R // tile,),
        in_specs=[pl.BlockSpec((N_DEV, tile, C), lambda i: (0, i, 0)), blk, blk, blk],
        out_specs=[blk] * 4, out_shape=[jax.ShapeDtypeStruct((R, C), F32)] * 4,
        compiler_params=_cparams(("parallel",), block_bytes),
    )(parts, w, m, v)


def _perm_w_in(w):
    cols = [w[:, ORIG_COLS[n][0]:ORIG_COLS[n][0] + ORIG_COLS[n][1]] for n in PERM_ORDER]
    cols.append(jnp.zeros((w.shape[0], IN_WIDTH_P - IN_WIDTH), w.dtype))
    return jnp.concatenate(cols, axis=1)


def _unperm_w_in(wp):
    order = sorted(ORIG_COLS, key=lambda n: ORIG_COLS[n][0])
    return jnp.concatenate([wp[:, PERM_OFF[n]:PERM_OFF[n] + ORIG_COLS[n][1]] for n in order], axis=1)


def _perm_w_uq(w):
    w3 = w.reshape(w.shape[0], MLA_HEADS, MLA_NOPE + MLA_ROPE)
    return jnp.pad(w3, ((0, 0), (0, 0), (0, LANES - MLA_NOPE - MLA_ROPE))).reshape(w.shape[0], MLA_QW)


def _unperm_w_uq(wp):
    return wp.reshape(wp.shape[0], MLA_HEADS, LANES)[:, :, :MLA_NOPE + MLA_ROPE].reshape(wp.shape[0], -1)


def _perm_w_ukv(w):
    w3 = w.reshape(w.shape[0], MLA_HEADS, MLA_NOPE + MLA_V)
    kp = jnp.pad(w3[:, :, :MLA_NOPE], ((0, 0), (0, 0), (0, LANES - MLA_NOPE))).reshape(w.shape[0], MLA_HEADS * LANES)
    return jnp.concatenate([kp, w3[:, :, MLA_NOPE:].reshape(w.shape[0], MLA_HEADS * MLA_V)], axis=1)


def _unperm_w_ukv(wp):
    n = wp.shape[0]
    kp = wp[:, :MLA_HEADS * LANES].reshape(n, MLA_HEADS, LANES)[:, :, :MLA_NOPE]
    vp = wp[:, MLA_HEADS * LANES:].reshape(n, MLA_HEADS, MLA_V)
    return jnp.concatenate([kp, vp], axis=2).reshape(n, MLA_HEADS * (MLA_NOPE + MLA_V))


BIG = ("w_in", "w_uq", "w_ukv", "w_o_mla", "w_o_swa", "w_o_sb", "w_out", "w_up", "w_down")
W_EARLY = ("w_in", "w_uq", "w_ukv")
W_LATE = ("w_o_mla", "w_o_swa", "w_o_sb", "w_out", "w_up", "w_down")
ROW_SHARDED = ("w_out", "w_down")
SMALL = ("g_mix_pre", "b_gate", "g_q_lat", "g_kv_lat", "swa_sinks", "g_mix_post", "g_mlp_pre", "g_mlp_post")
PACK_COLS = 1024


def _to_shards(name, full):
    L, R, C = full.shape
    if name in ROW_SHARDED:
        return full.reshape(L, N_DEV, R // N_DEV, C).transpose(1, 0, 2, 3).reshape(N_DEV, L * R // N_DEV, C)
    return full.reshape(L, R, N_DEV, C // N_DEV).transpose(2, 0, 1, 3).reshape(N_DEV, L * R, C // N_DEV)


def _from_shards(name, gathered, shard_shape):
    L, r, c = shard_shape
    a = gathered.reshape(N_DEV, L, r, c)
    if name in ROW_SHARDED:
        return a.transpose(1, 0, 2, 3).reshape(L, N_DEV * r, c)
    return a.transpose(1, 2, 0, 3).reshape(L, r, N_DEV * c)


def _tables(positions):
    pos = positions.astype(F32).reshape(-1, 1)

    def cs(d):
        inv = 1.0 / (ROPE_THETA ** (jnp.arange(0, d, 2, dtype=F32) / d))
        ang = pos * inv
        c, s = jnp.cos(ang), jnp.sin(ang)
        return jnp.concatenate([c, c], axis=1), jnp.concatenate([-s, s], axis=1)

    c64, s64 = cs(SWA_HEAD_DIM)
    c32, s32 = cs(MLA_ROPE)
    n = pos.shape[0]
    one, zero = jnp.ones((n, MLA_NOPE), F32), jnp.zeros((n, MLA_NOPE), F32)
    pad0 = jnp.zeros((n, LANES - MLA_NOPE - MLA_ROPE), F32)
    cq = jnp.concatenate([one, c32, pad0], axis=1)
    sq = jnp.concatenate([zero, s32, pad0], axis=1)
    padk = jnp.zeros((n, LANES - MLA_ROPE), F32)
    return dict(
        c_swa_q=jnp.tile(c64, (1, SWA_HEADS)), s_swa_q=jnp.tile(s64, (1, SWA_HEADS)),
        c_swa_k=jnp.tile(c64, (1, SWA_KV_HEADS)), s_swa_k=jnp.tile(s64, (1, SWA_KV_HEADS)),
        c_mla_q=jnp.tile(cq, (1, MLA_HEADS)), s_mla_q=jnp.tile(sq, (1, MLA_HEADS)),
        c_mla_k=jnp.concatenate([c32, padk], axis=1), s_mla_k=jnp.concatenate([s32, padk], axis=1))


def _cb(name):
    return PERM_OFF[name] // PERM_WIDTH[name]


MLA_SCALE = (MLA_NOPE + MLA_ROPE) ** -0.5
MLA_FWD_TK = 1024
MLA_SPEC = dict(groups=MLA_HEADS // 2, tq=512, tk=512,
                heads=[(slice(h * LANES, (h + 1) * LANES), slice(h * LANES, (h + 1) * LANES), slice(h * MLA_V, (h + 1) * MLA_V)) for h in range(2)])
SWA_G = SWA_HEADS // SWA_KV_HEADS
SWA_TQ = 256
SB_SPEC = dict(tq=512, ts=256, scale=SB_HEAD_DIM ** -0.5, q_cb=PERM_OFF["q_sb"] // LANES, k_cb=PERM_OFF["k_sb"] // LANES,
               v_cb=PERM_OFF["v_sb"] // LANES)


def _mla_specs(S):
    tq = MLA_SPEC["tq"]
    return dict(q_spec=((tq, 2 * LANES), lambda g, i: (i, g)), k_spec=((S, 2 * LANES), lambda g, i: (0, g)),
                v_spec=((S, 2 * MLA_V), lambda g, i: (0, g)))


def _layer_fwd(l, x, W, P, tb, T, ride, late):
    S = x.shape[0]
    nt = S // T
    h, = _rowwise(f"l{l}_norm_in", lambda xv, g: (_rms(xv, g),), [_ri(x, T), _bi(P["g_mix_pre"])], [_ro(S, D_MODEL, BF16, T)], n_tiles=nt)
    proj, proj16 = _mm_nn(f"l{l}_mm_in", h, W["w_in"], [F32, BF16], tm=512, tn=IN_WIDTH_P // 2)

    def mix_prep(cq, ckv, qs, ks, vs, gq, gkv, cq_t, sq_t, ck_t, sk_t):
        ksr = _rope(ks, ck_t, sk_t, SWA_HEAD_DIM // 2)
        hd = SWA_HEAD_DIM
        return (_rms(cq, gq), _rms(ckv, gkv), _rope(qs, cq_t, sq_t, hd // 2),
                [ksr[:, :hd], ksr[:, hd:]], [vs[:, :hd], vs[:, hd:]])

    kv3 = lambda dt: ((SWA_KV_HEADS, S, SWA_HEAD_DIM), dt, (SWA_KV_HEADS, T, SWA_HEAD_DIM), lambda i: (0, i, 0))
    cqn, ckvn, q_swa, k_swa, v_swa = _rowwise(
        f"l{l}_mix_prep", mix_prep,
        [_ri(proj, T, 256, _cb("c_q")), _ri(proj, T, 128, _cb("c_kv")), _ri(proj, T, 512, _cb("q_swa")), _ri(proj, T, 128, _cb("k_swa")),
         _ri(proj, T, 128, _cb("v_swa")), _bi(P["g_q_lat"]), _bi(P["g_kv_lat"]), _ri(tb["c_swa_q"], T), _ri(tb["s_swa_q"], T),
         _ri(tb["c_swa_k"], T), _ri(tb["s_swa_k"], T)],
        [_ro(S, 256, BF16, T), _ro(S, 128, BF16, T), _ro(S, 512, BF16, T), kv3(BF16), kv3(BF16)], n_tiles=nt)
    k_swa = k_swa.reshape(SWA_KV_HEADS * S, SWA_HEAD_DIM)
    v_swa = v_swa.reshape(SWA_KV_HEADS * S, SWA_HEAD_DIM)
    q_lat, = _mm_nn(f"l{l}_mm_uq", cqn, W["w_uq"], [F32], tm=1024, tn=MLA_QW)
    kv_lat, = _mm_nn(f"l{l}_mm_ukv", ckvn, W["w_ukv"], [F32], tm=1024, tn=MLA_KVW)

    def mla_prep(q, kk, vv, kr, cq_t, sq_t, ck_t, sk_t):
        kpe = pltpu.roll(_rope(kr, ck_t, sk_t, MLA_ROPE // 2), MLA_NOPE, axis=1)
        return _rope(q, cq_t, sq_t, MLA_ROPE // 2) * MLA_SCALE, kk + jnp.tile(kpe, (1, MLA_HEADS)), vv

    q_mla, k_mla, v_mla = _rowwise(
        f"l{l}_mla_prep", mla_prep,
        [_ri(q_lat, T), _ri(kv_lat, T, MLA_HEADS * LANES, 0), _ri(kv_lat, T, MLA_HEADS * MLA_V, 2), _ri(proj, T, 128, _cb("k_rope")),
         _ri(tb["c_mla_q"], T), _ri(tb["s_mla_q"], T), _ri(tb["c_mla_k"], T), _ri(tb["s_mla_k"], T)],
        [_ro(S, MLA_QW, BF16, T), _ro(S, MLA_HEADS * LANES, BF16, T), _ro(S, MLA_HEADS * MLA_V, BF16, T)], n_tiles=nt)

    (att_a, lse_a), rode = _softmax_fwd(f"l{l}_mla_fwd", q_mla, k_mla, v_mla, o_width=MLA_HEADS * MLA_V, ride=ride, **{**MLA_SPEC, "tk": min(MLA_FWD_TK, S)}, **_mla_specs(S))
    W = {**W, **late(rode)}
    att_b, lse_b = _swa_fwd(f"l{l}_swa_fwd", q_swa, k_swa, v_swa, P["swa_sinks"], tq=SWA_TQ)
    att_c, att_c32 = _sb_fwd(f"l{l}_sb_fwd", proj16, **SB_SPEC)
    o_a, = _mm_nn(f"l{l}_mm_oa", att_a, W["w_o_mla"], [F32], tm=1024, tn=D_MODEL)
    o_b, = _mm_nn(f"l{l}_mm_ob", att_b, W["w_o_swa"], [F32], tm=1024, tn=D_MODEL)
    o_c, = _mm_nn(f"l{l}_mm_oc", att_c, W["w_o_sb"], [F32], tm=1024, tn=D_MODEL)

    def gate_mix(gl, b, oa, ob, oc):
        gt = jax.nn.sigmoid(gl + b)
        return (gt[:, :D_MODEL] * oa + gt[:, D_MODEL:2 * D_MODEL] * ob + gt[:, 2 * D_MODEL:] * oc,)

    mixed, = _rowwise(f"l{l}_gate_mix", gate_mix, [_ri(proj, T, 3072, 0), _bi(P["b_gate"]), _ri(o_a, T), _ri(o_b, T), _ri(o_c, T)],
                      [_ro(S, D_MODEL, BF16, T)], n_tiles=nt)
    y, = _mm_nn(f"l{l}_mm_out", mixed, W["w_out"], [F32], tm=1024, tn=D_MODEL)

    def resid_norm(xv, yv, gpost, gpre):
        x1 = xv + _rms(yv, gpost)
        return x1, _rms(x1, gpre)

    x1, h2 = _rowwise(f"l{l}_resid_norm", resid_norm, [_ri(x, T), _ri(y, T), _bi(P["g_mix_post"]), _bi(P["g_mlp_pre"])],
                      [_ro(S, D_MODEL, F32, T), _ro(S, D_MODEL, BF16, T)], n_tiles=nt)
    up, u = _mm_nn(f"l{l}_mm_up", h2, W["w_up"], [F32, BF16], tm=512, tn=2048,
                   epilogue=lambda acc: (acc, jnp.square(jnp.maximum(acc, 0.0))))
    dn, = _mm_nn(f"l{l}_mm_down", u, W["w_down"], [F32], tm=512, tn=D_MODEL)
    x2, = _rowwise(f"l{l}_resid_out", lambda xv, dv, g: (xv + _rms(dv, g),), [_ri(x1, T), _ri(dn, T), _bi(P["g_mlp_post"])],
                   [_ro(S, D_MODEL, F32, T)], n_tiles=nt)
    saved = dict(x=x, h=h, proj=proj, proj16=proj16, cqn=cqn, ckvn=ckvn, q_swa=q_swa, k_swa=k_swa, v_swa=v_swa, q_mla=q_mla, k_mla=k_mla,
                 v_mla=v_mla, att_a=att_a, lse_a=lse_a, att_b=att_b, lse_b=lse_b, att_c=att_c, att_c32=att_c32, o_a=o_a, o_b=o_b, o_c=o_c,
                 mixed=mixed, y=y, x1=x1, h2=h2, up=up, u=u, dn=dn)
    return x2, saved, W, rode


def _layer_bwd(l, dx2, sv, W, P, tb, T, pending, to_send):
    S = dx2.shape[0]
    nt = S // T
    G = {}

    def post_norm_bwd(v, g, dy):
        return _rms_bwd(v, g, dy)

    d_dn, G["g_mlp_post"] = _rowwise(f"l{l}_b_post2", post_norm_bwd, [_ri(sv["dn"], T), _bi(P["g_mlp_post"]), _ri(dx2, T)],
                                    [_ro(S, D_MODEL, BF16, T)], [D_MODEL], n_tiles=nt)
    d_up, = _mm_nt(f"l{l}_b_mm_down", d_dn, W["w_down"], [BF16], tm=512, tn=2048, extras=[sv["up"]],
                   epilogue=lambda acc, upv: (acc * (2.0 * jnp.maximum(upv, 0.0)),))
    G["w_down"] = _mm_tn(f"l{l}_g_down", sv["u"], d_dn, tm=2048, tn=D_MODEL, ts=512)
    d_h2, = _mm_nt(f"l{l}_b_mm_up", d_up, W["w_up"], [F32], tm=512, tn=D_MODEL)
    G["w_up"] = _mm_tn(f"l{l}_g_up", sv["h2"], d_up, tm=D_MODEL, tn=2048, ts=512)

    def pre_norm_bwd(v, g, dy, dres):
        dx, dg = _rms_bwd(v, g, dy)
        return dres + dx, dg

    def mid_norms_bwd(x1v, gpre, dh2, dres, yv, gpost):
        dx1v, dgpre = pre_norm_bwd(x1v, gpre, dh2, dres)
        dyv, dgpost = post_norm_bwd(yv, gpost, dx1v)
        return dx1v, dyv, dgpre, dgpost

    dx1, d_y, G["g_mlp_pre"], G["g_mix_post"] = _rowwise(
        f"l{l}_b_mid_norms", mid_norms_bwd,
        [_ri(sv["x1"], T), _bi(P["g_mlp_pre"]), _ri(d_h2, T), _ri(dx2, T), _ri(sv["y"], T), _bi(P["g_mix_post"])],
        [_ro(S, D_MODEL, F32, T), _ro(S, D_MODEL, BF16, T)], [D_MODEL, D_MODEL], n_tiles=nt)
    d_mixed, = _mm_nt(f"l{l}_b_mm_out", d_y, W["w_out"], [F32], tm=1024, tn=D_MODEL)
    G["w_out"] = _mm_tn(f"l{l}_g_out", sv["mixed"], d_y, tm=D_MODEL, tn=D_MODEL, ts=512)

    def gate_bwd(dm, gl, b, oa, ob, oc):
        gt = jax.nn.sigmoid(gl + b)
        outs, dgl = [], []
        for k, o in enumerate((oa, ob, oc)):
            gk = gt[:, k * D_MODEL:(k + 1) * D_MODEL]
            outs.append(dm * gk)
            dgl.append(dm * o * gk * (1.0 - gk))
        dgl = jnp.concatenate(dgl, axis=1)
        return (*outs, dgl, jnp.sum(dgl, axis=0, keepdims=True))

    d_oa, d_ob, d_oc, d_gl, G["b_gate"] = _rowwise(
        f"l{l}_b_gate", gate_bwd, [_ri(d_mixed, T), _ri(sv["proj"], T, 3072, 0), _bi(P["b_gate"]), _ri(sv["o_a"], T), _ri(sv["o_b"], T), _ri(sv["o_c"], T)],
        [_ro(S, D_MODEL, BF16, T)] * 3 + [_ro(S, 3 * D_MODEL, BF16, T)], [3 * D_MODEL], n_tiles=nt)
    d_att = {}
    for br, d_o, att in (("mla", d_oa, sv["att_a"]), ("swa", d_ob, sv["att_b"]), ("sb", d_oc, sv["att_c"])):
        d_att[br], = _mm_nt(f"l{l}_b_mm_o_{br}", d_o, W["w_o_" + br], [F32], tm=1024, tn=512)
        G["w_o_" + br] = _mm_tn(f"l{l}_g_o_{br}", att, d_o, tm=512, tn=D_MODEL, ts=512)

    riders = list(pending) + [(n, l, to_send(n, G[n])) for n in W_LATE]
    (dq_mla, dkt_mla, dvt_mla), rode = _softmax_bwd(f"l{l}_mla_bwd", sv["q_mla"], sv["k_mla"], sv["v_mla"], sv["att_a"], d_att["mla"], sv["lse_a"],
                                                  scale=MLA_SCALE, ride=([s for _, _, s in riders], True), **MLA_SPEC, **_mla_specs(S))
    rode = [(n, ly, r) for (n, ly, _), r in zip(riders, rode)]
    dk_mla = dkt_mla.transpose(0, 2, 1).reshape(S, -1)
    dv_mla = dvt_mla.transpose(0, 2, 1).reshape(S, -1)
    dq_swa, dkt_swa, dvt_swa, dsink = _swa_bwd(f"l{l}_swa_bwd", sv["q_swa"], sv["k_swa"], sv["v_swa"], sv["att_b"], d_att["swa"], sv["lse_b"],
                                               P["swa_sinks"], tq=SWA_TQ)
    dk_swa = dkt_swa.transpose(0, 2, 1).reshape(SWA_KV_HEADS * S, SWA_HEAD_DIM)
    dv_swa = dvt_swa.transpose(0, 2, 1).reshape(SWA_KV_HEADS * S, SWA_HEAD_DIM)
    G["swa_sinks"] = dsink[:, :SWA_G, 0].reshape(1, SWA_HEADS)
    dq_sb, dkt_sb, dvt_sb = _sb_bwd(f"l{l}_sb_bwd", sv["proj16"], sv["att_c32"], d_att["sb"], **SB_SPEC)
    dk_sb = dkt_sb.transpose(0, 2, 1).reshape(S, -1)
    dv_sb = dvt_sb.transpose(0, 2, 1).reshape(S, -1)

    def mla_prep_bwd(dq, dk, dvv, cq_t, sq_t, ck_t, sk_t):
        dks = dk[:, :LANES]
        for hh in range(1, MLA_HEADS):
            dks = dks + dk[:, hh * LANES:(hh + 1) * LANES]
        d_kr = _rope_t(pltpu.roll(dks, LANES - MLA_NOPE, axis=1), ck_t, sk_t, MLA_ROPE // 2)
        return _rope_t(dq, cq_t, sq_t, MLA_ROPE // 2), jnp.concatenate([dk, dvv], axis=1), d_kr

    d_q_lat, d_kv_lat, d_krope = _rowwise(
        f"l{l}_b_mla_prep", mla_prep_bwd,
        [_ri(dq_mla, T), _ri(dk_mla, T), _ri(dv_mla, T), _ri(tb["c_mla_q"], T), _ri(tb["s_mla_q"], T), _ri(tb["c_mla_k"], T), _ri(tb["s_mla_k"], T)],
        [_ro(S, MLA_QW, BF16, T), _ro(S, MLA_KVW, BF16, T), _ro(S, LANES, BF16, T)], n_tiles=nt)
    d_cqn, = _mm_nt(f"l{l}_b_mm_uq", d_q_lat, W["w_uq"], [F32], tm=1024, tn=MLA_Q_LORA)
    G["w_uq"] = _mm_tn(f"l{l}_g_uq", sv["cqn"], d_q_lat, tm=MLA_Q_LORA, tn=MLA_QW, ts=512)
    d_ckvn, = _mm_nt(f"l{l}_b_mm_ukv", d_kv_lat, W["w_ukv"], [F32], tm=1024, tn=MLA_KV_LORA)
    G["w_ukv"] = _mm_tn(f"l{l}_g_ukv", sv["ckvn"], d_kv_lat, tm=MLA_KV_LORA, tn=MLA_KVW, ts=512)

    def mix_prep_bwd(cq, ckv, gq, gkv, dcqn, dckvn, dqs, dks, dvs, cq_t, sq_t, ck_t, sk_t):
        d_cq, dgq = _rms_bwd(cq, gq, dcqn)
        d_ckv, dgkv = _rms_bwd(ckv, gkv, dckvn)
        dk2 = jnp.concatenate([dks[0], dks[1]], axis=1)
        dv2 = jnp.concatenate([dvs[0], dvs[1]], axis=1)
        return (d_cq, d_ckv, _rope_t(dqs, cq_t, sq_t, SWA_HEAD_DIM // 2), _rope_t(dk2, ck_t, sk_t, SWA_HEAD_DIM // 2), dv2, dgq, dgkv)

    kv3 = lambda a: (a.reshape(SWA_KV_HEADS, S, SWA_HEAD_DIM), (SWA_KV_HEADS, T, SWA_HEAD_DIM), lambda i: (0, i, 0))
    d_cq, d_ckv, d_qswa, d_kswa, d_vswa, G["g_q_lat"], G["g_kv_lat"] = _rowwise(
        f"l{l}_b_mix_prep", mix_prep_bwd,
        [_ri(sv["proj"], T, 256, _cb("c_q")), _ri(sv["proj"], T, 128, _cb("c_kv")), _bi(P["g_q_lat"]), _bi(P["g_kv_lat"]), _ri(d_cqn, T), _ri(d_ckvn, T),
         _ri(dq_swa, T), kv3(dk_swa), kv3(dv_swa), _ri(tb["c_swa_q"], T), _ri(tb["s_swa_q"], T), _ri(tb["c_swa_k"], T), _ri(tb["s_swa_k"], T)],
        [_ro(S, 256, BF16, T), _ro(S, 128, BF16, T), _ro(S, 512, BF16, T), _ro(S, 128, BF16, T), _ro(S, 128, BF16, T)], [256, 128], n_tiles=nt)
    pieces = dict(gates=d_gl, q_swa=d_qswa, q_sb=dq_sb, k_sb=dk_sb, v_sb=dv_sb, c_q=d_cq, c_kv=d_ckv, k_swa=d_kswa, v_swa=d_vswa, k_rope=d_krope)
    d_proj = jnp.concatenate([pieces[n].astype(BF16) for n in PERM_ORDER], axis=1)
    d_h, = _mm_nt(f"l{l}_b_mm_in", d_proj, W["w_in"], [F32], tm=512, tn=512)
    G["w_in"] = _mm_tn(f"l{l}_g_in", sv["h"], d_proj, tm=D_MODEL, tn=IN_WIDTH_P // 2, ts=512)
    dx, G["g_mix_pre"] = _rowwise(f"l{l}_b_pre1", pre_norm_bwd, [_ri(sv["x"], T), _bi(P["g_mix_pre"]), _ri(d_h, T), _ri(dx1, T)],
                                 [_ro(S, D_MODEL, F32, T)], [D_MODEL], n_tiles=nt)
    return dx, G, rode


def _pack_rows(vecs, rows):
    flat = jnp.concatenate([v.reshape(-1) for v in vecs])
    return jnp.pad(flat, (0, rows * PACK_COLS - flat.shape[0])).reshape(rows, PACK_COLS)


def kernel(x, positions, g_mix_pre, w_in, b_gate, g_q_lat, g_kv_lat, w_uq, w_ukv, swa_sinks, w_o_mla, w_o_swa, w_o_sb, w_out, g_mix_post, g_mlp_pre, w_up, w_down, g_mlp_post, loss_target, m_g_mix_pre, m_w_in, m_b_gate, m_g_q_lat, m_g_kv_lat, m_w_uq, m_w_ukv, m_swa_sinks, m_w_o_mla, m_w_o_swa, m_w_o_sb, m_w_out, m_g_mix_post, m_g_mlp_pre, m_w_up, m_w_down, m_g_mlp_post, v_g_mix_pre, v_w_in, v_b_gate, v_g_q_lat, v_g_kv_lat, v_w_uq, v_w_ukv, v_swa_sinks, v_w_o_mla, v_w_o_swa, v_w_o_sb, v_w_out, v_g_mix_post, v_g_mlp_pre, v_w_up, v_w_down, v_g_mlp_post):
    a = dict(locals())
    S = x.shape[1]
    depth = w_in.shape[0]
    T = min(ROW_TILE, S)
    xs = x.reshape(S, D_MODEL)
    tb = _tables(positions)

    perm = dict(w_in=_perm_w_in, w_uq=_perm_w_uq, w_ukv=_perm_w_ukv)

    def shards(l, names):
        return [a[n][l].astype(BF16) for n in names]

    def whole(names, gathered):
        return {n: perm.get(n, lambda t: t)(_from_shards(n, g, (1,) + a[n].shape[1:])[0]) for n, g in zip(names, gathered)}

    early = whole(W_EARLY, _exchange("gather_weights", shards(0, W_EARLY), per_peer=False))
    layers, saved = [], []
    h = xs
    for l in range(depth):
        P = {n: a[n][l].reshape(1, -1) for n in SMALL if n != "swa_sinks"}
        P["swa_sinks"] = a["swa_sinks"][l]
        srcs = shards(l, W_LATE) + (shards(l + 1, W_EARLY) if l + 1 < depth else [])
        h, sv, W, rode = _layer_fwd(l, h, early, P, tb, T, (srcs, False), lambda r: whole(W_LATE, r[:len(W_LATE)]))
        early = whole(W_EARLY, rode[len(W_LATE):])
        layers.append((W, P))
        saved.append(sv)

    def loss_head(yv, tv):
        err = yv - tv
        part = 0.5 * jnp.sum(jnp.mean(err * err, axis=1, keepdims=True), axis=0, keepdims=True)
        return err * (1.0 / D_MODEL), jnp.broadcast_to(part, (1, LANES))

    dh, loss_row = _rowwise("loss_head", loss_head, [_ri(h, T), _ri(loss_target.reshape(S, D_MODEL), T)], [_ro(S, D_MODEL, F32, T)], [LANES],
                            n_tiles=S // T)
    loss = lax.psum(loss_row[0, 0], ("x", "y", "c"))

    unperm = dict(w_in=_unperm_w_in, w_uq=_unperm_w_uq, w_ukv=_unperm_w_ukv)

    def to_send(n, g):
        return _to_shards(n, unperm.get(n, lambda t: t)(g).astype(BF16)[None])

    grads, recv, pending = [None] * depth, {n: [None] * depth for n in BIG}, []
    for l in reversed(range(depth)):
        W, P = layers[l]
        dh, grads[l], rode = _layer_bwd(l, dh, saved[l], W, P, tb, T, pending, to_send)
        for n, ly, r in rode:
            recv[n][ly] = r
        pending = [(n, l, to_send(n, grads[l][n])) for n in W_EARLY]
    for (n, ly, _), r in zip(pending, _exchange("scatter_grads", [s for _, _, s in pending], per_peer=True)):
        recv[n][ly] = r
    grad_x = dh.reshape(x.shape)

    out = {}
    for n in BIG:
        shp = a[n].shape
        rows, cols = shp[0] * shp[1], shp[2]
        parts = jnp.concatenate(recv[n], axis=1)
        res = _adamw("adamw_" + n, parts, a[n].reshape(rows, cols), a["m_" + n].reshape(rows, cols), a["v_" + n].reshape(rows, cols),
                     tile=min(256, rows))
        out[n] = [r.reshape(shp) for r in res]

    small_total = sum(a[n].size for n in SMALL)
    small_rows = -(-small_total // (8 * PACK_COLS)) * 8
    sg = _pack_rows([jnp.stack([grads[l][n].reshape(-1) for l in range(depth)]) for n in SMALL], small_rows)
    sg_all, = _exchange("gather_small_grads", [sg], per_peer=False)
    res = _adamw("adamw_small", sg_all, _pack_rows([a[n] for n in SMALL], small_rows), _pack_rows([a["m_" + n] for n in SMALL], small_rows),
                 _pack_rows([a["v_" + n] for n in SMALL], small_rows), tile=small_rows)
    off = 0
    for n in SMALL:
        cnt = a[n].size
        out[n] = [r.reshape(-1)[off:off + cnt].reshape(a[n].shape) for r in res]
        off += cnt

    order = ("g_mix_pre", "w_in", "b_gate", "g_q_lat", "g_kv_lat", "w_uq", "w_ukv", "swa_sinks", "w_o_mla", "w_o_swa", "w_o_sb", "w_out",
             "g_mix_post", "g_mlp_pre", "w_up", "w_down", "g_mlp_post")
    return (loss, grad_x, *[out[n][0] for n in order], *[out[n][1] for n in order], *[out[n][2] for n in order], *[out[n][3] for n in order])
```

```python
import functools

import jax
import jax.numpy as jnp
from jax import lax
from jax.experimental import pallas as pl
from jax.experimental.pallas import tpu as pltpu

F32, BF16 = jnp.float32, jnp.bfloat16

D_MODEL = 1024
DEPTH = 4
MLA_HEADS, MLA_Q_LORA, MLA_KV_LORA, MLA_NOPE, MLA_ROPE, MLA_V = 8, 256, 128, 64, 32, 64
SWA_HEADS, SWA_KV_HEADS, SWA_HEAD_DIM, SWA_WINDOW = 8, 2, 64, 128
SB_HEADS, SB_HEAD_DIM = 8, 64
D_FF = 4 * D_MODEL
ROPE_THETA = 10000.0
EPS = 1e-6
N_DEV = 8
ADAM_LR, ADAM_B1, ADAM_B2, ADAM_EPS, ADAM_WD, ADAM_STEP = 0.001, 0.9, 0.999, 1e-08, 0.01, 10

LANES = 128
VMEM_LIMIT_MAX = 60 * 1024 * 1024
VMEM_LIMIT_MIN = 32 * 1024 * 1024

ORIG_COLS = dict(c_q=(0, 256), c_kv=(256, 128), k_rope=(384, 32), q_swa=(416, 512), k_swa=(928, 128), v_swa=(1056, 128),
                 q_sb=(1184, 512), k_sb=(1696, 512), v_sb=(2208, 512), gates=(2720, 3072))
IN_WIDTH = 5792
PERM_ORDER = ("gates", "q_swa", "q_sb", "k_sb", "v_sb", "c_q", "c_kv", "k_swa", "v_swa", "k_rope")
PERM_WIDTH = dict(gates=3072, q_swa=512, q_sb=512, k_sb=512, v_sb=512, c_q=256, c_kv=128, k_swa=128, v_swa=128, k_rope=128)
PERM_OFF = {}
_o = 0
for _n in PERM_ORDER:
    PERM_OFF[_n] = _o
    _o += PERM_WIDTH[_n]
IN_WIDTH_P = _o
MLA_QW = MLA_HEADS * LANES
MLA_KVW = MLA_HEADS * LANES + MLA_HEADS * MLA_V

ROW_TILE = 512
NT = (((1,), (1,)), ((), ()))
TN = (((0,), (0,)), ((), ()))
NEG = -1e30


def _cparams(sem, block_bytes):
    limit = int(min(VMEM_LIMIT_MAX, max(VMEM_LIMIT_MIN, 2 * block_bytes + (16 << 20))))
    return pltpu.CompilerParams(dimension_semantics=sem, vmem_limit_bytes=limit)


def _nbytes(shape, dtype):
    n = 1
    for s in shape:
        n *= s
    return n * jnp.dtype(dtype).itemsize


def _ri(arr, tile, width=None, cb=0):
    width = arr.shape[1] if width is None else width
    return (arr, (tile, width), lambda i, cb=cb: (i, cb))


def _bi(arr):
    return (arr, arr.shape, lambda i: (0, 0))


def _ro(rows, width, dtype, tile):
    return ((rows, width), dtype, (tile, width), lambda i: (i, 0))


def _rowwise(name, fn, ins, outs, reds=(), *, n_tiles):
    n_in, n_out = len(ins), len(outs)

    def body(*refs):
        vals = fn(*[r[...] for r in refs[:n_in]])
        for r, v in zip(refs[n_in:n_in + n_out], vals[:n_out]):
            if isinstance(v, (list, tuple)):
                for j, vj in enumerate(v):
                    r[j] = vj.astype(r.dtype)
            else:
                r[...] = v.astype(r.dtype)
        if reds:
            @pl.when(pl.program_id(0) == 0)
            def _():
                for r in refs[n_in + n_out:]:
                    r[...] = jnp.zeros_like(r)
            for r, v in zip(refs[n_in + n_out:], vals[n_out:]):
                r[...] += v

    block_bytes = sum(_nbytes(b, a.dtype) for a, b, _ in ins) + sum(_nbytes(b, d) for _, d, b, _ in outs)
    res = pl.pallas_call(
        body, name=name, grid=(n_tiles,),
        in_specs=[pl.BlockSpec(b, m) for _, b, m in ins],
        out_specs=[pl.BlockSpec(b, m) for _, _, b, m in outs] + [pl.BlockSpec((1, w), lambda i: (0, 0)) for w in reds],
        out_shape=[jax.ShapeDtypeStruct(s, d) for s, d, _, _ in outs] + [jax.ShapeDtypeStruct((1, w), F32) for w in reds],
        compiler_params=_cparams(("arbitrary",) if reds else ("parallel",), block_bytes),
    )(*[a for a, _, _ in ins])
    return res


def _rms(x, g):
    r = lax.rsqrt(jnp.mean(x * x, axis=1, keepdims=True) + EPS)
    return x * r * g


def _rms_bwd(x, g, dy):
    r = lax.rsqrt(jnp.mean(x * x, axis=1, keepdims=True) + EPS)
    xn = x * r
    dxn = dy * g
    dx = r * (dxn - xn * jnp.mean(dxn * xn, axis=1, keepdims=True))
    return dx, jnp.sum(dy * xn, axis=0, keepdims=True)


def _swap_halves(x, half):
    n = x.shape[1]
    lane = lax.broadcasted_iota(jnp.int32, x.shape, 1)
    first = (lane % (2 * half)) < half
    return jnp.where(first, pltpu.roll(x, n - half, axis=1), pltpu.roll(x, half, axis=1))


def _rope(x, c, sg, half):
    return x * c + _swap_halves(x, half) * sg


def _rope_t(dy, c, sg, half):
    return dy * c - _swap_halves(dy, half) * sg


def _mm_nn(name, a, b, outs, *, tm, tn, extras=(), epilogue=None):
    M, K = a.shape
    N = b.shape[1]
    tm = min(tm, M)
    n_e = len(extras)

    def body(*refs):
        a_ref, b_ref = refs[:2]
        acc = jnp.dot(a_ref[...].astype(BF16), b_ref[...].astype(BF16), preferred_element_type=F32)
        vals = (acc,) * len(outs) if epilogue is None else epilogue(acc, *[r[...] for r in refs[2:2 + n_e]])
        for r, v in zip(refs[2 + n_e:], vals):
            r[...] = v.astype(r.dtype)

    block_bytes = (_nbytes((tm, K), a.dtype) + _nbytes((K, tn), b.dtype) + sum(_nbytes((tm, tn), e.dtype) for e in extras)
                   + sum(_nbytes((tm, tn), d) for d in outs) + _nbytes((tm, tn), F32))
    return pl.pallas_call(
        body, name=name, grid=(N // tn, M // tm),
        in_specs=[pl.BlockSpec((tm, K), lambda j, i: (i, 0)), pl.BlockSpec((K, tn), lambda j, i: (0, j))]
        + [pl.BlockSpec((tm, tn), lambda j, i: (i, j)) for _ in extras],
        out_specs=[pl.BlockSpec((tm, tn), lambda j, i: (i, j)) for _ in outs],
        out_shape=[jax.ShapeDtypeStruct((M, N), d) for d in outs],
        compiler_params=_cparams(("parallel", "parallel"), block_bytes),
    )(a, b, *extras)


def _mm_nt(name, a, b, outs, *, tm, tn, extras=(), epilogue=None):
    M, N = a.shape
    K = b.shape[0]
    tm = min(tm, M)
    n_e = len(extras)

    def body(*refs):
        a_ref, b_ref = refs[:2]
        acc = lax.dot_general(a_ref[...].astype(BF16), b_ref[...].astype(BF16), NT, preferred_element_type=F32)
        vals = (acc,) * len(outs) if epilogue is None else epilogue(acc, *[r[...] for r in refs[2:2 + n_e]])
        for r, v in zip(refs[2 + n_e:], vals):
            r[...] = v.astype(r.dtype)

    block_bytes = (_nbytes((tm, N), a.dtype) + _nbytes((tn, N), b.dtype) + sum(_nbytes((tm, tn), e.dtype) for e in extras)
                   + sum(_nbytes((tm, tn), d) for d in outs) + _nbytes((tm, tn), F32))
    return pl.pallas_call(
        body, name=name, grid=(K // tn, M // tm),
        in_specs=[pl.BlockSpec((tm, N), lambda j, i: (i, 0)), pl.BlockSpec((tn, N), lambda j, i: (j, 0))]
        + [pl.BlockSpec((tm, tn), lambda j, i: (i, j)) for _ in extras],
        out_specs=[pl.BlockSpec((tm, tn), lambda j, i: (i, j)) for _ in outs],
        out_shape=[jax.ShapeDtypeStruct((M, K), d) for d in outs],
        compiler_params=_cparams(("parallel", "parallel"), block_bytes),
    )(a, b, *extras)


def _mm_tn(name, a, b, *, tm, tn, ts):
    S, K = a.shape
    N = b.shape[1]
    ts = min(ts, S)

    def body(a_ref, b_ref, o_ref):
        @pl.when(pl.program_id(2) == 0)
        def _():
            o_ref[...] = jnp.zeros_like(o_ref)
        o_ref[...] += lax.dot_general(a_ref[...].astype(BF16), b_ref[...].astype(BF16), TN, preferred_element_type=F32)

    block_bytes = _nbytes((ts, tm), a.dtype) + _nbytes((ts, tn), b.dtype) + 2 * _nbytes((tm, tn), F32)
    return pl.pallas_call(
        body, name=name, grid=(K // tm, N // tn, S // ts),
        in_specs=[pl.BlockSpec((ts, tm), lambda i, j, s: (s, i)), pl.BlockSpec((ts, tn), lambda i, j, s: (s, j))],
        out_specs=pl.BlockSpec((tm, tn), lambda i, j, s: (i, j)),
        out_shape=jax.ShapeDtypeStruct((K, N), F32),
        compiler_params=_cparams(("parallel", "parallel", "arbitrary"), block_bytes),
    )(a, b)


def _lane_pack(cols, rows):
    lane = lax.broadcasted_iota(jnp.int32, (rows, LANES), 1)
    val = jnp.zeros((rows, LANES), F32)
    for h, c in enumerate(cols):
        val = jnp.where(lane == h, c, val)
    return val


def _mask(kb, row, tk):
    return kb * tk + lax.broadcasted_iota(jnp.int32, (1, tk), 1) <= row


def _kb_range(i, tq, tk):
    return (i * tq) // tk, ((i + 1) * tq + tk - 1) // tk


def _softmax_fwd(name, q, k, v, *, groups, heads, q_spec, k_spec, v_spec, o_width, tq, tk, ride=None):
    S = q.shape[0]
    nq = S // tq
    nh = len(heads)
    dv = heads[0][2].stop - heads[0][2].start

    def body(q_ref, k_ref, v_ref, o_ref, lse_ref):
        i = pl.program_id(1)
        q_all = q_ref[...]
        row = i * tq + lax.broadcasted_iota(jnp.int32, (tq, 1), 0)
        mid, hi = _kb_range(i, tq, tk)
        qhs = [q_all[:, qs] for qs, _, _ in heads]

        def blk(kb, carry, masked):
            r0 = pl.multiple_of(kb * tk, tk)
            k_all = k_ref[pl.ds(r0, tk), :]
            v_all = v_ref[pl.ds(r0, tk), :]
            if masked:
                ok = _mask(kb, row, tk)
            ones = jnp.ones((tk, dv), BF16)
            new = []
            for qh, (_, ks, vs), (m, acc) in zip(qhs, heads, carry):
                s = lax.dot_general(qh, k_all[:, ks], NT, preferred_element_type=F32)
                if masked:
                    s = jnp.where(ok, s, NEG)
                m_new = jnp.maximum(m, jnp.max(s, axis=1, keepdims=True))
                p = jnp.exp(s - m_new)
                if masked:
                    p = jnp.where(ok, p, 0.0)
                pv = jnp.dot(p.astype(BF16), jnp.concatenate([v_all[:, vs], ones], axis=1), preferred_element_type=F32)
                new.append((m_new, jnp.exp(m - m_new) * acc + pv))
            return tuple(new)

        carry = tuple((jnp.full((tq, 1), NEG, F32), jnp.zeros((tq, 2 * dv), F32)) for _ in range(nh))
        carry = lax.fori_loop(0, mid, functools.partial(blk, masked=False), carry)
        carry = lax.fori_loop(mid, hi, functools.partial(blk, masked=True), carry)
        o_ref[...] = jnp.concatenate([acc[:, :dv] / acc[:, dv:] for _, acc in carry], axis=1).astype(o_ref.dtype)
        lse_ref[0] = _lane_pack([m + jnp.log(acc[:, dv:dv + 1]) for m, acc in carry], tq)

    wo = nh * dv
    block_bytes = _nbytes(q_spec[0], q.dtype) + _nbytes(k_spec[0], k.dtype) + _nbytes(v_spec[0], v.dtype) + 4 * tq * (wo + LANES)
    return _pallas(
        body, name=name, grid=(groups, nq), in_specs=[pl.BlockSpec(*q_spec), pl.BlockSpec(*k_spec), pl.BlockSpec(*v_spec)],
        out_specs=[pl.BlockSpec((tq, wo), lambda g, i: (i, g)), pl.BlockSpec((1, tq, LANES), lambda g, i: (g, i, 0))],
        out_shape=[jax.ShapeDtypeStruct((S, o_width), BF16), jax.ShapeDtypeStruct((groups, S, LANES), F32)],
        args=[q, k, v], sem=("parallel", "arbitrary"), block_bytes=block_bytes, ride=ride)


def _softmax_bwd(name, q, k, v, o, do, lse, *, groups, heads, q_spec, k_spec, v_spec, tq, tk, scale, ride=None):
    S = q.shape[0]
    nq, nkb = S // tq, S // tk
    nh = len(heads)
    dv = heads[0][2].stop - heads[0][2].start
    wo = nh * dv
    wk, wv = k_spec[0][1], v_spec[0][1]

    def body(q_ref, k_ref, v_ref, o_ref, do_ref, lse_ref, dq_ref, dkt_ref, dvt_ref):
        i = pl.program_id(1)

        @pl.when(i == 0)
        def _():
            dkt_ref[...] = jnp.zeros_like(dkt_ref)
            dvt_ref[...] = jnp.zeros_like(dvt_ref)

        q_all = q_ref[...]
        o_all = o_ref[...].astype(F32)
        do_all = do_ref[...].astype(F32)
        lse_all = lse_ref[0]
        row = i * tq + lax.broadcasted_iota(jnp.int32, (tq, 1), 0)
        mid, hi = _kb_range(i, tq, tk)
        per_head = []
        for h, (qs, ks, vs) in enumerate(heads):
            osl = slice(h * dv, (h + 1) * dv)
            doh = do_all[:, osl]
            delta = jnp.sum(doh * o_all[:, osl], axis=1, keepdims=True)
            qh = q_all[:, qs]
            per_head.append((qh, qh.astype(F32).T.astype(BF16), doh.astype(BF16), doh.T.astype(BF16), delta, lse_all[:, h:h + 1], ks, vs))

        def blk(kb, dqs, masked):
            r0 = pl.multiple_of(kb * tk, tk)
            k_all = k_ref[pl.ds(r0, tk), :]
            v_all = v_ref[pl.ds(r0, tk), :]
            if masked:
                ok = _mask(kb, row, tk)
            dkt, dvt, new_dqs = [], [], []
            for (qh, qt, doh, dot_, delta, lse_h, ks, vs), dq in zip(per_head, dqs):
                kk, vv = k_all[:, ks], v_all[:, vs]
                s = lax.dot_general(qh, kk, NT, preferred_element_type=F32)
                p = jnp.exp(s - lse_h)
                if masked:
                    p = jnp.where(ok, p, 0.0)
                dp = lax.dot_general(doh, vv, NT, preferred_element_type=F32)
                ds = (p * (dp - delta)).astype(BF16)
                new_dqs.append(dq + jnp.dot(ds, kk, preferred_element_type=F32))
                dkt.append(jnp.dot(qt, ds, preferred_element_type=F32))
                dvt.append(jnp.dot(dot_, p.astype(BF16), preferred_element_type=F32))
            dkt_ref[kb] += jnp.concatenate(dkt, axis=0)
            dvt_ref[kb] += jnp.concatenate(dvt, axis=0)
            return tuple(new_dqs)

        dqs = tuple(jnp.zeros((tq, qs.stop - qs.start), F32) for qs, _, _ in heads)
        dqs = lax.fori_loop(0, mid, functools.partial(blk, masked=False), dqs)
        dqs = lax.fori_loop(mid, hi, functools.partial(blk, masked=True), dqs)
        dq_ref[...] = jnp.concatenate([dq * scale for dq in dqs], axis=1)

    in_specs = [pl.BlockSpec(*q_spec), pl.BlockSpec(*k_spec), pl.BlockSpec(*v_spec),
                pl.BlockSpec((tq, wo), lambda g, i: (i, g)), pl.BlockSpec((tq, wo), lambda g, i: (i, g)),
                pl.BlockSpec((1, tq, LANES), lambda g, i: (g, i, 0))]
    args = [q, k, v, o, do, lse]
    out_specs = [pl.BlockSpec(*q_spec), pl.BlockSpec((nkb, wk, tk), lambda g, i: (0, g, 0)), pl.BlockSpec((nkb, wv, tk), lambda g, i: (0, g, 0))]
    out_shape = [jax.ShapeDtypeStruct(q.shape, F32), jax.ShapeDtypeStruct((nkb, k.shape[1], tk), F32),
                 jax.ShapeDtypeStruct((nkb, v.shape[1], tk), F32)]
    block_bytes = (_nbytes(q_spec[0], q.dtype) + _nbytes(k_spec[0], k.dtype) + _nbytes(v_spec[0], v.dtype) + 6 * tq * wo + 4 * tq * LANES
                   + _nbytes(q_spec[0], F32) + _nbytes(k_spec[0], F32) + _nbytes(v_spec[0], F32))
    return _pallas(body, name=name, grid=(groups, nq), in_specs=in_specs, out_specs=out_specs, out_shape=out_shape, args=args,
                   sem=("parallel", "arbitrary"), block_bytes=block_bytes, ride=ride)


def _swa_window(i, tq, row):
    start = pl.multiple_of(jnp.maximum(i * tq - SWA_WINDOW, 0), SWA_WINDOW)
    col = start + lax.broadcasted_iota(jnp.int32, (1, tq + SWA_WINDOW), 1)
    return start, (col <= row) & ((row - col) < SWA_WINDOW)


def _swa_fwd(name, q, k, v, sinks, *, tq):
    S = q.shape[0]
    hd, span = SWA_HEAD_DIM, tq + SWA_WINDOW
    scale = hd ** -0.5

    def body(sink_ref, q_ref, k_ref, v_ref, o_ref, lse_ref):
        g, i = pl.program_id(0), pl.program_id(1)
        row = i * tq + lax.broadcasted_iota(jnp.int32, (tq, 1), 0)
        start, ok = _swa_window(i, tq, row)
        kk, vv = k_ref[pl.ds(start, span), :], v_ref[pl.ds(start, span), :]
        q_all = q_ref[...]
        outs, lses = [], []
        for h in range(SWA_G):
            s = lax.dot_general(q_all[:, h * hd:(h + 1) * hd] * scale, kk, NT, preferred_element_type=F32)
            s = jnp.where(ok, s, NEG)
            sink = sink_ref[g * SWA_G + h]
            m = jnp.maximum(jnp.max(s, axis=1, keepdims=True), sink)
            p = jnp.exp(s - m)
            l = jnp.sum(p, axis=1, keepdims=True) + jnp.exp(sink - m)
            outs.append(jnp.dot(p.astype(BF16), vv, preferred_element_type=F32) / l)
            lses.append(m + jnp.log(l))
        o_ref[...] = jnp.concatenate(outs, axis=1).astype(o_ref.dtype)
        lse_ref[0] = _lane_pack(lses, tq)

    wq = SWA_G * hd
    block_bytes = 2 * tq * wq * 2 + 2 * 2 * S * hd + 4 * tq * LANES
    return pl.pallas_call(
        body, name=name, grid=(SWA_KV_HEADS, S // tq),
        in_specs=[pl.BlockSpec(memory_space=pltpu.SMEM), pl.BlockSpec((tq, wq), lambda g, i: (i, g)),
                  pl.BlockSpec((S, hd), lambda g, i: (g, 0)), pl.BlockSpec((S, hd), lambda g, i: (g, 0))],
        out_specs=[pl.BlockSpec((tq, wq), lambda g, i: (i, g)), pl.BlockSpec((1, tq, LANES), lambda g, i: (g, i, 0))],
        out_shape=[jax.ShapeDtypeStruct((S, SWA_HEADS * hd), BF16), jax.ShapeDtypeStruct((SWA_KV_HEADS, S, LANES), F32)],
        compiler_params=_cparams(("parallel", "parallel"), block_bytes),
    )(sinks, q, k, v)


def _swa_bwd(name, q, k, v, o, do, lse, sinks, *, tq):
    S = q.shape[0]
    hd, span = SWA_HEAD_DIM, tq + SWA_WINDOW
    scale = hd ** -0.5
    nblk = S // SWA_WINDOW

    def body(sink_ref, q_ref, k_ref, v_ref, o_ref, do_ref, lse_ref, dq_ref, dkt_ref, dvt_ref, dsink_ref):
        g, i = pl.program_id(0), pl.program_id(1)

        @pl.when(i == 0)
        def _():
            dkt_ref[...] = jnp.zeros_like(dkt_ref)
            dvt_ref[...] = jnp.zeros_like(dvt_ref)
            dsink_ref[...] = jnp.zeros_like(dsink_ref)

        row = i * tq + lax.broadcasted_iota(jnp.int32, (tq, 1), 0)
        start, ok = _swa_window(i, tq, row)
        kk, vv = k_ref[pl.ds(start, span), :], v_ref[pl.ds(start, span), :]
        q_all, o_all, do_all, lse_all = q_ref[...], o_ref[...].astype(F32), do_ref[...].astype(F32), lse_ref[0]
        dqs, qts, dots, dss, ps = [], [], [], [], []
        for h in range(SWA_G):
            sl = slice(h * hd, (h + 1) * hd)
            qh = q_all[:, sl] * scale
            doh = do_all[:, sl]
            delta = jnp.sum(doh * o_all[:, sl], axis=1, keepdims=True)
            lse_h = lse_all[:, h:h + 1]
            qts.append(qh.astype(F32).T.astype(BF16))
            dots.append(doh.T.astype(BF16))
            doh = doh.astype(BF16)
            s = jnp.where(ok, lax.dot_general(qh, kk, NT, preferred_element_type=F32), NEG)
            p = jnp.exp(s - lse_h)
            ds = (p * (lax.dot_general(doh, vv, NT, preferred_element_type=F32) - delta)).astype(BF16)
            dqs.append(jnp.dot(ds, kk, preferred_element_type=F32) * scale)
            dss.append(ds)
            ps.append(p.astype(BF16))
            p_sink = jnp.exp(sink_ref[g * SWA_G + h] - lse_h)
            dsink_ref[0, h:h + 1, :] += jnp.broadcast_to(-jnp.sum(p_sink * delta, axis=0, keepdims=True), (1, LANES))
        dq_ref[...] = jnp.concatenate(dqs, axis=1)
        dkt = jnp.dot(jnp.concatenate(qts, axis=1), jnp.concatenate(dss, axis=0), preferred_element_type=F32)
        dvt = jnp.dot(jnp.concatenate(dots, axis=1), jnp.concatenate(ps, axis=0), preferred_element_type=F32)
        for j in range(span // SWA_WINDOW):
            cols = slice(j * SWA_WINDOW, (j + 1) * SWA_WINDOW)
            dkt_ref[start // SWA_WINDOW + j] += dkt[:, cols]
            dvt_ref[start // SWA_WINDOW + j] += dvt[:, cols]

    wq = SWA_G * hd
    qb = pl.BlockSpec((tq, wq), lambda g, i: (i, g))
    kb = pl.BlockSpec((S, hd), lambda g, i: (g, 0))
    tb = pl.BlockSpec((nblk, hd, SWA_WINDOW), lambda g, i: (g, 0, 0))
    t_shape = jax.ShapeDtypeStruct((SWA_KV_HEADS * nblk, hd, SWA_WINDOW), F32)
    block_bytes = 2 * tq * wq * (2 + 2 + 4 + 4) + 2 * S * hd * (2 + 2 + 4 + 4) + 4 * tq * LANES
    return pl.pallas_call(
        body, name=name, grid=(SWA_KV_HEADS, S // tq),
        in_specs=[pl.BlockSpec(memory_space=pltpu.SMEM), qb, kb, kb, qb, qb, pl.BlockSpec((1, tq, LANES), lambda g, i: (g, i, 0))],
        out_specs=[qb, tb, tb, pl.BlockSpec((1, 8, LANES), lambda g, i: (g, 0, 0))],
        out_shape=[jax.ShapeDtypeStruct(q.shape, F32), t_shape, t_shape, jax.ShapeDtypeStruct((SWA_KV_HEADS, 8, LANES), F32)],
        compiler_params=_cparams(("parallel", "arbitrary"), block_bytes),
    )(sinks, q, k, v, o, do, lse)


def _split_dot(x, u2):
    hi = x.astype(BF16)
    lo = (x - hi.astype(F32)).astype(BF16)
    return jnp.dot(jnp.concatenate([hi, lo], axis=1), u2, preferred_element_type=F32)


def _suffix_ones():
    r = lax.broadcasted_iota(jnp.int32, (2 * LANES, 2 * LANES), 0) % LANES
    c = lax.broadcasted_iota(jnp.int32, (2 * LANES, 2 * LANES), 1)
    return ((r > c) | (c >= LANES)).astype(BF16)


def _suffix_scan(x, uo, run):
    nc = x.shape[1] // LANES
    out = [None] * nc
    for c in reversed(range(nc)):
        st = _split_dot(x[:, c * LANES:(c + 1) * LANES], uo)
        out[c] = st[:, :LANES] + run
        run = run + st[:, LANES:]
    return (out[0] if nc == 1 else jnp.concatenate(out, axis=1)), run


SB_DEAD_LOG = -110.0


def _sb_logits(qh, kk, r0, row, masked):
    z = lax.dot_general(qh, kk, NT, preferred_element_type=F32)
    nz = -z
    l = jnp.minimum(nz, 0.0) - jnp.log(1.0 + jnp.exp(jnp.minimum(z, nz)))
    ok = None
    if masked:
        col = r0 + lax.broadcasted_iota(jnp.int32, (1, kk.shape[0]), 1)
        ok = col < row
        l = jnp.where(ok, l, 0.0)
    return z, l, ok


def _sb_walk(blk, carry, i, tq, ts):
    carry = blk(pl.multiple_of(i * tq, tq), tq, carry, True)

    def live(c):
        t, heads = c
        top = jnp.max(heads[0][0])
        for h in heads[1:]:
            top = jnp.maximum(top, jnp.max(h[0]))
        return jnp.logical_and(t < i * (tq // ts), top > SB_DEAD_LOG)

    def step(c):
        t, heads = c
        return t + 1, blk(pl.multiple_of(i * tq - (t + 1) * ts, ts), ts, heads, False)

    return lax.while_loop(live, step, (jnp.int32(0), carry))[1]


def _sb_fwd(name, pb, *, q_cb, k_cb, v_cb, tq, ts, scale):
    S = pb.shape[0]
    nq = S // tq
    hd = SB_HEAD_DIM
    groups = SB_HEADS * hd // LANES
    nh = LANES // hd

    def body(q_ref, k_ref, v_ref, o16_ref, o32_ref):
        i = pl.program_id(1)
        q_all = q_ref[...]
        row = i * tq + lax.broadcasted_iota(jnp.int32, (tq, 1), 0)
        uo = _suffix_ones()
        sls = [slice(h * hd, (h + 1) * hd) for h in range(nh)]
        qhs = [q_all[:, sl] * scale for sl in sls]

        def blk(r0, width, carry, masked):
            k_all = k_ref[pl.ds(r0, width), :]
            v_all = v_ref[pl.ds(r0, width), :]
            new = []
            for qh, sl, (run_l, acc) in zip(qhs, sls, carry):
                z, l, ok = _sb_logits(qh, k_all[:, sl], r0, row, masked)
                tail, run_l = _suffix_scan(l, uo, run_l)
                e = z + l + tail
                if masked:
                    e = jnp.where(ok, e, NEG)
                new.append((run_l, acc + jnp.dot(jnp.exp(e).astype(BF16), v_all[:, sl], preferred_element_type=F32)))
            return tuple(new)

        carry = tuple((jnp.zeros((tq, LANES), F32), jnp.zeros((tq, hd), F32)) for _ in range(nh))
        carry = _sb_walk(blk, carry, i, tq, ts)
        o = jnp.concatenate([acc for _, acc in carry], axis=1)
        o16_ref[...] = o.astype(BF16)
        o32_ref[...] = o

    block_bytes = 2 * tq * LANES + 2 * 2 * S * LANES + 6 * tq * LANES
    return pl.pallas_call(
        body, name=name, grid=(groups, nq),
        in_specs=[pl.BlockSpec((tq, LANES), lambda g, i: (i, q_cb + g)), pl.BlockSpec((S, LANES), lambda g, i: (0, k_cb + g)),
                  pl.BlockSpec((S, LANES), lambda g, i: (0, v_cb + g))],
        out_specs=[pl.BlockSpec((tq, LANES), lambda g, i: (i, g)), pl.BlockSpec((tq, LANES), lambda g, i: (i, g))],
        out_shape=[jax.ShapeDtypeStruct((S, groups * LANES), BF16), jax.ShapeDtypeStruct((S, groups * LANES), F32)],
        compiler_params=_cparams(("parallel", "arbitrary"), block_bytes),
    )(pb, pb, pb)


def _sb_bwd(name, pb, o32, do, *, q_cb, k_cb, v_cb, tq, ts, scale):
    S = pb.shape[0]
    nq = S // tq
    hd = SB_HEAD_DIM
    groups = SB_HEADS * hd // LANES
    nh = LANES // hd

    def body(q_ref, k_ref, v_ref, o_ref, do_ref, dq_ref, dkt_ref, dvt_ref):
        i = pl.program_id(1)

        @pl.when(i == 0)
        def _():
            dkt_ref[...] = jnp.zeros_like(dkt_ref)
            dvt_ref[...] = jnp.zeros_like(dvt_ref)

        q_all = q_ref[...]
        o_all = o_ref[...]
        do_all = do_ref[...].astype(F32)
        row = i * tq + lax.broadcasted_iota(jnp.int32, (tq, 1), 0)
        uo = _suffix_ones()
        per_head = []
        for h in range(nh):
            sl = slice(h * hd, (h + 1) * hd)
            doh = do_all[:, sl].astype(BF16)
            total = jnp.sum(doh.astype(F32) * o_all[:, sl], axis=1, keepdims=True)
            qh = q_all[:, sl] * scale
            per_head.append((qh, qh.astype(F32).T.astype(BF16), doh, doh.astype(F32).T.astype(BF16), jnp.broadcast_to(total, (tq, LANES)), sl))

        def blk(r0, width, carry, masked):
            k_all = k_ref[pl.ds(r0, width), :]
            v_all = v_ref[pl.ds(r0, width), :]
            new, dk_c, dv_c = [], [], []
            for (qh, qt, doh, dot_, total, sl), (run_l, run_g, dq) in zip(per_head, carry):
                kk, vv = k_all[:, sl], v_all[:, sl]
                z, l, ok = _sb_logits(qh, kk, r0, row, masked)
                tail, run_l = _suffix_scan(l, uo, run_l)
                e = z + l
                beta = jnp.exp(e)
                e = e + tail
                if masked:
                    e = jnp.where(ok, e, NEG)
                a = jnp.exp(e).astype(BF16).astype(F32)
                gr = lax.dot_general(doh, vv, NT, preferred_element_type=F32) * a
                right, run_g = _suffix_scan(gr, uo, run_g)
                nc = width // LANES
                prefix = (total if nc == 1 else jnp.tile(total, (1, nc))) - right
                dz = gr - beta * prefix
                if masked:
                    dz = jnp.where(ok, dz, 0.0)
                dzb = dz.astype(BF16)
                new.append((run_l, run_g, dq + jnp.dot(dzb, kk, preferred_element_type=F32)))
                dk_c.append(jnp.dot(qt, dzb, preferred_element_type=F32))
                dv_c.append(jnp.dot(dot_, a.astype(BF16), preferred_element_type=F32))
            dkt, dvt = jnp.concatenate(dk_c, axis=0), jnp.concatenate(dv_c, axis=0)
            for j in range(width // ts):
                dkt_ref[r0 // ts + j] += dkt[:, j * ts:(j + 1) * ts]
                dvt_ref[r0 // ts + j] += dvt[:, j * ts:(j + 1) * ts]
            return tuple(new)

        zc = jnp.zeros((tq, LANES), F32)
        carry = tuple((zc, zc, jnp.zeros((tq, hd), F32)) for _ in range(nh))
        carry = _sb_walk(blk, carry, i, tq, ts)
        dq_ref[...] = jnp.concatenate([c[2] * scale for c in carry], axis=1)

    W = groups * LANES
    block_bytes = 2 * tq * LANES + 2 * 2 * S * LANES + 3 * 4 * tq * LANES + 2 * 4 * S * LANES
    return pl.pallas_call(
        body, name=name, grid=(groups, nq),
        in_specs=[pl.BlockSpec((tq, LANES), lambda g, i: (i, q_cb + g)), pl.BlockSpec((S, LANES), lambda g, i: (0, k_cb + g)),
                  pl.BlockSpec((S, LANES), lambda g, i: (0, v_cb + g)), pl.BlockSpec((tq, LANES), lambda g, i: (i, g)),
                  pl.BlockSpec((tq, LANES), lambda g, i: (i, g))],
        out_specs=[pl.BlockSpec((tq, LANES), lambda g, i: (i, g)), pl.BlockSpec((S // ts, LANES, ts), lambda g, i: (0, g, 0)),
                   pl.BlockSpec((S // ts, LANES, ts), lambda g, i: (0, g, 0))],
        out_shape=[jax.ShapeDtypeStruct((S, W), F32)] + [jax.ShapeDtypeStruct((S // ts, W, ts), F32)] * 2,
        compiler_params=_cparams(("parallel", "arbitrary"), block_bytes),
    )(pb, pb, pb, o32, do)


def _exchange_copies(src_refs, out_refs, send_sems, recv_sems, local_sems, per_peer):
    x, y, c = lax.axis_index("x"), lax.axis_index("y"), lax.axis_index("c")
    me = 4 * x + 2 * y + c
    n = len(src_refs)

    def copy(j, k):
        px, py, pc = x ^ (k >> 2), y ^ ((k >> 1) & 1), c ^ (k & 1)
        s = src_refs[j].at[4 * px + 2 * py + pc] if per_peer else src_refs[j]
        return pltpu.make_async_remote_copy(
            src_ref=s, dst_ref=out_refs[j].at[me], send_sem=send_sems.at[j, k - 1], recv_sem=recv_sems.at[j, k - 1],
            device_id=(px, py, pc), device_id_type=pl.DeviceIdType.MESH)

    mine = [pltpu.make_async_copy(src_refs[j].at[me] if per_peer else src_refs[j], out_refs[j].at[me], local_sems.at[j]) for j in range(n)]
    return mine, [copy(j, k) for k in range(1, N_DEV) for j in range(n)]


def _exchange_start(*refs, per_peer):
    mine, copies = _exchange_copies(*refs, per_peer)
    for cp in mine + copies:
        cp.start()


def _exchange_wait(*refs, per_peer):
    mine, copies = _exchange_copies(*refs, per_peer)
    for cp in copies:
        cp.wait_recv()
    for cp in copies:
        cp.wait_send()
    for cp in mine:
        cp.wait()


def _exchange_specs(srcs, per_peer):
    n = len(srcs)
    out_shape = [jax.ShapeDtypeStruct((N_DEV,) + tuple(s.shape[1:] if per_peer else s.shape), s.dtype) for s in srcs]
    sems = [pltpu.SemaphoreType.DMA((n, N_DEV - 1)), pltpu.SemaphoreType.DMA((n, N_DEV - 1)), pltpu.SemaphoreType.DMA((n,))]
    return [pl.BlockSpec(memory_space=pltpu.HBM)] * n, out_shape, sems


def _exchange(name, srcs, per_peer):
    n = len(srcs)

    def body(*refs):
        parts = (refs[:n], refs[n:2 * n], *refs[2 * n:])
        _exchange_start(*parts, per_peer=per_peer)
        _exchange_wait(*parts, per_peer=per_peer)

    hbm, out_shape, sems = _exchange_specs(srcs, per_peer)
    return pl.pallas_call(body, name=name, in_specs=hbm, out_specs=hbm, out_shape=out_shape, scratch_shapes=sems)(*srcs)


def _pallas(body, *, name, grid, in_specs, out_specs, out_shape, args, sem, block_bytes, ride=None):
    if ride is None:
        return pl.pallas_call(body, name=name, grid=grid, in_specs=in_specs, out_specs=out_specs, out_shape=out_shape,
                              compiler_params=_cparams(sem, block_bytes))(*args), None
    srcs, per_peer = ride
    n, n_in, n_out = len(srcs), len(in_specs), len(out_specs)

    def riding(*refs):
        ins, xsrc = refs[:n_in], refs[n_in:n_in + n]
        outs, xout = refs[n_in + n:n_in + n + n_out], refs[n_in + n + n_out:n_in + 2 * n + n_out]
        parts = (xsrc, xout, *refs[n_in + 2 * n + n_out:])
        ids = [pl.program_id(d) for d in range(len(grid))]
        first = functools.reduce(jnp.logical_and, [i == 0 for i in ids])
        last = functools.reduce(jnp.logical_and, [i == g - 1 for i, g in zip(ids, grid)])
        pl.when(first)(functools.partial(_exchange_start, *parts, per_peer=per_peer))
        body(*ins, *outs)
        pl.when(last)(functools.partial(_exchange_wait, *parts, per_peer=per_peer))

    hbm, x_shape, sems = _exchange_specs(srcs, per_peer)
    res = pl.pallas_call(riding, name=name, grid=grid, in_specs=list(in_specs) + hbm, out_specs=list(out_specs) + hbm,
                         out_shape=list(out_shape) + x_shape, scratch_shapes=sems,
                         compiler_params=_cparams(("arbitrary",) * len(grid), block_bytes))(*args, *srcs)
    return res[:n_out], res[n_out:]


def _adamw(name, parts, w, m, v, *, tile):
    R, C = w.shape

    def body(p_ref, w_ref, m_ref, v_ref, g_ref, d_ref, nm_ref, nv_ref):
        g = p_ref[0].astype(F32)
        for d in range(1, N_DEV):
            g = g + p_ref[d].astype(F32)
        wv = w_ref[...]
        mm = ADAM_B1 * m_ref[...] + (1.0 - ADAM_B1) * g
        vv = ADAM_B2 * v_ref[...] + (1.0 - ADAM_B2) * jnp.square(g)
        m_hat = mm / (1.0 - ADAM_B1 ** ADAM_STEP)
        v_hat = vv / (1.0 - ADAM_B2 ** ADAM_STEP)
        g_ref[...] = g
        d_ref[...] = -ADAM_LR * (m_hat / (jnp.sqrt(v_hat) + ADAM_EPS) + ADAM_WD * wv)
        nm_ref[...] = mm
        nv_ref[...] = vv

    blk = pl.BlockSpec((tile, C), lambda i: (i, 0))
    block_bytes = N_DEV * _nbytes((tile, C), parts.dtype) + 7 * _nbytes((tile, C), F32)
    return pl.pallas_call(
        body, name=name, grid=(R // tile,),
        in_specs=[pl.BlockSpec((N_DEV, tile, C), lambda i: (0, i, 0)), blk, blk, blk],
        out_specs=[blk] * 4, out_shape=[jax.ShapeDtypeStruct((R, C), F32)] * 4,
        compiler_params=_cparams(("parallel",), block_bytes),
    )(parts, w, m, v)


def _perm_w_in(w):
    cols = [w[:, ORIG_COLS[n][0]:ORIG_COLS[n][0] + ORIG_COLS[n][1]] for n in PERM_ORDER]
    cols.append(jnp.zeros((w.shape[0], IN_WIDTH_P - IN_WIDTH), w.dtype))
    return jnp.concatenate(cols, axis=1)


def _unperm_w_in(wp):
    order = sorted(ORIG_COLS, key=lambda n: ORIG_COLS[n][0])
    return jnp.concatenate([wp[:, PERM_OFF[n]:PERM_OFF[n] + ORIG_COLS[n][1]] for n in order], axis=1)


def _perm_w_uq(w):
    w3 = w.reshape(w.shape[0], MLA_HEADS, MLA_NOPE + MLA_ROPE)
    return jnp.pad(w3, ((0, 0), (0, 0), (0, LANES - MLA_NOPE - MLA_ROPE))).reshape(w.shape[0], MLA_QW)


def _unperm_w_uq(wp):
    return wp.reshape(wp.shape[0], MLA_HEADS, LANES)[:, :, :MLA_NOPE + MLA_ROPE].reshape(wp.shape[0], -1)


def _perm_w_ukv(w):
    w3 = w.reshape(w.shape[0], MLA_HEADS, MLA_NOPE + MLA_V)
    kp = jnp.pad(w3[:, :, :MLA_NOPE], ((0, 0), (0, 0), (0, LANES - MLA_NOPE))).reshape(w.shape[0], MLA_HEADS * LANES)
    return jnp.concatenate([kp, w3[:, :, MLA_NOPE:].reshape(w.shape[0], MLA_HEADS * MLA_V)], axis=1)


def _unperm_w_ukv(wp):
    n = wp.shape[0]
    kp = wp[:, :MLA_HEADS * LANES].reshape(n, MLA_HEADS, LANES)[:, :, :MLA_NOPE]
    vp = wp[:, MLA_HEADS * LANES:].reshape(n, MLA_HEADS, MLA_V)
    return jnp.concatenate([kp, vp], axis=2).reshape(n, MLA_HEADS * (MLA_NOPE + MLA_V))


BIG = ("w_in", "w_uq", "w_ukv", "w_o_mla", "w_o_swa", "w_o_sb", "w_out", "w_up", "w_down")
W_EARLY = ("w_in", "w_uq", "w_ukv")
W_LATE = ("w_o_mla", "w_o_swa", "w_o_sb", "w_out", "w_up", "w_down")
ROW_SHARDED = ("w_out", "w_down")
SMALL = ("g_mix_pre", "b_gate", "g_q_lat", "g_kv_lat", "swa_sinks", "g_mix_post", "g_mlp_pre", "g_mlp_post")
PACK_COLS = 1024


def _to_shards(name, full):
    L, R, C = full.shape
    if name in ROW_SHARDED:
        return full.reshape(L, N_DEV, R // N_DEV, C).transpose(1, 0, 2, 3).reshape(N_DEV, L * R // N_DEV, C)
    return full.reshape(L, R, N_DEV, C // N_DEV).transpose(2, 0, 1, 3).reshape(N_DEV, L * R, C // N_DEV)


def _from_shards(name, gathered, shard_shape):
    L, r, c = shard_shape
    a = gathered.reshape(N_DEV, L, r, c)
    if name in ROW_SHARDED:
        return a.transpose(1, 0, 2, 3).reshape(L, N_DEV * r, c)
    return a.transpose(1, 2, 0, 3).reshape(L, r, N_DEV * c)


def _tables(positions):
    pos = positions.astype(F32).reshape(-1, 1)

    def cs(d):
        inv = 1.0 / (ROPE_THETA ** (jnp.arange(0, d, 2, dtype=F32) / d))
        ang = pos * inv
        c, s = jnp.cos(ang), jnp.sin(ang)
        return jnp.concatenate([c, c], axis=1), jnp.concatenate([-s, s], axis=1)

    c64, s64 = cs(SWA_HEAD_DIM)
    c32, s32 = cs(MLA_ROPE)
    n = pos.shape[0]
    one, zero = jnp.ones((n, MLA_NOPE), F32), jnp.zeros((n, MLA_NOPE), F32)
    pad0 = jnp.zeros((n, LANES - MLA_NOPE - MLA_ROPE), F32)
    cq = jnp.concatenate([one, c32, pad0], axis=1)
    sq = jnp.concatenate([zero, s32, pad0], axis=1)
    padk = jnp.zeros((n, LANES - MLA_ROPE), F32)
    return dict(
        c_swa_q=jnp.tile(c64, (1, SWA_HEADS)), s_swa_q=jnp.tile(s64, (1, SWA_HEADS)),
        c_swa_k=jnp.tile(c64, (1, SWA_KV_HEADS)), s_swa_k=jnp.tile(s64, (1, SWA_KV_HEADS)),
        c_mla_q=jnp.tile(cq, (1, MLA_HEADS)), s_mla_q=jnp.tile(sq, (1, MLA_HEADS)),
        c_mla_k=jnp.concatenate([c32, padk], axis=1), s_mla_k=jnp.concatenate([s32, padk], axis=1))


def _cb(name):
    return PERM_OFF[name] // PERM_WIDTH[name]


MLA_SCALE = (MLA_NOPE + MLA_ROPE) ** -0.5
MLA_FWD_TQ, MLA_FWD_TK = 1024, 1024
MLA_SPEC = dict(groups=MLA_HEADS // 2, tq=512, tk=512,
                heads=[(slice(h * LANES, (h + 1) * LANES), slice(h * LANES, (h + 1) * LANES), slice(h * MLA_V, (h + 1) * MLA_V)) for h in range(2)])
SWA_G = SWA_HEADS // SWA_KV_HEADS
SWA_TQ = 256
SB_SPEC = dict(tq=512, ts=256, scale=SB_HEAD_DIM ** -0.5, q_cb=PERM_OFF["q_sb"] // LANES, k_cb=PERM_OFF["k_sb"] // LANES,
               v_cb=PERM_OFF["v_sb"] // LANES)


def _mla_specs(S, tq):
    return dict(q_spec=((tq, 2 * LANES), lambda g, i: (i, g)), k_spec=((S, 2 * LANES), lambda g, i: (0, g)),
                v_spec=((S, 2 * MLA_V), lambda g, i: (0, g)))


def _layer_fwd(l, x, W, P, tb, T, ride, late):
    S = x.shape[0]
    nt = S // T
    h, = _rowwise(f"l{l}_norm_in", lambda xv, g: (_rms(xv, g),), [_ri(x, T), _bi(P["g_mix_pre"])], [_ro(S, D_MODEL, BF16, T)], n_tiles=nt)
    proj, proj16 = _mm_nn(f"l{l}_mm_in", h, W["w_in"], [F32, BF16], tm=512, tn=IN_WIDTH_P // 2)

    def mix_prep(cq, ckv, qs, ks, vs, gq, gkv, cq_t, sq_t, ck_t, sk_t):
        ksr = _rope(ks, ck_t, sk_t, SWA_HEAD_DIM // 2)
        hd = SWA_HEAD_DIM
        return (_rms(cq, gq), _rms(ckv, gkv), _rope(qs, cq_t, sq_t, hd // 2),
                [ksr[:, :hd], ksr[:, hd:]], [vs[:, :hd], vs[:, hd:]])

    kv3 = lambda dt: ((SWA_KV_HEADS, S, SWA_HEAD_DIM), dt, (SWA_KV_HEADS, T, SWA_HEAD_DIM), lambda i: (0, i, 0))
    cqn, ckvn, q_swa, k_swa, v_swa = _rowwise(
        f"l{l}_mix_prep", mix_prep,
        [_ri(proj, T, 256, _cb("c_q")), _ri(proj, T, 128, _cb("c_kv")), _ri(proj, T, 512, _cb("q_swa")), _ri(proj, T, 128, _cb("k_swa")),
         _ri(proj, T, 128, _cb("v_swa")), _bi(P["g_q_lat"]), _bi(P["g_kv_lat"]), _ri(tb["c_swa_q"], T), _ri(tb["s_swa_q"], T),
         _ri(tb["c_swa_k"], T), _ri(tb["s_swa_k"], T)],
        [_ro(S, 256, BF16, T), _ro(S, 128, BF16, T), _ro(S, 512, BF16, T), kv3(BF16), kv3(BF16)], n_tiles=nt)
    k_swa = k_swa.reshape(SWA_KV_HEADS * S, SWA_HEAD_DIM)
    v_swa = v_swa.reshape(SWA_KV_HEADS * S, SWA_HEAD_DIM)
    q_lat, = _mm_nn(f"l{l}_mm_uq", cqn, W["w_uq"], [F32], tm=1024, tn=MLA_QW)
    kv_lat, = _mm_nn(f"l{l}_mm_ukv", ckvn, W["w_ukv"], [F32], tm=1024, tn=MLA_KVW)

    def mla_prep(q, kk, vv, kr, cq_t, sq_t, ck_t, sk_t):
        kpe = pltpu.roll(_rope(kr, ck_t, sk_t, MLA_ROPE // 2), MLA_NOPE, axis=1)
        return _rope(q, cq_t, sq_t, MLA_ROPE // 2) * MLA_SCALE, kk + jnp.tile(kpe, (1, MLA_HEADS)), vv

    q_mla, k_mla, v_mla = _rowwise(
        f"l{l}_mla_prep", mla_prep,
        [_ri(q_lat, T), _ri(kv_lat, T, MLA_HEADS * LANES, 0), _ri(kv_lat, T, MLA_HEADS * MLA_V, 2), _ri(proj, T, 128, _cb("k_rope")),
         _ri(tb["c_mla_q"], T), _ri(tb["s_mla_q"], T), _ri(tb["c_mla_k"], T), _ri(tb["s_mla_k"], T)],
        [_ro(S, MLA_QW, BF16, T), _ro(S, MLA_HEADS * LANES, BF16, T), _ro(S, MLA_HEADS * MLA_V, BF16, T)], n_tiles=nt)

    (att_a, lse_a), rode = _softmax_fwd(f"l{l}_mla_fwd", q_mla, k_mla, v_mla, o_width=MLA_HEADS * MLA_V, ride=ride,
                                        **{**MLA_SPEC, "tq": min(MLA_FWD_TQ, S), "tk": min(MLA_FWD_TK, S)}, **_mla_specs(S, min(MLA_FWD_TQ, S)))
    W = {**W, **late(rode)}
    att_b, lse_b = _swa_fwd(f"l{l}_swa_fwd", q_swa, k_swa, v_swa, P["swa_sinks"], tq=SWA_TQ)
    att_c, att_c32 = _sb_fwd(f"l{l}_sb_fwd", proj16, **SB_SPEC)
    o_a, = _mm_nn(f"l{l}_mm_oa", att_a, W["w_o_mla"], [F32], tm=1024, tn=D_MODEL)
    o_b, = _mm_nn(f"l{l}_mm_ob", att_b, W["w_o_swa"], [F32], tm=1024, tn=D_MODEL)
    o_c, = _mm_nn(f"l{l}_mm_oc", att_c, W["w_o_sb"], [F32], tm=1024, tn=D_MODEL)

    def gate_mix(gl, b, oa, ob, oc):
        gt = jax.nn.sigmoid(gl + b)
        return (gt[:, :D_MODEL] * oa + gt[:, D_MODEL:2 * D_MODEL] * ob + gt[:, 2 * D_MODEL:] * oc,)

    mixed, = _rowwise(f"l{l}_gate_mix", gate_mix, [_ri(proj, T, 3072, 0), _bi(P["b_gate"]), _ri(o_a, T), _ri(o_b, T), _ri(o_c, T)],
                      [_ro(S, D_MODEL, BF16, T)], n_tiles=nt)
    y, = _mm_nn(f"l{l}_mm_out", mixed, W["w_out"], [F32], tm=1024, tn=D_MODEL)

    def resid_norm(xv, yv, gpost, gpre):
        x1 = xv + _rms(yv, gpost)
        return x1, _rms(x1, gpre)

    x1, h2 = _rowwise(f"l{l}_resid_norm", resid_norm, [_ri(x, T), _ri(y, T), _bi(P["g_mix_post"]), _bi(P["g_mlp_pre"])],
                      [_ro(S, D_MODEL, F32, T), _ro(S, D_MODEL, BF16, T)], n_tiles=nt)
    up, u = _mm_nn(f"l{l}_mm_up", h2, W["w_up"], [F32, BF16], tm=512, tn=2048,
                   epilogue=lambda acc: (acc, jnp.square(jnp.maximum(acc, 0.0))))
    dn, = _mm_nn(f"l{l}_mm_down", u, W["w_down"], [F32], tm=512, tn=D_MODEL)
    x2, = _rowwise(f"l{l}_resid_out", lambda xv, dv, g: (xv + _rms(dv, g),), [_ri(x1, T), _ri(dn, T), _bi(P["g_mlp_post"])],
                   [_ro(S, D_MODEL, F32, T)], n_tiles=nt)
    saved = dict(x=x, h=h, proj=proj, proj16=proj16, cqn=cqn, ckvn=ckvn, q_swa=q_swa, k_swa=k_swa, v_swa=v_swa, q_mla=q_mla, k_mla=k_mla,
                 v_mla=v_mla, att_a=att_a, lse_a=lse_a, att_b=att_b, lse_b=lse_b, att_c=att_c, att_c32=att_c32, o_a=o_a, o_b=o_b, o_c=o_c,
                 mixed=mixed, y=y, x1=x1, h2=h2, up=up, u=u, dn=dn)
    return x2, saved, W, rode


def _layer_bwd(l, dx2, sv, W, P, tb, T, pending, to_send):
    S = dx2.shape[0]
    nt = S // T
    G = {}

    def post_norm_bwd(v, g, dy):
        return _rms_bwd(v, g, dy)

    d_dn, G["g_mlp_post"] = _rowwise(f"l{l}_b_post2", post_norm_bwd, [_ri(sv["dn"], T), _bi(P["g_mlp_post"]), _ri(dx2, T)],
                                    [_ro(S, D_MODEL, BF16, T)], [D_MODEL], n_tiles=nt)
    d_up, = _mm_nt(f"l{l}_b_mm_down", d_dn, W["w_down"], [BF16], tm=512, tn=2048, extras=[sv["up"]],
                   epilogue=lambda acc, upv: (acc * (2.0 * jnp.maximum(upv, 0.0)),))
    G["w_down"] = _mm_tn(f"l{l}_g_down", sv["u"], d_dn, tm=2048, tn=D_MODEL, ts=512)
    d_h2, = _mm_nt(f"l{l}_b_mm_up", d_up, W["w_up"], [F32], tm=512, tn=D_MODEL)
    G["w_up"] = _mm_tn(f"l{l}_g_up", sv["h2"], d_up, tm=D_MODEL, tn=2048, ts=512)

    def pre_norm_bwd(v, g, dy, dres):
        dx, dg = _rms_bwd(v, g, dy)
        return dres + dx, dg

    def mid_norms_bwd(x1v, gpre, dh2, dres, yv, gpost):
        dx1v, dgpre = pre_norm_bwd(x1v, gpre, dh2, dres)
        dyv, dgpost = post_norm_bwd(yv, gpost, dx1v)
        return dx1v, dyv, dgpre, dgpost

    dx1, d_y, G["g_mlp_pre"], G["g_mix_post"] = _rowwise(
        f"l{l}_b_mid_norms", mid_norms_bwd,
        [_ri(sv["x1"], T), _bi(P["g_mlp_pre"]), _ri(d_h2, T), _ri(dx2, T), _ri(sv["y"], T), _bi(P["g_mix_post"])],
        [_ro(S, D_MODEL, F32, T), _ro(S, D_MODEL, BF16, T)], [D_MODEL, D_MODEL], n_tiles=nt)
    d_mixed, = _mm_nt(f"l{l}_b_mm_out", d_y, W["w_out"], [F32], tm=1024, tn=D_MODEL)
    G["w_out"] = _mm_tn(f"l{l}_g_out", sv["mixed"], d_y, tm=D_MODEL, tn=D_MODEL, ts=512)

    def gate_bwd(dm, gl, b, oa, ob, oc):
        gt = jax.nn.sigmoid(gl + b)
        outs, dgl = [], []
        for k, o in enumerate((oa, ob, oc)):
            gk = gt[:, k * D_MODEL:(k + 1) * D_MODEL]
            outs.append(dm * gk)
            dgl.append(dm * o * gk * (1.0 - gk))
        dgl = jnp.concatenate(dgl, axis=1)
        return (*outs, dgl, jnp.sum(dgl, axis=0, keepdims=True))

    d_oa, d_ob, d_oc, d_gl, G["b_gate"] = _rowwise(
        f"l{l}_b_gate", gate_bwd, [_ri(d_mixed, T), _ri(sv["proj"], T, 3072, 0), _bi(P["b_gate"]), _ri(sv["o_a"], T), _ri(sv["o_b"], T), _ri(sv["o_c"], T)],
        [_ro(S, D_MODEL, BF16, T)] * 3 + [_ro(S, 3 * D_MODEL, BF16, T)], [3 * D_MODEL], n_tiles=nt)
    d_att = {}
    for br, d_o, att in (("mla", d_oa, sv["att_a"]), ("swa", d_ob, sv["att_b"]), ("sb", d_oc, sv["att_c"])):
        d_att[br], = _mm_nt(f"l{l}_b_mm_o_{br}", d_o, W["w_o_" + br], [F32], tm=1024, tn=512)
        G["w_o_" + br] = _mm_tn(f"l{l}_g_o_{br}", att, d_o, tm=512, tn=D_MODEL, ts=512)

    riders = list(pending) + [(n, l, to_send(n, G[n])) for n in W_LATE]
    (dq_mla, dkt_mla, dvt_mla), rode = _softmax_bwd(f"l{l}_mla_bwd", sv["q_mla"], sv["k_mla"], sv["v_mla"], sv["att_a"], d_att["mla"], sv["lse_a"],
                                                  scale=MLA_SCALE, ride=([s for _, _, s in riders], True), **MLA_SPEC, **_mla_specs(S, MLA_SPEC["tq"]))
    rode = [(n, ly, r) for (n, ly, _), r in zip(riders, rode)]
    dk_mla = dkt_mla.transpose(0, 2, 1).reshape(S, -1)
    dv_mla = dvt_mla.transpose(0, 2, 1).reshape(S, -1)
    dq_swa, dkt_swa, dvt_swa, dsink = _swa_bwd(f"l{l}_swa_bwd", sv["q_swa"], sv["k_swa"], sv["v_swa"], sv["att_b"], d_att["swa"], sv["lse_b"],
                                               P["swa_sinks"], tq=SWA_TQ)
    dk_swa = dkt_swa.transpose(0, 2, 1).reshape(SWA_KV_HEADS * S, SWA_HEAD_DIM)
    dv_swa = dvt_swa.transpose(0, 2, 1).reshape(SWA_KV_HEADS * S, SWA_HEAD_DIM)
    G["swa_sinks"] = dsink[:, :SWA_G, 0].reshape(1, SWA_HEADS)
    dq_sb, dkt_sb, dvt_sb = _sb_bwd(f"l{l}_sb_bwd", sv["proj16"], sv["att_c32"], d_att["sb"], **SB_SPEC)
    dk_sb = dkt_sb.transpose(0, 2, 1).reshape(S, -1)
    dv_sb = dvt_sb.transpose(0, 2, 1).reshape(S, -1)

    def mla_prep_bwd(dq, dk, dvv, cq_t, sq_t, ck_t, sk_t):
        dks = dk[:, :LANES]
        for hh in range(1, MLA_HEADS):
            dks = dks + dk[:, hh * LANES:(hh + 1) * LANES]
        d_kr = _rope_t(pltpu.roll(dks, LANES - MLA_NOPE, axis=1), ck_t, sk_t, MLA_ROPE // 2)
        return _rope_t(dq, cq_t, sq_t, MLA_ROPE // 2), jnp.concatenate([dk, dvv], axis=1), d_kr

    d_q_lat, d_kv_lat, d_krope = _rowwise(
        f"l{l}_b_mla_prep", mla_prep_bwd,
        [_ri(dq_mla, T), _ri(dk_mla, T), _ri(dv_mla, T), _ri(tb["c_mla_q"], T), _ri(tb["s_mla_q"], T), _ri(tb["c_mla_k"], T), _ri(tb["s_mla_k"], T)],
        [_ro(S, MLA_QW, BF16, T), _ro(S, MLA_KVW, BF16, T), _ro(S, LANES, BF16, T)], n_tiles=nt)
    d_cqn, = _mm_nt(f"l{l}_b_mm_uq", d_q_lat, W["w_uq"], [F32], tm=1024, tn=MLA_Q_LORA)
    G["w_uq"] = _mm_tn(f"l{l}_g_uq", sv["cqn"], d_q_lat, tm=MLA_Q_LORA, tn=MLA_QW, ts=512)
    d_ckvn, = _mm_nt(f"l{l}_b_mm_ukv", d_kv_lat, W["w_ukv"], [F32], tm=1024, tn=MLA_KV_LORA)
    G["w_ukv"] = _mm_tn(f"l{l}_g_ukv", sv["ckvn"], d_kv_lat, tm=MLA_KV_LORA, tn=MLA_KVW, ts=512)

    def mix_prep_bwd(cq, ckv, gq, gkv, dcqn, dckvn, dqs, dks, dvs, cq_t, sq_t, ck_t, sk_t):
        d_cq, dgq = _rms_bwd(cq, gq, dcqn)
        d_ckv, dgkv = _rms_bwd(ckv, gkv, dckvn)
        dk2 = jnp.concatenate([dks[0], dks[1]], axis=1)
        dv2 = jnp.concatenate([dvs[0], dvs[1]], axis=1)
        return (d_cq, d_ckv, _rope_t(dqs, cq_t, sq_t, SWA_HEAD_DIM // 2), _rope_t(dk2, ck_t, sk_t, SWA_HEAD_DIM // 2), dv2, dgq, dgkv)

    kv3 = lambda a: (a.reshape(SWA_KV_HEADS, S, SWA_HEAD_DIM), (SWA_KV_HEADS, T, SWA_HEAD_DIM), lambda i: (0, i, 0))
    d_cq, d_ckv, d_qswa, d_kswa, d_vswa, G["g_q_lat"], G["g_kv_lat"] = _rowwise(
        f"l{l}_b_mix_prep", mix_prep_bwd,
        [_ri(sv["proj"], T, 256, _cb("c_q")), _ri(sv["proj"], T, 128, _cb("c_kv")), _bi(P["g_q_lat"]), _bi(P["g_kv_lat"]), _ri(d_cqn, T), _ri(d_ckvn, T),
         _ri(dq_swa, T), kv3(dk_swa), kv3(dv_swa), _ri(tb["c_swa_q"], T), _ri(tb["s_swa_q"], T), _ri(tb["c_swa_k"], T), _ri(tb["s_swa_k"], T)],
        [_ro(S, 256, BF16, T), _ro(S, 128, BF16, T), _ro(S, 512, BF16, T), _ro(S, 128, BF16, T), _ro(S, 128, BF16, T)], [256, 128], n_tiles=nt)
    pieces = dict(gates=d_gl, q_swa=d_qswa, q_sb=dq_sb, k_sb=dk_sb, v_sb=dv_sb, c_q=d_cq, c_kv=d_ckv, k_swa=d_kswa, v_swa=d_vswa, k_rope=d_krope)
    d_proj = jnp.concatenate([pieces[n].astype(BF16) for n in PERM_ORDER], axis=1)
    d_h, = _mm_nt(f"l{l}_b_mm_in", d_proj, W["w_in"], [F32], tm=512, tn=512)
    G["w_in"] = _mm_tn(f"l{l}_g_in", sv["h"], d_proj, tm=D_MODEL, tn=IN_WIDTH_P // 2, ts=512)
    dx, G["g_mix_pre"] = _rowwise(f"l{l}_b_pre1", pre_norm_bwd, [_ri(sv["x"], T), _bi(P["g_mix_pre"]), _ri(d_h, T), _ri(dx1, T)],
                                 [_ro(S, D_MODEL, F32, T)], [D_MODEL], n_tiles=nt)
    return dx, G, rode


def _pack_rows(vecs, rows):
    flat = jnp.concatenate([v.reshape(-1) for v in vecs])
    return jnp.pad(flat, (0, rows * PACK_COLS - flat.shape[0])).reshape(rows, PACK_COLS)


def kernel(x, positions, g_mix_pre, w_in, b_gate, g_q_lat, g_kv_lat, w_uq, w_ukv, swa_sinks, w_o_mla, w_o_swa, w_o_sb, w_out, g_mix_post, g_mlp_pre, w_up, w_down, g_mlp_post, loss_target, m_g_mix_pre, m_w_in, m_b_gate, m_g_q_lat, m_g_kv_lat, m_w_uq, m_w_ukv, m_swa_sinks, m_w_o_mla, m_w_o_swa, m_w_o_sb, m_w_out, m_g_mix_post, m_g_mlp_pre, m_w_up, m_w_down, m_g_mlp_post, v_g_mix_pre, v_w_in, v_b_gate, v_g_q_lat, v_g_kv_lat, v_w_uq, v_w_ukv, v_swa_sinks, v_w_o_mla, v_w_o_swa, v_w_o_sb, v_w_out, v_g_mix_post, v_g_mlp_pre, v_w_up, v_w_down, v_g_mlp_post):
    a = dict(locals())
    S = x.shape[1]
    depth = w_in.shape[0]
    T = min(ROW_TILE, S)
    xs = x.reshape(S, D_MODEL)
    tb = _tables(positions)

    perm = dict(w_in=_perm_w_in, w_uq=_perm_w_uq, w_ukv=_perm_w_ukv)

    def shards(l, names):
        return [a[n][l].astype(BF16) for n in names]

    def whole(names, gathered):
        return {n: perm.get(n, lambda t: t)(_from_shards(n, g, (1,) + a[n].shape[1:])[0]) for n, g in zip(names, gathered)}

    early = whole(W_EARLY, _exchange("gather_weights", shards(0, W_EARLY), per_peer=False))
    layers, saved = [], []
    h = xs
    for l in range(depth):
        P = {n: a[n][l].reshape(1, -1) for n in SMALL if n != "swa_sinks"}
        P["swa_sinks"] = a["swa_sinks"][l]
        srcs = shards(l, W_LATE) + (shards(l + 1, W_EARLY) if l + 1 < depth else [])
        h, sv, W, rode = _layer_fwd(l, h, early, P, tb, T, (srcs, False), lambda r: whole(W_LATE, r[:len(W_LATE)]))
        early = whole(W_EARLY, rode[len(W_LATE):])
        layers.append((W, P))
        saved.append(sv)

    def loss_head(yv, tv):
        err = yv - tv
        part = 0.5 * jnp.sum(jnp.mean(err * err, axis=1, keepdims=True), axis=0, keepdims=True)
        return err * (1.0 / D_MODEL), jnp.broadcast_to(part, (1, LANES))

    dh, loss_row = _rowwise("loss_head", loss_head, [_ri(h, T), _ri(loss_target.reshape(S, D_MODEL), T)], [_ro(S, D_MODEL, F32, T)], [LANES],
                            n_tiles=S // T)
    loss = lax.psum(loss_row[0, 0], ("x", "y", "c"))

    unperm = dict(w_in=_unperm_w_in, w_uq=_unperm_w_uq, w_ukv=_unperm_w_ukv)

    def to_send(n, g):
        return _to_shards(n, unperm.get(n, lambda t: t)(g).astype(BF16)[None])

    grads, recv, pending = [None] * depth, {n: [None] * depth for n in BIG}, []
    for l in reversed(range(depth)):
        W, P = layers[l]
        dh, grads[l], rode = _layer_bwd(l, dh, saved[l], W, P, tb, T, pending, to_send)
        for n, ly, r in rode:
            recv[n][ly] = r
        pending = [(n, l, to_send(n, grads[l][n])) for n in W_EARLY]
    for (n, ly, _), r in zip(pending, _exchange("scatter_grads", [s for _, _, s in pending], per_peer=True)):
        recv[n][ly] = r
    grad_x = dh.reshape(x.shape)

    out = {}
    for n in BIG:
        shp = a[n].shape
        rows, cols = shp[0] * shp[1], shp[2]
        parts = jnp.concatenate(recv[n], axis=1)
        res = _adamw("adamw_" + n, parts, a[n].reshape(rows, cols), a["m_" + n].reshape(rows, cols), a["v_" + n].reshape(rows, cols),
                     tile=min(256, rows))
        out[n] = [r.reshape(shp) for r in res]

    small_total = sum(a[n].size for n in SMALL)
    small_rows = -(-small_total // (8 * PACK_COLS)) * 8
    sg = _pack_rows([jnp.stack([grads[l][n].reshape(-1) for l in range(depth)]) for n in SMALL], small_rows)
    sg_all, = _exchange("gather_small_grads", [sg], per_peer=False)
    res = _adamw("adamw_small", sg_all, _pack_rows([a[n] for n in SMALL], small_rows), _pack_rows([a["m_" + n] for n in SMALL], small_rows),
                 _pack_rows([a["v_" + n] for n in SMALL], small_rows), tile=small_rows)
    off = 0
    for n in SMALL:
        cnt = a[n].size
        out[n] = [r.reshape(-1)[off:off + cnt].reshape(a[n].shape) for r in res]
        off += cnt

    order = ("g_mix_pre", "w_in", "b_gate", "g_q_lat", "g_kv_lat", "w_uq", "w_ukv", "swa_sinks", "w_o_mla", "w_o_swa", "w_o_sb", "w_out",
             "g_mix_post", "g_mlp_pre", "w_up", "w_down", "g_mlp_post")
    return (loss, grad_x, *[out[n][0] for n in order], *[out[n][1] for n in order], *[out[n][2] for n in order], *[out[n][3] for n in order])
```

```python
import functools

import jax
import jax.numpy as jnp
from jax import lax
from jax.experimental import pallas as pl
from jax.experimental.pallas import tpu as pltpu

F32, BF16 = jnp.float32, jnp.bfloat16

D_MODEL = 1024
DEPTH = 4
MLA_HEADS, MLA_Q_LORA, MLA_KV_LORA, MLA_NOPE, MLA_ROPE, MLA_V = 8, 256, 128, 64, 32, 64
SWA_HEADS, SWA_KV_HEADS, SWA_HEAD_DIM, SWA_WINDOW = 8, 2, 64, 128
SB_HEADS, SB_HEAD_DIM = 8, 64
D_FF = 4 * D_MODEL
ROPE_THETA = 10000.0
EPS = 1e-6
N_DEV = 8
ADAM_LR, ADAM_B1, ADAM_B2, ADAM_EPS, ADAM_WD, ADAM_STEP = 0.001, 0.9, 0.999, 1e-08, 0.01, 10

LANES = 128
VMEM_LIMIT_MAX = 60 * 1024 * 1024
VMEM_LIMIT_MIN = 32 * 1024 * 1024

ORIG_COLS = dict(c_q=(0, 256), c_kv=(256, 128), k_rope=(384, 32), q_swa=(416, 512), k_swa=(928, 128), v_swa=(1056, 128),
                 q_sb=(1184, 512), k_sb=(1696, 512), v_sb=(2208, 512), gates=(2720, 3072))
IN_WIDTH = 5792
PERM_ORDER = ("gates", "q_swa", "q_sb", "k_sb", "v_sb", "c_q", "c_kv", "k_swa", "v_swa", "k_rope")
PERM_WIDTH = dict(gates=3072, q_swa=512, q_sb=512, k_sb=512, v_sb=512, c_q=256, c_kv=128, k_swa=128, v_swa=128, k_rope=128)
PERM_OFF = {}
_o = 0
for _n in PERM_ORDER:
    PERM_OFF[_n] = _o
    _o += PERM_WIDTH[_n]
IN_WIDTH_P = _o
MLA_QW = MLA_HEADS * LANES
MLA_KVW = MLA_HEADS * LANES + MLA_HEADS * MLA_V

ROW_TILE = 512
NT = (((1,), (1,)), ((), ()))
TN = (((0,), (0,)), ((), ()))
NEG = -1e30


def _cparams(sem, block_bytes):
    limit = int(min(VMEM_LIMIT_MAX, max(VMEM_LIMIT_MIN, 2 * block_bytes + (16 << 20))))
    return pltpu.CompilerParams(dimension_semantics=sem, vmem_limit_bytes=limit)


def _nbytes(shape, dtype):
    n = 1
    for s in shape:
        n *= s
    return n * jnp.dtype(dtype).itemsize


def _ri(arr, tile, width=None, cb=0):
    width = arr.shape[1] if width is None else width
    return (arr, (tile, width), lambda i, cb=cb: (i, cb))


def _bi(arr):
    return (arr, arr.shape, lambda i: (0, 0))


def _ro(rows, width, dtype, tile):
    return ((rows, width), dtype, (tile, width), lambda i: (i, 0))


def _rowwise(name, fn, ins, outs, reds=(), *, n_tiles):
    n_in, n_out = len(ins), len(outs)

    def body(*refs):
        vals = fn(*[r[...] for r in refs[:n_in]])
        for r, v in zip(refs[n_in:n_in + n_out], vals[:n_out]):
            if isinstance(v, (list, tuple)):
                for j, vj in enumerate(v):
                    r[j] = vj.astype(r.dtype)
            else:
                r[...] = v.astype(r.dtype)
        if reds:
            @pl.when(pl.program_id(0) == 0)
            def _():
                for r in refs[n_in + n_out:]:
                    r[...] = jnp.zeros_like(r)
            for r, v in zip(refs[n_in + n_out:], vals[n_out:]):
                r[...] += v

    block_bytes = sum(_nbytes(b, a.dtype) for a, b, _ in ins) + sum(_nbytes(b, d) for _, d, b, _ in outs)
    res = pl.pallas_call(
        body, name=name, grid=(n_tiles,),
        in_specs=[pl.BlockSpec(b, m) for _, b, m in ins],
        out_specs=[pl.BlockSpec(b, m) for _, _, b, m in outs] + [pl.BlockSpec((1, w), lambda i: (0, 0)) for w in reds],
        out_shape=[jax.ShapeDtypeStruct(s, d) for s, d, _, _ in outs] + [jax.ShapeDtypeStruct((1, w), F32) for w in reds],
        compiler_params=_cparams(("arbitrary",) if reds else ("parallel",), block_bytes),
    )(*[a for a, _, _ in ins])
    return res


def _rms(x, g):
    r = lax.rsqrt(jnp.mean(x * x, axis=1, keepdims=True) + EPS)
    return x * r * g


def _rms_bwd(x, g, dy):
    r = lax.rsqrt(jnp.mean(x * x, axis=1, keepdims=True) + EPS)
    xn = x * r
    dxn = dy * g
    dx = r * (dxn - xn * jnp.mean(dxn * xn, axis=1, keepdims=True))
    return dx, jnp.sum(dy * xn, axis=0, keepdims=True)


def _swap_halves(x, half):
    n = x.shape[1]
    lane = lax.broadcasted_iota(jnp.int32, x.shape, 1)
    first = (lane % (2 * half)) < half
    return jnp.where(first, pltpu.roll(x, n - half, axis=1), pltpu.roll(x, half, axis=1))


def _rope(x, c, sg, half):
    return x * c + _swap_halves(x, half) * sg


def _rope_t(dy, c, sg, half):
    return dy * c - _swap_halves(dy, half) * sg


def _mm_nn(name, a, b, outs, *, tm, tn, extras=(), epilogue=None):
    M, K = a.shape
    N = b.shape[1]
    tm = min(tm, M)
    n_e = len(extras)

    def body(*refs):
        a_ref, b_ref = refs[:2]
        acc = jnp.dot(a_ref[...].astype(BF16), b_ref[...].astype(BF16), preferred_element_type=F32)
        vals = (acc,) * len(outs) if epilogue is None else epilogue(acc, *[r[...] for r in refs[2:2 + n_e]])
        for r, v in zip(refs[2 + n_e:], vals):
            r[...] = v.astype(r.dtype)

    block_bytes = (_nbytes((tm, K), a.dtype) + _nbytes((K, tn), b.dtype) + sum(_nbytes((tm, tn), e.dtype) for e in extras)
                   + sum(_nbytes((tm, tn), d) for d in outs) + _nbytes((tm, tn), F32))
    return pl.pallas_call(
        body, name=name, grid=(N // tn, M // tm),
        in_specs=[pl.BlockSpec((tm, K), lambda j, i: (i, 0)), pl.BlockSpec((K, tn), lambda j, i: (0, j))]
        + [pl.BlockSpec((tm, tn), lambda j, i: (i, j)) for _ in extras],
        out_specs=[pl.BlockSpec((tm, tn), lambda j, i: (i, j)) for _ in outs],
        out_shape=[jax.ShapeDtypeStruct((M, N), d) for d in outs],
        compiler_params=_cparams(("parallel", "parallel"), block_bytes),
    )(a, b, *extras)


def _mm_nt(name, a, b, outs, *, tm, tn, extras=(), epilogue=None):
    M, N = a.shape
    K = b.shape[0]
    tm = min(tm, M)
    n_e = len(extras)

    def body(*refs):
        a_ref, b_ref = refs[:2]
        acc = lax.dot_general(a_ref[...].astype(BF16), b_ref[...].astype(BF16), NT, preferred_element_type=F32)
        vals = (acc,) * len(outs) if epilogue is None else epilogue(acc, *[r[...] for r in refs[2:2 + n_e]])
        for r, v in zip(refs[2 + n_e:], vals):
            r[...] = v.astype(r.dtype)

    block_bytes = (_nbytes((tm, N), a.dtype) + _nbytes((tn, N), b.dtype) + sum(_nbytes((tm, tn), e.dtype) for e in extras)
                   + sum(_nbytes((tm, tn), d) for d in outs) + _nbytes((tm, tn), F32))
    return pl.pallas_call(
        body, name=name, grid=(K // tn, M // tm),
        in_specs=[pl.BlockSpec((tm, N), lambda j, i: (i, 0)), pl.BlockSpec((tn, N), lambda j, i: (j, 0))]
        + [pl.BlockSpec((tm, tn), lambda j, i: (i, j)) for _ in extras],
        out_specs=[pl.BlockSpec((tm, tn), lambda j, i: (i, j)) for _ in outs],
        out_shape=[jax.ShapeDtypeStruct((M, K), d) for d in outs],
        compiler_params=_cparams(("parallel", "parallel"), block_bytes),
    )(a, b, *extras)


def _mm_tn(name, a, b, *, tm, tn, ts):
    S, K = a.shape
    N = b.shape[1]
    ts = min(ts, S)

    def body(a_ref, b_ref, o_ref):
        @pl.when(pl.program_id(2) == 0)
        def _():
            o_ref[...] = jnp.zeros_like(o_ref)
        o_ref[...] += lax.dot_general(a_ref[...].astype(BF16), b_ref[...].astype(BF16), TN, preferred_element_type=F32)

    block_bytes = _nbytes((ts, tm), a.dtype) + _nbytes((ts, tn), b.dtype) + 2 * _nbytes((tm, tn), F32)
    return pl.pallas_call(
        body, name=name, grid=(K // tm, N // tn, S // ts),
        in_specs=[pl.BlockSpec((ts, tm), lambda i, j, s: (s, i)), pl.BlockSpec((ts, tn), lambda i, j, s: (s, j))],
        out_specs=pl.BlockSpec((tm, tn), lambda i, j, s: (i, j)),
        out_shape=jax.ShapeDtypeStruct((K, N), F32),
        compiler_params=_cparams(("parallel", "parallel", "arbitrary"), block_bytes),
    )(a, b)


def _lane_pack(cols, rows):
    lane = lax.broadcasted_iota(jnp.int32, (rows, LANES), 1)
    val = jnp.zeros((rows, LANES), F32)
    for h, c in enumerate(cols):
        val = jnp.where(lane == h, c, val)
    return val


def _mask(kb, row, tk):
    return kb * tk + lax.broadcasted_iota(jnp.int32, (1, tk), 1) <= row


def _kb_range(i, tq, tk):
    return (i * tq) // tk, ((i + 1) * tq + tk - 1) // tk


def _softmax_fwd(name, q, k, v, *, groups, heads, q_spec, k_spec, v_spec, o_width, tq, tk, ride=None):
    S = q.shape[0]
    nq = S // tq
    nh = len(heads)
    dv = heads[0][2].stop - heads[0][2].start

    def body(q_ref, k_ref, v_ref, o_ref, lse_ref):
        i = pl.program_id(1)
        q_all = q_ref[...]
        row = i * tq + lax.broadcasted_iota(jnp.int32, (tq, 1), 0)
        mid, hi = _kb_range(i, tq, tk)
        qhs = [q_all[:, qs] for qs, _, _ in heads]

        def blk(kb, carry, masked):
            r0 = pl.multiple_of(kb * tk, tk)
            k_all = k_ref[pl.ds(r0, tk), :]
            v_all = v_ref[pl.ds(r0, tk), :]
            if masked:
                ok = _mask(kb, row, tk)
            ones = jnp.ones((tk, dv), BF16)
            new = []
            for qh, (_, ks, vs), (m, acc) in zip(qhs, heads, carry):
                s = lax.dot_general(qh, k_all[:, ks], NT, preferred_element_type=F32)
                if masked:
                    s = jnp.where(ok, s, NEG)
                m_new = jnp.maximum(m, jnp.max(s, axis=1, keepdims=True))
                p = jnp.exp(s - m_new)
                if masked:
                    p = jnp.where(ok, p, 0.0)
                pv = jnp.dot(p.astype(BF16), jnp.concatenate([v_all[:, vs], ones], axis=1), preferred_element_type=F32)
                new.append((m_new, jnp.exp(m - m_new) * acc + pv))
            return tuple(new)

        carry = tuple((jnp.full((tq, 1), NEG, F32), jnp.zeros((tq, 2 * dv), F32)) for _ in range(nh))
        carry = lax.fori_loop(0, mid, functools.partial(blk, masked=False), carry)
        carry = lax.fori_loop(mid, hi, functools.partial(blk, masked=True), carry)
        o_ref[...] = jnp.concatenate([acc[:, :dv] / acc[:, dv:] for _, acc in carry], axis=1).astype(o_ref.dtype)
        lse_ref[0] = _lane_pack([m + jnp.log(acc[:, dv:dv + 1]) for m, acc in carry], tq)

    wo = nh * dv
    block_bytes = _nbytes(q_spec[0], q.dtype) + _nbytes(k_spec[0], k.dtype) + _nbytes(v_spec[0], v.dtype) + 4 * tq * (wo + LANES)
    return _pallas(
        body, name=name, grid=(groups, nq), in_specs=[pl.BlockSpec(*q_spec), pl.BlockSpec(*k_spec), pl.BlockSpec(*v_spec)],
        out_specs=[pl.BlockSpec((tq, wo), lambda g, i: (i, g)), pl.BlockSpec((1, tq, LANES), lambda g, i: (g, i, 0))],
        out_shape=[jax.ShapeDtypeStruct((S, o_width), BF16), jax.ShapeDtypeStruct((groups, S, LANES), F32)],
        args=[q, k, v], sem=("parallel", "arbitrary"), block_bytes=block_bytes, ride=ride)


def _softmax_bwd(name, q, k, v, o, do, lse, *, groups, heads, q_spec, k_spec, v_spec, tq, tk, scale, ride=None):
    S = q.shape[0]
    nq, nkb = S // tq, S // tk
    nh = len(heads)
    dv = heads[0][2].stop - heads[0][2].start
    wo = nh * dv
    wk, wv = k_spec[0][1], v_spec[0][1]

    def body(q_ref, k_ref, v_ref, o_ref, do_ref, lse_ref, dq_ref, dkt_ref, dvt_ref):
        i = pl.program_id(1)

        @pl.when(i == 0)
        def _():
            dkt_ref[...] = jnp.zeros_like(dkt_ref)
            dvt_ref[...] = jnp.zeros_like(dvt_ref)

        q_all = q_ref[...]
        o_all = o_ref[...].astype(F32)
        do_all = do_ref[...].astype(F32)
        lse_all = lse_ref[0]
        row = i * tq + lax.broadcasted_iota(jnp.int32, (tq, 1), 0)
        mid, hi = _kb_range(i, tq, tk)
        per_head = []
        for h, (qs, ks, vs) in enumerate(heads):
            osl = slice(h * dv, (h + 1) * dv)
            doh = do_all[:, osl]
            delta = jnp.sum(doh * o_all[:, osl], axis=1, keepdims=True)
            qh = q_all[:, qs]
            per_head.append((qh, qh.astype(F32).T.astype(BF16), doh.astype(BF16), doh.T.astype(BF16), delta, lse_all[:, h:h + 1], ks, vs))

        def blk(kb, dqs, masked):
            r0 = pl.multiple_of(kb * tk, tk)
            k_all = k_ref[pl.ds(r0, tk), :]
            v_all = v_ref[pl.ds(r0, tk), :]
            if masked:
                ok = _mask(kb, row, tk)
            dkt, dvt, new_dqs = [], [], []
            for (qh, qt, doh, dot_, delta, lse_h, ks, vs), dq in zip(per_head, dqs):
                kk, vv = k_all[:, ks], v_all[:, vs]
                s = lax.dot_general(qh, kk, NT, preferred_element_type=F32)
                p = jnp.exp(s - lse_h)
                if masked:
                    p = jnp.where(ok, p, 0.0)
                dp = lax.dot_general(doh, vv, NT, preferred_element_type=F32)
                ds = (p * (dp - delta)).astype(BF16)
                new_dqs.append(dq + jnp.dot(ds, kk, preferred_element_type=F32))
                dkt.append(jnp.dot(qt, ds, preferred_element_type=F32))
                dvt.append(jnp.dot(dot_, p.astype(BF16), preferred_element_type=F32))
            dkt_ref[kb] += jnp.concatenate(dkt, axis=0)
            dvt_ref[kb] += jnp.concatenate(dvt, axis=0)
            return tuple(new_dqs)

        dqs = tuple(jnp.zeros((tq, qs.stop - qs.start), F32) for qs, _, _ in heads)
        dqs = lax.fori_loop(0, mid, functools.partial(blk, masked=False), dqs)
        dqs = lax.fori_loop(mid, hi, functools.partial(blk, masked=True), dqs)
        dq_ref[...] = jnp.concatenate([dq * scale for dq in dqs], axis=1)

    in_specs = [pl.BlockSpec(*q_spec), pl.BlockSpec(*k_spec), pl.BlockSpec(*v_spec),
                pl.BlockSpec((tq, wo), lambda g, i: (i, g)), pl.BlockSpec((tq, wo), lambda g, i: (i, g)),
                pl.BlockSpec((1, tq, LANES), lambda g, i: (g, i, 0))]
    args = [q, k, v, o, do, lse]
    out_specs = [pl.BlockSpec(*q_spec), pl.BlockSpec((nkb, wk, tk), lambda g, i: (0, g, 0)), pl.BlockSpec((nkb, wv, tk), lambda g, i: (0, g, 0))]
    out_shape = [jax.ShapeDtypeStruct(q.shape, F32), jax.ShapeDtypeStruct((nkb, k.shape[1], tk), F32),
                 jax.ShapeDtypeStruct((nkb, v.shape[1], tk), F32)]
    block_bytes = (_nbytes(q_spec[0], q.dtype) + _nbytes(k_spec[0], k.dtype) + _nbytes(v_spec[0], v.dtype) + 6 * tq * wo + 4 * tq * LANES
                   + _nbytes(q_spec[0], F32) + _nbytes(k_spec[0], F32) + _nbytes(v_spec[0], F32))
    return _pallas(body, name=name, grid=(groups, nq), in_specs=in_specs, out_specs=out_specs, out_shape=out_shape, args=args,
                   sem=("parallel", "arbitrary"), block_bytes=block_bytes, ride=ride)


def _swa_window(i, tq, row):
    start = pl.multiple_of(jnp.maximum(i * tq - SWA_WINDOW, 0), SWA_WINDOW)
    col = start + lax.broadcasted_iota(jnp.int32, (1, tq + SWA_WINDOW), 1)
    return start, (col <= row) & ((row - col) < SWA_WINDOW)


def _swa_fwd(name, q, k, v, sinks, *, tq):
    S = q.shape[0]
    hd, span = SWA_HEAD_DIM, tq + SWA_WINDOW
    scale = hd ** -0.5

    def body(sink_ref, q_ref, k_ref, v_ref, o_ref, lse_ref):
        g, i = pl.program_id(0), pl.program_id(1)
        row = i * tq + lax.broadcasted_iota(jnp.int32, (tq, 1), 0)
        start, ok = _swa_window(i, tq, row)
        kk, vv = k_ref[pl.ds(start, span), :], v_ref[pl.ds(start, span), :]
        q_all = q_ref[...]
        outs, lses = [], []
        for h in range(SWA_G):
            s = lax.dot_general(q_all[:, h * hd:(h + 1) * hd] * scale, kk, NT, preferred_element_type=F32)
            s = jnp.where(ok, s, NEG)
            sink = sink_ref[g * SWA_G + h]
            m = jnp.maximum(jnp.max(s, axis=1, keepdims=True), sink)
            p = jnp.exp(s - m)
            l = jnp.sum(p, axis=1, keepdims=True) + jnp.exp(sink - m)
            outs.append(jnp.dot(p.astype(BF16), vv, preferred_element_type=F32) / l)
            lses.append(m + jnp.log(l))
        o_ref[...] = jnp.concatenate(outs, axis=1).astype(o_ref.dtype)
        lse_ref[0] = _lane_pack(lses, tq)

    wq = SWA_G * hd
    block_bytes = 2 * tq * wq * 2 + 2 * 2 * S * hd + 4 * tq * LANES
    return pl.pallas_call(
        body, name=name, grid=(SWA_KV_HEADS, S // tq),
        in_specs=[pl.BlockSpec(memory_space=pltpu.SMEM), pl.BlockSpec((tq, wq), lambda g, i: (i, g)),
                  pl.BlockSpec((S, hd), lambda g, i: (g, 0)), pl.BlockSpec((S, hd), lambda g, i: (g, 0))],
        out_specs=[pl.BlockSpec((tq, wq), lambda g, i: (i, g)), pl.BlockSpec((1, tq, LANES), lambda g, i: (g, i, 0))],
        out_shape=[jax.ShapeDtypeStruct((S, SWA_HEADS * hd), BF16), jax.ShapeDtypeStruct((SWA_KV_HEADS, S, LANES), F32)],
        compiler_params=_cparams(("parallel", "parallel"), block_bytes),
    )(sinks, q, k, v)


def _swa_bwd(name, q, k, v, o, do, lse, sinks, *, tq):
    S = q.shape[0]
    hd, span = SWA_HEAD_DIM, tq + SWA_WINDOW
    scale = hd ** -0.5
    nblk = S // SWA_WINDOW

    def body(sink_ref, q_ref, k_ref, v_ref, o_ref, do_ref, lse_ref, dq_ref, dkt_ref, dvt_ref, dsink_ref):
        g, i = pl.program_id(0), pl.program_id(1)

        @pl.when(i == 0)
        def _():
            dkt_ref[...] = jnp.zeros_like(dkt_ref)
            dvt_ref[...] = jnp.zeros_like(dvt_ref)
            dsink_ref[...] = jnp.zeros_like(dsink_ref)

        row = i * tq + lax.broadcasted_iota(jnp.int32, (tq, 1), 0)
        start, ok = _swa_window(i, tq, row)
        kk, vv = k_ref[pl.ds(start, span), :], v_ref[pl.ds(start, span), :]
        q_all, o_all, do_all, lse_all = q_ref[...], o_ref[...].astype(F32), do_ref[...].astype(F32), lse_ref[0]
        dqs, qts, dots, dss, ps = [], [], [], [], []
        for h in range(SWA_G):
            sl = slice(h * hd, (h + 1) * hd)
            qh = q_all[:, sl] * scale
            doh = do_all[:, sl]
            delta = jnp.sum(doh * o_all[:, sl], axis=1, keepdims=True)
            lse_h = lse_all[:, h:h + 1]
            qts.append(qh.astype(F32).T.astype(BF16))
            dots.append(doh.T.astype(BF16))
            doh = doh.astype(BF16)
            s = jnp.where(ok, lax.dot_general(qh, kk, NT, preferred_element_type=F32), NEG)
            p = jnp.exp(s - lse_h)
            ds = (p * (lax.dot_general(doh, vv, NT, preferred_element_type=F32) - delta)).astype(BF16)
            dqs.append(jnp.dot(ds, kk, preferred_element_type=F32) * scale)
            dss.append(ds)
            ps.append(p.astype(BF16))
            p_sink = jnp.exp(sink_ref[g * SWA_G + h] - lse_h)
            dsink_ref[0, h:h + 1, :] += jnp.broadcast_to(-jnp.sum(p_sink * delta, axis=0, keepdims=True), (1, LANES))
        dq_ref[...] = jnp.concatenate(dqs, axis=1)
        dkt = jnp.dot(jnp.concatenate(qts, axis=1), jnp.concatenate(dss, axis=0), preferred_element_type=F32)
        dvt = jnp.dot(jnp.concatenate(dots, axis=1), jnp.concatenate(ps, axis=0), preferred_element_type=F32)
        for j in range(span // SWA_WINDOW):
            cols = slice(j * SWA_WINDOW, (j + 1) * SWA_WINDOW)
            dkt_ref[start // SWA_WINDOW + j] += dkt[:, cols]
            dvt_ref[start // SWA_WINDOW + j] += dvt[:, cols]

    wq = SWA_G * hd
    qb = pl.BlockSpec((tq, wq), lambda g, i: (i, g))
    kb = pl.BlockSpec((S, hd), lambda g, i: (g, 0))
    tb = pl.BlockSpec((nblk, hd, SWA_WINDOW), lambda g, i: (g, 0, 0))
    t_shape = jax.ShapeDtypeStruct((SWA_KV_HEADS * nblk, hd, SWA_WINDOW), F32)
    block_bytes = 2 * tq * wq * (2 + 2 + 4 + 4) + 2 * S * hd * (2 + 2 + 4 + 4) + 4 * tq * LANES
    return pl.pallas_call(
        body, name=name, grid=(SWA_KV_HEADS, S // tq),
        in_specs=[pl.BlockSpec(memory_space=pltpu.SMEM), qb, kb, kb, qb, qb, pl.BlockSpec((1, tq, LANES), lambda g, i: (g, i, 0))],
        out_specs=[qb, tb, tb, pl.BlockSpec((1, 8, LANES), lambda g, i: (g, 0, 0))],
        out_shape=[jax.ShapeDtypeStruct(q.shape, F32), t_shape, t_shape, jax.ShapeDtypeStruct((SWA_KV_HEADS, 8, LANES), F32)],
        compiler_params=_cparams(("parallel", "arbitrary"), block_bytes),
    )(sinks, q, k, v, o, do, lse)


def _split_dot(x, u2):
    hi = x.astype(BF16)
    lo = (x - hi.astype(F32)).astype(BF16)
    return jnp.dot(jnp.concatenate([hi, lo], axis=1), u2, preferred_element_type=F32)


def _suffix_ones():
    r = lax.broadcasted_iota(jnp.int32, (2 * LANES, 2 * LANES), 0) % LANES
    c = lax.broadcasted_iota(jnp.int32, (2 * LANES, 2 * LANES), 1)
    return ((r > c) | (c >= LANES)).astype(BF16)


def _suffix_scan(x, uo, run):
    nc = x.shape[1] // LANES
    out = [None] * nc
    for c in reversed(range(nc)):
        st = _split_dot(x[:, c * LANES:(c + 1) * LANES], uo)
        out[c] = st[:, :LANES] + run
        run = run + st[:, LANES:]
    return (out[0] if nc == 1 else jnp.concatenate(out, axis=1)), run


SB_DEAD_LOG = -110.0


def _sb_logits(qh, kk, r0, row, masked):
    z = lax.dot_general(qh, kk, NT, preferred_element_type=F32)
    nz = -z
    l = jnp.minimum(nz, 0.0) - jnp.log(1.0 + jnp.exp(jnp.minimum(z, nz)))
    ok = None
    if masked:
        col = r0 + lax.broadcasted_iota(jnp.int32, (1, kk.shape[0]), 1)
        ok = col < row
        l = jnp.where(ok, l, 0.0)
    return z, l, ok


def _sb_walk(blk, carry, i, tq, ts):
    carry = blk(pl.multiple_of(i * tq, tq), tq, carry, True)

    def live(c):
        t, heads = c
        top = jnp.max(heads[0][0])
        for h in heads[1:]:
            top = jnp.maximum(top, jnp.max(h[0]))
        return jnp.logical_and(t < i * (tq // ts), top > SB_DEAD_LOG)

    def step(c):
        t, heads = c
        return t + 1, blk(pl.multiple_of(i * tq - (t + 1) * ts, ts), ts, heads, False)

    return lax.while_loop(live, step, (jnp.int32(0), carry))[1]


def _sb_fwd(name, pb, *, q_cb, k_cb, v_cb, tq, ts, scale):
    S = pb.shape[0]
    nq = S // tq
    hd = SB_HEAD_DIM
    groups = SB_HEADS * hd // LANES
    nh = LANES // hd

    def body(q_ref, k_ref, v_ref, o16_ref, o32_ref):
        i = pl.program_id(1)
        q_all = q_ref[...]
        row = i * tq + lax.broadcasted_iota(jnp.int32, (tq, 1), 0)
        uo = _suffix_ones()
        sls = [slice(h * hd, (h + 1) * hd) for h in range(nh)]
        qhs = [q_all[:, sl] * scale for sl in sls]

        def blk(r0, width, carry, masked):
            k_all = k_ref[pl.ds(r0, width), :]
            v_all = v_ref[pl.ds(r0, width), :]
            new = []
            for qh, sl, (run_l, acc) in zip(qhs, sls, carry):
                z, l, ok = _sb_logits(qh, k_all[:, sl], r0, row, masked)
                tail, run_l = _suffix_scan(l, uo, run_l)
                e = z + l + tail
                if masked:
                    e = jnp.where(ok, e, NEG)
                new.append((run_l, acc + jnp.dot(jnp.exp(e).astype(BF16), v_all[:, sl], preferred_element_type=F32)))
            return tuple(new)

        carry = tuple((jnp.zeros((tq, LANES), F32), jnp.zeros((tq, hd), F32)) for _ in range(nh))
        carry = _sb_walk(blk, carry, i, tq, ts)
        o = jnp.concatenate([acc for _, acc in carry], axis=1)
        o16_ref[...] = o.astype(BF16)
        o32_ref[...] = o

    block_bytes = 2 * tq * LANES + 2 * 2 * S * LANES + 6 * tq * LANES
    return pl.pallas_call(
        body, name=name, grid=(groups, nq),
        in_specs=[pl.BlockSpec((tq, LANES), lambda g, i: (i, q_cb + g)), pl.BlockSpec((S, LANES), lambda g, i: (0, k_cb + g)),
                  pl.BlockSpec((S, LANES), lambda g, i: (0, v_cb + g))],
        out_specs=[pl.BlockSpec((tq, LANES), lambda g, i: (i, g)), pl.BlockSpec((tq, LANES), lambda g, i: (i, g))],
        out_shape=[jax.ShapeDtypeStruct((S, groups * LANES), BF16), jax.ShapeDtypeStruct((S, groups * LANES), F32)],
        compiler_params=_cparams(("parallel", "arbitrary"), block_bytes),
    )(pb, pb, pb)


def _sb_bwd(name, pb, o32, do, *, q_cb, k_cb, v_cb, tq, ts, scale):
    S = pb.shape[0]
    nq = S // tq
    hd = SB_HEAD_DIM
    groups = SB_HEADS * hd // LANES
    nh = LANES // hd

    def body(q_ref, k_ref, v_ref, o_ref, do_ref, dq_ref, dkt_ref, dvt_ref):
        i = pl.program_id(1)

        @pl.when(i == 0)
        def _():
            dkt_ref[...] = jnp.zeros_like(dkt_ref)
            dvt_ref[...] = jnp.zeros_like(dvt_ref)

        q_all = q_ref[...]
        o_all = o_ref[...]
        do_all = do_ref[...].astype(F32)
        row = i * tq + lax.broadcasted_iota(jnp.int32, (tq, 1), 0)
        uo = _suffix_ones()
        per_head = []
        for h in range(nh):
            sl = slice(h * hd, (h + 1) * hd)
            doh = do_all[:, sl].astype(BF16)
            total = jnp.sum(doh.astype(F32) * o_all[:, sl], axis=1, keepdims=True)
            qh = q_all[:, sl] * scale
            per_head.append((qh, qh.astype(F32).T.astype(BF16), doh, doh.astype(F32).T.astype(BF16), jnp.broadcast_to(total, (tq, LANES)), sl))

        def blk(r0, width, carry, masked):
            k_all = k_ref[pl.ds(r0, width), :]
            v_all = v_ref[pl.ds(r0, width), :]
            new, dk_c, dv_c = [], [], []
            for (qh, qt, doh, dot_, total, sl), (run_l, run_g, dq) in zip(per_head, carry):
                kk, vv = k_all[:, sl], v_all[:, sl]
                z, l, ok = _sb_logits(qh, kk, r0, row, masked)
                tail, run_l = _suffix_scan(l, uo, run_l)
                e = z + l
                beta = jnp.exp(e)
                e = e + tail
                if masked:
                    e = jnp.where(ok, e, NEG)
                a = jnp.exp(e).astype(BF16).astype(F32)
                gr = lax.dot_general(doh, vv, NT, preferred_element_type=F32) * a
                right, run_g = _suffix_scan(gr, uo, run_g)
                nc = width // LANES
                prefix = (total if nc == 1 else jnp.tile(total, (1, nc))) - right
                dz = gr - beta * prefix
                if masked:
                    dz = jnp.where(ok, dz, 0.0)
                dzb = dz.astype(BF16)
                new.append((run_l, run_g, dq + jnp.dot(dzb, kk, preferred_element_type=F32)))
                dk_c.append(jnp.dot(qt, dzb, preferred_element_type=F32))
                dv_c.append(jnp.dot(dot_, a.astype(BF16), preferred_element_type=F32))
            dkt, dvt = jnp.concatenate(dk_c, axis=0), jnp.concatenate(dv_c, axis=0)
            for j in range(width // ts):
                dkt_ref[r0 // ts + j] += dkt[:, j * ts:(j + 1) * ts]
                dvt_ref[r0 // ts + j] += dvt[:, j * ts:(j + 1) * ts]
            return tuple(new)

        zc = jnp.zeros((tq, LANES), F32)
        carry = tuple((zc, zc, jnp.zeros((tq, hd), F32)) for _ in range(nh))
        carry = _sb_walk(blk, carry, i, tq, ts)
        dq_ref[...] = jnp.concatenate([c[2] * scale for c in carry], axis=1)

    W = groups * LANES
    block_bytes = 2 * tq * LANES + 2 * 2 * S * LANES + 3 * 4 * tq * LANES + 2 * 4 * S * LANES
    return pl.pallas_call(
        body, name=name, grid=(groups, nq),
        in_specs=[pl.BlockSpec((tq, LANES), lambda g, i: (i, q_cb + g)), pl.BlockSpec((S, LANES), lambda g, i: (0, k_cb + g)),
                  pl.BlockSpec((S, LANES), lambda g, i: (0, v_cb + g)), pl.BlockSpec((tq, LANES), lambda g, i: (i, g)),
                  pl.BlockSpec((tq, LANES), lambda g, i: (i, g))],
        out_specs=[pl.BlockSpec((tq, LANES), lambda g, i: (i, g)), pl.BlockSpec((S // ts, LANES, ts), lambda g, i: (0, g, 0)),
                   pl.BlockSpec((S // ts, LANES, ts), lambda g, i: (0, g, 0))],
        out_shape=[jax.ShapeDtypeStruct((S, W), F32)] + [jax.ShapeDtypeStruct((S // ts, W, ts), F32)] * 2,
        compiler_params=_cparams(("parallel", "arbitrary"), block_bytes),
    )(pb, pb, pb, o32, do)


def _exchange_copies(src_refs, out_refs, send_sems, recv_sems, local_sems, per_peer):
    x, y, c = lax.axis_index("x"), lax.axis_index("y"), lax.axis_index("c")
    me = 4 * x + 2 * y + c
    n = len(src_refs)

    def copy(j, k):
        px, py, pc = x ^ (k >> 2), y ^ ((k >> 1) & 1), c ^ (k & 1)
        s = src_refs[j].at[4 * px + 2 * py + pc] if per_peer else src_refs[j]
        return pltpu.make_async_remote_copy(
            src_ref=s, dst_ref=out_refs[j].at[me], send_sem=send_sems.at[j, k - 1], recv_sem=recv_sems.at[j, k - 1],
            device_id=(px, py, pc), device_id_type=pl.DeviceIdType.MESH)

    mine = [pltpu.make_async_copy(src_refs[j].at[me] if per_peer else src_refs[j], out_refs[j].at[me], local_sems.at[j]) for j in range(n)]
    return mine, [copy(j, k) for k in range(1, N_DEV) for j in range(n)]


def _exchange_start(*refs, per_peer):
    mine, copies = _exchange_copies(*refs, per_peer)
    for cp in mine + copies:
        cp.start()


def _exchange_wait(*refs, per_peer):
    mine, copies = _exchange_copies(*refs, per_peer)
    for cp in copies:
        cp.wait_recv()
    for cp in copies:
        cp.wait_send()
    for cp in mine:
        cp.wait()


def _exchange_specs(srcs, per_peer):
    n = len(srcs)
    out_shape = [jax.ShapeDtypeStruct((N_DEV,) + tuple(s.shape[1:] if per_peer else s.shape), s.dtype) for s in srcs]
    sems = [pltpu.SemaphoreType.DMA((n, N_DEV - 1)), pltpu.SemaphoreType.DMA((n, N_DEV - 1)), pltpu.SemaphoreType.DMA((n,))]
    return [pl.BlockSpec(memory_space=pltpu.HBM)] * n, out_shape, sems


def _exchange(name, srcs, per_peer):
    n = len(srcs)

    def body(*refs):
        parts = (refs[:n], refs[n:2 * n], *refs[2 * n:])
        _exchange_start(*parts, per_peer=per_peer)
        _exchange_wait(*parts, per_peer=per_peer)

    hbm, out_shape, sems = _exchange_specs(srcs, per_peer)
    return pl.pallas_call(body, name=name, in_specs=hbm, out_specs=hbm, out_shape=out_shape, scratch_shapes=sems)(*srcs)


def _pallas(body, *, name, grid, in_specs, out_specs, out_shape, args, sem, block_bytes, ride=None):
    if ride is None:
        return pl.pallas_call(body, name=name, grid=grid, in_specs=in_specs, out_specs=out_specs, out_shape=out_shape,
                              compiler_params=_cparams(sem, block_bytes))(*args), None
    srcs, per_peer = ride
    n, n_in, n_out = len(srcs), len(in_specs), len(out_specs)

    def riding(*refs):
        ins, xsrc = refs[:n_in], refs[n_in:n_in + n]
        outs, xout = refs[n_in + n:n_in + n + n_out], refs[n_in + n + n_out:n_in + 2 * n + n_out]
        parts = (xsrc, xout, *refs[n_in + 2 * n + n_out:])
        ids = [pl.program_id(d) for d in range(len(grid))]
        first = functools.reduce(jnp.logical_and, [i == 0 for i in ids])
        last = functools.reduce(jnp.logical_and, [i == g - 1 for i, g in zip(ids, grid)])
        pl.when(first)(functools.partial(_exchange_start, *parts, per_peer=per_peer))
        body(*ins, *outs)
        pl.when(last)(functools.partial(_exchange_wait, *parts, per_peer=per_peer))

    hbm, x_shape, sems = _exchange_specs(srcs, per_peer)
    res = pl.pallas_call(riding, name=name, grid=grid, in_specs=list(in_specs) + hbm, out_specs=list(out_specs) + hbm,
                         out_shape=list(out_shape) + x_shape, scratch_shapes=sems,
                         compiler_params=_cparams(("arbitrary",) * len(grid), block_bytes))(*args, *srcs)
    return res[:n_out], res[n_out:]


def _adamw(name, parts, w, m, v, *, tile):
    R, C = w.shape

    def body(p_ref, w_ref, m_ref, v_ref, g_ref, d_ref, nm_ref, nv_ref):
        g = p_ref[0].astype(F32)
        for d in range(1, N_DEV):
            g = g + p_ref[d].astype(F32)
        wv = w_ref[...]
        mm = ADAM_B1 * m_ref[...] + (1.0 - ADAM_B1) * g
        vv = ADAM_B2 * v_ref[...] + (1.0 - ADAM_B2) * jnp.square(g)
        m_hat = mm / (1.0 - ADAM_B1 ** ADAM_STEP)
        v_hat = vv / (1.0 - ADAM_B2 ** ADAM_STEP)
        g_ref[...] = g
        d_ref[...] = -ADAM_LR * (m_hat / (jnp.sqrt(v_hat) + ADAM_EPS) + ADAM_WD * wv)
        nm_ref[...] = mm
        nv_ref[...] = vv

    blk = pl.BlockSpec((tile, C), lambda i: (i, 0))
    block_bytes = N_DEV * _nbytes((tile, C), parts.dtype) + 7 * _nbytes((tile, C), F32)
    return pl.pallas_call(
        body, name=name, grid=(R // tile,),
        in_specs=[pl.BlockSpec((N_DEV, tile, C), lambda i: (0, i, 0)), blk, blk, blk],
        out_specs=[blk] * 4, out_shape=[jax.ShapeDtypeStruct((R, C), F32)] * 4,
        compiler_params=_cparams(("parallel",), block_bytes),
    )(parts, w, m, v)


def _perm_w_in(w):
    cols = [w[:, ORIG_COLS[n][0]:ORIG_COLS[n][0] + ORIG_COLS[n][1]] for n in PERM_ORDER]
    cols.append(jnp.zeros((w.shape[0], IN_WIDTH_P - IN_WIDTH), w.dtype))
    return jnp.concatenate(cols, axis=1)


def _unperm_w_in(wp):
    order = sorted(ORIG_COLS, key=lambda n: ORIG_COLS[n][0])
    return jnp.concatenate([wp[:, PERM_OFF[n]:PERM_OFF[n] + ORIG_COLS[n][1]] for n in order], axis=1)


def _perm_w_uq(w):
    w3 = w.reshape(w.shape[0], MLA_HEADS, MLA_NOPE + MLA_ROPE)
    return jnp.pad(w3, ((0, 0), (0, 0), (0, LANES - MLA_NOPE - MLA_ROPE))).reshape(w.shape[0], MLA_QW)


def _unperm_w_uq(wp):
    return wp.reshape(wp.shape[0], MLA_HEADS, LANES)[:, :, :MLA_NOPE + MLA_ROPE].reshape(wp.shape[0], -1)


def _perm_w_ukv(w):
    w3 = w.reshape(w.shape[0], MLA_HEADS, MLA_NOPE + MLA_V)
    kp = jnp.pad(w3[:, :, :MLA_NOPE], ((0, 0), (0, 0), (0, LANES - MLA_NOPE))).reshape(w.shape[0], MLA_HEADS * LANES)
    return jnp.concatenate([kp, w3[:, :, MLA_NOPE:].reshape(w.shape[0], MLA_HEADS * MLA_V)], axis=1)


def _unperm_w_ukv(wp):
    n = wp.shape[0]
    kp = wp[:, :MLA_HEADS * LANES].reshape(n, MLA_HEADS, LANES)[:, :, :MLA_NOPE]
    vp = wp[:, MLA_HEADS * LANES:].reshape(n, MLA_HEADS, MLA_V)
    return jnp.concatenate([kp, vp], axis=2).reshape(n, MLA_HEADS * (MLA_NOPE + MLA_V))


BIG = ("w_in", "w_uq", "w_ukv", "w_o_mla", "w_o_swa", "w_o_sb", "w_out", "w_up", "w_down")
W_EARLY = ("w_in", "w_uq", "w_ukv")
W_LATE = ("w_o_mla", "w_o_swa", "w_o_sb", "w_out", "w_up", "w_down")
ROW_SHARDED = ("w_out", "w_down")
SMALL = ("g_mix_pre", "b_gate", "g_q_lat", "g_kv_lat", "swa_sinks", "g_mix_post", "g_mlp_pre", "g_mlp_post")
PACK_COLS = 1024


def _to_shards(name, full):
    L, R, C = full.shape
    if name in ROW_SHARDED:
        return full.reshape(L, N_DEV, R // N_DEV, C).transpose(1, 0, 2, 3).reshape(N_DEV, L * R // N_DEV, C)
    return full.reshape(L, R, N_DEV, C // N_DEV).transpose(2, 0, 1, 3).reshape(N_DEV, L * R, C // N_DEV)


def _from_shards(name, gathered, shard_shape):
    L, r, c = shard_shape
    a = gathered.reshape(N_DEV, L, r, c)
    if name in ROW_SHARDED:
        return a.transpose(1, 0, 2, 3).reshape(L, N_DEV * r, c)
    return a.transpose(1, 2, 0, 3).reshape(L, r, N_DEV * c)


def _tables(positions):
    pos = positions.astype(F32).reshape(-1, 1)

    def cs(d):
        inv = 1.0 / (ROPE_THETA ** (jnp.arange(0, d, 2, dtype=F32) / d))
        ang = pos * inv
        c, s = jnp.cos(ang), jnp.sin(ang)
        return jnp.concatenate([c, c], axis=1), jnp.concatenate([-s, s], axis=1)

    c64, s64 = cs(SWA_HEAD_DIM)
    c32, s32 = cs(MLA_ROPE)
    n = pos.shape[0]
    one, zero = jnp.ones((n, MLA_NOPE), F32), jnp.zeros((n, MLA_NOPE), F32)
    pad0 = jnp.zeros((n, LANES - MLA_NOPE - MLA_ROPE), F32)
    cq = jnp.concatenate([one, c32, pad0], axis=1)
    sq = jnp.concatenate([zero, s32, pad0], axis=1)
    padk = jnp.zeros((n, LANES - MLA_ROPE), F32)
    return dict(
        c_swa_q=jnp.tile(c64, (1, SWA_HEADS)), s_swa_q=jnp.tile(s64, (1, SWA_HEADS)),
        c_swa_k=jnp.tile(c64, (1, SWA_KV_HEADS)), s_swa_k=jnp.tile(s64, (1, SWA_KV_HEADS)),
        c_mla_q=jnp.tile(cq, (1, MLA_HEADS)), s_mla_q=jnp.tile(sq, (1, MLA_HEADS)),
        c_mla_k=jnp.concatenate([c32, padk], axis=1), s_mla_k=jnp.concatenate([s32, padk], axis=1))


def _cb(name):
    return PERM_OFF[name] // PERM_WIDTH[name]


MLA_SCALE = (MLA_NOPE + MLA_ROPE) ** -0.5
MLA_FWD_TQ, MLA_FWD_TK = 1024, 1024
MLA_SPEC = dict(groups=MLA_HEADS // 2, tq=512, tk=512,
                heads=[(slice(h * LANES, (h + 1) * LANES), slice(h * LANES, (h + 1) * LANES), slice(h * MLA_V, (h + 1) * MLA_V)) for h in range(2)])
SWA_G = SWA_HEADS // SWA_KV_HEADS
SWA_TQ = 512
SB_SPEC = dict(tq=512, ts=256, scale=SB_HEAD_DIM ** -0.5, q_cb=PERM_OFF["q_sb"] // LANES, k_cb=PERM_OFF["k_sb"] // LANES,
               v_cb=PERM_OFF["v_sb"] // LANES)


def _mla_specs(S, tq):
    return dict(q_spec=((tq, 2 * LANES), lambda g, i: (i, g)), k_spec=((S, 2 * LANES), lambda g, i: (0, g)),
                v_spec=((S, 2 * MLA_V), lambda g, i: (0, g)))


def _layer_fwd(l, x, W, P, tb, T, ride, late):
    S = x.shape[0]
    nt = S // T
    h, = _rowwise(f"l{l}_norm_in", lambda xv, g: (_rms(xv, g),), [_ri(x, T), _bi(P["g_mix_pre"])], [_ro(S, D_MODEL, BF16, T)], n_tiles=nt)
    proj, proj16 = _mm_nn(f"l{l}_mm_in", h, W["w_in"], [F32, BF16], tm=512, tn=IN_WIDTH_P // 2)

    def mix_prep(cq, ckv, qs, ks, vs, gq, gkv, cq_t, sq_t, ck_t, sk_t):
        ksr = _rope(ks, ck_t, sk_t, SWA_HEAD_DIM // 2)
        hd = SWA_HEAD_DIM
        return (_rms(cq, gq), _rms(ckv, gkv), _rope(qs, cq_t, sq_t, hd // 2),
                [ksr[:, :hd], ksr[:, hd:]], [vs[:, :hd], vs[:, hd:]])

    kv3 = lambda dt: ((SWA_KV_HEADS, S, SWA_HEAD_DIM), dt, (SWA_KV_HEADS, T, SWA_HEAD_DIM), lambda i: (0, i, 0))
    cqn, ckvn, q_swa, k_swa, v_swa = _rowwise(
        f"l{l}_mix_prep", mix_prep,
        [_ri(proj, T, 256, _cb("c_q")), _ri(proj, T, 128, _cb("c_kv")), _ri(proj, T, 512, _cb("q_swa")), _ri(proj, T, 128, _cb("k_swa")),
         _ri(proj, T, 128, _cb("v_swa")), _bi(P["g_q_lat"]), _bi(P["g_kv_lat"]), _ri(tb["c_swa_q"], T), _ri(tb["s_swa_q"], T),
         _ri(tb["c_swa_k"], T), _ri(tb["s_swa_k"], T)],
        [_ro(S, 256, BF16, T), _ro(S, 128, BF16, T), _ro(S, 512, BF16, T), kv3(BF16), kv3(BF16)], n_tiles=nt)
    k_swa = k_swa.reshape(SWA_KV_HEADS * S, SWA_HEAD_DIM)
    v_swa = v_swa.reshape(SWA_KV_HEADS * S, SWA_HEAD_DIM)
    q_lat, = _mm_nn(f"l{l}_mm_uq", cqn, W["w_uq"], [F32], tm=1024, tn=MLA_QW)
    kv_lat, = _mm_nn(f"l{l}_mm_ukv", ckvn, W["w_ukv"], [F32], tm=1024, tn=MLA_KVW)

    def mla_prep(q, kk, vv, kr, cq_t, sq_t, ck_t, sk_t):
        kpe = pltpu.roll(_rope(kr, ck_t, sk_t, MLA_ROPE // 2), MLA_NOPE, axis=1)
        return _rope(q, cq_t, sq_t, MLA_ROPE // 2) * MLA_SCALE, kk + jnp.tile(kpe, (1, MLA_HEADS)), vv

    q_mla, k_mla, v_mla = _rowwise(
        f"l{l}_mla_prep", mla_prep,
        [_ri(q_lat, T), _ri(kv_lat, T, MLA_HEADS * LANES, 0), _ri(kv_lat, T, MLA_HEADS * MLA_V, 2), _ri(proj, T, 128, _cb("k_rope")),
         _ri(tb["c_mla_q"], T), _ri(tb["s_mla_q"], T), _ri(tb["c_mla_k"], T), _ri(tb["s_mla_k"], T)],
        [_ro(S, MLA_QW, BF16, T), _ro(S, MLA_HEADS * LANES, BF16, T), _ro(S, MLA_HEADS * MLA_V, BF16, T)], n_tiles=nt)

    (att_a, lse_a), rode = _softmax_fwd(f"l{l}_mla_fwd", q_mla, k_mla, v_mla, o_width=MLA_HEADS * MLA_V, ride=ride,
                                        **{**MLA_SPEC, "tq": min(MLA_FWD_TQ, S), "tk": min(MLA_FWD_TK, S)}, **_mla_specs(S, min(MLA_FWD_TQ, S)))
    W = {**W, **late(rode)}
    att_b, lse_b = _swa_fwd(f"l{l}_swa_fwd", q_swa, k_swa, v_swa, P["swa_sinks"], tq=SWA_TQ)
    att_c, att_c32 = _sb_fwd(f"l{l}_sb_fwd", proj16, **SB_SPEC)
    o_a, = _mm_nn(f"l{l}_mm_oa", att_a, W["w_o_mla"], [F32], tm=1024, tn=D_MODEL)
    o_b, = _mm_nn(f"l{l}_mm_ob", att_b, W["w_o_swa"], [F32], tm=1024, tn=D_MODEL)
    o_c, = _mm_nn(f"l{l}_mm_oc", att_c, W["w_o_sb"], [F32], tm=1024, tn=D_MODEL)

    def gate_mix(gl, b, oa, ob, oc):
        gt = jax.nn.sigmoid(gl + b)
        return (gt[:, :D_MODEL] * oa + gt[:, D_MODEL:2 * D_MODEL] * ob + gt[:, 2 * D_MODEL:] * oc,)

    mixed, = _rowwise(f"l{l}_gate_mix", gate_mix, [_ri(proj, T, 3072, 0), _bi(P["b_gate"]), _ri(o_a, T), _ri(o_b, T), _ri(o_c, T)],
                      [_ro(S, D_MODEL, BF16, T)], n_tiles=nt)
    y, = _mm_nn(f"l{l}_mm_out", mixed, W["w_out"], [F32], tm=1024, tn=D_MODEL)

    def resid_norm(xv, yv, gpost, gpre):
        x1 = xv + _rms(yv, gpost)
        return x1, _rms(x1, gpre)

    x1, h2 = _rowwise(f"l{l}_resid_norm", resid_norm, [_ri(x, T), _ri(y, T), _bi(P["g_mix_post"]), _bi(P["g_mlp_pre"])],
                      [_ro(S, D_MODEL, F32, T), _ro(S, D_MODEL, BF16, T)], n_tiles=nt)
    up, u = _mm_nn(f"l{l}_mm_up", h2, W["w_up"], [F32, BF16], tm=512, tn=2048,
                   epilogue=lambda acc: (acc, jnp.square(jnp.maximum(acc, 0.0))))
    dn, = _mm_nn(f"l{l}_mm_down", u, W["w_down"], [F32], tm=512, tn=D_MODEL)
    x2, = _rowwise(f"l{l}_resid_out", lambda xv, dv, g: (xv + _rms(dv, g),), [_ri(x1, T), _ri(dn, T), _bi(P["g_mlp_post"])],
                   [_ro(S, D_MODEL, F32, T)], n_tiles=nt)
    saved = dict(x=x, h=h, proj=proj, proj16=proj16, cqn=cqn, ckvn=ckvn, q_swa=q_swa, k_swa=k_swa, v_swa=v_swa, q_mla=q_mla, k_mla=k_mla,
                 v_mla=v_mla, att_a=att_a, lse_a=lse_a, att_b=att_b, lse_b=lse_b, att_c=att_c, att_c32=att_c32, o_a=o_a, o_b=o_b, o_c=o_c,
                 mixed=mixed, y=y, x1=x1, h2=h2, up=up, u=u, dn=dn)
    return x2, saved, W, rode


def _layer_bwd(l, dx2, sv, W, P, tb, T, pending, to_send):
    S = dx2.shape[0]
    nt = S // T
    G = {}

    def post_norm_bwd(v, g, dy):
        return _rms_bwd(v, g, dy)

    d_dn, G["g_mlp_post"] = _rowwise(f"l{l}_b_post2", post_norm_bwd, [_ri(sv["dn"], T), _bi(P["g_mlp_post"]), _ri(dx2, T)],
                                    [_ro(S, D_MODEL, BF16, T)], [D_MODEL], n_tiles=nt)
    d_up, = _mm_nt(f"l{l}_b_mm_down", d_dn, W["w_down"], [BF16], tm=512, tn=2048, extras=[sv["up"]],
                   epilogue=lambda acc, upv: (acc * (2.0 * jnp.maximum(upv, 0.0)),))
    G["w_down"] = _mm_tn(f"l{l}_g_down", sv["u"], d_dn, tm=2048, tn=D_MODEL, ts=512)
    d_h2, = _mm_nt(f"l{l}_b_mm_up", d_up, W["w_up"], [F32], tm=512, tn=D_MODEL)
    G["w_up"] = _mm_tn(f"l{l}_g_up", sv["h2"], d_up, tm=D_MODEL, tn=2048, ts=512)

    def pre_norm_bwd(v, g, dy, dres):
        dx, dg = _rms_bwd(v, g, dy)
        return dres + dx, dg

    def mid_norms_bwd(x1v, gpre, dh2, dres, yv, gpost):
        dx1v, dgpre = pre_norm_bwd(x1v, gpre, dh2, dres)
        dyv, dgpost = post_norm_bwd(yv, gpost, dx1v)
        return dx1v, dyv, dgpre, dgpost

    dx1, d_y, G["g_mlp_pre"], G["g_mix_post"] = _rowwise(
        f"l{l}_b_mid_norms", mid_norms_bwd,
        [_ri(sv["x1"], T), _bi(P["g_mlp_pre"]), _ri(d_h2, T), _ri(dx2, T), _ri(sv["y"], T), _bi(P["g_mix_post"])],
        [_ro(S, D_MODEL, F32, T), _ro(S, D_MODEL, BF16, T)], [D_MODEL, D_MODEL], n_tiles=nt)
    d_mixed, = _mm_nt(f"l{l}_b_mm_out", d_y, W["w_out"], [F32], tm=1024, tn=D_MODEL)
    G["w_out"] = _mm_tn(f"l{l}_g_out", sv["mixed"], d_y, tm=D_MODEL, tn=D_MODEL, ts=512)

    def gate_bwd(dm, gl, b, oa, ob, oc):
        gt = jax.nn.sigmoid(gl + b)
        outs, dgl = [], []
        for k, o in enumerate((oa, ob, oc)):
            gk = gt[:, k * D_MODEL:(k + 1) * D_MODEL]
            outs.append(dm * gk)
            dgl.append(dm * o * gk * (1.0 - gk))
        dgl = jnp.concatenate(dgl, axis=1)
        return (*outs, dgl, jnp.sum(dgl, axis=0, keepdims=True))

    d_oa, d_ob, d_oc, d_gl, G["b_gate"] = _rowwise(
        f"l{l}_b_gate", gate_bwd, [_ri(d_mixed, T), _ri(sv["proj"], T, 3072, 0), _bi(P["b_gate"]), _ri(sv["o_a"], T), _ri(sv["o_b"], T), _ri(sv["o_c"], T)],
        [_ro(S, D_MODEL, BF16, T)] * 3 + [_ro(S, 3 * D_MODEL, BF16, T)], [3 * D_MODEL], n_tiles=nt)
    d_att = {}
    for br, d_o, att in (("mla", d_oa, sv["att_a"]), ("swa", d_ob, sv["att_b"]), ("sb", d_oc, sv["att_c"])):
        d_att[br], = _mm_nt(f"l{l}_b_mm_o_{br}", d_o, W["w_o_" + br], [F32], tm=1024, tn=512)
        G["w_o_" + br] = _mm_tn(f"l{l}_g_o_{br}", att, d_o, tm=512, tn=D_MODEL, ts=512)

    riders = list(pending) + [(n, l, to_send(n, G[n])) for n in W_LATE]
    (dq_mla, dkt_mla, dvt_mla), rode = _softmax_bwd(f"l{l}_mla_bwd", sv["q_mla"], sv["k_mla"], sv["v_mla"], sv["att_a"], d_att["mla"], sv["lse_a"],
                                                  scale=MLA_SCALE, ride=([s for _, _, s in riders], True), **MLA_SPEC, **_mla_specs(S, MLA_SPEC["tq"]))
    rode = [(n, ly, r) for (n, ly, _), r in zip(riders, rode)]
    dk_mla = dkt_mla.transpose(0, 2, 1).reshape(S, -1)
    dv_mla = dvt_mla.transpose(0, 2, 1).reshape(S, -1)
    dq_swa, dkt_swa, dvt_swa, dsink = _swa_bwd(f"l{l}_swa_bwd", sv["q_swa"], sv["k_swa"], sv["v_swa"], sv["att_b"], d_att["swa"], sv["lse_b"],
                                               P["swa_sinks"], tq=SWA_TQ)
    dk_swa = dkt_swa.transpose(0, 2, 1).reshape(SWA_KV_HEADS * S, SWA_HEAD_DIM)
    dv_swa = dvt_swa.transpose(0, 2, 1).reshape(SWA_KV_HEADS * S, SWA_HEAD_DIM)
    G["swa_sinks"] = dsink[:, :SWA_G, 0].reshape(1, SWA_HEADS)
    dq_sb, dkt_sb, dvt_sb = _sb_bwd(f"l{l}_sb_bwd", sv["proj16"], sv["att_c32"], d_att["sb"], **SB_SPEC)
    dk_sb = dkt_sb.transpose(0, 2, 1).reshape(S, -1)
    dv_sb = dvt_sb.transpose(0, 2, 1).reshape(S, -1)

    def mla_prep_bwd(dq, dk, dvv, cq_t, sq_t, ck_t, sk_t):
        dks = dk[:, :LANES]
        for hh in range(1, MLA_HEADS):
            dks = dks + dk[:, hh * LANES:(hh + 1) * LANES]
        d_kr = _rope_t(pltpu.roll(dks, LANES - MLA_NOPE, axis=1), ck_t, sk_t, MLA_ROPE // 2)
        return _rope_t(dq, cq_t, sq_t, MLA_ROPE // 2), jnp.concatenate([dk, dvv], axis=1), d_kr

    d_q_lat, d_kv_lat, d_krope = _rowwise(
        f"l{l}_b_mla_prep", mla_prep_bwd,
        [_ri(dq_mla, T), _ri(dk_mla, T), _ri(dv_mla, T), _ri(tb["c_mla_q"], T), _ri(tb["s_mla_q"], T), _ri(tb["c_mla_k"], T), _ri(tb["s_mla_k"], T)],
        [_ro(S, MLA_QW, BF16, T), _ro(S, MLA_KVW, BF16, T), _ro(S, LANES, BF16, T)], n_tiles=nt)
    d_cqn, = _mm_nt(f"l{l}_b_mm_uq", d_q_lat, W["w_uq"], [F32], tm=1024, tn=MLA_Q_LORA)
    G["w_uq"] = _mm_tn(f"l{l}_g_uq", sv["cqn"], d_q_lat, tm=MLA_Q_LORA, tn=MLA_QW, ts=512)
    d_ckvn, = _mm_nt(f"l{l}_b_mm_ukv", d_kv_lat, W["w_ukv"], [F32], tm=1024, tn=MLA_KV_LORA)
    G["w_ukv"] = _mm_tn(f"l{l}_g_ukv", sv["ckvn"], d_kv_lat, tm=MLA_KV_LORA, tn=MLA_KVW, ts=512)

    def mix_prep_bwd(cq, ckv, gq, gkv, dcqn, dckvn, dqs, dks, dvs, cq_t, sq_t, ck_t, sk_t):
        d_cq, dgq = _rms_bwd(cq, gq, dcqn)
        d_ckv, dgkv = _rms_bwd(ckv, gkv, dckvn)
        dk2 = jnp.concatenate([dks[0], dks[1]], axis=1)
        dv2 = jnp.concatenate([dvs[0], dvs[1]], axis=1)
        return (d_cq, d_ckv, _rope_t(dqs, cq_t, sq_t, SWA_HEAD_DIM // 2), _rope_t(dk2, ck_t, sk_t, SWA_HEAD_DIM // 2), dv2, dgq, dgkv)

    kv3 = lambda a: (a.reshape(SWA_KV_HEADS, S, SWA_HEAD_DIM), (SWA_KV_HEADS, T, SWA_HEAD_DIM), lambda i: (0, i, 0))
    d_cq, d_ckv, d_qswa, d_kswa, d_vswa, G["g_q_lat"], G["g_kv_lat"] = _rowwise(
        f"l{l}_b_mix_prep", mix_prep_bwd,
        [_ri(sv["proj"], T, 256, _cb("c_q")), _ri(sv["proj"], T, 128, _cb("c_kv")), _bi(P["g_q_lat"]), _bi(P["g_kv_lat"]), _ri(d_cqn, T), _ri(d_ckvn, T),
         _ri(dq_swa, T), kv3(dk_swa), kv3(dv_swa), _ri(tb["c_swa_q"], T), _ri(tb["s_swa_q"], T), _ri(tb["c_swa_k"], T), _ri(tb["s_swa_k"], T)],
        [_ro(S, 256, BF16, T), _ro(S, 128, BF16, T), _ro(S, 512, BF16, T), _ro(S, 128, BF16, T), _ro(S, 128, BF16, T)], [256, 128], n_tiles=nt)
    pieces = dict(gates=d_gl, q_swa=d_qswa, q_sb=dq_sb, k_sb=dk_sb, v_sb=dv_sb, c_q=d_cq, c_kv=d_ckv, k_swa=d_kswa, v_swa=d_vswa, k_rope=d_krope)
    d_proj = jnp.concatenate([pieces[n].astype(BF16) for n in PERM_ORDER], axis=1)
    d_h, = _mm_nt(f"l{l}_b_mm_in", d_proj, W["w_in"], [F32], tm=512, tn=512)
    G["w_in"] = _mm_tn(f"l{l}_g_in", sv["h"], d_proj, tm=D_MODEL, tn=IN_WIDTH_P // 2, ts=512)
    dx, G["g_mix_pre"] = _rowwise(f"l{l}_b_pre1", pre_norm_bwd, [_ri(sv["x"], T), _bi(P["g_mix_pre"]), _ri(d_h, T), _ri(dx1, T)],
                                 [_ro(S, D_MODEL, F32, T)], [D_MODEL], n_tiles=nt)
    return dx, G, rode


def _pack_rows(vecs, rows):
    flat = jnp.concatenate([v.reshape(-1) for v in vecs])
    return jnp.pad(flat, (0, rows * PACK_COLS - flat.shape[0])).reshape(rows, PACK_COLS)


def kernel(x, positions, g_mix_pre, w_in, b_gate, g_q_lat, g_kv_lat, w_uq, w_ukv, swa_sinks, w_o_mla, w_o_swa, w_o_sb, w_out, g_mix_post, g_mlp_pre, w_up, w_down, g_mlp_post, loss_target, m_g_mix_pre, m_w_in, m_b_gate, m_g_q_lat, m_g_kv_lat, m_w_uq, m_w_ukv, m_swa_sinks, m_w_o_mla, m_w_o_swa, m_w_o_sb, m_w_out, m_g_mix_post, m_g_mlp_pre, m_w_up, m_w_down, m_g_mlp_post, v_g_mix_pre, v_w_in, v_b_gate, v_g_q_lat, v_g_kv_lat, v_w_uq, v_w_ukv, v_swa_sinks, v_w_o_mla, v_w_o_swa, v_w_o_sb, v_w_out, v_g_mix_post, v_g_mlp_pre, v_w_up, v_w_down, v_g_mlp_post):
    a = dict(locals())
    S = x.shape[1]
    depth = w_in.shape[0]
    T = min(ROW_TILE, S)
    xs = x.reshape(S, D_MODEL)
    tb = _tables(positions)

    perm = dict(w_in=_perm_w_in, w_uq=_perm_w_uq, w_ukv=_perm_w_ukv)

    def shards(l, names):
        return [a[n][l].astype(BF16) for n in names]

    def whole(names, gathered):
        return {n: perm.get(n, lambda t: t)(_from_shards(n, g, (1,) + a[n].shape[1:])[0]) for n, g in zip(names, gathered)}

    early = whole(W_EARLY, _exchange("gather_weights", shards(0, W_EARLY), per_peer=False))
    layers, saved = [], []
    h = xs
    for l in range(depth):
        P = {n: a[n][l].reshape(1, -1) for n in SMALL if n != "swa_sinks"}
        P["swa_sinks"] = a["swa_sinks"][l]
        srcs = shards(l, W_LATE) + (shards(l + 1, W_EARLY) if l + 1 < depth else [])
        h, sv, W, rode = _layer_fwd(l, h, early, P, tb, T, (srcs, False), lambda r: whole(W_LATE, r[:len(W_LATE)]))
        early = whole(W_EARLY, rode[len(W_LATE):])
        layers.append((W, P))
        saved.append(sv)

    def loss_head(yv, tv):
        err = yv - tv
        part = 0.5 * jnp.sum(jnp.mean(err * err, axis=1, keepdims=True), axis=0, keepdims=True)
        return err * (1.0 / D_MODEL), jnp.broadcast_to(part, (1, LANES))

    dh, loss_row = _rowwise("loss_head", loss_head, [_ri(h, T), _ri(loss_target.reshape(S, D_MODEL), T)], [_ro(S, D_MODEL, F32, T)], [LANES],
                            n_tiles=S // T)
    loss = lax.psum(loss_row[0, 0], ("x", "y", "c"))

    unperm = dict(w_in=_unperm_w_in, w_uq=_unperm_w_uq, w_ukv=_unperm_w_ukv)

    def to_send(n, g):
        return _to_shards(n, unperm.get(n, lambda t: t)(g).astype(BF16)[None])

    grads, recv, pending = [None] * depth, {n: [None] * depth for n in BIG}, []
    for l in reversed(range(depth)):
        W, P = layers[l]
        dh, grads[l], rode = _layer_bwd(l, dh, saved[l], W, P, tb, T, pending, to_send)
        for n, ly, r in rode:
            recv[n][ly] = r
        pending = [(n, l, to_send(n, grads[l][n])) for n in W_EARLY]
    for (n, ly, _), r in zip(pending, _exchange("scatter_grads", [s for _, _, s in pending], per_peer=True)):
        recv[n][ly] = r
    grad_x = dh.reshape(x.shape)

    out = {}
    for n in BIG:
        shp = a[n].shape
        rows, cols = shp[0] * shp[1], shp[2]
        parts = jnp.concatenate(recv[n], axis=1)
        res = _adamw("adamw_" + n, parts, a[n].reshape(rows, cols), a["m_" + n].reshape(rows, cols), a["v_" + n].reshape(rows, cols),
                     tile=min(256, rows))
        out[n] = [r.reshape(shp) for r in res]

    small_total = sum(a[n].size for n in SMALL)
    small_rows = -(-small_total // (8 * PACK_COLS)) * 8
    sg = _pack_rows([jnp.stack([grads[l][n].reshape(-1) for l in range(depth)]) for n in SMALL], small_rows)
    sg_all, = _exchange("gather_small_grads", [sg], per_peer=False)
    res = _adamw("adamw_small", sg_all, _pack_rows([a[n] for n in SMALL], small_rows), _pack_rows([a["m_" + n] for n in SMALL], small_rows),
                 _pack_rows([a["v_" + n] for n in SMALL], small_rows), tile=small_rows)
    off = 0
    for n in SMALL:
        cnt = a[n].size
        out[n] = [r.reshape(-1)[off:off + cnt].reshape(a[n].shape) for r in res]
        off += cnt

    order = ("g_mix_pre", "w_in", "b_gate", "g_q_lat", "g_kv_lat", "w_uq", "w_ukv", "swa_sinks", "w_o_mla", "w_o_swa", "w_o_sb", "w_out",
             "g_mix_post", "g_mlp_pre", "w_up", "w_down", "g_mlp_post")
    return (loss, grad_x, *[out[n][0] for n in order], *[out[n][1] for n in order], *[out[n][2] for n in order], *[out[n][3] for n in order])
```

```python
import functools

import jax
import jax.numpy as jnp
from jax import lax
from jax.experimental import pallas as pl
from jax.experimental.pallas import tpu as pltpu

F32, BF16 = jnp.float32, jnp.bfloat16

D_MODEL = 1024
DEPTH = 4
MLA_HEADS, MLA_Q_LORA, MLA_KV_LORA, MLA_NOPE, MLA_ROPE, MLA_V = 8, 256, 128, 64, 32, 64
SWA_HEADS, SWA_KV_HEADS, SWA_HEAD_DIM, SWA_WINDOW = 8, 2, 64, 128
SB_HEADS, SB_HEAD_DIM = 8, 64
D_FF = 4 * D_MODEL
ROPE_THETA = 10000.0
EPS = 1e-6
N_DEV = 8
ADAM_LR, ADAM_B1, ADAM_B2, ADAM_EPS, ADAM_WD, ADAM_STEP = 0.001, 0.9, 0.999, 1e-08, 0.01, 10

LANES = 128
VMEM_LIMIT_MAX = 60 * 1024 * 1024
VMEM_LIMIT_MIN = 32 * 1024 * 1024

ORIG_COLS = dict(c_q=(0, 256), c_kv=(256, 128), k_rope=(384, 32), q_swa=(416, 512), k_swa=(928, 128), v_swa=(1056, 128),
                 q_sb=(1184, 512), k_sb=(1696, 512), v_sb=(2208, 512), gates=(2720, 3072))
IN_WIDTH = 5792
PERM_ORDER = ("gates", "q_swa", "q_sb", "k_sb", "v_sb", "c_q", "c_kv", "k_swa", "v_swa", "k_rope")
PERM_WIDTH = dict(gates=3072, q_swa=512, q_sb=512, k_sb=512, v_sb=512, c_q=256, c_kv=128, k_swa=128, v_swa=128, k_rope=128)
PERM_OFF = {}
_o = 0
for _n in PERM_ORDER:
    PERM_OFF[_n] = _o
    _o += PERM_WIDTH[_n]
IN_WIDTH_P = _o
MLA_QW = MLA_HEADS * LANES
MLA_KVW = MLA_HEADS * LANES + MLA_HEADS * MLA_V

ROW_TILE = 512
NT = (((1,), (1,)), ((), ()))
TN = (((0,), (0,)), ((), ()))
NEG = -1e30


def _cparams(sem, block_bytes):
    limit = int(min(VMEM_LIMIT_MAX, max(VMEM_LIMIT_MIN, 2 * block_bytes + (16 << 20))))
    return pltpu.CompilerParams(dimension_semantics=sem, vmem_limit_bytes=limit)


def _nbytes(shape, dtype):
    n = 1
    for s in shape:
        n *= s
    return n * jnp.dtype(dtype).itemsize


def _ri(arr, tile, width=None, cb=0):
    width = arr.shape[1] if width is None else width
    return (arr, (tile, width), lambda i, cb=cb: (i, cb))


def _bi(arr):
    return (arr, arr.shape, lambda i: (0, 0))


def _ro(rows, width, dtype, tile):
    return ((rows, width), dtype, (tile, width), lambda i: (i, 0))


def _rowwise(name, fn, ins, outs, reds=(), *, n_tiles):
    n_in, n_out = len(ins), len(outs)

    def body(*refs):
        vals = fn(*[r[...] for r in refs[:n_in]])
        for r, v in zip(refs[n_in:n_in + n_out], vals[:n_out]):
            if isinstance(v, (list, tuple)):
                for j, vj in enumerate(v):
                    r[j] = vj.astype(r.dtype)
            else:
                r[...] = v.astype(r.dtype)
        if reds:
            @pl.when(pl.program_id(0) == 0)
            def _():
                for r in refs[n_in + n_out:]:
                    r[...] = jnp.zeros_like(r)
            for r, v in zip(refs[n_in + n_out:], vals[n_out:]):
                r[...] += v

    block_bytes = sum(_nbytes(b, a.dtype) for a, b, _ in ins) + sum(_nbytes(b, d) for _, d, b, _ in outs)
    res = pl.pallas_call(
        body, name=name, grid=(n_tiles,),
        in_specs=[pl.BlockSpec(b, m) for _, b, m in ins],
        out_specs=[pl.BlockSpec(b, m) for _, _, b, m in outs] + [pl.BlockSpec((1, w), lambda i: (0, 0)) for w in reds],
        out_shape=[jax.ShapeDtypeStruct(s, d) for s, d, _, _ in outs] + [jax.ShapeDtypeStruct((1, w), F32) for w in reds],
        compiler_params=_cparams(("arbitrary",) if reds else ("parallel",), block_bytes),
    )(*[a for a, _, _ in ins])
    return res


def _rms(x, g):
    r = lax.rsqrt(jnp.mean(x * x, axis=1, keepdims=True) + EPS)
    return x * r * g


def _rms_bwd(x, g, dy):
    r = lax.rsqrt(jnp.mean(x * x, axis=1, keepdims=True) + EPS)
    xn = x * r
    dxn = dy * g
    dx = r * (dxn - xn * jnp.mean(dxn * xn, axis=1, keepdims=True))
    return dx, jnp.sum(dy * xn, axis=0, keepdims=True)


def _swap_halves(x, half):
    n = x.shape[1]
    lane = lax.broadcasted_iota(jnp.int32, x.shape, 1)
    first = (lane % (2 * half)) < half
    return jnp.where(first, pltpu.roll(x, n - half, axis=1), pltpu.roll(x, half, axis=1))


def _rope(x, c, sg, half):
    return x * c + _swap_halves(x, half) * sg


def _rope_t(dy, c, sg, half):
    return dy * c - _swap_halves(dy, half) * sg


def _mm_nn(name, a, b, outs, *, tm, tn, extras=(), epilogue=None):
    M, K = a.shape
    N = b.shape[1]
    tm = min(tm, M)
    n_e = len(extras)

    def body(*refs):
        a_ref, b_ref = refs[:2]
        acc = jnp.dot(a_ref[...].astype(BF16), b_ref[...].astype(BF16), preferred_element_type=F32)
        vals = (acc,) * len(outs) if epilogue is None else epilogue(acc, *[r[...] for r in refs[2:2 + n_e]])
        for r, v in zip(refs[2 + n_e:], vals):
            r[...] = v.astype(r.dtype)

    block_bytes = (_nbytes((tm, K), a.dtype) + _nbytes((K, tn), b.dtype) + sum(_nbytes((tm, tn), e.dtype) for e in extras)
                   + sum(_nbytes((tm, tn), d) for d in outs) + _nbytes((tm, tn), F32))
    return pl.pallas_call(
        body, name=name, grid=(N // tn, M // tm),
        in_specs=[pl.BlockSpec((tm, K), lambda j, i: (i, 0)), pl.BlockSpec((K, tn), lambda j, i: (0, j))]
        + [pl.BlockSpec((tm, tn), lambda j, i: (i, j)) for _ in extras],
        out_specs=[pl.BlockSpec((tm, tn), lambda j, i: (i, j)) for _ in outs],
        out_shape=[jax.ShapeDtypeStruct((M, N), d) for d in outs],
        compiler_params=_cparams(("parallel", "parallel"), block_bytes),
    )(a, b, *extras)


def _mm_nt(name, a, b, outs, *, tm, tn, extras=(), epilogue=None):
    M, N = a.shape
    K = b.shape[0]
    tm = min(tm, M)
    n_e = len(extras)

    def body(*refs):
        a_ref, b_ref = refs[:2]
        acc = lax.dot_general(a_ref[...].astype(BF16), b_ref[...].astype(BF16), NT, preferred_element_type=F32)
        vals = (acc,) * len(outs) if epilogue is None else epilogue(acc, *[r[...] for r in refs[2:2 + n_e]])
        for r, v in zip(refs[2 + n_e:], vals):
            r[...] = v.astype(r.dtype)

    block_bytes = (_nbytes((tm, N), a.dtype) + _nbytes((tn, N), b.dtype) + sum(_nbytes((tm, tn), e.dtype) for e in extras)
                   + sum(_nbytes((tm, tn), d) for d in outs) + _nbytes((tm, tn), F32))
    return pl.pallas_call(
        body, name=name, grid=(K // tn, M // tm),
        in_specs=[pl.BlockSpec((tm, N), lambda j, i: (i, 0)), pl.BlockSpec((tn, N), lambda j, i: (j, 0))]
        + [pl.BlockSpec((tm, tn), lambda j, i: (i, j)) for _ in extras],
        out_specs=[pl.BlockSpec((tm, tn), lambda j, i: (i, j)) for _ in outs],
        out_shape=[jax.ShapeDtypeStruct((M, K), d) for d in outs],
        compiler_params=_cparams(("parallel", "parallel"), block_bytes),
    )(a, b, *extras)


def _mm_tn(name, a, b, *, tm, tn, ts):
    S, K = a.shape
    N = b.shape[1]
    ts = min(ts, S)

    def body(a_ref, b_ref, o_ref):
        @pl.when(pl.program_id(2) == 0)
        def _():
            o_ref[...] = jnp.zeros_like(o_ref)
        o_ref[...] += lax.dot_general(a_ref[...].astype(BF16), b_ref[...].astype(BF16), TN, preferred_element_type=F32)

    block_bytes = _nbytes((ts, tm), a.dtype) + _nbytes((ts, tn), b.dtype) + 2 * _nbytes((tm, tn), F32)
    return pl.pallas_call(
        body, name=name, grid=(K // tm, N // tn, S // ts),
        in_specs=[pl.BlockSpec((ts, tm), lambda i, j, s: (s, i)), pl.BlockSpec((ts, tn), lambda i, j, s: (s, j))],
        out_specs=pl.BlockSpec((tm, tn), lambda i, j, s: (i, j)),
        out_shape=jax.ShapeDtypeStruct((K, N), F32),
        compiler_params=_cparams(("parallel", "parallel", "arbitrary"), block_bytes),
    )(a, b)


def _lane_pack(cols, rows):
    lane = lax.broadcasted_iota(jnp.int32, (rows, LANES), 1)
    val = jnp.zeros((rows, LANES), F32)
    for h, c in enumerate(cols):
        val = jnp.where(lane == h, c, val)
    return val


def _mask(kb, row, tk):
    return kb * tk + lax.broadcasted_iota(jnp.int32, (1, tk), 1) <= row


def _kb_range(i, tq, tk):
    return (i * tq) // tk, ((i + 1) * tq + tk - 1) // tk


def _softmax_fwd(name, q, k, v, *, groups, heads, q_spec, k_spec, v_spec, o_width, tq, tk, ride=None):
    S = q.shape[0]
    nq = S // tq
    nh = len(heads)
    dv = heads[0][2].stop - heads[0][2].start

    def body(q_ref, k_ref, v_ref, o_ref, lse_ref):
        i = pl.program_id(1)
        q_all = q_ref[...]
        row = i * tq + lax.broadcasted_iota(jnp.int32, (tq, 1), 0)
        mid, hi = _kb_range(i, tq, tk)
        qhs = [q_all[:, qs] for qs, _, _ in heads]

        def blk(kb, carry, masked):
            r0 = pl.multiple_of(kb * tk, tk)
            k_all = k_ref[pl.ds(r0, tk), :]
            v_all = v_ref[pl.ds(r0, tk), :]
            if masked:
                ok = _mask(kb, row, tk)
            ones = jnp.ones((tk, dv), BF16)
            new = []
            for qh, (_, ks, vs), (m, acc) in zip(qhs, heads, carry):
                s = lax.dot_general(qh, k_all[:, ks], NT, preferred_element_type=F32)
                if masked:
                    s = jnp.where(ok, s, NEG)
                m_new = jnp.maximum(m, jnp.max(s, axis=1, keepdims=True))
                p = jnp.exp(s - m_new)
                if masked:
                    p = jnp.where(ok, p, 0.0)
                pv = jnp.dot(p.astype(BF16), jnp.concatenate([v_all[:, vs], ones], axis=1), preferred_element_type=F32)
                new.append((m_new, jnp.exp(m - m_new) * acc + pv))
            return tuple(new)

        carry = tuple((jnp.full((tq, 1), NEG, F32), jnp.zeros((tq, 2 * dv), F32)) for _ in range(nh))
        carry = lax.fori_loop(0, mid, functools.partial(blk, masked=False), carry)
        carry = lax.fori_loop(mid, hi, functools.partial(blk, masked=True), carry)
        o_ref[...] = jnp.concatenate([acc[:, :dv] / acc[:, dv:] for _, acc in carry], axis=1).astype(o_ref.dtype)
        lse_ref[0] = _lane_pack([m + jnp.log(acc[:, dv:dv + 1]) for m, acc in carry], tq)

    wo = nh * dv
    block_bytes = _nbytes(q_spec[0], q.dtype) + _nbytes(k_spec[0], k.dtype) + _nbytes(v_spec[0], v.dtype) + 4 * tq * (wo + LANES)
    return _pallas(
        body, name=name, grid=(groups, nq), in_specs=[pl.BlockSpec(*q_spec), pl.BlockSpec(*k_spec), pl.BlockSpec(*v_spec)],
        out_specs=[pl.BlockSpec((tq, wo), lambda g, i: (i, g)), pl.BlockSpec((1, tq, LANES), lambda g, i: (g, i, 0))],
        out_shape=[jax.ShapeDtypeStruct((S, o_width), BF16), jax.ShapeDtypeStruct((groups, S, LANES), F32)],
        args=[q, k, v], sem=("parallel", "arbitrary"), block_bytes=block_bytes, ride=ride)


def _softmax_bwd(name, q, k, v, o, do, lse, *, groups, heads, q_spec, k_spec, v_spec, tq, tk, scale, ride=None):
    S = q.shape[0]
    nq, nkb = S // tq, S // tk
    nh = len(heads)
    dv = heads[0][2].stop - heads[0][2].start
    wo = nh * dv
    wk, wv = k_spec[0][1], v_spec[0][1]

    def body(q_ref, k_ref, v_ref, o_ref, do_ref, lse_ref, dq_ref, dkt_ref, dvt_ref):
        i = pl.program_id(1)

        @pl.when(i == 0)
        def _():
            dkt_ref[...] = jnp.zeros_like(dkt_ref)
            dvt_ref[...] = jnp.zeros_like(dvt_ref)

        q_all = q_ref[...]
        o_all = o_ref[...].astype(F32)
        do_all = do_ref[...].astype(F32)
        lse_all = lse_ref[0]
        row = i * tq + lax.broadcasted_iota(jnp.int32, (tq, 1), 0)
        mid, hi = _kb_range(i, tq, tk)
        per_head = []
        for h, (qs, ks, vs) in enumerate(heads):
            osl = slice(h * dv, (h + 1) * dv)
            doh = do_all[:, osl]
            delta = jnp.sum(doh * o_all[:, osl], axis=1, keepdims=True)
            qh = q_all[:, qs]
            per_head.append((qh, qh.astype(F32).T.astype(BF16), doh.astype(BF16), doh.T.astype(BF16), delta, lse_all[:, h:h + 1], ks, vs))

        def blk(kb, dqs, masked):
            r0 = pl.multiple_of(kb * tk, tk)
            k_all = k_ref[pl.ds(r0, tk), :]
            v_all = v_ref[pl.ds(r0, tk), :]
            if masked:
                ok = _mask(kb, row, tk)
            dkt, dvt, new_dqs = [], [], []
            for (qh, qt, doh, dot_, delta, lse_h, ks, vs), dq in zip(per_head, dqs):
                kk, vv = k_all[:, ks], v_all[:, vs]
                s = lax.dot_general(qh, kk, NT, preferred_element_type=F32)
                p = jnp.exp(s - lse_h)
                if masked:
                    p = jnp.where(ok, p, 0.0)
                dp = lax.dot_general(doh, vv, NT, preferred_element_type=F32)
                ds = (p * (dp - delta)).astype(BF16)
                new_dqs.append(dq + jnp.dot(ds, kk, preferred_element_type=F32))
                dkt.append(jnp.dot(qt, ds, preferred_element_type=F32))
                dvt.append(jnp.dot(dot_, p.astype(BF16), preferred_element_type=F32))
            dkt_ref[kb] += jnp.concatenate(dkt, axis=0)
            dvt_ref[kb] += jnp.concatenate(dvt, axis=0)
            return tuple(new_dqs)

        dqs = tuple(jnp.zeros((tq, qs.stop - qs.start), F32) for qs, _, _ in heads)
        dqs = lax.fori_loop(0, mid, functools.partial(blk, masked=False), dqs)
        dqs = lax.fori_loop(mid, hi, functools.partial(blk, masked=True), dqs)
        dq_ref[...] = jnp.concatenate([dq * scale for dq in dqs], axis=1)

    in_specs = [pl.BlockSpec(*q_spec), pl.BlockSpec(*k_spec), pl.BlockSpec(*v_spec),
                pl.BlockSpec((tq, wo), lambda g, i: (i, g)), pl.BlockSpec((tq, wo), lambda g, i: (i, g)),
                pl.BlockSpec((1, tq, LANES), lambda g, i: (g, i, 0))]
    args = [q, k, v, o, do, lse]
    out_specs = [pl.BlockSpec(*q_spec), pl.BlockSpec((nkb, wk, tk), lambda g, i: (0, g, 0)), pl.BlockSpec((nkb, wv, tk), lambda g, i: (0, g, 0))]
    out_shape = [jax.ShapeDtypeStruct(q.shape, F32), jax.ShapeDtypeStruct((nkb, k.shape[1], tk), F32),
                 jax.ShapeDtypeStruct((nkb, v.shape[1], tk), F32)]
    block_bytes = (_nbytes(q_spec[0], q.dtype) + _nbytes(k_spec[0], k.dtype) + _nbytes(v_spec[0], v.dtype) + 6 * tq * wo + 4 * tq * LANES
                   + _nbytes(q_spec[0], F32) + _nbytes(k_spec[0], F32) + _nbytes(v_spec[0], F32))
    return _pallas(body, name=name, grid=(groups, nq), in_specs=in_specs, out_specs=out_specs, out_shape=out_shape, args=args,
                   sem=("parallel", "arbitrary"), block_bytes=block_bytes, ride=ride)


def _swa_window(i, tq, row):
    start = pl.multiple_of(jnp.maximum(i * tq - SWA_WINDOW, 0), SWA_WINDOW)
    col = start + lax.broadcasted_iota(jnp.int32, (1, tq + SWA_WINDOW), 1)
    return start, (col <= row) & ((row - col) < SWA_WINDOW)


def _swa_fwd(name, q, k, v, sinks, *, tq):
    S = q.shape[0]
    hd, span = SWA_HEAD_DIM, tq + SWA_WINDOW
    scale = hd ** -0.5

    def body(sink_ref, q_ref, k_ref, v_ref, o_ref, lse_ref):
        g, i = pl.program_id(0), pl.program_id(1)
        row = i * tq + lax.broadcasted_iota(jnp.int32, (tq, 1), 0)
        start, ok = _swa_window(i, tq, row)
        kk, vv = k_ref[pl.ds(start, span), :], v_ref[pl.ds(start, span), :]
        q_all = q_ref[...]
        outs, lses = [], []
        for h in range(SWA_G):
            s = lax.dot_general(q_all[:, h * hd:(h + 1) * hd] * scale, kk, NT, preferred_element_type=F32)
            s = jnp.where(ok, s, NEG)
            sink = sink_ref[g * SWA_G + h]
            m = jnp.maximum(jnp.max(s, axis=1, keepdims=True), sink)
            p = jnp.exp(s - m)
            l = jnp.sum(p, axis=1, keepdims=True) + jnp.exp(sink - m)
            outs.append(jnp.dot(p.astype(BF16), vv, preferred_element_type=F32) / l)
            lses.append(m + jnp.log(l))
        o_ref[...] = jnp.concatenate(outs, axis=1).astype(o_ref.dtype)
        lse_ref[0] = _lane_pack(lses, tq)

    wq = SWA_G * hd
    block_bytes = 2 * tq * wq * 2 + 2 * 2 * S * hd + 4 * tq * LANES
    return pl.pallas_call(
        body, name=name, grid=(SWA_KV_HEADS, S // tq),
        in_specs=[pl.BlockSpec(memory_space=pltpu.SMEM), pl.BlockSpec((tq, wq), lambda g, i: (i, g)),
                  pl.BlockSpec((S, hd), lambda g, i: (g, 0)), pl.BlockSpec((S, hd), lambda g, i: (g, 0))],
        out_specs=[pl.BlockSpec((tq, wq), lambda g, i: (i, g)), pl.BlockSpec((1, tq, LANES), lambda g, i: (g, i, 0))],
        out_shape=[jax.ShapeDtypeStruct((S, SWA_HEADS * hd), BF16), jax.ShapeDtypeStruct((SWA_KV_HEADS, S, LANES), F32)],
        compiler_params=_cparams(("parallel", "parallel"), block_bytes),
    )(sinks, q, k, v)


def _swa_bwd(name, q, k, v, o, do, lse, sinks, *, tq):
    S = q.shape[0]
    hd, span = SWA_HEAD_DIM, tq + SWA_WINDOW
    scale = hd ** -0.5
    nblk = S // SWA_WINDOW

    def body(sink_ref, q_ref, k_ref, v_ref, o_ref, do_ref, lse_ref, dq_ref, dkt_ref, dvt_ref, dsink_ref):
        g, i = pl.program_id(0), pl.program_id(1)

        @pl.when(i == 0)
        def _():
            dkt_ref[...] = jnp.zeros_like(dkt_ref)
            dvt_ref[...] = jnp.zeros_like(dvt_ref)
            dsink_ref[...] = jnp.zeros_like(dsink_ref)

        row = i * tq + lax.broadcasted_iota(jnp.int32, (tq, 1), 0)
        start, ok = _swa_window(i, tq, row)
        kk, vv = k_ref[pl.ds(start, span), :], v_ref[pl.ds(start, span), :]
        q_all, o_all, do_all, lse_all = q_ref[...], o_ref[...].astype(F32), do_ref[...].astype(F32), lse_ref[0]
        dqs, qts, dots, dss, ps = [], [], [], [], []
        for h in range(SWA_G):
            sl = slice(h * hd, (h + 1) * hd)
            qh = q_all[:, sl] * scale
            doh = do_all[:, sl]
            delta = jnp.sum(doh * o_all[:, sl], axis=1, keepdims=True)
            lse_h = lse_all[:, h:h + 1]
            qts.append(qh.astype(F32).T.astype(BF16))
            dots.append(doh.T.astype(BF16))
            doh = doh.astype(BF16)
            s = jnp.where(ok, lax.dot_general(qh, kk, NT, preferred_element_type=F32), NEG)
            p = jnp.exp(s - lse_h)
            ds = (p * (lax.dot_general(doh, vv, NT, preferred_element_type=F32) - delta)).astype(BF16)
            dqs.append(jnp.dot(ds, kk, preferred_element_type=F32) * scale)
            dss.append(ds)
            ps.append(p.astype(BF16))
            p_sink = jnp.exp(sink_ref[g * SWA_G + h] - lse_h)
            dsink_ref[0, h:h + 1, :] += jnp.broadcast_to(-jnp.sum(p_sink * delta, axis=0, keepdims=True), (1, LANES))
        dq_ref[...] = jnp.concatenate(dqs, axis=1)
        dkt = jnp.dot(jnp.concatenate(qts, axis=1), jnp.concatenate(dss, axis=0), preferred_element_type=F32)
        dvt = jnp.dot(jnp.concatenate(dots, axis=1), jnp.concatenate(ps, axis=0), preferred_element_type=F32)
        for j in range(span // SWA_WINDOW):
            cols = slice(j * SWA_WINDOW, (j + 1) * SWA_WINDOW)
            dkt_ref[start // SWA_WINDOW + j] += dkt[:, cols]
            dvt_ref[start // SWA_WINDOW + j] += dvt[:, cols]

    wq = SWA_G * hd
    qb = pl.BlockSpec((tq, wq), lambda g, i: (i, g))
    kb = pl.BlockSpec((S, hd), lambda g, i: (g, 0))
    tb = pl.BlockSpec((nblk, hd, SWA_WINDOW), lambda g, i: (g, 0, 0))
    t_shape = jax.ShapeDtypeStruct((SWA_KV_HEADS * nblk, hd, SWA_WINDOW), F32)
    block_bytes = 2 * tq * wq * (2 + 2 + 4 + 4) + 2 * S * hd * (2 + 2 + 4 + 4) + 4 * tq * LANES
    return pl.pallas_call(
        body, name=name, grid=(SWA_KV_HEADS, S // tq),
        in_specs=[pl.BlockSpec(memory_space=pltpu.SMEM), qb, kb, kb, qb, qb, pl.BlockSpec((1, tq, LANES), lambda g, i: (g, i, 0))],
        out_specs=[qb, tb, tb, pl.BlockSpec((1, 8, LANES), lambda g, i: (g, 0, 0))],
        out_shape=[jax.ShapeDtypeStruct(q.shape, F32), t_shape, t_shape, jax.ShapeDtypeStruct((SWA_KV_HEADS, 8, LANES), F32)],
        compiler_params=_cparams(("parallel", "arbitrary"), block_bytes),
    )(sinks, q, k, v, o, do, lse)


def _split_dot(x, u2):
    hi = x.astype(BF16)
    lo = (x - hi.astype(F32)).astype(BF16)
    return jnp.dot(jnp.concatenate([hi, lo], axis=1), u2, preferred_element_type=F32)


def _suffix_ones():
    r = lax.broadcasted_iota(jnp.int32, (2 * LANES, 2 * LANES), 0) % LANES
    c = lax.broadcasted_iota(jnp.int32, (2 * LANES, 2 * LANES), 1)
    return ((r > c) | (c >= LANES)).astype(BF16)


def _suffix_scan(x, uo, run):
    nc = x.shape[1] // LANES
    out = [None] * nc
    for c in reversed(range(nc)):
        st = _split_dot(x[:, c * LANES:(c + 1) * LANES], uo)
        out[c] = st[:, :LANES] + run
        run = run + st[:, LANES:]
    return (out[0] if nc == 1 else jnp.concatenate(out, axis=1)), run


SB_DEAD_LOG = -110.0


def _sb_logits(qh, kk, r0, row, masked):
    z = lax.dot_general(qh, kk, NT, preferred_element_type=F32)
    nz = -z
    l = jnp.minimum(nz, 0.0) - jnp.log(1.0 + jnp.exp(jnp.minimum(z, nz)))
    ok = None
    if masked:
        col = r0 + lax.broadcasted_iota(jnp.int32, (1, kk.shape[0]), 1)
        ok = col < row
        l = jnp.where(ok, l, 0.0)
    return z, l, ok


def _sb_walk(blk, carry, i, tq, ts):
    for j in reversed(range(tq // ts)):
        carry = blk(pl.multiple_of(i * tq + j * ts, ts), ts, carry, True)

    def live(c):
        t, heads = c
        top = jnp.max(heads[0][0])
        for h in heads[1:]:
            top = jnp.maximum(top, jnp.max(h[0]))
        return jnp.logical_and(t < i * (tq // ts), top > SB_DEAD_LOG)

    def step(c):
        t, heads = c
        return t + 1, blk(pl.multiple_of(i * tq - (t + 1) * ts, ts), ts, heads, False)

    return lax.while_loop(live, step, (jnp.int32(0), carry))[1]


def _sb_fwd(name, pb, *, q_cb, k_cb, v_cb, tq, ts, scale):
    S = pb.shape[0]
    nq = S // tq
    hd = SB_HEAD_DIM
    groups = SB_HEADS * hd // LANES
    nh = LANES // hd

    def body(q_ref, k_ref, v_ref, o16_ref, o32_ref):
        i = pl.program_id(1)
        q_all = q_ref[...]
        row = i * tq + lax.broadcasted_iota(jnp.int32, (tq, 1), 0)
        uo = _suffix_ones()
        sls = [slice(h * hd, (h + 1) * hd) for h in range(nh)]
        qhs = [q_all[:, sl] * scale for sl in sls]

        def blk(r0, width, carry, masked):
            k_all = k_ref[pl.ds(r0, width), :]
            v_all = v_ref[pl.ds(r0, width), :]
            new = []
            for qh, sl, (run_l, acc) in zip(qhs, sls, carry):
                z, l, ok = _sb_logits(qh, k_all[:, sl], r0, row, masked)
                tail, run_l = _suffix_scan(l, uo, run_l)
                e = z + l + tail
                if masked:
                    e = jnp.where(ok, e, NEG)
                new.append((run_l, acc + jnp.dot(jnp.exp(e).astype(BF16), v_all[:, sl], preferred_element_type=F32)))
            return tuple(new)

        carry = tuple((jnp.zeros((tq, LANES), F32), jnp.zeros((tq, hd), F32)) for _ in range(nh))
        carry = _sb_walk(blk, carry, i, tq, ts)
        o = jnp.concatenate([acc for _, acc in carry], axis=1)
        o16_ref[...] = o.astype(BF16)
        o32_ref[...] = o

    block_bytes = 2 * tq * LANES + 2 * 2 * S * LANES + 6 * tq * LANES
    return pl.pallas_call(
        body, name=name, grid=(groups, nq),
        in_specs=[pl.BlockSpec((tq, LANES), lambda g, i: (i, q_cb + g)), pl.BlockSpec((S, LANES), lambda g, i: (0, k_cb + g)),
                  pl.BlockSpec((S, LANES), lambda g, i: (0, v_cb + g))],
        out_specs=[pl.BlockSpec((tq, LANES), lambda g, i: (i, g)), pl.BlockSpec((tq, LANES), lambda g, i: (i, g))],
        out_shape=[jax.ShapeDtypeStruct((S, groups * LANES), BF16), jax.ShapeDtypeStruct((S, groups * LANES), F32)],
        compiler_params=_cparams(("parallel", "arbitrary"), block_bytes),
    )(pb, pb, pb)


def _sb_bwd(name, pb, o32, do, *, q_cb, k_cb, v_cb, tq, ts, scale):
    S = pb.shape[0]
    nq = S // tq
    hd = SB_HEAD_DIM
    groups = SB_HEADS * hd // LANES
    nh = LANES // hd

    def body(q_ref, k_ref, v_ref, o_ref, do_ref, dq_ref, dkt_ref, dvt_ref):
        i = pl.program_id(1)

        @pl.when(i == 0)
        def _():
            dkt_ref[...] = jnp.zeros_like(dkt_ref)
            dvt_ref[...] = jnp.zeros_like(dvt_ref)

        q_all = q_ref[...]
        o_all = o_ref[...]
        do_all = do_ref[...].astype(F32)
        row = i * tq + lax.broadcasted_iota(jnp.int32, (tq, 1), 0)
        uo = _suffix_ones()
        per_head = []
        for h in range(nh):
            sl = slice(h * hd, (h + 1) * hd)
            doh = do_all[:, sl].astype(BF16)
            total = jnp.sum(doh.astype(F32) * o_all[:, sl], axis=1, keepdims=True)
            qh = q_all[:, sl] * scale
            per_head.append((qh, qh.astype(F32).T.astype(BF16), doh, doh.astype(F32).T.astype(BF16), jnp.broadcast_to(total, (tq, LANES)), sl))

        def blk(r0, width, carry, masked):
            k_all = k_ref[pl.ds(r0, width), :]
            v_all = v_ref[pl.ds(r0, width), :]
            new, dk_c, dv_c = [], [], []
            for (qh, qt, doh, dot_, total, sl), (run_l, run_g, dq) in zip(per_head, carry):
                kk, vv = k_all[:, sl], v_all[:, sl]
                z, l, ok = _sb_logits(qh, kk, r0, row, masked)
                tail, run_l = _suffix_scan(l, uo, run_l)
                e = z + l
                beta = jnp.exp(e)
                e = e + tail
                if masked:
                    e = jnp.where(ok, e, NEG)
                a = jnp.exp(e).astype(BF16).astype(F32)
                gr = lax.dot_general(doh, vv, NT, preferred_element_type=F32) * a
                right, run_g = _suffix_scan(gr, uo, run_g)
                nc = width // LANES
                prefix = (total if nc == 1 else jnp.tile(total, (1, nc))) - right
                dz = gr - beta * prefix
                if masked:
                    dz = jnp.where(ok, dz, 0.0)
                dzb = dz.astype(BF16)
                new.append((run_l, run_g, dq + jnp.dot(dzb, kk, preferred_element_type=F32)))
                dk_c.append(jnp.dot(qt, dzb, preferred_element_type=F32))
                dv_c.append(jnp.dot(dot_, a.astype(BF16), preferred_element_type=F32))
            dkt, dvt = jnp.concatenate(dk_c, axis=0), jnp.concatenate(dv_c, axis=0)
            for j in range(width // ts):
                dkt_ref[r0 // ts + j] += dkt[:, j * ts:(j + 1) * ts]
                dvt_ref[r0 // ts + j] += dvt[:, j * ts:(j + 1) * ts]
            return tuple(new)

        zc = jnp.zeros((tq, LANES), F32)
        carry = tuple((zc, zc, jnp.zeros((tq, hd), F32)) for _ in range(nh))
        carry = _sb_walk(blk, carry, i, tq, ts)
        dq_ref[...] = jnp.concatenate([c[2] * scale for c in carry], axis=1)

    W = groups * LANES
    block_bytes = 2 * tq * LANES + 2 * 2 * S * LANES + 3 * 4 * tq * LANES + 2 * 4 * S * LANES
    return pl.pallas_call(
        body, name=name, grid=(groups, nq),
        in_specs=[pl.BlockSpec((tq, LANES), lambda g, i: (i, q_cb + g)), pl.BlockSpec((S, LANES), lambda g, i: (0, k_cb + g)),
                  pl.BlockSpec((S, LANES), lambda g, i: (0, v_cb + g)), pl.BlockSpec((tq, LANES), lambda g, i: (i, g)),
                  pl.BlockSpec((tq, LANES), lambda g, i: (i, g))],
        out_specs=[pl.BlockSpec((tq, LANES), lambda g, i: (i, g)), pl.BlockSpec((S // ts, LANES, ts), lambda g, i: (0, g, 0)),
                   pl.BlockSpec((S // ts, LANES, ts), lambda g, i: (0, g, 0))],
        out_shape=[jax.ShapeDtypeStruct((S, W), F32)] + [jax.ShapeDtypeStruct((S // ts, W, ts), F32)] * 2,
        compiler_params=_cparams(("parallel", "arbitrary"), block_bytes),
    )(pb, pb, pb, o32, do)


def _exchange_copies(src_refs, out_refs, send_sems, recv_sems, local_sems, per_peer):
    x, y, c = lax.axis_index("x"), lax.axis_index("y"), lax.axis_index("c")
    me = 4 * x + 2 * y + c
    n = len(src_refs)

    def copy(j, k):
        px, py, pc = x ^ (k >> 2), y ^ ((k >> 1) & 1), c ^ (k & 1)
        s = src_refs[j].at[4 * px + 2 * py + pc] if per_peer else src_refs[j]
        return pltpu.make_async_remote_copy(
            src_ref=s, dst_ref=out_refs[j].at[me], send_sem=send_sems.at[j, k - 1], recv_sem=recv_sems.at[j, k - 1],
            device_id=(px, py, pc), device_id_type=pl.DeviceIdType.MESH)

    mine = [pltpu.make_async_copy(src_refs[j].at[me] if per_peer else src_refs[j], out_refs[j].at[me], local_sems.at[j]) for j in range(n)]
    return mine, [copy(j, k) for k in range(1, N_DEV) for j in range(n)]


def _exchange_start(*refs, per_peer):
    mine, copies = _exchange_copies(*refs, per_peer)
    for cp in mine + copies:
        cp.start()


def _exchange_wait(*refs, per_peer):
    mine, copies = _exchange_copies(*refs, per_peer)
    for cp in copies:
        cp.wait_recv()
    for cp in copies:
        cp.wait_send()
    for cp in mine:
        cp.wait()


def _exchange_specs(srcs, per_peer):
    n = len(srcs)
    out_shape = [jax.ShapeDtypeStruct((N_DEV,) + tuple(s.shape[1:] if per_peer else s.shape), s.dtype) for s in srcs]
    sems = [pltpu.SemaphoreType.DMA((n, N_DEV - 1)), pltpu.SemaphoreType.DMA((n, N_DEV - 1)), pltpu.SemaphoreType.DMA((n,))]
    return [pl.BlockSpec(memory_space=pltpu.HBM)] * n, out_shape, sems


def _exchange(name, srcs, per_peer):
    n = len(srcs)

    def body(*refs):
        parts = (refs[:n], refs[n:2 * n], *refs[2 * n:])
        _exchange_start(*parts, per_peer=per_peer)
        _exchange_wait(*parts, per_peer=per_peer)

    hbm, out_shape, sems = _exchange_specs(srcs, per_peer)
    return pl.pallas_call(body, name=name, in_specs=hbm, out_specs=hbm, out_shape=out_shape, scratch_shapes=sems)(*srcs)


def _pallas(body, *, name, grid, in_specs, out_specs, out_shape, args, sem, block_bytes, ride=None):
    if ride is None:
        return pl.pallas_call(body, name=name, grid=grid, in_specs=in_specs, out_specs=out_specs, out_shape=out_shape,
                              compiler_params=_cparams(sem, block_bytes))(*args), None
    srcs, per_peer = ride
    n, n_in, n_out = len(srcs), len(in_specs), len(out_specs)

    def riding(*refs):
        ins, xsrc = refs[:n_in], refs[n_in:n_in + n]
        outs, xout = refs[n_in + n:n_in + n + n_out], refs[n_in + n + n_out:n_in + 2 * n + n_out]
        parts = (xsrc, xout, *refs[n_in + 2 * n + n_out:])
        ids = [pl.program_id(d) for d in range(len(grid))]
        first = functools.reduce(jnp.logical_and, [i == 0 for i in ids])
        last = functools.reduce(jnp.logical_and, [i == g - 1 for i, g in zip(ids, grid)])
        pl.when(first)(functools.partial(_exchange_start, *parts, per_peer=per_peer))
        body(*ins, *outs)
        pl.when(last)(functools.partial(_exchange_wait, *parts, per_peer=per_peer))

    hbm, x_shape, sems = _exchange_specs(srcs, per_peer)
    res = pl.pallas_call(riding, name=name, grid=grid, in_specs=list(in_specs) + hbm, out_specs=list(out_specs) + hbm,
                         out_shape=list(out_shape) + x_shape, scratch_shapes=sems,
                         compiler_params=_cparams(("arbitrary",) * len(grid), block_bytes))(*args, *srcs)
    return res[:n_out], res[n_out:]


def _adamw(name, parts, w, m, v, *, tile):
    R, C = w.shape

    def body(p_ref, w_ref, m_ref, v_ref, g_ref, d_ref, nm_ref, nv_ref):
        g = p_ref[0].astype(F32)
        for d in range(1, N_DEV):
            g = g + p_ref[d].astype(F32)
        wv = w_ref[...]
        mm = ADAM_B1 * m_ref[...] + (1.0 - ADAM_B1) * g
        vv = ADAM_B2 * v_ref[...] + (1.0 - ADAM_B2) * jnp.square(g)
        m_hat = mm / (1.0 - ADAM_B1 ** ADAM_STEP)
        v_hat = vv / (1.0 - ADAM_B2 ** ADAM_STEP)
        g_ref[...] = g
        d_ref[...] = -ADAM_LR * (m_hat / (jnp.sqrt(v_hat) + ADAM_EPS) + ADAM_WD * wv)
        nm_ref[...] = mm
        nv_ref[...] = vv

    blk = pl.BlockSpec((tile, C), lambda i: (i, 0))
    block_bytes = N_DEV * _nbytes((tile, C), parts.dtype) + 7 * _nbytes((tile, C), F32)
    return pl.pallas_call(
        body, name=name, grid=(R // tile,),
        in_specs=[pl.BlockSpec((N_DEV, tile, C), lambda i: (0, i, 0)), blk, blk, blk],
        out_specs=[blk] * 4, out_shape=[jax.ShapeDtypeStruct((R, C), F32)] * 4,
        compiler_params=_cparams(("parallel",), block_bytes),
    )(parts, w, m, v)


def _perm_w_in(w):
    cols = [w[:, ORIG_COLS[n][0]:ORIG_COLS[n][0] + ORIG_COLS[n][1]] for n in PERM_ORDER]
    cols.append(jnp.zeros((w.shape[0], IN_WIDTH_P - IN_WIDTH), w.dtype))
    return jnp.concatenate(cols, axis=1)


def _unperm_w_in(wp):
    order = sorted(ORIG_COLS, key=lambda n: ORIG_COLS[n][0])
    return jnp.concatenate([wp[:, PERM_OFF[n]:PERM_OFF[n] + ORIG_COLS[n][1]] for n in order], axis=1)


def _perm_w_uq(w):
    w3 = w.reshape(w.shape[0], MLA_HEADS, MLA_NOPE + MLA_ROPE)
    return jnp.pad(w3, ((0, 0), (0, 0), (0, LANES - MLA_NOPE - MLA_ROPE))).reshape(w.shape[0], MLA_QW)


def _unperm_w_uq(wp):
    return wp.reshape(wp.shape[0], MLA_HEADS, LANES)[:, :, :MLA_NOPE + MLA_ROPE].reshape(wp.shape[0], -1)


def _perm_w_ukv(w):
    w3 = w.reshape(w.shape[0], MLA_HEADS, MLA_NOPE + MLA_V)
    kp = jnp.pad(w3[:, :, :MLA_NOPE], ((0, 0), (0, 0), (0, LANES - MLA_NOPE))).reshape(w.shape[0], MLA_HEADS * LANES)
    return jnp.concatenate([kp, w3[:, :, MLA_NOPE:].reshape(w.shape[0], MLA_HEADS * MLA_V)], axis=1)


def _unperm_w_ukv(wp):
    n = wp.shape[0]
    kp = wp[:, :MLA_HEADS * LANES].reshape(n, MLA_HEADS, LANES)[:, :, :MLA_NOPE]
    vp = wp[:, MLA_HEADS * LANES:].reshape(n, MLA_HEADS, MLA_V)
    return jnp.concatenate([kp, vp], axis=2).reshape(n, MLA_HEADS * (MLA_NOPE + MLA_V))


BIG = ("w_in", "w_uq", "w_ukv", "w_o_mla", "w_o_swa", "w_o_sb", "w_out", "w_up", "w_down")
W_EARLY = ("w_in", "w_uq", "w_ukv")
W_LATE = ("w_o_mla", "w_o_swa", "w_o_sb", "w_out", "w_up", "w_down")
ROW_SHARDED = ("w_out", "w_down")
SMALL = ("g_mix_pre", "b_gate", "g_q_lat", "g_kv_lat", "swa_sinks", "g_mix_post", "g_mlp_pre", "g_mlp_post")
PACK_COLS = 1024


def _to_shards(name, full):
    L, R, C = full.shape
    if name in ROW_SHARDED:
        return full.reshape(L, N_DEV, R // N_DEV, C).transpose(1, 0, 2, 3).reshape(N_DEV, L * R // N_DEV, C)
    return full.reshape(L, R, N_DEV, C // N_DEV).transpose(2, 0, 1, 3).reshape(N_DEV, L * R, C // N_DEV)


def _from_shards(name, gathered, shard_shape):
    L, r, c = shard_shape
    a = gathered.reshape(N_DEV, L, r, c)
    if name in ROW_SHARDED:
        return a.transpose(1, 0, 2, 3).reshape(L, N_DEV * r, c)
    return a.transpose(1, 2, 0, 3).reshape(L, r, N_DEV * c)


def _tables(positions):
    pos = positions.astype(F32).reshape(-1, 1)

    def cs(d):
        inv = 1.0 / (ROPE_THETA ** (jnp.arange(0, d, 2, dtype=F32) / d))
        ang = pos * inv
        c, s = jnp.cos(ang), jnp.sin(ang)
        return jnp.concatenate([c, c], axis=1), jnp.concatenate([-s, s], axis=1)

    c64, s64 = cs(SWA_HEAD_DIM)
    c32, s32 = cs(MLA_ROPE)
    n = pos.shape[0]
    one, zero = jnp.ones((n, MLA_NOPE), F32), jnp.zeros((n, MLA_NOPE), F32)
    pad0 = jnp.zeros((n, LANES - MLA_NOPE - MLA_ROPE), F32)
    cq = jnp.concatenate([one, c32, pad0], axis=1)
    sq = jnp.concatenate([zero, s32, pad0], axis=1)
    padk = jnp.zeros((n, LANES - MLA_ROPE), F32)
    return dict(
        c_swa_q=jnp.tile(c64, (1, SWA_HEADS)), s_swa_q=jnp.tile(s64, (1, SWA_HEADS)),
        c_swa_k=jnp.tile(c64, (1, SWA_KV_HEADS)), s_swa_k=jnp.tile(s64, (1, SWA_KV_HEADS)),
        c_mla_q=jnp.tile(cq, (1, MLA_HEADS)), s_mla_q=jnp.tile(sq, (1, MLA_HEADS)),
        c_mla_k=jnp.concatenate([c32, padk], axis=1), s_mla_k=jnp.concatenate([s32, padk], axis=1))


def _cb(name):
    return PERM_OFF[name] // PERM_WIDTH[name]


MLA_SCALE = (MLA_NOPE + MLA_ROPE) ** -0.5
MLA_FWD_TQ, MLA_FWD_TK = 1024, 1024
MLA_SPEC = dict(groups=MLA_HEADS // 2, tq=512, tk=512,
                heads=[(slice(h * LANES, (h + 1) * LANES), slice(h * LANES, (h + 1) * LANES), slice(h * MLA_V, (h + 1) * MLA_V)) for h in range(2)])
SWA_G = SWA_HEADS // SWA_KV_HEADS
SWA_TQ = 256
SB_SPEC = dict(tq=512, ts=256, scale=SB_HEAD_DIM ** -0.5, q_cb=PERM_OFF["q_sb"] // LANES, k_cb=PERM_OFF["k_sb"] // LANES,
               v_cb=PERM_OFF["v_sb"] // LANES)


def _mla_specs(S, tq):
    return dict(q_spec=((tq, 2 * LANES), lambda g, i: (i, g)), k_spec=((S, 2 * LANES), lambda g, i: (0, g)),
                v_spec=((S, 2 * MLA_V), lambda g, i: (0, g)))


def _layer_fwd(l, x, W, P, tb, T, ride, late):
    S = x.shape[0]
    nt = S // T
    h, = _rowwise(f"l{l}_norm_in", lambda xv, g: (_rms(xv, g),), [_ri(x, T), _bi(P["g_mix_pre"])], [_ro(S, D_MODEL, BF16, T)], n_tiles=nt)
    proj, proj16 = _mm_nn(f"l{l}_mm_in", h, W["w_in"], [F32, BF16], tm=512, tn=IN_WIDTH_P // 2)

    def mix_prep(cq, ckv, qs, ks, vs, gq, gkv, cq_t, sq_t, ck_t, sk_t):
        ksr = _rope(ks, ck_t, sk_t, SWA_HEAD_DIM // 2)
        hd = SWA_HEAD_DIM
        return (_rms(cq, gq), _rms(ckv, gkv), _rope(qs, cq_t, sq_t, hd // 2),
                [ksr[:, :hd], ksr[:, hd:]], [vs[:, :hd], vs[:, hd:]])

    kv3 = lambda dt: ((SWA_KV_HEADS, S, SWA_HEAD_DIM), dt, (SWA_KV_HEADS, T, SWA_HEAD_DIM), lambda i: (0, i, 0))
    cqn, ckvn, q_swa, k_swa, v_swa = _rowwise(
        f"l{l}_mix_prep", mix_prep,
        [_ri(proj, T, 256, _cb("c_q")), _ri(proj, T, 128, _cb("c_kv")), _ri(proj, T, 512, _cb("q_swa")), _ri(proj, T, 128, _cb("k_swa")),
         _ri(proj, T, 128, _cb("v_swa")), _bi(P["g_q_lat"]), _bi(P["g_kv_lat"]), _ri(tb["c_swa_q"], T), _ri(tb["s_swa_q"], T),
         _ri(tb["c_swa_k"], T), _ri(tb["s_swa_k"], T)],
        [_ro(S, 256, BF16, T), _ro(S, 128, BF16, T), _ro(S, 512, BF16, T), kv3(BF16), kv3(BF16)], n_tiles=nt)
    k_swa = k_swa.reshape(SWA_KV_HEADS * S, SWA_HEAD_DIM)
    v_swa = v_swa.reshape(SWA_KV_HEADS * S, SWA_HEAD_DIM)
    q_lat, = _mm_nn(f"l{l}_mm_uq", cqn, W["w_uq"], [F32], tm=1024, tn=MLA_QW)
    kv_lat, = _mm_nn(f"l{l}_mm_ukv", ckvn, W["w_ukv"], [F32], tm=1024, tn=MLA_KVW)

    def mla_prep(q, kk, vv, kr, cq_t, sq_t, ck_t, sk_t):
        kpe = pltpu.roll(_rope(kr, ck_t, sk_t, MLA_ROPE // 2), MLA_NOPE, axis=1)
        return _rope(q, cq_t, sq_t, MLA_ROPE // 2) * MLA_SCALE, kk + jnp.tile(kpe, (1, MLA_HEADS)), vv

    q_mla, k_mla, v_mla = _rowwise(
        f"l{l}_mla_prep", mla_prep,
        [_ri(q_lat, T), _ri(kv_lat, T, MLA_HEADS * LANES, 0), _ri(kv_lat, T, MLA_HEADS * MLA_V, 2), _ri(proj, T, 128, _cb("k_rope")),
         _ri(tb["c_mla_q"], T), _ri(tb["s_mla_q"], T), _ri(tb["c_mla_k"], T), _ri(tb["s_mla_k"], T)],
        [_ro(S, MLA_QW, BF16, T), _ro(S, MLA_HEADS * LANES, BF16, T), _ro(S, MLA_HEADS * MLA_V, BF16, T)], n_tiles=nt)

    (att_a, lse_a), rode = _softmax_fwd(f"l{l}_mla_fwd", q_mla, k_mla, v_mla, o_width=MLA_HEADS * MLA_V, ride=ride,
                                        **{**MLA_SPEC, "tq": min(MLA_FWD_TQ, S), "tk": min(MLA_FWD_TK, S)}, **_mla_specs(S, min(MLA_FWD_TQ, S)))
    W = {**W, **late(rode)}
    att_b, lse_b = _swa_fwd(f"l{l}_swa_fwd", q_swa, k_swa, v_swa, P["swa_sinks"], tq=SWA_TQ)
    att_c, att_c32 = _sb_fwd(f"l{l}_sb_fwd", proj16, **SB_SPEC)
    o_a, = _mm_nn(f"l{l}_mm_oa", att_a, W["w_o_mla"], [F32], tm=1024, tn=D_MODEL)
    o_b, = _mm_nn(f"l{l}_mm_ob", att_b, W["w_o_swa"], [F32], tm=1024, tn=D_MODEL)
    o_c, = _mm_nn(f"l{l}_mm_oc", att_c, W["w_o_sb"], [F32], tm=1024, tn=D_MODEL)

    def gate_mix(gl, b, oa, ob, oc):
        gt = jax.nn.sigmoid(gl + b)
        return (gt[:, :D_MODEL] * oa + gt[:, D_MODEL:2 * D_MODEL] * ob + gt[:, 2 * D_MODEL:] * oc,)

    mixed, = _rowwise(f"l{l}_gate_mix", gate_mix, [_ri(proj, T, 3072, 0), _bi(P["b_gate"]), _ri(o_a, T), _ri(o_b, T), _ri(o_c, T)],
                      [_ro(S, D_MODEL, BF16, T)], n_tiles=nt)
    y, = _mm_nn(f"l{l}_mm_out", mixed, W["w_out"], [F32], tm=1024, tn=D_MODEL)

    def resid_norm(xv, yv, gpost, gpre):
        x1 = xv + _rms(yv, gpost)
        return x1, _rms(x1, gpre)

    x1, h2 = _rowwise(f"l{l}_resid_norm", resid_norm, [_ri(x, T), _ri(y, T), _bi(P["g_mix_post"]), _bi(P["g_mlp_pre"])],
                      [_ro(S, D_MODEL, F32, T), _ro(S, D_MODEL, BF16, T)], n_tiles=nt)
    up, u = _mm_nn(f"l{l}_mm_up", h2, W["w_up"], [F32, BF16], tm=512, tn=2048,
                   epilogue=lambda acc: (acc, jnp.square(jnp.maximum(acc, 0.0))))
    dn, = _mm_nn(f"l{l}_mm_down", u, W["w_down"], [F32], tm=512, tn=D_MODEL)
    x2, = _rowwise(f"l{l}_resid_out", lambda xv, dv, g: (xv + _rms(dv, g),), [_ri(x1, T), _ri(dn, T), _bi(P["g_mlp_post"])],
                   [_ro(S, D_MODEL, F32, T)], n_tiles=nt)
    saved = dict(x=x, h=h, proj=proj, proj16=proj16, cqn=cqn, ckvn=ckvn, q_swa=q_swa, k_swa=k_swa, v_swa=v_swa, q_mla=q_mla, k_mla=k_mla,
                 v_mla=v_mla, att_a=att_a, lse_a=lse_a, att_b=att_b, lse_b=lse_b, att_c=att_c, att_c32=att_c32, o_a=o_a, o_b=o_b, o_c=o_c,
                 mixed=mixed, y=y, x1=x1, h2=h2, up=up, u=u, dn=dn)
    return x2, saved, W, rode


def _layer_bwd(l, dx2, sv, W, P, tb, T, pending, to_send):
    S = dx2.shape[0]
    nt = S // T
    G = {}

    def post_norm_bwd(v, g, dy):
        return _rms_bwd(v, g, dy)

    d_dn, G["g_mlp_post"] = _rowwise(f"l{l}_b_post2", post_norm_bwd, [_ri(sv["dn"], T), _bi(P["g_mlp_post"]), _ri(dx2, T)],
                                    [_ro(S, D_MODEL, BF16, T)], [D_MODEL], n_tiles=nt)
    d_up, = _mm_nt(f"l{l}_b_mm_down", d_dn, W["w_down"], [BF16], tm=512, tn=2048, extras=[sv["up"]],
                   epilogue=lambda acc, upv: (acc * (2.0 * jnp.maximum(upv, 0.0)),))
    G["w_down"] = _mm_tn(f"l{l}_g_down", sv["u"], d_dn, tm=2048, tn=D_MODEL, ts=512)
    d_h2, = _mm_nt(f"l{l}_b_mm_up", d_up, W["w_up"], [F32], tm=512, tn=D_MODEL)
    G["w_up"] = _mm_tn(f"l{l}_g_up", sv["h2"], d_up, tm=D_MODEL, tn=2048, ts=512)

    def pre_norm_bwd(v, g, dy, dres):
        dx, dg = _rms_bwd(v, g, dy)
        return dres + dx, dg

    def mid_norms_bwd(x1v, gpre, dh2, dres, yv, gpost):
        dx1v, dgpre = pre_norm_bwd(x1v, gpre, dh2, dres)
        dyv, dgpost = post_norm_bwd(yv, gpost, dx1v)
        return dx1v, dyv, dgpre, dgpost

    dx1, d_y, G["g_mlp_pre"], G["g_mix_post"] = _rowwise(
        f"l{l}_b_mid_norms", mid_norms_bwd,
        [_ri(sv["x1"], T), _bi(P["g_mlp_pre"]), _ri(d_h2, T), _ri(dx2, T), _ri(sv["y"], T), _bi(P["g_mix_post"])],
        [_ro(S, D_MODEL, F32, T), _ro(S, D_MODEL, BF16, T)], [D_MODEL, D_MODEL], n_tiles=nt)
    d_mixed, = _mm_nt(f"l{l}_b_mm_out", d_y, W["w_out"], [F32], tm=1024, tn=D_MODEL)
    G["w_out"] = _mm_tn(f"l{l}_g_out", sv["mixed"], d_y, tm=D_MODEL, tn=D_MODEL, ts=512)

    def gate_bwd(dm, gl, b, oa, ob, oc):
        gt = jax.nn.sigmoid(gl + b)
        outs, dgl = [], []
        for k, o in enumerate((oa, ob, oc)):
            gk = gt[:, k * D_MODEL:(k + 1) * D_MODEL]
            outs.append(dm * gk)
            dgl.append(dm * o * gk * (1.0 - gk))
        dgl = jnp.concatenate(dgl, axis=1)
        return (*outs, dgl, jnp.sum(dgl, axis=0, keepdims=True))

    d_oa, d_ob, d_oc, d_gl, G["b_gate"] = _rowwise(
        f"l{l}_b_gate", gate_bwd, [_ri(d_mixed, T), _ri(sv["proj"], T, 3072, 0), _bi(P["b_gate"]), _ri(sv["o_a"], T), _ri(sv["o_b"], T), _ri(sv["o_c"], T)],
        [_ro(S, D_MODEL, BF16, T)] * 3 + [_ro(S, 3 * D_MODEL, BF16, T)], [3 * D_MODEL], n_tiles=nt)
    d_att = {}
    for br, d_o, att in (("mla", d_oa, sv["att_a"]), ("swa", d_ob, sv["att_b"]), ("sb", d_oc, sv["att_c"])):
        d_att[br], = _mm_nt(f"l{l}_b_mm_o_{br}", d_o, W["w_o_" + br], [F32], tm=1024, tn=512)
        G["w_o_" + br] = _mm_tn(f"l{l}_g_o_{br}", att, d_o, tm=512, tn=D_MODEL, ts=512)

    riders = list(pending) + [(n, l, to_send(n, G[n])) for n in W_LATE]
    (dq_mla, dkt_mla, dvt_mla), rode = _softmax_bwd(f"l{l}_mla_bwd", sv["q_mla"], sv["k_mla"], sv["v_mla"], sv["att_a"], d_att["mla"], sv["lse_a"],
                                                  scale=MLA_SCALE, ride=([s for _, _, s in riders], True), **MLA_SPEC, **_mla_specs(S, MLA_SPEC["tq"]))
    rode = [(n, ly, r) for (n, ly, _), r in zip(riders, rode)]
    dk_mla = dkt_mla.transpose(0, 2, 1).reshape(S, -1)
    dv_mla = dvt_mla.transpose(0, 2, 1).reshape(S, -1)
    dq_swa, dkt_swa, dvt_swa, dsink = _swa_bwd(f"l{l}_swa_bwd", sv["q_swa"], sv["k_swa"], sv["v_swa"], sv["att_b"], d_att["swa"], sv["lse_b"],
                                               P["swa_sinks"], tq=SWA_TQ)
    dk_swa = dkt_swa.transpose(0, 2, 1).reshape(SWA_KV_HEADS * S, SWA_HEAD_DIM)
    dv_swa = dvt_swa.transpose(0, 2, 1).reshape(SWA_KV_HEADS * S, SWA_HEAD_DIM)
    G["swa_sinks"] = dsink[:, :SWA_G, 0].reshape(1, SWA_HEADS)
    dq_sb, dkt_sb, dvt_sb = _sb_bwd(f"l{l}_sb_bwd", sv["proj16"], sv["att_c32"], d_att["sb"], **SB_SPEC)
    dk_sb = dkt_sb.transpose(0, 2, 1).reshape(S, -1)
    dv_sb = dvt_sb.transpose(0, 2, 1).reshape(S, -1)

    def mla_prep_bwd(dq, dk, dvv, cq_t, sq_t, ck_t, sk_t):
        dks = dk[:, :LANES]
        for hh in range(1, MLA_HEADS):
            dks = dks + dk[:, hh * LANES:(hh + 1) * LANES]
        d_kr = _rope_t(pltpu.roll(dks, LANES - MLA_NOPE, axis=1), ck_t, sk_t, MLA_ROPE // 2)
        return _rope_t(dq, cq_t, sq_t, MLA_ROPE // 2), jnp.concatenate([dk, dvv], axis=1), d_kr

    d_q_lat, d_kv_lat, d_krope = _rowwise(
        f"l{l}_b_mla_prep", mla_prep_bwd,
        [_ri(dq_mla, T), _ri(dk_mla, T), _ri(dv_mla, T), _ri(tb["c_mla_q"], T), _ri(tb["s_mla_q"], T), _ri(tb["c_mla_k"], T), _ri(tb["s_mla_k"], T)],
        [_ro(S, MLA_QW, BF16, T), _ro(S, MLA_KVW, BF16, T), _ro(S, LANES, BF16, T)], n_tiles=nt)
    d_cqn, = _mm_nt(f"l{l}_b_mm_uq", d_q_lat, W["w_uq"], [F32], tm=1024, tn=MLA_Q_LORA)
    G["w_uq"] = _mm_tn(f"l{l}_g_uq", sv["cqn"], d_q_lat, tm=MLA_Q_LORA, tn=MLA_QW, ts=512)
    d_ckvn, = _mm_nt(f"l{l}_b_mm_ukv", d_kv_lat, W["w_ukv"], [F32], tm=1024, tn=MLA_KV_LORA)
    G["w_ukv"] = _mm_tn(f"l{l}_g_ukv", sv["ckvn"], d_kv_lat, tm=MLA_KV_LORA, tn=MLA_KVW, ts=512)

    def mix_prep_bwd(cq, ckv, gq, gkv, dcqn, dckvn, dqs, dks, dvs, cq_t, sq_t, ck_t, sk_t):
        d_cq, dgq = _rms_bwd(cq, gq, dcqn)
        d_ckv, dgkv = _rms_bwd(ckv, gkv, dckvn)
        dk2 = jnp.concatenate([dks[0], dks[1]], axis=1)
        dv2 = jnp.concatenate([dvs[0], dvs[1]], axis=1)
        return (d_cq, d_ckv, _rope_t(dqs, cq_t, sq_t, SWA_HEAD_DIM // 2), _rope_t(dk2, ck_t, sk_t, SWA_HEAD_DIM // 2), dv2, dgq, dgkv)

    kv3 = lambda a: (a.reshape(SWA_KV_HEADS, S, SWA_HEAD_DIM), (SWA_KV_HEADS, T, SWA_HEAD_DIM), lambda i: (0, i, 0))
    d_cq, d_ckv, d_qswa, d_kswa, d_vswa, G["g_q_lat"], G["g_kv_lat"] = _rowwise(
        f"l{l}_b_mix_prep", mix_prep_bwd,
        [_ri(sv["proj"], T, 256, _cb("c_q")), _ri(sv["proj"], T, 128, _cb("c_kv")), _bi(P["g_q_lat"]), _bi(P["g_kv_lat"]), _ri(d_cqn, T), _ri(d_ckvn, T),
         _ri(dq_swa, T), kv3(dk_swa), kv3(dv_swa), _ri(tb["c_swa_q"], T), _ri(tb["s_swa_q"], T), _ri(tb["c_swa_k"], T), _ri(tb["s_swa_k"], T)],
        [_ro(S, 256, BF16, T), _ro(S, 128, BF16, T), _ro(S, 512, BF16, T), _ro(S, 128, BF16, T), _ro(S, 128, BF16, T)], [256, 128], n_tiles=nt)
    pieces = dict(gates=d_gl, q_swa=d_qswa, q_sb=dq_sb, k_sb=dk_sb, v_sb=dv_sb, c_q=d_cq, c_kv=d_ckv, k_swa=d_kswa, v_swa=d_vswa, k_rope=d_krope)
    d_proj = jnp.concatenate([pieces[n].astype(BF16) for n in PERM_ORDER], axis=1)
    d_h, = _mm_nt(f"l{l}_b_mm_in", d_proj, W["w_in"], [F32], tm=512, tn=512)
    G["w_in"] = _mm_tn(f"l{l}_g_in", sv["h"], d_proj, tm=D_MODEL, tn=IN_WIDTH_P // 2, ts=512)
    dx, G["g_mix_pre"] = _rowwise(f"l{l}_b_pre1", pre_norm_bwd, [_ri(sv["x"], T), _bi(P["g_mix_pre"]), _ri(d_h, T), _ri(dx1, T)],
                                 [_ro(S, D_MODEL, F32, T)], [D_MODEL], n_tiles=nt)
    return dx, G, rode


def _pack_rows(vecs, rows):
    flat = jnp.concatenate([v.reshape(-1) for v in vecs])
    return jnp.pad(flat, (0, rows * PACK_COLS - flat.shape[0])).reshape(rows, PACK_COLS)


def kernel(x, positions, g_mix_pre, w_in, b_gate, g_q_lat, g_kv_lat, w_uq, w_ukv, swa_sinks, w_o_mla, w_o_swa, w_o_sb, w_out, g_mix_post, g_mlp_pre, w_up, w_down, g_mlp_post, loss_target, m_g_mix_pre, m_w_in, m_b_gate, m_g_q_lat, m_g_kv_lat, m_w_uq, m_w_ukv, m_swa_sinks, m_w_o_mla, m_w_o_swa, m_w_o_sb, m_w_out, m_g_mix_post, m_g_mlp_pre, m_w_up, m_w_down, m_g_mlp_post, v_g_mix_pre, v_w_in, v_b_gate, v_g_q_lat, v_g_kv_lat, v_w_uq, v_w_ukv, v_swa_sinks, v_w_o_mla, v_w_o_swa, v_w_o_sb, v_w_out, v_g_mix_post, v_g_mlp_pre, v_w_up, v_w_down, v_g_mlp_post):
    a = dict(locals())
    S = x.shape[1]
    depth = w_in.shape[0]
    T = min(ROW_TILE, S)
    xs = x.reshape(S, D_MODEL)
    tb = _tables(positions)

    perm = dict(w_in=_perm_w_in, w_uq=_perm_w_uq, w_ukv=_perm_w_ukv)

    def shards(l, names):
        return [a[n][l].astype(BF16) for n in names]

    def whole(names, gathered):
        return {n: perm.get(n, lambda t: t)(_from_shards(n, g, (1,) + a[n].shape[1:])[0]) for n, g in zip(names, gathered)}

    early = whole(W_EARLY, _exchange("gather_weights", shards(0, W_EARLY), per_peer=False))
    layers, saved = [], []
    h = xs
    for l in range(depth):
        P = {n: a[n][l].reshape(1, -1) for n in SMALL if n != "swa_sinks"}
        P["swa_sinks"] = a["swa_sinks"][l]
        srcs = shards(l, W_LATE) + (shards(l + 1, W_EARLY) if l + 1 < depth else [])
        h, sv, W, rode = _layer_fwd(l, h, early, P, tb, T, (srcs, False), lambda r: whole(W_LATE, r[:len(W_LATE)]))
        early = whole(W_EARLY, rode[len(W_LATE):])
        layers.append((W, P))
        saved.append(sv)

    def loss_head(yv, tv):
        err = yv - tv
        part = 0.5 * jnp.sum(jnp.mean(err * err, axis=1, keepdims=True), axis=0, keepdims=True)
        return err * (1.0 / D_MODEL), jnp.broadcast_to(part, (1, LANES))

    dh, loss_row = _rowwise("loss_head", loss_head, [_ri(h, T), _ri(loss_target.reshape(S, D_MODEL), T)], [_ro(S, D_MODEL, F32, T)], [LANES],
                            n_tiles=S // T)
    loss = lax.psum(loss_row[0, 0], ("x", "y", "c"))

    unperm = dict(w_in=_unperm_w_in, w_uq=_unperm_w_uq, w_ukv=_unperm_w_ukv)

    def to_send(n, g):
        return _to_shards(n, unperm.get(n, lambda t: t)(g).astype(BF16)[None])

    grads, recv, pending = [None] * depth, {n: [None] * depth for n in BIG}, []
    for l in reversed(range(depth)):
        W, P = layers[l]
        dh, grads[l], rode = _layer_bwd(l, dh, saved[l], W, P, tb, T, pending, to_send)
        for n, ly, r in rode:
            recv[n][ly] = r
        pending = [(n, l, to_send(n, grads[l][n])) for n in W_EARLY]
    for (n, ly, _), r in zip(pending, _exchange("scatter_grads", [s for _, _, s in pending], per_peer=True)):
        recv[n][ly] = r
    grad_x = dh.reshape(x.shape)

    out = {}
    for n in BIG:
        shp = a[n].shape
        rows, cols = shp[0] * shp[1], shp[2]
        parts = jnp.concatenate(recv[n], axis=1)
        res = _adamw("adamw_" + n, parts, a[n].reshape(rows, cols), a["m_" + n].reshape(rows, cols), a["v_" + n].reshape(rows, cols),
                     tile=min(256, rows))
        out[n] = [r.reshape(shp) for r in res]

    small_total = sum(a[n].size for n in SMALL)
    small_rows = -(-small_total // (8 * PACK_COLS)) * 8
    sg = _pack_rows([jnp.stack([grads[l][n].reshape(-1) for l in range(depth)]) for n in SMALL], small_rows)
    sg_all, = _exchange("gather_small_grads", [sg], per_peer=False)
    res = _adamw("adamw_small", sg_all, _pack_rows([a[n] for n in SMALL], small_rows), _pack_rows([a["m_" + n] for n in SMALL], small_rows),
                 _pack_rows([a["v_" + n] for n in SMALL], small_rows), tile=small_rows)
    off = 0
    for n in SMALL:
        cnt = a[n].size
        out[n] = [r.reshape(-1)[off:off + cnt].reshape(a[n].shape) for r in res]
        off += cnt

    order = ("g_mix_pre", "w_in", "b_gate", "g_q_lat", "g_kv_lat", "w_uq", "w_ukv", "swa_sinks", "w_o_mla", "w_o_swa", "w_o_sb", "w_out",
             "g_mix_post", "g_mlp_pre", "w_up", "w_down", "g_mlp_post")
    return (loss, grad_x, *[out[n][0] for n in order], *[out[n][1] for n in order], *[out[n][2] for n in order], *[out[n][3] for n in order])
```
